```python
import jax, jax.numpy as jnp
from jax import lax
import numpy as np

D_MODEL = 1024
BATCH = 8
SEQ = 8192
DEPTH = 4

A_HEADS = 4
A_HEAD_DIM = 64
A_WIDTH = A_HEADS * A_HEAD_DIM
B_HEADS = 4
B_HEAD_DIM = 128
B_WIDTH = B_HEADS * B_HEAD_DIM
C_BLOCKS = 4
C_WIDTH = 256
C_BLOCK_DIM = C_WIDTH // C_BLOCKS
D_MIX = A_WIDTH + B_WIDTH + C_WIDTH
IN_SPLITS = (A_WIDTH, A_WIDTH, A_WIDTH, A_WIDTH,
             B_WIDTH, B_WIDTH, B_WIDTH, B_WIDTH, B_HEADS, B_HEADS,
             C_WIDTH, C_WIDTH)
D_IN = 4 * A_WIDTH + 4 * B_WIDTH + 2 * B_HEADS + 2 * C_WIDTH
CONV_K = 4
CHUNK = 64
RG_C = 8.0
D_FF = 4 * D_MODEL
EPS = 1e-6
TINY = 1e-30

kernel_name = "hymba_style_hgrn2_gdn_rglru_hybrid"


def rmsnorm(x, g):
    xf = x.astype(jnp.float32)
    y = xf * lax.rsqrt(jnp.mean(xf * xf, axis=-1, keepdims=True) + EPS)
    return (y * g.astype(jnp.float32)).astype(x.dtype)


def l2norm(x):
    return x * lax.rsqrt(jnp.sum(x * x, axis=-1, keepdims=True) + EPS)


def split_cols(t, sizes):
    offsets = np.cumsum(np.array(sizes))[:-1].tolist()
    return jnp.split(t, offsets, axis=-1)


def causal_depthwise_conv(x, w):
    k = w.shape[0]
    s = x.shape[1]
    xp = jnp.pad(x, ((0, 0), (k - 1, 0), (0, 0)))
    y = xp[:, 0:s] * w[0]
    for j in range(1, k):
        y = y + xp[:, j:j + s] * w[j]
    return y


def masked_exp(mask, t):
    return jnp.where(mask, jnp.exp(jnp.where(mask, t, 0.0)), 0.0)


def to_chunks(t):
    b, s = t.shape[:2]
    t = t.reshape(b, s // CHUNK, CHUNK, *t.shape[2:])
    t = jnp.moveaxis(t, 2, 3)
    return jnp.moveaxis(t, 1, 0)


def from_chunks(t):
    n, b, h, c, d = t.shape
    return jnp.transpose(t, (1, 0, 3, 2, 4)).reshape(b, n * c, h, d)


def hgrn2_chunk(q, k, v, log_f):
    bsz, _, h, dk = q.shape
    dv = v.shape[-1]
    qc, kc, vc = to_chunks(q), to_chunks(k), to_chunks(v)
    cum = jnp.cumsum(to_chunks(log_f), axis=-2)
    cum_last = cum[..., -1:, :]
    q_dec = qc * jnp.exp(cum)
    k_dec = kc * jnp.exp(cum_last - cum)
    chunk_dec = jnp.exp(cum_last[..., 0, :])
    causal = jnp.tril(jnp.ones((CHUNK, CHUNK), dtype=bool))[:, :, None]

    def step(state, xs):
        q_, k_, v_, c_, qd, kd, cd = xs
        diff = c_[..., :, None, :] - c_[..., None, :, :]
        dec = masked_exp(causal, diff)
        attn = jnp.einsum('bhtd,bhsd,bhtsd->bhts', q_, k_, dec)
        o = jnp.einsum('bhtd,bhde->bhte', qd, state) + jnp.einsum('bhts,bhse->bhte', attn, v_)
        state = state * cd[..., :, None] + jnp.einsum('bhsd,bhse->bhde', kd, v_)
        return state, o

    s0 = jnp.zeros((bsz, h, dk, dv), jnp.float32)
    _, o = lax.scan(step, s0, (qc, kc, vc, cum, q_dec, k_dec, chunk_dec))
    return from_chunks(o)


def hgrn2_mixer(q_in, f_in, i_in, g_in, lb, norm_g):
    bsz, s, _ = q_in.shape
    shp = (bsz, s, A_HEADS, A_HEAD_DIM)
    q = jax.nn.silu(q_in.astype(jnp.float32)).reshape(shp)
    fp = f_in.astype(jnp.float32).reshape(shp)
    lbh = lb.astype(jnp.float32).reshape(A_HEADS, A_HEAD_DIM)
    f = lbh + (1.0 - lbh) * jax.nn.sigmoid(fp)
    log_f = jnp.log(jnp.maximum(f, TINY))
    k = (1.0 - lbh) * jax.nn.sigmoid(-fp)
    v = i_in.astype(jnp.float32).reshape(shp)
    o = hgrn2_chunk(q, k, v, log_f)
    o = rmsnorm(o, norm_g) * jax.nn.silu(g_in.astype(jnp.float32).reshape(shp))
    return o.reshape(bsz, s, A_WIDTH)


def gated_delta_chunk(q, k, v, log_alpha, beta):
    bsz, _, h, dk = q.shape
    dv = v.shape[-1]
    qc = to_chunks(q) * (dk ** -0.5)
    kc, vc = to_chunks(k), to_chunks(v)
    gc = jnp.cumsum(to_chunks(log_alpha), axis=-1)
    bc = to_chunks(beta)
    causal = jnp.tril(jnp.ones((CHUNK, CHUNK), dtype=bool))
    strict = jnp.tril(jnp.ones((CHUNK, CHUNK), dtype=bool), -1)
    decay = masked_exp(causal, gc[..., :, None] - gc[..., None, :])
    kb = kc * bc[..., None]
    a_mat = jnp.where(strict, jnp.einsum('nbhtd,nbhsd->nbhts', kb, kc) * decay, 0.0)
    u = lax.linalg.triangular_solve(a_mat, vc * bc[..., None], left_side=True, lower=True,
                                    unit_diagonal=True)
    w = lax.linalg.triangular_solve(a_mat, kb * jnp.exp(gc)[..., None], left_side=True,
                                    lower=True, unit_diagonal=True)
    qk = jnp.einsum('nbhtd,nbhsd->nbhts', qc, kc) * decay
    q_dec = qc * jnp.exp(gc)[..., None]
    k_dec = kc * jnp.exp(gc[..., -1:] - gc)[..., None]
    chunk_dec = jnp.exp(gc[..., -1])

    def step(state, xs):
        qd, kd, qk_, u_, w_, cd = xs
        v_new = u_ - jnp.einsum('bhtd,bhde->bhte', w_, state)
        o = jnp.einsum('bhtd,bhde->bhte', qd, state) + jnp.einsum('bhts,bhse->bhte', qk_, v_new)
        state = state * cd[..., None, None] + jnp.einsum('bhsd,bhse->bhde', kd, v_new)
        return state, o

    s0 = jnp.zeros((bsz, h, dk, dv), jnp.float32)
    _, o = lax.scan(step, s0, (q_dec, k_dec, qk, u, w, chunk_dec))
    return from_chunks(o)


def gdn_mixer(q_in, k_in, v_in, z_in, b_in, a_in, conv_w, a_log, dt_bias, norm_g):
    bsz, s, _ = q_in.shape
    shp = (bsz, s, B_HEADS, B_HEAD_DIM)
    qkv = jnp.concatenate([q_in, k_in, v_in], axis=-1).astype(jnp.float32)
    qkv = jax.nn.silu(causal_depthwise_conv(qkv, conv_w.astype(jnp.float32)))
    q, k, v = split_cols(qkv, (B_WIDTH, B_WIDTH, B_WIDTH))
    q = l2norm(q.reshape(shp))
    k = l2norm(k.reshape(shp))
    v = v.reshape(shp)
    beta = jax.nn.sigmoid(b_in.astype(jnp.float32))
    log_alpha = -jnp.exp(a_log.astype(jnp.float32)) * jax.nn.softplus(
        a_in.astype(jnp.float32) + dt_bias.astype(jnp.float32))
    o = gated_delta_chunk(q, k, v, log_alpha, beta)
    o = rmsnorm(o, norm_g) * jax.nn.silu(z_in.astype(jnp.float32).reshape(shp))
    return o.reshape(bsz, s, B_WIDTH)


def rglru(x, w_a, b_a, w_x, b_x, lam):
    bsz, s, _ = x.shape
    xb = x.reshape(bsz, s, C_BLOCKS, C_BLOCK_DIM)
    r = jax.nn.sigmoid(jnp.einsum('bsnd,nde->bsne', xb, w_a.astype(jnp.float32)).reshape(bsz, s, C_WIDTH)
                       + b_a.astype(jnp.float32))
    i = jax.nn.sigmoid(jnp.einsum('bsnd,nde->bsne', xb, w_x.astype(jnp.float32)).reshape(bsz, s, C_WIDTH)
                       + b_x.astype(jnp.float32))
    log_a = -RG_C * r * jax.nn.softplus(-lam.astype(jnp.float32))
    a = jnp.exp(log_a)
    pos = jnp.arange(s)[None, :, None]
    mult = jnp.where(pos == 0, 1.0, jnp.sqrt(jnp.maximum(-jnp.expm1(2.0 * log_a), EPS)))
    bx = mult * i * x

    def combine(c1, c2):
        a1, b1 = c1
        a2, b2 = c2
        return a1 * a2, a2 * b1 + b2

    _, hs = lax.associative_scan(combine, (a, bx), axis=1)
    return hs


def rglru_mixer(x_in, y_in, conv_w, conv_b, w_a, b_a, w_x, b_x, lam):
    xc = causal_depthwise_conv(x_in.astype(jnp.float32), conv_w.astype(jnp.float32)) + conv_b.astype(jnp.float32)
    h = rglru(xc, w_a, b_a, w_x, b_x, lam)
    return jax.nn.gelu(y_in.astype(jnp.float32), approximate=True) * h


def _fwd_setup_inputs(seed: int = 0) -> dict:
    key = jax.random.key(seed)
    ks = jax.random.split(key, 24)
    f32 = jnp.float32
    nrm = jax.random.normal
    uni = jax.random.uniform
    x = nrm(ks[0], (BATCH, SEQ, D_MODEL), f32)
    norm1_g = 1.0 + 0.02 * nrm(ks[1], (DEPTH, D_MODEL), f32)
    w_in = nrm(ks[2], (DEPTH, D_MODEL, D_IN), f32) * D_MODEL ** -0.5
    hgrn_lb_logits = 0.1 * nrm(ks[3], (DEPTH, A_WIDTH), f32)
    hgrn_norm_g = 1.0 + 0.02 * nrm(ks[4], (DEPTH, A_HEAD_DIM), f32)
    gdn_conv_w = nrm(ks[5], (DEPTH, CONV_K, 3 * B_WIDTH), f32) * CONV_K ** -0.5
    gdn_a_log = jnp.log(uni(ks[6], (DEPTH, B_HEADS), f32, 1.0, 16.0))
    dt = jnp.exp(uni(ks[7], (DEPTH, B_HEADS), f32, float(np.log(1e-3)), float(np.log(1e-1))))
    gdn_dt_bias = dt + jnp.log(-jnp.expm1(-dt))
    gdn_norm_g = 1.0 + 0.02 * nrm(ks[8], (DEPTH, B_HEAD_DIM), f32)
    lru_conv_w = nrm(ks[9], (DEPTH, CONV_K, C_WIDTH), f32) * CONV_K ** -0.5
    lru_conv_b = 0.01 * nrm(ks[10], (DEPTH, C_WIDTH), f32)
    lru_w_a = nrm(ks[11], (DEPTH, C_BLOCKS, C_BLOCK_DIM, C_BLOCK_DIM), f32) * C_BLOCK_DIM ** -0.5
    lru_b_a = 0.01 * nrm(ks[12], (DEPTH, C_WIDTH), f32)
    lru_w_x = nrm(ks[13], (DEPTH, C_BLOCKS, C_BLOCK_DIM, C_BLOCK_DIM), f32) * C_BLOCK_DIM ** -0.5
    lru_b_x = 0.01 * nrm(ks[14], (DEPTH, C_WIDTH), f32)
    a_c = uni(ks[15], (DEPTH, C_WIDTH), f32, 0.9, 0.999)
    s_a = a_c ** (1.0 / RG_C)
    lru_lambda = jnp.log(s_a) - jnp.log1p(-s_a)
    w_out = nrm(ks[16], (DEPTH, D_MIX, D_MODEL), f32) * D_MIX ** -0.5
    norm2_g = 1.0 + 0.02 * nrm(ks[17], (DEPTH, D_MODEL), f32)
    w_up = nrm(ks[18], (DEPTH, D_MODEL, D_FF), f32) * D_MODEL ** -0.5
    w_down = nrm(ks[19], (DEPTH, D_FF, D_MODEL), f32) * D_FF ** -0.5
    final_norm_g = 1.0 + 0.02 * nrm(ks[20], (D_MODEL,), f32)
    return {"x": x, "norm1_g": norm1_g, "w_in": w_in, "hgrn_lb_logits": hgrn_lb_logits,
            "hgrn_norm_g": hgrn_norm_g, "gdn_conv_w": gdn_conv_w, "gdn_a_log": gdn_a_log,
            "gdn_dt_bias": gdn_dt_bias, "gdn_norm_g": gdn_norm_g, "lru_conv_w": lru_conv_w,
            "lru_conv_b": lru_conv_b, "lru_w_a": lru_w_a, "lru_b_a": lru_b_a, "lru_w_x": lru_w_x,
            "lru_b_x": lru_b_x, "lru_lambda": lru_lambda, "w_out": w_out, "norm2_g": norm2_g,
            "w_up": w_up, "w_down": w_down, "final_norm_g": final_norm_g}


def _fwd_reference(x, norm1_g, w_in, hgrn_lb_logits, hgrn_norm_g, gdn_conv_w, gdn_a_log, gdn_dt_bias,
              gdn_norm_g, lru_conv_w, lru_conv_b, lru_w_a, lru_b_a, lru_w_x, lru_b_x, lru_lambda,
              w_out, norm2_g, w_up, w_down, final_norm_g):
    p = jax.nn.softmax(hgrn_lb_logits.astype(jnp.float32), axis=0)
    lower_bounds = jnp.clip(jnp.cumsum(p, axis=0) - p[0], 0.0, 1.0 - EPS)
    for l in range(DEPTH):
        h = rmsnorm(x, norm1_g[l])
        proj = h @ w_in[l]
        (aq, af, ai, ag, bq, bk, bv, bz, bb, ba, cx, cy) = split_cols(proj, IN_SPLITS)
        o_a = hgrn2_mixer(aq, af, ai, ag, lower_bounds[l], hgrn_norm_g[l])
        o_b = gdn_mixer(bq, bk, bv, bz, bb, ba, gdn_conv_w[l], gdn_a_log[l], gdn_dt_bias[l], gdn_norm_g[l])
        o_c = rglru_mixer(cx, cy, lru_conv_w[l], lru_conv_b[l], lru_w_a[l], lru_b_a[l],
                          lru_w_x[l], lru_b_x[l], lru_lambda[l])
        mixed = jnp.concatenate([o_a, o_b, o_c], axis=-1).astype(x.dtype)
        x = x + mixed @ w_out[l]
        h2 = rmsnorm(x, norm2_g[l])
        x = x + jnp.square(jax.nn.relu(h2 @ w_up[l])) @ w_down[l]
    return rmsnorm(x, final_norm_g)


import jax as _jax
import jax.numpy as _jnp

TWIN_FORMAT = 'train_step'
FWD_PARAMS = ['x', 'norm1_g', 'w_in', 'hgrn_lb_logits', 'hgrn_norm_g', 'gdn_conv_w', 'gdn_a_log', 'gdn_dt_bias', 'gdn_norm_g', 'lru_conv_w', 'lru_conv_b', 'lru_w_a', 'lru_b_a', 'lru_w_x', 'lru_b_x', 'lru_lambda', 'w_out', 'norm2_g', 'w_up', 'w_down', 'final_norm_g']
TWIN_WEIGHTS = ['norm1_g', 'w_in', 'hgrn_lb_logits', 'hgrn_norm_g', 'gdn_conv_w', 'gdn_a_log', 'gdn_dt_bias', 'gdn_norm_g', 'lru_conv_w', 'lru_conv_b', 'lru_w_a', 'lru_b_a', 'lru_w_x', 'lru_b_x', 'lru_lambda', 'w_out', 'norm2_g', 'w_up', 'w_down', 'final_norm_g']
TWIN_DIFF_INPUT = 'x'
TWIN_INPUTS = ['x', 'norm1_g', 'w_in', 'hgrn_lb_logits', 'hgrn_norm_g', 'gdn_conv_w', 'gdn_a_log', 'gdn_dt_bias', 'gdn_norm_g', 'lru_conv_w', 'lru_conv_b', 'lru_w_a', 'lru_b_a', 'lru_w_x', 'lru_b_x', 'lru_lambda', 'w_out', 'norm2_g', 'w_up', 'w_down', 'final_norm_g', 'loss_target', 'm_norm1_g', 'm_w_in', 'm_hgrn_lb_logits', 'm_hgrn_norm_g', 'm_gdn_conv_w', 'm_gdn_a_log', 'm_gdn_dt_bias', 'm_gdn_norm_g', 'm_lru_conv_w', 'm_lru_conv_b', 'm_lru_w_a', 'm_lru_b_a', 'm_lru_w_x', 'm_lru_b_x', 'm_lru_lambda', 'm_w_out', 'm_norm2_g', 'm_w_up', 'm_w_down', 'm_final_norm_g', 'v_norm1_g', 'v_w_in', 'v_hgrn_lb_logits', 'v_hgrn_norm_g', 'v_gdn_conv_w', 'v_gdn_a_log', 'v_gdn_dt_bias', 'v_gdn_norm_g', 'v_lru_conv_w', 'v_lru_conv_b', 'v_lru_w_a', 'v_lru_b_a', 'v_lru_w_x', 'v_lru_b_x', 'v_lru_lambda', 'v_w_out', 'v_norm2_g', 'v_w_up', 'v_w_down', 'v_final_norm_g']
TWIN_OUTPUTS = ['loss', 'grad_x', 'grad_norm1_g', 'grad_w_in', 'grad_hgrn_lb_logits', 'grad_hgrn_norm_g', 'grad_gdn_conv_w', 'grad_gdn_a_log', 'grad_gdn_dt_bias', 'grad_gdn_norm_g', 'grad_lru_conv_w', 'grad_lru_conv_b', 'grad_lru_w_a', 'grad_lru_b_a', 'grad_lru_w_x', 'grad_lru_b_x', 'grad_lru_lambda', 'grad_w_out', 'grad_norm2_g', 'grad_w_up', 'grad_w_down', 'grad_final_norm_g', 'delta_norm1_g', 'delta_w_in', 'delta_hgrn_lb_logits', 'delta_hgrn_norm_g', 'delta_gdn_conv_w', 'delta_gdn_a_log', 'delta_gdn_dt_bias', 'delta_gdn_norm_g', 'delta_lru_conv_w', 'delta_lru_conv_b', 'delta_lru_w_a', 'delta_lru_b_a', 'delta_lru_w_x', 'delta_lru_b_x', 'delta_lru_lambda', 'delta_w_out', 'delta_norm2_g', 'delta_w_up', 'delta_w_down', 'delta_final_norm_g', 'new_m_norm1_g', 'new_m_w_in', 'new_m_hgrn_lb_logits', 'new_m_hgrn_norm_g', 'new_m_gdn_conv_w', 'new_m_gdn_a_log', 'new_m_gdn_dt_bias', 'new_m_gdn_norm_g', 'new_m_lru_conv_w', 'new_m_lru_conv_b', 'new_m_lru_w_a', 'new_m_lru_b_a', 'new_m_lru_w_x', 'new_m_lru_b_x', 'new_m_lru_lambda', 'new_m_w_out', 'new_m_norm2_g', 'new_m_w_up', 'new_m_w_down', 'new_m_final_norm_g', 'new_v_norm1_g', 'new_v_w_in', 'new_v_hgrn_lb_logits', 'new_v_hgrn_norm_g', 'new_v_gdn_conv_w', 'new_v_gdn_a_log', 'new_v_gdn_dt_bias', 'new_v_gdn_norm_g', 'new_v_lru_conv_w', 'new_v_lru_conv_b', 'new_v_lru_w_a', 'new_v_lru_b_a', 'new_v_lru_w_x', 'new_v_lru_b_x', 'new_v_lru_lambda', 'new_v_w_out', 'new_v_norm2_g', 'new_v_w_up', 'new_v_w_down', 'new_v_final_norm_g']
TWIN_LEAF_KINDS = {'loss': 'loss', 'grad_x': 'grad_x', 'grad_norm1_g': 'grad_w', 'grad_w_in': 'grad_w', 'grad_hgrn_lb_logits': 'grad_w', 'grad_hgrn_norm_g': 'grad_w', 'grad_gdn_conv_w': 'grad_w', 'grad_gdn_a_log': 'grad_w', 'grad_gdn_dt_bias': 'grad_w', 'grad_gdn_norm_g': 'grad_w', 'grad_lru_conv_w': 'grad_w', 'grad_lru_conv_b': 'grad_w', 'grad_lru_w_a': 'grad_w', 'grad_lru_b_a': 'grad_w', 'grad_lru_w_x': 'grad_w', 'grad_lru_b_x': 'grad_w', 'grad_lru_lambda': 'grad_w', 'grad_w_out': 'grad_w', 'grad_norm2_g': 'grad_w', 'grad_w_up': 'grad_w', 'grad_w_down': 'grad_w', 'grad_final_norm_g': 'grad_w', 'delta_norm1_g': 'delta_w', 'delta_w_in': 'delta_w', 'delta_hgrn_lb_logits': 'delta_w', 'delta_hgrn_norm_g': 'delta_w', 'delta_gdn_conv_w': 'delta_w', 'delta_gdn_a_log': 'delta_w', 'delta_gdn_dt_bias': 'delta_w', 'delta_gdn_norm_g': 'delta_w', 'delta_lru_conv_w': 'delta_w', 'delta_lru_conv_b': 'delta_w', 'delta_lru_w_a': 'delta_w', 'delta_lru_b_a': 'delta_w', 'delta_lru_w_x': 'delta_w', 'delta_lru_b_x': 'delta_w', 'delta_lru_lambda': 'delta_w', 'delta_w_out': 'delta_w', 'delta_norm2_g': 'delta_w', 'delta_w_up': 'delta_w', 'delta_w_down': 'delta_w', 'delta_final_norm_g': 'delta_w', 'new_m_norm1_g': 'new_m', 'new_m_w_in': 'new_m', 'new_m_hgrn_lb_logits': 'new_m', 'new_m_hgrn_norm_g': 'new_m', 'new_m_gdn_conv_w': 'new_m', 'new_m_gdn_a_log': 'new_m', 'new_m_gdn_dt_bias': 'new_m', 'new_m_gdn_norm_g': 'new_m', 'new_m_lru_conv_w': 'new_m', 'new_m_lru_conv_b': 'new_m', 'new_m_lru_w_a': 'new_m', 'new_m_lru_b_a': 'new_m', 'new_m_lru_w_x': 'new_m', 'new_m_lru_b_x': 'new_m', 'new_m_lru_lambda': 'new_m', 'new_m_w_out': 'new_m', 'new_m_norm2_g': 'new_m', 'new_m_w_up': 'new_m', 'new_m_w_down': 'new_m', 'new_m_final_norm_g': 'new_m', 'new_v_norm1_g': 'new_v', 'new_v_w_in': 'new_v', 'new_v_hgrn_lb_logits': 'new_v', 'new_v_hgrn_norm_g': 'new_v', 'new_v_gdn_conv_w': 'new_v', 'new_v_gdn_a_log': 'new_v', 'new_v_gdn_dt_bias': 'new_v', 'new_v_gdn_norm_g': 'new_v', 'new_v_lru_conv_w': 'new_v', 'new_v_lru_conv_b': 'new_v', 'new_v_lru_w_a': 'new_v', 'new_v_lru_b_a': 'new_v', 'new_v_lru_w_x': 'new_v', 'new_v_lru_b_x': 'new_v', 'new_v_lru_lambda': 'new_v', 'new_v_w_out': 'new_v', 'new_v_norm2_g': 'new_v', 'new_v_w_up': 'new_v', 'new_v_w_down': 'new_v', 'new_v_final_norm_g': 'new_v'}


def _forward(args):
    return _fwd_reference(*[args[k] for k in FWD_PARAMS])


def _output_shape():
    def fwd():
        inp = _fwd_setup_inputs(0)
        return _fwd_reference(*[inp[k] for k in FWD_PARAMS])
    out = _jax.eval_shape(fwd)
    return out.shape, out.dtype

N_MICROBATCH = 1
ADAM_LR = 0.001
ADAM_B1 = 0.9
ADAM_B2 = 0.999
ADAM_EPS = 1e-08
ADAM_WD = 0.01
ADAM_STEP = 10
PER_EXAMPLE_BATCH_AXIS = {'x': 0, 'loss_target': 0}
SHARED_INPUTS = []
_WEIGHT_DTYPES = {'norm1_g': _jnp.float32, 'w_in': _jnp.float32, 'hgrn_lb_logits': _jnp.float32, 'hgrn_norm_g': _jnp.float32, 'gdn_conv_w': _jnp.float32, 'gdn_a_log': _jnp.float32, 'gdn_dt_bias': _jnp.float32, 'gdn_norm_g': _jnp.float32, 'lru_conv_w': _jnp.float32, 'lru_conv_b': _jnp.float32, 'lru_w_a': _jnp.float32, 'lru_b_a': _jnp.float32, 'lru_w_x': _jnp.float32, 'lru_b_x': _jnp.float32, 'lru_lambda': _jnp.float32, 'w_out': _jnp.float32, 'norm2_g': _jnp.float32, 'w_up': _jnp.float32, 'w_down': _jnp.float32, 'final_norm_g': _jnp.float32}
MOMENT_SCALE = {'norm1_g': 2.312057e-01, 'w_in': 1.212652e-01, 'hgrn_lb_logits': 7.900745e-03, 'hgrn_norm_g': 2.644436e-01, 'gdn_conv_w': 9.147659e-02, 'gdn_a_log': 4.406318e-01, 'gdn_dt_bias': 4.357132e-01, 'gdn_norm_g': 3.044701e-01, 'lru_conv_w': 3.699059e-01, 'lru_conv_b': 1.343311e+00, 'lru_w_a': 5.191493e-02, 'lru_b_a': 7.388321e-02, 'lru_w_x': 1.027129e-01, 'lru_b_x': 1.327631e-01, 'lru_lambda': 1.741606e-01, 'w_out': 1.672654e-01, 'norm2_g': 2.002122e-01, 'w_up': 9.828637e-02, 'w_down': 2.427401e-01, 'final_norm_g': 6.622887e+01}


def _to_microbatches(a, axis):
    t = _jnp.moveaxis(a, axis, 0)
    t = t.reshape((N_MICROBATCH, t.shape[0] // N_MICROBATCH) + t.shape[1:])
    return _jnp.moveaxis(t, 1, axis + 1)


def setup_inputs(seed: int = 0) -> dict:
    inp = _fwd_setup_inputs(seed)
    key = _jax.random.fold_in(_jax.random.key(seed), 7919)
    shape, _ = _output_shape()
    out = dict(inp)
    out["loss_target"] = _jax.random.normal(_jax.random.fold_in(key, 0), shape, _jnp.float32)
    for i, name in enumerate(TWIN_WEIGHTS):
        w = inp[name].astype(_jnp.float32)
        if MOMENT_SCALE is None:
            s = _jnp.sqrt(_jnp.mean(_jnp.square(w)) + 1e-30)
        else:
            s = MOMENT_SCALE[name]
        km, kv = _jax.random.split(_jax.random.fold_in(key, i + 1))
        out[name] = w
        out["m_" + name] = s * _jax.random.normal(km, w.shape, _jnp.float32)
        out["v_" + name] = (s * s) * _jax.random.uniform(kv, w.shape, _jnp.float32, 0.5, 1.5)
    if N_MICROBATCH > 1:
        for name, axis in PER_EXAMPLE_BATCH_AXIS.items():
            out[name] = _to_microbatches(out[name], axis)
    return {'x': out['x'], 'norm1_g': out['norm1_g'], 'w_in': out['w_in'], 'hgrn_lb_logits': out['hgrn_lb_logits'], 'hgrn_norm_g': out['hgrn_norm_g'], 'gdn_conv_w': out['gdn_conv_w'], 'gdn_a_log': out['gdn_a_log'], 'gdn_dt_bias': out['gdn_dt_bias'], 'gdn_norm_g': out['gdn_norm_g'], 'lru_conv_w': out['lru_conv_w'], 'lru_conv_b': out['lru_conv_b'], 'lru_w_a': out['lru_w_a'], 'lru_b_a': out['lru_b_a'], 'lru_w_x': out['lru_w_x'], 'lru_b_x': out['lru_b_x'], 'lru_lambda': out['lru_lambda'], 'w_out': out['w_out'], 'norm2_g': out['norm2_g'], 'w_up': out['w_up'], 'w_down': out['w_down'], 'final_norm_g': out['final_norm_g'], 'loss_target': out['loss_target'], 'm_norm1_g': out['m_norm1_g'], 'm_w_in': out['m_w_in'], 'm_hgrn_lb_logits': out['m_hgrn_lb_logits'], 'm_hgrn_norm_g': out['m_hgrn_norm_g'], 'm_gdn_conv_w': out['m_gdn_conv_w'], 'm_gdn_a_log': out['m_gdn_a_log'], 'm_gdn_dt_bias': out['m_gdn_dt_bias'], 'm_gdn_norm_g': out['m_gdn_norm_g'], 'm_lru_conv_w': out['m_lru_conv_w'], 'm_lru_conv_b': out['m_lru_conv_b'], 'm_lru_w_a': out['m_lru_w_a'], 'm_lru_b_a': out['m_lru_b_a'], 'm_lru_w_x': out['m_lru_w_x'], 'm_lru_b_x': out['m_lru_b_x'], 'm_lru_lambda': out['m_lru_lambda'], 'm_w_out': out['m_w_out'], 'm_norm2_g': out['m_norm2_g'], 'm_w_up': out['m_w_up'], 'm_w_down': out['m_w_down'], 'm_final_norm_g': out['m_final_norm_g'], 'v_norm1_g': out['v_norm1_g'], 'v_w_in': out['v_w_in'], 'v_hgrn_lb_logits': out['v_hgrn_lb_logits'], 'v_hgrn_norm_g': out['v_hgrn_norm_g'], 'v_gdn_conv_w': out['v_gdn_conv_w'], 'v_gdn_a_log': out['v_gdn_a_log'], 'v_gdn_dt_bias': out['v_gdn_dt_bias'], 'v_gdn_norm_g': out['v_gdn_norm_g'], 'v_lru_conv_w': out['v_lru_conv_w'], 'v_lru_conv_b': out['v_lru_conv_b'], 'v_lru_w_a': out['v_lru_w_a'], 'v_lru_b_a': out['v_lru_b_a'], 'v_lru_w_x': out['v_lru_w_x'], 'v_lru_b_x': out['v_lru_b_x'], 'v_lru_lambda': out['v_lru_lambda'], 'v_w_out': out['v_w_out'], 'v_norm2_g': out['v_norm2_g'], 'v_w_up': out['v_w_up'], 'v_w_down': out['v_w_down'], 'v_final_norm_g': out['v_final_norm_g']}


def _loss(weights, diff, rest, loss_target):
    with _jax.named_scope("forward"):
        args = {**rest, TWIN_DIFF_INPUT: diff, **{k: w.astype(_WEIGHT_DTYPES[k]) for k, w in weights.items()}}
        y = _forward(args)
    with _jax.named_scope("loss_head"):
        err = _jnp.square(y.astype(_jnp.float32) - loss_target)
        return 0.5 * _jnp.sum(_jnp.mean(err, axis=-1)) if err.ndim else 0.5 * err


def _adamw(w, g, m, v):
    m = ADAM_B1 * m + (1.0 - ADAM_B1) * g
    v = ADAM_B2 * v + (1.0 - ADAM_B2) * _jnp.square(g)
    m_hat = m / (1.0 - ADAM_B1 ** ADAM_STEP)
    v_hat = v / (1.0 - ADAM_B2 ** ADAM_STEP)
    delta = -ADAM_LR * (m_hat / (_jnp.sqrt(v_hat) + ADAM_EPS) + ADAM_WD * w)
    return delta, m, v


def reference(x, norm1_g, w_in, hgrn_lb_logits, hgrn_norm_g, gdn_conv_w, gdn_a_log, gdn_dt_bias, gdn_norm_g, lru_conv_w, lru_conv_b, lru_w_a, lru_b_a, lru_w_x, lru_b_x, lru_lambda, w_out, norm2_g, w_up, w_down, final_norm_g, loss_target, m_norm1_g, m_w_in, m_hgrn_lb_logits, m_hgrn_norm_g, m_gdn_conv_w, m_gdn_a_log, m_gdn_dt_bias, m_gdn_norm_g, m_lru_conv_w, m_lru_conv_b, m_lru_w_a, m_lru_b_a, m_lru_w_x, m_lru_b_x, m_lru_lambda, m_w_out, m_norm2_g, m_w_up, m_w_down, m_final_norm_g, v_norm1_g, v_w_in, v_hgrn_lb_logits, v_hgrn_norm_g, v_gdn_conv_w, v_gdn_a_log, v_gdn_dt_bias, v_gdn_norm_g, v_lru_conv_w, v_lru_conv_b, v_lru_w_a, v_lru_b_a, v_lru_w_x, v_lru_b_x, v_lru_lambda, v_w_out, v_norm2_g, v_w_up, v_w_down, v_final_norm_g):
    given = dict(x=x, norm1_g=norm1_g, w_in=w_in, hgrn_lb_logits=hgrn_lb_logits, hgrn_norm_g=hgrn_norm_g, gdn_conv_w=gdn_conv_w, gdn_a_log=gdn_a_log, gdn_dt_bias=gdn_dt_bias, gdn_norm_g=gdn_norm_g, lru_conv_w=lru_conv_w, lru_conv_b=lru_conv_b, lru_w_a=lru_w_a, lru_b_a=lru_b_a, lru_w_x=lru_w_x, lru_b_x=lru_b_x, lru_lambda=lru_lambda, w_out=w_out, norm2_g=norm2_g, w_up=w_up, w_down=w_down, final_norm_g=final_norm_g, loss_target=loss_target, m_norm1_g=m_norm1_g, m_w_in=m_w_in, m_hgrn_lb_logits=m_hgrn_lb_logits, m_hgrn_norm_g=m_hgrn_norm_g, m_gdn_conv_w=m_gdn_conv_w, m_gdn_a_log=m_gdn_a_log, m_gdn_dt_bias=m_gdn_dt_bias, m_gdn_norm_g=m_gdn_norm_g, m_lru_conv_w=m_lru_conv_w, m_lru_conv_b=m_lru_conv_b, m_lru_w_a=m_lru_w_a, m_lru_b_a=m_lru_b_a, m_lru_w_x=m_lru_w_x, m_lru_b_x=m_lru_b_x, m_lru_lambda=m_lru_lambda, m_w_out=m_w_out, m_norm2_g=m_norm2_g, m_w_up=m_w_up, m_w_down=m_w_down, m_final_norm_g=m_final_norm_g, v_norm1_g=v_norm1_g, v_w_in=v_w_in, v_hgrn_lb_logits=v_hgrn_lb_logits, v_hgrn_norm_g=v_hgrn_norm_g, v_gdn_conv_w=v_gdn_conv_w, v_gdn_a_log=v_gdn_a_log, v_gdn_dt_bias=v_gdn_dt_bias, v_gdn_norm_g=v_gdn_norm_g, v_lru_conv_w=v_lru_conv_w, v_lru_conv_b=v_lru_conv_b, v_lru_w_a=v_lru_w_a, v_lru_b_a=v_lru_b_a, v_lru_w_x=v_lru_w_x, v_lru_b_x=v_lru_b_x, v_lru_lambda=v_lru_lambda, v_w_out=v_w_out, v_norm2_g=v_norm2_g, v_w_up=v_w_up, v_w_down=v_w_down, v_final_norm_g=v_final_norm_g)
    weights = {n: given[n] for n in TWIN_WEIGHTS}
    shared = {n: given[n] for n in SHARED_INPUTS}
    per_example = {n: given[n] for n in ['x']}
    grad_fn = _jax.value_and_grad(_loss, argnums=(0, 1))

    def one_microbatch(ex, loss_target):
        ex = dict(ex)
        diff = ex.pop(TWIN_DIFF_INPUT)
        return grad_fn(weights, diff, {**shared, **ex}, loss_target)

    if N_MICROBATCH == 1:
        loss, (grad_w, grad_x) = one_microbatch(per_example, given["loss_target"])
    else:
        def body(carry, xs):
            loss_sum, grad_sum = carry
            l_k, (gw_k, gx_k) = one_microbatch(xs[0], xs[1])
            with _jax.named_scope("update"):
                return (loss_sum + l_k, _jax.tree.map(_jnp.add, grad_sum, gw_k)), gx_k

        init = (_jnp.zeros((), _jnp.float32), _jax.tree.map(_jnp.zeros_like, weights))
        (loss, grad_w), grad_x = _jax.lax.scan(body, init, (per_example, given["loss_target"]))
    with _jax.named_scope("update"):
        delta_w, new_m, new_v = {}, {}, {}
        for n in TWIN_WEIGHTS:
            delta_w[n], new_m[n], new_v[n] = _adamw(weights[n], grad_w[n], given["m_" + n], given["v_" + n])
    return (loss, grad_x, *[grad_w[n] for n in TWIN_WEIGHTS], *[delta_w[n] for n in TWIN_WEIGHTS],
            *[new_m[n] for n in TWIN_WEIGHTS], *[new_v[n] for n in TWIN_WEIGHTS])
```

```python
import functools

import jax
import jax.numpy as jnp
from jax import lax
from jax.experimental import pallas as pl
from jax.experimental.pallas import tpu as pltpu

F32 = jnp.float32
BF16 = jnp.bfloat16
HI = lax.Precision.HIGHEST
MESH = pl.DeviceIdType.MESH

N_DEV = 8
D_MODEL = 1024
DEPTH = 4
A_HEADS, A_DIM, A_WIDTH = 4, 64, 256
B_HEADS, B_DIM, B_WIDTH = 4, 128, 512
C_WIDTH, C_BLOCKS, C_BLOCK_DIM = 256, 4, 64
D_IN = 3592
D_IN_PAD = 3840
COL_A, COL_B, COL_C, COL_G = 0, 1024, 3072, 3584
D_FF = 4096
CONV_K = 4
CHUNK = 64
SUB = 16
RG_C = 8.0
EPS = 1e-6
TINY = 1e-30
EXP_CLAMP = 80.0
GDN_SCALE = B_DIM ** -0.5
ADAM_LR, ADAM_B1, ADAM_B2, ADAM_EPS, ADAM_WD, ADAM_STEP = 0.001, 0.9, 0.999, 1e-08, 0.01, 10
VMEM_LIMIT = 56 * 1024 * 1024


def _cparams(sem=None):
    return pltpu.CompilerParams(dimension_semantics=sem, vmem_limit_bytes=VMEM_LIMIT)


def _dot(a, b, mode="nn", hi=False):
    dims = {"nn": (((1,), (0,)), ((), ())), "nt": (((1,), (1,)), ((), ())), "tn": (((0,), (0,)), ((), ()))}[mode]
    if hi:
        return lax.dot_general(a.astype(F32), b.astype(F32), dims, precision=HI, preferred_element_type=F32)
    return lax.dot_general(a.astype(BF16), b.astype(BF16), dims, preferred_element_type=F32)


def _iota2(n, m):
    return lax.broadcasted_iota(jnp.int32, (n, m), 0), lax.broadcasted_iota(jnp.int32, (n, m), 1)


def _tril(n, strict=False):
    r, c = _iota2(n, n)
    return (r > c) if strict else (r >= c)


def _sigmoid(x):
    return 1.0 / (1.0 + jnp.exp(-x))


def _softplus(x):
    return jnp.maximum(x, 0.0) + jnp.log(1.0 + jnp.exp(-jnp.abs(x)))


def _neg_expm1(z):
    series = -z * (1.0 + z * (0.5 + z * (1.0 / 6.0)))
    return jnp.where(z > -1e-2, series, 1.0 - jnp.exp(z))


def _gelu_tanh(x):
    c = 0.7978845608028654
    u = c * (x + 0.044715 * x * x * x)
    t = jnp.tanh(u)
    g = 0.5 * x * (1.0 + t)
    dg = 0.5 * (1.0 + t) + 0.5 * x * (1.0 - t * t) * c * (1.0 + 3.0 * 0.044715 * x * x)
    return g, dg


def _shift_rows(cur, halo, s, down=True):
    n = cur.shape[0]
    ridx = lax.broadcasted_iota(jnp.int32, (8, cur.shape[1]), 0)
    if down:
        main = pltpu.roll(cur, s, 0)
        fix = jnp.where(ridx < s, pltpu.roll(halo, s, 0), main[0:8])
        return jnp.concatenate([fix, main[8:]], axis=0)
    main = pltpu.roll(cur, n - s, 0)
    fix = jnp.where(ridx >= 8 - s, pltpu.roll(halo, 8 - s, 0), main[n - 8:n])
    return jnp.concatenate([main[:n - 8], fix], axis=0)


def _conv_fwd(cur, prev8, w):
    y = cur * w[3:4]
    for j in range(3):
        y = y + _shift_rows(cur, prev8, 3 - j, down=True) * w[j:j + 1]
    return y


def _pick_tile(n, pref):
    best = None
    for cand in range(128, min(n, pref) + 1, 128):
        if n % cand == 0:
            best = cand
    return best if best is not None else n


def _matmul(a, b, mode, out_dtype=F32, residual=None, tm=512, tn=1024, tk=1024, name="matmul"):
    if mode == "nn":
        (m, k), n = a.shape, b.shape[1]
    elif mode == "nt":
        (m, k), n = a.shape, b.shape[0]
    else:
        (k, m), n = a.shape, b.shape[1]
    tm, tn, tk = _pick_tile(m, tm), _pick_tile(n, tn), _pick_tile(k, tk)
    nk = k // tk
    a_spec = pl.BlockSpec((tk, tm), lambda i, j, kk: (kk, i)) if mode == "tn" else pl.BlockSpec((tm, tk), lambda i, j, kk: (i, kk))
    b_spec = pl.BlockSpec((tn, tk), lambda i, j, kk: (j, kk)) if mode == "nt" else pl.BlockSpec((tk, tn), lambda i, j, kk: (kk, j))
    o_spec = pl.BlockSpec((tm, tn), lambda i, j, kk: (i, j))
    has_res = residual is not None

    def body(*refs):
        if has_res:
            a_ref, b_ref, r_ref, o_ref, acc = refs
        else:
            a_ref, b_ref, o_ref, acc = refs
        kk = pl.program_id(2)

        @pl.when(kk == 0)
        def _():
            acc[...] = jnp.zeros_like(acc)

        acc[...] += _dot(a_ref[...], b_ref[...], mode)

        @pl.when(kk == nk - 1)
        def _():
            r = acc[...]
            if has_res:
                r = r + r_ref[...]
            o_ref[...] = r.astype(out_dtype)

    ins = [a, b] + ([residual] if has_res else [])
    specs = [a_spec, b_spec] + ([o_spec] if has_res else [])
    return pl.pallas_call(
        body, grid=(m // tm, n // tn, nk), in_specs=specs, out_specs=o_spec,
        out_shape=jax.ShapeDtypeStruct((m, n), out_dtype), scratch_shapes=[pltpu.VMEM((tm, tn), F32)],
        compiler_params=_cparams(("parallel", "parallel", "arbitrary")), name=name)(*ins)


def _rms_fwd(x, g, tb=512, name="rms_fwd"):
    t, d = x.shape

    def body(x_ref, g_ref, h_ref):
        xv = x_ref[...]
        rinv = lax.rsqrt(jnp.mean(xv * xv, axis=-1, keepdims=True) + EPS)
        h_ref[...] = (xv * rinv * g_ref[...]).astype(BF16)

    return pl.pallas_call(
        body, grid=(t // tb,), in_specs=[pl.BlockSpec((tb, d), lambda i: (i, 0)), pl.BlockSpec((1, d), lambda i: (0, 0))],
        out_specs=pl.BlockSpec((tb, d), lambda i: (i, 0)), out_shape=jax.ShapeDtypeStruct((t, d), BF16),
        compiler_params=_cparams(("parallel",)), name=name)(x, g.reshape(1, d))


def _rms_bwd(dh, x, g, dres, tb=512, name="rms_bwd"):
    t, d = x.shape

    def body(dh_ref, x_ref, g_ref, dres_ref, dx_ref, dg_ref):
        @pl.when(pl.program_id(0) == 0)
        def _():
            dg_ref[...] = jnp.zeros_like(dg_ref)

        xv = x_ref[...]
        dhv = dh_ref[...].astype(F32)
        rinv = lax.rsqrt(jnp.mean(xv * xv, axis=-1, keepdims=True) + EPS)
        xhat = xv * rinv
        dxh = dhv * g_ref[...]
        dx_ref[...] = dres_ref[...] + rinv * (dxh - xhat * jnp.mean(dxh * xhat, axis=-1, keepdims=True))
        dg_ref[...] += jnp.sum(dhv * xhat, axis=0, keepdims=True)

    row = pl.BlockSpec((tb, d), lambda i: (i, 0))
    vec = pl.BlockSpec((1, d), lambda i: (0, 0))
    return pl.pallas_call(
        body, grid=(t // tb,), in_specs=[row, row, vec, row], out_specs=[row, vec],
        out_shape=[jax.ShapeDtypeStruct((t, d), F32), jax.ShapeDtypeStruct((1, d), F32)],
        compiler_params=_cparams(("arbitrary",)), name=name)(dh, x, g.reshape(1, d), dres)


def _loss_head(x, g, target, tb=512):
    t, d = x.shape

    def body(x_ref, g_ref, t_ref, loss_ref, dx_ref, dg_ref):
        @pl.when(pl.program_id(0) == 0)
        def _():
            dg_ref[...] = jnp.zeros_like(dg_ref)
            loss_ref[...] = jnp.zeros_like(loss_ref)

        xv = x_ref[...]
        rinv = lax.rsqrt(jnp.mean(xv * xv, axis=-1, keepdims=True) + EPS)
        xhat = xv * rinv
        err = xhat * g_ref[...] - t_ref[...]
        loss_ref[...] += 0.5 * jnp.sum(jnp.mean(err * err, axis=-1, keepdims=True), axis=0, keepdims=True)
        dy = err * (1.0 / d)
        dxh = dy * g_ref[...]
        dx_ref[...] = rinv * (dxh - xhat * jnp.mean(dxh * xhat, axis=-1, keepdims=True))
        dg_ref[...] += jnp.sum(dy * xhat, axis=0, keepdims=True)

    row = pl.BlockSpec((tb, d), lambda i: (i, 0))
    vec = pl.BlockSpec((1, d), lambda i: (0, 0))
    one = pl.BlockSpec((1, 1), lambda i: (0, 0))
    return pl.pallas_call(
        body, grid=(t // tb,), in_specs=[row, vec, row], out_specs=[one, row, vec],
        out_shape=[jax.ShapeDtypeStruct((1, 1), F32), jax.ShapeDtypeStruct((t, d), F32), jax.ShapeDtypeStruct((1, d), F32)],
        compiler_params=_cparams(("arbitrary",)), name="loss_head")(x, g.reshape(1, d), target)


def _relu2_fwd(up, tb=512):
    t, d = up.shape

    def body(u_ref, a_ref):
        r = jnp.maximum(u_ref[...], 0.0)
        a_ref[...] = (r * r).astype(BF16)

    row = pl.BlockSpec((tb, d), lambda i: (i, 0))
    return pl.pallas_call(body, grid=(t // tb,), in_specs=[row], out_specs=row, out_shape=jax.ShapeDtypeStruct((t, d), BF16),
                          compiler_params=_cparams(("parallel",)), name="relu2_fwd")(up)


def _relu2_bwd(dact, up, tb=512):
    t, d = up.shape

    def body(da_ref, u_ref, o_ref):
        o_ref[...] = (da_ref[...] * 2.0 * jnp.maximum(u_ref[...], 0.0)).astype(BF16)

    row = pl.BlockSpec((tb, d), lambda i: (i, 0))
    return pl.pallas_call(body, grid=(t // tb,), in_specs=[row, row], out_specs=row, out_shape=jax.ShapeDtypeStruct((t, d), BF16),
                          compiler_params=_cparams(("parallel",)), name="relu2_bwd")(dact, up)


def _lb_fwd(logits):
    def body(l_ref, o_ref):
        lg = l_ref[...]
        e = jnp.exp(lg - jnp.max(lg, axis=0, keepdims=True))
        p = e / jnp.sum(e, axis=0, keepdims=True)
        c = jnp.zeros_like(p[0:1])
        rows = [c]
        for l in range(1, DEPTH):
            c = c + p[l:l + 1]
            rows.append(c)
        o_ref[...] = jnp.minimum(jnp.maximum(jnp.concatenate(rows, axis=0), 0.0), 1.0 - EPS)

    return pl.pallas_call(body, out_shape=jax.ShapeDtypeStruct(logits.shape, F32), name="lb_fwd")(logits)


def _lb_bwd(logits, dlb):
    def body(l_ref, d_ref, o_ref):
        lg = l_ref[...]
        e = jnp.exp(lg - jnp.max(lg, axis=0, keepdims=True))
        p = e / jnp.sum(e, axis=0, keepdims=True)
        hi = 1.0 - EPS
        c = jnp.zeros_like(p[0:1])
        dc = []
        for l in range(1, DEPTH):
            c = c + p[l:l + 1]
            gl = jnp.where(c < 0.0, 0.0, jnp.where(c == 0.0, 0.5, 1.0)) * jnp.where(c > hi, 0.0, jnp.where(c == hi, 0.5, 1.0))
            dc.append(d_ref[l:l + 1, :] * gl)
        dp = [jnp.zeros_like(c)]
        for j in range(1, DEPTH):
            s = dc[j - 1]
            for l in range(j + 1, DEPTH):
                s = s + dc[l - 1]
            dp.append(s)
        dpm = jnp.concatenate(dp, axis=0)
        o_ref[...] = p * (dpm - jnp.sum(p * dpm, axis=0, keepdims=True))

    return pl.pallas_call(body, out_shape=jax.ShapeDtypeStruct(logits.shape, F32), name="lb_bwd")(logits, dlb)


def _a_gates(qi, fi, lbh):
    sq = _sigmoid(qi)
    q = qi * sq
    sg = _sigmoid(fi)
    sgn = _sigmoid(-fi)
    f = lbh + (1.0 - lbh) * sg
    logf = jnp.log(jnp.maximum(f, TINY))
    k = (1.0 - lbh) * sgn
    return q, sq, sg, sgn, f, logf, k


def _a_intra(q, k, cum):
    qts, kes, rows = [], [], []
    for i in range(CHUNK // SUB):
        lo = i * SUB
        r = cum[lo - 1:lo] if i > 0 else jnp.zeros_like(cum[0:1])
        eq = jnp.exp(cum[lo:lo + SUB] - r)
        ek = jnp.exp(jnp.minimum(r - cum, EXP_CLAMP))
        qt = q[lo:lo + SUB] * eq
        rows.append(_dot(qt, k * ek, "nt", hi=True))
        qts.append((qt, eq))
        kes.append(ek)
    attn = jnp.where(_tril(CHUNK), jnp.concatenate(rows, axis=0), 0.0)
    return attn, qts, kes


def _a_intra_bwd(dattn, k, qts, kes):
    dq_rows = []
    dk = jnp.zeros_like(k)
    for i in range(CHUNK // SUB):
        lo = i * SUB
        da = dattn[lo:lo + SUB]
        qt, eq = qts[i]
        dq_rows.append(_dot(da, k * kes[i], "nn", hi=True) * eq)
        dk = dk + _dot(da, qt, "tn", hi=True) * kes[i]
    return jnp.concatenate(dq_rows, axis=0), dk


def _headnorm_fwd(o, g, gate_in):
    rinv = lax.rsqrt(jnp.mean(o * o, axis=-1, keepdims=True) + EPS)
    sg = _sigmoid(gate_in)
    return o * rinv * g * (gate_in * sg)


def _headnorm_bwd(dout, o, g, gate_in):
    rinv = lax.rsqrt(jnp.mean(o * o, axis=-1, keepdims=True) + EPS)
    xhat = o * rinv
    sg = _sigmoid(gate_in)
    silu = gate_in * sg
    dy = dout * silu
    dgate = dout * xhat * g * (sg * (1.0 + gate_in * (1.0 - sg)))
    dxh = dy * g
    do = rinv * (dxh - xhat * jnp.mean(dxh * xhat, axis=-1, keepdims=True))
    return do, dgate, jnp.sum(dy * xhat, axis=0, keepdims=True)


def _a_fwd(proj, lb, norm_g, tb=256):
    t = proj.shape[0]
    nch = tb // CHUNK

    def body(q_ref, f_ref, i_ref, g_ref, lb_ref, ng_ref, out_ref, st_ref, s_scr):
        @pl.when(pl.program_id(0) == 0)
        def _():
            s_scr[...] = jnp.zeros_like(s_scr)

        ltri = _tril(CHUNK).astype(F32)

        def chunk(c, carry):
            rows = pl.ds(pl.multiple_of(c * CHUNK, CHUNK), CHUNK)
            outs = []
            for h in range(A_HEADS):
                cols = slice(h * A_DIM, (h + 1) * A_DIM)
                q, _, _, _, _, logf, k = _a_gates(q_ref[rows, cols], f_ref[rows, cols], lb_ref[:, cols])
                v = i_ref[rows, cols]
                cum = _dot(ltri, logf, hi=True)
                cl = cum[CHUNK - 1:CHUNK]
                s0 = s_scr[h]
                st_ref[c, h] = s0
                attn, _, _ = _a_intra(q, k, cum)
                o = _dot(q * jnp.exp(cum), s0) + _dot(attn, v)
                kd = k * jnp.exp(cl - cum)
                s_scr[h] = s0 * jnp.exp(cl).T + _dot(kd, v, "tn")
                outs.append(_headnorm_fwd(o, ng_ref[...], g_ref[rows, cols]))
            out_ref[rows, :] = jnp.concatenate(outs, axis=1).astype(BF16)
            return carry

        lax.fori_loop(0, nch, chunk, 0)

    colblk = lambda j: pl.BlockSpec((tb, A_WIDTH), lambda i, j=j: (i, j))
    return pl.pallas_call(
        body, grid=(t // tb,),
        in_specs=[colblk(0), colblk(1), colblk(2), colblk(3), pl.BlockSpec((1, A_WIDTH), lambda i: (0, 0)),
                  pl.BlockSpec((1, A_DIM), lambda i: (0, 0))],
        out_specs=[pl.BlockSpec((tb, A_WIDTH), lambda i: (i, 0)),
                   pl.BlockSpec((nch, A_HEADS, A_DIM, A_DIM), lambda i: (i, 0, 0, 0))],
        out_shape=[jax.ShapeDtypeStruct((t, A_WIDTH), BF16), jax.ShapeDtypeStruct((t // CHUNK, A_HEADS, A_DIM, A_DIM), F32)],
        scratch_shapes=[pltpu.VMEM((A_HEADS, A_DIM, A_DIM), F32)],
        compiler_params=_cparams(("arbitrary",)), name="hgrn_fwd")(proj, proj, proj, proj, lb.reshape(1, A_WIDTH), norm_g.reshape(1, A_DIM))


def _a_bwd(proj, lb, norm_g, states, dmixed, tb=256):
    t = proj.shape[0]
    nch = tb // CHUNK
    nb = t // tb

    def body(q_ref, f_ref, i_ref, g_ref, lb_ref, ng_ref, st_ref, dm_ref, dp_ref, dlb_ref, dng_ref, ds_scr):
        @pl.when(pl.program_id(0) == 0)
        def _():
            ds_scr[...] = jnp.zeros_like(ds_scr)
            dlb_ref[...] = jnp.zeros_like(dlb_ref)
            dng_ref[...] = jnp.zeros_like(dng_ref)

        ltri = _tril(CHUNK).astype(F32)
        mask = _tril(CHUNK)

        def chunk(cc, carry):
            c = nch - 1 - cc
            rows = pl.ds(pl.multiple_of(c * CHUNK, CHUNK), CHUNK)
            dqs, dfs, dis, dgs, dlbs = [], [], [], [], []
            dng = jnp.zeros((1, A_DIM), F32)
            for h in range(A_HEADS):
                cols = slice(h * A_DIM, (h + 1) * A_DIM)
                qi, fi, gi = q_ref[rows, cols], f_ref[rows, cols], g_ref[rows, cols]
                lbh = lb_ref[:, cols]
                q, sq, sg, sgn, f, logf, k = _a_gates(qi, fi, lbh)
                v = i_ref[rows, cols]
                cum = _dot(ltri, logf, hi=True)
                cl = cum[CHUNK - 1:CHUNK]
                ecum = jnp.exp(cum)
                ekd = jnp.exp(cl - cum)
                cd = jnp.exp(cl)
                qd, kd = q * ecum, k * ekd
                s0 = st_ref[c, h]
                ds = ds_scr[h]
                attn, qts, kes = _a_intra(q, k, cum)
                o = _dot(qd, s0) + _dot(attn, v)
                do, dgi, dng_h = _headnorm_bwd(dm_ref[rows, cols].astype(F32), o, ng_ref[...], gi)
                dng = dng + dng_h
                dqd = _dot(do, s0, "nt")
                dattn = jnp.where(mask, _dot(do, v, "nt"), 0.0)
                dv = _dot(attn, do, "tn") + _dot(kd, ds)
                dkd = _dot(v, ds, "nt")
                dcd = jnp.sum((s0 * ds).T, axis=0, keepdims=True)
                ds_scr[h] = _dot(qd, do, "tn") + ds * cd.T
                dq_i, dk_i = _a_intra_bwd(dattn, k, qts, kes)
                dq = dqd * ecum + dq_i
                dk = dkd * ekd + dk_i
                dkk = dkd * kd
                dcum = dqd * qd - dkk + q * dq_i - k * dk_i
                dcl = jnp.sum(dkk, axis=0, keepdims=True) + dcd * cd
                dlogf = _dot(ltri, dcum, "tn", hi=True) + dcl
                dfv = jnp.where(f > TINY, dlogf / f, 0.0)
                dfi = dfv * (1.0 - lbh) * sg * (1.0 - sg) - dk * (1.0 - lbh) * sgn * (1.0 - sgn)
                dlbs.append(jnp.sum(dfv * (1.0 - sg) - dk * sgn, axis=0, keepdims=True))
                dqs.append(dq * (sq * (1.0 + qi * (1.0 - sq))))
                dfs.append(dfi)
                dis.append(dv)
                dgs.append(dgi)
            dp_ref[rows, :] = jnp.concatenate(dqs + dfs + dis + dgs, axis=1).astype(BF16)
            dlb_ref[...] += jnp.concatenate(dlbs, axis=1)
            dng_ref[...] += dng
            return carry

        lax.fori_loop(0, nch, chunk, 0)

    colblk = lambda j: pl.BlockSpec((tb, A_WIDTH), lambda i, j=j: (nb - 1 - i, j))
    vec = lambda n: pl.BlockSpec((1, n), lambda i: (0, 0))
    outs = pl.pallas_call(
        body, grid=(nb,),
        in_specs=[colblk(0), colblk(1), colblk(2), colblk(3), vec(A_WIDTH), vec(A_DIM),
                  pl.BlockSpec((nch, A_HEADS, A_DIM, A_DIM), lambda i: (nb - 1 - i, 0, 0, 0)), colblk(0)],
        out_specs=[pl.BlockSpec((tb, 4 * A_WIDTH), lambda i: (nb - 1 - i, 0)), vec(A_WIDTH), vec(A_DIM)],
        out_shape=[jax.ShapeDtypeStruct((t, 4 * A_WIDTH), BF16), jax.ShapeDtypeStruct((1, A_WIDTH), F32), jax.ShapeDtypeStruct((1, A_DIM), F32)],
        scratch_shapes=[pltpu.VMEM((A_HEADS, A_DIM, A_DIM), F32)],
        compiler_params=_cparams(("arbitrary",)), name="hgrn_bwd")(
            proj, proj, proj, proj, lb.reshape(1, A_WIDTH), norm_g.reshape(1, A_DIM), states, dmixed)
    return outs


def _gate_lane_masks(shape):
    lane = lax.broadcasted_iota(jnp.int32, shape, 1)
    return lane < B_HEADS, (lane >= B_HEADS) & (lane < 2 * B_HEADS)


def _b_pre_fwd(proj, conv_w, alog_row, dtb_row, tb=512):
    t = proj.shape[0]
    cb0 = COL_B // B_WIDTH

    def body(q_ref, k_ref, v_ref, qp_ref, kp_ref, vp_ref, w_ref, gi_ref, al_ref, dt_ref, qkv_ref, gates_ref):
        first = pl.program_id(0) == 0
        for part, (c_ref, p_ref) in enumerate(((q_ref, qp_ref), (k_ref, kp_ref), (v_ref, vp_ref))):
            cols = slice(part * B_WIDTH, (part + 1) * B_WIDTH)
            prev = jnp.where(first, 0.0, p_ref[...])
            y = _conv_fwd(c_ref[...], prev, w_ref[:, cols])
            s = y * _sigmoid(y)
            if part < 2:
                outs = []
                for h in range(B_HEADS):
                    sh = s[:, h * B_DIM:(h + 1) * B_DIM]
                    outs.append(sh * lax.rsqrt(jnp.sum(sh * sh, axis=-1, keepdims=True) + EPS))
                s = jnp.concatenate(outs, axis=1)
            qkv_ref[:, cols] = s
        g = gi_ref[...]
        is_b, is_a = _gate_lane_masks(g.shape)
        la = -jnp.exp(al_ref[...]) * _softplus(g + dt_ref[...])
        gates_ref[...] = jnp.where(is_b, _sigmoid(g), jnp.where(is_a, la, 0.0))

    cur = lambda j: pl.BlockSpec((tb, B_WIDTH), lambda i, j=j: (i, cb0 + j))
    prv = lambda j: pl.BlockSpec((8, B_WIDTH), lambda i, j=j: (jnp.maximum(i * (tb // 8) - 1, 0), cb0 + j))
    vec = pl.BlockSpec((1, 128), lambda i: (0, 0))
    return pl.pallas_call(
        body, grid=(t // tb,),
        in_specs=[cur(0), cur(1), cur(2), prv(0), prv(1), prv(2), pl.BlockSpec((CONV_K, 3 * B_WIDTH), lambda i: (0, 0)),
                  pl.BlockSpec((tb, 128), lambda i: (i, COL_G // 128)), vec, vec],
        out_specs=[pl.BlockSpec((tb, 3 * B_WIDTH), lambda i: (i, 0)), pl.BlockSpec((tb, 128), lambda i: (i, 0))],
        out_shape=[jax.ShapeDtypeStruct((t, 3 * B_WIDTH), F32), jax.ShapeDtypeStruct((t, 128), F32)],
        compiler_params=_cparams(("parallel",)), name="gdn_pre_fwd")(proj, proj, proj, proj, proj, proj, conv_w, proj, alog_row, dtb_row)


def _inv_unit_lower(a):
    r, c = _iota2(CHUNK, CHUNK)
    p = jnp.where(r == c, 1.0, 0.0) - a
    ak = a
    for _ in range(5):
        ak = _dot(ak, ak, hi=True)
        p = p + _dot(p, ak, hi=True)
    return p


def _b_chunk(q, k, v, beta, gc, grow, gl, s0):
    causal = _tril(CHUNK)
    decay = jnp.where(causal, jnp.exp(jnp.minimum(gc - grow, 0.0)), 0.0)
    kb = k * beta
    kk = _dot(kb, k, "nt")
    amat = jnp.where(_tril(CHUNK, strict=True), kk * decay, 0.0)
    tinv = _inv_unit_lower(amat)
    eg = jnp.exp(gc)
    bv = v * beta
    kg = kb * eg
    u = _dot(tinv, bv, hi=True)
    w = _dot(tinv, kg, hi=True)
    qkr = _dot(q, k, "nt")
    qk = qkr * decay
    qd = q * eg
    ekd = jnp.exp(gl - gc)
    kd = k * ekd
    cd = jnp.exp(gl)
    vn = u - _dot(w, s0)
    o = _dot(qd, s0) + _dot(qk, vn)
    s1 = s0 * cd + _dot(kd, vn, "tn")
    return dict(decay=decay, kb=kb, kk=kk, tinv=tinv, eg=eg, bv=bv, kg=kg, w=w, qkr=qkr, qk=qk, qd=qd, ekd=ekd,
                kd=kd, cd=cd, vn=vn, o=o, s1=s1)


def _b_fwd(qkv, gates, proj, norm_g, tb=256):
    t = qkv.shape[0]
    nch = tb // CHUNK

    def body(q_ref, k_ref, v_ref, ga_ref, z_ref, ng_ref, out_ref, st_ref, s_scr):
        @pl.when(pl.program_id(0) == 0)
        def _():
            s_scr[...] = jnp.zeros_like(s_scr)

        ltri = _tril(CHUNK).astype(F32)

        def chunk(c, carry):
            rows = pl.ds(pl.multiple_of(c * CHUNK, CHUNK), CHUNK)
            ga = ga_ref[rows, :]
            gcum = _dot(ltri, ga, hi=True)
            gcum_t = gcum.T
            outs = []
            for h in range(B_HEADS):
                cols = slice(h * B_DIM, (h + 1) * B_DIM)
                ia = B_HEADS + h
                s0 = s_scr[h]
                st_ref[c, h] = s0
                r = _b_chunk(q_ref[rows, cols] * GDN_SCALE, k_ref[rows, cols], v_ref[rows, cols], ga[:, h:h + 1],
                             gcum[:, ia:ia + 1], gcum_t[ia:ia + 1, :], gcum[CHUNK - 1:CHUNK, ia:ia + 1], s0)
                s_scr[h] = r["s1"]
                outs.append(_headnorm_fwd(r["o"], ng_ref[...], z_ref[rows, cols]))
            out_ref[rows, :] = jnp.concatenate(outs, axis=1).astype(BF16)
            return carry

        lax.fori_loop(0, nch, chunk, 0)

    part = lambda j: pl.BlockSpec((tb, B_WIDTH), lambda i, j=j: (i, j))
    return pl.pallas_call(
        body, grid=(t // tb,),
        in_specs=[part(0), part(1), part(2), pl.BlockSpec((tb, 128), lambda i: (i, 0)),
                  pl.BlockSpec((tb, B_WIDTH), lambda i: (i, COL_B // B_WIDTH + 3)), pl.BlockSpec((1, B_DIM), lambda i: (0, 0))],
        out_specs=[pl.BlockSpec((tb, B_WIDTH), lambda i: (i, 0)),
                   pl.BlockSpec((nch, B_HEADS, B_DIM, B_DIM), lambda i: (i, 0, 0, 0))],
        out_shape=[jax.ShapeDtypeStruct((t, B_WIDTH), BF16), jax.ShapeDtypeStruct((t // CHUNK, B_HEADS, B_DIM, B_DIM), F32)],
        scratch_shapes=[pltpu.VMEM((B_HEADS, B_DIM, B_DIM), F32)],
        compiler_params=_cparams(("arbitrary",)), name="gdn_fwd")(qkv, qkv, qkv, gates, proj, norm_g.reshape(1, B_DIM))


def _b_bwd(qkv, gates, proj, norm_g, states, dmixed, tb=256):
    t = qkv.shape[0]
    nch = tb // CHUNK
    nb = t // tb

    def body(q_ref, k_ref, v_ref, ga_ref, z_ref, ng_ref, st_ref, dm0_ref, dm1_ref, dqkv_ref, dga_ref, dz_ref, dng_ref, ds_scr):
        @pl.when(pl.program_id(0) == 0)
        def _():
            ds_scr[...] = jnp.zeros_like(ds_scr)
            dng_ref[...] = jnp.zeros_like(dng_ref)

        ltri = _tril(CHUNK).astype(F32)
        strict = _tril(CHUNK, strict=True)
        lane = lax.broadcasted_iota(jnp.int32, (CHUNK, 128), 1)
        lane1 = lax.broadcasted_iota(jnp.int32, (1, 128), 1)

        def chunk(cc, carry):
            c = nch - 1 - cc
            rows = pl.ds(pl.multiple_of(c * CHUNK, CHUNK), CHUNK)
            ga = ga_ref[rows, :]
            gcum = _dot(ltri, ga, hi=True)
            gcum_t = gcum.T
            dqs, dks, dvs, dzs = [], [], [], []
            dng = jnp.zeros((1, B_DIM), F32)
            dbeta_m = jnp.zeros((CHUNK, 128), F32)
            dgc_m = jnp.zeros((CHUNK, 128), F32)
            dgl_m = jnp.zeros((1, 128), F32)
            for h in range(B_HEADS):
                cols = slice(h * B_DIM, (h + 1) * B_DIM)
                ia = B_HEADS + h
                q = q_ref[rows, cols] * GDN_SCALE
                k, v, z = k_ref[rows, cols], v_ref[rows, cols], z_ref[rows, cols]
                beta = ga[:, h:h + 1]
                gc = gcum[:, ia:ia + 1]
                s0 = st_ref[c, h]
                ds = ds_scr[h]
                r = _b_chunk(q, k, v, beta, gc, gcum_t[ia:ia + 1, :], gcum[CHUNK - 1:CHUNK, ia:ia + 1], s0)
                dm_ref = dm0_ref if h < 2 else dm1_ref
                dcols = slice((h % 2) * B_DIM, (h % 2 + 1) * B_DIM)
                do, dz, dng_h = _headnorm_bwd(dm_ref[rows, dcols].astype(F32), r["o"], ng_ref[...], z)
                dng = dng + dng_h
                dzs.append(dz)
                decay, tinv, eg, w, vn, qd, kd, kb = r["decay"], r["tinv"], r["eg"], r["w"], r["vn"], r["qd"], r["kd"], r["kb"]
                dvn = _dot(r["qk"], do, "tn") + _dot(kd, ds)
                dqk = _dot(do, vn, "nt")
                dqd = _dot(do, s0, "nt")
                dkd = _dot(vn, ds, "nt")
                dcd = jnp.sum(jnp.sum(s0 * ds, axis=0, keepdims=True), axis=1, keepdims=True)
                dw = -_dot(dvn, s0, "nt")
                ds_scr[h] = _dot(qd, do, "tn") + ds * r["cd"] - _dot(w, dvn, "tn")
                dbv = _dot(tinv, dvn, "tn", hi=True)
                dkg = _dot(tinv, dw, "tn", hi=True)
                dt = _dot(dvn, r["bv"], "nt", hi=True) + _dot(dw, r["kg"], "nt", hi=True)
                da = jnp.where(strict, -_dot(_dot(tinv, dt, "tn", hi=True), tinv, "nt", hi=True), 0.0)
                dm = da * decay
                dn = dqk * decay
                e = (da * r["kk"] + dqk * r["qkr"]) * decay
                dkb = _dot(dm, k) + dkg * eg
                dk = _dot(dm, kb, "tn") + _dot(dn, q, "tn") + dkd * r["ekd"]
                dq = _dot(dn, k) + dqd * eg
                tkd = jnp.sum(dkd * kd, axis=-1, keepdims=True)
                dgc = (jnp.sum(e, axis=-1, keepdims=True) - jnp.sum(e.T, axis=-1, keepdims=True)
                       + jnp.sum(dqd * qd, axis=-1, keepdims=True) - tkd + jnp.sum(dkg * r["kg"], axis=-1, keepdims=True))
                dgl = jnp.sum(tkd, axis=0, keepdims=True) + dcd * r["cd"]
                dk = dk + dkb * beta
                dbeta = jnp.sum(dbv * v, axis=-1, keepdims=True) + jnp.sum(dkb * k, axis=-1, keepdims=True)
                dqs.append(dq * GDN_SCALE)
                dks.append(dk)
                dvs.append(dbv * beta)
                dbeta_m = dbeta_m + jnp.where(lane == h, dbeta, 0.0)
                dgc_m = dgc_m + jnp.where(lane == ia, dgc, 0.0)
                dgl_m = dgl_m + jnp.where(lane1 == ia, dgl, 0.0)
            dqkv_ref[rows, :] = jnp.concatenate(dqs + dks + dvs, axis=1)
            dz_ref[rows, :] = jnp.concatenate(dzs, axis=1).astype(BF16)
            dga_ref[rows, :] = dbeta_m + _dot(ltri, dgc_m, "tn", hi=True) + dgl_m
            dng_ref[...] += dng
            return carry

        lax.fori_loop(0, nch, chunk, 0)

    part = lambda j: pl.BlockSpec((tb, B_WIDTH), lambda i, j=j: (nb - 1 - i, j))
    rowblk = lambda w, j=0: pl.BlockSpec((tb, w), lambda i, j=j: (nb - 1 - i, j))
    return pl.pallas_call(
        body, grid=(nb,),
        in_specs=[part(0), part(1), part(2), rowblk(128), rowblk(B_WIDTH, COL_B // B_WIDTH + 3),
                  pl.BlockSpec((1, B_DIM), lambda i: (0, 0)),
                  pl.BlockSpec((nch, B_HEADS, B_DIM, B_DIM), lambda i: (nb - 1 - i, 0, 0, 0)),
                  rowblk(256, 1), rowblk(256, 2)],
        out_specs=[rowblk(3 * B_WIDTH), rowblk(128), rowblk(B_WIDTH), pl.BlockSpec((1, B_DIM), lambda i: (0, 0))],
        out_shape=[jax.ShapeDtypeStruct((t, 3 * B_WIDTH), F32), jax.ShapeDtypeStruct((t, 128), F32),
                   jax.ShapeDtypeStruct((t, B_WIDTH), BF16), jax.ShapeDtypeStruct((1, B_DIM), F32)],
        scratch_shapes=[pltpu.VMEM((B_HEADS, B_DIM, B_DIM), F32)],
        compiler_params=_cparams(("arbitrary",)), name="gdn_bwd")(
            qkv, qkv, qkv, gates, proj, norm_g.reshape(1, B_DIM), states, dmixed, dmixed)


def _b_pre_bwd(proj, conv_w, alog_row, dtb_row, dqkv, dgates, tb=512):
    t = proj.shape[0]
    cb0 = COL_B // B_WIDTH

    def body(q_ref, k_ref, v_ref, qp_ref, kp_ref, vp_ref, w_ref, gi_ref, al_ref, dt_ref, dqkv_ref, dga_ref,
             dy_ref, dgi_ref, dw_ref, dal_ref, ddt_ref):
        first = pl.program_id(0) == 0

        @pl.when(first)
        def _():
            dw_ref[...] = jnp.zeros_like(dw_ref)
            dal_ref[...] = jnp.zeros_like(dal_ref)
            ddt_ref[...] = jnp.zeros_like(ddt_ref)

        for part, (c_ref, p_ref) in enumerate(((q_ref, qp_ref), (k_ref, kp_ref), (v_ref, vp_ref))):
            cols = slice(part * B_WIDTH, (part + 1) * B_WIDTH)
            cur = c_ref[...]
            prev = jnp.where(first, 0.0, p_ref[...])
            w = w_ref[:, cols]
            shifted = [_shift_rows(cur, prev, 3 - j, down=True) for j in range(3)] + [cur]
            y = shifted[0] * w[0:1] + shifted[1] * w[1:2] + shifted[2] * w[2:3] + shifted[3] * w[3:4]
            sg = _sigmoid(y)
            s = y * sg
            dsn = dqkv_ref[:, cols]
            if part < 2:
                outs = []
                for h in range(B_HEADS):
                    hc = slice(h * B_DIM, (h + 1) * B_DIM)
                    sh, dh = s[:, hc], dsn[:, hc]
                    rq = lax.rsqrt(jnp.sum(sh * sh, axis=-1, keepdims=True) + EPS)
                    nh = sh * rq
                    outs.append(rq * (dh - nh * jnp.sum(dh * nh, axis=-1, keepdims=True)))
                dsn = jnp.concatenate(outs, axis=1)
            dy = dsn * (sg * (1.0 + y * (1.0 - sg)))
            dy_ref[:, cols] = dy
            dw_ref[:, cols] += jnp.concatenate([jnp.sum(shifted[j] * dy, axis=0, keepdims=True) for j in range(CONV_K)], axis=0)
        g = gi_ref[...]
        dga = dga_ref[...]
        is_b, is_a = _gate_lane_masks(g.shape)
        beta = _sigmoid(g)
        pre = g + dt_ref[...]
        ea = jnp.exp(al_ref[...])
        la = -ea * _softplus(pre)
        dpre = jnp.where(is_a, dga * (-ea) * _sigmoid(pre), 0.0)
        dgi_ref[...] = jnp.where(is_b, dga * beta * (1.0 - beta), dpre).astype(BF16)
        dal_ref[...] += jnp.sum(jnp.where(is_a, dga * la, 0.0), axis=0, keepdims=True)
        ddt_ref[...] += jnp.sum(dpre, axis=0, keepdims=True)

    cur = lambda j: pl.BlockSpec((tb, B_WIDTH), lambda i, j=j: (i, cb0 + j))
    prv = lambda j: pl.BlockSpec((8, B_WIDTH), lambda i, j=j: (jnp.maximum(i * (tb // 8) - 1, 0), cb0 + j))
    vec = pl.BlockSpec((1, 128), lambda i: (0, 0))
    wspec = pl.BlockSpec((CONV_K, 3 * B_WIDTH), lambda i: (0, 0))
    return pl.pallas_call(
        body, grid=(t // tb,),
        in_specs=[cur(0), cur(1), cur(2), prv(0), prv(1), prv(2), wspec,
                  pl.BlockSpec((tb, 128), lambda i: (i, COL_G // 128)), vec, vec,
                  pl.BlockSpec((tb, 3 * B_WIDTH), lambda i: (i, 0)), pl.BlockSpec((tb, 128), lambda i: (i, 0))],
        out_specs=[pl.BlockSpec((tb, 3 * B_WIDTH), lambda i: (i, 0)), pl.BlockSpec((tb, 128), lambda i: (i, 0)), wspec, vec, vec],
        out_shape=[jax.ShapeDtypeStruct((t, 3 * B_WIDTH), F32), jax.ShapeDtypeStruct((t, 128), BF16),
                   jax.ShapeDtypeStruct((CONV_K, 3 * B_WIDTH), F32), jax.ShapeDtypeStruct((1, 128), F32), jax.ShapeDtypeStruct((1, 128), F32)],
        compiler_params=_cparams(("arbitrary",)), name="gdn_pre_bwd")(
            proj, proj, proj, proj, proj, proj, conv_w, proj, alog_row, dtb_row, dqkv, dgates)


def _conv_bwd_x(dy, w, cb, tb=512, name="conv_bwd_x"):
    t, c = dy.shape
    nb = t // tb

    def body(dy_ref, nx_ref, w_ref, dx_ref):
        cur = dy_ref[...]
        nxt = jnp.where(pl.program_id(0) == nb - 1, 0.0, nx_ref[...])
        w = w_ref[...]
        dx = cur * w[3:4]
        for j in range(3):
            dx = dx + _shift_rows(cur, nxt, 3 - j, down=False) * w[j:j + 1]
        dx_ref[...] = dx.astype(BF16)

    return pl.pallas_call(
        body, grid=(nb, c // cb),
        in_specs=[pl.BlockSpec((tb, cb), lambda i, j: (i, j)),
                  pl.BlockSpec((8, cb), lambda i, j: (jnp.minimum((i + 1) * (tb // 8), t // 8 - 1), j)),
                  pl.BlockSpec((CONV_K, cb), lambda i, j: (0, j))],
        out_specs=pl.BlockSpec((tb, cb), lambda i, j: (i, j)), out_shape=jax.ShapeDtypeStruct((t, c), BF16),
        compiler_params=_cparams(("parallel", "parallel")), name=name)(dy, dy, w)


def _c_gates(xc, wa_ref, ba_ref, wx_ref, bx_ref, lam_ref, is_row0):
    r = _sigmoid(_dot(xc, wa_ref[...]) + ba_ref[...])
    i = _sigmoid(_dot(xc, wx_ref[...]) + bx_ref[...])
    sp = _softplus(-lam_ref[...])
    log_a = -RG_C * r * sp
    a = jnp.exp(log_a)
    m2 = _neg_expm1(2.0 * log_a)
    mult = jnp.where(is_row0, 1.0, jnp.sqrt(jnp.maximum(m2, EPS)))
    return r, i, sp, log_a, a, m2, mult


def _row0_mask(tb, first):
    ridx = lax.broadcasted_iota(jnp.int32, (tb, C_WIDTH), 0)
    return (ridx == 0) & first


def _c_fwd(proj, conv_w, conv_b, wa, ba, wx, bx, lam, tb=512):
    t = proj.shape[0]
    cbx = COL_C // C_WIDTH

    def body(x_ref, xp_ref, y_ref, w_ref, cb_ref, wa_ref, ba_ref, wx_ref, bx_ref, lam_ref, out_ref, h_ref, a_scr, b_scr, h_scr):
        first = pl.program_id(0) == 0

        @pl.when(first)
        def _():
            h_scr[...] = jnp.zeros_like(h_scr)

        prev = jnp.where(first, 0.0, xp_ref[...])
        xc = _conv_fwd(x_ref[...], prev, w_ref[...]) + cb_ref[...]
        _, i, _, _, a, _, mult = _c_gates(xc, wa_ref, ba_ref, wx_ref, bx_ref, lam_ref, _row0_mask(tb, first))
        a_scr[...] = a
        b_scr[...] = mult * i * xc

        def step(blk, h):
            rows = pl.ds(pl.multiple_of(blk * 8, 8), 8)
            at, bt = a_scr[rows, :], b_scr[rows, :]
            out = []
            for j in range(8):
                h = at[j:j + 1] * h + bt[j:j + 1]
                out.append(h)
            h_ref[rows, :] = jnp.concatenate(out, axis=0)
            return h

        h_scr[...] = lax.fori_loop(0, tb // 8, step, h_scr[...])
        gl, _ = _gelu_tanh(y_ref[...])
        out_ref[...] = (gl * h_ref[...]).astype(BF16)

    vec = pl.BlockSpec((1, C_WIDTH), lambda i: (0, 0))
    mat = pl.BlockSpec((C_WIDTH, C_WIDTH), lambda i: (0, 0))
    row = pl.BlockSpec((tb, C_WIDTH), lambda i: (i, 0))
    return pl.pallas_call(
        body, grid=(t // tb,),
        in_specs=[pl.BlockSpec((tb, C_WIDTH), lambda i: (i, cbx)),
                  pl.BlockSpec((8, C_WIDTH), lambda i: (jnp.maximum(i * (tb // 8) - 1, 0), cbx)),
                  pl.BlockSpec((tb, C_WIDTH), lambda i: (i, cbx + 1)),
                  pl.BlockSpec((CONV_K, C_WIDTH), lambda i: (0, 0)), vec, mat, vec, mat, vec, vec],
        out_specs=[row, row],
        out_shape=[jax.ShapeDtypeStruct((t, C_WIDTH), BF16), jax.ShapeDtypeStruct((t, C_WIDTH), F32)],
        scratch_shapes=[pltpu.VMEM((tb, C_WIDTH), F32), pltpu.VMEM((tb, C_WIDTH), F32), pltpu.VMEM((1, C_WIDTH), F32)],
        compiler_params=_cparams(("arbitrary",)), name="lru_fwd")(proj, proj, proj, conv_w, conv_b, wa, ba, wx, bx, lam)


def _c_bwd(proj, conv_w, conv_b, wa, ba, wx, bx, lam, hs, dmixed, tb=512):
    t = proj.shape[0]
    nb = t // tb
    cbx = COL_C // C_WIDTH

    def body(x_ref, xp_ref, y_ref, w_ref, cb_ref, wa_ref, ba_ref, wx_ref, bx_ref, lam_ref, h_ref, hp_ref, dm_ref,
             dxc_ref, dyg_ref, dw_ref, dcb_ref, dwa_ref, dba_ref, dwx_ref, dbx_ref, dlam_ref, g_scr, a_scr, c_scr):
        step_id = pl.program_id(0)
        first = step_id == nb - 1

        @pl.when(step_id == 0)
        def _():
            c_scr[...] = jnp.zeros_like(c_scr)
            for ref in (dw_ref, dcb_ref, dwa_ref, dba_ref, dwx_ref, dbx_ref, dlam_ref):
                ref[...] = jnp.zeros_like(ref)

        cur = x_ref[...]
        prev = jnp.where(first, 0.0, xp_ref[...])
        w = w_ref[...]
        shifted = [_shift_rows(cur, prev, 3 - j, down=True) for j in range(3)] + [cur]
        xc = shifted[0] * w[0:1] + shifted[1] * w[1:2] + shifted[2] * w[2:3] + shifted[3] * w[3:4] + cb_ref[...]
        row0 = _row0_mask(tb, first)
        r, i, sp, log_a, a, m2, mult = _c_gates(xc, wa_ref, ba_ref, wx_ref, bx_ref, lam_ref, row0)
        h = h_ref[...]
        hprev = _shift_rows(h, jnp.where(first, 0.0, hp_ref[...]), 1, down=True)
        gl, dgl = _gelu_tanh(y_ref[...])
        dm = dm_ref[...].astype(F32)
        dyg_ref[...] = (dm * h * dgl).astype(BF16)
        g_scr[...] = dm * gl
        a_scr[...] = a

        def step(blk, carry):
            b = tb // 8 - 1 - blk
            rows = pl.ds(pl.multiple_of(b * 8, 8), 8)
            at, gt = a_scr[rows, :], g_scr[rows, :]
            out = [None] * 8
            for j in range(7, -1, -1):
                gj = gt[j:j + 1] + carry
                out[j] = gj
                carry = at[j:j + 1] * gj
            g_scr[rows, :] = jnp.concatenate(out, axis=0)
            return carry

        c_scr[...] = lax.fori_loop(0, tb // 8, step, c_scr[...])
        dbx = g_scr[...]
        da = dbx * hprev
        dmult = jnp.where(row0, 0.0, dbx * i * xc)
        di = dbx * mult * xc
        dxc = dbx * mult * i
        dm2 = jnp.where(m2 > EPS, dmult * 0.5 / mult, 0.0)
        dlog_a = da * a - 2.0 * a * a * dm2
        dr = dlog_a * (-RG_C) * sp
        dlam_ref[...] += jnp.sum(dlog_a * (-RG_C) * r, axis=0, keepdims=True) * (-_sigmoid(-lam_ref[...]))
        dpa = dr * r * (1.0 - r)
        dpx = di * i * (1.0 - i)
        dba_ref[...] += jnp.sum(dpa, axis=0, keepdims=True)
        dbx_ref[...] += jnp.sum(dpx, axis=0, keepdims=True)
        dwa_ref[...] += _dot(xc, dpa, "tn")
        dwx_ref[...] += _dot(xc, dpx, "tn")
        dxc = dxc + _dot(dpa, wa_ref[...], "nt") + _dot(dpx, wx_ref[...], "nt")
        dxc_ref[...] = dxc
        dcb_ref[...] += jnp.sum(dxc, axis=0, keepdims=True)
        dw_ref[...] += jnp.concatenate([jnp.sum(shifted[j] * dxc, axis=0, keepdims=True) for j in range(CONV_K)], axis=0)

    vec = pl.BlockSpec((1, C_WIDTH), lambda i: (0, 0))
    mat = pl.BlockSpec((C_WIDTH, C_WIDTH), lambda i: (0, 0))
    cw = pl.BlockSpec((CONV_K, C_WIDTH), lambda i: (0, 0))
    row = lambda j=0: pl.BlockSpec((tb, C_WIDTH), lambda i, j=j: (nb - 1 - i, j))
    halo = lambda j=0: pl.BlockSpec((8, C_WIDTH), lambda i, j=j: (jnp.maximum((nb - 1 - i) * (tb // 8) - 1, 0), j))
    return pl.pallas_call(
        body, grid=(nb,),
        in_specs=[row(cbx), halo(cbx), row(cbx + 1), cw, vec, mat, vec, mat, vec, vec, row(), halo(), row(3)],
        out_specs=[row(), row(), cw, vec, mat, vec, mat, vec, vec],
        out_shape=[jax.ShapeDtypeStruct((t, C_WIDTH), F32), jax.ShapeDtypeStruct((t, C_WIDTH), BF16),
                   jax.ShapeDtypeStruct((CONV_K, C_WIDTH), F32), jax.ShapeDtypeStruct((1, C_WIDTH), F32),
                   jax.ShapeDtypeStruct((C_WIDTH, C_WIDTH), F32), jax.ShapeDtypeStruct((1, C_WIDTH), F32),
                   jax.ShapeDtypeStruct((C_WIDTH, C_WIDTH), F32), jax.ShapeDtypeStruct((1, C_WIDTH), F32),
                   jax.ShapeDtypeStruct((1, C_WIDTH), F32)],
        scratch_shapes=[pltpu.VMEM((tb, C_WIDTH), F32), pltpu.VMEM((tb, C_WIDTH), F32), pltpu.VMEM((1, C_WIDTH), F32)],
        compiler_params=_cparams(("arbitrary",)), name="lru_bwd")(
            proj, proj, proj, conv_w, conv_b, wa, ba, wx, bx, lam, hs, hs, dmixed)


def _mesh_pos():
    return lax.axis_index("x"), lax.axis_index("y"), lax.axis_index("c")


def _all_gather(x, name):
    def body(x_ref, out_ref, send_sems, recv_sems, local_sem):
        mx, my, mc = _mesh_pos()
        me, sibling = (mx, my, mc), (mx, my, 1 - mc)
        chips = [(1 - mx, my), (mx, 1 - my), (1 - mx, 1 - my)]

        def slot(px, py, pc):
            return out_ref.at[4 * px + 2 * py + pc]

        def copy(k, block, to, src=None):
            return pltpu.make_async_remote_copy(
                src_ref=slot(*block) if src is None else src, dst_ref=slot(*block),
                send_sem=send_sems.at[k], recv_sem=recv_sems.at[k], device_id=to, device_id_type=MESH)

        mine = pltpu.make_async_copy(x_ref, slot(*me), local_sem)
        mine.start()
        first = [copy(0, me, sibling, src=x_ref)]
        first += [copy(1 + j, me, (*chip, mc), src=x_ref) for j, chip in enumerate(chips)]
        for cp in first:
            cp.start()
        passed = [copy(4 + j, (*chip, mc), sibling) for j, chip in enumerate(chips)]
        for j, chip in enumerate(chips):
            copy(1 + j, (*chip, mc), me).wait_recv()
            passed[j].start()
        copy(0, sibling, me).wait_recv()
        for j, chip in enumerate(chips):
            copy(4 + j, (*chip, 1 - mc), me).wait_recv()
        for cp in first + passed:
            cp.wait_send()
        mine.wait()

    return pl.pallas_call(
        body, out_shape=jax.ShapeDtypeStruct((N_DEV,) + x.shape, x.dtype),
        in_specs=[pl.BlockSpec(memory_space=pl.ANY)], out_specs=pl.BlockSpec(memory_space=pl.ANY),
        scratch_shapes=[pltpu.SemaphoreType.DMA((7,)), pltpu.SemaphoreType.DMA((7,)), pltpu.SemaphoreType.DMA(())],
        name=name)(x)


def _all_to_all(x, name):
    def body(x_ref, out_ref, send_sems, recv_sems, local_sem):
        mx, my, mc = _mesh_pos()
        me = 4 * mx + 2 * my + mc
        mine = pltpu.make_async_copy(x_ref.at[me], out_ref.at[me], local_sem)
        mine.start()
        copies = []
        for k in range(1, N_DEV):
            px = 1 - mx if k & 4 else mx
            py = 1 - my if k & 2 else my
            pc = 1 - mc if k & 1 else mc
            copies.append(pltpu.make_async_remote_copy(
                src_ref=x_ref.at[4 * px + 2 * py + pc], dst_ref=out_ref.at[me],
                send_sem=send_sems.at[k - 1], recv_sem=recv_sems.at[k - 1], device_id=(px, py, pc), device_id_type=MESH))
        for cp in copies:
            cp.start()
        for cp in copies:
            cp.wait()
        mine.wait()

    return pl.pallas_call(
        body, out_shape=jax.ShapeDtypeStruct(x.shape, x.dtype),
        in_specs=[pl.BlockSpec(memory_space=pl.ANY)], out_specs=pl.BlockSpec(memory_space=pl.ANY),
        scratch_shapes=[pltpu.SemaphoreType.DMA((7,)), pltpu.SemaphoreType.DMA((7,)), pltpu.SemaphoreType.DMA(())],
        name=name)(x)


def _adamw_math(w, g, m, v):
    m = ADAM_B1 * m + (1.0 - ADAM_B1) * g
    v = ADAM_B2 * v + (1.0 - ADAM_B2) * (g * g)
    m_hat = m / (1.0 - ADAM_B1 ** ADAM_STEP)
    v_hat = v / (1.0 - ADAM_B2 ** ADAM_STEP)
    delta = -ADAM_LR * (m_hat / (jnp.sqrt(v_hat) + ADAM_EPS) + ADAM_WD * w)
    return delta, m, v


def _sum_adamw(parts, w, m, v, tr, name):
    p, r, c = parts.shape
    tr = min(tr, r)
    assert r % tr == 0

    def body(p_ref, w_ref, m_ref, v_ref, g_ref, d_ref, nm_ref, nv_ref):
        g = p_ref[0]
        for j in range(1, p):
            g = g + p_ref[j]
        delta, nm, nv = _adamw_math(w_ref[...], g, m_ref[...], v_ref[...])
        g_ref[...] = g
        d_ref[...] = delta
        nm_ref[...] = nm
        nv_ref[...] = nv

    row = pl.BlockSpec((tr, c), lambda i: (i, 0))
    return pl.pallas_call(
        body, grid=(r // tr,), in_specs=[pl.BlockSpec((p, tr, c), lambda i: (0, i, 0)), row, row, row],
        out_specs=[row] * 4, out_shape=[jax.ShapeDtypeStruct((r, c), F32)] * 4,
        compiler_params=_cparams(("parallel",)), name=name)(parts, w, m, v)


def _sum_parts(parts, name):
    p, r, c = parts.shape

    def body(p_ref, o_ref):
        g = p_ref[0]
        for j in range(1, p):
            g = g + p_ref[j]
        o_ref[...] = g

    return pl.pallas_call(body, out_shape=jax.ShapeDtypeStruct((r, c), F32), name=name)(parts)


def _pack(arrs, mult=1024):
    flat = jnp.concatenate([a.reshape(-1).astype(F32) for a in arrs])
    n = flat.shape[0]
    npad = -n % mult
    return jnp.pad(flat, (0, npad)).reshape(-1, 128)


def _unpack(buf, shapes):
    flat = buf.reshape(-1)
    out, off = [], 0
    for s in shapes:
        n = 1
        for d in s:
            n *= d
        out.append(flat[off:off + n].reshape(s))
        off += n
    return out


def _block_diag(w):
    z = jnp.zeros((C_BLOCKS, C_BLOCK_DIM, C_BLOCKS, C_BLOCK_DIM), w.dtype)
    for i in range(C_BLOCKS):
        z = z.at[i, :, i, :].set(w[i])
    return z.reshape(C_WIDTH, C_WIDTH)


def _diag_blocks(m):
    m4 = m.reshape(C_BLOCKS, C_BLOCK_DIM, C_BLOCKS, C_BLOCK_DIM)
    return jnp.stack([m4[i, :, i, :] for i in range(C_BLOCKS)])


def _gate_row(v):
    return jnp.zeros((1, 128), F32).at[0, B_HEADS:2 * B_HEADS].set(v.astype(F32))


def _permute_w_in(w):
    pad = jnp.zeros(w.shape[:-1] + (D_IN_PAD - D_IN,), w.dtype)
    return jnp.concatenate([w[..., :3072], w[..., 3080:3592], w[..., 3072:3080], pad], axis=-1)


def _unpermute_w_in(w):
    return jnp.concatenate([w[..., :3072], w[..., COL_G:COL_G + 8], w[..., 3072:COL_G]], axis=-1)


_WEIGHTS = ['norm1_g', 'w_in', 'hgrn_lb_logits', 'hgrn_norm_g', 'gdn_conv_w', 'gdn_a_log', 'gdn_dt_bias', 'gdn_norm_g',
            'lru_conv_w', 'lru_conv_b', 'lru_w_a', 'lru_b_a', 'lru_w_x', 'lru_b_x', 'lru_lambda', 'w_out', 'norm2_g',
            'w_up', 'w_down', 'final_norm_g']
_BIG = ('w_in', 'w_out', 'w_up', 'w_down')
_SHARDED_SMALL = ('gdn_conv_w', 'lru_conv_w')


def _step(x, target, w, m, v):
    t = x.shape[0]
    mx, my, mc = _mesh_pos()
    me = 4 * mx + 2 * my + mc

    g_in = _all_gather(w['w_in'].astype(BF16), "ag_w_in")
    g_out = _all_gather(w['w_out'].astype(BF16), "ag_w_out")
    g_up = _all_gather(w['w_up'].astype(BF16), "ag_w_up")
    g_down = _all_gather(w['w_down'].astype(BF16), "ag_w_down")
    w_in = _permute_w_in(jnp.moveaxis(g_in, 0, 2).reshape(DEPTH, D_MODEL, D_IN))
    w_out = jnp.moveaxis(g_out, 0, 1).reshape(DEPTH, D_MODEL, D_MODEL)
    w_up = jnp.moveaxis(g_up, 0, 2).reshape(DEPTH, D_MODEL, D_FF)
    w_down = jnp.moveaxis(g_down, 0, 1).reshape(DEPTH, D_FF, D_MODEL)
    conv_shapes = [w['gdn_conv_w'].shape, w['lru_conv_w'].shape]
    g_conv = _all_gather(_pack([w['gdn_conv_w'], w['lru_conv_w']]), "ag_conv")
    gdn_cw, lru_cw = [], []
    for j in range(N_DEV):
        a, b = _unpack(g_conv[j], conv_shapes)
        gdn_cw.append(a)
        lru_cw.append(b)
    gdn_cw = jnp.concatenate(gdn_cw, axis=-1)
    lru_cw = jnp.concatenate(lru_cw, axis=-1)

    lbnd = _lb_fwd(w['hgrn_lb_logits'])
    row = lambda a: a.reshape(1, -1)

    def c_args(l):
        return (lru_cw[l], row(w['lru_conv_b'][l]), _block_diag(w['lru_w_a'][l]), row(w['lru_b_a'][l]),
                _block_diag(w['lru_w_x'][l]), row(w['lru_b_x'][l]), row(w['lru_lambda'][l]))

    saved = []
    xl = x
    for l in range(DEPTH):
        h = _rms_fwd(xl, w['norm1_g'][l], name="rms_fwd")
        proj = _matmul(h, w_in[l], "nn", name="mm_proj")
        mix_a, st_a = _a_fwd(proj, lbnd[l], w['hgrn_norm_g'][l])
        alr, dtr = _gate_row(w['gdn_a_log'][l]), _gate_row(w['gdn_dt_bias'][l])
        qkv, gates = _b_pre_fwd(proj, gdn_cw[l], alr, dtr)
        mix_b, st_b = _b_fwd(qkv, gates, proj, w['gdn_norm_g'][l])
        mix_c, hs = _c_fwd(proj, *c_args(l))
        mixed = jnp.concatenate([mix_a, mix_b, mix_c], axis=1)
        x_mid = _matmul(mixed, w_out[l], "nn", residual=xl, name="mm_out")
        h2 = _rms_fwd(x_mid, w['norm2_g'][l], name="rms_fwd")
        up = _matmul(h2, w_up[l], "nn", name="mm_up")
        act = _relu2_fwd(up)
        x_next = _matmul(act, w_down[l], "nn", residual=x_mid, name="mm_down")
        saved.append(dict(x=xl, h=h, proj=proj, st_a=st_a, qkv=qkv, gates=gates, st_b=st_b, hs=hs, mixed=mixed,
                          x_mid=x_mid, h2=h2, up=up, act=act, alr=alr, dtr=dtr))
        xl = x_next
    loss, dx, dgf = _loss_head(xl, w['final_norm_g'], target)

    gs = {n: [None] * DEPTH for n in _WEIGHTS}
    for l in reversed(range(DEPTH)):
        s = saved[l]
        dact = _matmul(dx, w_down[l], "nt", name="mm_dact")
        gs['w_down'][l] = _matmul(s['act'], dx, "tn", name="mm_dw_down")
        dup = _relu2_bwd(dact, s['up'])
        dh2 = _matmul(dup, w_up[l], "nt", name="mm_dh2")
        gs['w_up'][l] = _matmul(s['h2'], dup, "tn", name="mm_dw_up")
        dx_mid, dg2 = _rms_bwd(dh2, s['x_mid'], w['norm2_g'][l], dx, name="rms_bwd")
        gs['norm2_g'][l] = dg2[0]
        dmixed = _matmul(dx_mid, w_out[l], "nt", name="mm_dmixed")
        gs['w_out'][l] = _matmul(s['mixed'], dx_mid, "tn", name="mm_dw_out")
        proj = s['proj']
        dpa, dlb, dnga = _a_bwd(proj, lbnd[l], w['hgrn_norm_g'][l], s['st_a'], dmixed)
        gs['hgrn_lb_logits'][l] = dlb[0]
        gs['hgrn_norm_g'][l] = dnga[0]
        dqkv, dgates, dz, dngb = _b_bwd(s['qkv'], s['gates'], proj, w['gdn_norm_g'][l], s['st_b'], dmixed)
        dyb, dgi, dcwb, dal, ddt = _b_pre_bwd(proj, gdn_cw[l], s['alr'], s['dtr'], dqkv, dgates)
        dxb = _conv_bwd_x(dyb, gdn_cw[l], B_WIDTH, name="conv_bwd_x_gdn")
        gs['gdn_norm_g'][l] = dngb[0]
        gs['gdn_conv_w'][l] = dcwb
        gs['gdn_a_log'][l] = dal[0, B_HEADS:2 * B_HEADS]
        gs['gdn_dt_bias'][l] = ddt[0, B_HEADS:2 * B_HEADS]
        dxc, dyg, dcwc, dcb, dwa, dba, dwx, dbx, dlam = _c_bwd(proj, *c_args(l), s['hs'], dmixed)
        dxc_in = _conv_bwd_x(dxc, lru_cw[l], C_WIDTH, name="conv_bwd_x_lru")
        gs['lru_conv_w'][l] = dcwc
        gs['lru_conv_b'][l] = dcb[0]
        gs['lru_w_a'][l] = _diag_blocks(dwa)
        gs['lru_b_a'][l] = dba[0]
        gs['lru_w_x'][l] = _diag_blocks(dwx)
        gs['lru_b_x'][l] = dbx[0]
        gs['lru_lambda'][l] = dlam[0]
        pad = jnp.zeros((t, D_IN_PAD - COL_G - 128), BF16)
        dproj = jnp.concatenate([dpa, dxb, dz, dxc_in, dyg, dgi, pad], axis=1)
        dh = _matmul(dproj, w_in[l], "nt", name="mm_dh")
        gs['w_in'][l] = _matmul(s['h'], dproj, "tn", name="mm_dw_in")
        dx, dg1 = _rms_bwd(dh, s['x'], w['norm1_g'][l], dx_mid, name="rms_bwd")
        gs['norm1_g'][l] = dg1[0]
    grad_x = dx
    part = {n: jnp.stack(gs[n]) for n in _WEIGHTS if n != 'final_norm_g'}
    part['final_norm_g'] = dgf[0]
    part['hgrn_lb_logits'] = _lb_bwd(w['hgrn_lb_logits'], part['hgrn_lb_logits'])

    send = {
        'w_in': jnp.moveaxis(_unpermute_w_in(part['w_in']).reshape(DEPTH, D_MODEL, N_DEV, D_IN // N_DEV), 2, 0),
        'w_out': jnp.moveaxis(part['w_out'].reshape(DEPTH, N_DEV, D_MODEL // N_DEV, D_MODEL), 1, 0),
        'w_up': jnp.moveaxis(part['w_up'].reshape(DEPTH, D_MODEL, N_DEV, D_FF // N_DEV), 2, 0),
        'w_down': jnp.moveaxis(part['w_down'].reshape(DEPTH, N_DEV, D_FF // N_DEV, D_MODEL), 1, 0),
    }
    grads, deltas, new_m, new_v = {}, {}, {}, {}
    for n in _BIG:
        recv = _all_to_all(send[n], "a2a_" + n)
        shp = w[n].shape
        r2 = lambda a: a.reshape(-1, shp[-1])
        g, d, nm, nv = _sum_adamw(recv.reshape(N_DEV, -1, shp[-1]), r2(w[n]), r2(m[n]), r2(v[n]), 256, "adamw_" + n)
        grads[n], deltas[n], new_m[n], new_v[n] = (a.reshape(shp) for a in (g, d, nm, nv))

    small = [n for n in _WEIGHTS if n not in _BIG]
    packed = _pack([part[n] for n in small] + [loss])
    total = _sum_parts(_all_gather(packed, "ag_small"), "sum_small")
    summed = _unpack(total, [part[n].shape for n in small] + [(1, 1)])
    loss_total = summed[-1].reshape(())
    gsmall = dict(zip(small, summed[:-1]))
    for n in _SHARDED_SMALL:
        width = w[n].shape[-1]
        gsmall[n] = lax.dynamic_slice_in_dim(gsmall[n], me * width, width, axis=2)
    pk = lambda d: _pack([d[n] for n in small])
    _, d, nm, nv = _sum_adamw(pk(gsmall)[None], pk(w), pk(m), pk(v), 4096, "adamw_small")
    shapes = [w[n].shape for n in small]
    for n, dd, mm, vv in zip(small, _unpack(d, shapes), _unpack(nm, shapes), _unpack(nv, shapes)):
        grads[n], deltas[n], new_m[n], new_v[n] = gsmall[n], dd, mm, vv
    return loss_total, grad_x, grads, deltas, new_m, new_v


def kernel(x, norm1_g, w_in, hgrn_lb_logits, hgrn_norm_g, gdn_conv_w, gdn_a_log, gdn_dt_bias, gdn_norm_g, lru_conv_w, lru_conv_b, lru_w_a, lru_b_a, lru_w_x, lru_b_x, lru_lambda, w_out, norm2_g, w_up, w_down, final_norm_g, loss_target, m_norm1_g, m_w_in, m_hgrn_lb_logits, m_hgrn_norm_g, m_gdn_conv_w, m_gdn_a_log, m_gdn_dt_bias, m_gdn_norm_g, m_lru_conv_w, m_lru_conv_b, m_lru_w_a, m_lru_b_a, m_lru_w_x, m_lru_b_x, m_lru_lambda, m_w_out, m_norm2_g, m_w_up, m_w_down, m_final_norm_g, v_norm1_g, v_w_in, v_hgrn_lb_logits, v_hgrn_norm_g, v_gdn_conv_w, v_gdn_a_log, v_gdn_dt_bias, v_gdn_norm_g, v_lru_conv_w, v_lru_conv_b, v_lru_w_a, v_lru_b_a, v_lru_w_x, v_lru_b_x, v_lru_lambda, v_w_out, v_norm2_g, v_w_up, v_w_down, v_final_norm_g):
    w = dict(zip(_WEIGHTS, (norm1_g, w_in, hgrn_lb_logits, hgrn_norm_g, gdn_conv_w, gdn_a_log, gdn_dt_bias, gdn_norm_g, lru_conv_w, lru_conv_b, lru_w_a, lru_b_a, lru_w_x, lru_b_x, lru_lambda, w_out, norm2_g, w_up, w_down, final_norm_g)))
    m = dict(zip(_WEIGHTS, (m_norm1_g, m_w_in, m_hgrn_lb_logits, m_hgrn_norm_g, m_gdn_conv_w, m_gdn_a_log, m_gdn_dt_bias, m_gdn_norm_g, m_lru_conv_w, m_lru_conv_b, m_lru_w_a, m_lru_b_a, m_lru_w_x, m_lru_b_x, m_lru_lambda, m_w_out, m_norm2_g, m_w_up, m_w_down, m_final_norm_g)))
    v = dict(zip(_WEIGHTS, (v_norm1_g, v_w_in, v_hgrn_lb_logits, v_hgrn_norm_g, v_gdn_conv_w, v_gdn_a_log, v_gdn_dt_bias, v_gdn_norm_g, v_lru_conv_w, v_lru_conv_b, v_lru_w_a, v_lru_b_a, v_lru_w_x, v_lru_b_x, v_lru_lambda, v_w_out, v_norm2_g, v_w_up, v_w_down, v_final_norm_g)))
    loss, grad_x, grads, deltas, new_m, new_v = _step(x[0], loss_target[0], w, m, v)
    return (loss, grad_x[None], *[grads[n] for n in _WEIGHTS], *[deltas[n] for n in _WEIGHTS],
            *[new_m[n] for n in _WEIGHTS], *[new_v[n] for n in _WEIGHTS])
```

```python
import functools

import jax
import jax.numpy as jnp
from jax import lax
from jax.experimental import pallas as pl
from jax.experimental.pallas import tpu as pltpu

F32 = jnp.float32
BF16 = jnp.bfloat16
HI = lax.Precision.HIGHEST
MESH = pl.DeviceIdType.MESH

N_DEV = 8
D_MODEL = 1024
DEPTH = 4
A_HEADS, A_DIM, A_WIDTH = 4, 64, 256
B_HEADS, B_DIM, B_WIDTH = 4, 128, 512
C_WIDTH, C_BLOCKS, C_BLOCK_DIM = 256, 4, 64
D_IN = 3592
D_IN_PAD = 3840
COL_A, COL_B, COL_C, COL_G = 0, 1024, 3072, 3584
D_FF = 4096
CONV_K = 4
CHUNK = 64
SUB = 16
RG_C = 8.0
EPS = 1e-6
TINY = 1e-30
EXP_CLAMP = 80.0
GDN_SCALE = B_DIM ** -0.5
ADAM_LR, ADAM_B1, ADAM_B2, ADAM_EPS, ADAM_WD, ADAM_STEP = 0.001, 0.9, 0.999, 1e-08, 0.01, 10
VMEM_LIMIT = 56 * 1024 * 1024


def _cparams(sem=None):
    return pltpu.CompilerParams(dimension_semantics=sem, vmem_limit_bytes=VMEM_LIMIT)


_DIMS = {"nn": (((1,), (0,)), ((), ())), "nt": (((1,), (1,)), ((), ())), "tn": (((0,), (0,)), ((), ()))}


def _split_bf16(x):
    hi = x.astype(BF16)
    return hi, (x - hi.astype(F32)).astype(BF16)


def _dot(a, b, mode="nn", hi=False):
    if not hi:
        return lax.dot_general(a.astype(BF16), b.astype(BF16), _DIMS[mode], preferred_element_type=F32)
    ah, al = _split_bf16(a.astype(F32))
    bh, bl = _split_bf16(b.astype(F32))
    ka = 0 if mode == "tn" else 1
    kb = 1 if mode == "nt" else 0
    return lax.dot_general(jnp.concatenate([ah, ah, al], axis=ka), jnp.concatenate([bh, bl, bh], axis=kb),
                           _DIMS[mode], preferred_element_type=F32)


def _dot_exact_lhs(lhs, x, mode="nn"):
    l_bf16 = lhs.astype(BF16)
    x1 = x.astype(BF16)
    r1 = x - x1.astype(F32)
    x2 = r1.astype(BF16)
    x3 = (r1 - x2.astype(F32)).astype(BF16)
    ka = 0 if mode == "tn" else 1
    return lax.dot_general(jnp.concatenate([l_bf16] * 3, axis=ka), jnp.concatenate([x1, x2, x3], axis=0),
                           _DIMS[mode], preferred_element_type=F32)


def _iota2(n, m):
    return lax.broadcasted_iota(jnp.int32, (n, m), 0), lax.broadcasted_iota(jnp.int32, (n, m), 1)


def _tril(n, strict=False):
    r, c = _iota2(n, n)
    return (r > c) if strict else (r >= c)


def _sigmoid(x):
    return 1.0 / (1.0 + jnp.exp(-x))


def _softplus(x):
    return jnp.maximum(x, 0.0) + jnp.log(1.0 + jnp.exp(-jnp.abs(x)))


def _neg_expm1(z):
    series = -z * (1.0 + z * (0.5 + z * (1.0 / 6.0)))
    return jnp.where(z > -1e-2, series, 1.0 - jnp.exp(z))


def _gelu_tanh(x):
    c = 0.7978845608028654
    u = c * (x + 0.044715 * x * x * x)
    t = jnp.tanh(u)
    g = 0.5 * x * (1.0 + t)
    dg = 0.5 * (1.0 + t) + 0.5 * x * (1.0 - t * t) * c * (1.0 + 3.0 * 0.044715 * x * x)
    return g, dg


def _shift_rows(cur, halo, s, down=True):
    n = cur.shape[0]
    ridx = lax.broadcasted_iota(jnp.int32, (8, cur.shape[1]), 0)
    if down:
        main = pltpu.roll(cur, s, 0)
        fix = jnp.where(ridx < s, pltpu.roll(halo, s, 0), main[0:8])
        return jnp.concatenate([fix, main[8:]], axis=0)
    main = pltpu.roll(cur, n - s, 0)
    fix = jnp.where(ridx >= 8 - s, pltpu.roll(halo, 8 - s, 0), main[n - 8:n])
    return jnp.concatenate([main[:n - 8], fix], axis=0)


def _conv_fwd(cur, prev8, w):
    y = cur * w[3:4]
    for j in range(3):
        y = y + _shift_rows(cur, prev8, 3 - j, down=True) * w[j:j + 1]
    return y


def _pick_tile(n, pref):
    best = None
    for cand in range(128, min(n, pref) + 1, 128):
        if n % cand == 0:
            best = cand
    return best if best is not None else n


def _matmul(a, b, mode, out_dtype=F32, residual=None, tm=512, tn=1024, tk=1024, name="matmul"):
    if mode == "nn":
        (m, k), n = a.shape, b.shape[1]
    elif mode == "nt":
        (m, k), n = a.shape, b.shape[0]
    else:
        (k, m), n = a.shape, b.shape[1]
    tm, tn, tk = _pick_tile(m, tm), _pick_tile(n, tn), _pick_tile(k, tk)
    nk = k // tk
    a_spec = pl.BlockSpec((tk, tm), lambda i, j, kk: (kk, i)) if mode == "tn" else pl.BlockSpec((tm, tk), lambda i, j, kk: (i, kk))
    b_spec = pl.BlockSpec((tn, tk), lambda i, j, kk: (j, kk)) if mode == "nt" else pl.BlockSpec((tk, tn), lambda i, j, kk: (kk, j))
    o_spec = pl.BlockSpec((tm, tn), lambda i, j, kk: (i, j))
    has_res = residual is not None

    def body(*refs):
        if has_res:
            a_ref, b_ref, r_ref, o_ref, acc = refs
        else:
            a_ref, b_ref, o_ref, acc = refs
        kk = pl.program_id(2)

        @pl.when(kk == 0)
        def _():
            acc[...] = jnp.zeros_like(acc)

        acc[...] += _dot(a_ref[...], b_ref[...], mode)

        @pl.when(kk == nk - 1)
        def _():
            r = acc[...]
            if has_res:
                r = r + r_ref[...]
            o_ref[...] = r.astype(out_dtype)

    ins = [a, b] + ([residual] if has_res else [])
    specs = [a_spec, b_spec] + ([o_spec] if has_res else [])
    return pl.pallas_call(
        body, grid=(m // tm, n // tn, nk), in_specs=specs, out_specs=o_spec,
        out_shape=jax.ShapeDtypeStruct((m, n), out_dtype), scratch_shapes=[pltpu.VMEM((tm, tn), F32)],
        compiler_params=_cparams(("parallel", "parallel", "arbitrary")), name=name)(*ins)


def _rms_fwd(x, g, tb=512, name="rms_fwd"):
    t, d = x.shape

    def body(x_ref, g_ref, h_ref):
        xv = x_ref[...]
        rinv = lax.rsqrt(jnp.mean(xv * xv, axis=-1, keepdims=True) + EPS)
        h_ref[...] = (xv * rinv * g_ref[...]).astype(BF16)

    return pl.pallas_call(
        body, grid=(t // tb,), in_specs=[pl.BlockSpec((tb, d), lambda i: (i, 0)), pl.BlockSpec((1, d), lambda i: (0, 0))],
        out_specs=pl.BlockSpec((tb, d), lambda i: (i, 0)), out_shape=jax.ShapeDtypeStruct((t, d), BF16),
        compiler_params=_cparams(("parallel",)), name=name)(x, g.reshape(1, d))


def _rms_bwd(dh, x, g, dres, tb=512, name="rms_bwd"):
    t, d = x.shape

    def body(dh_ref, x_ref, g_ref, dres_ref, dx_ref, dg_ref):
        @pl.when(pl.program_id(0) == 0)
        def _():
            dg_ref[...] = jnp.zeros_like(dg_ref)

        xv = x_ref[...]
        dhv = dh_ref[...].astype(F32)
        rinv = lax.rsqrt(jnp.mean(xv * xv, axis=-1, keepdims=True) + EPS)
        xhat = xv * rinv
        dxh = dhv * g_ref[...]
        dx_ref[...] = dres_ref[...] + rinv * (dxh - xhat * jnp.mean(dxh * xhat, axis=-1, keepdims=True))
        dg_ref[...] += jnp.sum(dhv * xhat, axis=0, keepdims=True)

    row = pl.BlockSpec((tb, d), lambda i: (i, 0))
    vec = pl.BlockSpec((1, d), lambda i: (0, 0))
    return pl.pallas_call(
        body, grid=(t // tb,), in_specs=[row, row, vec, row], out_specs=[row, vec],
        out_shape=[jax.ShapeDtypeStruct((t, d), F32), jax.ShapeDtypeStruct((1, d), F32)],
        compiler_params=_cparams(("arbitrary",)), name=name)(dh, x, g.reshape(1, d), dres)


def _loss_head(x, g, target, tb=512):
    t, d = x.shape

    def body(x_ref, g_ref, t_ref, loss_ref, dx_ref, dg_ref):
        @pl.when(pl.program_id(0) == 0)
        def _():
            dg_ref[...] = jnp.zeros_like(dg_ref)
            loss_ref[...] = jnp.zeros_like(loss_ref)

        xv = x_ref[...]
        rinv = lax.rsqrt(jnp.mean(xv * xv, axis=-1, keepdims=True) + EPS)
        xhat = xv * rinv
        err = xhat * g_ref[...] - t_ref[...]
        loss_ref[...] += 0.5 * jnp.sum(jnp.mean(err * err, axis=-1, keepdims=True), axis=0, keepdims=True)
        dy = err * (1.0 / d)
        dxh = dy * g_ref[...]
        dx_ref[...] = rinv * (dxh - xhat * jnp.mean(dxh * xhat, axis=-1, keepdims=True))
        dg_ref[...] += jnp.sum(dy * xhat, axis=0, keepdims=True)

    row = pl.BlockSpec((tb, d), lambda i: (i, 0))
    vec = pl.BlockSpec((1, d), lambda i: (0, 0))
    one = pl.BlockSpec((1, 1), lambda i: (0, 0))
    return pl.pallas_call(
        body, grid=(t // tb,), in_specs=[row, vec, row], out_specs=[one, row, vec],
        out_shape=[jax.ShapeDtypeStruct((1, 1), F32), jax.ShapeDtypeStruct((t, d), F32), jax.ShapeDtypeStruct((1, d), F32)],
        compiler_params=_cparams(("arbitrary",)), name="loss_head")(x, g.reshape(1, d), target)


def _relu2_fwd(up, tb=512):
    t, d = up.shape

    def body(u_ref, a_ref):
        r = jnp.maximum(u_ref[...], 0.0)
        a_ref[...] = (r * r).astype(BF16)

    row = pl.BlockSpec((tb, d), lambda i: (i, 0))
    return pl.pallas_call(body, grid=(t // tb,), in_specs=[row], out_specs=row, out_shape=jax.ShapeDtypeStruct((t, d), BF16),
                          compiler_params=_cparams(("parallel",)), name="relu2_fwd")(up)


def _relu2_bwd(dact, up, tb=512):
    t, d = up.shape

    def body(da_ref, u_ref, o_ref):
        o_ref[...] = (da_ref[...] * 2.0 * jnp.maximum(u_ref[...], 0.0)).astype(BF16)

    row = pl.BlockSpec((tb, d), lambda i: (i, 0))
    return pl.pallas_call(body, grid=(t // tb,), in_specs=[row, row], out_specs=row, out_shape=jax.ShapeDtypeStruct((t, d), BF16),
                          compiler_params=_cparams(("parallel",)), name="relu2_bwd")(dact, up)


def _lb_fwd(logits):
    def body(l_ref, o_ref):
        lg = l_ref[...]
        e = jnp.exp(lg - jnp.max(lg, axis=0, keepdims=True))
        p = e / jnp.sum(e, axis=0, keepdims=True)
        c = jnp.zeros_like(p[0:1])
        rows = [c]
        for l in range(1, DEPTH):
            c = c + p[l:l + 1]
            rows.append(c)
        o_ref[...] = jnp.minimum(jnp.maximum(jnp.concatenate(rows, axis=0), 0.0), 1.0 - EPS)

    return pl.pallas_call(body, out_shape=jax.ShapeDtypeStruct(logits.shape, F32), name="lb_fwd")(logits)


def _lb_bwd(logits, dlb):
    def body(l_ref, d_ref, o_ref):
        lg = l_ref[...]
        e = jnp.exp(lg - jnp.max(lg, axis=0, keepdims=True))
        p = e / jnp.sum(e, axis=0, keepdims=True)
        hi = 1.0 - EPS
        c = jnp.zeros_like(p[0:1])
        dc = []
        for l in range(1, DEPTH):
            c = c + p[l:l + 1]
            gl = jnp.where(c < 0.0, 0.0, jnp.where(c == 0.0, 0.5, 1.0)) * jnp.where(c > hi, 0.0, jnp.where(c == hi, 0.5, 1.0))
            dc.append(d_ref[l:l + 1, :] * gl)
        dp = [jnp.zeros_like(c)]
        for j in range(1, DEPTH):
            s = dc[j - 1]
            for l in range(j + 1, DEPTH):
                s = s + dc[l - 1]
            dp.append(s)
        dpm = jnp.concatenate(dp, axis=0)
        o_ref[...] = p * (dpm - jnp.sum(p * dpm, axis=0, keepdims=True))

    return pl.pallas_call(body, out_shape=jax.ShapeDtypeStruct(logits.shape, F32), name="lb_bwd")(logits, dlb)


def _a_gates(qi, fi, lbh):
    sq = _sigmoid(qi)
    q = qi * sq
    sg = _sigmoid(fi)
    sgn = _sigmoid(-fi)
    f = lbh + (1.0 - lbh) * sg
    logf = jnp.log(jnp.maximum(f, TINY))
    k = (1.0 - lbh) * sgn
    return q, sq, sg, sgn, f, logf, k


def _a_intra(q, k, cum):
    qts, kes, rows = [], [], []
    for i in range(CHUNK // SUB):
        lo = i * SUB
        r = cum[lo - 1:lo] if i > 0 else jnp.zeros_like(cum[0:1])
        eq = jnp.exp(cum[lo:lo + SUB] - r)
        ek = jnp.exp(jnp.minimum(r - cum, EXP_CLAMP))
        qt = q[lo:lo + SUB] * eq
        rows.append(_dot(qt, k * ek, "nt", hi=True))
        qts.append((qt, eq))
        kes.append(ek)
    attn = jnp.where(_tril(CHUNK), jnp.concatenate(rows, axis=0), 0.0)
    return attn, qts, kes


def _a_intra_bwd(dattn, k, qts, kes):
    dq_rows = []
    dk = jnp.zeros_like(k)
    for i in range(CHUNK // SUB):
        lo = i * SUB
        da = dattn[lo:lo + SUB]
        qt, eq = qts[i]
        dq_rows.append(_dot(da, k * kes[i], "nn", hi=True) * eq)
        dk = dk + _dot(da, qt, "tn", hi=True) * kes[i]
    return jnp.concatenate(dq_rows, axis=0), dk


def _headnorm_fwd(o, g, gate_in):
    rinv = lax.rsqrt(jnp.mean(o * o, axis=-1, keepdims=True) + EPS)
    sg = _sigmoid(gate_in)
    return o * rinv * g * (gate_in * sg)


def _headnorm_bwd(dout, o, g, gate_in):
    rinv = lax.rsqrt(jnp.mean(o * o, axis=-1, keepdims=True) + EPS)
    xhat = o * rinv
    sg = _sigmoid(gate_in)
    silu = gate_in * sg
    dy = dout * silu
    dgate = dout * xhat * g * (sg * (1.0 + gate_in * (1.0 - sg)))
    dxh = dy * g
    do = rinv * (dxh - xhat * jnp.mean(dxh * xhat, axis=-1, keepdims=True))
    return do, dgate, jnp.sum(dy * xhat, axis=0, keepdims=True)


def _a_fwd(proj, lb, norm_g, tb=256):
    t = proj.shape[0]
    nch = tb // CHUNK

    def body(q_ref, f_ref, i_ref, g_ref, lb_ref, ng_ref, out_ref, st_ref, s_scr):
        @pl.when(pl.program_id(0) == 0)
        def _():
            s_scr[...] = jnp.zeros_like(s_scr)

        ltri = _tril(CHUNK).astype(F32)

        def chunk(c, carry):
            rows = pl.ds(pl.multiple_of(c * CHUNK, CHUNK), CHUNK)
            hs = range(A_HEADS)
            cols = [slice(h * A_DIM, (h + 1) * A_DIM) for h in hs]
            gates = [_a_gates(q_ref[rows, cols[h]], f_ref[rows, cols[h]], lb_ref[:, cols[h]]) for h in hs]
            q, k = [gates[h][0] for h in hs], [gates[h][6] for h in hs]
            v = [i_ref[rows, cols[h]] for h in hs]
            cum = [_dot_exact_lhs(ltri, gates[h][5]) for h in hs]
            cl = [cum[h][CHUNK - 1:CHUNK] for h in hs]
            s0 = [s_scr[h] for h in hs]
            for h in hs:
                st_ref[c, h] = s0[h]
            attn = [_a_intra(q[h], k[h], cum[h])[0] for h in hs]
            qs0 = [_dot(q[h] * jnp.exp(cum[h]), s0[h]) for h in hs]
            o = [qs0[h] + _dot(attn[h], v[h]) for h in hs]
            kd = [k[h] * jnp.exp(cl[h] - cum[h]) for h in hs]
            for h in hs:
                s_scr[h] = s0[h] * jnp.exp(cl[h]).T + _dot(kd[h], v[h], "tn")
            outs = [_headnorm_fwd(o[h], ng_ref[...], g_ref[rows, cols[h]]) for h in hs]
            out_ref[rows, :] = jnp.concatenate(outs, axis=1).astype(BF16)
            return carry

        lax.fori_loop(0, nch, chunk, 0)

    colblk = lambda j: pl.BlockSpec((tb, A_WIDTH), lambda i, j=j: (i, j))
    return pl.pallas_call(
        body, grid=(t // tb,),
        in_specs=[colblk(0), colblk(1), colblk(2), colblk(3), pl.BlockSpec((1, A_WIDTH), lambda i: (0, 0)),
                  pl.BlockSpec((1, A_DIM), lambda i: (0, 0))],
        out_specs=[pl.BlockSpec((tb, A_WIDTH), lambda i: (i, 0)),
                   pl.BlockSpec((nch, A_HEADS, A_DIM, A_DIM), lambda i: (i, 0, 0, 0))],
        out_shape=[jax.ShapeDtypeStruct((t, A_WIDTH), BF16), jax.ShapeDtypeStruct((t // CHUNK, A_HEADS, A_DIM, A_DIM), F32)],
        scratch_shapes=[pltpu.VMEM((A_HEADS, A_DIM, A_DIM), F32)],
        compiler_params=_cparams(("arbitrary",)), name="hgrn_fwd")(proj, proj, proj, proj, lb.reshape(1, A_WIDTH), norm_g.reshape(1, A_DIM))


def _a_bwd(proj, lb, norm_g, states, dmixed, tb=256):
    t = proj.shape[0]
    nch = tb // CHUNK
    nb = t // tb

    def body(q_ref, f_ref, i_ref, g_ref, lb_ref, ng_ref, st_ref, dm_ref, dp_ref, dlb_ref, dng_ref, ds_scr):
        @pl.when(pl.program_id(0) == 0)
        def _():
            ds_scr[...] = jnp.zeros_like(ds_scr)
            dlb_ref[...] = jnp.zeros_like(dlb_ref)
            dng_ref[...] = jnp.zeros_like(dng_ref)

        ltri = _tril(CHUNK).astype(F32)
        mask = _tril(CHUNK)

        def chunk(cc, carry):
            c = nch - 1 - cc
            rows = pl.ds(pl.multiple_of(c * CHUNK, CHUNK), CHUNK)
            hs = range(A_HEADS)
            cols = [slice(h * A_DIM, (h + 1) * A_DIM) for h in hs]
            qi = [q_ref[rows, cols[h]] for h in hs]
            gi = [g_ref[rows, cols[h]] for h in hs]
            lbh = [lb_ref[:, cols[h]] for h in hs]
            gates = [_a_gates(qi[h], f_ref[rows, cols[h]], lbh[h]) for h in hs]
            q, sq, sg, sgn, f, logf, k = ([gates[h][j] for h in hs] for j in range(7))
            v = [i_ref[rows, cols[h]] for h in hs]
            cum = [_dot_exact_lhs(ltri, logf[h]) for h in hs]
            cl = [cum[h][CHUNK - 1:CHUNK] for h in hs]
            ecum = [jnp.exp(cum[h]) for h in hs]
            ekd = [jnp.exp(cl[h] - cum[h]) for h in hs]
            cd = [jnp.exp(cl[h]) for h in hs]
            qd = [q[h] * ecum[h] for h in hs]
            kd = [k[h] * ekd[h] for h in hs]
            s0 = [st_ref[c, h] for h in hs]
            ds = [ds_scr[h] for h in hs]
            intra = [_a_intra(q[h], k[h], cum[h]) for h in hs]
            attn = [intra[h][0] for h in hs]
            qs0 = [_dot(qd[h], s0[h]) for h in hs]
            o = [qs0[h] + _dot(attn[h], v[h]) for h in hs]
            hn = [_headnorm_bwd(dm_ref[rows, cols[h]].astype(F32), o[h], ng_ref[...], gi[h]) for h in hs]
            do = [hn[h][0] for h in hs]
            dqd = [_dot(do[h], s0[h], "nt") for h in hs]
            dattn = [jnp.where(mask, _dot(do[h], v[h], "nt"), 0.0) for h in hs]
            dv = [_dot(attn[h], do[h], "tn") + _dot(kd[h], ds[h]) for h in hs]
            dkd = [_dot(v[h], ds[h], "nt") for h in hs]
            dcd = [jnp.sum((s0[h] * ds[h]).T, axis=0, keepdims=True) for h in hs]
            for h in hs:
                ds_scr[h] = _dot(qd[h], do[h], "tn") + ds[h] * cd[h].T
            ib = [_a_intra_bwd(dattn[h], k[h], intra[h][1], intra[h][2]) for h in hs]
            dq = [dqd[h] * ecum[h] + ib[h][0] for h in hs]
            dk = [dkd[h] * ekd[h] + ib[h][1] for h in hs]
            dkk = [dkd[h] * kd[h] for h in hs]
            dcum = [dqd[h] * qd[h] - dkk[h] + q[h] * ib[h][0] - k[h] * ib[h][1] for h in hs]
            dcl = [jnp.sum(dkk[h], axis=0, keepdims=True) + dcd[h] * cd[h] for h in hs]
            dlogf = [_dot_exact_lhs(ltri, dcum[h], "tn") + dcl[h] for h in hs]
            dfv = [jnp.where(f[h] > TINY, dlogf[h] / f[h], 0.0) for h in hs]
            dfi = [dfv[h] * (1.0 - lbh[h]) * sg[h] * (1.0 - sg[h]) - dk[h] * (1.0 - lbh[h]) * sgn[h] * (1.0 - sgn[h]) for h in hs]
            dlbs = [jnp.sum(dfv[h] * (1.0 - sg[h]) - dk[h] * sgn[h], axis=0, keepdims=True) for h in hs]
            dqs = [dq[h] * (sq[h] * (1.0 + qi[h] * (1.0 - sq[h]))) for h in hs]
            dp_ref[rows, :] = jnp.concatenate(dqs + dfi + dv + [hn[h][1] for h in hs], axis=1).astype(BF16)
            dlb_ref[...] += jnp.concatenate(dlbs, axis=1)
            dng_ref[...] += sum(hn[h][2] for h in hs)
            return carry

        lax.fori_loop(0, nch, chunk, 0)

    colblk = lambda j: pl.BlockSpec((tb, A_WIDTH), lambda i, j=j: (nb - 1 - i, j))
    vec = lambda n: pl.BlockSpec((1, n), lambda i: (0, 0))
    outs = pl.pallas_call(
        body, grid=(nb,),
        in_specs=[colblk(0), colblk(1), colblk(2), colblk(3), vec(A_WIDTH), vec(A_DIM),
                  pl.BlockSpec((nch, A_HEADS, A_DIM, A_DIM), lambda i: (nb - 1 - i, 0, 0, 0)), colblk(0)],
        out_specs=[pl.BlockSpec((tb, 4 * A_WIDTH), lambda i: (nb - 1 - i, 0)), vec(A_WIDTH), vec(A_DIM)],
        out_shape=[jax.ShapeDtypeStruct((t, 4 * A_WIDTH), BF16), jax.ShapeDtypeStruct((1, A_WIDTH), F32), jax.ShapeDtypeStruct((1, A_DIM), F32)],
        scratch_shapes=[pltpu.VMEM((A_HEADS, A_DIM, A_DIM), F32)],
        compiler_params=_cparams(("arbitrary",)), name="hgrn_bwd")(
            proj, proj, proj, proj, lb.reshape(1, A_WIDTH), norm_g.reshape(1, A_DIM), states, dmixed)
    return outs


def _gate_lane_masks(shape):
    lane = lax.broadcasted_iota(jnp.int32, shape, 1)
    return lane < B_HEADS, (lane >= B_HEADS) & (lane < 2 * B_HEADS)


def _b_pre_fwd(proj, conv_w, alog_row, dtb_row, tb=512):
    t = proj.shape[0]
    cb0 = COL_B // B_WIDTH

    def body(q_ref, k_ref, v_ref, qp_ref, kp_ref, vp_ref, w_ref, gi_ref, al_ref, dt_ref, qkv_ref, gates_ref):
        first = pl.program_id(0) == 0
        for part, (c_ref, p_ref) in enumerate(((q_ref, qp_ref), (k_ref, kp_ref), (v_ref, vp_ref))):
            cols = slice(part * B_WIDTH, (part + 1) * B_WIDTH)
            prev = jnp.where(first, 0.0, p_ref[...])
            y = _conv_fwd(c_ref[...], prev, w_ref[:, cols])
            s = y * _sigmoid(y)
            if part < 2:
                outs = []
                for h in range(B_HEADS):
                    sh = s[:, h * B_DIM:(h + 1) * B_DIM]
                    outs.append(sh * lax.rsqrt(jnp.sum(sh * sh, axis=-1, keepdims=True) + EPS))
                s = jnp.concatenate(outs, axis=1)
            qkv_ref[:, cols] = s
        g = gi_ref[...]
        is_b, is_a = _gate_lane_masks(g.shape)
        la = -jnp.exp(al_ref[...]) * _softplus(g + dt_ref[...])
        gates_ref[...] = jnp.where(is_b, _sigmoid(g), jnp.where(is_a, la, 0.0))

    cur = lambda j: pl.BlockSpec((tb, B_WIDTH), lambda i, j=j: (i, cb0 + j))
    prv = lambda j: pl.BlockSpec((8, B_WIDTH), lambda i, j=j: (jnp.maximum(i * (tb // 8) - 1, 0), cb0 + j))
    vec = pl.BlockSpec((1, 128), lambda i: (0, 0))
    return pl.pallas_call(
        body, grid=(t // tb,),
        in_specs=[cur(0), cur(1), cur(2), prv(0), prv(1), prv(2), pl.BlockSpec((CONV_K, 3 * B_WIDTH), lambda i: (0, 0)),
                  pl.BlockSpec((tb, 128), lambda i: (i, COL_G // 128)), vec, vec],
        out_specs=[pl.BlockSpec((tb, 3 * B_WIDTH), lambda i: (i, 0)), pl.BlockSpec((tb, 128), lambda i: (i, 0))],
        out_shape=[jax.ShapeDtypeStruct((t, 3 * B_WIDTH), F32), jax.ShapeDtypeStruct((t, 128), F32)],
        compiler_params=_cparams(("parallel",)), name="gdn_pre_fwd")(proj, proj, proj, proj, proj, proj, conv_w, proj, alog_row, dtb_row)


def _inv_unit_lower(amats):
    r, c = _iota2(CHUNK, CHUNK)
    eye = jnp.where(r == c, 1.0, 0.0)
    ps = [eye - a for a in amats]
    aks = amats
    for _ in range(5):
        aks = [_dot(ak, ak, hi=True) for ak in aks]
        ps = [p + _dot(p, ak, hi=True) for p, ak in zip(ps, aks)]
    return ps


def _b_chunk(qs, ks, vs, betas, gcs, grows, gls, s0s):
    hs = range(len(qs))
    causal, strict = _tril(CHUNK), _tril(CHUNK, strict=True)
    decay = [jnp.where(causal, jnp.exp(jnp.minimum(gcs[h] - grows[h], 0.0)), 0.0) for h in hs]
    kb = [ks[h] * betas[h] for h in hs]
    kk = [_dot(kb[h], ks[h], "nt") for h in hs]
    qkr = [_dot(qs[h], ks[h], "nt") for h in hs]
    tinv = _inv_unit_lower([jnp.where(strict, kk[h] * decay[h], 0.0) for h in hs])
    eg = [jnp.exp(gcs[h]) for h in hs]
    bv = [vs[h] * betas[h] for h in hs]
    kg = [kb[h] * eg[h] for h in hs]
    u = [_dot(tinv[h], bv[h], hi=True) for h in hs]
    w = [_dot(tinv[h], kg[h], hi=True) for h in hs]
    qk = [qkr[h] * decay[h] for h in hs]
    qd = [qs[h] * eg[h] for h in hs]
    ekd = [jnp.exp(gls[h] - gcs[h]) for h in hs]
    kd = [ks[h] * ekd[h] for h in hs]
    cd = [jnp.exp(gls[h]) for h in hs]
    ws = [_dot(w[h], s0s[h]) for h in hs]
    qs0 = [_dot(qd[h], s0s[h]) for h in hs]
    vn = [u[h] - ws[h] for h in hs]
    o = [qs0[h] + _dot(qk[h], vn[h]) for h in hs]
    s1 = [s0s[h] * cd[h] + _dot(kd[h], vn[h], "tn") for h in hs]
    return dict(decay=decay, kb=kb, kk=kk, tinv=tinv, eg=eg, bv=bv, kg=kg, w=w, qkr=qkr, qk=qk, qd=qd, ekd=ekd,
                kd=kd, cd=cd, vn=vn, o=o, s1=s1)


def _b_fwd(qkv, gates, proj, norm_g, tb=256):
    t = qkv.shape[0]
    nch = tb // CHUNK

    def body(q_ref, k_ref, v_ref, ga_ref, z_ref, ng_ref, out_ref, st_ref, s_scr):
        @pl.when(pl.program_id(0) == 0)
        def _():
            s_scr[...] = jnp.zeros_like(s_scr)

        ltri = _tril(CHUNK).astype(F32)

        def chunk(c, carry):
            rows = pl.ds(pl.multiple_of(c * CHUNK, CHUNK), CHUNK)
            ga = ga_ref[rows, :]
            gcum = _dot_exact_lhs(ltri, ga)
            gcum_t = gcum.T
            hs = range(B_HEADS)
            cols = [slice(h * B_DIM, (h + 1) * B_DIM) for h in hs]
            s0s = [s_scr[h] for h in hs]
            for h in hs:
                st_ref[c, h] = s0s[h]
            r = _b_chunk([q_ref[rows, cols[h]] * GDN_SCALE for h in hs], [k_ref[rows, cols[h]] for h in hs],
                         [v_ref[rows, cols[h]] for h in hs], [ga[:, h:h + 1] for h in hs],
                         [gcum[:, B_HEADS + h:B_HEADS + h + 1] for h in hs], [gcum_t[B_HEADS + h:B_HEADS + h + 1, :] for h in hs],
                         [gcum[CHUNK - 1:CHUNK, B_HEADS + h:B_HEADS + h + 1] for h in hs], s0s)
            for h in hs:
                s_scr[h] = r["s1"][h]
            outs = [_headnorm_fwd(r["o"][h], ng_ref[...], z_ref[rows, cols[h]]) for h in hs]
            out_ref[rows, :] = jnp.concatenate(outs, axis=1).astype(BF16)
            return carry

        lax.fori_loop(0, nch, chunk, 0)

    part = lambda j: pl.BlockSpec((tb, B_WIDTH), lambda i, j=j: (i, j))
    return pl.pallas_call(
        body, grid=(t // tb,),
        in_specs=[part(0), part(1), part(2), pl.BlockSpec((tb, 128), lambda i: (i, 0)),
                  pl.BlockSpec((tb, B_WIDTH), lambda i: (i, COL_B // B_WIDTH + 3)), pl.BlockSpec((1, B_DIM), lambda i: (0, 0))],
        out_specs=[pl.BlockSpec((tb, B_WIDTH), lambda i: (i, 0)),
                   pl.BlockSpec((nch, B_HEADS, B_DIM, B_DIM), lambda i: (i, 0, 0, 0))],
        out_shape=[jax.ShapeDtypeStruct((t, B_WIDTH), BF16), jax.ShapeDtypeStruct((t // CHUNK, B_HEADS, B_DIM, B_DIM), F32)],
        scratch_shapes=[pltpu.VMEM((B_HEADS, B_DIM, B_DIM), F32)],
        compiler_params=_cparams(("arbitrary",)), name="gdn_fwd")(qkv, qkv, qkv, gates, proj, norm_g.reshape(1, B_DIM))


def _b_bwd(qkv, gates, proj, norm_g, states, dmixed, tb=256):
    t = qkv.shape[0]
    nch = tb // CHUNK
    nb = t // tb

    def body(q_ref, k_ref, v_ref, ga_ref, z_ref, ng_ref, st_ref, dm0_ref, dm1_ref, dqkv_ref, dga_ref, dz_ref, dng_ref, ds_scr):
        @pl.when(pl.program_id(0) == 0)
        def _():
            ds_scr[...] = jnp.zeros_like(ds_scr)
            dng_ref[...] = jnp.zeros_like(dng_ref)

        ltri = _tril(CHUNK).astype(F32)
        strict = _tril(CHUNK, strict=True)
        lane = lax.broadcasted_iota(jnp.int32, (CHUNK, 128), 1)
        lane1 = lax.broadcasted_iota(jnp.int32, (1, 128), 1)

        def chunk(cc, carry):
            c = nch - 1 - cc
            rows = pl.ds(pl.multiple_of(c * CHUNK, CHUNK), CHUNK)
            ga = ga_ref[rows, :]
            gcum = _dot_exact_lhs(ltri, ga)
            gcum_t = gcum.T
            hs = range(B_HEADS)
            rsum = lambda a: jnp.sum(a, axis=-1, keepdims=True)
            cols = [slice(h * B_DIM, (h + 1) * B_DIM) for h in hs]
            q = [q_ref[rows, cols[h]] * GDN_SCALE for h in hs]
            k = [k_ref[rows, cols[h]] for h in hs]
            v = [v_ref[rows, cols[h]] for h in hs]
            z = [z_ref[rows, cols[h]] for h in hs]
            beta = [ga[:, h:h + 1] for h in hs]
            s0 = [st_ref[c, h] for h in hs]
            ds = [ds_scr[h] for h in hs]
            r = _b_chunk(q, k, v, beta, [gcum[:, B_HEADS + h:B_HEADS + h + 1] for h in hs],
                         [gcum_t[B_HEADS + h:B_HEADS + h + 1, :] for h in hs],
                         [gcum[CHUNK - 1:CHUNK, B_HEADS + h:B_HEADS + h + 1] for h in hs], s0)
            decay, tinv, eg, w, vn, qd, kd, kb, cd = (r[n] for n in ("decay", "tinv", "eg", "w", "vn", "qd", "kd", "kb", "cd"))
            dms = [(dm0_ref if h < 2 else dm1_ref)[rows, (h % 2) * B_DIM:(h % 2 + 1) * B_DIM].astype(F32) for h in hs]
            hn = [_headnorm_bwd(dms[h], r["o"][h], ng_ref[...], z[h]) for h in hs]
            do = [hn[h][0] for h in hs]
            dvn = [_dot(r["qk"][h], do[h], "tn") + _dot(kd[h], ds[h]) for h in hs]
            dqk = [_dot(do[h], vn[h], "nt") for h in hs]
            dqd = [_dot(do[h], s0[h], "nt") for h in hs]
            dkd = [_dot(vn[h], ds[h], "nt") for h in hs]
            dw = [-_dot(dvn[h], s0[h], "nt") for h in hs]
            dcd = [jnp.sum(jnp.sum(s0[h] * ds[h], axis=0, keepdims=True), axis=1, keepdims=True) for h in hs]
            for h in hs:
                ds_scr[h] = _dot(qd[h], do[h], "tn") + ds[h] * cd[h] - _dot(w[h], dvn[h], "tn")
            dbv = [_dot(tinv[h], dvn[h], "tn", hi=True) for h in hs]
            dkg = [_dot(tinv[h], dw[h], "tn", hi=True) for h in hs]
            dt = [_dot(dvn[h], r["bv"][h], "nt", hi=True) + _dot(dw[h], r["kg"][h], "nt", hi=True) for h in hs]
            tdt = [_dot(tinv[h], dt[h], "tn", hi=True) for h in hs]
            da = [jnp.where(strict, -_dot(tdt[h], tinv[h], "nt", hi=True), 0.0) for h in hs]
            dm = [da[h] * decay[h] for h in hs]
            dn = [dqk[h] * decay[h] for h in hs]
            e = [(da[h] * r["kk"][h] + dqk[h] * r["qkr"][h]) * decay[h] for h in hs]
            dkb = [_dot(dm[h], k[h]) + dkg[h] * eg[h] for h in hs]
            dk = [_dot(dm[h], kb[h], "tn") + _dot(dn[h], q[h], "tn") + dkd[h] * r["ekd"][h] + dkb[h] * beta[h] for h in hs]
            dq = [_dot(dn[h], k[h]) + dqd[h] * eg[h] for h in hs]
            tkd = [rsum(dkd[h] * kd[h]) for h in hs]
            dgc = [rsum(e[h]) - rsum(e[h].T) + rsum(dqd[h] * qd[h]) - tkd[h] + rsum(dkg[h] * r["kg"][h]) for h in hs]
            dgl = [jnp.sum(tkd[h], axis=0, keepdims=True) + dcd[h] * cd[h] for h in hs]
            dbeta = [rsum(dbv[h] * v[h]) + rsum(dkb[h] * k[h]) for h in hs]
            dbeta_m = sum(jnp.where(lane == h, dbeta[h], 0.0) for h in hs)
            dgc_m = sum(jnp.where(lane == B_HEADS + h, dgc[h], 0.0) for h in hs)
            dgl_m = sum(jnp.where(lane1 == B_HEADS + h, dgl[h], 0.0) for h in hs)
            dqkv_ref[rows, :] = jnp.concatenate([dq[h] * GDN_SCALE for h in hs] + dk + [dbv[h] * beta[h] for h in hs], axis=1)
            dz_ref[rows, :] = jnp.concatenate([hn[h][1] for h in hs], axis=1).astype(BF16)
            dga_ref[rows, :] = dbeta_m + _dot_exact_lhs(ltri, dgc_m, "tn") + dgl_m
            dng_ref[...] += sum(hn[h][2] for h in hs)
            return carry

        lax.fori_loop(0, nch, chunk, 0)

    part = lambda j: pl.BlockSpec((tb, B_WIDTH), lambda i, j=j: (nb - 1 - i, j))
    rowblk = lambda w, j=0: pl.BlockSpec((tb, w), lambda i, j=j: (nb - 1 - i, j))
    return pl.pallas_call(
        body, grid=(nb,),
        in_specs=[part(0), part(1), part(2), rowblk(128), rowblk(B_WIDTH, COL_B // B_WIDTH + 3),
                  pl.BlockSpec((1, B_DIM), lambda i: (0, 0)),
                  pl.BlockSpec((nch, B_HEADS, B_DIM, B_DIM), lambda i: (nb - 1 - i, 0, 0, 0)),
                  rowblk(256, 1), rowblk(256, 2)],
        out_specs=[rowblk(3 * B_WIDTH), rowblk(128), rowblk(B_WIDTH), pl.BlockSpec((1, B_DIM), lambda i: (0, 0))],
        out_shape=[jax.ShapeDtypeStruct((t, 3 * B_WIDTH), F32), jax.ShapeDtypeStruct((t, 128), F32),
                   jax.ShapeDtypeStruct((t, B_WIDTH), BF16), jax.ShapeDtypeStruct((1, B_DIM), F32)],
        scratch_shapes=[pltpu.VMEM((B_HEADS, B_DIM, B_DIM), F32)],
        compiler_params=_cparams(("arbitrary",)), name="gdn_bwd")(
            qkv, qkv, qkv, gates, proj, norm_g.reshape(1, B_DIM), states, dmixed, dmixed)


def _b_pre_bwd(proj, conv_w, alog_row, dtb_row, dqkv, dgates, tb=512):
    t = proj.shape[0]
    cb0 = COL_B // B_WIDTH

    def body(q_ref, k_ref, v_ref, qp_ref, kp_ref, vp_ref, w_ref, gi_ref, al_ref, dt_ref, dqkv_ref, dga_ref,
             dy_ref, dgi_ref, dw_ref, dal_ref, ddt_ref):
        first = pl.program_id(0) == 0

        @pl.when(first)
        def _():
            dw_ref[...] = jnp.zeros_like(dw_ref)
            dal_ref[...] = jnp.zeros_like(dal_ref)
            ddt_ref[...] = jnp.zeros_like(ddt_ref)

        for part, (c_ref, p_ref) in enumerate(((q_ref, qp_ref), (k_ref, kp_ref), (v_ref, vp_ref))):
            cols = slice(part * B_WIDTH, (part + 1) * B_WIDTH)
            cur = c_ref[...]
            prev = jnp.where(first, 0.0, p_ref[...])
            w = w_ref[:, cols]
            shifted = [_shift_rows(cur, prev, 3 - j, down=True) for j in range(3)] + [cur]
            y = shifted[0] * w[0:1] + shifted[1] * w[1:2] + shifted[2] * w[2:3] + shifted[3] * w[3:4]
            sg = _sigmoid(y)
            s = y * sg
            dsn = dqkv_ref[:, cols]
            if part < 2:
                outs = []
                for h in range(B_HEADS):
                    hc = slice(h * B_DIM, (h + 1) * B_DIM)
                    sh, dh = s[:, hc], dsn[:, hc]
                    rq = lax.rsqrt(jnp.sum(sh * sh, axis=-1, keepdims=True) + EPS)
                    nh = sh * rq
                    outs.append(rq * (dh - nh * jnp.sum(dh * nh, axis=-1, keepdims=True)))
                dsn = jnp.concatenate(outs, axis=1)
            dy = dsn * (sg * (1.0 + y * (1.0 - sg)))
            dy_ref[:, cols] = dy
            dw_ref[:, cols] += jnp.concatenate([jnp.sum(shifted[j] * dy, axis=0, keepdims=True) for j in range(CONV_K)], axis=0)
        g = gi_ref[...]
        dga = dga_ref[...]
        is_b, is_a = _gate_lane_masks(g.shape)
        beta = _sigmoid(g)
        pre = g + dt_ref[...]
        ea = jnp.exp(al_ref[...])
        la = -ea * _softplus(pre)
        dpre = jnp.where(is_a, dga * (-ea) * _sigmoid(pre), 0.0)
        dgi_ref[...] = jnp.where(is_b, dga * beta * (1.0 - beta), dpre).astype(BF16)
        dal_ref[...] += jnp.sum(jnp.where(is_a, dga * la, 0.0), axis=0, keepdims=True)
        ddt_ref[...] += jnp.sum(dpre, axis=0, keepdims=True)

    cur = lambda j: pl.BlockSpec((tb, B_WIDTH), lambda i, j=j: (i, cb0 + j))
    prv = lambda j: pl.BlockSpec((8, B_WIDTH), lambda i, j=j: (jnp.maximum(i * (tb // 8) - 1, 0), cb0 + j))
    vec = pl.BlockSpec((1, 128), lambda i: (0, 0))
    wspec = pl.BlockSpec((CONV_K, 3 * B_WIDTH), lambda i: (0, 0))
    return pl.pallas_call(
        body, grid=(t // tb,),
        in_specs=[cur(0), cur(1), cur(2), prv(0), prv(1), prv(2), wspec,
                  pl.BlockSpec((tb, 128), lambda i: (i, COL_G // 128)), vec, vec,
                  pl.BlockSpec((tb, 3 * B_WIDTH), lambda i: (i, 0)), pl.BlockSpec((tb, 128), lambda i: (i, 0))],
        out_specs=[pl.BlockSpec((tb, 3 * B_WIDTH), lambda i: (i, 0)), pl.BlockSpec((tb, 128), lambda i: (i, 0)), wspec, vec, vec],
        out_shape=[jax.ShapeDtypeStruct((t, 3 * B_WIDTH), F32), jax.ShapeDtypeStruct((t, 128), BF16),
                   jax.ShapeDtypeStruct((CONV_K, 3 * B_WIDTH), F32), jax.ShapeDtypeStruct((1, 128), F32), jax.ShapeDtypeStruct((1, 128), F32)],
        compiler_params=_cparams(("arbitrary",)), name="gdn_pre_bwd")(
            proj, proj, proj, proj, proj, proj, conv_w, proj, alog_row, dtb_row, dqkv, dgates)


def _conv_bwd_x(dy, w, cb, tb=512, name="conv_bwd_x"):
    t, c = dy.shape
    nb = t // tb

    def body(dy_ref, nx_ref, w_ref, dx_ref):
        cur = dy_ref[...]
        nxt = jnp.where(pl.program_id(0) == nb - 1, 0.0, nx_ref[...])
        w = w_ref[...]
        dx = cur * w[3:4]
        for j in range(3):
            dx = dx + _shift_rows(cur, nxt, 3 - j, down=False) * w[j:j + 1]
        dx_ref[...] = dx.astype(BF16)

    return pl.pallas_call(
        body, grid=(nb, c // cb),
        in_specs=[pl.BlockSpec((tb, cb), lambda i, j: (i, j)),
                  pl.BlockSpec((8, cb), lambda i, j: (jnp.minimum((i + 1) * (tb // 8), t // 8 - 1), j)),
                  pl.BlockSpec((CONV_K, cb), lambda i, j: (0, j))],
        out_specs=pl.BlockSpec((tb, cb), lambda i, j: (i, j)), out_shape=jax.ShapeDtypeStruct((t, c), BF16),
        compiler_params=_cparams(("parallel", "parallel")), name=name)(dy, dy, w)


def _c_gates(xc, wa_ref, ba_ref, wx_ref, bx_ref, lam_ref, is_row0):
    r = _sigmoid(_dot(xc, wa_ref[...]) + ba_ref[...])
    i = _sigmoid(_dot(xc, wx_ref[...]) + bx_ref[...])
    sp = _softplus(-lam_ref[...])
    log_a = -RG_C * r * sp
    a = jnp.exp(log_a)
    m2 = _neg_expm1(2.0 * log_a)
    mult = jnp.where(is_row0, 1.0, jnp.sqrt(jnp.maximum(m2, EPS)))
    return r, i, sp, log_a, a, m2, mult


def _row0_mask(tb, first):
    ridx = lax.broadcasted_iota(jnp.int32, (tb, C_WIDTH), 0)
    return (ridx == 0) & first


def _c_fwd(proj, conv_w, conv_b, wa, ba, wx, bx, lam, tb=512):
    t = proj.shape[0]
    cbx = COL_C // C_WIDTH

    def body(x_ref, xp_ref, y_ref, w_ref, cb_ref, wa_ref, ba_ref, wx_ref, bx_ref, lam_ref, out_ref, h_ref, a_scr, b_scr, h_scr):
        first = pl.program_id(0) == 0

        @pl.when(first)
        def _():
            h_scr[...] = jnp.zeros_like(h_scr)

        prev = jnp.where(first, 0.0, xp_ref[...])
        xc = _conv_fwd(x_ref[...], prev, w_ref[...]) + cb_ref[...]
        _, i, _, _, a, _, mult = _c_gates(xc, wa_ref, ba_ref, wx_ref, bx_ref, lam_ref, _row0_mask(tb, first))
        a_scr[...] = a
        b_scr[...] = mult * i * xc

        def step(blk, h):
            rows = pl.ds(pl.multiple_of(blk * 8, 8), 8)
            at, bt = a_scr[rows, :], b_scr[rows, :]
            out = []
            for j in range(8):
                h = at[j:j + 1] * h + bt[j:j + 1]
                out.append(h)
            h_ref[rows, :] = jnp.concatenate(out, axis=0)
            return h

        h_scr[...] = lax.fori_loop(0, tb // 8, step, h_scr[...])
        gl, _ = _gelu_tanh(y_ref[...])
        out_ref[...] = (gl * h_ref[...]).astype(BF16)

    vec = pl.BlockSpec((1, C_WIDTH), lambda i: (0, 0))
    mat = pl.BlockSpec((C_WIDTH, C_WIDTH), lambda i: (0, 0))
    row = pl.BlockSpec((tb, C_WIDTH), lambda i: (i, 0))
    return pl.pallas_call(
        body, grid=(t // tb,),
        in_specs=[pl.BlockSpec((tb, C_WIDTH), lambda i: (i, cbx)),
                  pl.BlockSpec((8, C_WIDTH), lambda i: (jnp.maximum(i * (tb // 8) - 1, 0), cbx)),
                  pl.BlockSpec((tb, C_WIDTH), lambda i: (i, cbx + 1)),
                  pl.BlockSpec((CONV_K, C_WIDTH), lambda i: (0, 0)), vec, mat, vec, mat, vec, vec],
        out_specs=[row, row],
        out_shape=[jax.ShapeDtypeStruct((t, C_WIDTH), BF16), jax.ShapeDtypeStruct((t, C_WIDTH), F32)],
        scratch_shapes=[pltpu.VMEM((tb, C_WIDTH), F32), pltpu.VMEM((tb, C_WIDTH), F32), pltpu.VMEM((1, C_WIDTH), F32)],
        compiler_params=_cparams(("arbitrary",)), name="lru_fwd")(proj, proj, proj, conv_w, conv_b, wa, ba, wx, bx, lam)


def _c_bwd(proj, conv_w, conv_b, wa, ba, wx, bx, lam, hs, dmixed, tb=512):
    t = proj.shape[0]
    nb = t // tb
    cbx = COL_C // C_WIDTH

    def body(x_ref, xp_ref, y_ref, w_ref, cb_ref, wa_ref, ba_ref, wx_ref, bx_ref, lam_ref, h_ref, hp_ref, dm_ref,
             dxc_ref, dyg_ref, dw_ref, dcb_ref, dwa_ref, dba_ref, dwx_ref, dbx_ref, dlam_ref, g_scr, a_scr, c_scr):
        step_id = pl.program_id(0)
        first = step_id == nb - 1

        @pl.when(step_id == 0)
        def _():
            c_scr[...] = jnp.zeros_like(c_scr)
            for ref in (dw_ref, dcb_ref, dwa_ref, dba_ref, dwx_ref, dbx_ref, dlam_ref):
                ref[...] = jnp.zeros_like(ref)

        cur = x_ref[...]
        prev = jnp.where(first, 0.0, xp_ref[...])
        w = w_ref[...]
        shifted = [_shift_rows(cur, prev, 3 - j, down=True) for j in range(3)] + [cur]
        xc = shifted[0] * w[0:1] + shifted[1] * w[1:2] + shifted[2] * w[2:3] + shifted[3] * w[3:4] + cb_ref[...]
        row0 = _row0_mask(tb, first)
        r, i, sp, log_a, a, m2, mult = _c_gates(xc, wa_ref, ba_ref, wx_ref, bx_ref, lam_ref, row0)
        h = h_ref[...]
        hprev = _shift_rows(h, jnp.where(first, 0.0, hp_ref[...]), 1, down=True)
        gl, dgl = _gelu_tanh(y_ref[...])
        dm = dm_ref[...].astype(F32)
        dyg_ref[...] = (dm * h * dgl).astype(BF16)
        g_scr[...] = dm * gl
        a_scr[...] = a

        def step(blk, carry):
            b = tb // 8 - 1 - blk
            rows = pl.ds(pl.multiple_of(b * 8, 8), 8)
            at, gt = a_scr[rows, :], g_scr[rows, :]
            out = [None] * 8
            for j in range(7, -1, -1):
                gj = gt[j:j + 1] + carry
                out[j] = gj
                carry = at[j:j + 1] * gj
            g_scr[rows, :] = jnp.concatenate(out, axis=0)
            return carry

        c_scr[...] = lax.fori_loop(0, tb // 8, step, c_scr[...])
        dbx = g_scr[...]
        da = dbx * hprev
        dmult = jnp.where(row0, 0.0, dbx * i * xc)
        di = dbx * mult * xc
        dxc = dbx * mult * i
        dm2 = jnp.where(m2 > EPS, dmult * 0.5 / mult, 0.0)
        dlog_a = da * a - 2.0 * a * a * dm2
        dr = dlog_a * (-RG_C) * sp
        dlam_ref[...] += jnp.sum(dlog_a * (-RG_C) * r, axis=0, keepdims=True) * (-_sigmoid(-lam_ref[...]))
        dpa = dr * r * (1.0 - r)
        dpx = di * i * (1.0 - i)
        dba_ref[...] += jnp.sum(dpa, axis=0, keepdims=True)
        dbx_ref[...] += jnp.sum(dpx, axis=0, keepdims=True)
        dwa_ref[...] += _dot(xc, dpa, "tn")
        dwx_ref[...] += _dot(xc, dpx, "tn")
        dxc = dxc + _dot(dpa, wa_ref[...], "nt") + _dot(dpx, wx_ref[...], "nt")
        dxc_ref[...] = dxc
        dcb_ref[...] += jnp.sum(dxc, axis=0, keepdims=True)
        dw_ref[...] += jnp.concatenate([jnp.sum(shifted[j] * dxc, axis=0, keepdims=True) for j in range(CONV_K)], axis=0)

    vec = pl.BlockSpec((1, C_WIDTH), lambda i: (0, 0))
    mat = pl.BlockSpec((C_WIDTH, C_WIDTH), lambda i: (0, 0))
    cw = pl.BlockSpec((CONV_K, C_WIDTH), lambda i: (0, 0))
    row = lambda j=0: pl.BlockSpec((tb, C_WIDTH), lambda i, j=j: (nb - 1 - i, j))
    halo = lambda j=0: pl.BlockSpec((8, C_WIDTH), lambda i, j=j: (jnp.maximum((nb - 1 - i) * (tb // 8) - 1, 0), j))
    return pl.pallas_call(
        body, grid=(nb,),
        in_specs=[row(cbx), halo(cbx), row(cbx + 1), cw, vec, mat, vec, mat, vec, vec, row(), halo(), row(3)],
        out_specs=[row(), row(), cw, vec, mat, vec, mat, vec, vec],
        out_shape=[jax.ShapeDtypeStruct((t, C_WIDTH), F32), jax.ShapeDtypeStruct((t, C_WIDTH), BF16),
                   jax.ShapeDtypeStruct((CONV_K, C_WIDTH), F32), jax.ShapeDtypeStruct((1, C_WIDTH), F32),
                   jax.ShapeDtypeStruct((C_WIDTH, C_WIDTH), F32), jax.ShapeDtypeStruct((1, C_WIDTH), F32),
                   jax.ShapeDtypeStruct((C_WIDTH, C_WIDTH), F32), jax.ShapeDtypeStruct((1, C_WIDTH), F32),
                   jax.ShapeDtypeStruct((1, C_WIDTH), F32)],
        scratch_shapes=[pltpu.VMEM((tb, C_WIDTH), F32), pltpu.VMEM((tb, C_WIDTH), F32), pltpu.VMEM((1, C_WIDTH), F32)],
        compiler_params=_cparams(("arbitrary",)), name="lru_bwd")(
            proj, proj, proj, conv_w, conv_b, wa, ba, wx, bx, lam, hs, hs, dmixed)


def _mesh_pos():
    return lax.axis_index("x"), lax.axis_index("y"), lax.axis_index("c")


def _all_gather(x, name):
    def body(x_ref, out_ref, send_sems, recv_sems, local_sem):
        mx, my, mc = _mesh_pos()
        me, sibling = (mx, my, mc), (mx, my, 1 - mc)
        chips = [(1 - mx, my), (mx, 1 - my), (1 - mx, 1 - my)]

        def slot(px, py, pc):
            return out_ref.at[4 * px + 2 * py + pc]

        def copy(k, block, to, src=None):
            return pltpu.make_async_remote_copy(
                src_ref=slot(*block) if src is None else src, dst_ref=slot(*block),
                send_sem=send_sems.at[k], recv_sem=recv_sems.at[k], device_id=to, device_id_type=MESH)

        mine = pltpu.make_async_copy(x_ref, slot(*me), local_sem)
        mine.start()
        first = [copy(0, me, sibling, src=x_ref)]
        first += [copy(1 + j, me, (*chip, mc), src=x_ref) for j, chip in enumerate(chips)]
        for cp in first:
            cp.start()
        passed = [copy(4 + j, (*chip, mc), sibling) for j, chip in enumerate(chips)]
        for j, chip in enumerate(chips):
            copy(1 + j, (*chip, mc), me).wait_recv()
            passed[j].start()
        copy(0, sibling, me).wait_recv()
        for j, chip in enumerate(chips):
            copy(4 + j, (*chip, 1 - mc), me).wait_recv()
        for cp in first + passed:
            cp.wait_send()
        mine.wait()

    return pl.pallas_call(
        body, out_shape=jax.ShapeDtypeStruct((N_DEV,) + x.shape, x.dtype),
        in_specs=[pl.BlockSpec(memory_space=pl.ANY)], out_specs=pl.BlockSpec(memory_space=pl.ANY),
        scratch_shapes=[pltpu.SemaphoreType.DMA((7,)), pltpu.SemaphoreType.DMA((7,)), pltpu.SemaphoreType.DMA(())],
        name=name)(x)


def _all_to_all(x, name):
    def body(x_ref, out_ref, send_sems, recv_sems, local_sem):
        mx, my, mc = _mesh_pos()
        me = 4 * mx + 2 * my + mc
        mine = pltpu.make_async_copy(x_ref.at[me], out_ref.at[me], local_sem)
        mine.start()
        copies = []
        for k in range(1, N_DEV):
            px = 1 - mx if k & 4 else mx
            py = 1 - my if k & 2 else my
            pc = 1 - mc if k & 1 else mc
            copies.append(pltpu.make_async_remote_copy(
                src_ref=x_ref.at[4 * px + 2 * py + pc], dst_ref=out_ref.at[me],
                send_sem=send_sems.at[k - 1], recv_sem=recv_sems.at[k - 1], device_id=(px, py, pc), device_id_type=MESH))
        for cp in copies:
            cp.start()
        for cp in copies:
            cp.wait()
        mine.wait()

    return pl.pallas_call(
        body, out_shape=jax.ShapeDtypeStruct(x.shape, x.dtype),
        in_specs=[pl.BlockSpec(memory_space=pl.ANY)], out_specs=pl.BlockSpec(memory_space=pl.ANY),
        scratch_shapes=[pltpu.SemaphoreType.DMA((7,)), pltpu.SemaphoreType.DMA((7,)), pltpu.SemaphoreType.DMA(())],
        name=name)(x)


def _adamw_math(w, g, m, v):
    m = ADAM_B1 * m + (1.0 - ADAM_B1) * g
    v = ADAM_B2 * v + (1.0 - ADAM_B2) * (g * g)
    m_hat = m / (1.0 - ADAM_B1 ** ADAM_STEP)
    v_hat = v / (1.0 - ADAM_B2 ** ADAM_STEP)
    delta = -ADAM_LR * (m_hat / (jnp.sqrt(v_hat) + ADAM_EPS) + ADAM_WD * w)
    return delta, m, v


def _sum_adamw(parts, w, m, v, tr, name):
    p, r, c = parts.shape
    tr = min(tr, r)
    assert r % tr == 0

    def body(p_ref, w_ref, m_ref, v_ref, g_ref, d_ref, nm_ref, nv_ref):
        g = p_ref[0].astype(F32)
        for j in range(1, p):
            g = g + p_ref[j].astype(F32)
        delta, nm, nv = _adamw_math(w_ref[...], g, m_ref[...], v_ref[...])
        g_ref[...] = g
        d_ref[...] = delta
        nm_ref[...] = nm
        nv_ref[...] = nv

    row = pl.BlockSpec((tr, c), lambda i: (i, 0))
    return pl.pallas_call(
        body, grid=(r // tr,), in_specs=[pl.BlockSpec((p, tr, c), lambda i: (0, i, 0)), row, row, row],
        out_specs=[row] * 4, out_shape=[jax.ShapeDtypeStruct((r, c), F32)] * 4,
        compiler_params=_cparams(("parallel",)), name=name)(parts, w, m, v)


def _sum_parts(parts, name):
    p, r, c = parts.shape

    def body(p_ref, o_ref):
        g = p_ref[0]
        for j in range(1, p):
            g = g + p_ref[j]
        o_ref[...] = g

    return pl.pallas_call(body, out_shape=jax.ShapeDtypeStruct((r, c), F32), name=name)(parts)


def _pack(arrs, mult=1024):
    flat = jnp.concatenate([a.reshape(-1).astype(F32) for a in arrs])
    n = flat.shape[0]
    npad = -n % mult
    return jnp.pad(flat, (0, npad)).reshape(-1, 128)


def _unpack(buf, shapes):
    flat = buf.reshape(-1)
    out, off = [], 0
    for s in shapes:
        n = 1
        for d in s:
            n *= d
        out.append(flat[off:off + n].reshape(s))
        off += n
    return out


def _block_diag(w):
    rows = [jnp.pad(w[i], ((0, 0), (i * C_BLOCK_DIM, C_WIDTH - (i + 1) * C_BLOCK_DIM))) for i in range(C_BLOCKS)]
    return jnp.concatenate(rows, axis=0)


def _diag_blocks(m):
    m4 = m.reshape(C_BLOCKS, C_BLOCK_DIM, C_BLOCKS, C_BLOCK_DIM)
    return jnp.stack([m4[i, :, i, :] for i in range(C_BLOCKS)])


def _gate_row(v):
    return jnp.pad(v.astype(F32), (B_HEADS, 128 - 2 * B_HEADS)).reshape(1, 128)


def _permute_w_in(w):
    pad = jnp.zeros(w.shape[:-1] + (D_IN_PAD - D_IN,), w.dtype)
    return jnp.concatenate([w[..., :3072], w[..., 3080:3592], w[..., 3072:3080], pad], axis=-1)


def _unpermute_w_in(w):
    return jnp.concatenate([w[..., :3072], w[..., COL_G:COL_G + 8], w[..., 3072:COL_G]], axis=-1)


_WEIGHTS = ['norm1_g', 'w_in', 'hgrn_lb_logits', 'hgrn_norm_g', 'gdn_conv_w', 'gdn_a_log', 'gdn_dt_bias', 'gdn_norm_g',
            'lru_conv_w', 'lru_conv_b', 'lru_w_a', 'lru_b_a', 'lru_w_x', 'lru_b_x', 'lru_lambda', 'w_out', 'norm2_g',
            'w_up', 'w_down', 'final_norm_g']
_BIG = ('w_in', 'w_out', 'w_up', 'w_down')
_SHARDED_SMALL = ('gdn_conv_w', 'lru_conv_w')


def _step(x, target, w, m, v):
    t = x.shape[0]
    mx, my, mc = _mesh_pos()
    me = 4 * mx + 2 * my + mc

    g_in = _all_gather(w['w_in'].astype(BF16), "ag_w_in")
    g_out = _all_gather(w['w_out'].astype(BF16), "ag_w_out")
    g_up = _all_gather(w['w_up'].astype(BF16), "ag_w_up")
    g_down = _all_gather(w['w_down'].astype(BF16), "ag_w_down")
    w_in = _permute_w_in(jnp.moveaxis(g_in, 0, 2).reshape(DEPTH, D_MODEL, D_IN))
    w_out = jnp.moveaxis(g_out, 0, 1).reshape(DEPTH, D_MODEL, D_MODEL)
    w_up = jnp.moveaxis(g_up, 0, 2).reshape(DEPTH, D_MODEL, D_FF)
    w_down = jnp.moveaxis(g_down, 0, 1).reshape(DEPTH, D_FF, D_MODEL)
    conv_shapes = [w['gdn_conv_w'].shape, w['lru_conv_w'].shape]
    g_conv = _all_gather(_pack([w['gdn_conv_w'], w['lru_conv_w']]), "ag_conv")
    gdn_cw, lru_cw = [], []
    for j in range(N_DEV):
        a, b = _unpack(g_conv[j], conv_shapes)
        gdn_cw.append(a)
        lru_cw.append(b)
    gdn_cw = jnp.concatenate(gdn_cw, axis=-1)
    lru_cw = jnp.concatenate(lru_cw, axis=-1)

    lbnd = _lb_fwd(w['hgrn_lb_logits'])
    row = lambda a: a.reshape(1, -1)

    def c_args(l):
        return (lru_cw[l], row(w['lru_conv_b'][l]), _block_diag(w['lru_w_a'][l]), row(w['lru_b_a'][l]),
                _block_diag(w['lru_w_x'][l]), row(w['lru_b_x'][l]), row(w['lru_lambda'][l]))

    saved = []
    xl = x
    for l in range(DEPTH):
        h = _rms_fwd(xl, w['norm1_g'][l], name="rms_fwd")
        proj = _matmul(h, w_in[l], "nn", name="mm_proj")
        mix_a, st_a = _a_fwd(proj, lbnd[l], w['hgrn_norm_g'][l])
        alr, dtr = _gate_row(w['gdn_a_log'][l]), _gate_row(w['gdn_dt_bias'][l])
        qkv, gates = _b_pre_fwd(proj, gdn_cw[l], alr, dtr)
        mix_b, st_b = _b_fwd(qkv, gates, proj, w['gdn_norm_g'][l])
        mix_c, hs = _c_fwd(proj, *c_args(l))
        mixed = jnp.concatenate([mix_a, mix_b, mix_c], axis=1)
        x_mid = _matmul(mixed, w_out[l], "nn", residual=xl, name="mm_out")
        h2 = _rms_fwd(x_mid, w['norm2_g'][l], name="rms_fwd")
        up = _matmul(h2, w_up[l], "nn", name="mm_up")
        act = _relu2_fwd(up)
        x_next = _matmul(act, w_down[l], "nn", residual=x_mid, name="mm_down")
        saved.append(dict(x=xl, h=h, proj=proj, st_a=st_a, qkv=qkv, gates=gates, st_b=st_b, hs=hs, mixed=mixed,
                          x_mid=x_mid, h2=h2, up=up, act=act, alr=alr, dtr=dtr))
        xl = x_next
    loss, dx, dgf = _loss_head(xl, w['final_norm_g'], target)

    gs = {n: [None] * DEPTH for n in _WEIGHTS}
    for l in reversed(range(DEPTH)):
        s = saved[l]
        dact = _matmul(dx, w_down[l], "nt", name="mm_dact")
        gs['w_down'][l] = _matmul(s['act'], dx, "tn", name="mm_dw_down")
        dup = _relu2_bwd(dact, s['up'])
        dh2 = _matmul(dup, w_up[l], "nt", name="mm_dh2")
        gs['w_up'][l] = _matmul(s['h2'], dup, "tn", name="mm_dw_up")
        dx_mid, dg2 = _rms_bwd(dh2, s['x_mid'], w['norm2_g'][l], dx, name="rms_bwd")
        gs['norm2_g'][l] = dg2[0]
        dmixed = _matmul(dx_mid, w_out[l], "nt", name="mm_dmixed")
        gs['w_out'][l] = _matmul(s['mixed'], dx_mid, "tn", name="mm_dw_out")
        proj = s['proj']
        dpa, dlb, dnga = _a_bwd(proj, lbnd[l], w['hgrn_norm_g'][l], s['st_a'], dmixed)
        gs['hgrn_lb_logits'][l] = dlb[0]
        gs['hgrn_norm_g'][l] = dnga[0]
        dqkv, dgates, dz, dngb = _b_bwd(s['qkv'], s['gates'], proj, w['gdn_norm_g'][l], s['st_b'], dmixed)
        dyb, dgi, dcwb, dal, ddt = _b_pre_bwd(proj, gdn_cw[l], s['alr'], s['dtr'], dqkv, dgates)
        dxb = _conv_bwd_x(dyb, gdn_cw[l], B_WIDTH, name="conv_bwd_x_gdn")
        gs['gdn_norm_g'][l] = dngb[0]
        gs['gdn_conv_w'][l] = dcwb
        gs['gdn_a_log'][l] = dal[0, B_HEADS:2 * B_HEADS]
        gs['gdn_dt_bias'][l] = ddt[0, B_HEADS:2 * B_HEADS]
        dxc, dyg, dcwc, dcb, dwa, dba, dwx, dbx, dlam = _c_bwd(proj, *c_args(l), s['hs'], dmixed)
        dxc_in = _conv_bwd_x(dxc, lru_cw[l], C_WIDTH, name="conv_bwd_x_lru")
        gs['lru_conv_w'][l] = dcwc
        gs['lru_conv_b'][l] = dcb[0]
        gs['lru_w_a'][l] = _diag_blocks(dwa)
        gs['lru_b_a'][l] = dba[0]
        gs['lru_w_x'][l] = _diag_blocks(dwx)
        gs['lru_b_x'][l] = dbx[0]
        gs['lru_lambda'][l] = dlam[0]
        pad = jnp.zeros((t, D_IN_PAD - COL_G - 128), BF16)
        dproj = jnp.concatenate([dpa, dxb, dz, dxc_in, dyg, dgi, pad], axis=1)
        dh = _matmul(dproj, w_in[l], "nt", name="mm_dh")
        gs['w_in'][l] = _matmul(s['h'], dproj, "tn", name="mm_dw_in")
        dx, dg1 = _rms_bwd(dh, s['x'], w['norm1_g'][l], dx_mid, name="rms_bwd")
        gs['norm1_g'][l] = dg1[0]
    grad_x = dx
    part = {n: jnp.stack(gs[n]) for n in _WEIGHTS if n != 'final_norm_g'}
    part['final_norm_g'] = dgf[0]
    part['hgrn_lb_logits'] = _lb_bwd(w['hgrn_lb_logits'], part['hgrn_lb_logits'])

    send = {
        'w_in': jnp.moveaxis(_unpermute_w_in(part['w_in']).reshape(DEPTH, D_MODEL, N_DEV, D_IN // N_DEV), 2, 0),
        'w_out': jnp.moveaxis(part['w_out'].reshape(DEPTH, N_DEV, D_MODEL // N_DEV, D_MODEL), 1, 0),
        'w_up': jnp.moveaxis(part['w_up'].reshape(DEPTH, D_MODEL, N_DEV, D_FF // N_DEV), 2, 0),
        'w_down': jnp.moveaxis(part['w_down'].reshape(DEPTH, N_DEV, D_FF // N_DEV, D_MODEL), 1, 0),
    }
    grads, deltas, new_m, new_v = {}, {}, {}, {}
    for n in _BIG:
        recv = _all_to_all(send[n].astype(BF16), "a2a_" + n)
        shp = w[n].shape
        r2 = lambda a: a.reshape(-1, shp[-1])
        g, d, nm, nv = _sum_adamw(recv.reshape(N_DEV, -1, shp[-1]), r2(w[n]), r2(m[n]), r2(v[n]), 256, "adamw_" + n)
        grads[n], deltas[n], new_m[n], new_v[n] = (a.reshape(shp) for a in (g, d, nm, nv))

    small = [n for n in _WEIGHTS if n not in _BIG]
    packed = _pack([part[n] for n in small] + [loss])
    total = _sum_parts(_all_gather(packed, "ag_small"), "sum_small")
    summed = _unpack(total, [part[n].shape for n in small] + [(1, 1)])
    loss_total = summed[-1].reshape(())
    gsmall = dict(zip(small, summed[:-1]))
    for n in _SHARDED_SMALL:
        width = w[n].shape[-1]
        gsmall[n] = lax.dynamic_slice_in_dim(gsmall[n], me * width, width, axis=2)
    pk = lambda d: _pack([d[n] for n in small])
    _, d, nm, nv = _sum_adamw(pk(gsmall)[None], pk(w), pk(m), pk(v), 4096, "adamw_small")
    shapes = [w[n].shape for n in small]
    for n, dd, mm, vv in zip(small, _unpack(d, shapes), _unpack(nm, shapes), _unpack(nv, shapes)):
        grads[n], deltas[n], new_m[n], new_v[n] = gsmall[n], dd, mm, vv
    return loss_total, grad_x, grads, deltas, new_m, new_v


def kernel(x, norm1_g, w_in, hgrn_lb_logits, hgrn_norm_g, gdn_conv_w, gdn_a_log, gdn_dt_bias, gdn_norm_g, lru_conv_w, lru_conv_b, lru_w_a, lru_b_a, lru_w_x, lru_b_x, lru_lambda, w_out, norm2_g, w_up, w_down, final_norm_g, loss_target, m_norm1_g, m_w_in, m_hgrn_lb_logits, m_hgrn_norm_g, m_gdn_conv_w, m_gdn_a_log, m_gdn_dt_bias, m_gdn_norm_g, m_lru_conv_w, m_lru_conv_b, m_lru_w_a, m_lru_b_a, m_lru_w_x, m_lru_b_x, m_lru_lambda, m_w_out, m_norm2_g, m_w_up, m_w_down, m_final_norm_g, v_norm1_g, v_w_in, v_hgrn_lb_logits, v_hgrn_norm_g, v_gdn_conv_w, v_gdn_a_log, v_gdn_dt_bias, v_gdn_norm_g, v_lru_conv_w, v_lru_conv_b, v_lru_w_a, v_lru_b_a, v_lru_w_x, v_lru_b_x, v_lru_lambda, v_w_out, v_norm2_g, v_w_up, v_w_down, v_final_norm_g):
    w = dict(zip(_WEIGHTS, (norm1_g, w_in, hgrn_lb_logits, hgrn_norm_g, gdn_conv_w, gdn_a_log, gdn_dt_bias, gdn_norm_g, lru_conv_w, lru_conv_b, lru_w_a, lru_b_a, lru_w_x, lru_b_x, lru_lambda, w_out, norm2_g, w_up, w_down, final_norm_g)))
    m = dict(zip(_WEIGHTS, (m_norm1_g, m_w_in, m_hgrn_lb_logits, m_hgrn_norm_g, m_gdn_conv_w, m_gdn_a_log, m_gdn_dt_bias, m_gdn_norm_g, m_lru_conv_w, m_lru_conv_b, m_lru_w_a, m_lru_b_a, m_lru_w_x, m_lru_b_x, m_lru_lambda, m_w_out, m_norm2_g, m_w_up, m_w_down, m_final_norm_g)))
    v = dict(zip(_WEIGHTS, (v_norm1_g, v_w_in, v_hgrn_lb_logits, v_hgrn_norm_g, v_gdn_conv_w, v_gdn_a_log, v_gdn_dt_bias, v_gdn_norm_g, v_lru_conv_w, v_lru_conv_b, v_lru_w_a, v_lru_b_a, v_lru_w_x, v_lru_b_x, v_lru_lambda, v_w_out, v_norm2_g, v_w_up, v_w_down, v_final_norm_g)))
    loss, grad_x, grads, deltas, new_m, new_v = _step(x[0], loss_target[0], w, m, v)
    return (loss, grad_x[None], *[grads[n] for n in _WEIGHTS], *[deltas[n] for n in _WEIGHTS],
            *[new_m[n] for n in _WEIGHTS], *[new_v[n] for n in _WEIGHTS])
```

```python
import functools

import jax
import jax.numpy as jnp
from jax import lax
from jax.experimental import pallas as pl
from jax.experimental.pallas import tpu as pltpu

F32 = jnp.float32
BF16 = jnp.bfloat16
HI = lax.Precision.HIGHEST
MESH = pl.DeviceIdType.MESH

N_DEV = 8
D_MODEL = 1024
DEPTH = 4
A_HEADS, A_DIM, A_WIDTH = 4, 64, 256
B_HEADS, B_DIM, B_WIDTH = 4, 128, 512
C_WIDTH, C_BLOCKS, C_BLOCK_DIM = 256, 4, 64
D_IN = 3592
D_IN_PAD = 3840
COL_A, COL_B, COL_C, COL_G = 0, 1024, 3072, 3584
D_FF = 4096
CONV_K = 4
CHUNK = 64
SUB = 16
RG_C = 8.0
EPS = 1e-6
TINY = 1e-30
EXP_CLAMP = 80.0
GDN_SCALE = B_DIM ** -0.5
ADAM_LR, ADAM_B1, ADAM_B2, ADAM_EPS, ADAM_WD, ADAM_STEP = 0.001, 0.9, 0.999, 1e-08, 0.01, 10
VMEM_LIMIT = 56 * 1024 * 1024


def _cparams(sem=None):
    return pltpu.CompilerParams(dimension_semantics=sem, vmem_limit_bytes=VMEM_LIMIT)


_DIMS = {"nn": (((1,), (0,)), ((), ())), "nt": (((1,), (1,)), ((), ())), "tn": (((0,), (0,)), ((), ()))}


def _split_bf16(x):
    hi = x.astype(BF16)
    return hi, (x - hi.astype(F32)).astype(BF16)


def _dot(a, b, mode="nn", hi=False):
    if not hi:
        return lax.dot_general(a.astype(BF16), b.astype(BF16), _DIMS[mode], preferred_element_type=F32)
    ah, al = _split_bf16(a.astype(F32))
    bh, bl = _split_bf16(b.astype(F32))
    ka = 0 if mode == "tn" else 1
    kb = 1 if mode == "nt" else 0
    return lax.dot_general(jnp.concatenate([ah, ah, al], axis=ka), jnp.concatenate([bh, bl, bh], axis=kb),
                           _DIMS[mode], preferred_element_type=F32)


def _dot_exact_lhs(lhs, x, mode="nn"):
    l_bf16 = lhs.astype(BF16)
    x1 = x.astype(BF16)
    r1 = x - x1.astype(F32)
    x2 = r1.astype(BF16)
    x3 = (r1 - x2.astype(F32)).astype(BF16)
    ka = 0 if mode == "tn" else 1
    return lax.dot_general(jnp.concatenate([l_bf16] * 3, axis=ka), jnp.concatenate([x1, x2, x3], axis=0),
                           _DIMS[mode], preferred_element_type=F32)


def _iota2(n, m):
    return lax.broadcasted_iota(jnp.int32, (n, m), 0), lax.broadcasted_iota(jnp.int32, (n, m), 1)


def _tril(n, strict=False):
    r, c = _iota2(n, n)
    return (r > c) if strict else (r >= c)


def _sigmoid(x):
    return 1.0 / (1.0 + jnp.exp(-x))


def _softplus(x):
    return jnp.maximum(x, 0.0) + jnp.log(1.0 + jnp.exp(-jnp.abs(x)))


def _neg_expm1(z):
    series = -z * (1.0 + z * (0.5 + z * (1.0 / 6.0)))
    return jnp.where(z > -1e-2, series, 1.0 - jnp.exp(z))


def _gelu_tanh(x):
    c = 0.7978845608028654
    u = c * (x + 0.044715 * x * x * x)
    t = jnp.tanh(u)
    g = 0.5 * x * (1.0 + t)
    dg = 0.5 * (1.0 + t) + 0.5 * x * (1.0 - t * t) * c * (1.0 + 3.0 * 0.044715 * x * x)
    return g, dg


def _shift_rows(cur, halo, s, down=True):
    n = cur.shape[0]
    ridx = lax.broadcasted_iota(jnp.int32, (8, cur.shape[1]), 0)
    if down:
        main = pltpu.roll(cur, s, 0)
        fix = jnp.where(ridx < s, pltpu.roll(halo, s, 0), main[0:8])
        return jnp.concatenate([fix, main[8:]], axis=0)
    main = pltpu.roll(cur, n - s, 0)
    fix = jnp.where(ridx >= 8 - s, pltpu.roll(halo, 8 - s, 0), main[n - 8:n])
    return jnp.concatenate([main[:n - 8], fix], axis=0)


def _conv_fwd(cur, prev8, w):
    y = cur * w[3:4]
    for j in range(3):
        y = y + _shift_rows(cur, prev8, 3 - j, down=True) * w[j:j + 1]
    return y


def _pick_tile(n, pref):
    best = None
    for cand in range(128, min(n, pref) + 1, 128):
        if n % cand == 0:
            best = cand
    return best if best is not None else n


def _matmul(a, b, mode, out_dtype=F32, residual=None, tm=512, tn=1024, tk=1024, name="matmul"):
    if mode == "nn":
        (m, k), n = a.shape, b.shape[1]
    elif mode == "nt":
        (m, k), n = a.shape, b.shape[0]
    else:
        (k, m), n = a.shape, b.shape[1]
    tm, tn, tk = _pick_tile(m, tm), _pick_tile(n, tn), _pick_tile(k, tk)
    nk = k // tk
    a_spec = pl.BlockSpec((tk, tm), lambda i, j, kk: (kk, i)) if mode == "tn" else pl.BlockSpec((tm, tk), lambda i, j, kk: (i, kk))
    b_spec = pl.BlockSpec((tn, tk), lambda i, j, kk: (j, kk)) if mode == "nt" else pl.BlockSpec((tk, tn), lambda i, j, kk: (kk, j))
    o_spec = pl.BlockSpec((tm, tn), lambda i, j, kk: (i, j))
    has_res = residual is not None

    def body(*refs):
        if has_res:
            a_ref, b_ref, r_ref, o_ref, acc = refs
        else:
            a_ref, b_ref, o_ref, acc = refs
        kk = pl.program_id(2)

        @pl.when(kk == 0)
        def _():
            acc[...] = jnp.zeros_like(acc)

        acc[...] += _dot(a_ref[...], b_ref[...], mode)

        @pl.when(kk == nk - 1)
        def _():
            r = acc[...]
            if has_res:
                r = r + r_ref[...]
            o_ref[...] = r.astype(out_dtype)

    ins = [a, b] + ([residual] if has_res else [])
    specs = [a_spec, b_spec] + ([o_spec] if has_res else [])
    return pl.pallas_call(
        body, grid=(m // tm, n // tn, nk), in_specs=specs, out_specs=o_spec,
        out_shape=jax.ShapeDtypeStruct((m, n), out_dtype), scratch_shapes=[pltpu.VMEM((tm, tn), F32)],
        compiler_params=_cparams(("parallel", "parallel", "arbitrary")), name=name)(*ins)


def _mm_rows(a, w, mode, tm, name, residual=None, epilogue=None, up=None, norm=None):
    t, k = a.shape
    n = w.shape[1] if mode == "nn" else w.shape[0]
    tm = min(tm, t)
    assert t % tm == 0

    def body(*refs):
        a_ref, w_ref = refs[0], refs[1]
        y = _dot(a_ref[...], w_ref[...], mode)
        if residual is not None:
            y = y + refs[2][...]
        if epilogue == "relu2":
            r = jnp.maximum(y, 0.0)
            refs[-2][...] = (r * r).astype(BF16)
            refs[-1][...] = y.astype(BF16)
        elif epilogue == "drelu2":
            refs[-1][...] = (y * 2.0 * jnp.maximum(refs[2][...].astype(F32), 0.0)).astype(BF16)
        elif epilogue == "rms_fwd":
            rinv = lax.rsqrt(jnp.mean(y * y, axis=-1, keepdims=True) + EPS)
            refs[-2][...] = y
            refs[-1][...] = (y * rinv * refs[-3][...]).astype(BF16)
        elif epilogue == "rms_bwd":
            x_ref, g_ref, dres_ref, dx_ref, dg_ref = refs[2:]

            @pl.when(pl.program_id(0) == 0)
            def _():
                dg_ref[...] = jnp.zeros_like(dg_ref)

            xv = x_ref[...]
            rinv = lax.rsqrt(jnp.mean(xv * xv, axis=-1, keepdims=True) + EPS)
            xhat = xv * rinv
            dxh = y * g_ref[...]
            dx_ref[...] = dres_ref[...] + rinv * (dxh - xhat * jnp.mean(dxh * xhat, axis=-1, keepdims=True))
            dg_ref[...] += jnp.sum(y * xhat, axis=0, keepdims=True)
        else:
            refs[-1][...] = y

    rows = lambda width: pl.BlockSpec((tm, width), lambda i: (i, 0))
    vec = pl.BlockSpec((1, n), lambda i: (0, 0))
    ins, specs = [a, w], [rows(k), pl.BlockSpec(w.shape, lambda i: (0, 0))]
    if residual is not None:
        ins.append(residual)
        specs.append(rows(n))
    if epilogue == "drelu2":
        ins.append(up)
        specs.append(rows(n))
    if epilogue == "rms_fwd":
        ins.append(norm.reshape(1, n))
        specs.append(vec)
        out_specs, out_shape = [rows(n), rows(n)], [jax.ShapeDtypeStruct((t, n), F32), jax.ShapeDtypeStruct((t, n), BF16)]
    elif epilogue == "rms_bwd":
        ins += [norm[0], norm[1].reshape(1, n), norm[2]]
        specs += [rows(n), vec, rows(n)]
        out_specs, out_shape = [rows(n), vec], [jax.ShapeDtypeStruct((t, n), F32), jax.ShapeDtypeStruct((1, n), F32)]
    elif epilogue == "relu2":
        out_specs, out_shape = [rows(n), rows(n)], [jax.ShapeDtypeStruct((t, n), BF16)] * 2
    else:
        out_specs, out_shape = rows(n), jax.ShapeDtypeStruct((t, n), BF16 if epilogue == "drelu2" else F32)
    return pl.pallas_call(body, grid=(t // tm,), in_specs=specs, out_specs=out_specs, out_shape=out_shape,
                          compiler_params=_cparams(("arbitrary" if epilogue == "rms_bwd" else "parallel",)), name=name)(*ins)


def _mm_tn(a, b, tn, tk, name, split=False):
    t, m = a.shape
    n = b.shape[1]
    tn, tk = min(tn, n), min(tk, t)
    assert n % tn == 0 and t % tk == 0
    nk = t // tk

    def body(a_ref, b_ref, o_ref, acc):
        kk = pl.program_id(1)

        @pl.when(kk == 0)
        def _():
            acc[...] = jnp.zeros_like(acc)

        acc[...] += _dot(a_ref[...], b_ref[...], "tn")

        @pl.when(kk == nk - 1)
        def _():
            o_ref[...] = acc[...].astype(BF16)

    if split:
        out_spec, out_shape = pl.BlockSpec((None, m, tn), lambda j, kk: (j, 0, 0)), jax.ShapeDtypeStruct((n // tn, m, tn), BF16)
    else:
        out_spec, out_shape = pl.BlockSpec((m, tn), lambda j, kk: (0, j)), jax.ShapeDtypeStruct((m, n), BF16)
    return pl.pallas_call(
        body, grid=(n // tn, nk),
        in_specs=[pl.BlockSpec((tk, m), lambda j, kk: (kk, 0)), pl.BlockSpec((tk, tn), lambda j, kk: (kk, j))],
        out_specs=out_spec, out_shape=out_shape, scratch_shapes=[pltpu.VMEM((m, tn), F32)],
        compiler_params=_cparams(("parallel", "arbitrary")), name=name)(a, b)


def _rms_fwd(x, g, tb=512, name="rms_fwd"):
    t, d = x.shape

    def body(x_ref, g_ref, h_ref):
        xv = x_ref[...]
        rinv = lax.rsqrt(jnp.mean(xv * xv, axis=-1, keepdims=True) + EPS)
        h_ref[...] = (xv * rinv * g_ref[...]).astype(BF16)

    return pl.pallas_call(
        body, grid=(t // tb,), in_specs=[pl.BlockSpec((tb, d), lambda i: (i, 0)), pl.BlockSpec((1, d), lambda i: (0, 0))],
        out_specs=pl.BlockSpec((tb, d), lambda i: (i, 0)), out_shape=jax.ShapeDtypeStruct((t, d), BF16),
        compiler_params=_cparams(("parallel",)), name=name)(x, g.reshape(1, d))


def _rms_bwd(dh, x, g, dres, tb=512, name="rms_bwd"):
    t, d = x.shape

    def body(dh_ref, x_ref, g_ref, dres_ref, dx_ref, dg_ref):
        @pl.when(pl.program_id(0) == 0)
        def _():
            dg_ref[...] = jnp.zeros_like(dg_ref)

        xv = x_ref[...]
        dhv = dh_ref[...].astype(F32)
        rinv = lax.rsqrt(jnp.mean(xv * xv, axis=-1, keepdims=True) + EPS)
        xhat = xv * rinv
        dxh = dhv * g_ref[...]
        dx_ref[...] = dres_ref[...] + rinv * (dxh - xhat * jnp.mean(dxh * xhat, axis=-1, keepdims=True))
        dg_ref[...] += jnp.sum(dhv * xhat, axis=0, keepdims=True)

    row = pl.BlockSpec((tb, d), lambda i: (i, 0))
    vec = pl.BlockSpec((1, d), lambda i: (0, 0))
    return pl.pallas_call(
        body, grid=(t // tb,), in_specs=[row, row, vec, row], out_specs=[row, vec],
        out_shape=[jax.ShapeDtypeStruct((t, d), F32), jax.ShapeDtypeStruct((1, d), F32)],
        compiler_params=_cparams(("arbitrary",)), name=name)(dh, x, g.reshape(1, d), dres)


def _loss_head(x, g, target, tb=512):
    t, d = x.shape

    def body(x_ref, g_ref, t_ref, loss_ref, dx_ref, dg_ref):
        @pl.when(pl.program_id(0) == 0)
        def _():
            dg_ref[...] = jnp.zeros_like(dg_ref)
            loss_ref[...] = jnp.zeros_like(loss_ref)

        xv = x_ref[...]
        rinv = lax.rsqrt(jnp.mean(xv * xv, axis=-1, keepdims=True) + EPS)
        xhat = xv * rinv
        err = xhat * g_ref[...] - t_ref[...]
        loss_ref[...] += 0.5 * jnp.sum(jnp.mean(err * err, axis=-1, keepdims=True), axis=0, keepdims=True)
        dy = err * (1.0 / d)
        dxh = dy * g_ref[...]
        dx_ref[...] = rinv * (dxh - xhat * jnp.mean(dxh * xhat, axis=-1, keepdims=True))
        dg_ref[...] += jnp.sum(dy * xhat, axis=0, keepdims=True)

    row = pl.BlockSpec((tb, d), lambda i: (i, 0))
    vec = pl.BlockSpec((1, d), lambda i: (0, 0))
    one = pl.BlockSpec((1, 1), lambda i: (0, 0))
    return pl.pallas_call(
        body, grid=(t // tb,), in_specs=[row, vec, row], out_specs=[one, row, vec],
        out_shape=[jax.ShapeDtypeStruct((1, 1), F32), jax.ShapeDtypeStruct((t, d), F32), jax.ShapeDtypeStruct((1, d), F32)],
        compiler_params=_cparams(("arbitrary",)), name="loss_head")(x, g.reshape(1, d), target)


def _relu2_fwd(up, tb=512):
    t, d = up.shape

    def body(u_ref, a_ref):
        r = jnp.maximum(u_ref[...], 0.0)
        a_ref[...] = (r * r).astype(BF16)

    row = pl.BlockSpec((tb, d), lambda i: (i, 0))
    return pl.pallas_call(body, grid=(t // tb,), in_specs=[row], out_specs=row, out_shape=jax.ShapeDtypeStruct((t, d), BF16),
                          compiler_params=_cparams(("parallel",)), name="relu2_fwd")(up)


def _relu2_bwd(dact, up, tb=512):
    t, d = up.shape

    def body(da_ref, u_ref, o_ref):
        o_ref[...] = (da_ref[...] * 2.0 * jnp.maximum(u_ref[...], 0.0)).astype(BF16)

    row = pl.BlockSpec((tb, d), lambda i: (i, 0))
    return pl.pallas_call(body, grid=(t // tb,), in_specs=[row, row], out_specs=row, out_shape=jax.ShapeDtypeStruct((t, d), BF16),
                          compiler_params=_cparams(("parallel",)), name="relu2_bwd")(dact, up)


def _lb_fwd(logits):
    def body(l_ref, o_ref):
        lg = l_ref[...]
        e = jnp.exp(lg - jnp.max(lg, axis=0, keepdims=True))
        p = e / jnp.sum(e, axis=0, keepdims=True)
        c = jnp.zeros_like(p[0:1])
        rows = [c]
        for l in range(1, DEPTH):
            c = c + p[l:l + 1]
            rows.append(c)
        o_ref[...] = jnp.minimum(jnp.maximum(jnp.concatenate(rows, axis=0), 0.0), 1.0 - EPS)

    return pl.pallas_call(body, out_shape=jax.ShapeDtypeStruct(logits.shape, F32), name="lb_fwd")(logits)


def _lb_bwd(logits, dlb):
    def body(l_ref, d_ref, o_ref):
        lg = l_ref[...]
        e = jnp.exp(lg - jnp.max(lg, axis=0, keepdims=True))
        p = e / jnp.sum(e, axis=0, keepdims=True)
        hi = 1.0 - EPS
        c = jnp.zeros_like(p[0:1])
        dc = []
        for l in range(1, DEPTH):
            c = c + p[l:l + 1]
            gl = jnp.where(c < 0.0, 0.0, jnp.where(c == 0.0, 0.5, 1.0)) * jnp.where(c > hi, 0.0, jnp.where(c == hi, 0.5, 1.0))
            dc.append(d_ref[l:l + 1, :] * gl)
        dp = [jnp.zeros_like(c)]
        for j in range(1, DEPTH):
            s = dc[j - 1]
            for l in range(j + 1, DEPTH):
                s = s + dc[l - 1]
            dp.append(s)
        dpm = jnp.concatenate(dp, axis=0)
        o_ref[...] = p * (dpm - jnp.sum(p * dpm, axis=0, keepdims=True))

    return pl.pallas_call(body, out_shape=jax.ShapeDtypeStruct(logits.shape, F32), name="lb_bwd")(logits, dlb)


def _a_gates(qi, fi, lbh):
    sq = _sigmoid(qi)
    q = qi * sq
    sg = _sigmoid(fi)
    sgn = _sigmoid(-fi)
    f = lbh + (1.0 - lbh) * sg
    logf = jnp.log(jnp.maximum(f, TINY))
    k = (1.0 - lbh) * sgn
    return q, sq, sg, sgn, f, logf, k


def _a_intra(q, k, cum):
    qts, kes, rows = [], [], []
    for i in range(CHUNK // SUB):
        lo = i * SUB
        r = cum[lo - 1:lo] if i > 0 else jnp.zeros_like(cum[0:1])
        eq = jnp.exp(cum[lo:lo + SUB] - r)
        ek = jnp.exp(jnp.minimum(r - cum, EXP_CLAMP))
        qt = q[lo:lo + SUB] * eq
        rows.append(_dot(qt, k * ek, "nt", hi=True))
        qts.append((qt, eq))
        kes.append(ek)
    attn = jnp.where(_tril(CHUNK), jnp.concatenate(rows, axis=0), 0.0)
    return attn, qts, kes


def _a_intra_bwd(dattn, k, qts, kes):
    dq_rows = []
    dk = jnp.zeros_like(k)
    for i in range(CHUNK // SUB):
        lo = i * SUB
        da = dattn[lo:lo + SUB]
        qt, eq = qts[i]
        dq_rows.append(_dot(da, k * kes[i], "nn", hi=True) * eq)
        dk = dk + _dot(da, qt, "tn", hi=True) * kes[i]
    return jnp.concatenate(dq_rows, axis=0), dk


def _headnorm_fwd(o, g, gate_in):
    rinv = lax.rsqrt(jnp.mean(o * o, axis=-1, keepdims=True) + EPS)
    sg = _sigmoid(gate_in)
    return o * rinv * g * (gate_in * sg)


def _headnorm_bwd(dout, o, g, gate_in):
    rinv = lax.rsqrt(jnp.mean(o * o, axis=-1, keepdims=True) + EPS)
    xhat = o * rinv
    sg = _sigmoid(gate_in)
    silu = gate_in * sg
    dy = dout * silu
    dgate = dout * xhat * g * (sg * (1.0 + gate_in * (1.0 - sg)))
    dxh = dy * g
    do = rinv * (dxh - xhat * jnp.mean(dxh * xhat, axis=-1, keepdims=True))
    return do, dgate, jnp.sum(dy * xhat, axis=0, keepdims=True)


def _a_fwd(proj, lb, norm_g, tb=256):
    t = proj.shape[0]
    nch = tb // CHUNK

    def body(q_ref, f_ref, i_ref, g_ref, lb_ref, ng_ref, out_ref, st_ref, s_scr):
        @pl.when(pl.program_id(0) == 0)
        def _():
            s_scr[...] = jnp.zeros_like(s_scr)

        ltri = _tril(CHUNK).astype(F32)

        def chunk(c, carry):
            rows = pl.ds(pl.multiple_of(c * CHUNK, CHUNK), CHUNK)
            hs = range(A_HEADS)
            cols = [slice(h * A_DIM, (h + 1) * A_DIM) for h in hs]
            gates = [_a_gates(q_ref[rows, cols[h]], f_ref[rows, cols[h]], lb_ref[:, cols[h]]) for h in hs]
            q, k = [gates[h][0] for h in hs], [gates[h][6] for h in hs]
            v = [i_ref[rows, cols[h]] for h in hs]
            cum = [_dot_exact_lhs(ltri, gates[h][5]) for h in hs]
            cl = [cum[h][CHUNK - 1:CHUNK] for h in hs]
            s0 = [s_scr[h] for h in hs]
            for h in hs:
                st_ref[c, h] = s0[h]
            attn = [_a_intra(q[h], k[h], cum[h])[0] for h in hs]
            qs0 = [_dot(q[h] * jnp.exp(cum[h]), s0[h]) for h in hs]
            o = [qs0[h] + _dot(attn[h], v[h]) for h in hs]
            kd = [k[h] * jnp.exp(cl[h] - cum[h]) for h in hs]
            for h in hs:
                s_scr[h] = s0[h] * jnp.exp(cl[h]).T + _dot(kd[h], v[h], "tn")
            outs = [_headnorm_fwd(o[h], ng_ref[...], g_ref[rows, cols[h]]) for h in hs]
            out_ref[rows, :] = jnp.concatenate(outs, axis=1).astype(BF16)
            return carry

        lax.fori_loop(0, nch, chunk, 0)

    colblk = lambda j: pl.BlockSpec((tb, A_WIDTH), lambda i, j=j: (i, j))
    return pl.pallas_call(
        body, grid=(t // tb,),
        in_specs=[colblk(0), colblk(1), colblk(2), colblk(3), pl.BlockSpec((1, A_WIDTH), lambda i: (0, 0)),
                  pl.BlockSpec((1, A_DIM), lambda i: (0, 0))],
        out_specs=[pl.BlockSpec((tb, A_WIDTH), lambda i: (i, 0)),
                   pl.BlockSpec((nch, A_HEADS, A_DIM, A_DIM), lambda i: (i, 0, 0, 0))],
        out_shape=[jax.ShapeDtypeStruct((t, A_WIDTH), BF16), jax.ShapeDtypeStruct((t // CHUNK, A_HEADS, A_DIM, A_DIM), F32)],
        scratch_shapes=[pltpu.VMEM((A_HEADS, A_DIM, A_DIM), F32)],
        compiler_params=_cparams(("arbitrary",)), name="hgrn_fwd")(proj, proj, proj, proj, lb.reshape(1, A_WIDTH), norm_g.reshape(1, A_DIM))


def _a_bwd(proj, lb, norm_g, states, dmixed, tb=256):
    t = proj.shape[0]
    nch = tb // CHUNK
    nb = t // tb

    def body(q_ref, f_ref, i_ref, g_ref, lb_ref, ng_ref, st_ref, dm_ref, dp_ref, dlb_ref, dng_ref, ds_scr):
        @pl.when(pl.program_id(0) == 0)
        def _():
            ds_scr[...] = jnp.zeros_like(ds_scr)
            dlb_ref[...] = jnp.zeros_like(dlb_ref)
            dng_ref[...] = jnp.zeros_like(dng_ref)

        ltri = _tril(CHUNK).astype(F32)
        mask = _tril(CHUNK)

        def chunk(cc, carry):
            c = nch - 1 - cc
            rows = pl.ds(pl.multiple_of(c * CHUNK, CHUNK), CHUNK)
            hs = range(A_HEADS)
            cols = [slice(h * A_DIM, (h + 1) * A_DIM) for h in hs]
            qi = [q_ref[rows, cols[h]] for h in hs]
            gi = [g_ref[rows, cols[h]] for h in hs]
            lbh = [lb_ref[:, cols[h]] for h in hs]
            gates = [_a_gates(qi[h], f_ref[rows, cols[h]], lbh[h]) for h in hs]
            q, sq, sg, sgn, f, logf, k = ([gates[h][j] for h in hs] for j in range(7))
            v = [i_ref[rows, cols[h]] for h in hs]
            cum = [_dot_exact_lhs(ltri, logf[h]) for h in hs]
            cl = [cum[h][CHUNK - 1:CHUNK] for h in hs]
            ecum = [jnp.exp(cum[h]) for h in hs]
            ekd = [jnp.exp(cl[h] - cum[h]) for h in hs]
            cd = [jnp.exp(cl[h]) for h in hs]
            qd = [q[h] * ecum[h] for h in hs]
            kd = [k[h] * ekd[h] for h in hs]
            s0 = [st_ref[c, h] for h in hs]
            ds = [ds_scr[h] for h in hs]
            intra = [_a_intra(q[h], k[h], cum[h]) for h in hs]
            attn = [intra[h][0] for h in hs]
            qs0 = [_dot(qd[h], s0[h]) for h in hs]
            o = [qs0[h] + _dot(attn[h], v[h]) for h in hs]
            hn = [_headnorm_bwd(dm_ref[rows, cols[h]].astype(F32), o[h], ng_ref[...], gi[h]) for h in hs]
            do = [hn[h][0] for h in hs]
            dqd = [_dot(do[h], s0[h], "nt") for h in hs]
            dattn = [jnp.where(mask, _dot(do[h], v[h], "nt"), 0.0) for h in hs]
            dv = [_dot(attn[h], do[h], "tn") + _dot(kd[h], ds[h]) for h in hs]
            dkd = [_dot(v[h], ds[h], "nt") for h in hs]
            dcd = [jnp.sum((s0[h] * ds[h]).T, axis=0, keepdims=True) for h in hs]
            for h in hs:
                ds_scr[h] = _dot(qd[h], do[h], "tn") + ds[h] * cd[h].T
            ib = [_a_intra_bwd(dattn[h], k[h], intra[h][1], intra[h][2]) for h in hs]
            dq = [dqd[h] * ecum[h] + ib[h][0] for h in hs]
            dk = [dkd[h] * ekd[h] + ib[h][1] for h in hs]
            dkk = [dkd[h] * kd[h] for h in hs]
            dcum = [dqd[h] * qd[h] - dkk[h] + q[h] * ib[h][0] - k[h] * ib[h][1] for h in hs]
            dcl = [jnp.sum(dkk[h], axis=0, keepdims=True) + dcd[h] * cd[h] for h in hs]
            dlogf = [_dot_exact_lhs(ltri, dcum[h], "tn") + dcl[h] for h in hs]
            dfv = [jnp.where(f[h] > TINY, dlogf[h] / f[h], 0.0) for h in hs]
            dfi = [dfv[h] * (1.0 - lbh[h]) * sg[h] * (1.0 - sg[h]) - dk[h] * (1.0 - lbh[h]) * sgn[h] * (1.0 - sgn[h]) for h in hs]
            dlbs = [jnp.sum(dfv[h] * (1.0 - sg[h]) - dk[h] * sgn[h], axis=0, keepdims=True) for h in hs]
            dqs = [dq[h] * (sq[h] * (1.0 + qi[h] * (1.0 - sq[h]))) for h in hs]
            dp_ref[rows, :] = jnp.concatenate(dqs + dfi + dv + [hn[h][1] for h in hs], axis=1).astype(BF16)
            dlb_ref[...] += jnp.concatenate(dlbs, axis=1)
            dng_ref[...] += sum(hn[h][2] for h in hs)
            return carry

        lax.fori_loop(0, nch, chunk, 0)

    colblk = lambda j: pl.BlockSpec((tb, A_WIDTH), lambda i, j=j: (nb - 1 - i, j))
    vec = lambda n: pl.BlockSpec((1, n), lambda i: (0, 0))
    outs = pl.pallas_call(
        body, grid=(nb,),
        in_specs=[colblk(0), colblk(1), colblk(2), colblk(3), vec(A_WIDTH), vec(A_DIM),
                  pl.BlockSpec((nch, A_HEADS, A_DIM, A_DIM), lambda i: (nb - 1 - i, 0, 0, 0)), colblk(0)],
        out_specs=[pl.BlockSpec((tb, 4 * A_WIDTH), lambda i: (nb - 1 - i, 0)), vec(A_WIDTH), vec(A_DIM)],
        out_shape=[jax.ShapeDtypeStruct((t, 4 * A_WIDTH), BF16), jax.ShapeDtypeStruct((1, A_WIDTH), F32), jax.ShapeDtypeStruct((1, A_DIM), F32)],
        scratch_shapes=[pltpu.VMEM((A_HEADS, A_DIM, A_DIM), F32)],
        compiler_params=_cparams(("arbitrary",)), name="hgrn_bwd")(
            proj, proj, proj, proj, lb.reshape(1, A_WIDTH), norm_g.reshape(1, A_DIM), states, dmixed)
    return outs


def _gate_lane_masks(shape):
    lane = lax.broadcasted_iota(jnp.int32, shape, 1)
    return lane < B_HEADS, (lane >= B_HEADS) & (lane < 2 * B_HEADS)


def _b_pre_fwd(proj, conv_w, alog_row, dtb_row, tb=512):
    t = proj.shape[0]
    cb0 = COL_B // B_WIDTH

    def body(q_ref, k_ref, v_ref, qp_ref, kp_ref, vp_ref, w_ref, gi_ref, al_ref, dt_ref, qkv_ref, gates_ref):
        first = pl.program_id(0) == 0
        for part, (c_ref, p_ref) in enumerate(((q_ref, qp_ref), (k_ref, kp_ref), (v_ref, vp_ref))):
            cols = slice(part * B_WIDTH, (part + 1) * B_WIDTH)
            prev = jnp.where(first, 0.0, p_ref[...])
            y = _conv_fwd(c_ref[...], prev, w_ref[:, cols])
            s = y * _sigmoid(y)
            if part < 2:
                outs = []
                for h in range(B_HEADS):
                    sh = s[:, h * B_DIM:(h + 1) * B_DIM]
                    outs.append(sh * lax.rsqrt(jnp.sum(sh * sh, axis=-1, keepdims=True) + EPS))
                s = jnp.concatenate(outs, axis=1)
            qkv_ref[:, cols] = s
        g = gi_ref[...]
        is_b, is_a = _gate_lane_masks(g.shape)
        la = -jnp.exp(al_ref[...]) * _softplus(g + dt_ref[...])
        gates_ref[...] = jnp.where(is_b, _sigmoid(g), jnp.where(is_a, la, 0.0))

    cur = lambda j: pl.BlockSpec((tb, B_WIDTH), lambda i, j=j: (i, cb0 + j))
    prv = lambda j: pl.BlockSpec((8, B_WIDTH), lambda i, j=j: (jnp.maximum(i * (tb // 8) - 1, 0), cb0 + j))
    vec = pl.BlockSpec((1, 128), lambda i: (0, 0))
    return pl.pallas_call(
        body, grid=(t // tb,),
        in_specs=[cur(0), cur(1), cur(2), prv(0), prv(1), prv(2), pl.BlockSpec((CONV_K, 3 * B_WIDTH), lambda i: (0, 0)),
                  pl.BlockSpec((tb, 128), lambda i: (i, COL_G // 128)), vec, vec],
        out_specs=[pl.BlockSpec((tb, 3 * B_WIDTH), lambda i: (i, 0)), pl.BlockSpec((tb, 128), lambda i: (i, 0))],
        out_shape=[jax.ShapeDtypeStruct((t, 3 * B_WIDTH), F32), jax.ShapeDtypeStruct((t, 128), F32)],
        compiler_params=_cparams(("parallel",)), name="gdn_pre_fwd")(proj, proj, proj, proj, proj, proj, conv_w, proj, alog_row, dtb_row)


def _inv_unit_lower(amats):
    r, c = _iota2(CHUNK, CHUNK)
    eye = jnp.where(r == c, 1.0, 0.0)
    ps = [eye - a for a in amats]
    aks = amats
    for _ in range(5):
        aks = [_dot(ak, ak, hi=True) for ak in aks]
        ps = [p + _dot(p, ak, hi=True) for p, ak in zip(ps, aks)]
    return ps


def _b_chunk(qs, ks, vs, betas, gcs, grows, gls, s0s):
    hs = range(len(qs))
    causal, strict = _tril(CHUNK), _tril(CHUNK, strict=True)
    decay = [jnp.where(causal, jnp.exp(jnp.minimum(gcs[h] - grows[h], 0.0)), 0.0) for h in hs]
    kb = [ks[h] * betas[h] for h in hs]
    kk = [_dot(kb[h], ks[h], "nt") for h in hs]
    qkr = [_dot(qs[h], ks[h], "nt") for h in hs]
    tinv = _inv_unit_lower([jnp.where(strict, kk[h] * decay[h], 0.0) for h in hs])
    eg = [jnp.exp(gcs[h]) for h in hs]
    bv = [vs[h] * betas[h] for h in hs]
    kg = [kb[h] * eg[h] for h in hs]
    u = [_dot(tinv[h], bv[h], hi=True) for h in hs]
    w = [_dot(tinv[h], kg[h], hi=True) for h in hs]
    qk = [qkr[h] * decay[h] for h in hs]
    qd = [qs[h] * eg[h] for h in hs]
    ekd = [jnp.exp(gls[h] - gcs[h]) for h in hs]
    kd = [ks[h] * ekd[h] for h in hs]
    cd = [jnp.exp(gls[h]) for h in hs]
    ws = [_dot(w[h], s0s[h]) for h in hs]
    qs0 = [_dot(qd[h], s0s[h]) for h in hs]
    vn = [u[h] - ws[h] for h in hs]
    o = [qs0[h] + _dot(qk[h], vn[h]) for h in hs]
    s1 = [s0s[h] * cd[h] + _dot(kd[h], vn[h], "tn") for h in hs]
    return dict(decay=decay, kb=kb, kk=kk, tinv=tinv, eg=eg, bv=bv, kg=kg, w=w, qkr=qkr, qk=qk, qd=qd, ekd=ekd,
                kd=kd, cd=cd, vn=vn, o=o, s1=s1)


def _b_fwd(qkv, gates, proj, norm_g, tb=256):
    t = qkv.shape[0]
    nch = tb // CHUNK

    def body(q_ref, k_ref, v_ref, ga_ref, z_ref, ng_ref, out_ref, st_ref, s_scr):
        @pl.when(pl.program_id(0) == 0)
        def _():
            s_scr[...] = jnp.zeros_like(s_scr)

        ltri = _tril(CHUNK).astype(F32)

        def chunk(c, carry):
            rows = pl.ds(pl.multiple_of(c * CHUNK, CHUNK), CHUNK)
            ga = ga_ref[rows, :]
            gcum = _dot_exact_lhs(ltri, ga)
            gcum_t = gcum.T
            hs = range(B_HEADS)
            cols = [slice(h * B_DIM, (h + 1) * B_DIM) for h in hs]
            s0s = [s_scr[h] for h in hs]
            for h in hs:
                st_ref[c, h] = s0s[h]
            r = _b_chunk([q_ref[rows, cols[h]] * GDN_SCALE for h in hs], [k_ref[rows, cols[h]] for h in hs],
                         [v_ref[rows, cols[h]] for h in hs], [ga[:, h:h + 1] for h in hs],
                         [gcum[:, B_HEADS + h:B_HEADS + h + 1] for h in hs], [gcum_t[B_HEADS + h:B_HEADS + h + 1, :] for h in hs],
                         [gcum[CHUNK - 1:CHUNK, B_HEADS + h:B_HEADS + h + 1] for h in hs], s0s)
            for h in hs:
                s_scr[h] = r["s1"][h]
            outs = [_headnorm_fwd(r["o"][h], ng_ref[...], z_ref[rows, cols[h]]) for h in hs]
            out_ref[rows, :] = jnp.concatenate(outs, axis=1).astype(BF16)
            return carry

        lax.fori_loop(0, nch, chunk, 0)

    part = lambda j: pl.BlockSpec((tb, B_WIDTH), lambda i, j=j: (i, j))
    return pl.pallas_call(
        body, grid=(t // tb,),
        in_specs=[part(0), part(1), part(2), pl.BlockSpec((tb, 128), lambda i: (i, 0)),
                  pl.BlockSpec((tb, B_WIDTH), lambda i: (i, COL_B // B_WIDTH + 3)), pl.BlockSpec((1, B_DIM), lambda i: (0, 0))],
        out_specs=[pl.BlockSpec((tb, B_WIDTH), lambda i: (i, 0)),
                   pl.BlockSpec((nch, B_HEADS, B_DIM, B_DIM), lambda i: (i, 0, 0, 0))],
        out_shape=[jax.ShapeDtypeStruct((t, B_WIDTH), BF16), jax.ShapeDtypeStruct((t // CHUNK, B_HEADS, B_DIM, B_DIM), F32)],
        scratch_shapes=[pltpu.VMEM((B_HEADS, B_DIM, B_DIM), F32)],
        compiler_params=_cparams(("arbitrary",)), name="gdn_fwd")(qkv, qkv, qkv, gates, proj, norm_g.reshape(1, B_DIM))


def _b_bwd(qkv, gates, proj, norm_g, states, dmixed, tb=256):
    t = qkv.shape[0]
    nch = tb // CHUNK
    nb = t // tb

    def body(q_ref, k_ref, v_ref, ga_ref, z_ref, ng_ref, st_ref, dm0_ref, dm1_ref, dqkv_ref, dga_ref, dz_ref, dng_ref, ds_scr):
        @pl.when(pl.program_id(0) == 0)
        def _():
            ds_scr[...] = jnp.zeros_like(ds_scr)
            dng_ref[...] = jnp.zeros_like(dng_ref)

        ltri = _tril(CHUNK).astype(F32)
        strict = _tril(CHUNK, strict=True)
        lane = lax.broadcasted_iota(jnp.int32, (CHUNK, 128), 1)
        lane1 = lax.broadcasted_iota(jnp.int32, (1, 128), 1)

        def chunk(cc, carry):
            c = nch - 1 - cc
            rows = pl.ds(pl.multiple_of(c * CHUNK, CHUNK), CHUNK)
            ga = ga_ref[rows, :]
            gcum = _dot_exact_lhs(ltri, ga)
            gcum_t = gcum.T
            hs = range(B_HEADS)
            rsum = lambda a: jnp.sum(a, axis=-1, keepdims=True)
            cols = [slice(h * B_DIM, (h + 1) * B_DIM) for h in hs]
            q = [q_ref[rows, cols[h]] * GDN_SCALE for h in hs]
            k = [k_ref[rows, cols[h]] for h in hs]
            v = [v_ref[rows, cols[h]] for h in hs]
            z = [z_ref[rows, cols[h]] for h in hs]
            beta = [ga[:, h:h + 1] for h in hs]
            s0 = [st_ref[c, h] for h in hs]
            ds = [ds_scr[h] for h in hs]
            r = _b_chunk(q, k, v, beta, [gcum[:, B_HEADS + h:B_HEADS + h + 1] for h in hs],
                         [gcum_t[B_HEADS + h:B_HEADS + h + 1, :] for h in hs],
                         [gcum[CHUNK - 1:CHUNK, B_HEADS + h:B_HEADS + h + 1] for h in hs], s0)
            decay, tinv, eg, w, vn, qd, kd, kb, cd = (r[n] for n in ("decay", "tinv", "eg", "w", "vn", "qd", "kd", "kb", "cd"))
            dms = [(dm0_ref if h < 2 else dm1_ref)[rows, (h % 2) * B_DIM:(h % 2 + 1) * B_DIM].astype(F32) for h in hs]
            hn = [_headnorm_bwd(dms[h], r["o"][h], ng_ref[...], z[h]) for h in hs]
            do = [hn[h][0] for h in hs]
            dvn = [_dot(r["qk"][h], do[h], "tn") + _dot(kd[h], ds[h]) for h in hs]
            dqk = [_dot(do[h], vn[h], "nt") for h in hs]
            dqd = [_dot(do[h], s0[h], "nt") for h in hs]
            dkd = [_dot(vn[h], ds[h], "nt") for h in hs]
            dw = [-_dot(dvn[h], s0[h], "nt") for h in hs]
            dcd = [jnp.sum(jnp.sum(s0[h] * ds[h], axis=0, keepdims=True), axis=1, keepdims=True) for h in hs]
            for h in hs:
                ds_scr[h] = _dot(qd[h], do[h], "tn") + ds[h] * cd[h] - _dot(w[h], dvn[h], "tn")
            dbv = [_dot(tinv[h], dvn[h], "tn", hi=True) for h in hs]
            dkg = [_dot(tinv[h], dw[h], "tn", hi=True) for h in hs]
            dt = [_dot(dvn[h], r["bv"][h], "nt", hi=True) + _dot(dw[h], r["kg"][h], "nt", hi=True) for h in hs]
            tdt = [_dot(tinv[h], dt[h], "tn", hi=True) for h in hs]
            da = [jnp.where(strict, -_dot(tdt[h], tinv[h], "nt", hi=True), 0.0) for h in hs]
            dm = [da[h] * decay[h] for h in hs]
            dn = [dqk[h] * decay[h] for h in hs]
            e = [(da[h] * r["kk"][h] + dqk[h] * r["qkr"][h]) * decay[h] for h in hs]
            dkb = [_dot(dm[h], k[h]) + dkg[h] * eg[h] for h in hs]
            dk = [_dot(dm[h], kb[h], "tn") + _dot(dn[h], q[h], "tn") + dkd[h] * r["ekd"][h] + dkb[h] * beta[h] for h in hs]
            dq = [_dot(dn[h], k[h]) + dqd[h] * eg[h] for h in hs]
            tkd = [rsum(dkd[h] * kd[h]) for h in hs]
            dgc = [rsum(e[h]) - rsum(e[h].T) + rsum(dqd[h] * qd[h]) - tkd[h] + rsum(dkg[h] * r["kg"][h]) for h in hs]
            dgl = [jnp.sum(tkd[h], axis=0, keepdims=True) + dcd[h] * cd[h] for h in hs]
            dbeta = [rsum(dbv[h] * v[h]) + rsum(dkb[h] * k[h]) for h in hs]
            dbeta_m = sum(jnp.where(lane == h, dbeta[h], 0.0) for h in hs)
            dgc_m = sum(jnp.where(lane == B_HEADS + h, dgc[h], 0.0) for h in hs)
            dgl_m = sum(jnp.where(lane1 == B_HEADS + h, dgl[h], 0.0) for h in hs)
            dqkv_ref[rows, :] = jnp.concatenate([dq[h] * GDN_SCALE for h in hs] + dk + [dbv[h] * beta[h] for h in hs], axis=1)
            dz_ref[rows, :] = jnp.concatenate([hn[h][1] for h in hs], axis=1).astype(BF16)
            dga_ref[rows, :] = dbeta_m + _dot_exact_lhs(ltri, dgc_m, "tn") + dgl_m
            dng_ref[...] += sum(hn[h][2] for h in hs)
            return carry

        lax.fori_loop(0, nch, chunk, 0)

    part = lambda j: pl.BlockSpec((tb, B_WIDTH), lambda i, j=j: (nb - 1 - i, j))
    rowblk = lambda w, j=0: pl.BlockSpec((tb, w), lambda i, j=j: (nb - 1 - i, j))
    return pl.pallas_call(
        body, grid=(nb,),
        in_specs=[part(0), part(1), part(2), rowblk(128), rowblk(B_WIDTH, COL_B // B_WIDTH + 3),
                  pl.BlockSpec((1, B_DIM), lambda i: (0, 0)),
                  pl.BlockSpec((nch, B_HEADS, B_DIM, B_DIM), lambda i: (nb - 1 - i, 0, 0, 0)),
                  rowblk(256, 1), rowblk(256, 2)],
        out_specs=[rowblk(3 * B_WIDTH), rowblk(128), rowblk(B_WIDTH), pl.BlockSpec((1, B_DIM), lambda i: (0, 0))],
        out_shape=[jax.ShapeDtypeStruct((t, 3 * B_WIDTH), F32), jax.ShapeDtypeStruct((t, 128), F32),
                   jax.ShapeDtypeStruct((t, B_WIDTH), BF16), jax.ShapeDtypeStruct((1, B_DIM), F32)],
        scratch_shapes=[pltpu.VMEM((B_HEADS, B_DIM, B_DIM), F32)],
        compiler_params=_cparams(("arbitrary",)), name="gdn_bwd")(
            qkv, qkv, qkv, gates, proj, norm_g.reshape(1, B_DIM), states, dmixed, dmixed)


def _b_pre_bwd(proj, conv_w, alog_row, dtb_row, dqkv, dgates, tb=512):
    t = proj.shape[0]
    cb0 = COL_B // B_WIDTH

    def body(q_ref, k_ref, v_ref, qp_ref, kp_ref, vp_ref, w_ref, gi_ref, al_ref, dt_ref, dqkv_ref, dga_ref,
             dy_ref, dgi_ref, dw_ref, dal_ref, ddt_ref):
        first = pl.program_id(0) == 0

        @pl.when(first)
        def _():
            dw_ref[...] = jnp.zeros_like(dw_ref)
            dal_ref[...] = jnp.zeros_like(dal_ref)
            ddt_ref[...] = jnp.zeros_like(ddt_ref)

        for part, (c_ref, p_ref) in enumerate(((q_ref, qp_ref), (k_ref, kp_ref), (v_ref, vp_ref))):
            cols = slice(part * B_WIDTH, (part + 1) * B_WIDTH)
            cur = c_ref[...]
            prev = jnp.where(first, 0.0, p_ref[...])
            w = w_ref[:, cols]
            shifted = [_shift_rows(cur, prev, 3 - j, down=True) for j in range(3)] + [cur]
            y = shifted[0] * w[0:1] + shifted[1] * w[1:2] + shifted[2] * w[2:3] + shifted[3] * w[3:4]
            sg = _sigmoid(y)
            s = y * sg
            dsn = dqkv_ref[:, cols]
            if part < 2:
                outs = []
                for h in range(B_HEADS):
                    hc = slice(h * B_DIM, (h + 1) * B_DIM)
                    sh, dh = s[:, hc], dsn[:, hc]
                    rq = lax.rsqrt(jnp.sum(sh * sh, axis=-1, keepdims=True) + EPS)
                    nh = sh * rq
                    outs.append(rq * (dh - nh * jnp.sum(dh * nh, axis=-1, keepdims=True)))
                dsn = jnp.concatenate(outs, axis=1)
            dy = dsn * (sg * (1.0 + y * (1.0 - sg)))
            dy_ref[:, cols] = dy
            dw_ref[:, cols] += jnp.concatenate([jnp.sum(shifted[j] * dy, axis=0, keepdims=True) for j in range(CONV_K)], axis=0)
        g = gi_ref[...]
        dga = dga_ref[...]
        is_b, is_a = _gate_lane_masks(g.shape)
        beta = _sigmoid(g)
        pre = g + dt_ref[...]
        ea = jnp.exp(al_ref[...])
        la = -ea * _softplus(pre)
        dpre = jnp.where(is_a, dga * (-ea) * _sigmoid(pre), 0.0)
        dgi_ref[...] = jnp.where(is_b, dga * beta * (1.0 - beta), dpre).astype(BF16)
        dal_ref[...] += jnp.sum(jnp.where(is_a, dga * la, 0.0), axis=0, keepdims=True)
        ddt_ref[...] += jnp.sum(dpre, axis=0, keepdims=True)

    cur = lambda j: pl.BlockSpec((tb, B_WIDTH), lambda i, j=j: (i, cb0 + j))
    prv = lambda j: pl.BlockSpec((8, B_WIDTH), lambda i, j=j: (jnp.maximum(i * (tb // 8) - 1, 0), cb0 + j))
    vec = pl.BlockSpec((1, 128), lambda i: (0, 0))
    wspec = pl.BlockSpec((CONV_K, 3 * B_WIDTH), lambda i: (0, 0))
    return pl.pallas_call(
        body, grid=(t // tb,),
        in_specs=[cur(0), cur(1), cur(2), prv(0), prv(1), prv(2), wspec,
                  pl.BlockSpec((tb, 128), lambda i: (i, COL_G // 128)), vec, vec,
                  pl.BlockSpec((tb, 3 * B_WIDTH), lambda i: (i, 0)), pl.BlockSpec((tb, 128), lambda i: (i, 0))],
        out_specs=[pl.BlockSpec((tb, 3 * B_WIDTH), lambda i: (i, 0)), pl.BlockSpec((tb, 128), lambda i: (i, 0)), wspec, vec, vec],
        out_shape=[jax.ShapeDtypeStruct((t, 3 * B_WIDTH), F32), jax.ShapeDtypeStruct((t, 128), BF16),
                   jax.ShapeDtypeStruct((CONV_K, 3 * B_WIDTH), F32), jax.ShapeDtypeStruct((1, 128), F32), jax.ShapeDtypeStruct((1, 128), F32)],
        compiler_params=_cparams(("arbitrary",)), name="gdn_pre_bwd")(
            proj, proj, proj, proj, proj, proj, conv_w, proj, alog_row, dtb_row, dqkv, dgates)


def _conv_bwd_x(dy, w, cb, tb=512, name="conv_bwd_x"):
    t, c = dy.shape
    nb = t // tb

    def body(dy_ref, nx_ref, w_ref, dx_ref):
        cur = dy_ref[...]
        nxt = jnp.where(pl.program_id(0) == nb - 1, 0.0, nx_ref[...])
        w = w_ref[...]
        dx = cur * w[3:4]
        for j in range(3):
            dx = dx + _shift_rows(cur, nxt, 3 - j, down=False) * w[j:j + 1]
        dx_ref[...] = dx.astype(BF16)

    return pl.pallas_call(
        body, grid=(nb, c // cb),
        in_specs=[pl.BlockSpec((tb, cb), lambda i, j: (i, j)),
                  pl.BlockSpec((8, cb), lambda i, j: (jnp.minimum((i + 1) * (tb // 8), t // 8 - 1), j)),
                  pl.BlockSpec((CONV_K, cb), lambda i, j: (0, j))],
        out_specs=pl.BlockSpec((tb, cb), lambda i, j: (i, j)), out_shape=jax.ShapeDtypeStruct((t, c), BF16),
        compiler_params=_cparams(("parallel", "parallel")), name=name)(dy, dy, w)


def _c_gates(xc, wa_ref, ba_ref, wx_ref, bx_ref, lam_ref, is_row0):
    r = _sigmoid(_dot(xc, wa_ref[...]) + ba_ref[...])
    i = _sigmoid(_dot(xc, wx_ref[...]) + bx_ref[...])
    sp = _softplus(-lam_ref[...])
    log_a = -RG_C * r * sp
    a = jnp.exp(log_a)
    m2 = _neg_expm1(2.0 * log_a)
    mult = jnp.where(is_row0, 1.0, jnp.sqrt(jnp.maximum(m2, EPS)))
    return r, i, sp, log_a, a, m2, mult


def _row0_mask(tb, first):
    ridx = lax.broadcasted_iota(jnp.int32, (tb, C_WIDTH), 0)
    return (ridx == 0) & first


def _c_fwd(proj, conv_w, conv_b, wa, ba, wx, bx, lam, tb=512):
    t = proj.shape[0]
    cbx = COL_C // C_WIDTH

    def body(x_ref, xp_ref, y_ref, w_ref, cb_ref, wa_ref, ba_ref, wx_ref, bx_ref, lam_ref, out_ref, h_ref, a_scr, b_scr, h_scr):
        first = pl.program_id(0) == 0

        @pl.when(first)
        def _():
            h_scr[...] = jnp.zeros_like(h_scr)

        prev = jnp.where(first, 0.0, xp_ref[...])
        xc = _conv_fwd(x_ref[...], prev, w_ref[...]) + cb_ref[...]
        _, i, _, _, a, _, mult = _c_gates(xc, wa_ref, ba_ref, wx_ref, bx_ref, lam_ref, _row0_mask(tb, first))
        a_scr[...] = a
        b_scr[...] = mult * i * xc

        def step(blk, h):
            rows = pl.ds(pl.multiple_of(blk * 8, 8), 8)
            at, bt = a_scr[rows, :], b_scr[rows, :]
            out = []
            for j in range(8):
                h = at[j:j + 1] * h + bt[j:j + 1]
                out.append(h)
            h_ref[rows, :] = jnp.concatenate(out, axis=0)
            return h

        h_scr[...] = lax.fori_loop(0, tb // 8, step, h_scr[...])
        gl, _ = _gelu_tanh(y_ref[...])
        out_ref[...] = (gl * h_ref[...]).astype(BF16)

    vec = pl.BlockSpec((1, C_WIDTH), lambda i: (0, 0))
    mat = pl.BlockSpec((C_WIDTH, C_WIDTH), lambda i: (0, 0))
    row = pl.BlockSpec((tb, C_WIDTH), lambda i: (i, 0))
    return pl.pallas_call(
        body, grid=(t // tb,),
        in_specs=[pl.BlockSpec((tb, C_WIDTH), lambda i: (i, cbx)),
                  pl.BlockSpec((8, C_WIDTH), lambda i: (jnp.maximum(i * (tb // 8) - 1, 0), cbx)),
                  pl.BlockSpec((tb, C_WIDTH), lambda i: (i, cbx + 1)),
                  pl.BlockSpec((CONV_K, C_WIDTH), lambda i: (0, 0)), vec, mat, vec, mat, vec, vec],
        out_specs=[row, row],
        out_shape=[jax.ShapeDtypeStruct((t, C_WIDTH), BF16), jax.ShapeDtypeStruct((t, C_WIDTH), F32)],
        scratch_shapes=[pltpu.VMEM((tb, C_WIDTH), F32), pltpu.VMEM((tb, C_WIDTH), F32), pltpu.VMEM((1, C_WIDTH), F32)],
        compiler_params=_cparams(("arbitrary",)), name="lru_fwd")(proj, proj, proj, conv_w, conv_b, wa, ba, wx, bx, lam)


def _c_bwd(proj, conv_w, conv_b, wa, ba, wx, bx, lam, hs, dmixed, tb=512):
    t = proj.shape[0]
    nb = t // tb
    cbx = COL_C // C_WIDTH

    def body(x_ref, xp_ref, y_ref, w_ref, cb_ref, wa_ref, ba_ref, wx_ref, bx_ref, lam_ref, h_ref, hp_ref, dm_ref,
             dxc_ref, dyg_ref, dw_ref, dcb_ref, dwa_ref, dba_ref, dwx_ref, dbx_ref, dlam_ref, g_scr, a_scr, c_scr):
        step_id = pl.program_id(0)
        first = step_id == nb - 1

        @pl.when(step_id == 0)
        def _():
            c_scr[...] = jnp.zeros_like(c_scr)
            for ref in (dw_ref, dcb_ref, dwa_ref, dba_ref, dwx_ref, dbx_ref, dlam_ref):
                ref[...] = jnp.zeros_like(ref)

        cur = x_ref[...]
        prev = jnp.where(first, 0.0, xp_ref[...])
        w = w_ref[...]
        shifted = [_shift_rows(cur, prev, 3 - j, down=True) for j in range(3)] + [cur]
        xc = shifted[0] * w[0:1] + shifted[1] * w[1:2] + shifted[2] * w[2:3] + shifted[3] * w[3:4] + cb_ref[...]
        row0 = _row0_mask(tb, first)
        r, i, sp, log_a, a, m2, mult = _c_gates(xc, wa_ref, ba_ref, wx_ref, bx_ref, lam_ref, row0)
        h = h_ref[...]
        hprev = _shift_rows(h, jnp.where(first, 0.0, hp_ref[...]), 1, down=True)
        gl, dgl = _gelu_tanh(y_ref[...])
        dm = dm_ref[...].astype(F32)
        dyg_ref[...] = (dm * h * dgl).astype(BF16)
        g_scr[...] = dm * gl
        a_scr[...] = a

        def step(blk, carry):
            b = tb // 8 - 1 - blk
            rows = pl.ds(pl.multiple_of(b * 8, 8), 8)
            at, gt = a_scr[rows, :], g_scr[rows, :]
            out = [None] * 8
            for j in range(7, -1, -1):
                gj = gt[j:j + 1] + carry
                out[j] = gj
                carry = at[j:j + 1] * gj
            g_scr[rows, :] = jnp.concatenate(out, axis=0)
            return carry

        c_scr[...] = lax.fori_loop(0, tb // 8, step, c_scr[...])
        dbx = g_scr[...]
        da = dbx * hprev
        dmult = jnp.where(row0, 0.0, dbx * i * xc)
        di = dbx * mult * xc
        dxc = dbx * mult * i
        dm2 = jnp.where(m2 > EPS, dmult * 0.5 / mult, 0.0)
        dlog_a = da * a - 2.0 * a * a * dm2
        dr = dlog_a * (-RG_C) * sp
        dlam_ref[...] += jnp.sum(dlog_a * (-RG_C) * r, axis=0, keepdims=True) * (-_sigmoid(-lam_ref[...]))
        dpa = dr * r * (1.0 - r)
        dpx = di * i * (1.0 - i)
        dba_ref[...] += jnp.sum(dpa, axis=0, keepdims=True)
        dbx_ref[...] += jnp.sum(dpx, axis=0, keepdims=True)
        dwa_ref[...] += _dot(xc, dpa, "tn")
        dwx_ref[...] += _dot(xc, dpx, "tn")
        dxc = dxc + _dot(dpa, wa_ref[...], "nt") + _dot(dpx, wx_ref[...], "nt")
        dxc_ref[...] = dxc
        dcb_ref[...] += jnp.sum(dxc, axis=0, keepdims=True)
        dw_ref[...] += jnp.concatenate([jnp.sum(shifted[j] * dxc, axis=0, keepdims=True) for j in range(CONV_K)], axis=0)

    vec = pl.BlockSpec((1, C_WIDTH), lambda i: (0, 0))
    mat = pl.BlockSpec((C_WIDTH, C_WIDTH), lambda i: (0, 0))
    cw = pl.BlockSpec((CONV_K, C_WIDTH), lambda i: (0, 0))
    row = lambda j=0: pl.BlockSpec((tb, C_WIDTH), lambda i, j=j: (nb - 1 - i, j))
    halo = lambda j=0: pl.BlockSpec((8, C_WIDTH), lambda i, j=j: (jnp.maximum((nb - 1 - i) * (tb // 8) - 1, 0), j))
    return pl.pallas_call(
        body, grid=(nb,),
        in_specs=[row(cbx), halo(cbx), row(cbx + 1), cw, vec, mat, vec, mat, vec, vec, row(), halo(), row(3)],
        out_specs=[row(), row(), cw, vec, mat, vec, mat, vec, vec],
        out_shape=[jax.ShapeDtypeStruct((t, C_WIDTH), F32), jax.ShapeDtypeStruct((t, C_WIDTH), BF16),
                   jax.ShapeDtypeStruct((CONV_K, C_WIDTH), F32), jax.ShapeDtypeStruct((1, C_WIDTH), F32),
                   jax.ShapeDtypeStruct((C_WIDTH, C_WIDTH), F32), jax.ShapeDtypeStruct((1, C_WIDTH), F32),
                   jax.ShapeDtypeStruct((C_WIDTH, C_WIDTH), F32), jax.ShapeDtypeStruct((1, C_WIDTH), F32),
                   jax.ShapeDtypeStruct((1, C_WIDTH), F32)],
        scratch_shapes=[pltpu.VMEM((tb, C_WIDTH), F32), pltpu.VMEM((tb, C_WIDTH), F32), pltpu.VMEM((1, C_WIDTH), F32)],
        compiler_params=_cparams(("arbitrary",)), name="lru_bwd")(
            proj, proj, proj, conv_w, conv_b, wa, ba, wx, bx, lam, hs, hs, dmixed)


def _mesh_pos():
    return lax.axis_index("x"), lax.axis_index("y"), lax.axis_index("c")


def _all_gather(x, name):
    def body(x_ref, out_ref, send_sems, recv_sems, local_sem):
        mx, my, mc = _mesh_pos()
        me, sibling = (mx, my, mc), (mx, my, 1 - mc)
        chips = [(1 - mx, my), (mx, 1 - my), (1 - mx, 1 - my)]

        def slot(px, py, pc):
            return out_ref.at[4 * px + 2 * py + pc]

        def copy(k, block, to, src=None):
            return pltpu.make_async_remote_copy(
                src_ref=slot(*block) if src is None else src, dst_ref=slot(*block),
                send_sem=send_sems.at[k], recv_sem=recv_sems.at[k], device_id=to, device_id_type=MESH)

        mine = pltpu.make_async_copy(x_ref, slot(*me), local_sem)
        mine.start()
        first = [copy(0, me, sibling, src=x_ref)]
        first += [copy(1 + j, me, (*chip, mc), src=x_ref) for j, chip in enumerate(chips)]
        for cp in first:
            cp.start()
        passed = [copy(4 + j, (*chip, mc), sibling) for j, chip in enumerate(chips)]
        for j, chip in enumerate(chips):
            copy(1 + j, (*chip, mc), me).wait_recv()
            passed[j].start()
        copy(0, sibling, me).wait_recv()
        for j, chip in enumerate(chips):
            copy(4 + j, (*chip, 1 - mc), me).wait_recv()
        for cp in first + passed:
            cp.wait_send()
        mine.wait()

    return pl.pallas_call(
        body, out_shape=jax.ShapeDtypeStruct((N_DEV,) + x.shape, x.dtype),
        in_specs=[pl.BlockSpec(memory_space=pl.ANY)], out_specs=pl.BlockSpec(memory_space=pl.ANY),
        scratch_shapes=[pltpu.SemaphoreType.DMA((7,)), pltpu.SemaphoreType.DMA((7,)), pltpu.SemaphoreType.DMA(())],
        name=name)(x)


def _all_to_all(x, name):
    def body(x_ref, out_ref, send_sems, recv_sems, local_sem):
        mx, my, mc = _mesh_pos()
        me = 4 * mx + 2 * my + mc
        mine = pltpu.make_async_copy(x_ref.at[me], out_ref.at[me], local_sem)
        mine.start()
        copies = []
        for k in range(1, N_DEV):
            px = 1 - mx if k & 4 else mx
            py = 1 - my if k & 2 else my
            pc = 1 - mc if k & 1 else mc
            copies.append(pltpu.make_async_remote_copy(
                src_ref=x_ref.at[4 * px + 2 * py + pc], dst_ref=out_ref.at[me],
                send_sem=send_sems.at[k - 1], recv_sem=recv_sems.at[k - 1], device_id=(px, py, pc), device_id_type=MESH))
        for cp in copies:
            cp.start()
        for cp in copies:
            cp.wait()
        mine.wait()

    return pl.pallas_call(
        body, out_shape=jax.ShapeDtypeStruct(x.shape, x.dtype),
        in_specs=[pl.BlockSpec(memory_space=pl.ANY)], out_specs=pl.BlockSpec(memory_space=pl.ANY),
        scratch_shapes=[pltpu.SemaphoreType.DMA((7,)), pltpu.SemaphoreType.DMA((7,)), pltpu.SemaphoreType.DMA(())],
        name=name)(x)


def _adamw_math(w, g, m, v):
    m = ADAM_B1 * m + (1.0 - ADAM_B1) * g
    v = ADAM_B2 * v + (1.0 - ADAM_B2) * (g * g)
    m_hat = m / (1.0 - ADAM_B1 ** ADAM_STEP)
    v_hat = v / (1.0 - ADAM_B2 ** ADAM_STEP)
    delta = -ADAM_LR * (m_hat / (jnp.sqrt(v_hat) + ADAM_EPS) + ADAM_WD * w)
    return delta, m, v


def _sum_adamw(parts, w, m, v, tr, name):
    p, r, c = parts.shape
    tr = min(tr, r)
    assert r % tr == 0

    def body(p_ref, w_ref, m_ref, v_ref, g_ref, d_ref, nm_ref, nv_ref):
        g = p_ref[0].astype(F32)
        for j in range(1, p):
            g = g + p_ref[j].astype(F32)
        delta, nm, nv = _adamw_math(w_ref[...], g, m_ref[...], v_ref[...])
        g_ref[...] = g
        d_ref[...] = delta
        nm_ref[...] = nm
        nv_ref[...] = nv

    row = pl.BlockSpec((tr, c), lambda i: (i, 0))
    return pl.pallas_call(
        body, grid=(r // tr,), in_specs=[pl.BlockSpec((p, tr, c), lambda i: (0, i, 0)), row, row, row],
        out_specs=[row] * 4, out_shape=[jax.ShapeDtypeStruct((r, c), F32)] * 4,
        compiler_params=_cparams(("parallel",)), name=name)(parts, w, m, v)


def _sum_parts(parts, name):
    p, r, c = parts.shape

    def body(p_ref, o_ref):
        g = p_ref[0]
        for j in range(1, p):
            g = g + p_ref[j]
        o_ref[...] = g

    return pl.pallas_call(body, out_shape=jax.ShapeDtypeStruct((r, c), F32), name=name)(parts)


def _pack(arrs, mult=1024):
    flat = jnp.concatenate([a.reshape(-1).astype(F32) for a in arrs])
    n = flat.shape[0]
    npad = -n % mult
    return jnp.pad(flat, (0, npad)).reshape(-1, 128)


def _unpack(buf, shapes):
    flat = buf.reshape(-1)
    out, off = [], 0
    for s in shapes:
        n = 1
        for d in s:
            n *= d
        out.append(flat[off:off + n].reshape(s))
        off += n
    return out


def _block_diag(w):
    rows = [jnp.pad(w[i], ((0, 0), (i * C_BLOCK_DIM, C_WIDTH - (i + 1) * C_BLOCK_DIM))) for i in range(C_BLOCKS)]
    return jnp.concatenate(rows, axis=0)


def _diag_blocks(m):
    m4 = m.reshape(C_BLOCKS, C_BLOCK_DIM, C_BLOCKS, C_BLOCK_DIM)
    return jnp.stack([m4[i, :, i, :] for i in range(C_BLOCKS)])


def _gate_row(v):
    return jnp.pad(v.astype(F32), (B_HEADS, 128 - 2 * B_HEADS)).reshape(1, 128)


def _permute_w_in(w):
    pad = jnp.zeros(w.shape[:-1] + (D_IN_PAD - D_IN,), w.dtype)
    return jnp.concatenate([w[..., :3072], w[..., 3080:3592], w[..., 3072:3080], pad], axis=-1)


def _unpermute_w_in(w):
    return jnp.concatenate([w[..., :3072], w[..., COL_G:COL_G + 8], w[..., 3072:COL_G]], axis=-1)


_WEIGHTS = ['norm1_g', 'w_in', 'hgrn_lb_logits', 'hgrn_norm_g', 'gdn_conv_w', 'gdn_a_log', 'gdn_dt_bias', 'gdn_norm_g',
            'lru_conv_w', 'lru_conv_b', 'lru_w_a', 'lru_b_a', 'lru_w_x', 'lru_b_x', 'lru_lambda', 'w_out', 'norm2_g',
            'w_up', 'w_down', 'final_norm_g']
_BIG = ('w_in', 'w_out', 'w_up', 'w_down')
_SHARDED_SMALL = ('gdn_conv_w', 'lru_conv_w')


def _step(x, target, w, m, v):
    t = x.shape[0]
    mx, my, mc = _mesh_pos()
    me = 4 * mx + 2 * my + mc

    g_in = _all_gather(w['w_in'].astype(BF16), "ag_w_in")
    g_out = _all_gather(w['w_out'].astype(BF16), "ag_w_out")
    g_up = _all_gather(w['w_up'].astype(BF16), "ag_w_up")
    g_down = _all_gather(w['w_down'].astype(BF16), "ag_w_down")
    w_in = _permute_w_in(jnp.moveaxis(g_in, 0, 2).reshape(DEPTH, D_MODEL, D_IN))
    w_out = jnp.moveaxis(g_out, 0, 1).reshape(DEPTH, D_MODEL, D_MODEL)
    w_up = jnp.moveaxis(g_up, 0, 2).reshape(DEPTH, D_MODEL, D_FF)
    w_down = jnp.moveaxis(g_down, 0, 1).reshape(DEPTH, D_FF, D_MODEL)
    conv_shapes = [w['gdn_conv_w'].shape, w['lru_conv_w'].shape]
    g_conv = _all_gather(_pack([w['gdn_conv_w'], w['lru_conv_w']]), "ag_conv")
    gdn_cw, lru_cw = [], []
    for j in range(N_DEV):
        a, b = _unpack(g_conv[j], conv_shapes)
        gdn_cw.append(a)
        lru_cw.append(b)
    gdn_cw = jnp.concatenate(gdn_cw, axis=-1)
    lru_cw = jnp.concatenate(lru_cw, axis=-1)

    lbnd = _lb_fwd(w['hgrn_lb_logits'])
    row = lambda a: a.reshape(1, -1)

    def c_args(l):
        return (lru_cw[l], row(w['lru_conv_b'][l]), _block_diag(w['lru_w_a'][l]), row(w['lru_b_a'][l]),
                _block_diag(w['lru_w_x'][l]), row(w['lru_b_x'][l]), row(w['lru_lambda'][l]))

    saved = []
    xl = x
    h = _rms_fwd(x, w['norm1_g'][0], name="rms_fwd")
    for l in range(DEPTH):
        proj = _mm_rows(h, w_in[l], "nn", 256, "mm_proj")
        mix_a, st_a = _a_fwd(proj, lbnd[l], w['hgrn_norm_g'][l])
        alr, dtr = _gate_row(w['gdn_a_log'][l]), _gate_row(w['gdn_dt_bias'][l])
        qkv, gates = _b_pre_fwd(proj, gdn_cw[l], alr, dtr)
        mix_b, st_b = _b_fwd(qkv, gates, proj, w['gdn_norm_g'][l])
        mix_c, hs = _c_fwd(proj, *c_args(l))
        mixed = jnp.concatenate([mix_a, mix_b, mix_c], axis=1)
        x_mid, h2 = _mm_rows(mixed, w_out[l], "nn", 1024, "mm_out", residual=xl, epilogue="rms_fwd", norm=w['norm2_g'][l])
        act, up = _mm_rows(h2, w_up[l], "nn", 256, "mm_up", epilogue="relu2")
        saved.append(dict(x=xl, h=h, proj=proj, st_a=st_a, qkv=qkv, gates=gates, st_b=st_b, hs=hs, mixed=mixed,
                          x_mid=x_mid, h2=h2, up=up, act=act, alr=alr, dtr=dtr))
        if l + 1 < DEPTH:
            xl, h = _mm_rows(act, w_down[l], "nn", 512, "mm_down", residual=x_mid, epilogue="rms_fwd", norm=w['norm1_g'][l + 1])
        else:
            xl = _mm_rows(act, w_down[l], "nn", 512, "mm_down_last", residual=x_mid)
    loss, dx, dgf = _loss_head(xl, w['final_norm_g'], target)

    gs = {n: [None] * DEPTH for n in _WEIGHTS}
    for l in reversed(range(DEPTH)):
        s = saved[l]
        dup = _mm_rows(dx, w_down[l], "nt", 256, "mm_dact", epilogue="drelu2", up=s['up'])
        gs['w_down'][l] = _mm_tn(s['act'], dx, 512, 512, "mm_dw_down").reshape(N_DEV, D_FF // N_DEV, D_MODEL)
        dx_mid, dg2 = _mm_rows(dup, w_up[l], "nt", 512, "mm_dh2", epilogue="rms_bwd", norm=(s['x_mid'], w['norm2_g'][l], dx))
        gs['w_up'][l] = _mm_tn(s['h2'], dup, D_FF // N_DEV, 1024, "mm_dw_up", split=True)
        gs['norm2_g'][l] = dg2[0]
        dmixed = _mm_rows(dx_mid, w_out[l], "nt", 1024, "mm_dmixed")
        gs['w_out'][l] = _mm_tn(s['mixed'], dx_mid, 1024, 1024, "mm_dw_out").reshape(N_DEV, D_MODEL // N_DEV, D_MODEL)
        proj = s['proj']
        dpa, dlb, dnga = _a_bwd(proj, lbnd[l], w['hgrn_norm_g'][l], s['st_a'], dmixed)
        gs['hgrn_lb_logits'][l] = dlb[0]
        gs['hgrn_norm_g'][l] = dnga[0]
        dqkv, dgates, dz, dngb = _b_bwd(s['qkv'], s['gates'], proj, w['gdn_norm_g'][l], s['st_b'], dmixed)
        dyb, dgi, dcwb, dal, ddt = _b_pre_bwd(proj, gdn_cw[l], s['alr'], s['dtr'], dqkv, dgates)
        dxb = _conv_bwd_x(dyb, gdn_cw[l], B_WIDTH, name="conv_bwd_x_gdn")
        gs['gdn_norm_g'][l] = dngb[0]
        gs['gdn_conv_w'][l] = dcwb
        gs['gdn_a_log'][l] = dal[0, B_HEADS:2 * B_HEADS]
        gs['gdn_dt_bias'][l] = ddt[0, B_HEADS:2 * B_HEADS]
        dxc, dyg, dcwc, dcb, dwa, dba, dwx, dbx, dlam = _c_bwd(proj, *c_args(l), s['hs'], dmixed)
        dxc_in = _conv_bwd_x(dxc, lru_cw[l], C_WIDTH, name="conv_bwd_x_lru")
        gs['lru_conv_w'][l] = dcwc
        gs['lru_conv_b'][l] = dcb[0]
        gs['lru_w_a'][l] = _diag_blocks(dwa)
        gs['lru_b_a'][l] = dba[0]
        gs['lru_w_x'][l] = _diag_blocks(dwx)
        gs['lru_b_x'][l] = dbx[0]
        gs['lru_lambda'][l] = dlam[0]
        pad = jnp.zeros((t, D_IN_PAD - COL_G - 128), BF16)
        dproj = jnp.concatenate([dpa, dxb, dz, dxc_in, dyg, dgi, pad], axis=1)
        dx, dg1 = _mm_rows(dproj, w_in[l], "nt", 512, "mm_dh", epilogue="rms_bwd", norm=(s['x'], w['norm1_g'][l], dx_mid))
        dw_in = _unpermute_w_in(_mm_tn(s['h'], dproj, 1280, 1024, "mm_dw_in"))
        gs['w_in'][l] = jnp.moveaxis(dw_in.reshape(D_MODEL, N_DEV, D_IN // N_DEV), 1, 0)
        gs['norm1_g'][l] = dg1[0]
    grad_x = dx
    part = {n: jnp.stack(gs[n]) for n in _WEIGHTS if n != 'final_norm_g' and n not in _BIG}
    part['final_norm_g'] = dgf[0]
    part['hgrn_lb_logits'] = _lb_bwd(w['hgrn_lb_logits'], part['hgrn_lb_logits'])

    grads, deltas, new_m, new_v = {}, {}, {}, {}
    for n in _BIG:
        recv = _all_to_all(jnp.stack(gs[n], axis=1), "a2a_" + n)
        shp = w[n].shape
        r2 = lambda a: a.reshape(-1, shp[-1])
        g, d, nm, nv = _sum_adamw(recv.reshape(N_DEV, -1, shp[-1]), r2(w[n]), r2(m[n]), r2(v[n]), 256, "adamw_" + n)
        grads[n], deltas[n], new_m[n], new_v[n] = (a.reshape(shp) for a in (g, d, nm, nv))

    small = [n for n in _WEIGHTS if n not in _BIG]
    packed = _pack([part[n] for n in small] + [loss])
    total = _sum_parts(_all_gather(packed, "ag_small"), "sum_small")
    summed = _unpack(total, [part[n].shape for n in small] + [(1, 1)])
    loss_total = summed[-1].reshape(())
    gsmall = dict(zip(small, summed[:-1]))
    for n in _SHARDED_SMALL:
        width = w[n].shape[-1]
        gsmall[n] = lax.dynamic_slice_in_dim(gsmall[n], me * width, width, axis=2)
    pk = lambda d: _pack([d[n] for n in small])
    _, d, nm, nv = _sum_adamw(pk(gsmall)[None], pk(w), pk(m), pk(v), 4096, "adamw_small")
    shapes = [w[n].shape for n in small]
    for n, dd, mm, vv in zip(small, _unpack(d, shapes), _unpack(nm, shapes), _unpack(nv, shapes)):
        grads[n], deltas[n], new_m[n], new_v[n] = gsmall[n], dd, mm, vv
    return loss_total, grad_x, grads, deltas, new_m, new_v


def kernel(x, norm1_g, w_in, hgrn_lb_logits, hgrn_norm_g, gdn_conv_w, gdn_a_log, gdn_dt_bias, gdn_norm_g, lru_conv_w, lru_conv_b, lru_w_a, lru_b_a, lru_w_x, lru_b_x, lru_lambda, w_out, norm2_g, w_up, w_down, final_norm_g, loss_target, m_norm1_g, m_w_in, m_hgrn_lb_logits, m_hgrn_norm_g, m_gdn_conv_w, m_gdn_a_log, m_gdn_dt_bias, m_gdn_norm_g, m_lru_conv_w, m_lru_conv_b, m_lru_w_a, m_lru_b_a, m_lru_w_x, m_lru_b_x, m_lru_lambda, m_w_out, m_norm2_g, m_w_up, m_w_down, m_final_norm_g, v_norm1_g, v_w_in, v_hgrn_lb_logits, v_hgrn_norm_g, v_gdn_conv_w, v_gdn_a_log, v_gdn_dt_bias, v_gdn_norm_g, v_lru_conv_w, v_lru_conv_b, v_lru_w_a, v_lru_b_a, v_lru_w_x, v_lru_b_x, v_lru_lambda, v_w_out, v_norm2_g, v_w_up, v_w_down, v_final_norm_g):
    w = dict(zip(_WEIGHTS, (norm1_g, w_in, hgrn_lb_logits, hgrn_norm_g, gdn_conv_w, gdn_a_log, gdn_dt_bias, gdn_norm_g, lru_conv_w, lru_conv_b, lru_w_a, lru_b_a, lru_w_x, lru_b_x, lru_lambda, w_out, norm2_g, w_up, w_down, final_norm_g)))
    m = dict(zip(_WEIGHTS, (m_norm1_g, m_w_in, m_hgrn_lb_logits, m_hgrn_norm_g, m_gdn_conv_w, m_gdn_a_log, m_gdn_dt_bias, m_gdn_norm_g, m_lru_conv_w, m_lru_conv_b, m_lru_w_a, m_lru_b_a, m_lru_w_x, m_lru_b_x, m_lru_lambda, m_w_out, m_norm2_g, m_w_up, m_w_down, m_final_norm_g)))
    v = dict(zip(_WEIGHTS, (v_norm1_g, v_w_in, v_hgrn_lb_logits, v_hgrn_norm_g, v_gdn_conv_w, v_gdn_a_log, v_gdn_dt_bias, v_gdn_norm_g, v_lru_conv_w, v_lru_conv_b, v_lru_w_a, v_lru_b_a, v_lru_w_x, v_lru_b_x, v_lru_lambda, v_w_out, v_norm2_g, v_w_up, v_w_down, v_final_norm_g)))
    loss, grad_x, grads, deltas, new_m, new_v = _step(x[0], loss_target[0], w, m, v)
    return (loss, grad_x[None], *[grads[n] for n in _WEIGHTS], *[deltas[n] for n in _WEIGHTS],
            *[new_m[n] for n in _WEIGHTS], *[new_v[n] for n in _WEIGHTS])
```

```python
import functools

import jax
import jax.numpy as jnp
from jax import lax
from jax.experimental import pallas as pl
from jax.experimental.pallas import tpu as pltpu

F32 = jnp.float32
BF16 = jnp.bfloat16
HI = lax.Precision.HIGHEST
MESH = pl.DeviceIdType.MESH

N_DEV = 8
D_MODEL = 1024
DEPTH = 4
A_HEADS, A_DIM, A_WIDTH = 4, 64, 256
B_HEADS, B_DIM, B_WIDTH = 4, 128, 512
C_WIDTH, C_BLOCKS, C_BLOCK_DIM = 256, 4, 64
D_IN = 3592
D_IN_PAD = 3840
COL_A, COL_B, COL_C, COL_G = 0, 1024, 3072, 3584
D_FF = 4096
CONV_K = 4
CHUNK = 64
SUB = 16
RG_C = 8.0
EPS = 1e-6
TINY = 1e-30
EXP_CLAMP = 80.0
GDN_SCALE = B_DIM ** -0.5
ADAM_LR, ADAM_B1, ADAM_B2, ADAM_EPS, ADAM_WD, ADAM_STEP = 0.001, 0.9, 0.999, 1e-08, 0.01, 10
VMEM_LIMIT = 56 * 1024 * 1024


def _cparams(sem=None):
    return pltpu.CompilerParams(dimension_semantics=sem, vmem_limit_bytes=VMEM_LIMIT)


_DIMS = {"nn": (((1,), (0,)), ((), ())), "nt": (((1,), (1,)), ((), ())), "tn": (((0,), (0,)), ((), ()))}


def _split_bf16(x):
    hi = x.astype(BF16)
    return hi, (x - hi.astype(F32)).astype(BF16)


def _dot(a, b, mode="nn", hi=False):
    if not hi:
        return lax.dot_general(a.astype(BF16), b.astype(BF16), _DIMS[mode], preferred_element_type=F32)
    ah, al = _split_bf16(a.astype(F32))
    bh, bl = _split_bf16(b.astype(F32))
    ka = 0 if mode == "tn" else 1
    kb = 1 if mode == "nt" else 0
    return lax.dot_general(jnp.concatenate([ah, ah, al], axis=ka), jnp.concatenate([bh, bl, bh], axis=kb),
                           _DIMS[mode], preferred_element_type=F32)


def _dot_exact_lhs(lhs, x, mode="nn"):
    l_bf16 = lhs.astype(BF16)
    x1 = x.astype(BF16)
    r1 = x - x1.astype(F32)
    x2 = r1.astype(BF16)
    x3 = (r1 - x2.astype(F32)).astype(BF16)
    ka = 0 if mode == "tn" else 1
    return lax.dot_general(jnp.concatenate([l_bf16] * 3, axis=ka), jnp.concatenate([x1, x2, x3], axis=0),
                           _DIMS[mode], preferred_element_type=F32)


def _iota2(n, m):
    return lax.broadcasted_iota(jnp.int32, (n, m), 0), lax.broadcasted_iota(jnp.int32, (n, m), 1)


def _tril(n, strict=False):
    r, c = _iota2(n, n)
    return (r > c) if strict else (r >= c)


def _sigmoid(x):
    return 1.0 / (1.0 + jnp.exp(-x))


def _softplus(x):
    return jnp.maximum(x, 0.0) + jnp.log(1.0 + jnp.exp(-jnp.abs(x)))


def _neg_expm1(z):
    series = -z * (1.0 + z * (0.5 + z * (1.0 / 6.0)))
    return jnp.where(z > -1e-2, series, 1.0 - jnp.exp(z))


def _gelu_tanh(x):
    c = 0.7978845608028654
    u = c * (x + 0.044715 * x * x * x)
    t = jnp.tanh(u)
    g = 0.5 * x * (1.0 + t)
    dg = 0.5 * (1.0 + t) + 0.5 * x * (1.0 - t * t) * c * (1.0 + 3.0 * 0.044715 * x * x)
    return g, dg


def _shift_rows(cur, halo, s, down=True):
    n = cur.shape[0]
    ridx = lax.broadcasted_iota(jnp.int32, (8, cur.shape[1]), 0)
    if down:
        main = pltpu.roll(cur, s, 0)
        fix = jnp.where(ridx < s, pltpu.roll(halo, s, 0), main[0:8])
        return jnp.concatenate([fix, main[8:]], axis=0)
    main = pltpu.roll(cur, n - s, 0)
    fix = jnp.where(ridx >= 8 - s, pltpu.roll(halo, 8 - s, 0), main[n - 8:n])
    return jnp.concatenate([main[:n - 8], fix], axis=0)


def _conv_fwd(cur, prev8, w):
    y = cur * w[3:4]
    for j in range(3):
        y = y + _shift_rows(cur, prev8, 3 - j, down=True) * w[j:j + 1]
    return y


def _pick_tile(n, pref):
    best = None
    for cand in range(128, min(n, pref) + 1, 128):
        if n % cand == 0:
            best = cand
    return best if best is not None else n


def _matmul(a, b, mode, out_dtype=F32, residual=None, tm=512, tn=1024, tk=1024, name="matmul"):
    if mode == "nn":
        (m, k), n = a.shape, b.shape[1]
    elif mode == "nt":
        (m, k), n = a.shape, b.shape[0]
    else:
        (k, m), n = a.shape, b.shape[1]
    tm, tn, tk = _pick_tile(m, tm), _pick_tile(n, tn), _pick_tile(k, tk)
    nk = k // tk
    a_spec = pl.BlockSpec((tk, tm), lambda i, j, kk: (kk, i)) if mode == "tn" else pl.BlockSpec((tm, tk), lambda i, j, kk: (i, kk))
    b_spec = pl.BlockSpec((tn, tk), lambda i, j, kk: (j, kk)) if mode == "nt" else pl.BlockSpec((tk, tn), lambda i, j, kk: (kk, j))
    o_spec = pl.BlockSpec((tm, tn), lambda i, j, kk: (i, j))
    has_res = residual is not None

    def body(*refs):
        if has_res:
            a_ref, b_ref, r_ref, o_ref, acc = refs
        else:
            a_ref, b_ref, o_ref, acc = refs
        kk = pl.program_id(2)

        @pl.when(kk == 0)
        def _():
            acc[...] = jnp.zeros_like(acc)

        acc[...] += _dot(a_ref[...], b_ref[...], mode)

        @pl.when(kk == nk - 1)
        def _():
            r = acc[...]
            if has_res:
                r = r + r_ref[...]
            o_ref[...] = r.astype(out_dtype)

    ins = [a, b] + ([residual] if has_res else [])
    specs = [a_spec, b_spec] + ([o_spec] if has_res else [])
    return pl.pallas_call(
        body, grid=(m // tm, n // tn, nk), in_specs=specs, out_specs=o_spec,
        out_shape=jax.ShapeDtypeStruct((m, n), out_dtype), scratch_shapes=[pltpu.VMEM((tm, tn), F32)],
        compiler_params=_cparams(("parallel", "parallel", "arbitrary")), name=name)(*ins)


def _mm_rows(a, w, mode, tm, name, residual=None, epilogue=None, up=None, norm=None):
    t, k = a.shape
    n = w.shape[1] if mode == "nn" else w.shape[0]
    tm = min(tm, t)
    assert t % tm == 0

    def body(*refs):
        a_ref, w_ref = refs[0], refs[1]
        y = _dot(a_ref[...], w_ref[...], mode)
        if residual is not None:
            y = y + refs[2][...]
        if epilogue == "relu2":
            r = jnp.maximum(y, 0.0)
            refs[-2][...] = (r * r).astype(BF16)
            refs[-1][...] = y.astype(BF16)
        elif epilogue == "drelu2":
            refs[-1][...] = (y * 2.0 * jnp.maximum(refs[2][...].astype(F32), 0.0)).astype(BF16)
        elif epilogue == "rms_fwd":
            rinv = lax.rsqrt(jnp.mean(y * y, axis=-1, keepdims=True) + EPS)
            refs[-2][...] = y
            refs[-1][...] = (y * rinv * refs[-3][...]).astype(BF16)
        elif epilogue == "rms_bwd":
            x_ref, g_ref, dres_ref, dx_ref, dg_ref = refs[2:]

            @pl.when(pl.program_id(0) == 0)
            def _():
                dg_ref[...] = jnp.zeros_like(dg_ref)

            xv = x_ref[...]
            rinv = lax.rsqrt(jnp.mean(xv * xv, axis=-1, keepdims=True) + EPS)
            xhat = xv * rinv
            dxh = y * g_ref[...]
            dx_ref[...] = dres_ref[...] + rinv * (dxh - xhat * jnp.mean(dxh * xhat, axis=-1, keepdims=True))
            dg_ref[...] += jnp.sum(y * xhat, axis=0, keepdims=True)
        else:
            refs[-1][...] = y

    rows = lambda width: pl.BlockSpec((tm, width), lambda i: (i, 0))
    vec = pl.BlockSpec((1, n), lambda i: (0, 0))
    ins, specs = [a, w], [rows(k), pl.BlockSpec(w.shape, lambda i: (0, 0))]
    if residual is not None:
        ins.append(residual)
        specs.append(rows(n))
    if epilogue == "drelu2":
        ins.append(up)
        specs.append(rows(n))
    if epilogue == "rms_fwd":
        ins.append(norm.reshape(1, n))
        specs.append(vec)
        out_specs, out_shape = [rows(n), rows(n)], [jax.ShapeDtypeStruct((t, n), F32), jax.ShapeDtypeStruct((t, n), BF16)]
    elif epilogue == "rms_bwd":
        ins += [norm[0], norm[1].reshape(1, n), norm[2]]
        specs += [rows(n), vec, rows(n)]
        out_specs, out_shape = [rows(n), vec], [jax.ShapeDtypeStruct((t, n), F32), jax.ShapeDtypeStruct((1, n), F32)]
    elif epilogue == "relu2":
        out_specs, out_shape = [rows(n), rows(n)], [jax.ShapeDtypeStruct((t, n), BF16)] * 2
    else:
        out_specs, out_shape = rows(n), jax.ShapeDtypeStruct((t, n), BF16 if epilogue == "drelu2" else F32)
    return pl.pallas_call(body, grid=(t // tm,), in_specs=specs, out_specs=out_specs, out_shape=out_shape,
                          compiler_params=_cparams(("arbitrary" if epilogue == "rms_bwd" else "parallel",)), name=name)(*ins)


def _mm_tn(a, b, tn, tk, name, split=False):
    t, m = a.shape
    n = b.shape[1]
    tn, tk = min(tn, n), min(tk, t)
    assert n % tn == 0 and t % tk == 0
    nk = t // tk

    def body(a_ref, b_ref, o_ref, acc):
        kk = pl.program_id(1)

        @pl.when(kk == 0)
        def _():
            acc[...] = jnp.zeros_like(acc)

        acc[...] += _dot(a_ref[...], b_ref[...], "tn")

        @pl.when(kk == nk - 1)
        def _():
            o_ref[...] = acc[...].astype(BF16)

    if split:
        out_spec, out_shape = pl.BlockSpec((None, m, tn), lambda j, kk: (j, 0, 0)), jax.ShapeDtypeStruct((n // tn, m, tn), BF16)
    else:
        out_spec, out_shape = pl.BlockSpec((m, tn), lambda j, kk: (0, j)), jax.ShapeDtypeStruct((m, n), BF16)
    return pl.pallas_call(
        body, grid=(n // tn, nk),
        in_specs=[pl.BlockSpec((tk, m), lambda j, kk: (kk, 0)), pl.BlockSpec((tk, tn), lambda j, kk: (kk, j))],
        out_specs=out_spec, out_shape=out_shape, scratch_shapes=[pltpu.VMEM((m, tn), F32)],
        compiler_params=_cparams(("parallel", "arbitrary")), name=name)(a, b)


def _rms_fwd(x, g, tb=512, name="rms_fwd"):
    t, d = x.shape

    def body(x_ref, g_ref, h_ref):
        xv = x_ref[...]
        rinv = lax.rsqrt(jnp.mean(xv * xv, axis=-1, keepdims=True) + EPS)
        h_ref[...] = (xv * rinv * g_ref[...]).astype(BF16)

    return pl.pallas_call(
        body, grid=(t // tb,), in_specs=[pl.BlockSpec((tb, d), lambda i: (i, 0)), pl.BlockSpec((1, d), lambda i: (0, 0))],
        out_specs=pl.BlockSpec((tb, d), lambda i: (i, 0)), out_shape=jax.ShapeDtypeStruct((t, d), BF16),
        compiler_params=_cparams(("parallel",)), name=name)(x, g.reshape(1, d))


def _rms_bwd(dh, x, g, dres, tb=512, name="rms_bwd"):
    t, d = x.shape

    def body(dh_ref, x_ref, g_ref, dres_ref, dx_ref, dg_ref):
        @pl.when(pl.program_id(0) == 0)
        def _():
            dg_ref[...] = jnp.zeros_like(dg_ref)

        xv = x_ref[...]
        dhv = dh_ref[...].astype(F32)
        rinv = lax.rsqrt(jnp.mean(xv * xv, axis=-1, keepdims=True) + EPS)
        xhat = xv * rinv
        dxh = dhv * g_ref[...]
        dx_ref[...] = dres_ref[...] + rinv * (dxh - xhat * jnp.mean(dxh * xhat, axis=-1, keepdims=True))
        dg_ref[...] += jnp.sum(dhv * xhat, axis=0, keepdims=True)

    row = pl.BlockSpec((tb, d), lambda i: (i, 0))
    vec = pl.BlockSpec((1, d), lambda i: (0, 0))
    return pl.pallas_call(
        body, grid=(t // tb,), in_specs=[row, row, vec, row], out_specs=[row, vec],
        out_shape=[jax.ShapeDtypeStruct((t, d), F32), jax.ShapeDtypeStruct((1, d), F32)],
        compiler_params=_cparams(("arbitrary",)), name=name)(dh, x, g.reshape(1, d), dres)


def _loss_head(x, g, target, tb=512):
    t, d = x.shape

    def body(x_ref, g_ref, t_ref, loss_ref, dx_ref, dg_ref):
        @pl.when(pl.program_id(0) == 0)
        def _():
            dg_ref[...] = jnp.zeros_like(dg_ref)
            loss_ref[...] = jnp.zeros_like(loss_ref)

        xv = x_ref[...]
        rinv = lax.rsqrt(jnp.mean(xv * xv, axis=-1, keepdims=True) + EPS)
        xhat = xv * rinv
        err = xhat * g_ref[...] - t_ref[...]
        loss_ref[...] += 0.5 * jnp.sum(jnp.mean(err * err, axis=-1, keepdims=True), axis=0, keepdims=True)
        dy = err * (1.0 / d)
        dxh = dy * g_ref[...]
        dx_ref[...] = rinv * (dxh - xhat * jnp.mean(dxh * xhat, axis=-1, keepdims=True))
        dg_ref[...] += jnp.sum(dy * xhat, axis=0, keepdims=True)

    row = pl.BlockSpec((tb, d), lambda i: (i, 0))
    vec = pl.BlockSpec((1, d), lambda i: (0, 0))
    one = pl.BlockSpec((1, 1), lambda i: (0, 0))
    return pl.pallas_call(
        body, grid=(t // tb,), in_specs=[row, vec, row], out_specs=[one, row, vec],
        out_shape=[jax.ShapeDtypeStruct((1, 1), F32), jax.ShapeDtypeStruct((t, d), F32), jax.ShapeDtypeStruct((1, d), F32)],
        compiler_params=_cparams(("arbitrary",)), name="loss_head")(x, g.reshape(1, d), target)


def _relu2_fwd(up, tb=512):
    t, d = up.shape

    def body(u_ref, a_ref):
        r = jnp.maximum(u_ref[...], 0.0)
        a_ref[...] = (r * r).astype(BF16)

    row = pl.BlockSpec((tb, d), lambda i: (i, 0))
    return pl.pallas_call(body, grid=(t // tb,), in_specs=[row], out_specs=row, out_shape=jax.ShapeDtypeStruct((t, d), BF16),
                          compiler_params=_cparams(("parallel",)), name="relu2_fwd")(up)


def _relu2_bwd(dact, up, tb=512):
    t, d = up.shape

    def body(da_ref, u_ref, o_ref):
        o_ref[...] = (da_ref[...] * 2.0 * jnp.maximum(u_ref[...], 0.0)).astype(BF16)

    row = pl.BlockSpec((tb, d), lambda i: (i, 0))
    return pl.pallas_call(body, grid=(t // tb,), in_specs=[row, row], out_specs=row, out_shape=jax.ShapeDtypeStruct((t, d), BF16),
                          compiler_params=_cparams(("parallel",)), name="relu2_bwd")(dact, up)


def _lb_fwd(logits):
    def body(l_ref, o_ref):
        lg = l_ref[...]
        e = jnp.exp(lg - jnp.max(lg, axis=0, keepdims=True))
        p = e / jnp.sum(e, axis=0, keepdims=True)
        c = jnp.zeros_like(p[0:1])
        rows = [c]
        for l in range(1, DEPTH):
            c = c + p[l:l + 1]
            rows.append(c)
        o_ref[...] = jnp.minimum(jnp.maximum(jnp.concatenate(rows, axis=0), 0.0), 1.0 - EPS)

    return pl.pallas_call(body, out_shape=jax.ShapeDtypeStruct(logits.shape, F32), name="lb_fwd")(logits)


def _lb_bwd(logits, dlb):
    def body(l_ref, d_ref, o_ref):
        lg = l_ref[...]
        e = jnp.exp(lg - jnp.max(lg, axis=0, keepdims=True))
        p = e / jnp.sum(e, axis=0, keepdims=True)
        hi = 1.0 - EPS
        c = jnp.zeros_like(p[0:1])
        dc = []
        for l in range(1, DEPTH):
            c = c + p[l:l + 1]
            gl = jnp.where(c < 0.0, 0.0, jnp.where(c == 0.0, 0.5, 1.0)) * jnp.where(c > hi, 0.0, jnp.where(c == hi, 0.5, 1.0))
            dc.append(d_ref[l:l + 1, :] * gl)
        dp = [jnp.zeros_like(c)]
        for j in range(1, DEPTH):
            s = dc[j - 1]
            for l in range(j + 1, DEPTH):
                s = s + dc[l - 1]
            dp.append(s)
        dpm = jnp.concatenate(dp, axis=0)
        o_ref[...] = p * (dpm - jnp.sum(p * dpm, axis=0, keepdims=True))

    return pl.pallas_call(body, out_shape=jax.ShapeDtypeStruct(logits.shape, F32), name="lb_bwd")(logits, dlb)


def _a_gates(qi, fi, lbh):
    sq = _sigmoid(qi)
    q = qi * sq
    sg = _sigmoid(fi)
    sgn = _sigmoid(-fi)
    f = lbh + (1.0 - lbh) * sg
    logf = jnp.log(jnp.maximum(f, TINY))
    k = (1.0 - lbh) * sgn
    return q, sq, sg, sgn, f, logf, k


def _a_intra(q, k, cum):
    qts, kes, rows = [], [], []
    for i in range(CHUNK // SUB):
        lo = i * SUB
        r = cum[lo - 1:lo] if i > 0 else jnp.zeros_like(cum[0:1])
        eq = jnp.exp(cum[lo:lo + SUB] - r)
        ek = jnp.exp(jnp.minimum(r - cum, EXP_CLAMP))
        qt = q[lo:lo + SUB] * eq
        rows.append(_dot(qt, k * ek, "nt", hi=True))
        qts.append((qt, eq))
        kes.append(ek)
    attn = jnp.where(_tril(CHUNK), jnp.concatenate(rows, axis=0), 0.0)
    return attn, qts, kes


def _a_intra_bwd(dattn, k, qts, kes):
    dq_rows = []
    dk = jnp.zeros_like(k)
    for i in range(CHUNK // SUB):
        lo = i * SUB
        da = dattn[lo:lo + SUB]
        qt, eq = qts[i]
        dq_rows.append(_dot(da, k * kes[i], "nn", hi=True) * eq)
        dk = dk + _dot(da, qt, "tn", hi=True) * kes[i]
    return jnp.concatenate(dq_rows, axis=0), dk


def _headnorm_fwd(o, g, gate_in):
    rinv = lax.rsqrt(jnp.mean(o * o, axis=-1, keepdims=True) + EPS)
    sg = _sigmoid(gate_in)
    return o * rinv * g * (gate_in * sg)


def _headnorm_bwd(dout, o, g, gate_in):
    rinv = lax.rsqrt(jnp.mean(o * o, axis=-1, keepdims=True) + EPS)
    xhat = o * rinv
    sg = _sigmoid(gate_in)
    silu = gate_in * sg
    dy = dout * silu
    dgate = dout * xhat * g * (sg * (1.0 + gate_in * (1.0 - sg)))
    dxh = dy * g
    do = rinv * (dxh - xhat * jnp.mean(dxh * xhat, axis=-1, keepdims=True))
    return do, dgate, jnp.sum(dy * xhat, axis=0, keepdims=True)


def _a_fwd(proj, lb, norm_g, tb=256):
    t = proj.shape[0]
    nch = tb // CHUNK

    def body(q_ref, f_ref, i_ref, g_ref, lb_ref, ng_ref, out_ref, st_ref, s_scr):
        @pl.when(pl.program_id(0) == 0)
        def _():
            s_scr[...] = jnp.zeros_like(s_scr)

        ltri = _tril(CHUNK).astype(F32)

        def chunk(c, carry):
            rows = pl.ds(pl.multiple_of(c * CHUNK, CHUNK), CHUNK)
            hs = range(A_HEADS)
            cols = [slice(h * A_DIM, (h + 1) * A_DIM) for h in hs]
            gates = [_a_gates(q_ref[rows, cols[h]], f_ref[rows, cols[h]], lb_ref[:, cols[h]]) for h in hs]
            q, k = [gates[h][0] for h in hs], [gates[h][6] for h in hs]
            v = [i_ref[rows, cols[h]] for h in hs]
            cum = [_dot_exact_lhs(ltri, gates[h][5]) for h in hs]
            cl = [cum[h][CHUNK - 1:CHUNK] for h in hs]
            s0 = [s_scr[h] for h in hs]
            for h in hs:
                st_ref[c, h] = s0[h]
            attn = [_a_intra(q[h], k[h], cum[h])[0] for h in hs]
            qs0 = [_dot(q[h] * jnp.exp(cum[h]), s0[h]) for h in hs]
            o = [qs0[h] + _dot(attn[h], v[h]) for h in hs]
            kd = [k[h] * jnp.exp(cl[h] - cum[h]) for h in hs]
            for h in hs:
                s_scr[h] = s0[h] * jnp.exp(cl[h]).T + _dot(kd[h], v[h], "tn")
            outs = [_headnorm_fwd(o[h], ng_ref[...], g_ref[rows, cols[h]]) for h in hs]
            out_ref[rows, :] = jnp.concatenate(outs, axis=1).astype(BF16)
            return carry

        lax.fori_loop(0, nch, chunk, 0, unroll=2)

    colblk = lambda j: pl.BlockSpec((tb, A_WIDTH), lambda i, j=j: (i, j))
    return pl.pallas_call(
        body, grid=(t // tb,),
        in_specs=[colblk(0), colblk(1), colblk(2), colblk(3), pl.BlockSpec((1, A_WIDTH), lambda i: (0, 0)),
                  pl.BlockSpec((1, A_DIM), lambda i: (0, 0))],
        out_specs=[pl.BlockSpec((tb, A_WIDTH), lambda i: (i, 0)),
                   pl.BlockSpec((nch, A_HEADS, A_DIM, A_DIM), lambda i: (i, 0, 0, 0))],
        out_shape=[jax.ShapeDtypeStruct((t, A_WIDTH), BF16), jax.ShapeDtypeStruct((t // CHUNK, A_HEADS, A_DIM, A_DIM), F32)],
        scratch_shapes=[pltpu.VMEM((A_HEADS, A_DIM, A_DIM), F32)],
        compiler_params=_cparams(("arbitrary",)), name="hgrn_fwd")(proj, proj, proj, proj, lb.reshape(1, A_WIDTH), norm_g.reshape(1, A_DIM))


def _a_bwd(proj, lb, norm_g, states, dmixed, beside=None, tb=256):
    t = proj.shape[0]
    nch = tb // CHUNK
    nb = t // tb

    def body(q_ref, f_ref, i_ref, g_ref, lb_ref, ng_ref, st_ref, dm_ref, dp_ref, dlb_ref, dng_ref, ds_scr):
        @pl.when(pl.program_id(0) == 0)
        def _():
            ds_scr[...] = jnp.zeros_like(ds_scr)
            dlb_ref[...] = jnp.zeros_like(dlb_ref)
            dng_ref[...] = jnp.zeros_like(dng_ref)

        ltri = _tril(CHUNK).astype(F32)
        mask = _tril(CHUNK)

        def chunk(cc, carry):
            c = nch - 1 - cc
            rows = pl.ds(pl.multiple_of(c * CHUNK, CHUNK), CHUNK)
            hs = range(A_HEADS)
            cols = [slice(h * A_DIM, (h + 1) * A_DIM) for h in hs]
            qi = [q_ref[rows, cols[h]] for h in hs]
            gi = [g_ref[rows, cols[h]] for h in hs]
            lbh = [lb_ref[:, cols[h]] for h in hs]
            gates = [_a_gates(qi[h], f_ref[rows, cols[h]], lbh[h]) for h in hs]
            q, sq, sg, sgn, f, logf, k = ([gates[h][j] for h in hs] for j in range(7))
            v = [i_ref[rows, cols[h]] for h in hs]
            cum = [_dot_exact_lhs(ltri, logf[h]) for h in hs]
            cl = [cum[h][CHUNK - 1:CHUNK] for h in hs]
            ecum = [jnp.exp(cum[h]) for h in hs]
            ekd = [jnp.exp(cl[h] - cum[h]) for h in hs]
            cd = [jnp.exp(cl[h]) for h in hs]
            qd = [q[h] * ecum[h] for h in hs]
            kd = [k[h] * ekd[h] for h in hs]
            s0 = [st_ref[c, h] for h in hs]
            ds = [ds_scr[h] for h in hs]
            intra = [_a_intra(q[h], k[h], cum[h]) for h in hs]
            attn = [intra[h][0] for h in hs]
            qs0 = [_dot(qd[h], s0[h]) for h in hs]
            o = [qs0[h] + _dot(attn[h], v[h]) for h in hs]
            hn = [_headnorm_bwd(dm_ref[rows, cols[h]].astype(F32), o[h], ng_ref[...], gi[h]) for h in hs]
            do = [hn[h][0] for h in hs]
            dqd = [_dot(do[h], s0[h], "nt") for h in hs]
            dattn = [jnp.where(mask, _dot(do[h], v[h], "nt"), 0.0) for h in hs]
            dv = [_dot(attn[h], do[h], "tn") + _dot(kd[h], ds[h]) for h in hs]
            dkd = [_dot(v[h], ds[h], "nt") for h in hs]
            dcd = [jnp.sum((s0[h] * ds[h]).T, axis=0, keepdims=True) for h in hs]
            for h in hs:
                ds_scr[h] = _dot(qd[h], do[h], "tn") + ds[h] * cd[h].T
            ib = [_a_intra_bwd(dattn[h], k[h], intra[h][1], intra[h][2]) for h in hs]
            dq = [dqd[h] * ecum[h] + ib[h][0] for h in hs]
            dk = [dkd[h] * ekd[h] + ib[h][1] for h in hs]
            dkk = [dkd[h] * kd[h] for h in hs]
            dcum = [dqd[h] * qd[h] - dkk[h] + q[h] * ib[h][0] - k[h] * ib[h][1] for h in hs]
            dcl = [jnp.sum(dkk[h], axis=0, keepdims=True) + dcd[h] * cd[h] for h in hs]
            dlogf = [_dot_exact_lhs(ltri, dcum[h], "tn") + dcl[h] for h in hs]
            dfv = [jnp.where(f[h] > TINY, dlogf[h] / f[h], 0.0) for h in hs]
            dfi = [dfv[h] * (1.0 - lbh[h]) * sg[h] * (1.0 - sg[h]) - dk[h] * (1.0 - lbh[h]) * sgn[h] * (1.0 - sgn[h]) for h in hs]
            dlbs = [jnp.sum(dfv[h] * (1.0 - sg[h]) - dk[h] * sgn[h], axis=0, keepdims=True) for h in hs]
            dqs = [dq[h] * (sq[h] * (1.0 + qi[h] * (1.0 - sq[h]))) for h in hs]
            dp_ref[rows, :] = jnp.concatenate(dqs + dfi + dv + [hn[h][1] for h in hs], axis=1).astype(BF16)
            dlb_ref[...] += jnp.concatenate(dlbs, axis=1)
            dng_ref[...] += sum(hn[h][2] for h in hs)
            return carry

        lax.fori_loop(0, nch, chunk, 0, unroll=2)

    colblk = lambda j: pl.BlockSpec((tb, A_WIDTH), lambda i, j=j: (nb - 1 - i, j))
    vec = lambda n: pl.BlockSpec((1, n), lambda i: (0, 0))
    return _call_beside(
        body, beside, nb, (proj, proj, proj, proj, lb.reshape(1, A_WIDTH), norm_g.reshape(1, A_DIM), states, dmixed), grid=(nb,),
        in_specs=[colblk(0), colblk(1), colblk(2), colblk(3), vec(A_WIDTH), vec(A_DIM),
                  pl.BlockSpec((nch, A_HEADS, A_DIM, A_DIM), lambda i: (nb - 1 - i, 0, 0, 0)), colblk(0)],
        out_specs=[pl.BlockSpec((tb, 4 * A_WIDTH), lambda i: (nb - 1 - i, 0)), vec(A_WIDTH), vec(A_DIM)],
        out_shape=[jax.ShapeDtypeStruct((t, 4 * A_WIDTH), BF16), jax.ShapeDtypeStruct((1, A_WIDTH), F32), jax.ShapeDtypeStruct((1, A_DIM), F32)],
        scratch_shapes=[pltpu.VMEM((A_HEADS, A_DIM, A_DIM), F32)], name="hgrn_bwd")


def _gate_lane_masks(shape):
    lane = lax.broadcasted_iota(jnp.int32, shape, 1)
    return lane < B_HEADS, (lane >= B_HEADS) & (lane < 2 * B_HEADS)


def _b_pre_fwd(proj, conv_w, alog_row, dtb_row, tb=512):
    t = proj.shape[0]
    cb0 = COL_B // B_WIDTH

    def body(q_ref, k_ref, v_ref, qp_ref, kp_ref, vp_ref, w_ref, gi_ref, al_ref, dt_ref, qkv_ref, gates_ref):
        first = pl.program_id(0) == 0
        for part, (c_ref, p_ref) in enumerate(((q_ref, qp_ref), (k_ref, kp_ref), (v_ref, vp_ref))):
            cols = slice(part * B_WIDTH, (part + 1) * B_WIDTH)
            prev = jnp.where(first, 0.0, p_ref[...])
            y = _conv_fwd(c_ref[...], prev, w_ref[:, cols])
            s = y * _sigmoid(y)
            if part < 2:
                outs = []
                for h in range(B_HEADS):
                    sh = s[:, h * B_DIM:(h + 1) * B_DIM]
                    outs.append(sh * lax.rsqrt(jnp.sum(sh * sh, axis=-1, keepdims=True) + EPS))
                s = jnp.concatenate(outs, axis=1)
            qkv_ref[:, cols] = s
        g = gi_ref[...]
        is_b, is_a = _gate_lane_masks(g.shape)
        la = -jnp.exp(al_ref[...]) * _softplus(g + dt_ref[...])
        gates_ref[...] = jnp.where(is_b, _sigmoid(g), jnp.where(is_a, la, 0.0))

    cur = lambda j: pl.BlockSpec((tb, B_WIDTH), lambda i, j=j: (i, cb0 + j))
    prv = lambda j: pl.BlockSpec((8, B_WIDTH), lambda i, j=j: (jnp.maximum(i * (tb // 8) - 1, 0), cb0 + j))
    vec = pl.BlockSpec((1, 128), lambda i: (0, 0))
    return pl.pallas_call(
        body, grid=(t // tb,),
        in_specs=[cur(0), cur(1), cur(2), prv(0), prv(1), prv(2), pl.BlockSpec((CONV_K, 3 * B_WIDTH), lambda i: (0, 0)),
                  pl.BlockSpec((tb, 128), lambda i: (i, COL_G // 128)), vec, vec],
        out_specs=[pl.BlockSpec((tb, 3 * B_WIDTH), lambda i: (i, 0)), pl.BlockSpec((tb, 128), lambda i: (i, 0))],
        out_shape=[jax.ShapeDtypeStruct((t, 3 * B_WIDTH), F32), jax.ShapeDtypeStruct((t, 128), F32)],
        compiler_params=_cparams(("parallel",)), name="gdn_pre_fwd")(proj, proj, proj, proj, proj, proj, conv_w, proj, alog_row, dtb_row)


def _inv_unit_lower(amats):
    r, c = _iota2(CHUNK, CHUNK)
    eye = jnp.where(r == c, 1.0, 0.0)
    ps = [eye - a for a in amats]
    aks = amats
    for _ in range(5):
        aks = [_dot(ak, ak, hi=True) for ak in aks]
        ps = [p + _dot(p, ak, hi=True) for p, ak in zip(ps, aks)]
    return ps


def _b_chunk(qs, ks, vs, betas, gcs, grows, gls, s0s, saved=None):
    hs = range(len(qs))
    causal, strict = _tril(CHUNK), _tril(CHUNK, strict=True)
    decay = [jnp.where(causal, jnp.exp(jnp.minimum(gcs[h] - grows[h], 0.0)), 0.0) for h in hs]
    kb = [ks[h] * betas[h] for h in hs]
    kk = [_dot(kb[h], ks[h], "nt") for h in hs]
    qkr = [_dot(qs[h], ks[h], "nt") for h in hs]
    eg = [jnp.exp(gcs[h]) for h in hs]
    bv = [vs[h] * betas[h] for h in hs]
    kg = [kb[h] * eg[h] for h in hs]
    qk = [qkr[h] * decay[h] for h in hs]
    qd = [qs[h] * eg[h] for h in hs]
    ekd = [jnp.exp(gls[h] - gcs[h]) for h in hs]
    kd = [ks[h] * ekd[h] for h in hs]
    cd = [jnp.exp(gls[h]) for h in hs]
    if saved is None:
        tinv = _inv_unit_lower([jnp.where(strict, kk[h] * decay[h], 0.0) for h in hs])
        u = [_dot(tinv[h], bv[h], hi=True) for h in hs]
        w = [_dot(tinv[h], kg[h], hi=True) for h in hs]
        ws = [_dot(w[h], s0s[h]) for h in hs]
        qs0 = [_dot(qd[h], s0s[h]) for h in hs]
        vn = [u[h] - ws[h] for h in hs]
        o = [qs0[h] + _dot(qk[h], vn[h]) for h in hs]
        s1 = [s0s[h] * cd[h] + _dot(kd[h], vn[h], "tn") for h in hs]
    else:
        tinv, w, vn, o = saved
        s1 = None
    return dict(decay=decay, kb=kb, kk=kk, tinv=tinv, eg=eg, bv=bv, kg=kg, w=w, qkr=qkr, qk=qk, qd=qd, ekd=ekd,
                kd=kd, cd=cd, vn=vn, o=o, s1=s1)


def _b_fwd(qkv, gates, proj, norm_g, beside=None, tb=256):
    t = qkv.shape[0]
    nch = tb // CHUNK

    def body(q_ref, k_ref, v_ref, ga_ref, z_ref, ng_ref, out_ref, st_ref, ti_ref, w_ref, vn_ref, o_ref, s_scr):
        @pl.when(pl.program_id(0) == 0)
        def _():
            s_scr[...] = jnp.zeros_like(s_scr)

        ltri = _tril(CHUNK).astype(F32)

        def chunk(c, carry):
            rows = pl.ds(pl.multiple_of(c * CHUNK, CHUNK), CHUNK)
            ga = ga_ref[rows, :]
            gcum = _dot_exact_lhs(ltri, ga)
            gcum_t = gcum.T
            hs = range(B_HEADS)
            cols = [slice(h * B_DIM, (h + 1) * B_DIM) for h in hs]
            s0s = [s_scr[h] for h in hs]
            for h in hs:
                st_ref[c, h] = s0s[h]
            r = _b_chunk([q_ref[rows, cols[h]] * GDN_SCALE for h in hs], [k_ref[rows, cols[h]] for h in hs],
                         [v_ref[rows, cols[h]] for h in hs], [ga[:, h:h + 1] for h in hs],
                         [gcum[:, B_HEADS + h:B_HEADS + h + 1] for h in hs], [gcum_t[B_HEADS + h:B_HEADS + h + 1, :] for h in hs],
                         [gcum[CHUNK - 1:CHUNK, B_HEADS + h:B_HEADS + h + 1] for h in hs], s0s)
            for h in hs:
                s_scr[h] = r["s1"][h]
                ti_ref[c, h] = r["tinv"][h]
            w_ref[rows, :] = jnp.concatenate(r["w"], axis=1)
            vn_ref[rows, :] = jnp.concatenate(r["vn"], axis=1)
            o_ref[rows, :] = jnp.concatenate(r["o"], axis=1)
            outs = [_headnorm_fwd(r["o"][h], ng_ref[...], z_ref[rows, cols[h]]) for h in hs]
            out_ref[rows, :] = jnp.concatenate(outs, axis=1).astype(BF16)
            return carry

        lax.fori_loop(0, nch, chunk, 0, unroll=2)

    part = lambda j: pl.BlockSpec((tb, B_WIDTH), lambda i, j=j: (i, j))
    wide = pl.BlockSpec((tb, B_WIDTH), lambda i: (i, 0))
    wide_shape = jax.ShapeDtypeStruct((t, B_WIDTH), F32)
    return _call_beside(
        body, beside, t // tb, (qkv, qkv, qkv, gates, proj, norm_g.reshape(1, B_DIM)), grid=(t // tb,),
        in_specs=[part(0), part(1), part(2), pl.BlockSpec((tb, 128), lambda i: (i, 0)),
                  pl.BlockSpec((tb, B_WIDTH), lambda i: (i, COL_B // B_WIDTH + 3)), pl.BlockSpec((1, B_DIM), lambda i: (0, 0))],
        out_specs=[wide, pl.BlockSpec((nch, B_HEADS, B_DIM, B_DIM), lambda i: (i, 0, 0, 0)),
                   pl.BlockSpec((nch, B_HEADS, CHUNK, CHUNK), lambda i: (i, 0, 0, 0)), wide, wide, wide],
        out_shape=[jax.ShapeDtypeStruct((t, B_WIDTH), BF16), jax.ShapeDtypeStruct((t // CHUNK, B_HEADS, B_DIM, B_DIM), F32),
                   jax.ShapeDtypeStruct((t // CHUNK, B_HEADS, CHUNK, CHUNK), F32), wide_shape, wide_shape, wide_shape],
        scratch_shapes=[pltpu.VMEM((B_HEADS, B_DIM, B_DIM), F32)], name="gdn_fwd")


def _b_bwd(qkv, gates, proj, norm_g, states, fwd_saved, dmixed, beside=None, tb=256):
    t = qkv.shape[0]
    nch = tb // CHUNK
    nb = t // tb

    def body(q_ref, k_ref, v_ref, ga_ref, z_ref, ng_ref, st_ref, ti_ref, w_ref, vn_ref, o_ref, dm0_ref, dm1_ref,
             dqkv_ref, dga_ref, dz_ref, dng_ref, ds_scr):
        @pl.when(pl.program_id(0) == 0)
        def _():
            ds_scr[...] = jnp.zeros_like(ds_scr)
            dng_ref[...] = jnp.zeros_like(dng_ref)

        ltri = _tril(CHUNK).astype(F32)
        strict = _tril(CHUNK, strict=True)
        lane = lax.broadcasted_iota(jnp.int32, (CHUNK, 128), 1)
        lane1 = lax.broadcasted_iota(jnp.int32, (1, 128), 1)

        def chunk(cc, carry):
            c = nch - 1 - cc
            rows = pl.ds(pl.multiple_of(c * CHUNK, CHUNK), CHUNK)
            ga = ga_ref[rows, :]
            gcum = _dot_exact_lhs(ltri, ga)
            gcum_t = gcum.T
            hs = range(B_HEADS)
            rsum = lambda a: jnp.sum(a, axis=-1, keepdims=True)
            cols = [slice(h * B_DIM, (h + 1) * B_DIM) for h in hs]
            q = [q_ref[rows, cols[h]] * GDN_SCALE for h in hs]
            k = [k_ref[rows, cols[h]] for h in hs]
            v = [v_ref[rows, cols[h]] for h in hs]
            z = [z_ref[rows, cols[h]] for h in hs]
            beta = [ga[:, h:h + 1] for h in hs]
            s0 = [st_ref[c, h] for h in hs]
            ds = [ds_scr[h] for h in hs]
            r = _b_chunk(q, k, v, beta, [gcum[:, B_HEADS + h:B_HEADS + h + 1] for h in hs],
                         [gcum_t[B_HEADS + h:B_HEADS + h + 1, :] for h in hs],
                         [gcum[CHUNK - 1:CHUNK, B_HEADS + h:B_HEADS + h + 1] for h in hs], s0,
                         saved=([ti_ref[c, h] for h in hs], [w_ref[rows, cols[h]] for h in hs],
                                [vn_ref[rows, cols[h]] for h in hs], [o_ref[rows, cols[h]] for h in hs]))
            decay, tinv, eg, w, vn, qd, kd, kb, cd = (r[n] for n in ("decay", "tinv", "eg", "w", "vn", "qd", "kd", "kb", "cd"))
            dms = [(dm0_ref if h < 2 else dm1_ref)[rows, (h % 2) * B_DIM:(h % 2 + 1) * B_DIM].astype(F32) for h in hs]
            hn = [_headnorm_bwd(dms[h], r["o"][h], ng_ref[...], z[h]) for h in hs]
            do = [hn[h][0] for h in hs]
            dvn = [_dot(r["qk"][h], do[h], "tn") + _dot(kd[h], ds[h]) for h in hs]
            dqk = [_dot(do[h], vn[h], "nt") for h in hs]
            dqd = [_dot(do[h], s0[h], "nt") for h in hs]
            dkd = [_dot(vn[h], ds[h], "nt") for h in hs]
            dw = [-_dot(dvn[h], s0[h], "nt") for h in hs]
            dcd = [jnp.sum(jnp.sum(s0[h] * ds[h], axis=0, keepdims=True), axis=1, keepdims=True) for h in hs]
            for h in hs:
                ds_scr[h] = _dot(qd[h], do[h], "tn") + ds[h] * cd[h] - _dot(w[h], dvn[h], "tn")
            dbv = [_dot(tinv[h], dvn[h], "tn", hi=True) for h in hs]
            dkg = [_dot(tinv[h], dw[h], "tn", hi=True) for h in hs]
            dt = [_dot(dvn[h], r["bv"][h], "nt", hi=True) + _dot(dw[h], r["kg"][h], "nt", hi=True) for h in hs]
            tdt = [_dot(tinv[h], dt[h], "tn", hi=True) for h in hs]
            da = [jnp.where(strict, -_dot(tdt[h], tinv[h], "nt", hi=True), 0.0) for h in hs]
            dm = [da[h] * decay[h] for h in hs]
            dn = [dqk[h] * decay[h] for h in hs]
            e = [(da[h] * r["kk"][h] + dqk[h] * r["qkr"][h]) * decay[h] for h in hs]
            dkb = [_dot(dm[h], k[h]) + dkg[h] * eg[h] for h in hs]
            dk = [_dot(dm[h], kb[h], "tn") + _dot(dn[h], q[h], "tn") + dkd[h] * r["ekd"][h] + dkb[h] * beta[h] for h in hs]
            dq = [_dot(dn[h], k[h]) + dqd[h] * eg[h] for h in hs]
            tkd = [rsum(dkd[h] * kd[h]) for h in hs]
            dgc = [rsum(e[h]) - rsum(e[h].T) + rsum(dqd[h] * qd[h]) - tkd[h] + rsum(dkg[h] * r["kg"][h]) for h in hs]
            dgl = [jnp.sum(tkd[h], axis=0, keepdims=True) + dcd[h] * cd[h] for h in hs]
            dbeta = [rsum(dbv[h] * v[h]) + rsum(dkb[h] * k[h]) for h in hs]
            dbeta_m = sum(jnp.where(lane == h, dbeta[h], 0.0) for h in hs)
            dgc_m = sum(jnp.where(lane == B_HEADS + h, dgc[h], 0.0) for h in hs)
            dgl_m = sum(jnp.where(lane1 == B_HEADS + h, dgl[h], 0.0) for h in hs)
            dqkv_ref[rows, :] = jnp.concatenate([dq[h] * GDN_SCALE for h in hs] + dk + [dbv[h] * beta[h] for h in hs], axis=1)
            dz_ref[rows, :] = jnp.concatenate([hn[h][1] for h in hs], axis=1).astype(BF16)
            dga_ref[rows, :] = dbeta_m + _dot_exact_lhs(ltri, dgc_m, "tn") + dgl_m
            dng_ref[...] += sum(hn[h][2] for h in hs)
            return carry

        lax.fori_loop(0, nch, chunk, 0, unroll=2)

    part = lambda j: pl.BlockSpec((tb, B_WIDTH), lambda i, j=j: (nb - 1 - i, j))
    rowblk = lambda w, j=0: pl.BlockSpec((tb, w), lambda i, j=j: (nb - 1 - i, j))
    return _call_beside(
        body, beside, nb, (qkv, qkv, qkv, gates, proj, norm_g.reshape(1, B_DIM), states, *fwd_saved, dmixed, dmixed), grid=(nb,),
        in_specs=[part(0), part(1), part(2), rowblk(128), rowblk(B_WIDTH, COL_B // B_WIDTH + 3),
                  pl.BlockSpec((1, B_DIM), lambda i: (0, 0)),
                  pl.BlockSpec((nch, B_HEADS, B_DIM, B_DIM), lambda i: (nb - 1 - i, 0, 0, 0)),
                  pl.BlockSpec((nch, B_HEADS, CHUNK, CHUNK), lambda i: (nb - 1 - i, 0, 0, 0)),
                  rowblk(B_WIDTH), rowblk(B_WIDTH), rowblk(B_WIDTH), rowblk(256, 1), rowblk(256, 2)],
        out_specs=[rowblk(3 * B_WIDTH), rowblk(128), rowblk(B_WIDTH), pl.BlockSpec((1, B_DIM), lambda i: (0, 0))],
        out_shape=[jax.ShapeDtypeStruct((t, 3 * B_WIDTH), F32), jax.ShapeDtypeStruct((t, 128), F32),
                   jax.ShapeDtypeStruct((t, B_WIDTH), BF16), jax.ShapeDtypeStruct((1, B_DIM), F32)],
        scratch_shapes=[pltpu.VMEM((B_HEADS, B_DIM, B_DIM), F32)], name="gdn_bwd")


def _b_pre_bwd(proj, conv_w, alog_row, dtb_row, dqkv, dgates, tb=512):
    t = proj.shape[0]
    cb0 = COL_B // B_WIDTH

    def body(q_ref, k_ref, v_ref, qp_ref, kp_ref, vp_ref, w_ref, gi_ref, al_ref, dt_ref, dqkv_ref, dga_ref,
             dy_ref, dgi_ref, dw_ref, dal_ref, ddt_ref):
        first = pl.program_id(0) == 0

        @pl.when(first)
        def _():
            dw_ref[...] = jnp.zeros_like(dw_ref)
            dal_ref[...] = jnp.zeros_like(dal_ref)
            ddt_ref[...] = jnp.zeros_like(ddt_ref)

        for part, (c_ref, p_ref) in enumerate(((q_ref, qp_ref), (k_ref, kp_ref), (v_ref, vp_ref))):
            cols = slice(part * B_WIDTH, (part + 1) * B_WIDTH)
            cur = c_ref[...]
            prev = jnp.where(first, 0.0, p_ref[...])
            w = w_ref[:, cols]
            shifted = [_shift_rows(cur, prev, 3 - j, down=True) for j in range(3)] + [cur]
            y = shifted[0] * w[0:1] + shifted[1] * w[1:2] + shifted[2] * w[2:3] + shifted[3] * w[3:4]
            sg = _sigmoid(y)
            s = y * sg
            dsn = dqkv_ref[:, cols]
            if part < 2:
                outs = []
                for h in range(B_HEADS):
                    hc = slice(h * B_DIM, (h + 1) * B_DIM)
                    sh, dh = s[:, hc], dsn[:, hc]
                    rq = lax.rsqrt(jnp.sum(sh * sh, axis=-1, keepdims=True) + EPS)
                    nh = sh * rq
                    outs.append(rq * (dh - nh * jnp.sum(dh * nh, axis=-1, keepdims=True)))
                dsn = jnp.concatenate(outs, axis=1)
            dy = dsn * (sg * (1.0 + y * (1.0 - sg)))
            dy_ref[:, cols] = dy
            dw_ref[:, cols] += jnp.concatenate([jnp.sum(shifted[j] * dy, axis=0, keepdims=True) for j in range(CONV_K)], axis=0)
        g = gi_ref[...]
        dga = dga_ref[...]
        is_b, is_a = _gate_lane_masks(g.shape)
        beta = _sigmoid(g)
        pre = g + dt_ref[...]
        ea = jnp.exp(al_ref[...])
        la = -ea * _softplus(pre)
        dpre = jnp.where(is_a, dga * (-ea) * _sigmoid(pre), 0.0)
        dgi_ref[...] = jnp.where(is_b, dga * beta * (1.0 - beta), dpre).astype(BF16)
        dal_ref[...] += jnp.sum(jnp.where(is_a, dga * la, 0.0), axis=0, keepdims=True)
        ddt_ref[...] += jnp.sum(dpre, axis=0, keepdims=True)

    cur = lambda j: pl.BlockSpec((tb, B_WIDTH), lambda i, j=j: (i, cb0 + j))
    prv = lambda j: pl.BlockSpec((8, B_WIDTH), lambda i, j=j: (jnp.maximum(i * (tb // 8) - 1, 0), cb0 + j))
    vec = pl.BlockSpec((1, 128), lambda i: (0, 0))
    wspec = pl.BlockSpec((CONV_K, 3 * B_WIDTH), lambda i: (0, 0))
    return pl.pallas_call(
        body, grid=(t // tb,),
        in_specs=[cur(0), cur(1), cur(2), prv(0), prv(1), prv(2), wspec,
                  pl.BlockSpec((tb, 128), lambda i: (i, COL_G // 128)), vec, vec,
                  pl.BlockSpec((tb, 3 * B_WIDTH), lambda i: (i, 0)), pl.BlockSpec((tb, 128), lambda i: (i, 0))],
        out_specs=[pl.BlockSpec((tb, 3 * B_WIDTH), lambda i: (i, 0)), pl.BlockSpec((tb, 128), lambda i: (i, 0)), wspec, vec, vec],
        out_shape=[jax.ShapeDtypeStruct((t, 3 * B_WIDTH), F32), jax.ShapeDtypeStruct((t, 128), BF16),
                   jax.ShapeDtypeStruct((CONV_K, 3 * B_WIDTH), F32), jax.ShapeDtypeStruct((1, 128), F32), jax.ShapeDtypeStruct((1, 128), F32)],
        compiler_params=_cparams(("arbitrary",)), name="gdn_pre_bwd")(
            proj, proj, proj, proj, proj, proj, conv_w, proj, alog_row, dtb_row, dqkv, dgates)


def _conv_bwd_x(dy, w, cb, tb=512, name="conv_bwd_x"):
    t, c = dy.shape
    nb = t // tb

    def body(dy_ref, nx_ref, w_ref, dx_ref):
        cur = dy_ref[...]
        nxt = jnp.where(pl.program_id(0) == nb - 1, 0.0, nx_ref[...])
        w = w_ref[...]
        dx = cur * w[3:4]
        for j in range(3):
            dx = dx + _shift_rows(cur, nxt, 3 - j, down=False) * w[j:j + 1]
        dx_ref[...] = dx.astype(BF16)

    return pl.pallas_call(
        body, grid=(nb, c // cb),
        in_specs=[pl.BlockSpec((tb, cb), lambda i, j: (i, j)),
                  pl.BlockSpec((8, cb), lambda i, j: (jnp.minimum((i + 1) * (tb // 8), t // 8 - 1), j)),
                  pl.BlockSpec((CONV_K, cb), lambda i, j: (0, j))],
        out_specs=pl.BlockSpec((tb, cb), lambda i, j: (i, j)), out_shape=jax.ShapeDtypeStruct((t, c), BF16),
        compiler_params=_cparams(("parallel", "parallel")), name=name)(dy, dy, w)


def _c_gates(xc, wa_ref, ba_ref, wx_ref, bx_ref, lam_ref, is_row0):
    r = _sigmoid(_dot(xc, wa_ref[...]) + ba_ref[...])
    i = _sigmoid(_dot(xc, wx_ref[...]) + bx_ref[...])
    sp = _softplus(-lam_ref[...])
    log_a = -RG_C * r * sp
    a = jnp.exp(log_a)
    m2 = _neg_expm1(2.0 * log_a)
    mult = jnp.where(is_row0, 1.0, jnp.sqrt(jnp.maximum(m2, EPS)))
    return r, i, sp, log_a, a, m2, mult


def _row0_mask(tb, first):
    ridx = lax.broadcasted_iota(jnp.int32, (tb, C_WIDTH), 0)
    return (ridx == 0) & first


def _c_fwd(proj, conv_w, conv_b, wa, ba, wx, bx, lam, tb=512):
    t = proj.shape[0]
    cbx = COL_C // C_WIDTH

    def body(x_ref, xp_ref, y_ref, w_ref, cb_ref, wa_ref, ba_ref, wx_ref, bx_ref, lam_ref, out_ref, h_ref, a_scr, b_scr, h_scr):
        first = pl.program_id(0) == 0

        @pl.when(first)
        def _():
            h_scr[...] = jnp.zeros_like(h_scr)

        prev = jnp.where(first, 0.0, xp_ref[...])
        xc = _conv_fwd(x_ref[...], prev, w_ref[...]) + cb_ref[...]
        _, i, _, _, a, _, mult = _c_gates(xc, wa_ref, ba_ref, wx_ref, bx_ref, lam_ref, _row0_mask(tb, first))
        a_scr[...] = a
        b_scr[...] = mult * i * xc

        def step(blk, h):
            rows = pl.ds(pl.multiple_of(blk * 8, 8), 8)
            at, bt = a_scr[rows, :], b_scr[rows, :]
            out = []
            for j in range(8):
                h = at[j:j + 1] * h + bt[j:j + 1]
                out.append(h)
            h_ref[rows, :] = jnp.concatenate(out, axis=0)
            return h

        h_scr[...] = lax.fori_loop(0, tb // 8, step, h_scr[...])
        gl, _ = _gelu_tanh(y_ref[...])
        out_ref[...] = (gl * h_ref[...]).astype(BF16)

    vec = pl.BlockSpec((1, C_WIDTH), lambda i: (0, 0))
    mat = pl.BlockSpec((C_WIDTH, C_WIDTH), lambda i: (0, 0))
    row = pl.BlockSpec((tb, C_WIDTH), lambda i: (i, 0))
    return pl.pallas_call(
        body, grid=(t // tb,),
        in_specs=[pl.BlockSpec((tb, C_WIDTH), lambda i: (i, cbx)),
                  pl.BlockSpec((8, C_WIDTH), lambda i: (jnp.maximum(i * (tb // 8) - 1, 0), cbx)),
                  pl.BlockSpec((tb, C_WIDTH), lambda i: (i, cbx + 1)),
                  pl.BlockSpec((CONV_K, C_WIDTH), lambda i: (0, 0)), vec, mat, vec, mat, vec, vec],
        out_specs=[row, row],
        out_shape=[jax.ShapeDtypeStruct((t, C_WIDTH), BF16), jax.ShapeDtypeStruct((t, C_WIDTH), F32)],
        scratch_shapes=[pltpu.VMEM((tb, C_WIDTH), F32), pltpu.VMEM((tb, C_WIDTH), F32), pltpu.VMEM((1, C_WIDTH), F32)],
        compiler_params=_cparams(("arbitrary",)), name="lru_fwd")(proj, proj, proj, conv_w, conv_b, wa, ba, wx, bx, lam)


def _c_bwd(proj, conv_w, conv_b, wa, ba, wx, bx, lam, hs, dmixed, tb=512):
    t = proj.shape[0]
    nb = t // tb
    cbx = COL_C // C_WIDTH

    def body(x_ref, xp_ref, y_ref, w_ref, cb_ref, wa_ref, ba_ref, wx_ref, bx_ref, lam_ref, h_ref, hp_ref, dm_ref,
             dxc_ref, dyg_ref, dw_ref, dcb_ref, dwa_ref, dba_ref, dwx_ref, dbx_ref, dlam_ref, g_scr, a_scr, c_scr):
        step_id = pl.program_id(0)
        first = step_id == nb - 1

        @pl.when(step_id == 0)
        def _():
            c_scr[...] = jnp.zeros_like(c_scr)
            for ref in (dw_ref, dcb_ref, dwa_ref, dba_ref, dwx_ref, dbx_ref, dlam_ref):
                ref[...] = jnp.zeros_like(ref)

        cur = x_ref[...]
        prev = jnp.where(first, 0.0, xp_ref[...])
        w = w_ref[...]
        shifted = [_shift_rows(cur, prev, 3 - j, down=True) for j in range(3)] + [cur]
        xc = shifted[0] * w[0:1] + shifted[1] * w[1:2] + shifted[2] * w[2:3] + shifted[3] * w[3:4] + cb_ref[...]
        row0 = _row0_mask(tb, first)
        r, i, sp, log_a, a, m2, mult = _c_gates(xc, wa_ref, ba_ref, wx_ref, bx_ref, lam_ref, row0)
        h = h_ref[...]
        hprev = _shift_rows(h, jnp.where(first, 0.0, hp_ref[...]), 1, down=True)
        gl, dgl = _gelu_tanh(y_ref[...])
        dm = dm_ref[...].astype(F32)
        dyg_ref[...] = (dm * h * dgl).astype(BF16)
        g_scr[...] = dm * gl
        a_scr[...] = a

        def step(blk, carry):
            b = tb // 8 - 1 - blk
            rows = pl.ds(pl.multiple_of(b * 8, 8), 8)
            at, gt = a_scr[rows, :], g_scr[rows, :]
            out = [None] * 8
            for j in range(7, -1, -1):
                gj = gt[j:j + 1] + carry
                out[j] = gj
                carry = at[j:j + 1] * gj
            g_scr[rows, :] = jnp.concatenate(out, axis=0)
            return carry

        c_scr[...] = lax.fori_loop(0, tb // 8, step, c_scr[...])
        dbx = g_scr[...]
        da = dbx * hprev
        dmult = jnp.where(row0, 0.0, dbx * i * xc)
        di = dbx * mult * xc
        dxc = dbx * mult * i
        dm2 = jnp.where(m2 > EPS, dmult * 0.5 / mult, 0.0)
        dlog_a = da * a - 2.0 * a * a * dm2
        dr = dlog_a * (-RG_C) * sp
        dlam_ref[...] += jnp.sum(dlog_a * (-RG_C) * r, axis=0, keepdims=True) * (-_sigmoid(-lam_ref[...]))
        dpa = dr * r * (1.0 - r)
        dpx = di * i * (1.0 - i)
        dba_ref[...] += jnp.sum(dpa, axis=0, keepdims=True)
        dbx_ref[...] += jnp.sum(dpx, axis=0, keepdims=True)
        dwa_ref[...] += _dot(xc, dpa, "tn")
        dwx_ref[...] += _dot(xc, dpx, "tn")
        dxc = dxc + _dot(dpa, wa_ref[...], "nt") + _dot(dpx, wx_ref[...], "nt")
        dxc_ref[...] = dxc
        dcb_ref[...] += jnp.sum(dxc, axis=0, keepdims=True)
        dw_ref[...] += jnp.concatenate([jnp.sum(shifted[j] * dxc, axis=0, keepdims=True) for j in range(CONV_K)], axis=0)

    vec = pl.BlockSpec((1, C_WIDTH), lambda i: (0, 0))
    mat = pl.BlockSpec((C_WIDTH, C_WIDTH), lambda i: (0, 0))
    cw = pl.BlockSpec((CONV_K, C_WIDTH), lambda i: (0, 0))
    row = lambda j=0: pl.BlockSpec((tb, C_WIDTH), lambda i, j=j: (nb - 1 - i, j))
    halo = lambda j=0: pl.BlockSpec((8, C_WIDTH), lambda i, j=j: (jnp.maximum((nb - 1 - i) * (tb // 8) - 1, 0), j))
    return pl.pallas_call(
        body, grid=(nb,),
        in_specs=[row(cbx), halo(cbx), row(cbx + 1), cw, vec, mat, vec, mat, vec, vec, row(), halo(), row(3)],
        out_specs=[row(), row(), cw, vec, mat, vec, mat, vec, vec],
        out_shape=[jax.ShapeDtypeStruct((t, C_WIDTH), F32), jax.ShapeDtypeStruct((t, C_WIDTH), BF16),
                   jax.ShapeDtypeStruct((CONV_K, C_WIDTH), F32), jax.ShapeDtypeStruct((1, C_WIDTH), F32),
                   jax.ShapeDtypeStruct((C_WIDTH, C_WIDTH), F32), jax.ShapeDtypeStruct((1, C_WIDTH), F32),
                   jax.ShapeDtypeStruct((C_WIDTH, C_WIDTH), F32), jax.ShapeDtypeStruct((1, C_WIDTH), F32),
                   jax.ShapeDtypeStruct((1, C_WIDTH), F32)],
        scratch_shapes=[pltpu.VMEM((tb, C_WIDTH), F32), pltpu.VMEM((tb, C_WIDTH), F32), pltpu.VMEM((1, C_WIDTH), F32)],
        compiler_params=_cparams(("arbitrary",)), name="lru_bwd")(
            proj, proj, proj, conv_w, conv_b, wa, ba, wx, bx, lam, hs, hs, dmixed)


def _mesh_pos():
    return lax.axis_index("x"), lax.axis_index("y"), lax.axis_index("c")


def _all_gather(x, name):
    def body(x_ref, out_ref, send_sems, recv_sems, local_sem):
        mx, my, mc = _mesh_pos()
        me, sibling = (mx, my, mc), (mx, my, 1 - mc)
        chips = [(1 - mx, my), (mx, 1 - my), (1 - mx, 1 - my)]

        def slot(px, py, pc):
            return out_ref.at[4 * px + 2 * py + pc]

        def copy(k, block, to, src=None):
            return pltpu.make_async_remote_copy(
                src_ref=slot(*block) if src is None else src, dst_ref=slot(*block),
                send_sem=send_sems.at[k], recv_sem=recv_sems.at[k], device_id=to, device_id_type=MESH)

        mine = pltpu.make_async_copy(x_ref, slot(*me), local_sem)
        mine.start()
        first = [copy(0, me, sibling, src=x_ref)]
        first += [copy(1 + j, me, (*chip, mc), src=x_ref) for j, chip in enumerate(chips)]
        for cp in first:
            cp.start()
        passed = [copy(4 + j, (*chip, mc), sibling) for j, chip in enumerate(chips)]
        for j, chip in enumerate(chips):
            copy(1 + j, (*chip, mc), me).wait_recv()
            passed[j].start()
        copy(0, sibling, me).wait_recv()
        for j, chip in enumerate(chips):
            copy(4 + j, (*chip, 1 - mc), me).wait_recv()
        for cp in first + passed:
            cp.wait_send()
        mine.wait()

    return pl.pallas_call(
        body, out_shape=jax.ShapeDtypeStruct((N_DEV,) + x.shape, x.dtype),
        in_specs=[pl.BlockSpec(memory_space=pl.ANY)], out_specs=pl.BlockSpec(memory_space=pl.ANY),
        scratch_shapes=[pltpu.SemaphoreType.DMA((7,)), pltpu.SemaphoreType.DMA((7,)), pltpu.SemaphoreType.DMA(())],
        name=name)(x)


def _all_to_all(x, name):
    def body(x_ref, out_ref, send_sems, recv_sems, local_sem):
        mx, my, mc = _mesh_pos()
        me = 4 * mx + 2 * my + mc
        mine = pltpu.make_async_copy(x_ref.at[me], out_ref.at[me], local_sem)
        mine.start()
        copies = []
        for k in range(1, N_DEV):
            px = 1 - mx if k & 4 else mx
            py = 1 - my if k & 2 else my
            pc = 1 - mc if k & 1 else mc
            copies.append(pltpu.make_async_remote_copy(
                src_ref=x_ref.at[4 * px + 2 * py + pc], dst_ref=out_ref.at[me],
                send_sem=send_sems.at[k - 1], recv_sem=recv_sems.at[k - 1], device_id=(px, py, pc), device_id_type=MESH))
        for cp in copies:
            cp.start()
        for cp in copies:
            cp.wait()
        mine.wait()

    return pl.pallas_call(
        body, out_shape=jax.ShapeDtypeStruct(x.shape, x.dtype),
        in_specs=[pl.BlockSpec(memory_space=pl.ANY)], out_specs=pl.BlockSpec(memory_space=pl.ANY),
        scratch_shapes=[pltpu.SemaphoreType.DMA((7,)), pltpu.SemaphoreType.DMA((7,)), pltpu.SemaphoreType.DMA(())],
        name=name)(x)


class _Exchange:
    def __init__(self, arrays, layouts):
        self.arrays, self.layouts = list(arrays), list(layouts)
        self.out_shapes = []
        for a, lay in zip(self.arrays, self.layouts):
            if lay == 'a2a':
                shp = a.shape
            elif lay == 'slot':
                shp = (N_DEV,) + a.shape
            elif lay == 'rows':
                shp = (N_DEV * a.shape[0], a.shape[1])
            else:
                shp = (a.shape[0], N_DEV * a.shape[1])
            self.out_shapes.append(jax.ShapeDtypeStruct(shp, a.dtype))
        n = len(self.arrays)
        self.scratch = [pltpu.SemaphoreType.DMA((7 * n,)), pltpu.SemaphoreType.DMA((7 * n,)), pltpu.SemaphoreType.DMA((n,))]

    def _landing(self, a, dst_ref, idx):
        lay, shape = self.layouts[a], self.arrays[a].shape
        if lay in ('a2a', 'slot'):
            return dst_ref.at[idx]
        if lay == 'rows':
            return dst_ref.at[pl.ds(pl.multiple_of(idx * shape[0], shape[0]), shape[0]), :]
        return dst_ref.at[:, pl.ds(pl.multiple_of(idx * shape[1], shape[1]), shape[1])]

    def copies(self, src_refs, dst_refs, send_sems, recv_sems, local_sems):
        mx, my, mc = _mesh_pos()
        me = 4 * mx + 2 * my + mc
        out = []
        for a, (src, dst) in enumerate(zip(src_refs, dst_refs)):
            a2a = self.layouts[a] == 'a2a'
            out.append(pltpu.make_async_copy(src.at[me] if a2a else src, self._landing(a, dst, me), local_sems.at[a]))
            for k in range(1, N_DEV):
                px = 1 - mx if k & 4 else mx
                py = 1 - my if k & 2 else my
                pc = 1 - mc if k & 1 else mc
                out.append(pltpu.make_async_remote_copy(
                    src_ref=src.at[4 * px + 2 * py + pc] if a2a else src, dst_ref=self._landing(a, dst, me),
                    send_sem=send_sems.at[7 * a + k - 1], recv_sem=recv_sems.at[7 * a + k - 1],
                    device_id=(px, py, pc), device_id_type=MESH))
        return out


_ANY = pl.BlockSpec(memory_space=pl.ANY)


def _run_exchange(ex, name):
    n = len(ex.arrays)

    def body(*refs):
        cps = ex.copies(refs[:n], refs[n:2 * n], *refs[2 * n:])
        for cp in cps:
            cp.start()
        for cp in cps:
            cp.wait()

    return pl.pallas_call(body, out_shape=ex.out_shapes, in_specs=[_ANY] * n, out_specs=[_ANY] * n,
                          scratch_shapes=ex.scratch, name=name)(*ex.arrays)


def _call_beside(body, ex, nsteps, args, *, grid, in_specs, out_specs, out_shape, scratch_shapes, name):
    if ex is None:
        outs = pl.pallas_call(body, grid=grid, in_specs=in_specs, out_specs=out_specs, out_shape=out_shape,
                              scratch_shapes=scratch_shapes, compiler_params=_cparams(("arbitrary",)), name=name)(*args)
        return outs, None
    n_in, n_out, n_scr, n = len(in_specs), len(out_specs), len(scratch_shapes), len(ex.arrays)

    def wrapped(*refs):
        ins, refs = refs[:n_in], refs[n_in:]
        ex_ins, refs = refs[:n], refs[n:]
        outs, refs = refs[:n_out], refs[n_out:]
        ex_outs, refs = refs[:n], refs[n:]
        scr, sems = refs[:n_scr], refs[n_scr:]
        step = pl.program_id(0)

        @pl.when(step == 0)
        def _():
            for cp in ex.copies(ex_ins, ex_outs, *sems):
                cp.start()

        body(*ins, *outs, *scr)

        @pl.when(step == nsteps - 1)
        def _():
            for cp in ex.copies(ex_ins, ex_outs, *sems):
                cp.wait()

    res = pl.pallas_call(
        wrapped, grid=grid, in_specs=list(in_specs) + [_ANY] * n, out_specs=list(out_specs) + [_ANY] * n,
        out_shape=list(out_shape) + ex.out_shapes, scratch_shapes=list(scratch_shapes) + ex.scratch,
        compiler_params=_cparams(("arbitrary",)), name=name)(*args, *ex.arrays)
    return res[:n_out], res[n_out:]


def _adamw_math(w, g, m, v):
    m = ADAM_B1 * m + (1.0 - ADAM_B1) * g
    v = ADAM_B2 * v + (1.0 - ADAM_B2) * (g * g)
    m_hat = m / (1.0 - ADAM_B1 ** ADAM_STEP)
    v_hat = v / (1.0 - ADAM_B2 ** ADAM_STEP)
    delta = -ADAM_LR * (m_hat / (jnp.sqrt(v_hat) + ADAM_EPS) + ADAM_WD * w)
    return delta, m, v


def _sum_adamw(parts, w, m, v, tr, name):
    p, r, c = parts.shape
    tr = min(tr, r)
    assert r % tr == 0

    def body(p_ref, w_ref, m_ref, v_ref, g_ref, d_ref, nm_ref, nv_ref):
        g = p_ref[0].astype(F32)
        for j in range(1, p):
            g = g + p_ref[j].astype(F32)
        delta, nm, nv = _adamw_math(w_ref[...], g, m_ref[...], v_ref[...])
        g_ref[...] = g
        d_ref[...] = delta
        nm_ref[...] = nm
        nv_ref[...] = nv

    row = pl.BlockSpec((tr, c), lambda i: (i, 0))
    return pl.pallas_call(
        body, grid=(r // tr,), in_specs=[pl.BlockSpec((p, tr, c), lambda i: (0, i, 0)), row, row, row],
        out_specs=[row] * 4, out_shape=[jax.ShapeDtypeStruct((r, c), F32)] * 4,
        compiler_params=_cparams(("parallel",)), name=name)(parts, w, m, v)


def _sum_parts(parts, name):
    p, r, c = parts.shape

    def body(p_ref, o_ref):
        g = p_ref[0]
        for j in range(1, p):
            g = g + p_ref[j]
        o_ref[...] = g

    return pl.pallas_call(body, out_shape=jax.ShapeDtypeStruct((r, c), F32), name=name)(parts)


def _pack(arrs, mult=1024):
    flat = jnp.concatenate([a.reshape(-1).astype(F32) for a in arrs])
    n = flat.shape[0]
    npad = -n % mult
    return jnp.pad(flat, (0, npad)).reshape(-1, 128)


def _unpack(buf, shapes):
    flat = buf.reshape(-1)
    out, off = [], 0
    for s in shapes:
        n = 1
        for d in s:
            n *= d
        out.append(flat[off:off + n].reshape(s))
        off += n
    return out


def _block_diag(w):
    rows = [jnp.pad(w[i], ((0, 0), (i * C_BLOCK_DIM, C_WIDTH - (i + 1) * C_BLOCK_DIM))) for i in range(C_BLOCKS)]
    return jnp.concatenate(rows, axis=0)


def _diag_blocks(m):
    m4 = m.reshape(C_BLOCKS, C_BLOCK_DIM, C_BLOCKS, C_BLOCK_DIM)
    return jnp.stack([m4[i, :, i, :] for i in range(C_BLOCKS)])


def _gate_row(v):
    return jnp.pad(v.astype(F32), (B_HEADS, 128 - 2 * B_HEADS)).reshape(1, 128)


def _permute_w_in(w):
    pad = jnp.zeros(w.shape[:-1] + (D_IN_PAD - D_IN,), w.dtype)
    return jnp.concatenate([w[..., :3072], w[..., 3080:3592], w[..., 3072:3080], pad], axis=-1)


def _unpermute_w_in(w):
    return jnp.concatenate([w[..., :3072], w[..., COL_G:COL_G + 8], w[..., 3072:COL_G]], axis=-1)


_WEIGHTS = ['norm1_g', 'w_in', 'hgrn_lb_logits', 'hgrn_norm_g', 'gdn_conv_w', 'gdn_a_log', 'gdn_dt_bias', 'gdn_norm_g',
            'lru_conv_w', 'lru_conv_b', 'lru_w_a', 'lru_b_a', 'lru_w_x', 'lru_b_x', 'lru_lambda', 'w_out', 'norm2_g',
            'w_up', 'w_down', 'final_norm_g']
_BIG = ('w_in', 'w_out', 'w_up', 'w_down')
_SHARDED_SMALL = ('gdn_conv_w', 'lru_conv_w')


def _step(x, target, w, m, v):
    t = x.shape[0]
    mx, my, mc = _mesh_pos()
    me = 4 * mx + 2 * my + mc

    bf = lambda a: a.astype(BF16)

    def full_w_in(g):
        return _permute_w_in(jnp.moveaxis(g, 0, 1).reshape(D_MODEL, D_IN))

    conv_shapes = [w['gdn_conv_w'].shape, w['lru_conv_w'].shape]
    g_in, g_conv = _run_exchange(
        _Exchange([bf(w['w_in'][0]), _pack([w['gdn_conv_w'], w['lru_conv_w']])], ['slot', 'slot']), "gather_first")
    w_in = [full_w_in(g_in)]
    w_out, w_up, w_down = [], [], []
    gdn_cw, lru_cw = [], []
    for j in range(N_DEV):
        a, b = _unpack(g_conv[j], conv_shapes)
        gdn_cw.append(a)
        lru_cw.append(b)
    gdn_cw = jnp.concatenate(gdn_cw, axis=-1)
    lru_cw = jnp.concatenate(lru_cw, axis=-1)

    lbnd = _lb_fwd(w['hgrn_lb_logits'])
    row = lambda a: a.reshape(1, -1)

    def c_args(l):
        return (lru_cw[l], row(w['lru_conv_b'][l]), _block_diag(w['lru_w_a'][l]), row(w['lru_b_a'][l]),
                _block_diag(w['lru_w_x'][l]), row(w['lru_b_x'][l]), row(w['lru_lambda'][l]))

    saved = []
    xl = x
    h = _rms_fwd(x, w['norm1_g'][0], name="rms_fwd")
    for l in range(DEPTH):
        proj = _mm_rows(h, w_in[l], "nn", 256, "mm_proj")
        mix_a, st_a = _a_fwd(proj, lbnd[l], w['hgrn_norm_g'][l])
        alr, dtr = _gate_row(w['gdn_a_log'][l]), _gate_row(w['gdn_dt_bias'][l])
        qkv, gates = _b_pre_fwd(proj, gdn_cw[l], alr, dtr)
        nxt = [bf(w['w_in'][l + 1])] if l + 1 < DEPTH else []
        gather = _Exchange([bf(w['w_out'][l]), bf(w['w_up'][l]), bf(w['w_down'][l])] + nxt, ['rows', 'cols', 'rows'] + ['slot'] * len(nxt))
        (mix_b, st_b, *b_saved), got = _b_fwd(qkv, gates, proj, w['gdn_norm_g'][l], beside=gather)
        w_out.append(got[0])
        w_up.append(got[1])
        w_down.append(got[2])
        if nxt:
            w_in.append(full_w_in(got[3]))
        mix_c, hs = _c_fwd(proj, *c_args(l))
        mixed = jnp.concatenate([mix_a, mix_b, mix_c], axis=1)
        x_mid, h2 = _mm_rows(mixed, w_out[l], "nn", 1024, "mm_out", residual=xl, epilogue="rms_fwd", norm=w['norm2_g'][l])
        act, up = _mm_rows(h2, w_up[l], "nn", 256, "mm_up", epilogue="relu2")
        saved.append(dict(x=xl, h=h, proj=proj, st_a=st_a, qkv=qkv, gates=gates, st_b=st_b, b_saved=b_saved, hs=hs, mixed=mixed,
                          x_mid=x_mid, h2=h2, up=up, act=act, alr=alr, dtr=dtr))
        if l + 1 < DEPTH:
            xl, h = _mm_rows(act, w_down[l], "nn", 512, "mm_down", residual=x_mid, epilogue="rms_fwd", norm=w['norm1_g'][l + 1])
        else:
            xl = _mm_rows(act, w_down[l], "nn", 512, "mm_down_last", residual=x_mid)
    loss, dx, dgf = _loss_head(xl, w['final_norm_g'], target)

    gs = {n: [None] * DEPTH for n in _WEIGHTS}
    recv = {n: [None] * DEPTH for n in _BIG}
    dw_in_above = None
    for l in reversed(range(DEPTH)):
        s = saved[l]
        dup = _mm_rows(dx, w_down[l], "nt", 256, "mm_dact", epilogue="drelu2", up=s['up'])
        dw_down = _mm_tn(s['act'], dx, 512, 512, "mm_dw_down").reshape(N_DEV, D_FF // N_DEV, D_MODEL)
        dx_mid, dg2 = _mm_rows(dup, w_up[l], "nt", 512, "mm_dh2", epilogue="rms_bwd", norm=(s['x_mid'], w['norm2_g'][l], dx))
        dw_up = _mm_tn(s['h2'], dup, D_FF // N_DEV, 1024, "mm_dw_up", split=True)
        gs['norm2_g'][l] = dg2[0]
        dmixed = _mm_rows(dx_mid, w_out[l], "nt", 1024, "mm_dmixed")
        dw_out = _mm_tn(s['mixed'], dx_mid, 1024, 1024, "mm_dw_out").reshape(N_DEV, D_MODEL // N_DEV, D_MODEL)
        proj = s['proj']
        above = [dw_in_above] if dw_in_above is not None else []
        (dpa, dlb, dnga), got = _a_bwd(proj, lbnd[l], w['hgrn_norm_g'][l], s['st_a'], dmixed,
                                       beside=_Exchange([dw_out] + above, ['a2a'] * (1 + len(above))))
        recv['w_out'][l] = got[0]
        if above:
            recv['w_in'][l + 1] = got[1]
        gs['hgrn_lb_logits'][l] = dlb[0]
        gs['hgrn_norm_g'][l] = dnga[0]
        (dqkv, dgates, dz, dngb), got = _b_bwd(s['qkv'], s['gates'], proj, w['gdn_norm_g'][l], s['st_b'], s['b_saved'], dmixed,
                                               beside=_Exchange([dw_up, dw_down], ['a2a', 'a2a']))
        recv['w_up'][l], recv['w_down'][l] = got
        dyb, dgi, dcwb, dal, ddt = _b_pre_bwd(proj, gdn_cw[l], s['alr'], s['dtr'], dqkv, dgates)
        dxb = _conv_bwd_x(dyb, gdn_cw[l], B_WIDTH, name="conv_bwd_x_gdn")
        gs['gdn_norm_g'][l] = dngb[0]
        gs['gdn_conv_w'][l] = dcwb
        gs['gdn_a_log'][l] = dal[0, B_HEADS:2 * B_HEADS]
        gs['gdn_dt_bias'][l] = ddt[0, B_HEADS:2 * B_HEADS]
        dxc, dyg, dcwc, dcb, dwa, dba, dwx, dbx, dlam = _c_bwd(proj, *c_args(l), s['hs'], dmixed)
        dxc_in = _conv_bwd_x(dxc, lru_cw[l], C_WIDTH, name="conv_bwd_x_lru")
        gs['lru_conv_w'][l] = dcwc
        gs['lru_conv_b'][l] = dcb[0]
        gs['lru_w_a'][l] = _diag_blocks(dwa)
        gs['lru_b_a'][l] = dba[0]
        gs['lru_w_x'][l] = _diag_blocks(dwx)
        gs['lru_b_x'][l] = dbx[0]
        gs['lru_lambda'][l] = dlam[0]
        pad = jnp.zeros((t, D_IN_PAD - COL_G - 128), BF16)
        dproj = jnp.concatenate([dpa, dxb, dz, dxc_in, dyg, dgi, pad], axis=1)
        dx, dg1 = _mm_rows(dproj, w_in[l], "nt", 512, "mm_dh", epilogue="rms_bwd", norm=(s['x'], w['norm1_g'][l], dx_mid))
        dw_in = _unpermute_w_in(_mm_tn(s['h'], dproj, 1280, 1024, "mm_dw_in"))
        dw_in_above = jnp.moveaxis(dw_in.reshape(D_MODEL, N_DEV, D_IN // N_DEV), 1, 0)
        gs['norm1_g'][l] = dg1[0]
    grad_x = dx
    part = {n: jnp.stack(gs[n]) for n in _WEIGHTS if n != 'final_norm_g' and n not in _BIG}
    part['final_norm_g'] = dgf[0]
    part['hgrn_lb_logits'] = _lb_bwd(w['hgrn_lb_logits'], part['hgrn_lb_logits'])

    small = [n for n in _WEIGHTS if n not in _BIG]
    packed = _pack([part[n] for n in small] + [loss])
    recv['w_in'][0], all_small = _run_exchange(_Exchange([dw_in_above, packed], ['a2a', 'slot']), "exchange_last")

    grads, deltas, new_m, new_v = {}, {}, {}, {}
    for n in _BIG:
        shp = w[n].shape
        r2 = lambda a: a.reshape(-1, shp[-1])
        parts = jnp.stack(recv[n], axis=1).reshape(N_DEV, -1, shp[-1])
        g, d, nm, nv = _sum_adamw(parts, r2(w[n]), r2(m[n]), r2(v[n]), 256, "adamw_" + n)
        grads[n], deltas[n], new_m[n], new_v[n] = (a.reshape(shp) for a in (g, d, nm, nv))

    total = _sum_parts(all_small, "sum_small")
    summed = _unpack(total, [part[n].shape for n in small] + [(1, 1)])
    loss_total = summed[-1].reshape(())
    gsmall = dict(zip(small, summed[:-1]))
    for n in _SHARDED_SMALL:
        width = w[n].shape[-1]
        gsmall[n] = lax.dynamic_slice_in_dim(gsmall[n], me * width, width, axis=2)
    pk = lambda d: _pack([d[n] for n in small])
    _, d, nm, nv = _sum_adamw(pk(gsmall)[None], pk(w), pk(m), pk(v), 4096, "adamw_small")
    shapes = [w[n].shape for n in small]
    for n, dd, mm, vv in zip(small, _unpack(d, shapes), _unpack(nm, shapes), _unpack(nv, shapes)):
        grads[n], deltas[n], new_m[n], new_v[n] = gsmall[n], dd, mm, vv
    return loss_total, grad_x, grads, deltas, new_m, new_v


def kernel(x, norm1_g, w_in, hgrn_lb_logits, hgrn_norm_g, gdn_conv_w, gdn_a_log, gdn_dt_bias, gdn_norm_g, lru_conv_w, lru_conv_b, lru_w_a, lru_b_a, lru_w_x, lru_b_x, lru_lambda, w_out, norm2_g, w_up, w_down, final_norm_g, loss_target, m_norm1_g, m_w_in, m_hgrn_lb_logits, m_hgrn_norm_g, m_gdn_conv_w, m_gdn_a_log, m_gdn_dt_bias, m_gdn_norm_g, m_lru_conv_w, m_lru_conv_b, m_lru_w_a, m_lru_b_a, m_lru_w_x, m_lru_b_x, m_lru_lambda, m_w_out, m_norm2_g, m_w_up, m_w_down, m_final_norm_g, v_norm1_g, v_w_in, v_hgrn_lb_logits, v_hgrn_norm_g, v_gdn_conv_w, v_gdn_a_log, v_gdn_dt_bias, v_gdn_norm_g, v_lru_conv_w, v_lru_conv_b, v_lru_w_a, v_lru_b_a, v_lru_w_x, v_lru_b_x, v_lru_lambda, v_w_out, v_norm2_g, v_w_up, v_w_down, v_final_norm_g):
    w = dict(zip(_WEIGHTS, (norm1_g, w_in, hgrn_lb_logits, hgrn_norm_g, gdn_conv_w, gdn_a_log, gdn_dt_bias, gdn_norm_g, lru_conv_w, lru_conv_b, lru_w_a, lru_b_a, lru_w_x, lru_b_x, lru_lambda, w_out, norm2_g, w_up, w_down, final_norm_g)))
    m = dict(zip(_WEIGHTS, (m_norm1_g, m_w_in, m_hgrn_lb_logits, m_hgrn_norm_g, m_gdn_conv_w, m_gdn_a_log, m_gdn_dt_bias, m_gdn_norm_g, m_lru_conv_w, m_lru_conv_b, m_lru_w_a, m_lru_b_a, m_lru_w_x, m_lru_b_x, m_lru_lambda, m_w_out, m_norm2_g, m_w_up, m_w_down, m_final_norm_g)))
    v = dict(zip(_WEIGHTS, (v_norm1_g, v_w_in, v_hgrn_lb_logits, v_hgrn_norm_g, v_gdn_conv_w, v_gdn_a_log, v_gdn_dt_bias, v_gdn_norm_g, v_lru_conv_w, v_lru_conv_b, v_lru_w_a, v_lru_b_a, v_lru_w_x, v_lru_b_x, v_lru_lambda, v_w_out, v_norm2_g, v_w_up, v_w_down, v_final_norm_g)))
    loss, grad_x, grads, deltas, new_m, new_v = _step(x[0], loss_target[0], w, m, v)
    return (loss, grad_x[None], *[grads[n] for n in _WEIGHTS], *[deltas[n] for n in _WEIGHTS],
            *[new_m[n] for n in _WEIGHTS], *[new_v[n] for n in _WEIGHTS])
```

```python
import functools

import jax
import jax.numpy as jnp
from jax import lax
from jax.experimental import pallas as pl
from jax.experimental.pallas import tpu as pltpu

F32 = jnp.float32
BF16 = jnp.bfloat16
HI = lax.Precision.HIGHEST
MESH = pl.DeviceIdType.MESH

N_DEV = 8
D_MODEL = 1024
DEPTH = 4
A_HEADS, A_DIM, A_WIDTH = 4, 64, 256
B_HEADS, B_DIM, B_WIDTH = 4, 128, 512
C_WIDTH, C_BLOCKS, C_BLOCK_DIM = 256, 4, 64
D_IN = 3592
D_IN_PAD = 3840
COL_A, COL_B, COL_C, COL_G = 0, 1024, 3072, 3584
D_FF = 4096
CONV_K = 4
CHUNK = 64
SUB = 16
RG_C = 8.0
EPS = 1e-6
TINY = 1e-30
EXP_CLAMP = 80.0
GDN_SCALE = B_DIM ** -0.5
ADAM_LR, ADAM_B1, ADAM_B2, ADAM_EPS, ADAM_WD, ADAM_STEP = 0.001, 0.9, 0.999, 1e-08, 0.01, 10
VMEM_LIMIT = 56 * 1024 * 1024


def _cparams(sem=None):
    return pltpu.CompilerParams(dimension_semantics=sem, vmem_limit_bytes=VMEM_LIMIT)


_DIMS = {"nn": (((1,), (0,)), ((), ())), "nt": (((1,), (1,)), ((), ())), "tn": (((0,), (0,)), ((), ()))}


def _split_bf16(x):
    hi = x.astype(BF16)
    return hi, (x - hi.astype(F32)).astype(BF16)


def _dot(a, b, mode="nn", hi=False):
    if not hi:
        return lax.dot_general(a.astype(BF16), b.astype(BF16), _DIMS[mode], preferred_element_type=F32)
    ah, al = _split_bf16(a.astype(F32))
    bh, bl = _split_bf16(b.astype(F32))
    ka = 0 if mode == "tn" else 1
    kb = 1 if mode == "nt" else 0
    return lax.dot_general(jnp.concatenate([ah, ah, al], axis=ka), jnp.concatenate([bh, bl, bh], axis=kb),
                           _DIMS[mode], preferred_element_type=F32)


def _dot_exact_lhs(lhs, x, mode="nn"):
    l_bf16 = lhs.astype(BF16)
    x1 = x.astype(BF16)
    r1 = x - x1.astype(F32)
    x2 = r1.astype(BF16)
    x3 = (r1 - x2.astype(F32)).astype(BF16)
    ka = 0 if mode == "tn" else 1
    return lax.dot_general(jnp.concatenate([l_bf16] * 3, axis=ka), jnp.concatenate([x1, x2, x3], axis=0),
                           _DIMS[mode], preferred_element_type=F32)


def _iota2(n, m):
    return lax.broadcasted_iota(jnp.int32, (n, m), 0), lax.broadcasted_iota(jnp.int32, (n, m), 1)


def _tril(n, strict=False):
    r, c = _iota2(n, n)
    return (r > c) if strict else (r >= c)


def _sigmoid(x):
    return 1.0 / (1.0 + jnp.exp(-x))


def _softplus(x):
    return jnp.maximum(x, 0.0) + jnp.log(1.0 + jnp.exp(-jnp.abs(x)))


def _neg_expm1(z):
    series = -z * (1.0 + z * (0.5 + z * (1.0 / 6.0)))
    return jnp.where(z > -1e-2, series, 1.0 - jnp.exp(z))


def _gelu_tanh(x):
    c = 0.7978845608028654
    u = c * (x + 0.044715 * x * x * x)
    t = jnp.tanh(u)
    g = 0.5 * x * (1.0 + t)
    dg = 0.5 * (1.0 + t) + 0.5 * x * (1.0 - t * t) * c * (1.0 + 3.0 * 0.044715 * x * x)
    return g, dg


def _shift_rows(cur, halo, s, down=True):
    n = cur.shape[0]
    ridx = lax.broadcasted_iota(jnp.int32, (8, cur.shape[1]), 0)
    if down:
        main = pltpu.roll(cur, s, 0)
        fix = jnp.where(ridx < s, pltpu.roll(halo, s, 0), main[0:8])
        return jnp.concatenate([fix, main[8:]], axis=0)
    main = pltpu.roll(cur, n - s, 0)
    fix = jnp.where(ridx >= 8 - s, pltpu.roll(halo, 8 - s, 0), main[n - 8:n])
    return jnp.concatenate([main[:n - 8], fix], axis=0)


def _conv_fwd(cur, prev8, w):
    y = cur * w[3:4]
    for j in range(3):
        y = y + _shift_rows(cur, prev8, 3 - j, down=True) * w[j:j + 1]
    return y


def _pick_tile(n, pref):
    best = None
    for cand in range(128, min(n, pref) + 1, 128):
        if n % cand == 0:
            best = cand
    return best if best is not None else n


def _matmul(a, b, mode, out_dtype=F32, residual=None, tm=512, tn=1024, tk=1024, name="matmul"):
    if mode == "nn":
        (m, k), n = a.shape, b.shape[1]
    elif mode == "nt":
        (m, k), n = a.shape, b.shape[0]
    else:
        (k, m), n = a.shape, b.shape[1]
    tm, tn, tk = _pick_tile(m, tm), _pick_tile(n, tn), _pick_tile(k, tk)
    nk = k // tk
    a_spec = pl.BlockSpec((tk, tm), lambda i, j, kk: (kk, i)) if mode == "tn" else pl.BlockSpec((tm, tk), lambda i, j, kk: (i, kk))
    b_spec = pl.BlockSpec((tn, tk), lambda i, j, kk: (j, kk)) if mode == "nt" else pl.BlockSpec((tk, tn), lambda i, j, kk: (kk, j))
    o_spec = pl.BlockSpec((tm, tn), lambda i, j, kk: (i, j))
    has_res = residual is not None

    def body(*refs):
        if has_res:
            a_ref, b_ref, r_ref, o_ref, acc = refs
        else:
            a_ref, b_ref, o_ref, acc = refs
        kk = pl.program_id(2)

        @pl.when(kk == 0)
        def _():
            acc[...] = jnp.zeros_like(acc)

        acc[...] += _dot(a_ref[...], b_ref[...], mode)

        @pl.when(kk == nk - 1)
        def _():
            r = acc[...]
            if has_res:
                r = r + r_ref[...]
            o_ref[...] = r.astype(out_dtype)

    ins = [a, b] + ([residual] if has_res else [])
    specs = [a_spec, b_spec] + ([o_spec] if has_res else [])
    return pl.pallas_call(
        body, grid=(m // tm, n // tn, nk), in_specs=specs, out_specs=o_spec,
        out_shape=jax.ShapeDtypeStruct((m, n), out_dtype), scratch_shapes=[pltpu.VMEM((tm, tn), F32)],
        compiler_params=_cparams(("parallel", "parallel", "arbitrary")), name=name)(*ins)


def _mm_rows(a, w, mode, tm, name, residual=None, epilogue=None, up=None, norm=None):
    parts = list(a) if isinstance(a, (list, tuple)) else [a]
    widths = [p.shape[1] for p in parts]
    t = parts[0].shape[0]
    n = w.shape[1] if mode == "nn" else w.shape[0]
    tm = min(tm, t)
    assert t % tm == 0 and all(wd % 128 == 0 for wd in widths)

    def body(*refs):
        a_refs, w_ref, rest = refs[:len(parts)], refs[len(parts)], refs[len(parts) + 1:]
        y, off = None, 0
        for a_ref, width in zip(a_refs, widths):
            wk = w_ref[off:off + width, :] if mode == "nn" else w_ref[:, off:off + width]
            d = _dot(a_ref[...], wk, mode)
            y = d if y is None else y + d
            off += width
        if residual is not None:
            y = y + rest[0][...]
        if epilogue == "relu2":
            r = jnp.maximum(y, 0.0)
            refs[-2][...] = (r * r).astype(BF16)
            refs[-1][...] = y.astype(BF16)
        elif epilogue == "drelu2":
            refs[-1][...] = (y * 2.0 * jnp.maximum(rest[0][...].astype(F32), 0.0)).astype(BF16)
        elif epilogue == "rms_fwd":
            rinv = lax.rsqrt(jnp.mean(y * y, axis=-1, keepdims=True) + EPS)
            refs[-2][...] = y
            refs[-1][...] = (y * rinv * refs[-3][...]).astype(BF16)
        elif epilogue == "rms_bwd":
            x_ref, g_ref, dres_ref, dx_ref, dg_ref = rest

            @pl.when(pl.program_id(0) == 0)
            def _():
                dg_ref[...] = jnp.zeros_like(dg_ref)

            xv = x_ref[...]
            rinv = lax.rsqrt(jnp.mean(xv * xv, axis=-1, keepdims=True) + EPS)
            xhat = xv * rinv
            dxh = y * g_ref[...]
            dx_ref[...] = dres_ref[...] + rinv * (dxh - xhat * jnp.mean(dxh * xhat, axis=-1, keepdims=True))
            dg_ref[...] += jnp.sum(y * xhat, axis=0, keepdims=True)
        else:
            refs[-1][...] = y

    rows = lambda width: pl.BlockSpec((tm, width), lambda i: (i, 0))
    vec = pl.BlockSpec((1, n), lambda i: (0, 0))
    ins, specs = parts + [w], [rows(wd) for wd in widths] + [pl.BlockSpec(w.shape, lambda i: (0, 0))]
    if residual is not None:
        ins.append(residual)
        specs.append(rows(n))
    if epilogue == "drelu2":
        ins.append(up)
        specs.append(rows(n))
    if epilogue == "rms_fwd":
        ins.append(norm.reshape(1, n))
        specs.append(vec)
        out_specs, out_shape = [rows(n), rows(n)], [jax.ShapeDtypeStruct((t, n), F32), jax.ShapeDtypeStruct((t, n), BF16)]
    elif epilogue == "rms_bwd":
        ins += [norm[0], norm[1].reshape(1, n), norm[2]]
        specs += [rows(n), vec, rows(n)]
        out_specs, out_shape = [rows(n), vec], [jax.ShapeDtypeStruct((t, n), F32), jax.ShapeDtypeStruct((1, n), F32)]
    elif epilogue == "relu2":
        out_specs, out_shape = [rows(n), rows(n)], [jax.ShapeDtypeStruct((t, n), BF16)] * 2
    else:
        out_specs, out_shape = rows(n), jax.ShapeDtypeStruct((t, n), BF16 if epilogue == "drelu2" else F32)
    return pl.pallas_call(body, grid=(t // tm,), in_specs=specs, out_specs=out_specs, out_shape=out_shape,
                          compiler_params=_cparams(("arbitrary" if epilogue == "rms_bwd" else "parallel",)), name=name)(*ins)


MM_TN_TILE = 1024


def _mm_tn(a, b, tk, name, slab=None):
    a_parts = list(a) if isinstance(a, (list, tuple)) else [a]
    b_parts = list(b) if isinstance(b, (list, tuple)) else [b]
    wa, wb = [p.shape[1] for p in a_parts], [p.shape[1] for p in b_parts]
    t, m, n = a_parts[0].shape[0], sum(wa), sum(wb)
    tk = min(tk, t)
    assert t % tk == 0 and all(x % 128 == 0 for x in wa + wb)
    nk = t // tk

    def body(*refs):
        a_refs, b_refs = refs[:len(wa)], refs[len(wa):len(wa) + len(wb)]
        o_ref, acc = refs[-2], refs[-1]
        kk = pl.program_id(0)

        @pl.when(kk == 0)
        def _():
            acc[...] = jnp.zeros_like(acc)

        ro = 0
        for a_ref, width_a in zip(a_refs, wa):
            for r0 in range(0, width_a, MM_TN_TILE):
                rw = min(MM_TN_TILE, width_a - r0)
                av = a_ref[:, r0:r0 + rw]
                co = 0
                for b_ref, width_b in zip(b_refs, wb):
                    for c0 in range(0, width_b, MM_TN_TILE):
                        cw = min(MM_TN_TILE, width_b - c0)
                        acc[ro + r0:ro + r0 + rw, co + c0:co + c0 + cw] += _dot(av, b_ref[:, c0:c0 + cw], "tn")
                    co += width_b
            ro += width_a

        @pl.when(kk == nk - 1)
        def _():
            if slab is None:
                o_ref[...] = acc[...].astype(BF16)
            else:
                for s in range(n // slab):
                    o_ref[s] = acc[:, s * slab:(s + 1) * slab].astype(BF16)

    if slab is None:
        out_spec, out_shape = pl.BlockSpec((m, n), lambda kk: (0, 0)), jax.ShapeDtypeStruct((m, n), BF16)
    else:
        out_spec, out_shape = pl.BlockSpec((n // slab, m, slab), lambda kk: (0, 0, 0)), jax.ShapeDtypeStruct((n // slab, m, slab), BF16)
    return pl.pallas_call(
        body, grid=(nk,),
        in_specs=[pl.BlockSpec((tk, x), lambda kk: (kk, 0)) for x in wa + wb],
        out_specs=out_spec, out_shape=out_shape, scratch_shapes=[pltpu.VMEM((m, n), F32)],
        compiler_params=_cparams(("arbitrary",)), name=name)(*a_parts, *b_parts)


def _rms_fwd(x, g, tb=512, name="rms_fwd"):
    t, d = x.shape

    def body(x_ref, g_ref, h_ref):
        xv = x_ref[...]
        rinv = lax.rsqrt(jnp.mean(xv * xv, axis=-1, keepdims=True) + EPS)
        h_ref[...] = (xv * rinv * g_ref[...]).astype(BF16)

    return pl.pallas_call(
        body, grid=(t // tb,), in_specs=[pl.BlockSpec((tb, d), lambda i: (i, 0)), pl.BlockSpec((1, d), lambda i: (0, 0))],
        out_specs=pl.BlockSpec((tb, d), lambda i: (i, 0)), out_shape=jax.ShapeDtypeStruct((t, d), BF16),
        compiler_params=_cparams(("parallel",)), name=name)(x, g.reshape(1, d))


def _rms_bwd(dh, x, g, dres, tb=512, name="rms_bwd"):
    t, d = x.shape

    def body(dh_ref, x_ref, g_ref, dres_ref, dx_ref, dg_ref):
        @pl.when(pl.program_id(0) == 0)
        def _():
            dg_ref[...] = jnp.zeros_like(dg_ref)

        xv = x_ref[...]
        dhv = dh_ref[...].astype(F32)
        rinv = lax.rsqrt(jnp.mean(xv * xv, axis=-1, keepdims=True) + EPS)
        xhat = xv * rinv
        dxh = dhv * g_ref[...]
        dx_ref[...] = dres_ref[...] + rinv * (dxh - xhat * jnp.mean(dxh * xhat, axis=-1, keepdims=True))
        dg_ref[...] += jnp.sum(dhv * xhat, axis=0, keepdims=True)

    row = pl.BlockSpec((tb, d), lambda i: (i, 0))
    vec = pl.BlockSpec((1, d), lambda i: (0, 0))
    return pl.pallas_call(
        body, grid=(t // tb,), in_specs=[row, row, vec, row], out_specs=[row, vec],
        out_shape=[jax.ShapeDtypeStruct((t, d), F32), jax.ShapeDtypeStruct((1, d), F32)],
        compiler_params=_cparams(("arbitrary",)), name=name)(dh, x, g.reshape(1, d), dres)


def _loss_head(x, g, target, tb=512):
    t, d = x.shape

    def body(x_ref, g_ref, t_ref, loss_ref, dx_ref, dg_ref):
        @pl.when(pl.program_id(0) == 0)
        def _():
            dg_ref[...] = jnp.zeros_like(dg_ref)
            loss_ref[...] = jnp.zeros_like(loss_ref)

        xv = x_ref[...]
        rinv = lax.rsqrt(jnp.mean(xv * xv, axis=-1, keepdims=True) + EPS)
        xhat = xv * rinv
        err = xhat * g_ref[...] - t_ref[...]
        loss_ref[...] += 0.5 * jnp.sum(jnp.mean(err * err, axis=-1, keepdims=True), axis=0, keepdims=True)
        dy = err * (1.0 / d)
        dxh = dy * g_ref[...]
        dx_ref[...] = rinv * (dxh - xhat * jnp.mean(dxh * xhat, axis=-1, keepdims=True))
        dg_ref[...] += jnp.sum(dy * xhat, axis=0, keepdims=True)

    row = pl.BlockSpec((tb, d), lambda i: (i, 0))
    vec = pl.BlockSpec((1, d), lambda i: (0, 0))
    one = pl.BlockSpec((1, 1), lambda i: (0, 0))
    return pl.pallas_call(
        body, grid=(t // tb,), in_specs=[row, vec, row], out_specs=[one, row, vec],
        out_shape=[jax.ShapeDtypeStruct((1, 1), F32), jax.ShapeDtypeStruct((t, d), F32), jax.ShapeDtypeStruct((1, d), F32)],
        compiler_params=_cparams(("arbitrary",)), name="loss_head")(x, g.reshape(1, d), target)


def _relu2_fwd(up, tb=512):
    t, d = up.shape

    def body(u_ref, a_ref):
        r = jnp.maximum(u_ref[...], 0.0)
        a_ref[...] = (r * r).astype(BF16)

    row = pl.BlockSpec((tb, d), lambda i: (i, 0))
    return pl.pallas_call(body, grid=(t // tb,), in_specs=[row], out_specs=row, out_shape=jax.ShapeDtypeStruct((t, d), BF16),
                          compiler_params=_cparams(("parallel",)), name="relu2_fwd")(up)


def _relu2_bwd(dact, up, tb=512):
    t, d = up.shape

    def body(da_ref, u_ref, o_ref):
        o_ref[...] = (da_ref[...] * 2.0 * jnp.maximum(u_ref[...], 0.0)).astype(BF16)

    row = pl.BlockSpec((tb, d), lambda i: (i, 0))
    return pl.pallas_call(body, grid=(t // tb,), in_specs=[row, row], out_specs=row, out_shape=jax.ShapeDtypeStruct((t, d), BF16),
                          compiler_params=_cparams(("parallel",)), name="relu2_bwd")(dact, up)


def _lb_fwd(logits):
    def body(l_ref, o_ref):
        lg = l_ref[...]
        e = jnp.exp(lg - jnp.max(lg, axis=0, keepdims=True))
        p = e / jnp.sum(e, axis=0, keepdims=True)
        c = jnp.zeros_like(p[0:1])
        rows = [c]
        for l in range(1, DEPTH):
            c = c + p[l:l + 1]
            rows.append(c)
        o_ref[...] = jnp.minimum(jnp.maximum(jnp.concatenate(rows, axis=0), 0.0), 1.0 - EPS)

    return pl.pallas_call(body, out_shape=jax.ShapeDtypeStruct(logits.shape, F32), name="lb_fwd")(logits)


def _lb_bwd(logits, dlb):
    def body(l_ref, d_ref, o_ref):
        lg = l_ref[...]
        e = jnp.exp(lg - jnp.max(lg, axis=0, keepdims=True))
        p = e / jnp.sum(e, axis=0, keepdims=True)
        hi = 1.0 - EPS
        c = jnp.zeros_like(p[0:1])
        dc = []
        for l in range(1, DEPTH):
            c = c + p[l:l + 1]
            gl = jnp.where(c < 0.0, 0.0, jnp.where(c == 0.0, 0.5, 1.0)) * jnp.where(c > hi, 0.0, jnp.where(c == hi, 0.5, 1.0))
            dc.append(d_ref[l:l + 1, :] * gl)
        dp = [jnp.zeros_like(c)]
        for j in range(1, DEPTH):
            s = dc[j - 1]
            for l in range(j + 1, DEPTH):
                s = s + dc[l - 1]
            dp.append(s)
        dpm = jnp.concatenate(dp, axis=0)
        o_ref[...] = p * (dpm - jnp.sum(p * dpm, axis=0, keepdims=True))

    return pl.pallas_call(body, out_shape=jax.ShapeDtypeStruct(logits.shape, F32), name="lb_bwd")(logits, dlb)


def _a_gates(qi, fi, lbh):
    sq = _sigmoid(qi)
    q = qi * sq
    sg = _sigmoid(fi)
    sgn = _sigmoid(-fi)
    f = lbh + (1.0 - lbh) * sg
    logf = jnp.log(jnp.maximum(f, TINY))
    k = (1.0 - lbh) * sgn
    return q, sq, sg, sgn, f, logf, k


def _a_intra(q, k, cum):
    qts, kes, rows = [], [], []
    for i in range(CHUNK // SUB):
        lo = i * SUB
        r = cum[lo - 1:lo] if i > 0 else jnp.zeros_like(cum[0:1])
        eq = jnp.exp(cum[lo:lo + SUB] - r)
        ek = jnp.exp(jnp.minimum(r - cum, EXP_CLAMP))
        qt = q[lo:lo + SUB] * eq
        rows.append(_dot(qt, k * ek, "nt", hi=True))
        qts.append((qt, eq))
        kes.append(ek)
    attn = jnp.where(_tril(CHUNK), jnp.concatenate(rows, axis=0), 0.0)
    return attn, qts, kes


def _a_intra_bwd(dattn, k, qts, kes):
    dq_rows = []
    dk = jnp.zeros_like(k)
    for i in range(CHUNK // SUB):
        lo = i * SUB
        da = dattn[lo:lo + SUB]
        qt, eq = qts[i]
        dq_rows.append(_dot(da, k * kes[i], "nn", hi=True) * eq)
        dk = dk + _dot(da, qt, "tn", hi=True) * kes[i]
    return jnp.concatenate(dq_rows, axis=0), dk


def _headnorm_fwd(o, g, gate_in):
    rinv = lax.rsqrt(jnp.mean(o * o, axis=-1, keepdims=True) + EPS)
    sg = _sigmoid(gate_in)
    return o * rinv * g * (gate_in * sg)


def _headnorm_bwd(dout, o, g, gate_in):
    rinv = lax.rsqrt(jnp.mean(o * o, axis=-1, keepdims=True) + EPS)
    xhat = o * rinv
    sg = _sigmoid(gate_in)
    silu = gate_in * sg
    dy = dout * silu
    dgate = dout * xhat * g * (sg * (1.0 + gate_in * (1.0 - sg)))
    dxh = dy * g
    do = rinv * (dxh - xhat * jnp.mean(dxh * xhat, axis=-1, keepdims=True))
    return do, dgate, jnp.sum(dy * xhat, axis=0, keepdims=True)


def _a_fwd(proj, lb, norm_g, tb=256):
    t = proj.shape[0]
    nch = tb // CHUNK

    def body(q_ref, f_ref, i_ref, g_ref, lb_ref, ng_ref, out_ref, st_ref, s_scr):
        @pl.when(pl.program_id(0) == 0)
        def _():
            s_scr[...] = jnp.zeros_like(s_scr)

        ltri = _tril(CHUNK).astype(F32)

        def chunk(c, carry):
            rows = pl.ds(pl.multiple_of(c * CHUNK, CHUNK), CHUNK)
            hs = range(A_HEADS)
            cols = [slice(h * A_DIM, (h + 1) * A_DIM) for h in hs]
            gates = [_a_gates(q_ref[rows, cols[h]], f_ref[rows, cols[h]], lb_ref[:, cols[h]]) for h in hs]
            q, k = [gates[h][0] for h in hs], [gates[h][6] for h in hs]
            v = [i_ref[rows, cols[h]] for h in hs]
            cum = [_dot_exact_lhs(ltri, gates[h][5]) for h in hs]
            cl = [cum[h][CHUNK - 1:CHUNK] for h in hs]
            s0 = [s_scr[h] for h in hs]
            for h in hs:
                st_ref[c, h] = s0[h]
            attn = [_a_intra(q[h], k[h], cum[h])[0] for h in hs]
            qs0 = [_dot(q[h] * jnp.exp(cum[h]), s0[h]) for h in hs]
            o = [qs0[h] + _dot(attn[h], v[h]) for h in hs]
            kd = [k[h] * jnp.exp(cl[h] - cum[h]) for h in hs]
            for h in hs:
                s_scr[h] = s0[h] * jnp.exp(cl[h]).T + _dot(kd[h], v[h], "tn")
            outs = [_headnorm_fwd(o[h], ng_ref[...], g_ref[rows, cols[h]]) for h in hs]
            out_ref[rows, :] = jnp.concatenate(outs, axis=1).astype(BF16)
            return carry

        lax.fori_loop(0, nch, chunk, 0, unroll=2)

    colblk = lambda j: pl.BlockSpec((tb, A_WIDTH), lambda i, j=j: (i, j))
    return pl.pallas_call(
        body, grid=(t // tb,),
        in_specs=[colblk(0), colblk(1), colblk(2), colblk(3), pl.BlockSpec((1, A_WIDTH), lambda i: (0, 0)),
                  pl.BlockSpec((1, A_DIM), lambda i: (0, 0))],
        out_specs=[pl.BlockSpec((tb, A_WIDTH), lambda i: (i, 0)),
                   pl.BlockSpec((nch, A_HEADS, A_DIM, A_DIM), lambda i: (i, 0, 0, 0))],
        out_shape=[jax.ShapeDtypeStruct((t, A_WIDTH), BF16), jax.ShapeDtypeStruct((t // CHUNK, A_HEADS, A_DIM, A_DIM), F32)],
        scratch_shapes=[pltpu.VMEM((A_HEADS, A_DIM, A_DIM), F32)],
        compiler_params=_cparams(("arbitrary",)), name="hgrn_fwd")(proj, proj, proj, proj, lb.reshape(1, A_WIDTH), norm_g.reshape(1, A_DIM))


def _a_bwd(proj, lb, norm_g, states, dmixed, beside=None, tb=256):
    t = proj.shape[0]
    nch = tb // CHUNK
    nb = t // tb

    def body(q_ref, f_ref, i_ref, g_ref, lb_ref, ng_ref, st_ref, dm_ref, dp_ref, dlb_ref, dng_ref, ds_scr):
        @pl.when(pl.program_id(0) == 0)
        def _():
            ds_scr[...] = jnp.zeros_like(ds_scr)
            dlb_ref[...] = jnp.zeros_like(dlb_ref)
            dng_ref[...] = jnp.zeros_like(dng_ref)

        ltri = _tril(CHUNK).astype(F32)
        mask = _tril(CHUNK)

        def chunk(cc, carry):
            c = nch - 1 - cc
            rows = pl.ds(pl.multiple_of(c * CHUNK, CHUNK), CHUNK)
            hs = range(A_HEADS)
            cols = [slice(h * A_DIM, (h + 1) * A_DIM) for h in hs]
            qi = [q_ref[rows, cols[h]] for h in hs]
            gi = [g_ref[rows, cols[h]] for h in hs]
            lbh = [lb_ref[:, cols[h]] for h in hs]
            gates = [_a_gates(qi[h], f_ref[rows, cols[h]], lbh[h]) for h in hs]
            q, sq, sg, sgn, f, logf, k = ([gates[h][j] for h in hs] for j in range(7))
            v = [i_ref[rows, cols[h]] for h in hs]
            cum = [_dot_exact_lhs(ltri, logf[h]) for h in hs]
            cl = [cum[h][CHUNK - 1:CHUNK] for h in hs]
            ecum = [jnp.exp(cum[h]) for h in hs]
            ekd = [jnp.exp(cl[h] - cum[h]) for h in hs]
            cd = [jnp.exp(cl[h]) for h in hs]
            qd = [q[h] * ecum[h] for h in hs]
            kd = [k[h] * ekd[h] for h in hs]
            s0 = [st_ref[c, h] for h in hs]
            ds = [ds_scr[h] for h in hs]
            intra = [_a_intra(q[h], k[h], cum[h]) for h in hs]
            attn = [intra[h][0] for h in hs]
            qs0 = [_dot(qd[h], s0[h]) for h in hs]
            o = [qs0[h] + _dot(attn[h], v[h]) for h in hs]
            hn = [_headnorm_bwd(dm_ref[rows, cols[h]].astype(F32), o[h], ng_ref[...], gi[h]) for h in hs]
            do = [hn[h][0] for h in hs]
            dqd = [_dot(do[h], s0[h], "nt") for h in hs]
            dattn = [jnp.where(mask, _dot(do[h], v[h], "nt"), 0.0) for h in hs]
            dv = [_dot(attn[h], do[h], "tn") + _dot(kd[h], ds[h]) for h in hs]
            dkd = [_dot(v[h], ds[h], "nt") for h in hs]
            dcd = [jnp.sum((s0[h] * ds[h]).T, axis=0, keepdims=True) for h in hs]
            for h in hs:
                ds_scr[h] = _dot(qd[h], do[h], "tn") + ds[h] * cd[h].T
            ib = [_a_intra_bwd(dattn[h], k[h], intra[h][1], intra[h][2]) for h in hs]
            dq = [dqd[h] * ecum[h] + ib[h][0] for h in hs]
            dk = [dkd[h] * ekd[h] + ib[h][1] for h in hs]
            dkk = [dkd[h] * kd[h] for h in hs]
            dcum = [dqd[h] * qd[h] - dkk[h] + q[h] * ib[h][0] - k[h] * ib[h][1] for h in hs]
            dcl = [jnp.sum(dkk[h], axis=0, keepdims=True) + dcd[h] * cd[h] for h in hs]
            dlogf = [_dot_exact_lhs(ltri, dcum[h], "tn") + dcl[h] for h in hs]
            dfv = [jnp.where(f[h] > TINY, dlogf[h] / f[h], 0.0) for h in hs]
            dfi = [dfv[h] * (1.0 - lbh[h]) * sg[h] * (1.0 - sg[h]) - dk[h] * (1.0 - lbh[h]) * sgn[h] * (1.0 - sgn[h]) for h in hs]
            dlbs = [jnp.sum(dfv[h] * (1.0 - sg[h]) - dk[h] * sgn[h], axis=0, keepdims=True) for h in hs]
            dqs = [dq[h] * (sq[h] * (1.0 + qi[h] * (1.0 - sq[h]))) for h in hs]
            dp_ref[rows, :] = jnp.concatenate(dqs + dfi + dv + [hn[h][1] for h in hs], axis=1).astype(BF16)
            dlb_ref[...] += jnp.concatenate(dlbs, axis=1)
            dng_ref[...] += sum(hn[h][2] for h in hs)
            return carry

        lax.fori_loop(0, nch, chunk, 0, unroll=2)

    colblk = lambda j: pl.BlockSpec((tb, A_WIDTH), lambda i, j=j: (nb - 1 - i, j))
    vec = lambda n: pl.BlockSpec((1, n), lambda i: (0, 0))
    return _call_beside(
        body, beside, nb, (proj, proj, proj, proj, lb.reshape(1, A_WIDTH), norm_g.reshape(1, A_DIM), states, dmixed), grid=(nb,),
        in_specs=[colblk(0), colblk(1), colblk(2), colblk(3), vec(A_WIDTH), vec(A_DIM),
                  pl.BlockSpec((nch, A_HEADS, A_DIM, A_DIM), lambda i: (nb - 1 - i, 0, 0, 0)), colblk(0)],
        out_specs=[pl.BlockSpec((tb, 4 * A_WIDTH), lambda i: (nb - 1 - i, 0)), vec(A_WIDTH), vec(A_DIM)],
        out_shape=[jax.ShapeDtypeStruct((t, 4 * A_WIDTH), BF16), jax.ShapeDtypeStruct((1, A_WIDTH), F32), jax.ShapeDtypeStruct((1, A_DIM), F32)],
        scratch_shapes=[pltpu.VMEM((A_HEADS, A_DIM, A_DIM), F32)], name="hgrn_bwd")


def _gate_lane_masks(shape):
    lane = lax.broadcasted_iota(jnp.int32, shape, 1)
    return lane < B_HEADS, (lane >= B_HEADS) & (lane < 2 * B_HEADS)


def _b_pre_fwd(proj, conv_w, alog_row, dtb_row, tb=512):
    t = proj.shape[0]
    cb0 = COL_B // B_WIDTH

    def body(q_ref, k_ref, v_ref, qp_ref, kp_ref, vp_ref, w_ref, gi_ref, al_ref, dt_ref, qkv_ref, gates_ref):
        first = pl.program_id(0) == 0
        for part, (c_ref, p_ref) in enumerate(((q_ref, qp_ref), (k_ref, kp_ref), (v_ref, vp_ref))):
            cols = slice(part * B_WIDTH, (part + 1) * B_WIDTH)
            prev = jnp.where(first, 0.0, p_ref[...])
            y = _conv_fwd(c_ref[...], prev, w_ref[:, cols])
            s = y * _sigmoid(y)
            if part < 2:
                outs = []
                for h in range(B_HEADS):
                    sh = s[:, h * B_DIM:(h + 1) * B_DIM]
                    outs.append(sh * lax.rsqrt(jnp.sum(sh * sh, axis=-1, keepdims=True) + EPS))
                s = jnp.concatenate(outs, axis=1)
            qkv_ref[:, cols] = s
        g = gi_ref[...]
        is_b, is_a = _gate_lane_masks(g.shape)
        la = -jnp.exp(al_ref[...]) * _softplus(g + dt_ref[...])
        gates_ref[...] = jnp.where(is_b, _sigmoid(g), jnp.where(is_a, la, 0.0))

    cur = lambda j: pl.BlockSpec((tb, B_WIDTH), lambda i, j=j: (i, cb0 + j))
    prv = lambda j: pl.BlockSpec((8, B_WIDTH), lambda i, j=j: (jnp.maximum(i * (tb // 8) - 1, 0), cb0 + j))
    vec = pl.BlockSpec((1, 128), lambda i: (0, 0))
    return pl.pallas_call(
        body, grid=(t // tb,),
        in_specs=[cur(0), cur(1), cur(2), prv(0), prv(1), prv(2), pl.BlockSpec((CONV_K, 3 * B_WIDTH), lambda i: (0, 0)),
                  pl.BlockSpec((tb, 128), lambda i: (i, COL_G // 128)), vec, vec],
        out_specs=[pl.BlockSpec((tb, 3 * B_WIDTH), lambda i: (i, 0)), pl.BlockSpec((tb, 128), lambda i: (i, 0))],
        out_shape=[jax.ShapeDtypeStruct((t, 3 * B_WIDTH), F32), jax.ShapeDtypeStruct((t, 128), F32)],
        compiler_params=_cparams(("parallel",)), name="gdn_pre_fwd")(proj, proj, proj, proj, proj, proj, conv_w, proj, alog_row, dtb_row)


def _inv_unit_lower(amats):
    r, c = _iota2(CHUNK, CHUNK)
    eye = jnp.where(r == c, 1.0, 0.0)
    ps = [eye - a for a in amats]
    aks = amats
    for _ in range(5):
        aks = [_dot(ak, ak, hi=True) for ak in aks]
        ps = [p + _dot(p, ak, hi=True) for p, ak in zip(ps, aks)]
    return ps


def _b_local(qs, ks, vs, betas, gcs, grows, gls, solve=True):
    hs = range(len(qs))
    causal, strict = _tril(CHUNK), _tril(CHUNK, strict=True)
    decay = [jnp.where(causal, jnp.exp(jnp.minimum(gcs[h] - grows[h], 0.0)), 0.0) for h in hs]
    kb = [ks[h] * betas[h] for h in hs]
    kk = [_dot(kb[h], ks[h], "nt") for h in hs]
    qkr = [_dot(qs[h], ks[h], "nt") for h in hs]
    eg = [jnp.exp(gcs[h]) for h in hs]
    bv = [vs[h] * betas[h] for h in hs]
    kg = [kb[h] * eg[h] for h in hs]
    qk = [qkr[h] * decay[h] for h in hs]
    qd = [qs[h] * eg[h] for h in hs]
    ekd = [jnp.exp(gls[h] - gcs[h]) for h in hs]
    kd = [ks[h] * ekd[h] for h in hs]
    cd = [jnp.exp(gls[h]) for h in hs]
    loc = dict(decay=decay, kb=kb, kk=kk, eg=eg, bv=bv, kg=kg, qkr=qkr, qk=qk, qd=qd, ekd=ekd, kd=kd, cd=cd)
    if solve:
        tinv = _inv_unit_lower([jnp.where(strict, kk[h] * decay[h], 0.0) for h in hs])
        loc.update(tinv=tinv, u=[_dot(tinv[h], bv[h], hi=True) for h in hs], w=[_dot(tinv[h], kg[h], hi=True) for h in hs])
    return loc


def _b_state(loc, ids, s0s):
    n = range(len(ids))
    ws = [_dot(loc["w"][ids[j]], s0s[j]) for j in n]
    qs0 = [_dot(loc["qd"][ids[j]], s0s[j]) for j in n]
    vn = [loc["u"][ids[j]] - ws[j] for j in n]
    o = [qs0[j] + _dot(loc["qk"][ids[j]], vn[j]) for j in n]
    s1 = [s0s[j] * loc["cd"][ids[j]] + _dot(loc["kd"][ids[j]], vn[j], "tn") for j in n]
    return vn, o, s1


def _b_fwd(qkv, gates, proj, norm_g, beside=None, tb=256):
    t = qkv.shape[0]
    nch = tb // CHUNK

    def body(q_ref, k_ref, v_ref, ga_ref, z_ref, ng_ref, out_ref, st_ref, ti_ref, w_ref, vn_ref, o_ref, s_scr):
        @pl.when(pl.program_id(0) == 0)
        def _():
            s_scr[...] = jnp.zeros_like(s_scr)

        ltri = _tril(CHUNK).astype(F32)

        hs = range(B_HEADS)
        cols = [slice(h * B_DIM, (h + 1) * B_DIM) for h in hs]

        def pair(p, carry):
            cs = [2 * p, 2 * p + 1]
            rows = [pl.ds(pl.multiple_of(c * CHUNK, CHUNK), CHUNK) for c in cs]
            ga = [ga_ref[r, :] for r in rows]
            gcum = [_dot_exact_lhs(ltri, g) for g in ga]
            gcum_t = [g.T for g in gcum]
            items = [(i, h) for i in range(2) for h in hs]
            loc = _b_local([q_ref[rows[i], cols[h]] * GDN_SCALE for i, h in items], [k_ref[rows[i], cols[h]] for i, h in items],
                           [v_ref[rows[i], cols[h]] for i, h in items], [ga[i][:, h:h + 1] for i, h in items],
                           [gcum[i][:, B_HEADS + h:B_HEADS + h + 1] for i, h in items],
                           [gcum_t[i][B_HEADS + h:B_HEADS + h + 1, :] for i, h in items],
                           [gcum[i][CHUNK - 1:CHUNK, B_HEADS + h:B_HEADS + h + 1] for i, h in items])
            s0s = [s_scr[h] for h in hs]
            for i in range(2):
                ids = [i * B_HEADS + h for h in hs]
                for h in hs:
                    st_ref[cs[i], h] = s0s[h]
                    ti_ref[cs[i], h] = loc["tinv"][ids[h]]
                vn, o, s0s = _b_state(loc, ids, s0s)
                w_ref[rows[i], :] = jnp.concatenate([loc["w"][j] for j in ids], axis=1)
                vn_ref[rows[i], :] = jnp.concatenate(vn, axis=1)
                o_ref[rows[i], :] = jnp.concatenate(o, axis=1)
                outs = [_headnorm_fwd(o[h], ng_ref[...], z_ref[rows[i], cols[h]]) for h in hs]
                out_ref[rows[i], :] = jnp.concatenate(outs, axis=1).astype(BF16)
            for h in hs:
                s_scr[h] = s0s[h]
            return carry

        lax.fori_loop(0, nch // 2, pair, 0)

    part = lambda j: pl.BlockSpec((tb, B_WIDTH), lambda i, j=j: (i, j))
    wide = pl.BlockSpec((tb, B_WIDTH), lambda i: (i, 0))
    wide_shape = jax.ShapeDtypeStruct((t, B_WIDTH), F32)
    return _call_beside(
        body, beside, t // tb, (qkv, qkv, qkv, gates, proj, norm_g.reshape(1, B_DIM)), grid=(t // tb,),
        in_specs=[part(0), part(1), part(2), pl.BlockSpec((tb, 128), lambda i: (i, 0)),
                  pl.BlockSpec((tb, B_WIDTH), lambda i: (i, COL_B // B_WIDTH + 3)), pl.BlockSpec((1, B_DIM), lambda i: (0, 0))],
        out_specs=[wide, pl.BlockSpec((nch, B_HEADS, B_DIM, B_DIM), lambda i: (i, 0, 0, 0)),
                   pl.BlockSpec((nch, B_HEADS, CHUNK, CHUNK), lambda i: (i, 0, 0, 0)), wide, wide, wide],
        out_shape=[jax.ShapeDtypeStruct((t, B_WIDTH), BF16), jax.ShapeDtypeStruct((t // CHUNK, B_HEADS, B_DIM, B_DIM), F32),
                   jax.ShapeDtypeStruct((t // CHUNK, B_HEADS, CHUNK, CHUNK), F32), wide_shape, wide_shape, wide_shape],
        scratch_shapes=[pltpu.VMEM((B_HEADS, B_DIM, B_DIM), F32)], name="gdn_fwd")


def _b_bwd(qkv, gates, proj, norm_g, states, fwd_saved, dmixed, beside=None, tb=256):
    t = qkv.shape[0]
    nch = tb // CHUNK
    nb = t // tb

    def body(q_ref, k_ref, v_ref, ga_ref, z_ref, ng_ref, st_ref, ti_ref, w_ref, vn_ref, o_ref, dm0_ref, dm1_ref,
             dqkv_ref, dga_ref, dz_ref, dng_ref, ds_scr):
        @pl.when(pl.program_id(0) == 0)
        def _():
            ds_scr[...] = jnp.zeros_like(ds_scr)
            dng_ref[...] = jnp.zeros_like(dng_ref)

        ltri = _tril(CHUNK).astype(F32)
        strict = _tril(CHUNK, strict=True)
        lane = lax.broadcasted_iota(jnp.int32, (CHUNK, 128), 1)
        lane1 = lax.broadcasted_iota(jnp.int32, (1, 128), 1)

        nh = range(B_HEADS)
        cols = [slice(h * B_DIM, (h + 1) * B_DIM) for h in nh]
        rsum = lambda a: jnp.sum(a, axis=-1, keepdims=True)

        def pair(p, carry):
            cs = [nch - 1 - 2 * p, nch - 2 - 2 * p]
            crow = [pl.ds(pl.multiple_of(c * CHUNK, CHUNK), CHUNK) for c in cs]
            gas = [ga_ref[r, :] for r in crow]
            gcum = [_dot_exact_lhs(ltri, g) for g in gas]
            gcum_t = [g.T for g in gcum]
            items = [(i, h) for i in range(2) for h in nh]
            hs = range(len(items))
            q = [q_ref[crow[i], cols[h]] * GDN_SCALE for i, h in items]
            k = [k_ref[crow[i], cols[h]] for i, h in items]
            v = [v_ref[crow[i], cols[h]] for i, h in items]
            z = [z_ref[crow[i], cols[h]] for i, h in items]
            beta = [gas[i][:, h:h + 1] for i, h in items]
            s0 = [st_ref[cs[i], h] for i, h in items]
            r = _b_local(q, k, v, beta, [gcum[i][:, B_HEADS + h:B_HEADS + h + 1] for i, h in items],
                         [gcum_t[i][B_HEADS + h:B_HEADS + h + 1, :] for i, h in items],
                         [gcum[i][CHUNK - 1:CHUNK, B_HEADS + h:B_HEADS + h + 1] for i, h in items], solve=False)
            tinv = [ti_ref[cs[i], h] for i, h in items]
            w = [w_ref[crow[i], cols[h]] for i, h in items]
            vn = [vn_ref[crow[i], cols[h]] for i, h in items]
            decay, eg, qd, kd, kb, cd = (r[n] for n in ("decay", "eg", "qd", "kd", "kb", "cd"))
            dms = [(dm0_ref if h < 2 else dm1_ref)[crow[i], (h % 2) * B_DIM:(h % 2 + 1) * B_DIM].astype(F32) for i, h in items]
            hn = [_headnorm_bwd(dms[j], o_ref[crow[i], cols[h]], ng_ref[...], z[j]) for j, (i, h) in enumerate(items)]
            do = [hn[j][0] for j in hs]
            dvn_o = [_dot(r["qk"][j], do[j], "tn") for j in hs]
            dqk = [_dot(do[j], vn[j], "nt") for j in hs]
            dqd = [_dot(do[j], s0[j], "nt") for j in hs]
            ds_o = [_dot(qd[j], do[j], "tn") for j in hs]
            ds = [ds_scr[h] for h in nh]
            dvn, dkd, dcd = [None] * 8, [None] * 8, [None] * 8
            for i in range(2):
                for h in nh:
                    j = i * B_HEADS + h
                    dvn[j] = dvn_o[j] + _dot(kd[j], ds[h])
                    dkd[j] = _dot(vn[j], ds[h], "nt")
                    dcd[j] = jnp.sum(jnp.sum(s0[j] * ds[h], axis=0, keepdims=True), axis=1, keepdims=True)
                ds = [ds_o[i * B_HEADS + h] + ds[h] * cd[i * B_HEADS + h] - _dot(w[i * B_HEADS + h], dvn[i * B_HEADS + h], "tn")
                      for h in nh]
            for h in nh:
                ds_scr[h] = ds[h]
            dw = [-_dot(dvn[j], s0[j], "nt") for j in hs]
            dbv = [_dot(tinv[h], dvn[h], "tn", hi=True) for h in hs]
            dkg = [_dot(tinv[h], dw[h], "tn", hi=True) for h in hs]
            dt = [_dot(dvn[h], r["bv"][h], "nt", hi=True) + _dot(dw[h], r["kg"][h], "nt", hi=True) for h in hs]
            tdt = [_dot(tinv[h], dt[h], "tn", hi=True) for h in hs]
            da = [jnp.where(strict, -_dot(tdt[h], tinv[h], "nt", hi=True), 0.0) for h in hs]
            dm = [da[h] * decay[h] for h in hs]
            dn = [dqk[h] * decay[h] for h in hs]
            e = [(da[h] * r["kk"][h] + dqk[h] * r["qkr"][h]) * decay[h] for h in hs]
            dkb = [_dot(dm[h], k[h]) + dkg[h] * eg[h] for h in hs]
            dk = [_dot(dm[h], kb[h], "tn") + _dot(dn[h], q[h], "tn") + dkd[h] * r["ekd"][h] + dkb[h] * beta[h] for h in hs]
            dq = [_dot(dn[h], k[h]) + dqd[h] * eg[h] for h in hs]
            tkd = [rsum(dkd[h] * kd[h]) for h in hs]
            dgc = [rsum(e[h]) - rsum(e[h].T) + rsum(dqd[h] * qd[h]) - tkd[h] + rsum(dkg[h] * r["kg"][h]) for h in hs]
            dgl = [jnp.sum(tkd[h], axis=0, keepdims=True) + dcd[h] * cd[h] for h in hs]
            dbeta = [rsum(dbv[h] * v[h]) + rsum(dkb[h] * k[h]) for h in hs]
            for i in range(2):
                ids = [i * B_HEADS + h for h in nh]
                dbeta_m = sum(jnp.where(lane == h, dbeta[ids[h]], 0.0) for h in nh)
                dgc_m = sum(jnp.where(lane == B_HEADS + h, dgc[ids[h]], 0.0) for h in nh)
                dgl_m = sum(jnp.where(lane1 == B_HEADS + h, dgl[ids[h]], 0.0) for h in nh)
                dqkv_ref[crow[i], :] = jnp.concatenate(
                    [dq[j] * GDN_SCALE for j in ids] + [dk[j] for j in ids] + [dbv[j] * beta[j] for j in ids], axis=1)
                dz_ref[crow[i], :] = jnp.concatenate([hn[j][1] for j in ids], axis=1).astype(BF16)
                dga_ref[crow[i], :] = dbeta_m + _dot_exact_lhs(ltri, dgc_m, "tn") + dgl_m
            dng_ref[...] += sum(hn[j][2] for j in hs)
            return carry

        lax.fori_loop(0, nch // 2, pair, 0)

    part = lambda j: pl.BlockSpec((tb, B_WIDTH), lambda i, j=j: (nb - 1 - i, j))
    rowblk = lambda w, j=0: pl.BlockSpec((tb, w), lambda i, j=j: (nb - 1 - i, j))
    return _call_beside(
        body, beside, nb, (qkv, qkv, qkv, gates, proj, norm_g.reshape(1, B_DIM), states, *fwd_saved, dmixed, dmixed), grid=(nb,),
        in_specs=[part(0), part(1), part(2), rowblk(128), rowblk(B_WIDTH, COL_B // B_WIDTH + 3),
                  pl.BlockSpec((1, B_DIM), lambda i: (0, 0)),
                  pl.BlockSpec((nch, B_HEADS, B_DIM, B_DIM), lambda i: (nb - 1 - i, 0, 0, 0)),
                  pl.BlockSpec((nch, B_HEADS, CHUNK, CHUNK), lambda i: (nb - 1 - i, 0, 0, 0)),
                  rowblk(B_WIDTH), rowblk(B_WIDTH), rowblk(B_WIDTH), rowblk(256, 1), rowblk(256, 2)],
        out_specs=[rowblk(3 * B_WIDTH), rowblk(128), rowblk(B_WIDTH), pl.BlockSpec((1, B_DIM), lambda i: (0, 0))],
        out_shape=[jax.ShapeDtypeStruct((t, 3 * B_WIDTH), F32), jax.ShapeDtypeStruct((t, 128), F32),
                   jax.ShapeDtypeStruct((t, B_WIDTH), BF16), jax.ShapeDtypeStruct((1, B_DIM), F32)],
        scratch_shapes=[pltpu.VMEM((B_HEADS, B_DIM, B_DIM), F32)], name="gdn_bwd")


def _b_pre_bwd(proj, conv_w, alog_row, dtb_row, dqkv, dgates, tb=512):
    t = proj.shape[0]
    cb0 = COL_B // B_WIDTH

    def body(q_ref, k_ref, v_ref, qp_ref, kp_ref, vp_ref, w_ref, gi_ref, al_ref, dt_ref, dqkv_ref, dga_ref,
             dy_ref, dgi_ref, dw_ref, dal_ref, ddt_ref):
        first = pl.program_id(0) == 0

        @pl.when(first)
        def _():
            dw_ref[...] = jnp.zeros_like(dw_ref)
            dal_ref[...] = jnp.zeros_like(dal_ref)
            ddt_ref[...] = jnp.zeros_like(ddt_ref)

        for part, (c_ref, p_ref) in enumerate(((q_ref, qp_ref), (k_ref, kp_ref), (v_ref, vp_ref))):
            cols = slice(part * B_WIDTH, (part + 1) * B_WIDTH)
            cur = c_ref[...]
            prev = jnp.where(first, 0.0, p_ref[...])
            w = w_ref[:, cols]
            shifted = [_shift_rows(cur, prev, 3 - j, down=True) for j in range(3)] + [cur]
            y = shifted[0] * w[0:1] + shifted[1] * w[1:2] + shifted[2] * w[2:3] + shifted[3] * w[3:4]
            sg = _sigmoid(y)
            s = y * sg
            dsn = dqkv_ref[:, cols]
            if part < 2:
                outs = []
                for h in range(B_HEADS):
                    hc = slice(h * B_DIM, (h + 1) * B_DIM)
                    sh, dh = s[:, hc], dsn[:, hc]
                    rq = lax.rsqrt(jnp.sum(sh * sh, axis=-1, keepdims=True) + EPS)
                    nh = sh * rq
                    outs.append(rq * (dh - nh * jnp.sum(dh * nh, axis=-1, keepdims=True)))
                dsn = jnp.concatenate(outs, axis=1)
            dy = dsn * (sg * (1.0 + y * (1.0 - sg)))
            dy_ref[:, cols] = dy
            dw_ref[:, cols] += jnp.concatenate([jnp.sum(shifted[j] * dy, axis=0, keepdims=True) for j in range(CONV_K)], axis=0)
        g = gi_ref[...]
        dga = dga_ref[...]
        is_b, is_a = _gate_lane_masks(g.shape)
        beta = _sigmoid(g)
        pre = g + dt_ref[...]
        ea = jnp.exp(al_ref[...])
        la = -ea * _softplus(pre)
        dpre = jnp.where(is_a, dga * (-ea) * _sigmoid(pre), 0.0)
        dgi_ref[...] = jnp.where(is_b, dga * beta * (1.0 - beta), dpre).astype(BF16)
        dal_ref[...] += jnp.sum(jnp.where(is_a, dga * la, 0.0), axis=0, keepdims=True)
        ddt_ref[...] += jnp.sum(dpre, axis=0, keepdims=True)

    cur = lambda j: pl.BlockSpec((tb, B_WIDTH), lambda i, j=j: (i, cb0 + j))
    prv = lambda j: pl.BlockSpec((8, B_WIDTH), lambda i, j=j: (jnp.maximum(i * (tb // 8) - 1, 0), cb0 + j))
    vec = pl.BlockSpec((1, 128), lambda i: (0, 0))
    wspec = pl.BlockSpec((CONV_K, 3 * B_WIDTH), lambda i: (0, 0))
    return pl.pallas_call(
        body, grid=(t // tb,),
        in_specs=[cur(0), cur(1), cur(2), prv(0), prv(1), prv(2), wspec,
                  pl.BlockSpec((tb, 128), lambda i: (i, COL_G // 128)), vec, vec,
                  pl.BlockSpec((tb, 3 * B_WIDTH), lambda i: (i, 0)), pl.BlockSpec((tb, 128), lambda i: (i, 0))],
        out_specs=[pl.BlockSpec((tb, 3 * B_WIDTH), lambda i: (i, 0)), pl.BlockSpec((tb, 128), lambda i: (i, 0)), wspec, vec, vec],
        out_shape=[jax.ShapeDtypeStruct((t, 3 * B_WIDTH), F32), jax.ShapeDtypeStruct((t, 128), BF16),
                   jax.ShapeDtypeStruct((CONV_K, 3 * B_WIDTH), F32), jax.ShapeDtypeStruct((1, 128), F32), jax.ShapeDtypeStruct((1, 128), F32)],
        compiler_params=_cparams(("arbitrary",)), name="gdn_pre_bwd")(
            proj, proj, proj, proj, proj, proj, conv_w, proj, alog_row, dtb_row, dqkv, dgates)


def _conv_bwd_x(dy, w, cb, tb=512, name="conv_bwd_x"):
    t, c = dy.shape
    nb = t // tb

    def body(dy_ref, nx_ref, w_ref, dx_ref):
        cur = dy_ref[...]
        nxt = jnp.where(pl.program_id(0) == nb - 1, 0.0, nx_ref[...])
        w = w_ref[...]
        dx = cur * w[3:4]
        for j in range(3):
            dx = dx + _shift_rows(cur, nxt, 3 - j, down=False) * w[j:j + 1]
        dx_ref[...] = dx.astype(BF16)

    return pl.pallas_call(
        body, grid=(nb, c // cb),
        in_specs=[pl.BlockSpec((tb, cb), lambda i, j: (i, j)),
                  pl.BlockSpec((8, cb), lambda i, j: (jnp.minimum((i + 1) * (tb // 8), t // 8 - 1), j)),
                  pl.BlockSpec((CONV_K, cb), lambda i, j: (0, j))],
        out_specs=pl.BlockSpec((tb, cb), lambda i, j: (i, j)), out_shape=jax.ShapeDtypeStruct((t, c), BF16),
        compiler_params=_cparams(("parallel", "parallel")), name=name)(dy, dy, w)


def _c_gates(xc, wa_ref, ba_ref, wx_ref, bx_ref, lam_ref, is_row0):
    r = _sigmoid(_dot(xc, wa_ref[...]) + ba_ref[...])
    i = _sigmoid(_dot(xc, wx_ref[...]) + bx_ref[...])
    sp = _softplus(-lam_ref[...])
    log_a = -RG_C * r * sp
    a = jnp.exp(log_a)
    m2 = _neg_expm1(2.0 * log_a)
    mult = jnp.where(is_row0, 1.0, jnp.sqrt(jnp.maximum(m2, EPS)))
    return r, i, sp, log_a, a, m2, mult


def _row0_mask(tb, first):
    ridx = lax.broadcasted_iota(jnp.int32, (tb, C_WIDTH), 0)
    return (ridx == 0) & first


def _c_fwd(proj, conv_w, conv_b, wa, ba, wx, bx, lam, tb=512):
    t = proj.shape[0]
    cbx = COL_C // C_WIDTH

    def body(x_ref, xp_ref, y_ref, w_ref, cb_ref, wa_ref, ba_ref, wx_ref, bx_ref, lam_ref, out_ref, h_ref, a_scr, b_scr, h_scr):
        first = pl.program_id(0) == 0

        @pl.when(first)
        def _():
            h_scr[...] = jnp.zeros_like(h_scr)

        prev = jnp.where(first, 0.0, xp_ref[...])
        xc = _conv_fwd(x_ref[...], prev, w_ref[...]) + cb_ref[...]
        _, i, _, _, a, _, mult = _c_gates(xc, wa_ref, ba_ref, wx_ref, bx_ref, lam_ref, _row0_mask(tb, first))
        a_scr[...] = a
        b_scr[...] = mult * i * xc

        def step(blk, h):
            rows = pl.ds(pl.multiple_of(blk * 8, 8), 8)
            at, bt = a_scr[rows, :], b_scr[rows, :]
            out = []
            for j in range(8):
                h = at[j:j + 1] * h + bt[j:j + 1]
                out.append(h)
            h_ref[rows, :] = jnp.concatenate(out, axis=0)
            return h

        h_scr[...] = lax.fori_loop(0, tb // 8, step, h_scr[...])
        gl, _ = _gelu_tanh(y_ref[...])
        out_ref[...] = (gl * h_ref[...]).astype(BF16)

    vec = pl.BlockSpec((1, C_WIDTH), lambda i: (0, 0))
    mat = pl.BlockSpec((C_WIDTH, C_WIDTH), lambda i: (0, 0))
    row = pl.BlockSpec((tb, C_WIDTH), lambda i: (i, 0))
    return pl.pallas_call(
        body, grid=(t // tb,),
        in_specs=[pl.BlockSpec((tb, C_WIDTH), lambda i: (i, cbx)),
                  pl.BlockSpec((8, C_WIDTH), lambda i: (jnp.maximum(i * (tb // 8) - 1, 0), cbx)),
                  pl.BlockSpec((tb, C_WIDTH), lambda i: (i, cbx + 1)),
                  pl.BlockSpec((CONV_K, C_WIDTH), lambda i: (0, 0)), vec, mat, vec, mat, vec, vec],
        out_specs=[row, row],
        out_shape=[jax.ShapeDtypeStruct((t, C_WIDTH), BF16), jax.ShapeDtypeStruct((t, C_WIDTH), F32)],
        scratch_shapes=[pltpu.VMEM((tb, C_WIDTH), F32), pltpu.VMEM((tb, C_WIDTH), F32), pltpu.VMEM((1, C_WIDTH), F32)],
        compiler_params=_cparams(("arbitrary",)), name="lru_fwd")(proj, proj, proj, conv_w, conv_b, wa, ba, wx, bx, lam)


def _c_bwd(proj, conv_w, conv_b, wa, ba, wx, bx, lam, hs, dmixed, tb=512):
    t = proj.shape[0]
    nb = t // tb
    cbx = COL_C // C_WIDTH

    def body(x_ref, xp_ref, y_ref, w_ref, cb_ref, wa_ref, ba_ref, wx_ref, bx_ref, lam_ref, h_ref, hp_ref, dm_ref,
             dxc_ref, dyg_ref, dw_ref, dcb_ref, dwa_ref, dba_ref, dwx_ref, dbx_ref, dlam_ref, g_scr, a_scr, c_scr):
        step_id = pl.program_id(0)
        first = step_id == nb - 1

        @pl.when(step_id == 0)
        def _():
            c_scr[...] = jnp.zeros_like(c_scr)
            for ref in (dw_ref, dcb_ref, dwa_ref, dba_ref, dwx_ref, dbx_ref, dlam_ref):
                ref[...] = jnp.zeros_like(ref)

        cur = x_ref[...]
        prev = jnp.where(first, 0.0, xp_ref[...])
        w = w_ref[...]
        shifted = [_shift_rows(cur, prev, 3 - j, down=True) for j in range(3)] + [cur]
        xc = shifted[0] * w[0:1] + shifted[1] * w[1:2] + shifted[2] * w[2:3] + shifted[3] * w[3:4] + cb_ref[...]
        row0 = _row0_mask(tb, first)
        r, i, sp, log_a, a, m2, mult = _c_gates(xc, wa_ref, ba_ref, wx_ref, bx_ref, lam_ref, row0)
        h = h_ref[...]
        hprev = _shift_rows(h, jnp.where(first, 0.0, hp_ref[...]), 1, down=True)
        gl, dgl = _gelu_tanh(y_ref[...])
        dm = dm_ref[...].astype(F32)
        dyg_ref[...] = (dm * h * dgl).astype(BF16)
        g_scr[...] = dm * gl
        a_scr[...] = a

        def step(blk, carry):
            b = tb // 8 - 1 - blk
            rows = pl.ds(pl.multiple_of(b * 8, 8), 8)
            at, gt = a_scr[rows, :], g_scr[rows, :]
            out = [None] * 8
            for j in range(7, -1, -1):
                gj = gt[j:j + 1] + carry
                out[j] = gj
                carry = at[j:j + 1] * gj
            g_scr[rows, :] = jnp.concatenate(out, axis=0)
            return carry

        c_scr[...] = lax.fori_loop(0, tb // 8, step, c_scr[...])
        dbx = g_scr[...]
        da = dbx * hprev
        dmult = jnp.where(row0, 0.0, dbx * i * xc)
        di = dbx * mult * xc
        dxc = dbx * mult * i
        dm2 = jnp.where(m2 > EPS, dmult * 0.5 / mult, 0.0)
        dlog_a = da * a - 2.0 * a * a * dm2
        dr = dlog_a * (-RG_C) * sp
        dlam_ref[...] += jnp.sum(dlog_a * (-RG_C) * r, axis=0, keepdims=True) * (-_sigmoid(-lam_ref[...]))
        dpa = dr * r * (1.0 - r)
        dpx = di * i * (1.0 - i)
        dba_ref[...] += jnp.sum(dpa, axis=0, keepdims=True)
        dbx_ref[...] += jnp.sum(dpx, axis=0, keepdims=True)
        dwa_ref[...] += _dot(xc, dpa, "tn")
        dwx_ref[...] += _dot(xc, dpx, "tn")
        dxc = dxc + _dot(dpa, wa_ref[...], "nt") + _dot(dpx, wx_ref[...], "nt")
        dxc_ref[...] = dxc
        dcb_ref[...] += jnp.sum(dxc, axis=0, keepdims=True)
        dw_ref[...] += jnp.concatenate([jnp.sum(shifted[j] * dxc, axis=0, keepdims=True) for j in range(CONV_K)], axis=0)

    vec = pl.BlockSpec((1, C_WIDTH), lambda i: (0, 0))
    mat = pl.BlockSpec((C_WIDTH, C_WIDTH), lambda i: (0, 0))
    cw = pl.BlockSpec((CONV_K, C_WIDTH), lambda i: (0, 0))
    row = lambda j=0: pl.BlockSpec((tb, C_WIDTH), lambda i, j=j: (nb - 1 - i, j))
    halo = lambda j=0: pl.BlockSpec((8, C_WIDTH), lambda i, j=j: (jnp.maximum((nb - 1 - i) * (tb // 8) - 1, 0), j))
    return pl.pallas_call(
        body, grid=(nb,),
        in_specs=[row(cbx), halo(cbx), row(cbx + 1), cw, vec, mat, vec, mat, vec, vec, row(), halo(), row(3)],
        out_specs=[row(), row(), cw, vec, mat, vec, mat, vec, vec],
        out_shape=[jax.ShapeDtypeStruct((t, C_WIDTH), F32), jax.ShapeDtypeStruct((t, C_WIDTH), BF16),
                   jax.ShapeDtypeStruct((CONV_K, C_WIDTH), F32), jax.ShapeDtypeStruct((1, C_WIDTH), F32),
                   jax.ShapeDtypeStruct((C_WIDTH, C_WIDTH), F32), jax.ShapeDtypeStruct((1, C_WIDTH), F32),
                   jax.ShapeDtypeStruct((C_WIDTH, C_WIDTH), F32), jax.ShapeDtypeStruct((1, C_WIDTH), F32),
                   jax.ShapeDtypeStruct((1, C_WIDTH), F32)],
        scratch_shapes=[pltpu.VMEM((tb, C_WIDTH), F32), pltpu.VMEM((tb, C_WIDTH), F32), pltpu.VMEM((1, C_WIDTH), F32)],
        compiler_params=_cparams(("arbitrary",)), name="lru_bwd")(
            proj, proj, proj, conv_w, conv_b, wa, ba, wx, bx, lam, hs, hs, dmixed)


def _mesh_pos():
    return lax.axis_index("x"), lax.axis_index("y"), lax.axis_index("c")


def _all_gather(x, name):
    def body(x_ref, out_ref, send_sems, recv_sems, local_sem):
        mx, my, mc = _mesh_pos()
        me, sibling = (mx, my, mc), (mx, my, 1 - mc)
        chips = [(1 - mx, my), (mx, 1 - my), (1 - mx, 1 - my)]

        def slot(px, py, pc):
            return out_ref.at[4 * px + 2 * py + pc]

        def copy(k, block, to, src=None):
            return pltpu.make_async_remote_copy(
                src_ref=slot(*block) if src is None else src, dst_ref=slot(*block),
                send_sem=send_sems.at[k], recv_sem=recv_sems.at[k], device_id=to, device_id_type=MESH)

        mine = pltpu.make_async_copy(x_ref, slot(*me), local_sem)
        mine.start()
        first = [copy(0, me, sibling, src=x_ref)]
        first += [copy(1 + j, me, (*chip, mc), src=x_ref) for j, chip in enumerate(chips)]
        for cp in first:
            cp.start()
        passed = [copy(4 + j, (*chip, mc), sibling) for j, chip in enumerate(chips)]
        for j, chip in enumerate(chips):
            copy(1 + j, (*chip, mc), me).wait_recv()
            passed[j].start()
        copy(0, sibling, me).wait_recv()
        for j, chip in enumerate(chips):
            copy(4 + j, (*chip, 1 - mc), me).wait_recv()
        for cp in first + passed:
            cp.wait_send()
        mine.wait()

    return pl.pallas_call(
        body, out_shape=jax.ShapeDtypeStruct((N_DEV,) + x.shape, x.dtype),
        in_specs=[pl.BlockSpec(memory_space=pl.ANY)], out_specs=pl.BlockSpec(memory_space=pl.ANY),
        scratch_shapes=[pltpu.SemaphoreType.DMA((7,)), pltpu.SemaphoreType.DMA((7,)), pltpu.SemaphoreType.DMA(())],
        name=name)(x)


def _all_to_all(x, name):
    def body(x_ref, out_ref, send_sems, recv_sems, local_sem):
        mx, my, mc = _mesh_pos()
        me = 4 * mx + 2 * my + mc
        mine = pltpu.make_async_copy(x_ref.at[me], out_ref.at[me], local_sem)
        mine.start()
        copies = []
        for k in range(1, N_DEV):
            px = 1 - mx if k & 4 else mx
            py = 1 - my if k & 2 else my
            pc = 1 - mc if k & 1 else mc
            copies.append(pltpu.make_async_remote_copy(
                src_ref=x_ref.at[4 * px + 2 * py + pc], dst_ref=out_ref.at[me],
                send_sem=send_sems.at[k - 1], recv_sem=recv_sems.at[k - 1], device_id=(px, py, pc), device_id_type=MESH))
        for cp in copies:
            cp.start()
        for cp in copies:
            cp.wait()
        mine.wait()

    return pl.pallas_call(
        body, out_shape=jax.ShapeDtypeStruct(x.shape, x.dtype),
        in_specs=[pl.BlockSpec(memory_space=pl.ANY)], out_specs=pl.BlockSpec(memory_space=pl.ANY),
        scratch_shapes=[pltpu.SemaphoreType.DMA((7,)), pltpu.SemaphoreType.DMA((7,)), pltpu.SemaphoreType.DMA(())],
        name=name)(x)


class _Exchange:
    def __init__(self, arrays, layouts):
        self.arrays, self.layouts = list(arrays), list(layouts)
        self.out_shapes = []
        for a, lay in zip(self.arrays, self.layouts):
            if lay == 'a2a':
                shp = a.shape
            elif lay == 'slot':
                shp = (N_DEV,) + a.shape
            elif lay == 'rows':
                shp = (N_DEV * a.shape[0], a.shape[1])
            else:
                shp = (a.shape[0], N_DEV * a.shape[1])
            self.out_shapes.append(jax.ShapeDtypeStruct(shp, a.dtype))
        n = len(self.arrays)
        self.scratch = [pltpu.SemaphoreType.DMA((7 * n,)), pltpu.SemaphoreType.DMA((7 * n,)), pltpu.SemaphoreType.DMA((n,))]

    def _landing(self, a, dst_ref, idx):
        lay, shape = self.layouts[a], self.arrays[a].shape
        if lay in ('a2a', 'slot'):
            return dst_ref.at[idx]
        if lay == 'rows':
            return dst_ref.at[pl.ds(pl.multiple_of(idx * shape[0], shape[0]), shape[0]), :]
        return dst_ref.at[:, pl.ds(pl.multiple_of(idx * shape[1], shape[1]), shape[1])]

    def copies(self, src_refs, dst_refs, send_sems, recv_sems, local_sems):
        mx, my, mc = _mesh_pos()
        me = 4 * mx + 2 * my + mc
        out = []
        for a, (src, dst) in enumerate(zip(src_refs, dst_refs)):
            a2a = self.layouts[a] == 'a2a'
            out.append(pltpu.make_async_copy(src.at[me] if a2a else src, self._landing(a, dst, me), local_sems.at[a]))
            for k in range(1, N_DEV):
                px = 1 - mx if k & 4 else mx
                py = 1 - my if k & 2 else my
                pc = 1 - mc if k & 1 else mc
                out.append(pltpu.make_async_remote_copy(
                    src_ref=src.at[4 * px + 2 * py + pc] if a2a else src, dst_ref=self._landing(a, dst, me),
                    send_sem=send_sems.at[7 * a + k - 1], recv_sem=recv_sems.at[7 * a + k - 1],
                    device_id=(px, py, pc), device_id_type=MESH))
        return out


_ANY = pl.BlockSpec(memory_space=pl.ANY)


def _run_exchange(ex, name):
    n = len(ex.arrays)

    def body(*refs):
        cps = ex.copies(refs[:n], refs[n:2 * n], *refs[2 * n:])
        for cp in cps:
            cp.start()
        for cp in cps:
            cp.wait()

    return pl.pallas_call(body, out_shape=ex.out_shapes, in_specs=[_ANY] * n, out_specs=[_ANY] * n,
                          scratch_shapes=ex.scratch, name=name)(*ex.arrays)


def _call_beside(body, ex, nsteps, args, *, grid, in_specs, out_specs, out_shape, scratch_shapes, name):
    if ex is None:
        outs = pl.pallas_call(body, grid=grid, in_specs=in_specs, out_specs=out_specs, out_shape=out_shape,
                              scratch_shapes=scratch_shapes, compiler_params=_cparams(("arbitrary",)), name=name)(*args)
        return outs, None
    n_in, n_out, n_scr, n = len(in_specs), len(out_specs), len(scratch_shapes), len(ex.arrays)

    def wrapped(*refs):
        ins, refs = refs[:n_in], refs[n_in:]
        ex_ins, refs = refs[:n], refs[n:]
        outs, refs = refs[:n_out], refs[n_out:]
        ex_outs, refs = refs[:n], refs[n:]
        scr, sems = refs[:n_scr], refs[n_scr:]
        step = pl.program_id(0)

        @pl.when(step == 0)
        def _():
            for cp in ex.copies(ex_ins, ex_outs, *sems):
                cp.start()

        body(*ins, *outs, *scr)

        @pl.when(step == nsteps - 1)
        def _():
            for cp in ex.copies(ex_ins, ex_outs, *sems):
                cp.wait()

    res = pl.pallas_call(
        wrapped, grid=grid, in_specs=list(in_specs) + [_ANY] * n, out_specs=list(out_specs) + [_ANY] * n,
        out_shape=list(out_shape) + ex.out_shapes, scratch_shapes=list(scratch_shapes) + ex.scratch,
        compiler_params=_cparams(("arbitrary",)), name=name)(*args, *ex.arrays)
    return res[:n_out], res[n_out:]


def _adamw_math(w, g, m, v):
    m = ADAM_B1 * m + (1.0 - ADAM_B1) * g
    v = ADAM_B2 * v + (1.0 - ADAM_B2) * (g * g)
    m_hat = m / (1.0 - ADAM_B1 ** ADAM_STEP)
    v_hat = v / (1.0 - ADAM_B2 ** ADAM_STEP)
    delta = -ADAM_LR * (m_hat / (jnp.sqrt(v_hat) + ADAM_EPS) + ADAM_WD * w)
    return delta, m, v


def _sum_adamw(parts, w, m, v, tr, name):
    p, r, c = parts.shape
    tr = min(tr, r)
    assert r % tr == 0

    def body(p_ref, w_ref, m_ref, v_ref, g_ref, d_ref, nm_ref, nv_ref):
        g = p_ref[0].astype(F32)
        for j in range(1, p):
            g = g + p_ref[j].astype(F32)
        delta, nm, nv = _adamw_math(w_ref[...], g, m_ref[...], v_ref[...])
        g_ref[...] = g
        d_ref[...] = delta
        nm_ref[...] = nm
        nv_ref[...] = nv

    row = pl.BlockSpec((tr, c), lambda i: (i, 0))
    return pl.pallas_call(
        body, grid=(r // tr,), in_specs=[pl.BlockSpec((p, tr, c), lambda i: (0, i, 0)), row, row, row],
        out_specs=[row] * 4, out_shape=[jax.ShapeDtypeStruct((r, c), F32)] * 4,
        compiler_params=_cparams(("parallel",)), name=name)(parts, w, m, v)


def _sum_parts(parts, name):
    p, r, c = parts.shape

    def body(p_ref, o_ref):
        g = p_ref[0]
        for j in range(1, p):
            g = g + p_ref[j]
        o_ref[...] = g

    return pl.pallas_call(body, out_shape=jax.ShapeDtypeStruct((r, c), F32), name=name)(parts)


def _pack(arrs, mult=1024):
    flat = jnp.concatenate([a.reshape(-1).astype(F32) for a in arrs])
    n = flat.shape[0]
    npad = -n % mult
    return jnp.pad(flat, (0, npad)).reshape(-1, 128)


def _unpack(buf, shapes):
    flat = buf.reshape(-1)
    out, off = [], 0
    for s in shapes:
        n = 1
        for d in s:
            n *= d
        out.append(flat[off:off + n].reshape(s))
        off += n
    return out


def _block_diag(w):
    rows = [jnp.pad(w[i], ((0, 0), (i * C_BLOCK_DIM, C_WIDTH - (i + 1) * C_BLOCK_DIM))) for i in range(C_BLOCKS)]
    return jnp.concatenate(rows, axis=0)


def _diag_blocks(m):
    m4 = m.reshape(C_BLOCKS, C_BLOCK_DIM, C_BLOCKS, C_BLOCK_DIM)
    return jnp.stack([m4[i, :, i, :] for i in range(C_BLOCKS)])


def _gate_row(v):
    return jnp.pad(v.astype(F32), (B_HEADS, 128 - 2 * B_HEADS)).reshape(1, 128)


def _permute_w_in(w):
    pad = jnp.zeros(w.shape[:-1] + (D_IN_PAD - D_IN,), w.dtype)
    return jnp.concatenate([w[..., :3072], w[..., 3080:3592], w[..., 3072:3080], pad], axis=-1)


def _unpermute_w_in(w):
    return jnp.concatenate([w[..., :3072], w[..., COL_G:COL_G + 8], w[..., 3072:COL_G]], axis=-1)


_WEIGHTS = ['norm1_g', 'w_in', 'hgrn_lb_logits', 'hgrn_norm_g', 'gdn_conv_w', 'gdn_a_log', 'gdn_dt_bias', 'gdn_norm_g',
            'lru_conv_w', 'lru_conv_b', 'lru_w_a', 'lru_b_a', 'lru_w_x', 'lru_b_x', 'lru_lambda', 'w_out', 'norm2_g',
            'w_up', 'w_down', 'final_norm_g']
_BIG = ('w_in', 'w_out', 'w_up', 'w_down')
_SHARDED_SMALL = ('gdn_conv_w', 'lru_conv_w')


def _step(x, target, w, m, v):
    t = x.shape[0]
    mx, my, mc = _mesh_pos()
    me = 4 * mx + 2 * my + mc

    bf = lambda a: a.astype(BF16)

    def full_w_in(g):
        return _permute_w_in(jnp.moveaxis(g, 0, 1).reshape(D_MODEL, D_IN))

    conv_shapes = [w['gdn_conv_w'].shape, w['lru_conv_w'].shape]
    g_in, g_conv = _run_exchange(
        _Exchange([bf(w['w_in'][0]), _pack([w['gdn_conv_w'], w['lru_conv_w']])], ['slot', 'slot']), "gather_first")
    w_in = [full_w_in(g_in)]
    w_out, w_up, w_down = [], [], []
    gdn_cw, lru_cw = [], []
    for j in range(N_DEV):
        a, b = _unpack(g_conv[j], conv_shapes)
        gdn_cw.append(a)
        lru_cw.append(b)
    gdn_cw = jnp.concatenate(gdn_cw, axis=-1)
    lru_cw = jnp.concatenate(lru_cw, axis=-1)

    lbnd = _lb_fwd(w['hgrn_lb_logits'])
    row = lambda a: a.reshape(1, -1)

    def c_args(l):
        return (lru_cw[l], row(w['lru_conv_b'][l]), _block_diag(w['lru_w_a'][l]), row(w['lru_b_a'][l]),
                _block_diag(w['lru_w_x'][l]), row(w['lru_b_x'][l]), row(w['lru_lambda'][l]))

    saved = []
    xl = x
    h = _rms_fwd(x, w['norm1_g'][0], name="rms_fwd")
    for l in range(DEPTH):
        proj = _mm_rows(h, w_in[l], "nn", 256, "mm_proj")
        mix_a, st_a = _a_fwd(proj, lbnd[l], w['hgrn_norm_g'][l])
        alr, dtr = _gate_row(w['gdn_a_log'][l]), _gate_row(w['gdn_dt_bias'][l])
        qkv, gates = _b_pre_fwd(proj, gdn_cw[l], alr, dtr)
        nxt = [bf(w['w_in'][l + 1])] if l + 1 < DEPTH else []
        gather = _Exchange([bf(w['w_out'][l]), bf(w['w_up'][l]), bf(w['w_down'][l])] + nxt, ['rows', 'cols', 'rows'] + ['slot'] * len(nxt))
        (mix_b, st_b, *b_saved), got = _b_fwd(qkv, gates, proj, w['gdn_norm_g'][l], beside=gather)
        w_out.append(got[0])
        w_up.append(got[1])
        w_down.append(got[2])
        if nxt:
            w_in.append(full_w_in(got[3]))
        mix_c, hs = _c_fwd(proj, *c_args(l))
        mixed = [mix_a, mix_b, mix_c]
        x_mid, h2 = _mm_rows(mixed, w_out[l], "nn", 1024, "mm_out", residual=xl, epilogue="rms_fwd", norm=w['norm2_g'][l])
        act, up = _mm_rows(h2, w_up[l], "nn", 256, "mm_up", epilogue="relu2")
        saved.append(dict(x=xl, h=h, proj=proj, st_a=st_a, qkv=qkv, gates=gates, st_b=st_b, b_saved=b_saved, hs=hs, mixed=mixed,
                          x_mid=x_mid, h2=h2, up=up, act=act, alr=alr, dtr=dtr))
        if l + 1 < DEPTH:
            xl, h = _mm_rows(act, w_down[l], "nn", 512, "mm_down", residual=x_mid, epilogue="rms_fwd", norm=w['norm1_g'][l + 1])
        else:
            xl = _mm_rows(act, w_down[l], "nn", 512, "mm_down_last", residual=x_mid)
    loss, dx, dgf = _loss_head(xl, w['final_norm_g'], target)

    gs = {n: [None] * DEPTH for n in _WEIGHTS}
    recv = {n: [None] * DEPTH for n in _BIG}
    dw_in_above = None
    for l in reversed(range(DEPTH)):
        s = saved[l]
        dup = _mm_rows(dx, w_down[l], "nt", 256, "mm_dact", epilogue="drelu2", up=s['up'])
        dw_down = _mm_tn(s['act'], dx, 512, "mm_dw_down").reshape(N_DEV, D_FF // N_DEV, D_MODEL)
        dx_mid, dg2 = _mm_rows(dup, w_up[l], "nt", 512, "mm_dh2", epilogue="rms_bwd", norm=(s['x_mid'], w['norm2_g'][l], dx))
        dw_up = _mm_tn(s['h2'], dup, 512, "mm_dw_up", slab=D_FF // N_DEV)
        gs['norm2_g'][l] = dg2[0]
        dmixed = _mm_rows(dx_mid, w_out[l], "nt", 1024, "mm_dmixed")
        dw_out = _mm_tn(s['mixed'], dx_mid, 1024, "mm_dw_out").reshape(N_DEV, D_MODEL // N_DEV, D_MODEL)
        proj = s['proj']
        above = [dw_in_above] if dw_in_above is not None else []
        (dpa, dlb, dnga), got = _a_bwd(proj, lbnd[l], w['hgrn_norm_g'][l], s['st_a'], dmixed,
                                       beside=_Exchange([dw_out] + above, ['a2a'] * (1 + len(above))))
        recv['w_out'][l] = got[0]
        if above:
            recv['w_in'][l + 1] = got[1]
        gs['hgrn_lb_logits'][l] = dlb[0]
        gs['hgrn_norm_g'][l] = dnga[0]
        (dqkv, dgates, dz, dngb), got = _b_bwd(s['qkv'], s['gates'], proj, w['gdn_norm_g'][l], s['st_b'], s['b_saved'], dmixed,
                                               beside=_Exchange([dw_up, dw_down], ['a2a', 'a2a']))
        recv['w_up'][l], recv['w_down'][l] = got
        dyb, dgi, dcwb, dal, ddt = _b_pre_bwd(proj, gdn_cw[l], s['alr'], s['dtr'], dqkv, dgates)
        dxb = _conv_bwd_x(dyb, gdn_cw[l], B_WIDTH, name="conv_bwd_x_gdn")
        gs['gdn_norm_g'][l] = dngb[0]
        gs['gdn_conv_w'][l] = dcwb
        gs['gdn_a_log'][l] = dal[0, B_HEADS:2 * B_HEADS]
        gs['gdn_dt_bias'][l] = ddt[0, B_HEADS:2 * B_HEADS]
        dxc, dyg, dcwc, dcb, dwa, dba, dwx, dbx, dlam = _c_bwd(proj, *c_args(l), s['hs'], dmixed)
        dxc_in = _conv_bwd_x(dxc, lru_cw[l], C_WIDTH, name="conv_bwd_x_lru")
        gs['lru_conv_w'][l] = dcwc
        gs['lru_conv_b'][l] = dcb[0]
        gs['lru_w_a'][l] = _diag_blocks(dwa)
        gs['lru_b_a'][l] = dba[0]
        gs['lru_w_x'][l] = _diag_blocks(dwx)
        gs['lru_b_x'][l] = dbx[0]
        gs['lru_lambda'][l] = dlam[0]
        dproj = [dpa, dxb, dz, dxc_in, dyg, dgi]
        dx, dg1 = _mm_rows(dproj, w_in[l], "nt", 512, "mm_dh", epilogue="rms_bwd", norm=(s['x'], w['norm1_g'][l], dx_mid))
        dw_in = _unpermute_w_in(_mm_tn(s['h'], dproj, 512, "mm_dw_in"))
        dw_in_above = jnp.moveaxis(dw_in.reshape(D_MODEL, N_DEV, D_IN // N_DEV), 1, 0)
        gs['norm1_g'][l] = dg1[0]
    grad_x = dx
    part = {n: jnp.stack(gs[n]) for n in _WEIGHTS if n != 'final_norm_g' and n not in _BIG}
    part['final_norm_g'] = dgf[0]
    part['hgrn_lb_logits'] = _lb_bwd(w['hgrn_lb_logits'], part['hgrn_lb_logits'])

    small = [n for n in _WEIGHTS if n not in _BIG]
    packed = _pack([part[n] for n in small] + [loss])
    recv['w_in'][0], all_small = _run_exchange(_Exchange([dw_in_above, packed], ['a2a', 'slot']), "exchange_last")

    grads, deltas, new_m, new_v = {}, {}, {}, {}
    for n in _BIG:
        shp = w[n].shape
        r2 = lambda a: a.reshape(-1, shp[-1])
        parts = jnp.stack(recv[n], axis=1).reshape(N_DEV, -1, shp[-1])
        g, d, nm, nv = _sum_adamw(parts, r2(w[n]), r2(m[n]), r2(v[n]), 256, "adamw_" + n)
        grads[n], deltas[n], new_m[n], new_v[n] = (a.reshape(shp) for a in (g, d, nm, nv))

    total = _sum_parts(all_small, "sum_small")
    summed = _unpack(total, [part[n].shape for n in small] + [(1, 1)])
    loss_total = summed[-1].reshape(())
    gsmall = dict(zip(small, summed[:-1]))
    for n in _SHARDED_SMALL:
        width = w[n].shape[-1]
        gsmall[n] = lax.dynamic_slice_in_dim(gsmall[n], me * width, width, axis=2)
    pk = lambda d: _pack([d[n] for n in small])
    _, d, nm, nv = _sum_adamw(pk(gsmall)[None], pk(w), pk(m), pk(v), 4096, "adamw_small")
    shapes = [w[n].shape for n in small]
    for n, dd, mm, vv in zip(small, _unpack(d, shapes), _unpack(nm, shapes), _unpack(nv, shapes)):
        grads[n], deltas[n], new_m[n], new_v[n] = gsmall[n], dd, mm, vv
    return loss_total, grad_x, grads, deltas, new_m, new_v


def kernel(x, norm1_g, w_in, hgrn_lb_logits, hgrn_norm_g, gdn_conv_w, gdn_a_log, gdn_dt_bias, gdn_norm_g, lru_conv_w, lru_conv_b, lru_w_a, lru_b_a, lru_w_x, lru_b_x, lru_lambda, w_out, norm2_g, w_up, w_down, final_norm_g, loss_target, m_norm1_g, m_w_in, m_hgrn_lb_logits, m_hgrn_norm_g, m_gdn_conv_w, m_gdn_a_log, m_gdn_dt_bias, m_gdn_norm_g, m_lru_conv_w, m_lru_conv_b, m_lru_w_a, m_lru_b_a, m_lru_w_x, m_lru_b_x, m_lru_lambda, m_w_out, m_norm2_g, m_w_up, m_w_down, m_final_norm_g, v_norm1_g, v_w_in, v_hgrn_lb_logits, v_hgrn_norm_g, v_gdn_conv_w, v_gdn_a_log, v_gdn_dt_bias, v_gdn_norm_g, v_lru_conv_w, v_lru_conv_b, v_lru_w_a, v_lru_b_a, v_lru_w_x, v_lru_b_x, v_lru_lambda, v_w_out, v_norm2_g, v_w_up, v_w_down, v_final_norm_g):
    w = dict(zip(_WEIGHTS, (norm1_g, w_in, hgrn_lb_logits, hgrn_norm_g, gdn_conv_w, gdn_a_log, gdn_dt_bias, gdn_norm_g, lru_conv_w, lru_conv_b, lru_w_a, lru_b_a, lru_w_x, lru_b_x, lru_lambda, w_out, norm2_g, w_up, w_down, final_norm_g)))
    m = dict(zip(_WEIGHTS, (m_norm1_g, m_w_in, m_hgrn_lb_logits, m_hgrn_norm_g, m_gdn_conv_w, m_gdn_a_log, m_gdn_dt_bias, m_gdn_norm_g, m_lru_conv_w, m_lru_conv_b, m_lru_w_a, m_lru_b_a, m_lru_w_x, m_lru_b_x, m_lru_lambda, m_w_out, m_norm2_g, m_w_up, m_w_down, m_final_norm_g)))
    v = dict(zip(_WEIGHTS, (v_norm1_g, v_w_in, v_hgrn_lb_logits, v_hgrn_norm_g, v_gdn_conv_w, v_gdn_a_log, v_gdn_dt_bias, v_gdn_norm_g, v_lru_conv_w, v_lru_conv_b, v_lru_w_a, v_lru_b_a, v_lru_w_x, v_lru_b_x, v_lru_lambda, v_w_out, v_norm2_g, v_w_up, v_w_down, v_final_norm_g)))
    loss, grad_x, grads, deltas, new_m, new_v = _step(x.reshape(x.shape[1:]), loss_target.reshape(x.shape[1:]), w, m, v)
    return (loss, grad_x[None], *[grads[n] for n in _WEIGHTS], *[deltas[n] for n in _WEIGHTS],
            *[new_m[n] for n in _WEIGHTS], *[new_v[n] for n in _WEIGHTS])
```

```python
import functools

import jax
import jax.numpy as jnp
from jax import lax
from jax.experimental import pallas as pl
from jax.experimental.pallas import tpu as pltpu

F32 = jnp.float32
BF16 = jnp.bfloat16
HI = lax.Precision.HIGHEST
MESH = pl.DeviceIdType.MESH

N_DEV = 8
D_MODEL = 1024
DEPTH = 4
A_HEADS, A_DIM, A_WIDTH = 4, 64, 256
B_HEADS, B_DIM, B_WIDTH = 4, 128, 512
C_WIDTH, C_BLOCKS, C_BLOCK_DIM = 256, 4, 64
D_IN = 3592
D_IN_PAD = 3840
COL_A, COL_B, COL_C, COL_G = 0, 1024, 3072, 3584
D_FF = 4096
CONV_K = 4
CHUNK = 64
SUB = 16
RG_C = 8.0
EPS = 1e-6
TINY = 1e-30
EXP_CLAMP = 80.0
GDN_SCALE = B_DIM ** -0.5
ADAM_LR, ADAM_B1, ADAM_B2, ADAM_EPS, ADAM_WD, ADAM_STEP = 0.001, 0.9, 0.999, 1e-08, 0.01, 10
VMEM_LIMIT = 56 * 1024 * 1024


def _cparams(sem=None):
    return pltpu.CompilerParams(dimension_semantics=sem, vmem_limit_bytes=VMEM_LIMIT)


_DIMS = {"nn": (((1,), (0,)), ((), ())), "nt": (((1,), (1,)), ((), ())), "tn": (((0,), (0,)), ((), ()))}


def _split_bf16(x):
    hi = x.astype(BF16)
    return hi, (x - hi.astype(F32)).astype(BF16)


def _dot(a, b, mode="nn", hi=False):
    if not hi:
        return lax.dot_general(a.astype(BF16), b.astype(BF16), _DIMS[mode], preferred_element_type=F32)
    ah, al = _split_bf16(a.astype(F32))
    bh, bl = _split_bf16(b.astype(F32))
    ka = 0 if mode == "tn" else 1
    kb = 1 if mode == "nt" else 0
    return lax.dot_general(jnp.concatenate([ah, ah, al], axis=ka), jnp.concatenate([bh, bl, bh], axis=kb),
                           _DIMS[mode], preferred_element_type=F32)


def _dot_exact_lhs(lhs, x, mode="nn"):
    l_bf16 = lhs.astype(BF16)
    x1 = x.astype(BF16)
    r1 = x - x1.astype(F32)
    x2 = r1.astype(BF16)
    x3 = (r1 - x2.astype(F32)).astype(BF16)
    ka = 0 if mode == "tn" else 1
    return lax.dot_general(jnp.concatenate([l_bf16] * 3, axis=ka), jnp.concatenate([x1, x2, x3], axis=0),
                           _DIMS[mode], preferred_element_type=F32)


def _iota2(n, m):
    return lax.broadcasted_iota(jnp.int32, (n, m), 0), lax.broadcasted_iota(jnp.int32, (n, m), 1)


def _tril(n, strict=False):
    r, c = _iota2(n, n)
    return (r > c) if strict else (r >= c)


def _sigmoid(x):
    return 1.0 / (1.0 + jnp.exp(-x))


def _softplus(x):
    return jnp.maximum(x, 0.0) + jnp.log(1.0 + jnp.exp(-jnp.abs(x)))


def _neg_expm1(z):
    series = -z * (1.0 + z * (0.5 + z * (1.0 / 6.0)))
    return jnp.where(z > -1e-2, series, 1.0 - jnp.exp(z))


def _gelu_tanh(x):
    c = 0.7978845608028654
    u = c * (x + 0.044715 * x * x * x)
    t = jnp.tanh(u)
    g = 0.5 * x * (1.0 + t)
    dg = 0.5 * (1.0 + t) + 0.5 * x * (1.0 - t * t) * c * (1.0 + 3.0 * 0.044715 * x * x)
    return g, dg


def _shift_rows(cur, halo, s, down=True):
    n = cur.shape[0]
    ridx = lax.broadcasted_iota(jnp.int32, (8, cur.shape[1]), 0)
    if down:
        main = pltpu.roll(cur, s, 0)
        fix = jnp.where(ridx < s, pltpu.roll(halo, s, 0), main[0:8])
        return jnp.concatenate([fix, main[8:]], axis=0)
    main = pltpu.roll(cur, n - s, 0)
    fix = jnp.where(ridx >= 8 - s, pltpu.roll(halo, 8 - s, 0), main[n - 8:n])
    return jnp.concatenate([main[:n - 8], fix], axis=0)


def _conv_bwd_rows(dy, nxt8, w):
    dx = dy * w[3:4]
    for j in range(3):
        dx = dx + _shift_rows(dy, nxt8, 3 - j, down=False) * w[j:j + 1]
    return dx


def _tile_scan(a, b, reverse=False):
    n = a.shape[0]
    r = lax.broadcasted_iota(jnp.int32, a.shape, 0) % 8
    for s in (1, 2, 4):
        keep = (r < 8 - s) if reverse else (r >= s)
        shift = n - s if reverse else s
        a_sh = jnp.where(keep, pltpu.roll(a, shift, 0), 1.0)
        b_sh = jnp.where(keep, pltpu.roll(b, shift, 0), 0.0)
        b = b + a * b_sh
        a = a * a_sh
    return a, b


def _conv_fwd(cur, prev8, w):
    y = cur * w[3:4]
    for j in range(3):
        y = y + _shift_rows(cur, prev8, 3 - j, down=True) * w[j:j + 1]
    return y


def _pick_tile(n, pref):
    best = None
    for cand in range(128, min(n, pref) + 1, 128):
        if n % cand == 0:
            best = cand
    return best if best is not None else n


def _matmul(a, b, mode, out_dtype=F32, residual=None, tm=512, tn=1024, tk=1024, name="matmul"):
    if mode == "nn":
        (m, k), n = a.shape, b.shape[1]
    elif mode == "nt":
        (m, k), n = a.shape, b.shape[0]
    else:
        (k, m), n = a.shape, b.shape[1]
    tm, tn, tk = _pick_tile(m, tm), _pick_tile(n, tn), _pick_tile(k, tk)
    nk = k // tk
    a_spec = pl.BlockSpec((tk, tm), lambda i, j, kk: (kk, i)) if mode == "tn" else pl.BlockSpec((tm, tk), lambda i, j, kk: (i, kk))
    b_spec = pl.BlockSpec((tn, tk), lambda i, j, kk: (j, kk)) if mode == "nt" else pl.BlockSpec((tk, tn), lambda i, j, kk: (kk, j))
    o_spec = pl.BlockSpec((tm, tn), lambda i, j, kk: (i, j))
    has_res = residual is not None

    def body(*refs):
        if has_res:
            a_ref, b_ref, r_ref, o_ref, acc = refs
        else:
            a_ref, b_ref, o_ref, acc = refs
        kk = pl.program_id(2)

        @pl.when(kk == 0)
        def _():
            acc[...] = jnp.zeros_like(acc)

        acc[...] += _dot(a_ref[...], b_ref[...], mode)

        @pl.when(kk == nk - 1)
        def _():
            r = acc[...]
            if has_res:
                r = r + r_ref[...]
            o_ref[...] = r.astype(out_dtype)

    ins = [a, b] + ([residual] if has_res else [])
    specs = [a_spec, b_spec] + ([o_spec] if has_res else [])
    return pl.pallas_call(
        body, grid=(m // tm, n // tn, nk), in_specs=specs, out_specs=o_spec,
        out_shape=jax.ShapeDtypeStruct((m, n), out_dtype), scratch_shapes=[pltpu.VMEM((tm, tn), F32)],
        compiler_params=_cparams(("parallel", "parallel", "arbitrary")), name=name)(*ins)


def _mm_rows(a, w, mode, tm, name, residual=None, epilogue=None, up=None, norm=None):
    parts = list(a) if isinstance(a, (list, tuple)) else [a]
    widths = [p.shape[1] for p in parts]
    t = parts[0].shape[0]
    n = w.shape[1] if mode == "nn" else w.shape[0]
    tm = min(tm, t)
    assert t % tm == 0 and all(wd % 128 == 0 for wd in widths)

    def body(*refs):
        a_refs, w_ref, rest = refs[:len(parts)], refs[len(parts)], refs[len(parts) + 1:]
        y, off = None, 0
        for a_ref, width in zip(a_refs, widths):
            wk = w_ref[off:off + width, :] if mode == "nn" else w_ref[:, off:off + width]
            d = _dot(a_ref[...], wk, mode)
            y = d if y is None else y + d
            off += width
        if residual is not None:
            y = y + rest[0][...]
        if epilogue == "relu2":
            r = jnp.maximum(y, 0.0)
            refs[-2][...] = (r * r).astype(BF16)
            refs[-1][...] = y.astype(BF16)
        elif epilogue == "drelu2":
            refs[-1][...] = (y * 2.0 * jnp.maximum(rest[0][...].astype(F32), 0.0)).astype(BF16)
        elif epilogue == "rms_fwd":
            rinv = lax.rsqrt(jnp.mean(y * y, axis=-1, keepdims=True) + EPS)
            refs[-2][...] = y
            refs[-1][...] = (y * rinv * refs[-3][...]).astype(BF16)
        elif epilogue == "rms_bwd":
            x_ref, g_ref, dres_ref, dx_ref, dg_ref = rest

            @pl.when(pl.program_id(0) == 0)
            def _():
                dg_ref[...] = jnp.zeros_like(dg_ref)

            xv = x_ref[...]
            rinv = lax.rsqrt(jnp.mean(xv * xv, axis=-1, keepdims=True) + EPS)
            xhat = xv * rinv
            dxh = y * g_ref[...]
            dx_ref[...] = dres_ref[...] + rinv * (dxh - xhat * jnp.mean(dxh * xhat, axis=-1, keepdims=True))
            dg_ref[...] += jnp.sum(y * xhat, axis=0, keepdims=True)
        else:
            refs[-1][...] = y

    rows = lambda width: pl.BlockSpec((tm, width), lambda i: (i, 0))
    vec = pl.BlockSpec((1, n), lambda i: (0, 0))
    ins, specs = parts + [w], [rows(wd) for wd in widths] + [pl.BlockSpec(w.shape, lambda i: (0, 0))]
    if residual is not None:
        ins.append(residual)
        specs.append(rows(n))
    if epilogue == "drelu2":
        ins.append(up)
        specs.append(rows(n))
    if epilogue == "rms_fwd":
        ins.append(norm.reshape(1, n))
        specs.append(vec)
        out_specs, out_shape = [rows(n), rows(n)], [jax.ShapeDtypeStruct((t, n), F32), jax.ShapeDtypeStruct((t, n), BF16)]
    elif epilogue == "rms_bwd":
        ins += [norm[0], norm[1].reshape(1, n), norm[2]]
        specs += [rows(n), vec, rows(n)]
        out_specs, out_shape = [rows(n), vec], [jax.ShapeDtypeStruct((t, n), F32), jax.ShapeDtypeStruct((1, n), F32)]
    elif epilogue == "relu2":
        out_specs, out_shape = [rows(n), rows(n)], [jax.ShapeDtypeStruct((t, n), BF16)] * 2
    else:
        out_specs, out_shape = rows(n), jax.ShapeDtypeStruct((t, n), BF16 if epilogue == "drelu2" else F32)
    return pl.pallas_call(body, grid=(t // tm,), in_specs=specs, out_specs=out_specs, out_shape=out_shape,
                          compiler_params=_cparams(("arbitrary" if epilogue == "rms_bwd" else "parallel",)), name=name)(*ins)


MM_TN_TILE = 1024


def _mm_tn(a, b, tk, name, slab=None):
    a_parts = list(a) if isinstance(a, (list, tuple)) else [a]
    b_parts = list(b) if isinstance(b, (list, tuple)) else [b]
    wa, wb = [p.shape[1] for p in a_parts], [p.shape[1] for p in b_parts]
    t, m, n = a_parts[0].shape[0], sum(wa), sum(wb)
    tk = min(tk, t)
    assert t % tk == 0 and all(x % 128 == 0 for x in wa + wb)
    nk = t // tk

    def body(*refs):
        a_refs, b_refs = refs[:len(wa)], refs[len(wa):len(wa) + len(wb)]
        o_ref, acc = refs[-2], refs[-1]
        kk = pl.program_id(0)

        @pl.when(kk == 0)
        def _():
            acc[...] = jnp.zeros_like(acc)

        ro = 0
        for a_ref, width_a in zip(a_refs, wa):
            for r0 in range(0, width_a, MM_TN_TILE):
                rw = min(MM_TN_TILE, width_a - r0)
                av = a_ref[:, r0:r0 + rw]
                co = 0
                for b_ref, width_b in zip(b_refs, wb):
                    for c0 in range(0, width_b, MM_TN_TILE):
                        cw = min(MM_TN_TILE, width_b - c0)
                        acc[ro + r0:ro + r0 + rw, co + c0:co + c0 + cw] += _dot(av, b_ref[:, c0:c0 + cw], "tn")
                    co += width_b
            ro += width_a

        @pl.when(kk == nk - 1)
        def _():
            if slab is None:
                o_ref[...] = acc[...].astype(BF16)
            else:
                for s in range(n // slab):
                    o_ref[s] = acc[:, s * slab:(s + 1) * slab].astype(BF16)

    if slab is None:
        out_spec, out_shape = pl.BlockSpec((m, n), lambda kk: (0, 0)), jax.ShapeDtypeStruct((m, n), BF16)
    else:
        out_spec, out_shape = pl.BlockSpec((n // slab, m, slab), lambda kk: (0, 0, 0)), jax.ShapeDtypeStruct((n // slab, m, slab), BF16)
    return pl.pallas_call(
        body, grid=(nk,),
        in_specs=[pl.BlockSpec((tk, x), lambda kk: (kk, 0)) for x in wa + wb],
        out_specs=out_spec, out_shape=out_shape, scratch_shapes=[pltpu.VMEM((m, n), F32)],
        compiler_params=_cparams(("arbitrary",)), name=name)(*a_parts, *b_parts)


def _rms_fwd(x, g, tb=512, name="rms_fwd"):
    t, d = x.shape

    def body(x_ref, g_ref, h_ref):
        xv = x_ref[...]
        rinv = lax.rsqrt(jnp.mean(xv * xv, axis=-1, keepdims=True) + EPS)
        h_ref[...] = (xv * rinv * g_ref[...]).astype(BF16)

    return pl.pallas_call(
        body, grid=(t // tb,), in_specs=[pl.BlockSpec((tb, d), lambda i: (i, 0)), pl.BlockSpec((1, d), lambda i: (0, 0))],
        out_specs=pl.BlockSpec((tb, d), lambda i: (i, 0)), out_shape=jax.ShapeDtypeStruct((t, d), BF16),
        compiler_params=_cparams(("parallel",)), name=name)(x, g.reshape(1, d))


def _rms_bwd(dh, x, g, dres, tb=512, name="rms_bwd"):
    t, d = x.shape

    def body(dh_ref, x_ref, g_ref, dres_ref, dx_ref, dg_ref):
        @pl.when(pl.program_id(0) == 0)
        def _():
            dg_ref[...] = jnp.zeros_like(dg_ref)

        xv = x_ref[...]
        dhv = dh_ref[...].astype(F32)
        rinv = lax.rsqrt(jnp.mean(xv * xv, axis=-1, keepdims=True) + EPS)
        xhat = xv * rinv
        dxh = dhv * g_ref[...]
        dx_ref[...] = dres_ref[...] + rinv * (dxh - xhat * jnp.mean(dxh * xhat, axis=-1, keepdims=True))
        dg_ref[...] += jnp.sum(dhv * xhat, axis=0, keepdims=True)

    row = pl.BlockSpec((tb, d), lambda i: (i, 0))
    vec = pl.BlockSpec((1, d), lambda i: (0, 0))
    return pl.pallas_call(
        body, grid=(t // tb,), in_specs=[row, row, vec, row], out_specs=[row, vec],
        out_shape=[jax.ShapeDtypeStruct((t, d), F32), jax.ShapeDtypeStruct((1, d), F32)],
        compiler_params=_cparams(("arbitrary",)), name=name)(dh, x, g.reshape(1, d), dres)


def _loss_head(x, g, target, tb=512):
    t, d = x.shape

    def body(x_ref, g_ref, t_ref, loss_ref, dx_ref, dg_ref):
        @pl.when(pl.program_id(0) == 0)
        def _():
            dg_ref[...] = jnp.zeros_like(dg_ref)
            loss_ref[...] = jnp.zeros_like(loss_ref)

        xv = x_ref[...]
        rinv = lax.rsqrt(jnp.mean(xv * xv, axis=-1, keepdims=True) + EPS)
        xhat = xv * rinv
        err = xhat * g_ref[...] - t_ref[...]
        loss_ref[...] += 0.5 * jnp.sum(jnp.mean(err * err, axis=-1, keepdims=True), axis=0, keepdims=True)
        dy = err * (1.0 / d)
        dxh = dy * g_ref[...]
        dx_ref[...] = rinv * (dxh - xhat * jnp.mean(dxh * xhat, axis=-1, keepdims=True))
        dg_ref[...] += jnp.sum(dy * xhat, axis=0, keepdims=True)

    row = pl.BlockSpec((tb, d), lambda i: (i, 0))
    vec = pl.BlockSpec((1, d), lambda i: (0, 0))
    one = pl.BlockSpec((1, 1), lambda i: (0, 0))
    return pl.pallas_call(
        body, grid=(t // tb,), in_specs=[row, vec, row], out_specs=[one, row, vec],
        out_shape=[jax.ShapeDtypeStruct((1, 1), F32), jax.ShapeDtypeStruct((t, d), F32), jax.ShapeDtypeStruct((1, d), F32)],
        compiler_params=_cparams(("arbitrary",)), name="loss_head")(x, g.reshape(1, d), target)


def _relu2_fwd(up, tb=512):
    t, d = up.shape

    def body(u_ref, a_ref):
        r = jnp.maximum(u_ref[...], 0.0)
        a_ref[...] = (r * r).astype(BF16)

    row = pl.BlockSpec((tb, d), lambda i: (i, 0))
    return pl.pallas_call(body, grid=(t // tb,), in_specs=[row], out_specs=row, out_shape=jax.ShapeDtypeStruct((t, d), BF16),
                          compiler_params=_cparams(("parallel",)), name="relu2_fwd")(up)


def _relu2_bwd(dact, up, tb=512):
    t, d = up.shape

    def body(da_ref, u_ref, o_ref):
        o_ref[...] = (da_ref[...] * 2.0 * jnp.maximum(u_ref[...], 0.0)).astype(BF16)

    row = pl.BlockSpec((tb, d), lambda i: (i, 0))
    return pl.pallas_call(body, grid=(t // tb,), in_specs=[row, row], out_specs=row, out_shape=jax.ShapeDtypeStruct((t, d), BF16),
                          compiler_params=_cparams(("parallel",)), name="relu2_bwd")(dact, up)


def _lb_fwd(logits):
    def body(l_ref, o_ref):
        lg = l_ref[...]
        e = jnp.exp(lg - jnp.max(lg, axis=0, keepdims=True))
        p = e / jnp.sum(e, axis=0, keepdims=True)
        c = jnp.zeros_like(p[0:1])
        rows = [c]
        for l in range(1, DEPTH):
            c = c + p[l:l + 1]
            rows.append(c)
        o_ref[...] = jnp.minimum(jnp.maximum(jnp.concatenate(rows, axis=0), 0.0), 1.0 - EPS)

    return pl.pallas_call(body, out_shape=jax.ShapeDtypeStruct(logits.shape, F32), name="lb_fwd")(logits)


def _lb_bwd(logits, dlb):
    def body(l_ref, d_ref, o_ref):
        lg = l_ref[...]
        e = jnp.exp(lg - jnp.max(lg, axis=0, keepdims=True))
        p = e / jnp.sum(e, axis=0, keepdims=True)
        hi = 1.0 - EPS
        c = jnp.zeros_like(p[0:1])
        dc = []
        for l in range(1, DEPTH):
            c = c + p[l:l + 1]
            gl = jnp.where(c < 0.0, 0.0, jnp.where(c == 0.0, 0.5, 1.0)) * jnp.where(c > hi, 0.0, jnp.where(c == hi, 0.5, 1.0))
            dc.append(d_ref[l:l + 1, :] * gl)
        dp = [jnp.zeros_like(c)]
        for j in range(1, DEPTH):
            s = dc[j - 1]
            for l in range(j + 1, DEPTH):
                s = s + dc[l - 1]
            dp.append(s)
        dpm = jnp.concatenate(dp, axis=0)
        o_ref[...] = p * (dpm - jnp.sum(p * dpm, axis=0, keepdims=True))

    return pl.pallas_call(body, out_shape=jax.ShapeDtypeStruct(logits.shape, F32), name="lb_bwd")(logits, dlb)


def _a_gates(qi, fi, lbh):
    sq = _sigmoid(qi)
    q = qi * sq
    e = jnp.exp(-jnp.abs(fi))
    rec = 1.0 / (1.0 + e)
    pos = fi >= 0.0
    sg = jnp.where(pos, rec, e * rec)
    sgn = jnp.where(pos, e * rec, rec)
    f = lbh + (1.0 - lbh) * sg
    logf = jnp.log(jnp.maximum(f, TINY))
    k = (1.0 - lbh) * sgn
    return q, sq, sg, sgn, f, logf, k


def _a_intra(q, k, cum):
    subs, rows = [], []
    for i in range(CHUNK // SUB):
        lo = i * SUB
        r = cum[lo - 1:lo] if i > 0 else jnp.zeros_like(cum[0:1])
        eq = jnp.exp(cum[lo:lo + SUB] - r)
        ek = jnp.exp(jnp.minimum(r - cum, EXP_CLAMP))
        qt = q[lo:lo + SUB] * eq
        kt = k * ek
        rows.append(_dot(qt, kt, "nt", hi=True))
        subs.append((qt, eq, kt, ek))
    attn = jnp.where(_tril(CHUNK), jnp.concatenate(rows, axis=0), 0.0)
    return attn, subs


def _a_intra_bwd(dattn, subs):
    dq_rows, dk = [], None
    for i, (qt, eq, kt, ek) in enumerate(subs):
        da = dattn[i * SUB:(i + 1) * SUB]
        dq_rows.append(_dot(da, kt, "nn", hi=True) * eq)
        d = _dot(da, qt, "tn", hi=True) * ek
        dk = d if dk is None else dk + d
    return jnp.concatenate(dq_rows, axis=0), dk


def _headnorm_fwd(o, g, gate_in):
    rinv = lax.rsqrt(jnp.mean(o * o, axis=-1, keepdims=True) + EPS)
    sg = _sigmoid(gate_in)
    return o * rinv * g * (gate_in * sg)


def _headnorm_bwd(dout, o, g, gate_in):
    rinv = lax.rsqrt(jnp.mean(o * o, axis=-1, keepdims=True) + EPS)
    xhat = o * rinv
    sg = _sigmoid(gate_in)
    silu = gate_in * sg
    dy = dout * silu
    dgate = dout * xhat * g * (sg * (1.0 + gate_in * (1.0 - sg)))
    dxh = dy * g
    do = rinv * (dxh - xhat * jnp.mean(dxh * xhat, axis=-1, keepdims=True))
    return do, dgate, jnp.sum(dy * xhat, axis=0, keepdims=True)


def _a_fwd(proj, lb, norm_g, tb=256):
    t = proj.shape[0]
    nch = tb // CHUNK

    def body(q_ref, f_ref, i_ref, g_ref, lb_ref, ng_ref, out_ref, st_ref, s_scr):
        @pl.when(pl.program_id(0) == 0)
        def _():
            s_scr[...] = jnp.zeros_like(s_scr)

        ltri = _tril(CHUNK).astype(F32)

        def chunk(c, carry):
            rows = pl.ds(pl.multiple_of(c * CHUNK, CHUNK), CHUNK)
            hs = range(A_HEADS)
            cols = [slice(h * A_DIM, (h + 1) * A_DIM) for h in hs]
            gates = [_a_gates(q_ref[rows, cols[h]], f_ref[rows, cols[h]], lb_ref[:, cols[h]]) for h in hs]
            q, k = [gates[h][0] for h in hs], [gates[h][6] for h in hs]
            v = [i_ref[rows, cols[h]] for h in hs]
            cum = [_dot_exact_lhs(ltri, gates[h][5]) for h in hs]
            cl = [cum[h][CHUNK - 1:CHUNK] for h in hs]
            s0 = [s_scr[h] for h in hs]
            for h in hs:
                st_ref[c, h] = s0[h]
            attn = [_a_intra(q[h], k[h], cum[h])[0] for h in hs]
            qs0 = [_dot(q[h] * jnp.exp(cum[h]), s0[h]) for h in hs]
            o = [qs0[h] + _dot(attn[h], v[h]) for h in hs]
            kd = [k[h] * jnp.exp(cl[h] - cum[h]) for h in hs]
            for h in hs:
                s_scr[h] = s0[h] * jnp.exp(cl[h]).T + _dot(kd[h], v[h], "tn")
            outs = [_headnorm_fwd(o[h], ng_ref[...], g_ref[rows, cols[h]]) for h in hs]
            out_ref[rows, :] = jnp.concatenate(outs, axis=1).astype(BF16)
            return carry

        lax.fori_loop(0, nch, chunk, 0, unroll=2)

    colblk = lambda j: pl.BlockSpec((tb, A_WIDTH), lambda i, j=j: (i, j))
    return pl.pallas_call(
        body, grid=(t // tb,),
        in_specs=[colblk(0), colblk(1), colblk(2), colblk(3), pl.BlockSpec((1, A_WIDTH), lambda i: (0, 0)),
                  pl.BlockSpec((1, A_DIM), lambda i: (0, 0))],
        out_specs=[pl.BlockSpec((tb, A_WIDTH), lambda i: (i, 0)),
                   pl.BlockSpec((nch, A_HEADS, A_DIM, A_DIM), lambda i: (i, 0, 0, 0))],
        out_shape=[jax.ShapeDtypeStruct((t, A_WIDTH), BF16), jax.ShapeDtypeStruct((t // CHUNK, A_HEADS, A_DIM, A_DIM), F32)],
        scratch_shapes=[pltpu.VMEM((A_HEADS, A_DIM, A_DIM), F32)],
        compiler_params=_cparams(("arbitrary",)), name="hgrn_fwd")(proj, proj, proj, proj, lb.reshape(1, A_WIDTH), norm_g.reshape(1, A_DIM))


def _a_bwd(proj, lb, norm_g, states, dmixed, beside=None, tb=256):
    t = proj.shape[0]
    nch = tb // CHUNK
    nb = t // tb

    def body(q_ref, f_ref, i_ref, g_ref, lb_ref, ng_ref, st_ref, dm_ref, dp_ref, dlb_ref, dng_ref, ds_scr):
        @pl.when(pl.program_id(0) == 0)
        def _():
            ds_scr[...] = jnp.zeros_like(ds_scr)
            dlb_ref[...] = jnp.zeros_like(dlb_ref)
            dng_ref[...] = jnp.zeros_like(dng_ref)

        ltri = _tril(CHUNK).astype(F32)
        mask = _tril(CHUNK)

        def chunk(cc, carry):
            c = nch - 1 - cc
            rows = pl.ds(pl.multiple_of(c * CHUNK, CHUNK), CHUNK)
            hs = range(A_HEADS)
            cols = [slice(h * A_DIM, (h + 1) * A_DIM) for h in hs]
            qi = [q_ref[rows, cols[h]] for h in hs]
            gi = [g_ref[rows, cols[h]] for h in hs]
            lbh = [lb_ref[:, cols[h]] for h in hs]
            gates = [_a_gates(qi[h], f_ref[rows, cols[h]], lbh[h]) for h in hs]
            q, sq, sg, sgn, f, logf, k = ([gates[h][j] for h in hs] for j in range(7))
            v = [i_ref[rows, cols[h]] for h in hs]
            cum = [_dot_exact_lhs(ltri, logf[h]) for h in hs]
            cl = [cum[h][CHUNK - 1:CHUNK] for h in hs]
            ecum = [jnp.exp(cum[h]) for h in hs]
            ekd = [jnp.exp(cl[h] - cum[h]) for h in hs]
            cd = [jnp.exp(cl[h]) for h in hs]
            qd = [q[h] * ecum[h] for h in hs]
            kd = [k[h] * ekd[h] for h in hs]
            s0 = [st_ref[c, h] for h in hs]
            ds = [ds_scr[h] for h in hs]
            intra = [_a_intra(q[h], k[h], cum[h]) for h in hs]
            attn = [intra[h][0] for h in hs]
            qs0 = [_dot(qd[h], s0[h]) for h in hs]
            o = [qs0[h] + _dot(attn[h], v[h]) for h in hs]
            hn = [_headnorm_bwd(dm_ref[rows, cols[h]].astype(F32), o[h], ng_ref[...], gi[h]) for h in hs]
            do = [hn[h][0] for h in hs]
            dqd = [_dot(do[h], s0[h], "nt") for h in hs]
            dattn = [jnp.where(mask, _dot(do[h], v[h], "nt"), 0.0) for h in hs]
            dv = [_dot(attn[h], do[h], "tn") + _dot(kd[h], ds[h]) for h in hs]
            dkd = [_dot(v[h], ds[h], "nt") for h in hs]
            dcd = [jnp.sum((s0[h] * ds[h]).T, axis=0, keepdims=True) for h in hs]
            for h in hs:
                ds_scr[h] = _dot(qd[h], do[h], "tn") + ds[h] * cd[h].T
            ib = [_a_intra_bwd(dattn[h], intra[h][1]) for h in hs]
            dq = [dqd[h] * ecum[h] + ib[h][0] for h in hs]
            dk = [dkd[h] * ekd[h] + ib[h][1] for h in hs]
            dkk = [dkd[h] * kd[h] for h in hs]
            dcum = [dqd[h] * qd[h] - dkk[h] + q[h] * ib[h][0] - k[h] * ib[h][1] for h in hs]
            dcl = [jnp.sum(dkk[h], axis=0, keepdims=True) + dcd[h] * cd[h] for h in hs]
            dlogf = [_dot_exact_lhs(ltri, dcum[h], "tn") + dcl[h] for h in hs]
            dfv = [jnp.where(f[h] > TINY, dlogf[h] / f[h], 0.0) for h in hs]
            dfi = [dfv[h] * (1.0 - lbh[h]) * sg[h] * (1.0 - sg[h]) - dk[h] * (1.0 - lbh[h]) * sgn[h] * (1.0 - sgn[h]) for h in hs]
            dlbs = [jnp.sum(dfv[h] * (1.0 - sg[h]) - dk[h] * sgn[h], axis=0, keepdims=True) for h in hs]
            dqs = [dq[h] * (sq[h] * (1.0 + qi[h] * (1.0 - sq[h]))) for h in hs]
            dp_ref[rows, :] = jnp.concatenate(dqs + dfi + dv + [hn[h][1] for h in hs], axis=1).astype(BF16)
            dlb_ref[...] += jnp.concatenate(dlbs, axis=1)
            dng_ref[...] += sum(hn[h][2] for h in hs)
            return carry

        lax.fori_loop(0, nch, chunk, 0, unroll=2)

    colblk = lambda j: pl.BlockSpec((tb, A_WIDTH), lambda i, j=j: (nb - 1 - i, j))
    vec = lambda n: pl.BlockSpec((1, n), lambda i: (0, 0))
    return _call_beside(
        body, beside, nb, (proj, proj, proj, proj, lb.reshape(1, A_WIDTH), norm_g.reshape(1, A_DIM), states, dmixed), grid=(nb,),
        in_specs=[colblk(0), colblk(1), colblk(2), colblk(3), vec(A_WIDTH), vec(A_DIM),
                  pl.BlockSpec((nch, A_HEADS, A_DIM, A_DIM), lambda i: (nb - 1 - i, 0, 0, 0)), colblk(0)],
        out_specs=[pl.BlockSpec((tb, 4 * A_WIDTH), lambda i: (nb - 1 - i, 0)), vec(A_WIDTH), vec(A_DIM)],
        out_shape=[jax.ShapeDtypeStruct((t, 4 * A_WIDTH), BF16), jax.ShapeDtypeStruct((1, A_WIDTH), F32), jax.ShapeDtypeStruct((1, A_DIM), F32)],
        scratch_shapes=[pltpu.VMEM((A_HEADS, A_DIM, A_DIM), F32)], name="hgrn_bwd")


def _gate_lane_masks(shape):
    lane = lax.broadcasted_iota(jnp.int32, shape, 1)
    return lane < B_HEADS, (lane >= B_HEADS) & (lane < 2 * B_HEADS)


def _b_pre_fwd(proj, conv_w, alog_row, dtb_row, tb=512):
    t = proj.shape[0]
    cb0 = COL_B // B_WIDTH

    def body(q_ref, k_ref, v_ref, qp_ref, kp_ref, vp_ref, w_ref, gi_ref, al_ref, dt_ref, qkv_ref, gates_ref):
        first = pl.program_id(0) == 0
        for part, (c_ref, p_ref) in enumerate(((q_ref, qp_ref), (k_ref, kp_ref), (v_ref, vp_ref))):
            cols = slice(part * B_WIDTH, (part + 1) * B_WIDTH)
            prev = jnp.where(first, 0.0, p_ref[...])
            y = _conv_fwd(c_ref[...], prev, w_ref[:, cols])
            s = y * _sigmoid(y)
            if part < 2:
                outs = []
                for h in range(B_HEADS):
                    sh = s[:, h * B_DIM:(h + 1) * B_DIM]
                    outs.append(sh * lax.rsqrt(jnp.sum(sh * sh, axis=-1, keepdims=True) + EPS))
                s = jnp.concatenate(outs, axis=1)
            qkv_ref[:, cols] = s
        g = gi_ref[...]
        is_b, is_a = _gate_lane_masks(g.shape)
        la = -jnp.exp(al_ref[...]) * _softplus(g + dt_ref[...])
        gates_ref[...] = jnp.where(is_b, _sigmoid(g), jnp.where(is_a, la, 0.0))

    cur = lambda j: pl.BlockSpec((tb, B_WIDTH), lambda i, j=j: (i, cb0 + j))
    prv = lambda j: pl.BlockSpec((8, B_WIDTH), lambda i, j=j: (jnp.maximum(i * (tb // 8) - 1, 0), cb0 + j))
    vec = pl.BlockSpec((1, 128), lambda i: (0, 0))
    return pl.pallas_call(
        body, grid=(t // tb,),
        in_specs=[cur(0), cur(1), cur(2), prv(0), prv(1), prv(2), pl.BlockSpec((CONV_K, 3 * B_WIDTH), lambda i: (0, 0)),
                  pl.BlockSpec((tb, 128), lambda i: (i, COL_G // 128)), vec, vec],
        out_specs=[pl.BlockSpec((tb, 3 * B_WIDTH), lambda i: (i, 0)), pl.BlockSpec((tb, 128), lambda i: (i, 0))],
        out_shape=[jax.ShapeDtypeStruct((t, 3 * B_WIDTH), F32), jax.ShapeDtypeStruct((t, 128), F32)],
        compiler_params=_cparams(("parallel",)), name="gdn_pre_fwd")(proj, proj, proj, proj, proj, proj, conv_w, proj, alog_row, dtb_row)


def _inv_unit_lower(amats):
    r, c = _iota2(CHUNK, CHUNK)
    eye = jnp.where(r == c, 1.0, 0.0)
    ps = [eye - a for a in amats]
    aks = amats
    for _ in range(5):
        aks = [_dot(ak, ak, hi=True) for ak in aks]
        ps = [p + _dot(p, ak, hi=True) for p, ak in zip(ps, aks)]
    return ps


def _b_local(qs, ks, vs, betas, gcs, grows, gls, solve=True):
    hs = range(len(qs))
    causal, strict = _tril(CHUNK), _tril(CHUNK, strict=True)
    decay = [jnp.where(causal, jnp.exp(jnp.minimum(gcs[h] - grows[h], 0.0)), 0.0) for h in hs]
    kb = [ks[h] * betas[h] for h in hs]
    kk = [_dot(kb[h], ks[h], "nt") for h in hs]
    qkr = [_dot(qs[h], ks[h], "nt") for h in hs]
    eg = [jnp.exp(gcs[h]) for h in hs]
    bv = [vs[h] * betas[h] for h in hs]
    kg = [kb[h] * eg[h] for h in hs]
    qk = [qkr[h] * decay[h] for h in hs]
    qd = [qs[h] * eg[h] for h in hs]
    ekd = [jnp.exp(gls[h] - gcs[h]) for h in hs]
    kd = [ks[h] * ekd[h] for h in hs]
    cd = [jnp.exp(gls[h]) for h in hs]
    loc = dict(decay=decay, kb=kb, kk=kk, eg=eg, bv=bv, kg=kg, qkr=qkr, qk=qk, qd=qd, ekd=ekd, kd=kd, cd=cd)
    if solve:
        tinv = _inv_unit_lower([jnp.where(strict, kk[h] * decay[h], 0.0) for h in hs])
        loc.update(tinv=tinv, u=[_dot(tinv[h], bv[h], hi=True) for h in hs], w=[_dot(tinv[h], kg[h], hi=True) for h in hs])
    return loc


def _b_state(loc, ids, s0s):
    n = range(len(ids))
    ws = [_dot(loc["w"][ids[j]], s0s[j]) for j in n]
    qs0 = [_dot(loc["qd"][ids[j]], s0s[j]) for j in n]
    vn = [loc["u"][ids[j]] - ws[j] for j in n]
    o = [qs0[j] + _dot(loc["qk"][ids[j]], vn[j]) for j in n]
    s1 = [s0s[j] * loc["cd"][ids[j]] + _dot(loc["kd"][ids[j]], vn[j], "tn") for j in n]
    return vn, o, s1


def _b_fwd(qkv, gates, proj, norm_g, beside=None, tb=256):
    t = qkv.shape[0]
    nch = tb // CHUNK

    def body(q_ref, k_ref, v_ref, ga_ref, z_ref, ng_ref, out_ref, st_ref, ti_ref, w_ref, vn_ref, o_ref, s_scr):
        @pl.when(pl.program_id(0) == 0)
        def _():
            s_scr[...] = jnp.zeros_like(s_scr)

        ltri = _tril(CHUNK).astype(F32)

        hs = range(B_HEADS)
        cols = [slice(h * B_DIM, (h + 1) * B_DIM) for h in hs]

        def pair(p, carry):
            cs = [2 * p, 2 * p + 1]
            rows = [pl.ds(pl.multiple_of(c * CHUNK, CHUNK), CHUNK) for c in cs]
            ga = [ga_ref[r, :] for r in rows]
            gcum = [_dot_exact_lhs(ltri, g) for g in ga]
            gcum_t = [g.T for g in gcum]
            items = [(i, h) for i in range(2) for h in hs]
            loc = _b_local([q_ref[rows[i], cols[h]] * GDN_SCALE for i, h in items], [k_ref[rows[i], cols[h]] for i, h in items],
                           [v_ref[rows[i], cols[h]] for i, h in items], [ga[i][:, h:h + 1] for i, h in items],
                           [gcum[i][:, B_HEADS + h:B_HEADS + h + 1] for i, h in items],
                           [gcum_t[i][B_HEADS + h:B_HEADS + h + 1, :] for i, h in items],
                           [gcum[i][CHUNK - 1:CHUNK, B_HEADS + h:B_HEADS + h + 1] for i, h in items])
            s0s = [s_scr[h] for h in hs]
            for i in range(2):
                ids = [i * B_HEADS + h for h in hs]
                for h in hs:
                    st_ref[cs[i], h] = s0s[h]
                    ti_ref[cs[i], h] = loc["tinv"][ids[h]]
                vn, o, s0s = _b_state(loc, ids, s0s)
                w_ref[rows[i], :] = jnp.concatenate([loc["w"][j] for j in ids], axis=1)
                vn_ref[rows[i], :] = jnp.concatenate(vn, axis=1)
                o_ref[rows[i], :] = jnp.concatenate(o, axis=1)
                outs = [_headnorm_fwd(o[h], ng_ref[...], z_ref[rows[i], cols[h]]) for h in hs]
                out_ref[rows[i], :] = jnp.concatenate(outs, axis=1).astype(BF16)
            for h in hs:
                s_scr[h] = s0s[h]
            return carry

        lax.fori_loop(0, nch // 2, pair, 0)

    part = lambda j: pl.BlockSpec((tb, B_WIDTH), lambda i, j=j: (i, j))
    wide = pl.BlockSpec((tb, B_WIDTH), lambda i: (i, 0))
    wide_shape = jax.ShapeDtypeStruct((t, B_WIDTH), F32)
    return _call_beside(
        body, beside, t // tb, (qkv, qkv, qkv, gates, proj, norm_g.reshape(1, B_DIM)), grid=(t // tb,),
        in_specs=[part(0), part(1), part(2), pl.BlockSpec((tb, 128), lambda i: (i, 0)),
                  pl.BlockSpec((tb, B_WIDTH), lambda i: (i, COL_B // B_WIDTH + 3)), pl.BlockSpec((1, B_DIM), lambda i: (0, 0))],
        out_specs=[wide, pl.BlockSpec((nch, B_HEADS, B_DIM, B_DIM), lambda i: (i, 0, 0, 0)),
                   pl.BlockSpec((nch, B_HEADS, CHUNK, CHUNK), lambda i: (i, 0, 0, 0)), wide, wide, wide],
        out_shape=[jax.ShapeDtypeStruct((t, B_WIDTH), BF16), jax.ShapeDtypeStruct((t // CHUNK, B_HEADS, B_DIM, B_DIM), F32),
                   jax.ShapeDtypeStruct((t // CHUNK, B_HEADS, CHUNK, CHUNK), F32), wide_shape, wide_shape, wide_shape],
        scratch_shapes=[pltpu.VMEM((B_HEADS, B_DIM, B_DIM), F32)], name="gdn_fwd")


def _b_bwd(qkv, gates, proj, norm_g, states, fwd_saved, dmixed, beside=None, tb=256):
    t = qkv.shape[0]
    nch = tb // CHUNK
    nb = t // tb

    def body(q_ref, k_ref, v_ref, ga_ref, z_ref, ng_ref, st_ref, ti_ref, w_ref, vn_ref, o_ref, dm0_ref, dm1_ref,
             dqkv_ref, dga_ref, dz_ref, dng_ref, ds_scr):
        @pl.when(pl.program_id(0) == 0)
        def _():
            ds_scr[...] = jnp.zeros_like(ds_scr)
            dng_ref[...] = jnp.zeros_like(dng_ref)

        ltri = _tril(CHUNK).astype(F32)
        strict = _tril(CHUNK, strict=True)
        lane = lax.broadcasted_iota(jnp.int32, (CHUNK, 128), 1)
        lane1 = lax.broadcasted_iota(jnp.int32, (1, 128), 1)

        nh = range(B_HEADS)
        cols = [slice(h * B_DIM, (h + 1) * B_DIM) for h in nh]
        rsum = lambda a: jnp.sum(a, axis=-1, keepdims=True)

        def pair(p, carry):
            cs = [nch - 1 - 2 * p, nch - 2 - 2 * p]
            crow = [pl.ds(pl.multiple_of(c * CHUNK, CHUNK), CHUNK) for c in cs]
            gas = [ga_ref[r, :] for r in crow]
            gcum = [_dot_exact_lhs(ltri, g) for g in gas]
            gcum_t = [g.T for g in gcum]
            items = [(i, h) for i in range(2) for h in nh]
            hs = range(len(items))
            q = [q_ref[crow[i], cols[h]] * GDN_SCALE for i, h in items]
            k = [k_ref[crow[i], cols[h]] for i, h in items]
            v = [v_ref[crow[i], cols[h]] for i, h in items]
            z = [z_ref[crow[i], cols[h]] for i, h in items]
            beta = [gas[i][:, h:h + 1] for i, h in items]
            s0 = [st_ref[cs[i], h] for i, h in items]
            r = _b_local(q, k, v, beta, [gcum[i][:, B_HEADS + h:B_HEADS + h + 1] for i, h in items],
                         [gcum_t[i][B_HEADS + h:B_HEADS + h + 1, :] for i, h in items],
                         [gcum[i][CHUNK - 1:CHUNK, B_HEADS + h:B_HEADS + h + 1] for i, h in items], solve=False)
            tinv = [ti_ref[cs[i], h] for i, h in items]
            w = [w_ref[crow[i], cols[h]] for i, h in items]
            vn = [vn_ref[crow[i], cols[h]] for i, h in items]
            decay, eg, qd, kd, kb, cd = (r[n] for n in ("decay", "eg", "qd", "kd", "kb", "cd"))
            dms = [(dm0_ref if h < 2 else dm1_ref)[crow[i], (h % 2) * B_DIM:(h % 2 + 1) * B_DIM].astype(F32) for i, h in items]
            hn = [_headnorm_bwd(dms[j], o_ref[crow[i], cols[h]], ng_ref[...], z[j]) for j, (i, h) in enumerate(items)]
            do = [hn[j][0] for j in hs]
            dvn_o = [_dot(r["qk"][j], do[j], "tn") for j in hs]
            dqk = [_dot(do[j], vn[j], "nt") for j in hs]
            dqd = [_dot(do[j], s0[j], "nt") for j in hs]
            ds_o = [_dot(qd[j], do[j], "tn") for j in hs]
            ds = [ds_scr[h] for h in nh]
            dvn, dkd, dcd = [None] * 8, [None] * 8, [None] * 8
            for i in range(2):
                for h in nh:
                    j = i * B_HEADS + h
                    dvn[j] = dvn_o[j] + _dot(kd[j], ds[h])
                    dkd[j] = _dot(vn[j], ds[h], "nt")
                    dcd[j] = jnp.sum(jnp.sum(s0[j] * ds[h], axis=0, keepdims=True), axis=1, keepdims=True)
                ds = [ds_o[i * B_HEADS + h] + ds[h] * cd[i * B_HEADS + h] - _dot(w[i * B_HEADS + h], dvn[i * B_HEADS + h], "tn")
                      for h in nh]
            for h in nh:
                ds_scr[h] = ds[h]
            dw = [-_dot(dvn[j], s0[j], "nt") for j in hs]
            dbv = [_dot(tinv[h], dvn[h], "tn", hi=True) for h in hs]
            dkg = [_dot(tinv[h], dw[h], "tn", hi=True) for h in hs]
            dt = [_dot(dvn[h], r["bv"][h], "nt", hi=True) + _dot(dw[h], r["kg"][h], "nt", hi=True) for h in hs]
            tdt = [_dot(tinv[h], dt[h], "tn", hi=True) for h in hs]
            da = [jnp.where(strict, -_dot(tdt[h], tinv[h], "nt", hi=True), 0.0) for h in hs]
            dm = [da[h] * decay[h] for h in hs]
            dn = [dqk[h] * decay[h] for h in hs]
            e = [(da[h] * r["kk"][h] + dqk[h] * r["qkr"][h]) * decay[h] for h in hs]
            dkb = [_dot(dm[h], k[h]) + dkg[h] * eg[h] for h in hs]
            dk = [_dot(dm[h], kb[h], "tn") + _dot(dn[h], q[h], "tn") + dkd[h] * r["ekd"][h] + dkb[h] * beta[h] for h in hs]
            dq = [_dot(dn[h], k[h]) + dqd[h] * eg[h] for h in hs]
            tkd = [rsum(dkd[h] * kd[h]) for h in hs]
            dgc = [rsum(e[h]) - rsum(e[h].T) + rsum(dqd[h] * qd[h]) - tkd[h] + rsum(dkg[h] * r["kg"][h]) for h in hs]
            dgl = [jnp.sum(tkd[h], axis=0, keepdims=True) + dcd[h] * cd[h] for h in hs]
            dbeta = [rsum(dbv[h] * v[h]) + rsum(dkb[h] * k[h]) for h in hs]
            for i in range(2):
                ids = [i * B_HEADS + h for h in nh]
                dbeta_m = sum(jnp.where(lane == h, dbeta[ids[h]], 0.0) for h in nh)
                dgc_m = sum(jnp.where(lane == B_HEADS + h, dgc[ids[h]], 0.0) for h in nh)
                dgl_m = sum(jnp.where(lane1 == B_HEADS + h, dgl[ids[h]], 0.0) for h in nh)
                dqkv_ref[crow[i], :] = jnp.concatenate(
                    [dq[j] * GDN_SCALE for j in ids] + [dk[j] for j in ids] + [dbv[j] * beta[j] for j in ids], axis=1)
                dz_ref[crow[i], :] = jnp.concatenate([hn[j][1] for j in ids], axis=1).astype(BF16)
                dga_ref[crow[i], :] = dbeta_m + _dot_exact_lhs(ltri, dgc_m, "tn") + dgl_m
            dng_ref[...] += sum(hn[j][2] for j in hs)
            return carry

        lax.fori_loop(0, nch // 2, pair, 0)

    part = lambda j: pl.BlockSpec((tb, B_WIDTH), lambda i, j=j: (nb - 1 - i, j))
    rowblk = lambda w, j=0: pl.BlockSpec((tb, w), lambda i, j=j: (nb - 1 - i, j))
    return _call_beside(
        body, beside, nb, (qkv, qkv, qkv, gates, proj, norm_g.reshape(1, B_DIM), states, *fwd_saved, dmixed, dmixed), grid=(nb,),
        in_specs=[part(0), part(1), part(2), rowblk(128), rowblk(B_WIDTH, COL_B // B_WIDTH + 3),
                  pl.BlockSpec((1, B_DIM), lambda i: (0, 0)),
                  pl.BlockSpec((nch, B_HEADS, B_DIM, B_DIM), lambda i: (nb - 1 - i, 0, 0, 0)),
                  pl.BlockSpec((nch, B_HEADS, CHUNK, CHUNK), lambda i: (nb - 1 - i, 0, 0, 0)),
                  rowblk(B_WIDTH), rowblk(B_WIDTH), rowblk(B_WIDTH), rowblk(256, 1), rowblk(256, 2)],
        out_specs=[rowblk(3 * B_WIDTH), rowblk(128), rowblk(B_WIDTH), pl.BlockSpec((1, B_DIM), lambda i: (0, 0))],
        out_shape=[jax.ShapeDtypeStruct((t, 3 * B_WIDTH), F32), jax.ShapeDtypeStruct((t, 128), F32),
                   jax.ShapeDtypeStruct((t, B_WIDTH), BF16), jax.ShapeDtypeStruct((1, B_DIM), F32)],
        scratch_shapes=[pltpu.VMEM((B_HEADS, B_DIM, B_DIM), F32)], name="gdn_bwd")


def _b_pre_bwd(proj, conv_w, alog_row, dtb_row, dqkv, dgates, tb=512):
    t = proj.shape[0]
    nb = t // tb
    cb0 = COL_B // B_WIDTH

    def body(q_ref, k_ref, v_ref, qp_ref, kp_ref, vp_ref, w_ref, gi_ref, al_ref, dt_ref, dqkv_ref, dga_ref,
             dy_ref, dgi_ref, dw_ref, dal_ref, ddt_ref, nxt_scr):
        step_id = pl.program_id(0)
        first = step_id == nb - 1

        @pl.when(step_id == 0)
        def _():
            dw_ref[...] = jnp.zeros_like(dw_ref)
            dal_ref[...] = jnp.zeros_like(dal_ref)
            ddt_ref[...] = jnp.zeros_like(ddt_ref)

        for part, (c_ref, p_ref) in enumerate(((q_ref, qp_ref), (k_ref, kp_ref), (v_ref, vp_ref))):
            cols = slice(part * B_WIDTH, (part + 1) * B_WIDTH)
            cur = c_ref[...]
            prev = jnp.where(first, 0.0, p_ref[...])
            w = w_ref[:, cols]
            shifted = [_shift_rows(cur, prev, 3 - j, down=True) for j in range(3)] + [cur]
            y = shifted[0] * w[0:1] + shifted[1] * w[1:2] + shifted[2] * w[2:3] + shifted[3] * w[3:4]
            sg = _sigmoid(y)
            s = y * sg
            dsn = dqkv_ref[:, cols]
            if part < 2:
                outs = []
                for h in range(B_HEADS):
                    hc = slice(h * B_DIM, (h + 1) * B_DIM)
                    sh, dh = s[:, hc], dsn[:, hc]
                    rq = lax.rsqrt(jnp.sum(sh * sh, axis=-1, keepdims=True) + EPS)
                    nh = sh * rq
                    outs.append(rq * (dh - nh * jnp.sum(dh * nh, axis=-1, keepdims=True)))
                dsn = jnp.concatenate(outs, axis=1)
            dy = dsn * (sg * (1.0 + y * (1.0 - sg)))
            dy_ref[:, cols] = _conv_bwd_rows(dy, jnp.where(step_id == 0, 0.0, nxt_scr[:, cols]), w).astype(BF16)
            nxt_scr[:, cols] = dy[0:8]
            dw_ref[:, cols] += jnp.concatenate([jnp.sum(shifted[j] * dy, axis=0, keepdims=True) for j in range(CONV_K)], axis=0)
        g = gi_ref[...]
        dga = dga_ref[...]
        is_b, is_a = _gate_lane_masks(g.shape)
        beta = _sigmoid(g)
        pre = g + dt_ref[...]
        ea = jnp.exp(al_ref[...])
        la = -ea * _softplus(pre)
        dpre = jnp.where(is_a, dga * (-ea) * _sigmoid(pre), 0.0)
        dgi_ref[...] = jnp.where(is_b, dga * beta * (1.0 - beta), dpre).astype(BF16)
        dal_ref[...] += jnp.sum(jnp.where(is_a, dga * la, 0.0), axis=0, keepdims=True)
        ddt_ref[...] += jnp.sum(dpre, axis=0, keepdims=True)

    cur = lambda j: pl.BlockSpec((tb, B_WIDTH), lambda i, j=j: (nb - 1 - i, cb0 + j))
    prv = lambda j: pl.BlockSpec((8, B_WIDTH), lambda i, j=j: (jnp.maximum((nb - 1 - i) * (tb // 8) - 1, 0), cb0 + j))
    vec = pl.BlockSpec((1, 128), lambda i: (0, 0))
    wspec = pl.BlockSpec((CONV_K, 3 * B_WIDTH), lambda i: (0, 0))
    rowblk = lambda width, j=0: pl.BlockSpec((tb, width), lambda i, j=j: (nb - 1 - i, j))
    return pl.pallas_call(
        body, grid=(nb,),
        in_specs=[cur(0), cur(1), cur(2), prv(0), prv(1), prv(2), wspec, rowblk(128, COL_G // 128), vec, vec,
                  rowblk(3 * B_WIDTH), rowblk(128)],
        out_specs=[rowblk(3 * B_WIDTH), rowblk(128), wspec, vec, vec],
        out_shape=[jax.ShapeDtypeStruct((t, 3 * B_WIDTH), BF16), jax.ShapeDtypeStruct((t, 128), BF16),
                   jax.ShapeDtypeStruct((CONV_K, 3 * B_WIDTH), F32), jax.ShapeDtypeStruct((1, 128), F32), jax.ShapeDtypeStruct((1, 128), F32)],
        scratch_shapes=[pltpu.VMEM((8, 3 * B_WIDTH), F32)],
        compiler_params=_cparams(("arbitrary",)), name="gdn_pre_bwd")(
            proj, proj, proj, proj, proj, proj, conv_w, proj, alog_row, dtb_row, dqkv, dgates)


def _conv_bwd_x(dy, w, cb, tb=512, name="conv_bwd_x"):
    t, c = dy.shape
    nb = t // tb

    def body(dy_ref, nx_ref, w_ref, dx_ref):
        cur = dy_ref[...]
        nxt = jnp.where(pl.program_id(0) == nb - 1, 0.0, nx_ref[...])
        w = w_ref[...]
        dx = cur * w[3:4]
        for j in range(3):
            dx = dx + _shift_rows(cur, nxt, 3 - j, down=False) * w[j:j + 1]
        dx_ref[...] = dx.astype(BF16)

    return pl.pallas_call(
        body, grid=(nb, c // cb),
        in_specs=[pl.BlockSpec((tb, cb), lambda i, j: (i, j)),
                  pl.BlockSpec((8, cb), lambda i, j: (jnp.minimum((i + 1) * (tb // 8), t // 8 - 1), j)),
                  pl.BlockSpec((CONV_K, cb), lambda i, j: (0, j))],
        out_specs=pl.BlockSpec((tb, cb), lambda i, j: (i, j)), out_shape=jax.ShapeDtypeStruct((t, c), BF16),
        compiler_params=_cparams(("parallel", "parallel")), name=name)(dy, dy, w)


def _c_gates(xc, wa_ref, ba_ref, wx_ref, bx_ref, lam_ref, is_row0):
    r = _sigmoid(_dot(xc, wa_ref[...]) + ba_ref[...])
    i = _sigmoid(_dot(xc, wx_ref[...]) + bx_ref[...])
    sp = _softplus(-lam_ref[...])
    log_a = -RG_C * r * sp
    a = jnp.exp(log_a)
    m2 = _neg_expm1(2.0 * log_a)
    mult = jnp.where(is_row0, 1.0, jnp.sqrt(jnp.maximum(m2, EPS)))
    return r, i, sp, log_a, a, m2, mult


def _row0_mask(tb, first):
    ridx = lax.broadcasted_iota(jnp.int32, (tb, C_WIDTH), 0)
    return (ridx == 0) & first


def _c_fwd(proj, conv_w, conv_b, wa, ba, wx, bx, lam, tb=512):
    t = proj.shape[0]
    cbx = COL_C // C_WIDTH

    def body(x_ref, xp_ref, y_ref, w_ref, cb_ref, wa_ref, ba_ref, wx_ref, bx_ref, lam_ref, out_ref, h_ref, a_scr, b_scr, h_scr):
        first = pl.program_id(0) == 0

        @pl.when(first)
        def _():
            h_scr[...] = jnp.zeros_like(h_scr)

        prev = jnp.where(first, 0.0, xp_ref[...])
        xc = _conv_fwd(x_ref[...], prev, w_ref[...]) + cb_ref[...]
        _, i, _, _, a, _, mult = _c_gates(xc, wa_ref, ba_ref, wx_ref, bx_ref, lam_ref, _row0_mask(tb, first))
        ta, tb_ = _tile_scan(a, mult * i * xc)
        a_scr[...] = ta
        b_scr[...] = tb_

        def step(blk, h):
            rows = pl.ds(pl.multiple_of(blk * 8, 8), 8)
            h_ref[rows, :] = jnp.broadcast_to(h, (8, C_WIDTH))
            return a_scr[rows, :][7:8] * h + b_scr[rows, :][7:8]

        h_scr[...] = lax.fori_loop(0, tb // 8, step, h_scr[...], unroll=8)
        hs = ta * h_ref[...] + tb_
        h_ref[...] = hs
        gl, _ = _gelu_tanh(y_ref[...])
        out_ref[...] = (gl * hs).astype(BF16)

    vec = pl.BlockSpec((1, C_WIDTH), lambda i: (0, 0))
    mat = pl.BlockSpec((C_WIDTH, C_WIDTH), lambda i: (0, 0))
    row = pl.BlockSpec((tb, C_WIDTH), lambda i: (i, 0))
    return pl.pallas_call(
        body, grid=(t // tb,),
        in_specs=[pl.BlockSpec((tb, C_WIDTH), lambda i: (i, cbx)),
                  pl.BlockSpec((8, C_WIDTH), lambda i: (jnp.maximum(i * (tb // 8) - 1, 0), cbx)),
                  pl.BlockSpec((tb, C_WIDTH), lambda i: (i, cbx + 1)),
                  pl.BlockSpec((CONV_K, C_WIDTH), lambda i: (0, 0)), vec, mat, vec, mat, vec, vec],
        out_specs=[row, row],
        out_shape=[jax.ShapeDtypeStruct((t, C_WIDTH), BF16), jax.ShapeDtypeStruct((t, C_WIDTH), F32)],
        scratch_shapes=[pltpu.VMEM((tb, C_WIDTH), F32), pltpu.VMEM((tb, C_WIDTH), F32), pltpu.VMEM((1, C_WIDTH), F32)],
        compiler_params=_cparams(("arbitrary",)), name="lru_fwd")(proj, proj, proj, conv_w, conv_b, wa, ba, wx, bx, lam)


def _c_bwd(proj, conv_w, conv_b, wa, ba, wx, bx, lam, hs, dmixed, tb=512):
    t = proj.shape[0]
    nb = t // tb
    cbx = COL_C // C_WIDTH

    def body(x_ref, xp_ref, y_ref, w_ref, cb_ref, wa_ref, ba_ref, wx_ref, bx_ref, lam_ref, h_ref, hp_ref, dm_ref,
             dxc_ref, dyg_ref, dw_ref, dcb_ref, dwa_ref, dba_ref, dwx_ref, dbx_ref, dlam_ref, g_scr, a_scr, cin_scr, c_scr, nxt_scr):
        step_id = pl.program_id(0)
        first = step_id == nb - 1

        @pl.when(step_id == 0)
        def _():
            c_scr[...] = jnp.zeros_like(c_scr)
            for ref in (dw_ref, dcb_ref, dwa_ref, dba_ref, dwx_ref, dbx_ref, dlam_ref):
                ref[...] = jnp.zeros_like(ref)

        cur = x_ref[...]
        prev = jnp.where(first, 0.0, xp_ref[...])
        w = w_ref[...]
        shifted = [_shift_rows(cur, prev, 3 - j, down=True) for j in range(3)] + [cur]
        xc = shifted[0] * w[0:1] + shifted[1] * w[1:2] + shifted[2] * w[2:3] + shifted[3] * w[3:4] + cb_ref[...]
        row0 = _row0_mask(tb, first)
        r, i, sp, log_a, a, m2, mult = _c_gates(xc, wa_ref, ba_ref, wx_ref, bx_ref, lam_ref, row0)
        h = h_ref[...]
        hprev = _shift_rows(h, jnp.where(first, 0.0, hp_ref[...]), 1, down=True)
        gl, dgl = _gelu_tanh(y_ref[...])
        dm = dm_ref[...].astype(F32)
        dyg_ref[...] = (dm * h * dgl).astype(BF16)
        dout = dm * gl
        ta, te = _tile_scan(a, a * dout, reverse=True)
        a_scr[...] = ta
        g_scr[...] = te

        def step(blk, carry):
            rows = pl.ds(pl.multiple_of((tb // 8 - 1 - blk) * 8, 8), 8)
            cin_scr[rows, :] = jnp.broadcast_to(carry, (8, C_WIDTH))
            return a_scr[rows, :][0:1] * carry + g_scr[rows, :][0:1]

        c_scr[...] = lax.fori_loop(0, tb // 8, step, c_scr[...], unroll=8)
        cin = cin_scr[...]
        cout = ta * cin + te
        last_in_tile = lax.broadcasted_iota(jnp.int32, (tb, C_WIDTH), 0) % 8 == 7
        dbx = dout + jnp.where(last_in_tile, cin, pltpu.roll(cout, tb - 1, 0))
        da = dbx * hprev
        dmult = jnp.where(row0, 0.0, dbx * i * xc)
        di = dbx * mult * xc
        dxc = dbx * mult * i
        dm2 = jnp.where(m2 > EPS, dmult * 0.5 / mult, 0.0)
        dlog_a = da * a - 2.0 * a * a * dm2
        dr = dlog_a * (-RG_C) * sp
        dlam_ref[...] += jnp.sum(dlog_a * (-RG_C) * r, axis=0, keepdims=True) * (-_sigmoid(-lam_ref[...]))
        dpa = dr * r * (1.0 - r)
        dpx = di * i * (1.0 - i)
        dba_ref[...] += jnp.sum(dpa, axis=0, keepdims=True)
        dbx_ref[...] += jnp.sum(dpx, axis=0, keepdims=True)
        dwa_ref[...] += _dot(xc, dpa, "tn")
        dwx_ref[...] += _dot(xc, dpx, "tn")
        dxc = dxc + _dot(dpa, wa_ref[...], "nt") + _dot(dpx, wx_ref[...], "nt")
        dxc_ref[...] = _conv_bwd_rows(dxc, jnp.where(step_id == 0, 0.0, nxt_scr[...]), w).astype(BF16)
        nxt_scr[...] = dxc[0:8]
        dcb_ref[...] += jnp.sum(dxc, axis=0, keepdims=True)
        dw_ref[...] += jnp.concatenate([jnp.sum(shifted[j] * dxc, axis=0, keepdims=True) for j in range(CONV_K)], axis=0)

    vec = pl.BlockSpec((1, C_WIDTH), lambda i: (0, 0))
    mat = pl.BlockSpec((C_WIDTH, C_WIDTH), lambda i: (0, 0))
    cw = pl.BlockSpec((CONV_K, C_WIDTH), lambda i: (0, 0))
    row = lambda j=0: pl.BlockSpec((tb, C_WIDTH), lambda i, j=j: (nb - 1 - i, j))
    halo = lambda j=0: pl.BlockSpec((8, C_WIDTH), lambda i, j=j: (jnp.maximum((nb - 1 - i) * (tb // 8) - 1, 0), j))
    return pl.pallas_call(
        body, grid=(nb,),
        in_specs=[row(cbx), halo(cbx), row(cbx + 1), cw, vec, mat, vec, mat, vec, vec, row(), halo(), row(3)],
        out_specs=[row(), row(), cw, vec, mat, vec, mat, vec, vec],
        out_shape=[jax.ShapeDtypeStruct((t, C_WIDTH), BF16), jax.ShapeDtypeStruct((t, C_WIDTH), BF16),
                   jax.ShapeDtypeStruct((CONV_K, C_WIDTH), F32), jax.ShapeDtypeStruct((1, C_WIDTH), F32),
                   jax.ShapeDtypeStruct((C_WIDTH, C_WIDTH), F32), jax.ShapeDtypeStruct((1, C_WIDTH), F32),
                   jax.ShapeDtypeStruct((C_WIDTH, C_WIDTH), F32), jax.ShapeDtypeStruct((1, C_WIDTH), F32),
                   jax.ShapeDtypeStruct((1, C_WIDTH), F32)],
        scratch_shapes=[pltpu.VMEM((tb, C_WIDTH), F32), pltpu.VMEM((tb, C_WIDTH), F32), pltpu.VMEM((tb, C_WIDTH), F32),
                        pltpu.VMEM((1, C_WIDTH), F32), pltpu.VMEM((8, C_WIDTH), F32)],
        compiler_params=_cparams(("arbitrary",)), name="lru_bwd")(
            proj, proj, proj, conv_w, conv_b, wa, ba, wx, bx, lam, hs, hs, dmixed)


def _mesh_pos():
    return lax.axis_index("x"), lax.axis_index("y"), lax.axis_index("c")


def _all_gather(x, name):
    def body(x_ref, out_ref, send_sems, recv_sems, local_sem):
        mx, my, mc = _mesh_pos()
        me, sibling = (mx, my, mc), (mx, my, 1 - mc)
        chips = [(1 - mx, my), (mx, 1 - my), (1 - mx, 1 - my)]

        def slot(px, py, pc):
            return out_ref.at[4 * px + 2 * py + pc]

        def copy(k, block, to, src=None):
            return pltpu.make_async_remote_copy(
                src_ref=slot(*block) if src is None else src, dst_ref=slot(*block),
                send_sem=send_sems.at[k], recv_sem=recv_sems.at[k], device_id=to, device_id_type=MESH)

        mine = pltpu.make_async_copy(x_ref, slot(*me), local_sem)
        mine.start()
        first = [copy(0, me, sibling, src=x_ref)]
        first += [copy(1 + j, me, (*chip, mc), src=x_ref) for j, chip in enumerate(chips)]
        for cp in first:
            cp.start()
        passed = [copy(4 + j, (*chip, mc), sibling) for j, chip in enumerate(chips)]
        for j, chip in enumerate(chips):
            copy(1 + j, (*chip, mc), me).wait_recv()
            passed[j].start()
        copy(0, sibling, me).wait_recv()
        for j, chip in enumerate(chips):
            copy(4 + j, (*chip, 1 - mc), me).wait_recv()
        for cp in first + passed:
            cp.wait_send()
        mine.wait()

    return pl.pallas_call(
        body, out_shape=jax.ShapeDtypeStruct((N_DEV,) + x.shape, x.dtype),
        in_specs=[pl.BlockSpec(memory_space=pl.ANY)], out_specs=pl.BlockSpec(memory_space=pl.ANY),
        scratch_shapes=[pltpu.SemaphoreType.DMA((7,)), pltpu.SemaphoreType.DMA((7,)), pltpu.SemaphoreType.DMA(())],
        name=name)(x)


def _all_to_all(x, name):
    def body(x_ref, out_ref, send_sems, recv_sems, local_sem):
        mx, my, mc = _mesh_pos()
        me = 4 * mx + 2 * my + mc
        mine = pltpu.make_async_copy(x_ref.at[me], out_ref.at[me], local_sem)
        mine.start()
        copies = []
        for k in range(1, N_DEV):
            px = 1 - mx if k & 4 else mx
            py = 1 - my if k & 2 else my
            pc = 1 - mc if k & 1 else mc
            copies.append(pltpu.make_async_remote_copy(
                src_ref=x_ref.at[4 * px + 2 * py + pc], dst_ref=out_ref.at[me],
                send_sem=send_sems.at[k - 1], recv_sem=recv_sems.at[k - 1], device_id=(px, py, pc), device_id_type=MESH))
        for cp in copies:
            cp.start()
        for cp in copies:
            cp.wait()
        mine.wait()

    return pl.pallas_call(
        body, out_shape=jax.ShapeDtypeStruct(x.shape, x.dtype),
        in_specs=[pl.BlockSpec(memory_space=pl.ANY)], out_specs=pl.BlockSpec(memory_space=pl.ANY),
        scratch_shapes=[pltpu.SemaphoreType.DMA((7,)), pltpu.SemaphoreType.DMA((7,)), pltpu.SemaphoreType.DMA(())],
        name=name)(x)


class _Exchange:
    def __init__(self, arrays, layouts):
        self.arrays, self.layouts = list(arrays), list(layouts)
        self.out_shapes = []
        for a, lay in zip(self.arrays, self.layouts):
            if lay == 'a2a':
                shp = a.shape
            elif lay == 'slot':
                shp = (N_DEV,) + a.shape
            elif lay == 'rows':
                shp = (N_DEV * a.shape[0], a.shape[1])
            else:
                shp = (a.shape[0], N_DEV * a.shape[1])
            self.out_shapes.append(jax.ShapeDtypeStruct(shp, a.dtype))
        n = len(self.arrays)
        self.scratch = [pltpu.SemaphoreType.DMA((7 * n,)), pltpu.SemaphoreType.DMA((7 * n,)), pltpu.SemaphoreType.DMA((n,))]

    def _landing(self, a, dst_ref, idx):
        lay, shape = self.layouts[a], self.arrays[a].shape
        if lay in ('a2a', 'slot'):
            return dst_ref.at[idx]
        if lay == 'rows':
            return dst_ref.at[pl.ds(pl.multiple_of(idx * shape[0], shape[0]), shape[0]), :]
        return dst_ref.at[:, pl.ds(pl.multiple_of(idx * shape[1], shape[1]), shape[1])]

    def copies(self, src_refs, dst_refs, send_sems, recv_sems, local_sems):
        mx, my, mc = _mesh_pos()
        me = 4 * mx + 2 * my + mc
        out = []
        for a, (src, dst) in enumerate(zip(src_refs, dst_refs)):
            a2a = self.layouts[a] == 'a2a'
            out.append(pltpu.make_async_copy(src.at[me] if a2a else src, self._landing(a, dst, me), local_sems.at[a]))
            for k in range(1, N_DEV):
                px = 1 - mx if k & 4 else mx
                py = 1 - my if k & 2 else my
                pc = 1 - mc if k & 1 else mc
                out.append(pltpu.make_async_remote_copy(
                    src_ref=src.at[4 * px + 2 * py + pc] if a2a else src, dst_ref=self._landing(a, dst, me),
                    send_sem=send_sems.at[7 * a + k - 1], recv_sem=recv_sems.at[7 * a + k - 1],
                    device_id=(px, py, pc), device_id_type=MESH))
        return out


_ANY = pl.BlockSpec(memory_space=pl.ANY)


def _run_exchange(ex, name):
    n = len(ex.arrays)

    def body(*refs):
        cps = ex.copies(refs[:n], refs[n:2 * n], *refs[2 * n:])
        for cp in cps:
            cp.start()
        for cp in cps:
            cp.wait()

    return pl.pallas_call(body, out_shape=ex.out_shapes, in_specs=[_ANY] * n, out_specs=[_ANY] * n,
                          scratch_shapes=ex.scratch, name=name)(*ex.arrays)


def _call_beside(body, ex, nsteps, args, *, grid, in_specs, out_specs, out_shape, scratch_shapes, name):
    if ex is None:
        outs = pl.pallas_call(body, grid=grid, in_specs=in_specs, out_specs=out_specs, out_shape=out_shape,
                              scratch_shapes=scratch_shapes, compiler_params=_cparams(("arbitrary",)), name=name)(*args)
        return outs, None
    n_in, n_out, n_scr, n = len(in_specs), len(out_specs), len(scratch_shapes), len(ex.arrays)

    def wrapped(*refs):
        ins, refs = refs[:n_in], refs[n_in:]
        ex_ins, refs = refs[:n], refs[n:]
        outs, refs = refs[:n_out], refs[n_out:]
        ex_outs, refs = refs[:n], refs[n:]
        scr, sems = refs[:n_scr], refs[n_scr:]
        step = pl.program_id(0)

        @pl.when(step == 0)
        def _():
            for cp in ex.copies(ex_ins, ex_outs, *sems):
                cp.start()

        body(*ins, *outs, *scr)

        @pl.when(step == nsteps - 1)
        def _():
            for cp in ex.copies(ex_ins, ex_outs, *sems):
                cp.wait()

    res = pl.pallas_call(
        wrapped, grid=grid, in_specs=list(in_specs) + [_ANY] * n, out_specs=list(out_specs) + [_ANY] * n,
        out_shape=list(out_shape) + ex.out_shapes, scratch_shapes=list(scratch_shapes) + ex.scratch,
        compiler_params=_cparams(("arbitrary",)), name=name)(*args, *ex.arrays)
    return res[:n_out], res[n_out:]


def _adamw_math(w, g, m, v):
    m = ADAM_B1 * m + (1.0 - ADAM_B1) * g
    v = ADAM_B2 * v + (1.0 - ADAM_B2) * (g * g)
    m_hat = m / (1.0 - ADAM_B1 ** ADAM_STEP)
    v_hat = v / (1.0 - ADAM_B2 ** ADAM_STEP)
    delta = -ADAM_LR * (m_hat / (jnp.sqrt(v_hat) + ADAM_EPS) + ADAM_WD * w)
    return delta, m, v


def _sum_adamw(parts, w, m, v, tr, name):
    p, r, c = parts.shape
    tr = min(tr, r)
    assert r % tr == 0

    def body(p_ref, w_ref, m_ref, v_ref, g_ref, d_ref, nm_ref, nv_ref):
        g = p_ref[0].astype(F32)
        for j in range(1, p):
            g = g + p_ref[j].astype(F32)
        delta, nm, nv = _adamw_math(w_ref[...], g, m_ref[...], v_ref[...])
        g_ref[...] = g
        d_ref[...] = delta
        nm_ref[...] = nm
        nv_ref[...] = nv

    row = pl.BlockSpec((tr, c), lambda i: (i, 0))
    return pl.pallas_call(
        body, grid=(r // tr,), in_specs=[pl.BlockSpec((p, tr, c), lambda i: (0, i, 0)), row, row, row],
        out_specs=[row] * 4, out_shape=[jax.ShapeDtypeStruct((r, c), F32)] * 4,
        compiler_params=_cparams(("parallel",)), name=name)(parts, w, m, v)


def _sum_parts(parts, name):
    p, r, c = parts.shape

    def body(p_ref, o_ref):
        g = p_ref[0]
        for j in range(1, p):
            g = g + p_ref[j]
        o_ref[...] = g

    return pl.pallas_call(body, out_shape=jax.ShapeDtypeStruct((r, c), F32), name=name)(parts)


def _pack(arrs, mult=1024):
    flat = jnp.concatenate([a.reshape(-1).astype(F32) for a in arrs])
    n = flat.shape[0]
    npad = -n % mult
    return jnp.pad(flat, (0, npad)).reshape(-1, 128)


def _unpack(buf, shapes):
    flat = buf.reshape(-1)
    out, off = [], 0
    for s in shapes:
        n = 1
        for d in s:
            n *= d
        out.append(flat[off:off + n].reshape(s))
        off += n
    return out


def _block_diag(w):
    rows = [jnp.pad(w[i], ((0, 0), (i * C_BLOCK_DIM, C_WIDTH - (i + 1) * C_BLOCK_DIM))) for i in range(C_BLOCKS)]
    return jnp.concatenate(rows, axis=0)


def _diag_blocks(m):
    m4 = m.reshape(C_BLOCKS, C_BLOCK_DIM, C_BLOCKS, C_BLOCK_DIM)
    return jnp.stack([m4[i, :, i, :] for i in range(C_BLOCKS)])


def _gate_row(v):
    return jnp.pad(v.astype(F32), (B_HEADS, 128 - 2 * B_HEADS)).reshape(1, 128)


def _permute_w_in(w):
    pad = jnp.zeros(w.shape[:-1] + (D_IN_PAD - D_IN,), w.dtype)
    return jnp.concatenate([w[..., :3072], w[..., 3080:3592], w[..., 3072:3080], pad], axis=-1)


def _unpermute_w_in(w):
    return jnp.concatenate([w[..., :3072], w[..., COL_G:COL_G + 8], w[..., 3072:COL_G]], axis=-1)


_WEIGHTS = ['norm1_g', 'w_in', 'hgrn_lb_logits', 'hgrn_norm_g', 'gdn_conv_w', 'gdn_a_log', 'gdn_dt_bias', 'gdn_norm_g',
            'lru_conv_w', 'lru_conv_b', 'lru_w_a', 'lru_b_a', 'lru_w_x', 'lru_b_x', 'lru_lambda', 'w_out', 'norm2_g',
            'w_up', 'w_down', 'final_norm_g']
_BIG = ('w_in', 'w_out', 'w_up', 'w_down')
_SHARDED_SMALL = ('gdn_conv_w', 'lru_conv_w')


def _step(x, target, w, m, v):
    t = x.shape[0]
    mx, my, mc = _mesh_pos()
    me = 4 * mx + 2 * my + mc

    bf = lambda a: a.astype(BF16)

    def full_w_in(g):
        return _permute_w_in(jnp.moveaxis(g, 0, 1).reshape(D_MODEL, D_IN))

    conv_shapes = [w['gdn_conv_w'].shape, w['lru_conv_w'].shape]
    g_in, g_conv = _run_exchange(
        _Exchange([bf(w['w_in'][0]), _pack([w['gdn_conv_w'], w['lru_conv_w']])], ['slot', 'slot']), "gather_first")
    w_in = [full_w_in(g_in)]
    w_out, w_up, w_down = [], [], []
    gdn_cw, lru_cw = [], []
    for j in range(N_DEV):
        a, b = _unpack(g_conv[j], conv_shapes)
        gdn_cw.append(a)
        lru_cw.append(b)
    gdn_cw = jnp.concatenate(gdn_cw, axis=-1)
    lru_cw = jnp.concatenate(lru_cw, axis=-1)

    lbnd = _lb_fwd(w['hgrn_lb_logits'])
    row = lambda a: a.reshape(1, -1)

    def c_args(l):
        return (lru_cw[l], row(w['lru_conv_b'][l]), _block_diag(w['lru_w_a'][l]), row(w['lru_b_a'][l]),
                _block_diag(w['lru_w_x'][l]), row(w['lru_b_x'][l]), row(w['lru_lambda'][l]))

    saved = []
    xl = x
    h = _rms_fwd(x, w['norm1_g'][0], name="rms_fwd")
    for l in range(DEPTH):
        proj = _mm_rows(h, w_in[l], "nn", 256, "mm_proj")
        mix_a, st_a = _a_fwd(proj, lbnd[l], w['hgrn_norm_g'][l])
        alr, dtr = _gate_row(w['gdn_a_log'][l]), _gate_row(w['gdn_dt_bias'][l])
        qkv, gates = _b_pre_fwd(proj, gdn_cw[l], alr, dtr)
        nxt = [bf(w['w_in'][l + 1])] if l + 1 < DEPTH else []
        gather = _Exchange([bf(w['w_out'][l]), bf(w['w_up'][l]), bf(w['w_down'][l])] + nxt, ['rows', 'cols', 'rows'] + ['slot'] * len(nxt))
        (mix_b, st_b, *b_saved), got = _b_fwd(qkv, gates, proj, w['gdn_norm_g'][l], beside=gather)
        w_out.append(got[0])
        w_up.append(got[1])
        w_down.append(got[2])
        if nxt:
            w_in.append(full_w_in(got[3]))
        mix_c, hs = _c_fwd(proj, *c_args(l))
        mixed = [mix_a, mix_b, mix_c]
        x_mid, h2 = _mm_rows(mixed, w_out[l], "nn", 1024, "mm_out", residual=xl, epilogue="rms_fwd", norm=w['norm2_g'][l])
        act, up = _mm_rows(h2, w_up[l], "nn", 256, "mm_up", epilogue="relu2")
        saved.append(dict(x=xl, h=h, proj=proj, st_a=st_a, qkv=qkv, gates=gates, st_b=st_b, b_saved=b_saved, hs=hs, mixed=mixed,
                          x_mid=x_mid, h2=h2, up=up, act=act, alr=alr, dtr=dtr))
        if l + 1 < DEPTH:
            xl, h = _mm_rows(act, w_down[l], "nn", 512, "mm_down", residual=x_mid, epilogue="rms_fwd", norm=w['norm1_g'][l + 1])
        else:
            xl = _mm_rows(act, w_down[l], "nn", 512, "mm_down_last", residual=x_mid)
    loss, dx, dgf = _loss_head(xl, w['final_norm_g'], target)

    gs = {n: [None] * DEPTH for n in _WEIGHTS}
    recv = {n: [None] * DEPTH for n in _BIG}
    dw_in_above = None
    for l in reversed(range(DEPTH)):
        s = saved[l]
        dup = _mm_rows(dx, w_down[l].T, "nn", 256, "mm_dact", epilogue="drelu2", up=s['up'])
        dw_down = _mm_tn(s['act'], dx, 512, "mm_dw_down").reshape(N_DEV, D_FF // N_DEV, D_MODEL)
        dx_mid, dg2 = _mm_rows(dup, w_up[l].T, "nn", 512, "mm_dh2", epilogue="rms_bwd", norm=(s['x_mid'], w['norm2_g'][l], dx))
        dw_up = _mm_tn(s['h2'], dup, 512, "mm_dw_up", slab=D_FF // N_DEV)
        gs['norm2_g'][l] = dg2[0]
        dmixed = _mm_rows(dx_mid, w_out[l].T, "nn", 1024, "mm_dmixed")
        dw_out = _mm_tn(s['mixed'], dx_mid, 1024, "mm_dw_out").reshape(N_DEV, D_MODEL // N_DEV, D_MODEL)
        proj = s['proj']
        above = [dw_in_above] if dw_in_above is not None else []
        (dpa, dlb, dnga), got = _a_bwd(proj, lbnd[l], w['hgrn_norm_g'][l], s['st_a'], dmixed,
                                       beside=_Exchange([dw_out] + above, ['a2a'] * (1 + len(above))))
        recv['w_out'][l] = got[0]
        if above:
            recv['w_in'][l + 1] = got[1]
        gs['hgrn_lb_logits'][l] = dlb[0]
        gs['hgrn_norm_g'][l] = dnga[0]
        (dqkv, dgates, dz, dngb), got = _b_bwd(s['qkv'], s['gates'], proj, w['gdn_norm_g'][l], s['st_b'], s['b_saved'], dmixed,
                                               beside=_Exchange([dw_up, dw_down], ['a2a', 'a2a']))
        recv['w_up'][l], recv['w_down'][l] = got
        dxb, dgi, dcwb, dal, ddt = _b_pre_bwd(proj, gdn_cw[l], s['alr'], s['dtr'], dqkv, dgates)
        gs['gdn_norm_g'][l] = dngb[0]
        gs['gdn_conv_w'][l] = dcwb
        gs['gdn_a_log'][l] = dal[0, B_HEADS:2 * B_HEADS]
        gs['gdn_dt_bias'][l] = ddt[0, B_HEADS:2 * B_HEADS]
        dxc_in, dyg, dcwc, dcb, dwa, dba, dwx, dbx, dlam = _c_bwd(proj, *c_args(l), s['hs'], dmixed)
        gs['lru_conv_w'][l] = dcwc
        gs['lru_conv_b'][l] = dcb[0]
        gs['lru_w_a'][l] = _diag_blocks(dwa)
        gs['lru_b_a'][l] = dba[0]
        gs['lru_w_x'][l] = _diag_blocks(dwx)
        gs['lru_b_x'][l] = dbx[0]
        gs['lru_lambda'][l] = dlam[0]
        dproj = [dpa, dxb, dz, dxc_in, dyg, dgi]
        dx, dg1 = _mm_rows(dproj, w_in[l].T, "nn", 512, "mm_dh", epilogue="rms_bwd", norm=(s['x'], w['norm1_g'][l], dx_mid))
        dw_in = _unpermute_w_in(_mm_tn(s['h'], dproj, 512, "mm_dw_in"))
        dw_in_above = jnp.moveaxis(dw_in.reshape(D_MODEL, N_DEV, D_IN // N_DEV), 1, 0)
        gs['norm1_g'][l] = dg1[0]
    grad_x = dx
    part = {n: jnp.stack(gs[n]) for n in _WEIGHTS if n != 'final_norm_g' and n not in _BIG}
    part['final_norm_g'] = dgf[0]
    part['hgrn_lb_logits'] = _lb_bwd(w['hgrn_lb_logits'], part['hgrn_lb_logits'])

    small = [n for n in _WEIGHTS if n not in _BIG]
    packed = _pack([part[n] for n in small] + [loss])
    recv['w_in'][0], all_small = _run_exchange(_Exchange([dw_in_above, packed], ['a2a', 'slot']), "exchange_last")

    grads, deltas, new_m, new_v = {}, {}, {}, {}
    for n in _BIG:
        shp = w[n].shape
        r2 = lambda a: a.reshape(-1, shp[-1])
        parts = jnp.stack(recv[n], axis=1).reshape(N_DEV, -1, shp[-1])
        g, d, nm, nv = _sum_adamw(parts, r2(w[n]), r2(m[n]), r2(v[n]), 256, "adamw_" + n)
        grads[n], deltas[n], new_m[n], new_v[n] = (a.reshape(shp) for a in (g, d, nm, nv))

    total = _sum_parts(all_small, "sum_small")
    summed = _unpack(total, [part[n].shape for n in small] + [(1, 1)])
    loss_total = summed[-1].reshape(())
    gsmall = dict(zip(small, summed[:-1]))
    for n in _SHARDED_SMALL:
        width = w[n].shape[-1]
        gsmall[n] = lax.dynamic_slice_in_dim(gsmall[n], me * width, width, axis=2)
    pk = lambda d: _pack([d[n] for n in small])
    _, d, nm, nv = _sum_adamw(pk(gsmall)[None], pk(w), pk(m), pk(v), 4096, "adamw_small")
    shapes = [w[n].shape for n in small]
    for n, dd, mm, vv in zip(small, _unpack(d, shapes), _unpack(nm, shapes), _unpack(nv, shapes)):
        grads[n], deltas[n], new_m[n], new_v[n] = gsmall[n], dd, mm, vv
    return loss_total, grad_x, grads, deltas, new_m, new_v


def kernel(x, norm1_g, w_in, hgrn_lb_logits, hgrn_norm_g, gdn_conv_w, gdn_a_log, gdn_dt_bias, gdn_norm_g, lru_conv_w, lru_conv_b, lru_w_a, lru_b_a, lru_w_x, lru_b_x, lru_lambda, w_out, norm2_g, w_up, w_down, final_norm_g, loss_target, m_norm1_g, m_w_in, m_hgrn_lb_logits, m_hgrn_norm_g, m_gdn_conv_w, m_gdn_a_log, m_gdn_dt_bias, m_gdn_norm_g, m_lru_conv_w, m_lru_conv_b, m_lru_w_a, m_lru_b_a, m_lru_w_x, m_lru_b_x, m_lru_lambda, m_w_out, m_norm2_g, m_w_up, m_w_down, m_final_norm_g, v_norm1_g, v_w_in, v_hgrn_lb_logits, v_hgrn_norm_g, v_gdn_conv_w, v_gdn_a_log, v_gdn_dt_bias, v_gdn_norm_g, v_lru_conv_w, v_lru_conv_b, v_lru_w_a, v_lru_b_a, v_lru_w_x, v_lru_b_x, v_lru_lambda, v_w_out, v_norm2_g, v_w_up, v_w_down, v_final_norm_g):
    w = dict(zip(_WEIGHTS, (norm1_g, w_in, hgrn_lb_logits, hgrn_norm_g, gdn_conv_w, gdn_a_log, gdn_dt_bias, gdn_norm_g, lru_conv_w, lru_conv_b, lru_w_a, lru_b_a, lru_w_x, lru_b_x, lru_lambda, w_out, norm2_g, w_up, w_down, final_norm_g)))
    m = dict(zip(_WEIGHTS, (m_norm1_g, m_w_in, m_hgrn_lb_logits, m_hgrn_norm_g, m_gdn_conv_w, m_gdn_a_log, m_gdn_dt_bias, m_gdn_norm_g, m_lru_conv_w, m_lru_conv_b, m_lru_w_a, m_lru_b_a, m_lru_w_x, m_lru_b_x, m_lru_lambda, m_w_out, m_norm2_g, m_w_up, m_w_down, m_final_norm_g)))
    v = dict(zip(_WEIGHTS, (v_norm1_g, v_w_in, v_hgrn_lb_logits, v_hgrn_norm_g, v_gdn_conv_w, v_gdn_a_log, v_gdn_dt_bias, v_gdn_norm_g, v_lru_conv_w, v_lru_conv_b, v_lru_w_a, v_lru_b_a, v_lru_w_x, v_lru_b_x, v_lru_lambda, v_w_out, v_norm2_g, v_w_up, v_w_down, v_final_norm_g)))
    loss, grad_x, grads, deltas, new_m, new_v = _step(x.reshape(x.shape[1:]), loss_target.reshape(x.shape[1:]), w, m, v)
    return (loss, grad_x[None], *[grads[n] for n in _WEIGHTS], *[deltas[n] for n in _WEIGHTS],
            *[new_m[n] for n in _WEIGHTS], *[new_v[n] for n in _WEIGHTS])
```

```python
import jax
import jax.numpy as jnp
from jax import lax
from jax.experimental import pallas as pl
from jax.experimental.pallas import tpu as pltpu

F32 = jnp.float32
BF16 = jnp.bfloat16
MESH = pl.DeviceIdType.MESH

N_DEV = 8
D_MODEL = 1024
DEPTH = 4
A_HEADS, A_DIM, A_WIDTH = 4, 64, 256
B_HEADS, B_DIM, B_WIDTH = 4, 128, 512
C_WIDTH, C_BLOCKS, C_BLOCK_DIM = 256, 4, 64
D_IN = 3592
D_IN_PAD = 3840
COL_A, COL_B, COL_C, COL_G = 0, 1024, 3072, 3584
D_FF = 4096
CONV_K = 4
CHUNK = 64
SUB = 16
RG_C = 8.0
EPS = 1e-6
TINY = 1e-30
EXP_CLAMP = 80.0
GDN_SCALE = B_DIM ** -0.5
ADAM_LR, ADAM_B1, ADAM_B2, ADAM_EPS, ADAM_WD, ADAM_STEP = 0.001, 0.9, 0.999, 1e-08, 0.01, 10
VMEM_LIMIT = 56 * 1024 * 1024


def _cparams(sem=None):
    return pltpu.CompilerParams(dimension_semantics=sem, vmem_limit_bytes=VMEM_LIMIT)


_DIMS = {"nn": (((1,), (0,)), ((), ())), "nt": (((1,), (1,)), ((), ())), "tn": (((0,), (0,)), ((), ()))}


def _split_bf16(x):
    hi = x.astype(BF16)
    return hi, (x - hi.astype(F32)).astype(BF16)


def _dot(a, b, mode="nn", hi=False):
    if not hi:
        return lax.dot_general(a.astype(BF16), b.astype(BF16), _DIMS[mode], preferred_element_type=F32)
    ah, al = _split_bf16(a.astype(F32))
    bh, bl = _split_bf16(b.astype(F32))
    ka = 0 if mode == "tn" else 1
    kb = 1 if mode == "nt" else 0
    return lax.dot_general(jnp.concatenate([ah, ah, al], axis=ka), jnp.concatenate([bh, bl, bh], axis=kb),
                           _DIMS[mode], preferred_element_type=F32)


def _dot_exact_lhs(lhs, x, mode="nn"):
    l_bf16 = lhs.astype(BF16)
    x1 = x.astype(BF16)
    r1 = x - x1.astype(F32)
    x2 = r1.astype(BF16)
    x3 = (r1 - x2.astype(F32)).astype(BF16)
    ka = 0 if mode == "tn" else 1
    return lax.dot_general(jnp.concatenate([l_bf16] * 3, axis=ka), jnp.concatenate([x1, x2, x3], axis=0),
                           _DIMS[mode], preferred_element_type=F32)


def _iota2(n, m):
    return lax.broadcasted_iota(jnp.int32, (n, m), 0), lax.broadcasted_iota(jnp.int32, (n, m), 1)


def _tril(n, strict=False):
    r, c = _iota2(n, n)
    return (r > c) if strict else (r >= c)


def _sigmoid(x):
    return 1.0 / (1.0 + jnp.exp(-x))


def _softplus(x):
    return jnp.maximum(x, 0.0) + jnp.log(1.0 + jnp.exp(-jnp.abs(x)))


def _neg_expm1(z):
    series = -z * (1.0 + z * (0.5 + z * (1.0 / 6.0)))
    return jnp.where(z > -1e-2, series, 1.0 - jnp.exp(z))


def _gelu_tanh(x):
    c = 0.7978845608028654
    u = c * (x + 0.044715 * x * x * x)
    t = jnp.tanh(u)
    g = 0.5 * x * (1.0 + t)
    dg = 0.5 * (1.0 + t) + 0.5 * x * (1.0 - t * t) * c * (1.0 + 3.0 * 0.044715 * x * x)
    return g, dg


def _shift_rows(cur, halo, s, down=True):
    n = cur.shape[0]
    ridx = lax.broadcasted_iota(jnp.int32, (8, cur.shape[1]), 0)
    if down:
        main = pltpu.roll(cur, s, 0)
        fix = jnp.where(ridx < s, pltpu.roll(halo, s, 0), main[0:8])
        return jnp.concatenate([fix, main[8:]], axis=0)
    main = pltpu.roll(cur, n - s, 0)
    fix = jnp.where(ridx >= 8 - s, pltpu.roll(halo, 8 - s, 0), main[n - 8:n])
    return jnp.concatenate([main[:n - 8], fix], axis=0)


def _conv_bwd_rows(dy, nxt8, w):
    dx = dy * w[3:4]
    for j in range(3):
        dx = dx + _shift_rows(dy, nxt8, 3 - j, down=False) * w[j:j + 1]
    return dx


def _tile_scan(a, b, reverse=False):
    n = a.shape[0]
    r = lax.broadcasted_iota(jnp.int32, a.shape, 0) % 8
    for s in (1, 2, 4):
        keep = (r < 8 - s) if reverse else (r >= s)
        shift = n - s if reverse else s
        a_sh = jnp.where(keep, pltpu.roll(a, shift, 0), 1.0)
        b_sh = jnp.where(keep, pltpu.roll(b, shift, 0), 0.0)
        b = b + a * b_sh
        a = a * a_sh
    return a, b


def _conv_fwd(cur, prev8, w):
    y = cur * w[3:4]
    for j in range(3):
        y = y + _shift_rows(cur, prev8, 3 - j, down=True) * w[j:j + 1]
    return y


def _mm_rows(a, w, mode, tm, name, residual=None, epilogue=None, up=None, norm=None):
    parts = list(a) if isinstance(a, (list, tuple)) else [a]
    widths = [p.shape[1] for p in parts]
    t = parts[0].shape[0]
    n = w.shape[1] if mode == "nn" else w.shape[0]
    tm = min(tm, t)
    assert t % tm == 0 and all(wd % 128 == 0 for wd in widths)

    def body(*refs):
        a_refs, w_ref, rest = refs[:len(parts)], refs[len(parts)], refs[len(parts) + 1:]
        y, off = None, 0
        for a_ref, width in zip(a_refs, widths):
            wk = w_ref[off:off + width, :] if mode == "nn" else w_ref[:, off:off + width]
            d = _dot(a_ref[...], wk, mode)
            y = d if y is None else y + d
            off += width
        if residual is not None:
            y = y + rest[0][...]
        if epilogue == "relu2":
            r = jnp.maximum(y, 0.0)
            refs[-2][...] = (r * r).astype(BF16)
            refs[-1][...] = y.astype(BF16)
        elif epilogue == "drelu2":
            refs[-1][...] = (y * 2.0 * jnp.maximum(rest[0][...].astype(F32), 0.0)).astype(BF16)
        elif epilogue == "rms_fwd":
            rinv = lax.rsqrt(jnp.mean(y * y, axis=-1, keepdims=True) + EPS)
            refs[-2][...] = y
            refs[-1][...] = (y * rinv * refs[-3][...]).astype(BF16)
        elif epilogue == "rms_bwd":
            x_ref, g_ref, dres_ref, dx_ref, dg_ref = rest

            @pl.when(pl.program_id(0) == 0)
            def _():
                dg_ref[...] = jnp.zeros_like(dg_ref)

            xv = x_ref[...]
            rinv = lax.rsqrt(jnp.mean(xv * xv, axis=-1, keepdims=True) + EPS)
            xhat = xv * rinv
            dxh = y * g_ref[...]
            dx_ref[...] = dres_ref[...] + rinv * (dxh - xhat * jnp.mean(dxh * xhat, axis=-1, keepdims=True))
            dg_ref[...] += jnp.sum(y * xhat, axis=0, keepdims=True)
        else:
            refs[-1][...] = y

    rows = lambda width: pl.BlockSpec((tm, width), lambda i: (i, 0))
    vec = pl.BlockSpec((1, n), lambda i: (0, 0))
    ins, specs = parts + [w], [rows(wd) for wd in widths] + [pl.BlockSpec(w.shape, lambda i: (0, 0))]
    if residual is not None:
        ins.append(residual)
        specs.append(rows(n))
    if epilogue == "drelu2":
        ins.append(up)
        specs.append(rows(n))
    if epilogue == "rms_fwd":
        ins.append(norm.reshape(1, n))
        specs.append(vec)
        out_specs, out_shape = [rows(n), rows(n)], [jax.ShapeDtypeStruct((t, n), F32), jax.ShapeDtypeStruct((t, n), BF16)]
    elif epilogue == "rms_bwd":
        ins += [norm[0], norm[1].reshape(1, n), norm[2]]
        specs += [rows(n), vec, rows(n)]
        out_specs, out_shape = [rows(n), vec], [jax.ShapeDtypeStruct((t, n), F32), jax.ShapeDtypeStruct((1, n), F32)]
    elif epilogue == "relu2":
        out_specs, out_shape = [rows(n), rows(n)], [jax.ShapeDtypeStruct((t, n), BF16)] * 2
    else:
        out_specs, out_shape = rows(n), jax.ShapeDtypeStruct((t, n), BF16 if epilogue == "drelu2" else F32)
    return pl.pallas_call(body, grid=(t // tm,), in_specs=specs, out_specs=out_specs, out_shape=out_shape,
                          compiler_params=_cparams(("arbitrary" if epilogue == "rms_bwd" else "parallel",)), name=name)(*ins)


MM_TN_TILE = 1024


def _mm_tn(a, b, tk, name, slab=None):
    a_parts = list(a) if isinstance(a, (list, tuple)) else [a]
    b_parts = list(b) if isinstance(b, (list, tuple)) else [b]
    wa, wb = [p.shape[1] for p in a_parts], [p.shape[1] for p in b_parts]
    t, m, n = a_parts[0].shape[0], sum(wa), sum(wb)
    tk = min(tk, t)
    assert t % tk == 0 and all(x % 128 == 0 for x in wa + wb)
    nk = t // tk

    def body(*refs):
        a_refs, b_refs = refs[:len(wa)], refs[len(wa):len(wa) + len(wb)]
        o_ref, acc = refs[-2], refs[-1]
        kk = pl.program_id(0)

        @pl.when(kk == 0)
        def _():
            acc[...] = jnp.zeros_like(acc)

        ro = 0
        for a_ref, width_a in zip(a_refs, wa):
            for r0 in range(0, width_a, MM_TN_TILE):
                rw = min(MM_TN_TILE, width_a - r0)
                av = a_ref[:, r0:r0 + rw]
                co = 0
                for b_ref, width_b in zip(b_refs, wb):
                    for c0 in range(0, width_b, MM_TN_TILE):
                        cw = min(MM_TN_TILE, width_b - c0)
                        acc[ro + r0:ro + r0 + rw, co + c0:co + c0 + cw] += _dot(av, b_ref[:, c0:c0 + cw], "tn")
                    co += width_b
            ro += width_a

        @pl.when(kk == nk - 1)
        def _():
            if slab is None:
                o_ref[...] = acc[...].astype(BF16)
            else:
                for s in range(n // slab):
                    o_ref[s] = acc[:, s * slab:(s + 1) * slab].astype(BF16)

    if slab is None:
        out_spec, out_shape = pl.BlockSpec((m, n), lambda kk: (0, 0)), jax.ShapeDtypeStruct((m, n), BF16)
    else:
        out_spec, out_shape = pl.BlockSpec((n // slab, m, slab), lambda kk: (0, 0, 0)), jax.ShapeDtypeStruct((n // slab, m, slab), BF16)
    return pl.pallas_call(
        body, grid=(nk,),
        in_specs=[pl.BlockSpec((tk, x), lambda kk: (kk, 0)) for x in wa + wb],
        out_specs=out_spec, out_shape=out_shape, scratch_shapes=[pltpu.VMEM((m, n), F32)],
        compiler_params=_cparams(("arbitrary",)), name=name)(*a_parts, *b_parts)


def _rms_fwd(x, g, tb=512, name="rms_fwd"):
    t, d = x.shape

    def body(x_ref, g_ref, h_ref):
        xv = x_ref[...]
        rinv = lax.rsqrt(jnp.mean(xv * xv, axis=-1, keepdims=True) + EPS)
        h_ref[...] = (xv * rinv * g_ref[...]).astype(BF16)

    return pl.pallas_call(
        body, grid=(t // tb,), in_specs=[pl.BlockSpec((tb, d), lambda i: (i, 0)), pl.BlockSpec((1, d), lambda i: (0, 0))],
        out_specs=pl.BlockSpec((tb, d), lambda i: (i, 0)), out_shape=jax.ShapeDtypeStruct((t, d), BF16),
        compiler_params=_cparams(("parallel",)), name=name)(x, g.reshape(1, d))


def _loss_head(x, g, target, tb=512):
    t, d = x.shape

    def body(x_ref, g_ref, t_ref, loss_ref, dx_ref, dg_ref):
        @pl.when(pl.program_id(0) == 0)
        def _():
            dg_ref[...] = jnp.zeros_like(dg_ref)
            loss_ref[...] = jnp.zeros_like(loss_ref)

        xv = x_ref[...]
        rinv = lax.rsqrt(jnp.mean(xv * xv, axis=-1, keepdims=True) + EPS)
        xhat = xv * rinv
        err = xhat * g_ref[...] - t_ref[...]
        loss_ref[...] += 0.5 * jnp.sum(jnp.mean(err * err, axis=-1, keepdims=True), axis=0, keepdims=True)
        dy = err * (1.0 / d)
        dxh = dy * g_ref[...]
        dx_ref[...] = rinv * (dxh - xhat * jnp.mean(dxh * xhat, axis=-1, keepdims=True))
        dg_ref[...] += jnp.sum(dy * xhat, axis=0, keepdims=True)

    row = pl.BlockSpec((tb, d), lambda i: (i, 0))
    vec = pl.BlockSpec((1, d), lambda i: (0, 0))
    one = pl.BlockSpec((1, 1), lambda i: (0, 0))
    return pl.pallas_call(
        body, grid=(t // tb,), in_specs=[row, vec, row], out_specs=[one, row, vec],
        out_shape=[jax.ShapeDtypeStruct((1, 1), F32), jax.ShapeDtypeStruct((t, d), F32), jax.ShapeDtypeStruct((1, d), F32)],
        compiler_params=_cparams(("arbitrary",)), name="loss_head")(x, g.reshape(1, d), target)


def _lb_fwd(logits):
    def body(l_ref, o_ref):
        lg = l_ref[...]
        e = jnp.exp(lg - jnp.max(lg, axis=0, keepdims=True))
        p = e / jnp.sum(e, axis=0, keepdims=True)
        c = jnp.zeros_like(p[0:1])
        rows = [c]
        for l in range(1, DEPTH):
            c = c + p[l:l + 1]
            rows.append(c)
        o_ref[...] = jnp.minimum(jnp.maximum(jnp.concatenate(rows, axis=0), 0.0), 1.0 - EPS)

    return pl.pallas_call(body, out_shape=jax.ShapeDtypeStruct(logits.shape, F32), name="lb_fwd")(logits)


def _lb_bwd(logits, dlb):
    def body(l_ref, d_ref, o_ref):
        lg = l_ref[...]
        e = jnp.exp(lg - jnp.max(lg, axis=0, keepdims=True))
        p = e / jnp.sum(e, axis=0, keepdims=True)
        hi = 1.0 - EPS
        c = jnp.zeros_like(p[0:1])
        dc = []
        for l in range(1, DEPTH):
            c = c + p[l:l + 1]
            gl = jnp.where(c < 0.0, 0.0, jnp.where(c == 0.0, 0.5, 1.0)) * jnp.where(c > hi, 0.0, jnp.where(c == hi, 0.5, 1.0))
            dc.append(d_ref[l:l + 1, :] * gl)
        dp = [jnp.zeros_like(c)]
        for j in range(1, DEPTH):
            s = dc[j - 1]
            for l in range(j + 1, DEPTH):
                s = s + dc[l - 1]
            dp.append(s)
        dpm = jnp.concatenate(dp, axis=0)
        o_ref[...] = p * (dpm - jnp.sum(p * dpm, axis=0, keepdims=True))

    return pl.pallas_call(body, out_shape=jax.ShapeDtypeStruct(logits.shape, F32), name="lb_bwd")(logits, dlb)


def _a_gates(qi, fi, lbh):
    sq = _sigmoid(qi)
    q = qi * sq
    e = jnp.exp(-jnp.abs(fi))
    rec = 1.0 / (1.0 + e)
    pos = fi >= 0.0
    sg = jnp.where(pos, rec, e * rec)
    sgn = jnp.where(pos, e * rec, rec)
    f = lbh + (1.0 - lbh) * sg
    logf = jnp.log(jnp.maximum(f, TINY))
    k = (1.0 - lbh) * sgn
    return q, sq, sg, sgn, f, logf, k


def _a_intra(q, k, cum):
    subs, rows = [], []
    for i in range(CHUNK // SUB):
        lo = i * SUB
        r = cum[lo - 1:lo] if i > 0 else jnp.zeros_like(cum[0:1])
        eq = jnp.exp(cum[lo:lo + SUB] - r)
        ek = jnp.exp(jnp.minimum(r - cum, EXP_CLAMP))
        qt = q[lo:lo + SUB] * eq
        kt = k * ek
        rows.append(_dot(qt, kt, "nt", hi=True))
        subs.append((qt, eq, kt, ek))
    attn = jnp.where(_tril(CHUNK), jnp.concatenate(rows, axis=0), 0.0)
    return attn, subs


def _a_intra_bwd(dattn, subs):
    dq_rows, dk = [], None
    for i, (qt, eq, kt, ek) in enumerate(subs):
        da = dattn[i * SUB:(i + 1) * SUB]
        dq_rows.append(_dot(da, kt, "nn", hi=True) * eq)
        d = _dot(da, qt, "tn", hi=True) * ek
        dk = d if dk is None else dk + d
    return jnp.concatenate(dq_rows, axis=0), dk


def _headnorm_fwd(o, g, gate_in):
    rinv = lax.rsqrt(jnp.mean(o * o, axis=-1, keepdims=True) + EPS)
    sg = _sigmoid(gate_in)
    return o * rinv * g * (gate_in * sg)


def _headnorm_bwd(dout, o, g, gate_in):
    rinv = lax.rsqrt(jnp.mean(o * o, axis=-1, keepdims=True) + EPS)
    xhat = o * rinv
    sg = _sigmoid(gate_in)
    silu = gate_in * sg
    dy = dout * silu
    dgate = dout * xhat * g * (sg * (1.0 + gate_in * (1.0 - sg)))
    dxh = dy * g
    do = rinv * (dxh - xhat * jnp.mean(dxh * xhat, axis=-1, keepdims=True))
    return do, dgate, jnp.sum(dy * xhat, axis=0, keepdims=True)


def _a_fwd(proj, lb, norm_g, tb=256):
    t = proj.shape[0]
    nch = tb // CHUNK

    def body(q_ref, f_ref, i_ref, g_ref, lb_ref, ng_ref, out_ref, st_ref, s_scr):
        @pl.when(pl.program_id(0) == 0)
        def _():
            s_scr[...] = jnp.zeros_like(s_scr)

        ltri = _tril(CHUNK).astype(F32)

        def chunk(c, carry):
            rows = pl.ds(pl.multiple_of(c * CHUNK, CHUNK), CHUNK)
            hs = range(A_HEADS)
            cols = [slice(h * A_DIM, (h + 1) * A_DIM) for h in hs]
            gates = [_a_gates(q_ref[rows, cols[h]], f_ref[rows, cols[h]], lb_ref[:, cols[h]]) for h in hs]
            q, k = [gates[h][0] for h in hs], [gates[h][6] for h in hs]
            v = [i_ref[rows, cols[h]] for h in hs]
            cum = [_dot_exact_lhs(ltri, gates[h][5]) for h in hs]
            cl = [cum[h][CHUNK - 1:CHUNK] for h in hs]
            s0 = [s_scr[h] for h in hs]
            for h in hs:
                st_ref[c, h] = s0[h]
            attn = [_a_intra(q[h], k[h], cum[h])[0] for h in hs]
            qs0 = [_dot(q[h] * jnp.exp(cum[h]), s0[h]) for h in hs]
            o = [qs0[h] + _dot(attn[h], v[h]) for h in hs]
            kd = [k[h] * jnp.exp(cl[h] - cum[h]) for h in hs]
            for h in hs:
                s_scr[h] = s0[h] * jnp.exp(cl[h]).T + _dot(kd[h], v[h], "tn")
            outs = [_headnorm_fwd(o[h], ng_ref[...], g_ref[rows, cols[h]]) for h in hs]
            out_ref[rows, :] = jnp.concatenate(outs, axis=1).astype(BF16)
            return carry

        lax.fori_loop(0, nch, chunk, 0, unroll=2)

    colblk = lambda j: pl.BlockSpec((tb, A_WIDTH), lambda i, j=j: (i, j))
    return pl.pallas_call(
        body, grid=(t // tb,),
        in_specs=[colblk(0), colblk(1), colblk(2), colblk(3), pl.BlockSpec((1, A_WIDTH), lambda i: (0, 0)),
                  pl.BlockSpec((1, A_DIM), lambda i: (0, 0))],
        out_specs=[pl.BlockSpec((tb, A_WIDTH), lambda i: (i, 0)),
                   pl.BlockSpec((nch, A_HEADS, A_DIM, A_DIM), lambda i: (i, 0, 0, 0))],
        out_shape=[jax.ShapeDtypeStruct((t, A_WIDTH), BF16), jax.ShapeDtypeStruct((t // CHUNK, A_HEADS, A_DIM, A_DIM), F32)],
        scratch_shapes=[pltpu.VMEM((A_HEADS, A_DIM, A_DIM), F32)],
        compiler_params=_cparams(("arbitrary",)), name="hgrn_fwd")(proj, proj, proj, proj, lb.reshape(1, A_WIDTH), norm_g.reshape(1, A_DIM))


def _a_bwd(proj, lb, norm_g, states, dmixed, beside=None, tb=256):
    t = proj.shape[0]
    nch = tb // CHUNK
    nb = t // tb

    def body(q_ref, f_ref, i_ref, g_ref, lb_ref, ng_ref, st_ref, dm_ref, dp_ref, dlb_ref, dng_ref, ds_scr):
        @pl.when(pl.program_id(0) == 0)
        def _():
            ds_scr[...] = jnp.zeros_like(ds_scr)
            dlb_ref[...] = jnp.zeros_like(dlb_ref)
            dng_ref[...] = jnp.zeros_like(dng_ref)

        ltri = _tril(CHUNK).astype(F32)
        mask = _tril(CHUNK)

        def chunk(cc, carry):
            c = nch - 1 - cc
            rows = pl.ds(pl.multiple_of(c * CHUNK, CHUNK), CHUNK)
            hs = range(A_HEADS)
            cols = [slice(h * A_DIM, (h + 1) * A_DIM) for h in hs]
            qi = [q_ref[rows, cols[h]] for h in hs]
            gi = [g_ref[rows, cols[h]] for h in hs]
            lbh = [lb_ref[:, cols[h]] for h in hs]
            gates = [_a_gates(qi[h], f_ref[rows, cols[h]], lbh[h]) for h in hs]
            q, sq, sg, sgn, f, logf, k = ([gates[h][j] for h in hs] for j in range(7))
            v = [i_ref[rows, cols[h]] for h in hs]
            cum = [_dot_exact_lhs(ltri, logf[h]) for h in hs]
            cl = [cum[h][CHUNK - 1:CHUNK] for h in hs]
            ecum = [jnp.exp(cum[h]) for h in hs]
            ekd = [jnp.exp(cl[h] - cum[h]) for h in hs]
            cd = [jnp.exp(cl[h]) for h in hs]
            qd = [q[h] * ecum[h] for h in hs]
            kd = [k[h] * ekd[h] for h in hs]
            s0 = [st_ref[c, h] for h in hs]
            ds = [ds_scr[h] for h in hs]
            intra = [_a_intra(q[h], k[h], cum[h]) for h in hs]
            attn = [intra[h][0] for h in hs]
            qs0 = [_dot(qd[h], s0[h]) for h in hs]
            o = [qs0[h] + _dot(attn[h], v[h]) for h in hs]
            hn = [_headnorm_bwd(dm_ref[rows, cols[h]].astype(F32), o[h], ng_ref[...], gi[h]) for h in hs]
            do = [hn[h][0] for h in hs]
            dqd = [_dot(do[h], s0[h], "nt") for h in hs]
            dattn = [jnp.where(mask, _dot(do[h], v[h], "nt"), 0.0) for h in hs]
            dv = [_dot(attn[h], do[h], "tn") + _dot(kd[h], ds[h]) for h in hs]
            dkd = [_dot(v[h], ds[h], "nt") for h in hs]
            dcd = [jnp.sum((s0[h] * ds[h]).T, axis=0, keepdims=True) for h in hs]
            for h in hs:
                ds_scr[h] = _dot(qd[h], do[h], "tn") + ds[h] * cd[h].T
            ib = [_a_intra_bwd(dattn[h], intra[h][1]) for h in hs]
            dq = [dqd[h] * ecum[h] + ib[h][0] for h in hs]
            dk = [dkd[h] * ekd[h] + ib[h][1] for h in hs]
            dkk = [dkd[h] * kd[h] for h in hs]
            dcum = [dqd[h] * qd[h] - dkk[h] + q[h] * ib[h][0] - k[h] * ib[h][1] for h in hs]
            dcl = [jnp.sum(dkk[h], axis=0, keepdims=True) + dcd[h] * cd[h] for h in hs]
            dlogf = [_dot_exact_lhs(ltri, dcum[h], "tn") + dcl[h] for h in hs]
            dfv = [jnp.where(f[h] > TINY, dlogf[h] / f[h], 0.0) for h in hs]
            dfi = [dfv[h] * (1.0 - lbh[h]) * sg[h] * (1.0 - sg[h]) - dk[h] * (1.0 - lbh[h]) * sgn[h] * (1.0 - sgn[h]) for h in hs]
            dlbs = [jnp.sum(dfv[h] * (1.0 - sg[h]) - dk[h] * sgn[h], axis=0, keepdims=True) for h in hs]
            dqs = [dq[h] * (sq[h] * (1.0 + qi[h] * (1.0 - sq[h]))) for h in hs]
            dp_ref[rows, :] = jnp.concatenate(dqs + dfi + dv + [hn[h][1] for h in hs], axis=1).astype(BF16)
            dlb_ref[...] += jnp.concatenate(dlbs, axis=1)
            dng_ref[...] += sum(hn[h][2] for h in hs)
            return carry

        lax.fori_loop(0, nch, chunk, 0, unroll=2)

    colblk = lambda j: pl.BlockSpec((tb, A_WIDTH), lambda i, j=j: (nb - 1 - i, j))
    vec = lambda n: pl.BlockSpec((1, n), lambda i: (0, 0))
    return _call_beside(
        body, beside, nb, (proj, proj, proj, proj, lb.reshape(1, A_WIDTH), norm_g.reshape(1, A_DIM), states, dmixed), grid=(nb,),
        in_specs=[colblk(0), colblk(1), colblk(2), colblk(3), vec(A_WIDTH), vec(A_DIM),
                  pl.BlockSpec((nch, A_HEADS, A_DIM, A_DIM), lambda i: (nb - 1 - i, 0, 0, 0)), colblk(0)],
        out_specs=[pl.BlockSpec((tb, 4 * A_WIDTH), lambda i: (nb - 1 - i, 0)), vec(A_WIDTH), vec(A_DIM)],
        out_shape=[jax.ShapeDtypeStruct((t, 4 * A_WIDTH), BF16), jax.ShapeDtypeStruct((1, A_WIDTH), F32), jax.ShapeDtypeStruct((1, A_DIM), F32)],
        scratch_shapes=[pltpu.VMEM((A_HEADS, A_DIM, A_DIM), F32)], name="hgrn_bwd")


def _gate_lane_masks(shape):
    lane = lax.broadcasted_iota(jnp.int32, shape, 1)
    return lane < B_HEADS, (lane >= B_HEADS) & (lane < 2 * B_HEADS)


def _b_pre_fwd(proj, conv_w, alog_row, dtb_row, tb=512):
    t = proj.shape[0]
    cb0 = COL_B // B_WIDTH

    def body(q_ref, k_ref, v_ref, qp_ref, kp_ref, vp_ref, w_ref, gi_ref, al_ref, dt_ref, qkv_ref, gates_ref):
        first = pl.program_id(0) == 0
        for part, (c_ref, p_ref) in enumerate(((q_ref, qp_ref), (k_ref, kp_ref), (v_ref, vp_ref))):
            cols = slice(part * B_WIDTH, (part + 1) * B_WIDTH)
            prev = jnp.where(first, 0.0, p_ref[...])
            y = _conv_fwd(c_ref[...], prev, w_ref[:, cols])
            s = y * _sigmoid(y)
            if part < 2:
                outs = []
                for h in range(B_HEADS):
                    sh = s[:, h * B_DIM:(h + 1) * B_DIM]
                    outs.append(sh * lax.rsqrt(jnp.sum(sh * sh, axis=-1, keepdims=True) + EPS))
                s = jnp.concatenate(outs, axis=1)
            qkv_ref[:, cols] = s
        g = gi_ref[...]
        is_b, is_a = _gate_lane_masks(g.shape)
        la = -jnp.exp(al_ref[...]) * _softplus(g + dt_ref[...])
        gates_ref[...] = jnp.where(is_b, _sigmoid(g), jnp.where(is_a, la, 0.0))

    cur = lambda j: pl.BlockSpec((tb, B_WIDTH), lambda i, j=j: (i, cb0 + j))
    prv = lambda j: pl.BlockSpec((8, B_WIDTH), lambda i, j=j: (jnp.maximum(i * (tb // 8) - 1, 0), cb0 + j))
    vec = pl.BlockSpec((1, 128), lambda i: (0, 0))
    return pl.pallas_call(
        body, grid=(t // tb,),
        in_specs=[cur(0), cur(1), cur(2), prv(0), prv(1), prv(2), pl.BlockSpec((CONV_K, 3 * B_WIDTH), lambda i: (0, 0)),
                  pl.BlockSpec((tb, 128), lambda i: (i, COL_G // 128)), vec, vec],
        out_specs=[pl.BlockSpec((tb, 3 * B_WIDTH), lambda i: (i, 0)), pl.BlockSpec((tb, 128), lambda i: (i, 0))],
        out_shape=[jax.ShapeDtypeStruct((t, 3 * B_WIDTH), F32), jax.ShapeDtypeStruct((t, 128), F32)],
        compiler_params=_cparams(("parallel",)), name="gdn_pre_fwd")(proj, proj, proj, proj, proj, proj, conv_w, proj, alog_row, dtb_row)


def _inv_unit_lower(amats):
    r, c = _iota2(CHUNK, CHUNK)
    eye = jnp.where(r == c, 1.0, 0.0)
    ps = [eye - a for a in amats]
    aks = amats
    for _ in range(5):
        aks = [_dot(ak, ak, hi=True) for ak in aks]
        ps = [p + _dot(p, ak, hi=True) for p, ak in zip(ps, aks)]
    return ps


def _b_local(qs, ks, vs, betas, gcs, grows, gls, solve=True):
    hs = range(len(qs))
    causal, strict = _tril(CHUNK), _tril(CHUNK, strict=True)
    decay = [jnp.where(causal, jnp.exp(jnp.minimum(gcs[h] - grows[h], 0.0)), 0.0) for h in hs]
    kb = [ks[h] * betas[h] for h in hs]
    kk = [_dot(kb[h], ks[h], "nt") for h in hs]
    qkr = [_dot(qs[h], ks[h], "nt") for h in hs]
    eg = [jnp.exp(gcs[h]) for h in hs]
    bv = [vs[h] * betas[h] for h in hs]
    kg = [kb[h] * eg[h] for h in hs]
    qk = [qkr[h] * decay[h] for h in hs]
    qd = [qs[h] * eg[h] for h in hs]
    ekd = [jnp.exp(gls[h] - gcs[h]) for h in hs]
    kd = [ks[h] * ekd[h] for h in hs]
    cd = [jnp.exp(gls[h]) for h in hs]
    loc = dict(decay=decay, kb=kb, kk=kk, eg=eg, bv=bv, kg=kg, qkr=qkr, qk=qk, qd=qd, ekd=ekd, kd=kd, cd=cd)
    if solve:
        tinv = _inv_unit_lower([jnp.where(strict, kk[h] * decay[h], 0.0) for h in hs])
        loc.update(tinv=tinv, u=[_dot(tinv[h], bv[h], hi=True) for h in hs], w=[_dot(tinv[h], kg[h], hi=True) for h in hs])
    return loc


def _b_state(loc, ids, s0s):
    n = range(len(ids))
    ws = [_dot(loc["w"][ids[j]], s0s[j]) for j in n]
    qs0 = [_dot(loc["qd"][ids[j]], s0s[j]) for j in n]
    vn = [loc["u"][ids[j]] - ws[j] for j in n]
    o = [qs0[j] + _dot(loc["qk"][ids[j]], vn[j]) for j in n]
    s1 = [s0s[j] * loc["cd"][ids[j]] + _dot(loc["kd"][ids[j]], vn[j], "tn") for j in n]
    return vn, o, s1


def _b_fwd(qkv, gates, proj, norm_g, beside=None, tb=256):
    t = qkv.shape[0]
    nch = tb // CHUNK

    def body(q_ref, k_ref, v_ref, ga_ref, z_ref, ng_ref, out_ref, st_ref, ti_ref, w_ref, vn_ref, o_ref, s_scr):
        @pl.when(pl.program_id(0) == 0)
        def _():
            s_scr[...] = jnp.zeros_like(s_scr)

        ltri = _tril(CHUNK).astype(F32)

        hs = range(B_HEADS)
        cols = [slice(h * B_DIM, (h + 1) * B_DIM) for h in hs]

        def pair(p, carry):
            cs = [2 * p, 2 * p + 1]
            rows = [pl.ds(pl.multiple_of(c * CHUNK, CHUNK), CHUNK) for c in cs]
            ga = [ga_ref[r, :] for r in rows]
            gcum = [_dot_exact_lhs(ltri, g) for g in ga]
            gcum_t = [g.T for g in gcum]
            items = [(i, h) for i in range(2) for h in hs]
            loc = _b_local([q_ref[rows[i], cols[h]] * GDN_SCALE for i, h in items], [k_ref[rows[i], cols[h]] for i, h in items],
                           [v_ref[rows[i], cols[h]] for i, h in items], [ga[i][:, h:h + 1] for i, h in items],
                           [gcum[i][:, B_HEADS + h:B_HEADS + h + 1] for i, h in items],
                           [gcum_t[i][B_HEADS + h:B_HEADS + h + 1, :] for i, h in items],
                           [gcum[i][CHUNK - 1:CHUNK, B_HEADS + h:B_HEADS + h + 1] for i, h in items])
            s0s = [s_scr[h] for h in hs]
            for i in range(2):
                ids = [i * B_HEADS + h for h in hs]
                for h in hs:
                    st_ref[cs[i], h] = s0s[h]
                    ti_ref[cs[i], h] = loc["tinv"][ids[h]]
                vn, o, s0s = _b_state(loc, ids, s0s)
                w_ref[rows[i], :] = jnp.concatenate([loc["w"][j] for j in ids], axis=1)
                vn_ref[rows[i], :] = jnp.concatenate(vn, axis=1)
                o_ref[rows[i], :] = jnp.concatenate(o, axis=1)
                outs = [_headnorm_fwd(o[h], ng_ref[...], z_ref[rows[i], cols[h]]) for h in hs]
                out_ref[rows[i], :] = jnp.concatenate(outs, axis=1).astype(BF16)
            for h in hs:
                s_scr[h] = s0s[h]
            return carry

        lax.fori_loop(0, nch // 2, pair, 0)

    part = lambda j: pl.BlockSpec((tb, B_WIDTH), lambda i, j=j: (i, j))
    wide = pl.BlockSpec((tb, B_WIDTH), lambda i: (i, 0))
    wide_shape = jax.ShapeDtypeStruct((t, B_WIDTH), F32)
    return _call_beside(
        body, beside, t // tb, (qkv, qkv, qkv, gates, proj, norm_g.reshape(1, B_DIM)), grid=(t // tb,),
        in_specs=[part(0), part(1), part(2), pl.BlockSpec((tb, 128), lambda i: (i, 0)),
                  pl.BlockSpec((tb, B_WIDTH), lambda i: (i, COL_B // B_WIDTH + 3)), pl.BlockSpec((1, B_DIM), lambda i: (0, 0))],
        out_specs=[wide, pl.BlockSpec((nch, B_HEADS, B_DIM, B_DIM), lambda i: (i, 0, 0, 0)),
                   pl.BlockSpec((nch, B_HEADS, CHUNK, CHUNK), lambda i: (i, 0, 0, 0)), wide, wide, wide],
        out_shape=[jax.ShapeDtypeStruct((t, B_WIDTH), BF16), jax.ShapeDtypeStruct((t // CHUNK, B_HEADS, B_DIM, B_DIM), F32),
                   jax.ShapeDtypeStruct((t // CHUNK, B_HEADS, CHUNK, CHUNK), F32), wide_shape, wide_shape, wide_shape],
        scratch_shapes=[pltpu.VMEM((B_HEADS, B_DIM, B_DIM), F32)], name="gdn_fwd")


def _b_bwd(qkv, gates, proj, norm_g, states, fwd_saved, dmixed, beside=None, tb=256):
    t = qkv.shape[0]
    nch = tb // CHUNK
    nb = t // tb

    def body(q_ref, k_ref, v_ref, ga_ref, z_ref, ng_ref, st_ref, ti_ref, w_ref, vn_ref, o_ref, dm0_ref, dm1_ref,
             dqkv_ref, dga_ref, dz_ref, dng_ref, ds_scr):
        @pl.when(pl.program_id(0) == 0)
        def _():
            ds_scr[...] = jnp.zeros_like(ds_scr)
            dng_ref[...] = jnp.zeros_like(dng_ref)

        ltri = _tril(CHUNK).astype(F32)
        strict = _tril(CHUNK, strict=True)
        lane = lax.broadcasted_iota(jnp.int32, (CHUNK, 128), 1)
        lane1 = lax.broadcasted_iota(jnp.int32, (1, 128), 1)

        nh = range(B_HEADS)
        cols = [slice(h * B_DIM, (h + 1) * B_DIM) for h in nh]
        rsum = lambda a: jnp.sum(a, axis=-1, keepdims=True)

        def pair(p, carry):
            cs = [nch - 1 - 2 * p, nch - 2 - 2 * p]
            crow = [pl.ds(pl.multiple_of(c * CHUNK, CHUNK), CHUNK) for c in cs]
            gas = [ga_ref[r, :] for r in crow]
            gcum = [_dot_exact_lhs(ltri, g) for g in gas]
            gcum_t = [g.T for g in gcum]
            items = [(i, h) for i in range(2) for h in nh]
            hs = range(len(items))
            q = [q_ref[crow[i], cols[h]] * GDN_SCALE for i, h in items]
            k = [k_ref[crow[i], cols[h]] for i, h in items]
            v = [v_ref[crow[i], cols[h]] for i, h in items]
            z = [z_ref[crow[i], cols[h]] for i, h in items]
            beta = [gas[i][:, h:h + 1] for i, h in items]
            s0 = [st_ref[cs[i], h] for i, h in items]
            r = _b_local(q, k, v, beta, [gcum[i][:, B_HEADS + h:B_HEADS + h + 1] for i, h in items],
                         [gcum_t[i][B_HEADS + h:B_HEADS + h + 1, :] for i, h in items],
                         [gcum[i][CHUNK - 1:CHUNK, B_HEADS + h:B_HEADS + h + 1] for i, h in items], solve=False)
            tinv = [ti_ref[cs[i], h] for i, h in items]
            w = [w_ref[crow[i], cols[h]] for i, h in items]
            vn = [vn_ref[crow[i], cols[h]] for i, h in items]
            decay, eg, qd, kd, kb, cd = (r[n] for n in ("decay", "eg", "qd", "kd", "kb", "cd"))
            dms = [(dm0_ref if h < 2 else dm1_ref)[crow[i], (h % 2) * B_DIM:(h % 2 + 1) * B_DIM].astype(F32) for i, h in items]
            hn = [_headnorm_bwd(dms[j], o_ref[crow[i], cols[h]], ng_ref[...], z[j]) for j, (i, h) in enumerate(items)]
            do = [hn[j][0] for j in hs]
            dvn_o = [_dot(r["qk"][j], do[j], "tn") for j in hs]
            dqk = [_dot(do[j], vn[j], "nt") for j in hs]
            dqd = [_dot(do[j], s0[j], "nt") for j in hs]
            ds_o = [_dot(qd[j], do[j], "tn") for j in hs]
            ds = [ds_scr[h] for h in nh]
            dvn, dkd, dcd = [None] * 8, [None] * 8, [None] * 8
            for i in range(2):
                for h in nh:
                    j = i * B_HEADS + h
                    dvn[j] = dvn_o[j] + _dot(kd[j], ds[h])
                    dkd[j] = _dot(vn[j], ds[h], "nt")
                    dcd[j] = jnp.sum(jnp.sum(s0[j] * ds[h], axis=0, keepdims=True), axis=1, keepdims=True)
                ds = [ds_o[i * B_HEADS + h] + ds[h] * cd[i * B_HEADS + h] - _dot(w[i * B_HEADS + h], dvn[i * B_HEADS + h], "tn")
                      for h in nh]
            for h in nh:
                ds_scr[h] = ds[h]
            dw = [-_dot(dvn[j], s0[j], "nt") for j in hs]
            dbv = [_dot(tinv[h], dvn[h], "tn", hi=True) for h in hs]
            dkg = [_dot(tinv[h], dw[h], "tn", hi=True) for h in hs]
            dt = [_dot(dvn[h], r["bv"][h], "nt", hi=True) + _dot(dw[h], r["kg"][h], "nt", hi=True) for h in hs]
            tdt = [_dot(tinv[h], dt[h], "tn", hi=True) for h in hs]
            da = [jnp.where(strict, -_dot(tdt[h], tinv[h], "nt", hi=True), 0.0) for h in hs]
            dm = [da[h] * decay[h] for h in hs]
            dn = [dqk[h] * decay[h] for h in hs]
            e = [(da[h] * r["kk"][h] + dqk[h] * r["qkr"][h]) * decay[h] for h in hs]
            dkb = [_dot(dm[h], k[h]) + dkg[h] * eg[h] for h in hs]
            dk = [_dot(dm[h], kb[h], "tn") + _dot(dn[h], q[h], "tn") + dkd[h] * r["ekd"][h] + dkb[h] * beta[h] for h in hs]
            dq = [_dot(dn[h], k[h]) + dqd[h] * eg[h] for h in hs]
            tkd = [rsum(dkd[h] * kd[h]) for h in hs]
            dgc = [rsum(e[h]) - rsum(e[h].T) + rsum(dqd[h] * qd[h]) - tkd[h] + rsum(dkg[h] * r["kg"][h]) for h in hs]
            dgl = [jnp.sum(tkd[h], axis=0, keepdims=True) + dcd[h] * cd[h] for h in hs]
            dbeta = [rsum(dbv[h] * v[h]) + rsum(dkb[h] * k[h]) for h in hs]
            for i in range(2):
                ids = [i * B_HEADS + h for h in nh]
                dbeta_m = sum(jnp.where(lane == h, dbeta[ids[h]], 0.0) for h in nh)
                dgc_m = sum(jnp.where(lane == B_HEADS + h, dgc[ids[h]], 0.0) for h in nh)
                dgl_m = sum(jnp.where(lane1 == B_HEADS + h, dgl[ids[h]], 0.0) for h in nh)
                dqkv_ref[crow[i], :] = jnp.concatenate(
                    [dq[j] * GDN_SCALE for j in ids] + [dk[j] for j in ids] + [dbv[j] * beta[j] for j in ids], axis=1)
                dz_ref[crow[i], :] = jnp.concatenate([hn[j][1] for j in ids], axis=1).astype(BF16)
                dga_ref[crow[i], :] = dbeta_m + _dot_exact_lhs(ltri, dgc_m, "tn") + dgl_m
            dng_ref[...] += sum(hn[j][2] for j in hs)
            return carry

        lax.fori_loop(0, nch // 2, pair, 0)

    part = lambda j: pl.BlockSpec((tb, B_WIDTH), lambda i, j=j: (nb - 1 - i, j))
    rowblk = lambda w, j=0: pl.BlockSpec((tb, w), lambda i, j=j: (nb - 1 - i, j))
    return _call_beside(
        body, beside, nb, (qkv, qkv, qkv, gates, proj, norm_g.reshape(1, B_DIM), states, *fwd_saved, dmixed, dmixed), grid=(nb,),
        in_specs=[part(0), part(1), part(2), rowblk(128), rowblk(B_WIDTH, COL_B // B_WIDTH + 3),
                  pl.BlockSpec((1, B_DIM), lambda i: (0, 0)),
                  pl.BlockSpec((nch, B_HEADS, B_DIM, B_DIM), lambda i: (nb - 1 - i, 0, 0, 0)),
                  pl.BlockSpec((nch, B_HEADS, CHUNK, CHUNK), lambda i: (nb - 1 - i, 0, 0, 0)),
                  rowblk(B_WIDTH), rowblk(B_WIDTH), rowblk(B_WIDTH), rowblk(256, 1), rowblk(256, 2)],
        out_specs=[rowblk(3 * B_WIDTH), rowblk(128), rowblk(B_WIDTH), pl.BlockSpec((1, B_DIM), lambda i: (0, 0))],
        out_shape=[jax.ShapeDtypeStruct((t, 3 * B_WIDTH), F32), jax.ShapeDtypeStruct((t, 128), F32),
                   jax.ShapeDtypeStruct((t, B_WIDTH), BF16), jax.ShapeDtypeStruct((1, B_DIM), F32)],
        scratch_shapes=[pltpu.VMEM((B_HEADS, B_DIM, B_DIM), F32)], name="gdn_bwd")


def _b_pre_bwd(proj, conv_w, alog_row, dtb_row, dqkv, dgates, tb=512):
    t = proj.shape[0]
    nb = t // tb
    cb0 = COL_B // B_WIDTH

    def body(q_ref, k_ref, v_ref, qp_ref, kp_ref, vp_ref, w_ref, gi_ref, al_ref, dt_ref, dqkv_ref, dga_ref,
             dy_ref, dgi_ref, dw_ref, dal_ref, ddt_ref, nxt_scr):
        step_id = pl.program_id(0)
        first = step_id == nb - 1

        @pl.when(step_id == 0)
        def _():
            dw_ref[...] = jnp.zeros_like(dw_ref)
            dal_ref[...] = jnp.zeros_like(dal_ref)
            ddt_ref[...] = jnp.zeros_like(ddt_ref)

        for part, (c_ref, p_ref) in enumerate(((q_ref, qp_ref), (k_ref, kp_ref), (v_ref, vp_ref))):
            cols = slice(part * B_WIDTH, (part + 1) * B_WIDTH)
            cur = c_ref[...]
            prev = jnp.where(first, 0.0, p_ref[...])
            w = w_ref[:, cols]
            shifted = [_shift_rows(cur, prev, 3 - j, down=True) for j in range(3)] + [cur]
            y = shifted[0] * w[0:1] + shifted[1] * w[1:2] + shifted[2] * w[2:3] + shifted[3] * w[3:4]
            sg = _sigmoid(y)
            s = y * sg
            dsn = dqkv_ref[:, cols]
            if part < 2:
                outs = []
                for h in range(B_HEADS):
                    hc = slice(h * B_DIM, (h + 1) * B_DIM)
                    sh, dh = s[:, hc], dsn[:, hc]
                    rq = lax.rsqrt(jnp.sum(sh * sh, axis=-1, keepdims=True) + EPS)
                    nh = sh * rq
                    outs.append(rq * (dh - nh * jnp.sum(dh * nh, axis=-1, keepdims=True)))
                dsn = jnp.concatenate(outs, axis=1)
            dy = dsn * (sg * (1.0 + y * (1.0 - sg)))
            dy_ref[:, cols] = _conv_bwd_rows(dy, jnp.where(step_id == 0, 0.0, nxt_scr[:, cols]), w).astype(BF16)
            nxt_scr[:, cols] = dy[0:8]
            dw_ref[:, cols] += jnp.concatenate([jnp.sum(shifted[j] * dy, axis=0, keepdims=True) for j in range(CONV_K)], axis=0)
        g = gi_ref[...]
        dga = dga_ref[...]
        is_b, is_a = _gate_lane_masks(g.shape)
        beta = _sigmoid(g)
        pre = g + dt_ref[...]
        ea = jnp.exp(al_ref[...])
        la = -ea * _softplus(pre)
        dpre = jnp.where(is_a, dga * (-ea) * _sigmoid(pre), 0.0)
        dgi_ref[...] = jnp.where(is_b, dga * beta * (1.0 - beta), dpre).astype(BF16)
        dal_ref[...] += jnp.sum(jnp.where(is_a, dga * la, 0.0), axis=0, keepdims=True)
        ddt_ref[...] += jnp.sum(dpre, axis=0, keepdims=True)

    cur = lambda j: pl.BlockSpec((tb, B_WIDTH), lambda i, j=j: (nb - 1 - i, cb0 + j))
    prv = lambda j: pl.BlockSpec((8, B_WIDTH), lambda i, j=j: (jnp.maximum((nb - 1 - i) * (tb // 8) - 1, 0), cb0 + j))
    vec = pl.BlockSpec((1, 128), lambda i: (0, 0))
    wspec = pl.BlockSpec((CONV_K, 3 * B_WIDTH), lambda i: (0, 0))
    rowblk = lambda width, j=0: pl.BlockSpec((tb, width), lambda i, j=j: (nb - 1 - i, j))
    return pl.pallas_call(
        body, grid=(nb,),
        in_specs=[cur(0), cur(1), cur(2), prv(0), prv(1), prv(2), wspec, rowblk(128, COL_G // 128), vec, vec,
                  rowblk(3 * B_WIDTH), rowblk(128)],
        out_specs=[rowblk(3 * B_WIDTH), rowblk(128), wspec, vec, vec],
        out_shape=[jax.ShapeDtypeStruct((t, 3 * B_WIDTH), BF16), jax.ShapeDtypeStruct((t, 128), BF16),
                   jax.ShapeDtypeStruct((CONV_K, 3 * B_WIDTH), F32), jax.ShapeDtypeStruct((1, 128), F32), jax.ShapeDtypeStruct((1, 128), F32)],
        scratch_shapes=[pltpu.VMEM((8, 3 * B_WIDTH), F32)],
        compiler_params=_cparams(("arbitrary",)), name="gdn_pre_bwd")(
            proj, proj, proj, proj, proj, proj, conv_w, proj, alog_row, dtb_row, dqkv, dgates)


def _c_gates(xc, wa_ref, ba_ref, wx_ref, bx_ref, lam_ref, is_row0):
    r = _sigmoid(_dot(xc, wa_ref[...]) + ba_ref[...])
    i = _sigmoid(_dot(xc, wx_ref[...]) + bx_ref[...])
    sp = _softplus(-lam_ref[...])
    log_a = -RG_C * r * sp
    a = jnp.exp(log_a)
    m2 = _neg_expm1(2.0 * log_a)
    mult = jnp.where(is_row0, 1.0, jnp.sqrt(jnp.maximum(m2, EPS)))
    return r, i, sp, log_a, a, m2, mult


def _row0_mask(tb, first):
    ridx = lax.broadcasted_iota(jnp.int32, (tb, C_WIDTH), 0)
    return (ridx == 0) & first


def _c_fwd(proj, conv_w, conv_b, wa, ba, wx, bx, lam, tb=512):
    t = proj.shape[0]
    cbx = COL_C // C_WIDTH

    def body(x_ref, xp_ref, y_ref, w_ref, cb_ref, wa_ref, ba_ref, wx_ref, bx_ref, lam_ref, out_ref, h_ref, a_scr, b_scr, h_scr):
        first = pl.program_id(0) == 0

        @pl.when(first)
        def _():
            h_scr[...] = jnp.zeros_like(h_scr)

        prev = jnp.where(first, 0.0, xp_ref[...])
        xc = _conv_fwd(x_ref[...], prev, w_ref[...]) + cb_ref[...]
        _, i, _, _, a, _, mult = _c_gates(xc, wa_ref, ba_ref, wx_ref, bx_ref, lam_ref, _row0_mask(tb, first))
        ta, tb_ = _tile_scan(a, mult * i * xc)
        a_scr[...] = ta
        b_scr[...] = tb_

        def step(blk, h):
            rows = pl.ds(pl.multiple_of(blk * 8, 8), 8)
            h_ref[rows, :] = jnp.broadcast_to(h, (8, C_WIDTH))
            return a_scr[rows, :][7:8] * h + b_scr[rows, :][7:8]

        h_scr[...] = lax.fori_loop(0, tb // 8, step, h_scr[...], unroll=8)
        hs = ta * h_ref[...] + tb_
        h_ref[...] = hs
        gl, _ = _gelu_tanh(y_ref[...])
        out_ref[...] = (gl * hs).astype(BF16)

    vec = pl.BlockSpec((1, C_WIDTH), lambda i: (0, 0))
    mat = pl.BlockSpec((C_WIDTH, C_WIDTH), lambda i: (0, 0))
    row = pl.BlockSpec((tb, C_WIDTH), lambda i: (i, 0))
    return pl.pallas_call(
        body, grid=(t // tb,),
        in_specs=[pl.BlockSpec((tb, C_WIDTH), lambda i: (i, cbx)),
                  pl.BlockSpec((8, C_WIDTH), lambda i: (jnp.maximum(i * (tb // 8) - 1, 0), cbx)),
                  pl.BlockSpec((tb, C_WIDTH), lambda i: (i, cbx + 1)),
                  pl.BlockSpec((CONV_K, C_WIDTH), lambda i: (0, 0)), vec, mat, vec, mat, vec, vec],
        out_specs=[row, row],
        out_shape=[jax.ShapeDtypeStruct((t, C_WIDTH), BF16), jax.ShapeDtypeStruct((t, C_WIDTH), F32)],
        scratch_shapes=[pltpu.VMEM((tb, C_WIDTH), F32), pltpu.VMEM((tb, C_WIDTH), F32), pltpu.VMEM((1, C_WIDTH), F32)],
        compiler_params=_cparams(("arbitrary",)), name="lru_fwd")(proj, proj, proj, conv_w, conv_b, wa, ba, wx, bx, lam)


def _c_bwd(proj, conv_w, conv_b, wa, ba, wx, bx, lam, hs, dmixed, tb=512):
    t = proj.shape[0]
    nb = t // tb
    cbx = COL_C // C_WIDTH

    def body(x_ref, xp_ref, y_ref, w_ref, cb_ref, wa_ref, ba_ref, wx_ref, bx_ref, lam_ref, h_ref, hp_ref, dm_ref,
             dxc_ref, dyg_ref, dw_ref, dcb_ref, dwa_ref, dba_ref, dwx_ref, dbx_ref, dlam_ref, g_scr, a_scr, cin_scr, c_scr, nxt_scr):
        step_id = pl.program_id(0)
        first = step_id == nb - 1

        @pl.when(step_id == 0)
        def _():
            c_scr[...] = jnp.zeros_like(c_scr)
            for ref in (dw_ref, dcb_ref, dwa_ref, dba_ref, dwx_ref, dbx_ref, dlam_ref):
                ref[...] = jnp.zeros_like(ref)

        cur = x_ref[...]
        prev = jnp.where(first, 0.0, xp_ref[...])
        w = w_ref[...]
        shifted = [_shift_rows(cur, prev, 3 - j, down=True) for j in range(3)] + [cur]
        xc = shifted[0] * w[0:1] + shifted[1] * w[1:2] + shifted[2] * w[2:3] + shifted[3] * w[3:4] + cb_ref[...]
        row0 = _row0_mask(tb, first)
        r, i, sp, log_a, a, m2, mult = _c_gates(xc, wa_ref, ba_ref, wx_ref, bx_ref, lam_ref, row0)
        h = h_ref[...]
        hprev = _shift_rows(h, jnp.where(first, 0.0, hp_ref[...]), 1, down=True)
        gl, dgl = _gelu_tanh(y_ref[...])
        dm = dm_ref[...].astype(F32)
        dyg_ref[...] = (dm * h * dgl).astype(BF16)
        dout = dm * gl
        ta, te = _tile_scan(a, a * dout, reverse=True)
        a_scr[...] = ta
        g_scr[...] = te

        def step(blk, carry):
            rows = pl.ds(pl.multiple_of((tb // 8 - 1 - blk) * 8, 8), 8)
            cin_scr[rows, :] = jnp.broadcast_to(carry, (8, C_WIDTH))
            return a_scr[rows, :][0:1] * carry + g_scr[rows, :][0:1]

        c_scr[...] = lax.fori_loop(0, tb // 8, step, c_scr[...], unroll=8)
        cin = cin_scr[...]
        cout = ta * cin + te
        last_in_tile = lax.broadcasted_iota(jnp.int32, (tb, C_WIDTH), 0) % 8 == 7
        dbx = dout + jnp.where(last_in_tile, cin, pltpu.roll(cout, tb - 1, 0))
        da = dbx * hprev
        dmult = jnp.where(row0, 0.0, dbx * i * xc)
        di = dbx * mult * xc
        dxc = dbx * mult * i
        dm2 = jnp.where(m2 > EPS, dmult * 0.5 / mult, 0.0)
        dlog_a = da * a - 2.0 * a * a * dm2
        dr = dlog_a * (-RG_C) * sp
        dlam_ref[...] += jnp.sum(dlog_a * (-RG_C) * r, axis=0, keepdims=True) * (-_sigmoid(-lam_ref[...]))
        dpa = dr * r * (1.0 - r)
        dpx = di * i * (1.0 - i)
        dba_ref[...] += jnp.sum(dpa, axis=0, keepdims=True)
        dbx_ref[...] += jnp.sum(dpx, axis=0, keepdims=True)
        dwa_ref[...] += _dot(xc, dpa, "tn")
        dwx_ref[...] += _dot(xc, dpx, "tn")
        dxc = dxc + _dot(dpa, wa_ref[...], "nt") + _dot(dpx, wx_ref[...], "nt")
        dxc_ref[...] = _conv_bwd_rows(dxc, jnp.where(step_id == 0, 0.0, nxt_scr[...]), w).astype(BF16)
        nxt_scr[...] = dxc[0:8]
        dcb_ref[...] += jnp.sum(dxc, axis=0, keepdims=True)
        dw_ref[...] += jnp.concatenate([jnp.sum(shifted[j] * dxc, axis=0, keepdims=True) for j in range(CONV_K)], axis=0)

    vec = pl.BlockSpec((1, C_WIDTH), lambda i: (0, 0))
    mat = pl.BlockSpec((C_WIDTH, C_WIDTH), lambda i: (0, 0))
    cw = pl.BlockSpec((CONV_K, C_WIDTH), lambda i: (0, 0))
    row = lambda j=0: pl.BlockSpec((tb, C_WIDTH), lambda i, j=j: (nb - 1 - i, j))
    halo = lambda j=0: pl.BlockSpec((8, C_WIDTH), lambda i, j=j: (jnp.maximum((nb - 1 - i) * (tb // 8) - 1, 0), j))
    return pl.pallas_call(
        body, grid=(nb,),
        in_specs=[row(cbx), halo(cbx), row(cbx + 1), cw, vec, mat, vec, mat, vec, vec, row(), halo(), row(3)],
        out_specs=[row(), row(), cw, vec, mat, vec, mat, vec, vec],
        out_shape=[jax.ShapeDtypeStruct((t, C_WIDTH), BF16), jax.ShapeDtypeStruct((t, C_WIDTH), BF16),
                   jax.ShapeDtypeStruct((CONV_K, C_WIDTH), F32), jax.ShapeDtypeStruct((1, C_WIDTH), F32),
                   jax.ShapeDtypeStruct((C_WIDTH, C_WIDTH), F32), jax.ShapeDtypeStruct((1, C_WIDTH), F32),
                   jax.ShapeDtypeStruct((C_WIDTH, C_WIDTH), F32), jax.ShapeDtypeStruct((1, C_WIDTH), F32),
                   jax.ShapeDtypeStruct((1, C_WIDTH), F32)],
        scratch_shapes=[pltpu.VMEM((tb, C_WIDTH), F32), pltpu.VMEM((tb, C_WIDTH), F32), pltpu.VMEM((tb, C_WIDTH), F32),
                        pltpu.VMEM((1, C_WIDTH), F32), pltpu.VMEM((8, C_WIDTH), F32)],
        compiler_params=_cparams(("arbitrary",)), name="lru_bwd")(
            proj, proj, proj, conv_w, conv_b, wa, ba, wx, bx, lam, hs, hs, dmixed)


def _mesh_pos():
    return lax.axis_index("x"), lax.axis_index("y"), lax.axis_index("c")


class _Exchange:
    def __init__(self, arrays, layouts):
        self.arrays, self.layouts = list(arrays), list(layouts)
        self.out_shapes = []
        for a, lay in zip(self.arrays, self.layouts):
            if lay == 'a2a':
                shp = a.shape
            elif lay == 'slot':
                shp = (N_DEV,) + a.shape
            elif lay == 'rows':
                shp = (N_DEV * a.shape[0], a.shape[1])
            else:
                shp = (a.shape[0], N_DEV * a.shape[1])
            self.out_shapes.append(jax.ShapeDtypeStruct(shp, a.dtype))
        n = len(self.arrays)
        self.scratch = [pltpu.SemaphoreType.DMA((7 * n,)), pltpu.SemaphoreType.DMA((7 * n,)), pltpu.SemaphoreType.DMA((n,))]

    def _landing(self, a, dst_ref, idx):
        lay, shape = self.layouts[a], self.arrays[a].shape
        if lay in ('a2a', 'slot'):
            return dst_ref.at[idx]
        if lay == 'rows':
            return dst_ref.at[pl.ds(pl.multiple_of(idx * shape[0], shape[0]), shape[0]), :]
        return dst_ref.at[:, pl.ds(pl.multiple_of(idx * shape[1], shape[1]), shape[1])]

    def copies(self, src_refs, dst_refs, send_sems, recv_sems, local_sems):
        mx, my, mc = _mesh_pos()
        me = 4 * mx + 2 * my + mc
        out = []
        for a, (src, dst) in enumerate(zip(src_refs, dst_refs)):
            a2a = self.layouts[a] == 'a2a'
            out.append(pltpu.make_async_copy(src.at[me] if a2a else src, self._landing(a, dst, me), local_sems.at[a]))
            for k in range(1, N_DEV):
                px = 1 - mx if k & 4 else mx
                py = 1 - my if k & 2 else my
                pc = 1 - mc if k & 1 else mc
                out.append(pltpu.make_async_remote_copy(
                    src_ref=src.at[4 * px + 2 * py + pc] if a2a else src, dst_ref=self._landing(a, dst, me),
                    send_sem=send_sems.at[7 * a + k - 1], recv_sem=recv_sems.at[7 * a + k - 1],
                    device_id=(px, py, pc), device_id_type=MESH))
        return out


_ANY = pl.BlockSpec(memory_space=pl.ANY)


def _run_exchange(ex, name):
    n = len(ex.arrays)

    def body(*refs):
        cps = ex.copies(refs[:n], refs[n:2 * n], *refs[2 * n:])
        for cp in cps:
            cp.start()
        for cp in cps:
            cp.wait()

    return pl.pallas_call(body, out_shape=ex.out_shapes, in_specs=[_ANY] * n, out_specs=[_ANY] * n,
                          scratch_shapes=ex.scratch, name=name)(*ex.arrays)


def _call_beside(body, ex, nsteps, args, *, grid, in_specs, out_specs, out_shape, scratch_shapes, name):
    if ex is None:
        outs = pl.pallas_call(body, grid=grid, in_specs=in_specs, out_specs=out_specs, out_shape=out_shape,
                              scratch_shapes=scratch_shapes, compiler_params=_cparams(("arbitrary",)), name=name)(*args)
        return outs, None
    n_in, n_out, n_scr, n = len(in_specs), len(out_specs), len(scratch_shapes), len(ex.arrays)

    def wrapped(*refs):
        ins, refs = refs[:n_in], refs[n_in:]
        ex_ins, refs = refs[:n], refs[n:]
        outs, refs = refs[:n_out], refs[n_out:]
        ex_outs, refs = refs[:n], refs[n:]
        scr, sems = refs[:n_scr], refs[n_scr:]
        step = pl.program_id(0)

        @pl.when(step == 0)
        def _():
            for cp in ex.copies(ex_ins, ex_outs, *sems):
                cp.start()

        body(*ins, *outs, *scr)

        @pl.when(step == nsteps - 1)
        def _():
            for cp in ex.copies(ex_ins, ex_outs, *sems):
                cp.wait()

    res = pl.pallas_call(
        wrapped, grid=grid, in_specs=list(in_specs) + [_ANY] * n, out_specs=list(out_specs) + [_ANY] * n,
        out_shape=list(out_shape) + ex.out_shapes, scratch_shapes=list(scratch_shapes) + ex.scratch,
        compiler_params=_cparams(("arbitrary",)), name=name)(*args, *ex.arrays)
    return res[:n_out], res[n_out:]


def _adamw_math(w, g, m, v):
    m = ADAM_B1 * m + (1.0 - ADAM_B1) * g
    v = ADAM_B2 * v + (1.0 - ADAM_B2) * (g * g)
    m_hat = m / (1.0 - ADAM_B1 ** ADAM_STEP)
    v_hat = v / (1.0 - ADAM_B2 ** ADAM_STEP)
    delta = -ADAM_LR * (m_hat / (jnp.sqrt(v_hat) + ADAM_EPS) + ADAM_WD * w)
    return delta, m, v


def _sum_adamw(parts, w, m, v, tr, name):
    p, r, c = parts.shape
    tr = min(tr, r)
    assert r % tr == 0

    def body(p_ref, w_ref, m_ref, v_ref, g_ref, d_ref, nm_ref, nv_ref):
        g = p_ref[0].astype(F32)
        for j in range(1, p):
            g = g + p_ref[j].astype(F32)
        delta, nm, nv = _adamw_math(w_ref[...], g, m_ref[...], v_ref[...])
        g_ref[...] = g
        d_ref[...] = delta
        nm_ref[...] = nm
        nv_ref[...] = nv

    row = pl.BlockSpec((tr, c), lambda i: (i, 0))
    return pl.pallas_call(
        body, grid=(r // tr,), in_specs=[pl.BlockSpec((p, tr, c), lambda i: (0, i, 0)), row, row, row],
        out_specs=[row] * 4, out_shape=[jax.ShapeDtypeStruct((r, c), F32)] * 4,
        compiler_params=_cparams(("parallel",)), name=name)(parts, w, m, v)


def _sum_parts(parts, name):
    p, r, c = parts.shape

    def body(p_ref, o_ref):
        g = p_ref[0]
        for j in range(1, p):
            g = g + p_ref[j]
        o_ref[...] = g

    return pl.pallas_call(body, out_shape=jax.ShapeDtypeStruct((r, c), F32), name=name)(parts)


def _rows_of(shape):
    n = 1
    for d in shape:
        n *= d
    return n, -(-n // 128)


def _pack(arrs):
    blocks = []
    for a in arrs:
        n, nr = _rows_of(a.shape)
        blocks.append(jnp.pad(a.reshape(-1).astype(F32), (0, nr * 128 - n)).reshape(nr, 128))
    rows = sum(b.shape[0] for b in blocks)
    if rows % 8:
        blocks.append(jnp.zeros((8 - rows % 8, 128), F32))
    return jnp.concatenate(blocks, axis=0)


def _unpack(buf, shapes):
    out, r0 = [], 0
    for s in shapes:
        n, nr = _rows_of(s)
        out.append(buf[r0:r0 + nr].reshape(-1)[:n].reshape(s))
        r0 += nr
    return out


def _block_diag(w):
    rows = [jnp.pad(w[i], ((0, 0), (i * C_BLOCK_DIM, C_WIDTH - (i + 1) * C_BLOCK_DIM))) for i in range(C_BLOCKS)]
    return jnp.concatenate(rows, axis=0)


def _diag_blocks(m):
    m4 = m.reshape(C_BLOCKS, C_BLOCK_DIM, C_BLOCKS, C_BLOCK_DIM)
    return jnp.stack([m4[i, :, i, :] for i in range(C_BLOCKS)])


def _gate_row(v):
    return jnp.pad(v.astype(F32), (B_HEADS, 128 - 2 * B_HEADS)).reshape(1, 128)


def _permute_w_in(w):
    pad = jnp.zeros(w.shape[:-1] + (D_IN_PAD - D_IN,), w.dtype)
    return jnp.concatenate([w[..., :3072], w[..., 3080:3592], w[..., 3072:3080], pad], axis=-1)


def _unpermute_w_in(w):
    return jnp.concatenate([w[..., :3072], w[..., COL_G:COL_G + 8], w[..., 3072:COL_G]], axis=-1)


_WEIGHTS = ['norm1_g', 'w_in', 'hgrn_lb_logits', 'hgrn_norm_g', 'gdn_conv_w', 'gdn_a_log', 'gdn_dt_bias', 'gdn_norm_g',
            'lru_conv_w', 'lru_conv_b', 'lru_w_a', 'lru_b_a', 'lru_w_x', 'lru_b_x', 'lru_lambda', 'w_out', 'norm2_g',
            'w_up', 'w_down', 'final_norm_g']
_BIG = ('w_in', 'w_out', 'w_up', 'w_down')
_SHARDED_SMALL = ('gdn_conv_w', 'lru_conv_w')


def _step(x, target, w, m, v):
    t = x.shape[0]
    mx, my, mc = _mesh_pos()
    me = 4 * mx + 2 * my + mc

    bf = lambda a: a.astype(BF16)

    def full_w_in(g):
        return _permute_w_in(jnp.moveaxis(g, 0, 1).reshape(D_MODEL, D_IN))

    conv_shapes = [w['gdn_conv_w'].shape, w['lru_conv_w'].shape]
    g_in, g_conv = _run_exchange(
        _Exchange([bf(w['w_in'][0]), _pack([w['gdn_conv_w'], w['lru_conv_w']])], ['slot', 'slot']), "gather_first")
    w_in = [full_w_in(g_in)]
    w_out, w_up, w_down = [], [], []
    gdn_cw, lru_cw = [], []
    for j in range(N_DEV):
        a, b = _unpack(g_conv[j], conv_shapes)
        gdn_cw.append(a)
        lru_cw.append(b)
    gdn_cw = jnp.concatenate(gdn_cw, axis=-1)
    lru_cw = jnp.concatenate(lru_cw, axis=-1)

    lbnd = _lb_fwd(w['hgrn_lb_logits'])
    row = lambda a: a.reshape(1, -1)

    def c_args(l):
        return (lru_cw[l], row(w['lru_conv_b'][l]), _block_diag(w['lru_w_a'][l]), row(w['lru_b_a'][l]),
                _block_diag(w['lru_w_x'][l]), row(w['lru_b_x'][l]), row(w['lru_lambda'][l]))

    saved = []
    xl = x
    h = _rms_fwd(x, w['norm1_g'][0], name="rms_fwd")
    for l in range(DEPTH):
        proj = _mm_rows(h, w_in[l], "nn", 256, "mm_proj")
        mix_a, st_a = _a_fwd(proj, lbnd[l], w['hgrn_norm_g'][l])
        alr, dtr = _gate_row(w['gdn_a_log'][l]), _gate_row(w['gdn_dt_bias'][l])
        qkv, gates = _b_pre_fwd(proj, gdn_cw[l], alr, dtr)
        nxt = [bf(w['w_in'][l + 1])] if l + 1 < DEPTH else []
        gather = _Exchange([bf(w['w_out'][l]), bf(w['w_up'][l]), bf(w['w_down'][l])] + nxt, ['rows', 'cols', 'rows'] + ['slot'] * len(nxt))
        (mix_b, st_b, *b_saved), got = _b_fwd(qkv, gates, proj, w['gdn_norm_g'][l], beside=gather)
        w_out.append(got[0])
        w_up.append(got[1])
        w_down.append(got[2])
        if nxt:
            w_in.append(full_w_in(got[3]))
        mix_c, hs = _c_fwd(proj, *c_args(l))
        mixed = [mix_a, mix_b, mix_c]
        x_mid, h2 = _mm_rows(mixed, w_out[l], "nn", 1024, "mm_out", residual=xl, epilogue="rms_fwd", norm=w['norm2_g'][l])
        act, up = _mm_rows(h2, w_up[l], "nn", 256, "mm_up", epilogue="relu2")
        saved.append(dict(x=xl, h=h, proj=proj, st_a=st_a, qkv=qkv, gates=gates, st_b=st_b, b_saved=b_saved, hs=hs, mixed=mixed,
                          x_mid=x_mid, h2=h2, up=up, act=act, alr=alr, dtr=dtr))
        if l + 1 < DEPTH:
            xl, h = _mm_rows(act, w_down[l], "nn", 512, "mm_down", residual=x_mid, epilogue="rms_fwd", norm=w['norm1_g'][l + 1])
        else:
            xl = _mm_rows(act, w_down[l], "nn", 512, "mm_down_last", residual=x_mid)
    loss, dx, dgf = _loss_head(xl, w['final_norm_g'], target)

    gs = {n: [None] * DEPTH for n in _WEIGHTS}
    recv = {n: [None] * DEPTH for n in _BIG}
    dw_in_above = None
    for l in reversed(range(DEPTH)):
        s = saved[l]
        dup = _mm_rows(dx, w_down[l], "nt", 256, "mm_dact", epilogue="drelu2", up=s['up'])
        dw_down = _mm_tn(s['act'], dx, 512, "mm_dw_down").reshape(N_DEV, D_FF // N_DEV, D_MODEL)
        dx_mid, dg2 = _mm_rows(dup, w_up[l], "nt", 512, "mm_dh2", epilogue="rms_bwd", norm=(s['x_mid'], w['norm2_g'][l], dx))
        dw_up = _mm_tn(s['h2'], dup, 512, "mm_dw_up", slab=D_FF // N_DEV)
        gs['norm2_g'][l] = dg2[0]
        dmixed = _mm_rows(dx_mid, w_out[l], "nt", 1024, "mm_dmixed")
        dw_out = _mm_tn(s['mixed'], dx_mid, 1024, "mm_dw_out").reshape(N_DEV, D_MODEL // N_DEV, D_MODEL)
        proj = s['proj']
        above = [dw_in_above] if dw_in_above is not None else []
        (dpa, dlb, dnga), got = _a_bwd(proj, lbnd[l], w['hgrn_norm_g'][l], s['st_a'], dmixed,
                                       beside=_Exchange([dw_out] + above, ['a2a'] * (1 + len(above))))
        recv['w_out'][l] = got[0]
        if above:
            recv['w_in'][l + 1] = got[1]
        gs['hgrn_lb_logits'][l] = dlb[0]
        gs['hgrn_norm_g'][l] = dnga[0]
        (dqkv, dgates, dz, dngb), got = _b_bwd(s['qkv'], s['gates'], proj, w['gdn_norm_g'][l], s['st_b'], s['b_saved'], dmixed,
                                               beside=_Exchange([dw_up, dw_down], ['a2a', 'a2a']))
        recv['w_up'][l], recv['w_down'][l] = got
        dxb, dgi, dcwb, dal, ddt = _b_pre_bwd(proj, gdn_cw[l], s['alr'], s['dtr'], dqkv, dgates)
        gs['gdn_norm_g'][l] = dngb[0]
        gs['gdn_conv_w'][l] = dcwb
        gs['gdn_a_log'][l] = dal[0, B_HEADS:2 * B_HEADS]
        gs['gdn_dt_bias'][l] = ddt[0, B_HEADS:2 * B_HEADS]
        dxc_in, dyg, dcwc, dcb, dwa, dba, dwx, dbx, dlam = _c_bwd(proj, *c_args(l), s['hs'], dmixed)
        gs['lru_conv_w'][l] = dcwc
        gs['lru_conv_b'][l] = dcb[0]
        gs['lru_w_a'][l] = _diag_blocks(dwa)
        gs['lru_b_a'][l] = dba[0]
        gs['lru_w_x'][l] = _diag_blocks(dwx)
        gs['lru_b_x'][l] = dbx[0]
        gs['lru_lambda'][l] = dlam[0]
        dproj = [dpa, dxb, dz, dxc_in, dyg, dgi]
        dx, dg1 = _mm_rows(dproj, w_in[l], "nt", 512, "mm_dh", epilogue="rms_bwd", norm=(s['x'], w['norm1_g'][l], dx_mid))
        dw_in = _unpermute_w_in(_mm_tn(s['h'], dproj, 512, "mm_dw_in"))
        dw_in_above = jnp.moveaxis(dw_in.reshape(D_MODEL, N_DEV, D_IN // N_DEV), 1, 0)
        gs['norm1_g'][l] = dg1[0]
    grad_x = dx
    part = {n: jnp.stack(gs[n]) for n in _WEIGHTS if n != 'final_norm_g' and n not in _BIG}
    part['final_norm_g'] = dgf[0]
    part['hgrn_lb_logits'] = _lb_bwd(w['hgrn_lb_logits'], part['hgrn_lb_logits'])

    small = [n for n in _WEIGHTS if n not in _BIG]
    packed = _pack([part[n] for n in small] + [loss])
    recv['w_in'][0], all_small = _run_exchange(_Exchange([dw_in_above, packed], ['a2a', 'slot']), "exchange_last")

    grads, deltas, new_m, new_v = {}, {}, {}, {}
    for n in _BIG:
        shp = w[n].shape
        r2 = lambda a: a.reshape(-1, shp[-1])
        parts = jnp.stack(recv[n], axis=1).reshape(N_DEV, -1, shp[-1])
        g, d, nm, nv = _sum_adamw(parts, r2(w[n]), r2(m[n]), r2(v[n]), 256, "adamw_" + n)
        grads[n], deltas[n], new_m[n], new_v[n] = (a.reshape(shp) for a in (g, d, nm, nv))

    total = _sum_parts(all_small, "sum_small")
    summed = _unpack(total, [part[n].shape for n in small] + [(1, 1)])
    loss_total = summed[-1].reshape(())
    gsmall = dict(zip(small, summed[:-1]))
    for n in _SHARDED_SMALL:
        width = w[n].shape[-1]
        gsmall[n] = lax.dynamic_slice_in_dim(gsmall[n], me * width, width, axis=2)
    pk = lambda d: _pack([d[n] for n in small])
    _, d, nm, nv = _sum_adamw(pk(gsmall)[None], pk(w), pk(m), pk(v), 4096, "adamw_small")
    shapes = [w[n].shape for n in small]
    for n, dd, mm, vv in zip(small, _unpack(d, shapes), _unpack(nm, shapes), _unpack(nv, shapes)):
        grads[n], deltas[n], new_m[n], new_v[n] = gsmall[n], dd, mm, vv
    return loss_total, grad_x, grads, deltas, new_m, new_v


def kernel(x, norm1_g, w_in, hgrn_lb_logits, hgrn_norm_g, gdn_conv_w, gdn_a_log, gdn_dt_bias, gdn_norm_g, lru_conv_w, lru_conv_b, lru_w_a, lru_b_a, lru_w_x, lru_b_x, lru_lambda, w_out, norm2_g, w_up, w_down, final_norm_g, loss_target, m_norm1_g, m_w_in, m_hgrn_lb_logits, m_hgrn_norm_g, m_gdn_conv_w, m_gdn_a_log, m_gdn_dt_bias, m_gdn_norm_g, m_lru_conv_w, m_lru_conv_b, m_lru_w_a, m_lru_b_a, m_lru_w_x, m_lru_b_x, m_lru_lambda, m_w_out, m_norm2_g, m_w_up, m_w_down, m_final_norm_g, v_norm1_g, v_w_in, v_hgrn_lb_logits, v_hgrn_norm_g, v_gdn_conv_w, v_gdn_a_log, v_gdn_dt_bias, v_gdn_norm_g, v_lru_conv_w, v_lru_conv_b, v_lru_w_a, v_lru_b_a, v_lru_w_x, v_lru_b_x, v_lru_lambda, v_w_out, v_norm2_g, v_w_up, v_w_down, v_final_norm_g):
    w = dict(zip(_WEIGHTS, (norm1_g, w_in, hgrn_lb_logits, hgrn_norm_g, gdn_conv_w, gdn_a_log, gdn_dt_bias, gdn_norm_g, lru_conv_w, lru_conv_b, lru_w_a, lru_b_a, lru_w_x, lru_b_x, lru_lambda, w_out, norm2_g, w_up, w_down, final_norm_g)))
    m = dict(zip(_WEIGHTS, (m_norm1_g, m_w_in, m_hgrn_lb_logits, m_hgrn_norm_g, m_gdn_conv_w, m_gdn_a_log, m_gdn_dt_bias, m_gdn_norm_g, m_lru_conv_w, m_lru_conv_b, m_lru_w_a, m_lru_b_a, m_lru_w_x, m_lru_b_x, m_lru_lambda, m_w_out, m_norm2_g, m_w_up, m_w_down, m_final_norm_g)))
    v = dict(zip(_WEIGHTS, (v_norm1_g, v_w_in, v_hgrn_lb_logits, v_hgrn_norm_g, v_gdn_conv_w, v_gdn_a_log, v_gdn_dt_bias, v_gdn_norm_g, v_lru_conv_w, v_lru_conv_b, v_lru_w_a, v_lru_b_a, v_lru_w_x, v_lru_b_x, v_lru_lambda, v_w_out, v_norm2_g, v_w_up, v_w_down, v_final_norm_g)))
    loss, grad_x, grads, deltas, new_m, new_v = _step(x.reshape(x.shape[1:]), loss_target.reshape(x.shape[1:]), w, m, v)
    return (loss, grad_x[None], *[grads[n] for n in _WEIGHTS], *[deltas[n] for n in _WEIGHTS],
            *[new_m[n] for n in _WEIGHTS], *[new_v[n] for n in _WEIGHTS])
```

```python
import jax
import jax.numpy as jnp
from jax import lax
from jax.experimental import pallas as pl
from jax.experimental.pallas import tpu as pltpu

F32 = jnp.float32
BF16 = jnp.bfloat16
MESH = pl.DeviceIdType.MESH

N_DEV = 8
D_MODEL = 1024
DEPTH = 4
A_HEADS, A_DIM, A_WIDTH = 4, 64, 256
B_HEADS, B_DIM, B_WIDTH = 4, 128, 512
C_WIDTH, C_BLOCKS, C_BLOCK_DIM = 256, 4, 64
D_IN = 3592
D_IN_PAD = 3840
COL_A, COL_B, COL_C, COL_G = 0, 1024, 3072, 3584
D_FF = 4096
CONV_K = 4
CHUNK = 64
SUB = 16
RG_C = 8.0
EPS = 1e-6
TINY = 1e-30
EXP_CLAMP = 80.0
GDN_SCALE = B_DIM ** -0.5
ADAM_LR, ADAM_B1, ADAM_B2, ADAM_EPS, ADAM_WD, ADAM_STEP = 0.001, 0.9, 0.999, 1e-08, 0.01, 10
VMEM_LIMIT = 56 * 1024 * 1024


def _cparams(sem=None):
    return pltpu.CompilerParams(dimension_semantics=sem, vmem_limit_bytes=VMEM_LIMIT)


_DIMS = {"nn": (((1,), (0,)), ((), ())), "nt": (((1,), (1,)), ((), ())), "tn": (((0,), (0,)), ((), ()))}


def _split_bf16(x):
    hi = x.astype(BF16)
    return hi, (x - hi.astype(F32)).astype(BF16)


def _dot(a, b, mode="nn", hi=False):
    if not hi:
        return lax.dot_general(a.astype(BF16), b.astype(BF16), _DIMS[mode], preferred_element_type=F32)
    ah, al = _split_bf16(a.astype(F32))
    bh, bl = _split_bf16(b.astype(F32))
    ka = 0 if mode == "tn" else 1
    kb = 1 if mode == "nt" else 0
    return lax.dot_general(jnp.concatenate([ah, ah, al], axis=ka), jnp.concatenate([bh, bl, bh], axis=kb),
                           _DIMS[mode], preferred_element_type=F32)


def _dot_exact_lhs(lhs, x, mode="nn"):
    l_bf16 = lhs.astype(BF16)
    x1 = x.astype(BF16)
    r1 = x - x1.astype(F32)
    x2 = r1.astype(BF16)
    x3 = (r1 - x2.astype(F32)).astype(BF16)
    ka = 0 if mode == "tn" else 1
    return lax.dot_general(jnp.concatenate([l_bf16] * 3, axis=ka), jnp.concatenate([x1, x2, x3], axis=0),
                           _DIMS[mode], preferred_element_type=F32)


def _iota2(n, m):
    return lax.broadcasted_iota(jnp.int32, (n, m), 0), lax.broadcasted_iota(jnp.int32, (n, m), 1)


def _tril(n, strict=False):
    r, c = _iota2(n, n)
    return (r > c) if strict else (r >= c)


def _sigmoid(x):
    return 1.0 / (1.0 + jnp.exp(-x))


def _softplus(x):
    return jnp.maximum(x, 0.0) + jnp.log(1.0 + jnp.exp(-jnp.abs(x)))


def _neg_expm1(z):
    series = -z * (1.0 + z * (0.5 + z * (1.0 / 6.0)))
    return jnp.where(z > -1e-2, series, 1.0 - jnp.exp(z))


def _gelu_tanh(x):
    c = 0.7978845608028654
    u = c * (x + 0.044715 * x * x * x)
    t = jnp.tanh(u)
    g = 0.5 * x * (1.0 + t)
    dg = 0.5 * (1.0 + t) + 0.5 * x * (1.0 - t * t) * c * (1.0 + 3.0 * 0.044715 * x * x)
    return g, dg


def _shift_rows(cur, halo, s, down=True):
    n = cur.shape[0]
    ridx = lax.broadcasted_iota(jnp.int32, (8, cur.shape[1]), 0)
    if down:
        main = pltpu.roll(cur, s, 0)
        fix = jnp.where(ridx < s, pltpu.roll(halo, s, 0), main[0:8])
        return jnp.concatenate([fix, main[8:]], axis=0)
    main = pltpu.roll(cur, n - s, 0)
    fix = jnp.where(ridx >= 8 - s, pltpu.roll(halo, 8 - s, 0), main[n - 8:n])
    return jnp.concatenate([main[:n - 8], fix], axis=0)


def _conv_bwd_rows(dy, nxt8, w):
    dx = dy * w[3:4]
    for j in range(3):
        dx = dx + _shift_rows(dy, nxt8, 3 - j, down=False) * w[j:j + 1]
    return dx


def _tile_scan(a, b, reverse=False):
    n = a.shape[0]
    r = lax.broadcasted_iota(jnp.int32, a.shape, 0) % 8
    for s in (1, 2, 4):
        keep = (r < 8 - s) if reverse else (r >= s)
        shift = n - s if reverse else s
        a_sh = jnp.where(keep, pltpu.roll(a, shift, 0), 1.0)
        b_sh = jnp.where(keep, pltpu.roll(b, shift, 0), 0.0)
        b = b + a * b_sh
        a = a * a_sh
    return a, b


def _conv_fwd(cur, prev8, w):
    y = cur * w[3:4]
    for j in range(3):
        y = y + _shift_rows(cur, prev8, 3 - j, down=True) * w[j:j + 1]
    return y


def _mm_rows(a, w, mode, tm, name, residual=None, epilogue=None, up=None, norm=None):
    parts = list(a) if isinstance(a, (list, tuple)) else [a]
    widths = [p.shape[1] for p in parts]
    t = parts[0].shape[0]
    n = w.shape[1] if mode == "nn" else w.shape[0]
    tm = min(tm, t)
    assert t % tm == 0 and all(wd % 128 == 0 for wd in widths)

    def body(*refs):
        a_refs, w_ref, rest = refs[:len(parts)], refs[len(parts)], refs[len(parts) + 1:]
        y, off = None, 0
        for a_ref, width in zip(a_refs, widths):
            wk = w_ref[off:off + width, :] if mode == "nn" else w_ref[:, off:off + width]
            d = _dot(a_ref[...], wk, mode)
            y = d if y is None else y + d
            off += width
        if residual is not None:
            y = y + rest[0][...]
        if epilogue == "relu2":
            r = jnp.maximum(y, 0.0)
            refs[-2][...] = (r * r).astype(BF16)
            refs[-1][...] = y.astype(BF16)
        elif epilogue == "drelu2":
            refs[-1][...] = (y * 2.0 * jnp.maximum(rest[0][...].astype(F32), 0.0)).astype(BF16)
        elif epilogue == "rms_fwd":
            rinv = lax.rsqrt(jnp.mean(y * y, axis=-1, keepdims=True) + EPS)
            refs[-2][...] = y
            refs[-1][...] = (y * rinv * refs[-3][...]).astype(BF16)
        elif epilogue == "rms_bwd":
            x_ref, g_ref, dres_ref, dx_ref, dg_ref = rest

            @pl.when(pl.program_id(0) == 0)
            def _():
                dg_ref[...] = jnp.zeros_like(dg_ref)

            xv = x_ref[...]
            rinv = lax.rsqrt(jnp.mean(xv * xv, axis=-1, keepdims=True) + EPS)
            xhat = xv * rinv
            dxh = y * g_ref[...]
            dx_ref[...] = dres_ref[...] + rinv * (dxh - xhat * jnp.mean(dxh * xhat, axis=-1, keepdims=True))
            dg_ref[...] += jnp.sum(y * xhat, axis=0, keepdims=True)
        else:
            refs[-1][...] = y

    rows = lambda width: pl.BlockSpec((tm, width), lambda i: (i, 0))
    vec = pl.BlockSpec((1, n), lambda i: (0, 0))
    ins, specs = parts + [w], [rows(wd) for wd in widths] + [pl.BlockSpec(w.shape, lambda i: (0, 0))]
    if residual is not None:
        ins.append(residual)
        specs.append(rows(n))
    if epilogue == "drelu2":
        ins.append(up)
        specs.append(rows(n))
    if epilogue == "rms_fwd":
        ins.append(norm.reshape(1, n))
        specs.append(vec)
        out_specs, out_shape = [rows(n), rows(n)], [jax.ShapeDtypeStruct((t, n), F32), jax.ShapeDtypeStruct((t, n), BF16)]
    elif epilogue == "rms_bwd":
        ins += [norm[0], norm[1].reshape(1, n), norm[2]]
        specs += [rows(n), vec, rows(n)]
        out_specs, out_shape = [rows(n), vec], [jax.ShapeDtypeStruct((t, n), F32), jax.ShapeDtypeStruct((1, n), F32)]
    elif epilogue == "relu2":
        out_specs, out_shape = [rows(n), rows(n)], [jax.ShapeDtypeStruct((t, n), BF16)] * 2
    else:
        out_specs, out_shape = rows(n), jax.ShapeDtypeStruct((t, n), BF16 if epilogue == "drelu2" else F32)
    return pl.pallas_call(body, grid=(t // tm,), in_specs=specs, out_specs=out_specs, out_shape=out_shape,
                          compiler_params=_cparams(("arbitrary" if epilogue == "rms_bwd" else "parallel",)), name=name)(*ins)


MM_TN_TILE = 1024


def _mm_tn(a, b, tk, name, slab=None):
    a_parts = list(a) if isinstance(a, (list, tuple)) else [a]
    b_parts = list(b) if isinstance(b, (list, tuple)) else [b]
    wa, wb = [p.shape[1] for p in a_parts], [p.shape[1] for p in b_parts]
    t, m, n = a_parts[0].shape[0], sum(wa), sum(wb)
    tk = min(tk, t)
    assert t % tk == 0 and all(x % 128 == 0 for x in wa + wb)
    nk = t // tk

    def body(*refs):
        a_refs, b_refs = refs[:len(wa)], refs[len(wa):len(wa) + len(wb)]
        o_ref, acc = refs[-2], refs[-1]
        kk = pl.program_id(0)

        @pl.when(kk == 0)
        def _():
            acc[...] = jnp.zeros_like(acc)

        ro = 0
        for a_ref, width_a in zip(a_refs, wa):
            for r0 in range(0, width_a, MM_TN_TILE):
                rw = min(MM_TN_TILE, width_a - r0)
                av = a_ref[:, r0:r0 + rw]
                co = 0
                for b_ref, width_b in zip(b_refs, wb):
                    for c0 in range(0, width_b, MM_TN_TILE):
                        cw = min(MM_TN_TILE, width_b - c0)
                        acc[ro + r0:ro + r0 + rw, co + c0:co + c0 + cw] += _dot(av, b_ref[:, c0:c0 + cw], "tn")
                    co += width_b
            ro += width_a

        @pl.when(kk == nk - 1)
        def _():
            if slab is None:
                o_ref[...] = acc[...].astype(BF16)
            else:
                for s in range(n // slab):
                    o_ref[s] = acc[:, s * slab:(s + 1) * slab].astype(BF16)

    if slab is None:
        out_spec, out_shape = pl.BlockSpec((m, n), lambda kk: (0, 0)), jax.ShapeDtypeStruct((m, n), BF16)
    else:
        out_spec, out_shape = pl.BlockSpec((n // slab, m, slab), lambda kk: (0, 0, 0)), jax.ShapeDtypeStruct((n // slab, m, slab), BF16)
    return pl.pallas_call(
        body, grid=(nk,),
        in_specs=[pl.BlockSpec((tk, x), lambda kk: (kk, 0)) for x in wa + wb],
        out_specs=out_spec, out_shape=out_shape, scratch_shapes=[pltpu.VMEM((m, n), F32)],
        compiler_params=_cparams(("arbitrary",)), name=name)(*a_parts, *b_parts)


def _rms_fwd(x, g, tb=512, name="rms_fwd"):
    t, d = x.shape

    def body(x_ref, g_ref, h_ref):
        xv = x_ref[...]
        rinv = lax.rsqrt(jnp.mean(xv * xv, axis=-1, keepdims=True) + EPS)
        h_ref[...] = (xv * rinv * g_ref[...]).astype(BF16)

    return pl.pallas_call(
        body, grid=(t // tb,), in_specs=[pl.BlockSpec((tb, d), lambda i: (i, 0)), pl.BlockSpec((1, d), lambda i: (0, 0))],
        out_specs=pl.BlockSpec((tb, d), lambda i: (i, 0)), out_shape=jax.ShapeDtypeStruct((t, d), BF16),
        compiler_params=_cparams(("parallel",)), name=name)(x, g.reshape(1, d))


def _loss_head(x, g, target, tb=512):
    t, d = x.shape

    def body(x_ref, g_ref, t_ref, loss_ref, dx_ref, dg_ref):
        @pl.when(pl.program_id(0) == 0)
        def _():
            dg_ref[...] = jnp.zeros_like(dg_ref)
            loss_ref[...] = jnp.zeros_like(loss_ref)

        xv = x_ref[...]
        rinv = lax.rsqrt(jnp.mean(xv * xv, axis=-1, keepdims=True) + EPS)
        xhat = xv * rinv
        err = xhat * g_ref[...] - t_ref[...]
        loss_ref[...] += 0.5 * jnp.sum(jnp.mean(err * err, axis=-1, keepdims=True), axis=0, keepdims=True)
        dy = err * (1.0 / d)
        dxh = dy * g_ref[...]
        dx_ref[...] = rinv * (dxh - xhat * jnp.mean(dxh * xhat, axis=-1, keepdims=True))
        dg_ref[...] += jnp.sum(dy * xhat, axis=0, keepdims=True)

    row = pl.BlockSpec((tb, d), lambda i: (i, 0))
    vec = pl.BlockSpec((1, d), lambda i: (0, 0))
    one = pl.BlockSpec((1, 1), lambda i: (0, 0))
    return pl.pallas_call(
        body, grid=(t // tb,), in_specs=[row, vec, row], out_specs=[one, row, vec],
        out_shape=[jax.ShapeDtypeStruct((1, 1), F32), jax.ShapeDtypeStruct((t, d), F32), jax.ShapeDtypeStruct((1, d), F32)],
        compiler_params=_cparams(("arbitrary",)), name="loss_head")(x, g.reshape(1, d), target)


def _lb_fwd(logits):
    def body(l_ref, o_ref):
        lg = l_ref[...]
        e = jnp.exp(lg - jnp.max(lg, axis=0, keepdims=True))
        p = e / jnp.sum(e, axis=0, keepdims=True)
        c = jnp.zeros_like(p[0:1])
        rows = [c]
        for l in range(1, DEPTH):
            c = c + p[l:l + 1]
            rows.append(c)
        o_ref[...] = jnp.minimum(jnp.maximum(jnp.concatenate(rows, axis=0), 0.0), 1.0 - EPS)

    return pl.pallas_call(body, out_shape=jax.ShapeDtypeStruct(logits.shape, F32), name="lb_fwd")(logits)


def _lb_bwd(logits, dlb):
    def body(l_ref, d_ref, o_ref):
        lg = l_ref[...]
        e = jnp.exp(lg - jnp.max(lg, axis=0, keepdims=True))
        p = e / jnp.sum(e, axis=0, keepdims=True)
        hi = 1.0 - EPS
        c = jnp.zeros_like(p[0:1])
        dc = []
        for l in range(1, DEPTH):
            c = c + p[l:l + 1]
            gl = jnp.where(c < 0.0, 0.0, jnp.where(c == 0.0, 0.5, 1.0)) * jnp.where(c > hi, 0.0, jnp.where(c == hi, 0.5, 1.0))
            dc.append(d_ref[l:l + 1, :] * gl)
        dp = [jnp.zeros_like(c)]
        for j in range(1, DEPTH):
            s = dc[j - 1]
            for l in range(j + 1, DEPTH):
                s = s + dc[l - 1]
            dp.append(s)
        dpm = jnp.concatenate(dp, axis=0)
        o_ref[...] = p * (dpm - jnp.sum(p * dpm, axis=0, keepdims=True))

    return pl.pallas_call(body, out_shape=jax.ShapeDtypeStruct(logits.shape, F32), name="lb_bwd")(logits, dlb)


def _a_gates(qi, fi, lbh):
    sq = _sigmoid(qi)
    q = qi * sq
    e = jnp.exp(-jnp.abs(fi))
    rec = 1.0 / (1.0 + e)
    pos = fi >= 0.0
    sg = jnp.where(pos, rec, e * rec)
    sgn = jnp.where(pos, e * rec, rec)
    f = lbh + (1.0 - lbh) * sg
    logf = jnp.log(jnp.maximum(f, TINY))
    k = (1.0 - lbh) * sgn
    return q, sq, sg, sgn, f, logf, k


def _headnorm_fwd(o, g, gate_in):
    rinv = lax.rsqrt(jnp.mean(o * o, axis=-1, keepdims=True) + EPS)
    sg = _sigmoid(gate_in)
    return o * rinv * g * (gate_in * sg)


def _headnorm_bwd(dout, o, g, gate_in):
    rinv = lax.rsqrt(jnp.mean(o * o, axis=-1, keepdims=True) + EPS)
    xhat = o * rinv
    sg = _sigmoid(gate_in)
    silu = gate_in * sg
    dy = dout * silu
    dgate = dout * xhat * g * (sg * (1.0 + gate_in * (1.0 - sg)))
    dxh = dy * g
    do = rinv * (dxh - xhat * jnp.mean(dxh * xhat, axis=-1, keepdims=True))
    return do, dgate, jnp.sum(dy * xhat, axis=0, keepdims=True)


A_PAIRS, A_PAIR_W = A_HEADS // 2, 2 * A_DIM


def _lo_half(shape):
    return lax.broadcasted_iota(jnp.int32, shape, len(shape) - 1) < A_DIM


def _pair_blockdiag(x):
    lo = _lo_half(x.shape)
    return jnp.concatenate([jnp.where(lo, x, 0.0), jnp.where(lo, 0.0, x)], axis=0)


def _pair_fold(m):
    n = m.shape[0] // 2
    return jnp.where(_lo_half((n, A_PAIR_W)), m[:n], m[n:])


def _pair_norm_stats(o):
    lo = _lo_half(o.shape)
    sq = o * o
    s0 = jnp.sum(jnp.where(lo, sq, 0.0), axis=-1, keepdims=True)
    s1 = jnp.sum(sq, axis=-1, keepdims=True) - s0
    return jnp.where(lo, lax.rsqrt(s0 * (1.0 / A_DIM) + EPS), lax.rsqrt(s1 * (1.0 / A_DIM) + EPS))


def _pair_mean(x):
    lo = _lo_half(x.shape)
    s0 = jnp.sum(jnp.where(lo, x, 0.0), axis=-1, keepdims=True)
    s1 = jnp.sum(x, axis=-1, keepdims=True) - s0
    return jnp.where(lo, s0, s1) * (1.0 / A_DIM)


def _a_pair_chunk(qi, fi, v, lb2, s_bd, ltri, causal2):
    q, sq, sg, sgn, f, logf, k = _a_gates(qi, fi, lb2)
    cum = _dot_exact_lhs(ltri, logf)
    cl = cum[CHUNK - 1:CHUNK]
    ecum, ekd, cd = jnp.exp(cum), jnp.exp(cl - cum), jnp.exp(cl)
    qd, kd = q * ecum, k * ekd
    subs, rows = [], []
    for i in range(CHUNK // SUB):
        lo = i * SUB
        r = cum[lo - 1:lo] if i > 0 else jnp.zeros_like(cl)
        eq = jnp.exp(cum[lo:lo + SUB] - r)
        ek = jnp.exp(jnp.minimum(r - cum, EXP_CLAMP))
        qt = q[lo:lo + SUB] * eq
        kt_bd = _pair_blockdiag(k * ek)
        rows.append(_dot(qt, kt_bd, "nt", hi=True))
        subs.append((qt, eq, kt_bd, ek))
    attn = jnp.where(causal2, jnp.concatenate(rows, axis=0), 0.0)
    v_bd = _pair_blockdiag(v)
    o = _dot(qd, s_bd) + _dot(attn, v_bd)
    return dict(q=q, sq=sq, sg=sg, sgn=sgn, f=f, k=k, cum=cum, cl=cl, ecum=ecum, ekd=ekd, cd=cd, qd=qd, kd=kd,
                subs=subs, attn=attn, v_bd=v_bd, o=o)


def _a_fwd(proj, lb, norm_g, tb=256):
    t = proj.shape[0]
    nch = tb // CHUNK

    def body(q_ref, f_ref, i_ref, g_ref, lb_ref, ng_ref, out_ref, st_ref, s_scr):
        @pl.when(pl.program_id(0) == 0)
        def _():
            s_scr[...] = jnp.zeros_like(s_scr)

        ltri = _tril(CHUNK).astype(F32)
        r, c = _iota2(CHUNK, A_PAIR_W)
        causal2 = r >= c % CHUNK
        rb, cb = _iota2(A_PAIR_W, A_PAIR_W)
        diag_blocks = (rb < A_DIM) == (cb < A_DIM)

        def chunk(c, carry):
            rows = pl.ds(pl.multiple_of(c * CHUNK, CHUNK), CHUNK)
            ps = range(A_PAIRS)
            cols = [slice(p * A_PAIR_W, (p + 1) * A_PAIR_W) for p in ps]
            s0 = [s_scr[p] for p in ps]
            for p in ps:
                st_ref[c, p] = s0[p]
            v = [i_ref[rows, cols[p]] for p in ps]
            ch = [_a_pair_chunk(q_ref[rows, cols[p]], f_ref[rows, cols[p]], v[p], lb_ref[:, cols[p]], s0[p], ltri, causal2)
                  for p in ps]
            for p in ps:
                upd = jnp.where(diag_blocks, _dot(ch[p]["kd"], v[p], "tn"), 0.0)
                s_scr[p] = s0[p] * ch[p]["cd"].T + upd
            outs = []
            for p in ps:
                gi = g_ref[rows, cols[p]]
                outs.append(ch[p]["o"] * _pair_norm_stats(ch[p]["o"]) * ng_ref[...] * (gi * _sigmoid(gi)))
            out_ref[rows, :] = jnp.concatenate(outs, axis=1).astype(BF16)
            return carry

        lax.fori_loop(0, nch, chunk, 0, unroll=2)

    colblk = lambda j: pl.BlockSpec((tb, A_WIDTH), lambda i, j=j: (i, j))
    return pl.pallas_call(
        body, grid=(t // tb,),
        in_specs=[colblk(0), colblk(1), colblk(2), colblk(3), pl.BlockSpec((1, A_WIDTH), lambda i: (0, 0)),
                  pl.BlockSpec((1, A_PAIR_W), lambda i: (0, 0))],
        out_specs=[pl.BlockSpec((tb, A_WIDTH), lambda i: (i, 0)),
                   pl.BlockSpec((nch, A_PAIRS, A_PAIR_W, A_PAIR_W), lambda i: (i, 0, 0, 0))],
        out_shape=[jax.ShapeDtypeStruct((t, A_WIDTH), BF16),
                   jax.ShapeDtypeStruct((t // CHUNK, A_PAIRS, A_PAIR_W, A_PAIR_W), F32)],
        scratch_shapes=[pltpu.VMEM((A_PAIRS, A_PAIR_W, A_PAIR_W), F32)],
        compiler_params=_cparams(("arbitrary",)), name="hgrn_fwd")(
            proj, proj, proj, proj, lb.reshape(1, A_WIDTH), jnp.tile(norm_g.reshape(1, A_DIM), (1, 2)))


def _a_bwd(proj, lb, norm_g, states, dmixed, beside=None, tb=256):
    t = proj.shape[0]
    nch = tb // CHUNK
    nb = t // tb

    def body(q_ref, f_ref, i_ref, g_ref, lb_ref, ng_ref, st_ref, dm_ref, dp_ref, dlb_ref, dng_ref, ds_scr):
        @pl.when(pl.program_id(0) == 0)
        def _():
            ds_scr[...] = jnp.zeros_like(ds_scr)
            dlb_ref[...] = jnp.zeros_like(dlb_ref)
            dng_ref[...] = jnp.zeros_like(dng_ref)

        ltri = _tril(CHUNK).astype(F32)
        r, c = _iota2(CHUNK, A_PAIR_W)
        causal2 = r >= c % CHUNK
        rb, cb = _iota2(A_PAIR_W, A_PAIR_W)
        diag_blocks = (rb < A_DIM) == (cb < A_DIM)
        ones8 = jnp.ones((8, A_PAIR_W), F32)

        def chunk(cc, carry):
            c = nch - 1 - cc
            rows = pl.ds(pl.multiple_of(c * CHUNK, CHUNK), CHUNK)
            ps = range(A_PAIRS)
            cols = [slice(p * A_PAIR_W, (p + 1) * A_PAIR_W) for p in ps]
            qi = [q_ref[rows, cols[p]] for p in ps]
            gi = [g_ref[rows, cols[p]] for p in ps]
            v = [i_ref[rows, cols[p]] for p in ps]
            lb2 = [lb_ref[:, cols[p]] for p in ps]
            s0 = [st_ref[c, p] for p in ps]
            ds = [ds_scr[p] for p in ps]
            ch = [_a_pair_chunk(qi[p], f_ref[rows, cols[p]], v[p], lb2[p], s0[p], ltri, causal2) for p in ps]
            o = [ch[p]["o"] for p in ps]
            rinv = [_pair_norm_stats(o[p]) for p in ps]
            xhat = [o[p] * rinv[p] for p in ps]
            sgg = [_sigmoid(gi[p]) for p in ps]
            dout = [dm_ref[rows, cols[p]].astype(F32) for p in ps]
            dy = [dout[p] * (gi[p] * sgg[p]) for p in ps]
            dgi = [dout[p] * xhat[p] * ng_ref[...] * (sgg[p] * (1.0 + gi[p] * (1.0 - sgg[p]))) for p in ps]
            dxh = [dy[p] * ng_ref[...] for p in ps]
            do = [rinv[p] * (dxh[p] - xhat[p] * _pair_mean(dxh[p] * xhat[p])) for p in ps]
            dng = sum(jnp.sum(dy[p] * xhat[p], axis=0, keepdims=True) for p in ps)
            dqd = [_dot(do[p], s0[p], "nt") for p in ps]
            dattn = [jnp.where(causal2, _dot(do[p], ch[p]["v_bd"], "nt"), 0.0) for p in ps]
            dv = [_pair_fold(_dot(ch[p]["attn"], do[p], "tn")) + _dot(ch[p]["kd"], ds[p]) for p in ps]
            dkd = [_dot(v[p], ds[p], "nt") for p in ps]
            dcd = [_dot(ones8, s0[p] * ds[p], "nt", hi=True)[0:1] for p in ps]
            for p in ps:
                ds_scr[p] = jnp.where(diag_blocks, _dot(ch[p]["qd"], do[p], "tn"), 0.0) + ds[p] * ch[p]["cd"].T
            dq_i, dk_i = [], []
            for p in ps:
                dq_rows, dk = [], None
                for i, (qt, eq, kt_bd, ek) in enumerate(ch[p]["subs"]):
                    da = dattn[p][i * SUB:(i + 1) * SUB]
                    dq_rows.append(_dot(da, kt_bd, "nn", hi=True) * eq)
                    d = _pair_fold(_dot(da, qt, "tn", hi=True)) * ek
                    dk = d if dk is None else dk + d
                dq_i.append(jnp.concatenate(dq_rows, axis=0))
                dk_i.append(dk)
            dq = [dqd[p] * ch[p]["ecum"] + dq_i[p] for p in ps]
            dk = [dkd[p] * ch[p]["ekd"] + dk_i[p] for p in ps]
            dkk = [dkd[p] * ch[p]["kd"] for p in ps]
            dcum = [dqd[p] * ch[p]["qd"] - dkk[p] + ch[p]["q"] * dq_i[p] - ch[p]["k"] * dk_i[p] for p in ps]
            dcl = [jnp.sum(dkk[p], axis=0, keepdims=True) + dcd[p] * ch[p]["cd"] for p in ps]
            dlogf = [_dot_exact_lhs(ltri, dcum[p], "tn") + dcl[p] for p in ps]
            dfv = [jnp.where(ch[p]["f"] > TINY, dlogf[p] / ch[p]["f"], 0.0) for p in ps]
            dfi = [dfv[p] * (1.0 - lb2[p]) * ch[p]["sg"] * (1.0 - ch[p]["sg"])
                   - dk[p] * (1.0 - lb2[p]) * ch[p]["sgn"] * (1.0 - ch[p]["sgn"]) for p in ps]
            dlbs = [jnp.sum(dfv[p] * (1.0 - ch[p]["sg"]) - dk[p] * ch[p]["sgn"], axis=0, keepdims=True) for p in ps]
            dqs = [dq[p] * (ch[p]["sq"] * (1.0 + qi[p] * (1.0 - ch[p]["sq"]))) for p in ps]
            dp_ref[rows, :] = jnp.concatenate(dqs + dfi + dv + dgi, axis=1).astype(BF16)
            dlb_ref[...] += jnp.concatenate(dlbs, axis=1)
            dng_ref[...] += dng
            return carry

        lax.fori_loop(0, nch, chunk, 0, unroll=2)

    colblk = lambda j: pl.BlockSpec((tb, A_WIDTH), lambda i, j=j: (nb - 1 - i, j))
    vec = lambda n: pl.BlockSpec((1, n), lambda i: (0, 0))
    return _call_beside(
        body, beside, nb,
        (proj, proj, proj, proj, lb.reshape(1, A_WIDTH), jnp.tile(norm_g.reshape(1, A_DIM), (1, 2)), states, dmixed), grid=(nb,),
        in_specs=[colblk(0), colblk(1), colblk(2), colblk(3), vec(A_WIDTH), vec(A_PAIR_W),
                  pl.BlockSpec((nch, A_PAIRS, A_PAIR_W, A_PAIR_W), lambda i: (nb - 1 - i, 0, 0, 0)), colblk(0)],
        out_specs=[pl.BlockSpec((tb, 4 * A_WIDTH), lambda i: (nb - 1 - i, 0)), vec(A_WIDTH), vec(A_PAIR_W)],
        out_shape=[jax.ShapeDtypeStruct((t, 4 * A_WIDTH), BF16), jax.ShapeDtypeStruct((1, A_WIDTH), F32),
                   jax.ShapeDtypeStruct((1, A_PAIR_W), F32)],
        scratch_shapes=[pltpu.VMEM((A_PAIRS, A_PAIR_W, A_PAIR_W), F32)], name="hgrn_bwd")


def _gate_lane_masks(shape):
    lane = lax.broadcasted_iota(jnp.int32, shape, 1)
    return lane < B_HEADS, (lane >= B_HEADS) & (lane < 2 * B_HEADS)


def _b_pre_fwd(proj, conv_w, alog_row, dtb_row, tb=512):
    t = proj.shape[0]
    cb0 = COL_B // B_WIDTH

    def body(q_ref, k_ref, v_ref, qp_ref, kp_ref, vp_ref, w_ref, gi_ref, al_ref, dt_ref, qkv_ref, gates_ref):
        first = pl.program_id(0) == 0
        for part, (c_ref, p_ref) in enumerate(((q_ref, qp_ref), (k_ref, kp_ref), (v_ref, vp_ref))):
            cols = slice(part * B_WIDTH, (part + 1) * B_WIDTH)
            prev = jnp.where(first, 0.0, p_ref[...])
            y = _conv_fwd(c_ref[...], prev, w_ref[:, cols])
            s = y * _sigmoid(y)
            if part < 2:
                outs = []
                for h in range(B_HEADS):
                    sh = s[:, h * B_DIM:(h + 1) * B_DIM]
                    outs.append(sh * lax.rsqrt(jnp.sum(sh * sh, axis=-1, keepdims=True) + EPS))
                s = jnp.concatenate(outs, axis=1)
            qkv_ref[:, cols] = s
        g = gi_ref[...]
        is_b, is_a = _gate_lane_masks(g.shape)
        la = -jnp.exp(al_ref[...]) * _softplus(g + dt_ref[...])
        gates_ref[...] = jnp.where(is_b, _sigmoid(g), jnp.where(is_a, la, 0.0))

    cur = lambda j: pl.BlockSpec((tb, B_WIDTH), lambda i, j=j: (i, cb0 + j))
    prv = lambda j: pl.BlockSpec((8, B_WIDTH), lambda i, j=j: (jnp.maximum(i * (tb // 8) - 1, 0), cb0 + j))
    vec = pl.BlockSpec((1, 128), lambda i: (0, 0))
    return pl.pallas_call(
        body, grid=(t // tb,),
        in_specs=[cur(0), cur(1), cur(2), prv(0), prv(1), prv(2), pl.BlockSpec((CONV_K, 3 * B_WIDTH), lambda i: (0, 0)),
                  pl.BlockSpec((tb, 128), lambda i: (i, COL_G // 128)), vec, vec],
        out_specs=[pl.BlockSpec((tb, 3 * B_WIDTH), lambda i: (i, 0)), pl.BlockSpec((tb, 128), lambda i: (i, 0))],
        out_shape=[jax.ShapeDtypeStruct((t, 3 * B_WIDTH), F32), jax.ShapeDtypeStruct((t, 128), F32)],
        compiler_params=_cparams(("parallel",)), name="gdn_pre_fwd")(proj, proj, proj, proj, proj, proj, conv_w, proj, alog_row, dtb_row)


def _inv_unit_lower(amats):
    r, c = _iota2(CHUNK, CHUNK)
    eye = jnp.where(r == c, 1.0, 0.0)
    ps = [eye - a for a in amats]
    aks = amats
    for _ in range(5):
        aks = [_dot(ak, ak, hi=True) for ak in aks]
        ps = [p + _dot(p, ak, hi=True) for p, ak in zip(ps, aks)]
    return ps


def _b_local(qs, ks, vs, betas, gcs, grows, gls, solve=True):
    hs = range(len(qs))
    causal, strict = _tril(CHUNK), _tril(CHUNK, strict=True)
    decay = [jnp.where(causal, jnp.exp(jnp.minimum(gcs[h] - grows[h], 0.0)), 0.0) for h in hs]
    kb = [ks[h] * betas[h] for h in hs]
    kk = [_dot(kb[h], ks[h], "nt") for h in hs]
    qkr = [_dot(qs[h], ks[h], "nt") for h in hs]
    eg = [jnp.exp(gcs[h]) for h in hs]
    bv = [vs[h] * betas[h] for h in hs]
    kg = [kb[h] * eg[h] for h in hs]
    qk = [qkr[h] * decay[h] for h in hs]
    qd = [qs[h] * eg[h] for h in hs]
    ekd = [jnp.exp(gls[h] - gcs[h]) for h in hs]
    kd = [ks[h] * ekd[h] for h in hs]
    cd = [jnp.exp(gls[h]) for h in hs]
    loc = dict(decay=decay, kb=kb, kk=kk, eg=eg, bv=bv, kg=kg, qkr=qkr, qk=qk, qd=qd, ekd=ekd, kd=kd, cd=cd)
    if solve:
        tinv = _inv_unit_lower([jnp.where(strict, kk[h] * decay[h], 0.0) for h in hs])
        loc.update(tinv=tinv, u=[_dot(tinv[h], bv[h], hi=True) for h in hs], w=[_dot(tinv[h], kg[h], hi=True) for h in hs])
    return loc


def _b_state(loc, ids, s0s):
    n = range(len(ids))
    ws = [_dot(loc["w"][ids[j]], s0s[j]) for j in n]
    qs0 = [_dot(loc["qd"][ids[j]], s0s[j]) for j in n]
    vn = [loc["u"][ids[j]] - ws[j] for j in n]
    o = [qs0[j] + _dot(loc["qk"][ids[j]], vn[j]) for j in n]
    s1 = [s0s[j] * loc["cd"][ids[j]] + _dot(loc["kd"][ids[j]], vn[j], "tn") for j in n]
    return vn, o, s1


def _b_fwd(qkv, gates, proj, norm_g, beside=None, tb=256):
    t = qkv.shape[0]
    nch = tb // CHUNK

    def body(q_ref, k_ref, v_ref, ga_ref, z_ref, ng_ref, out_ref, st_ref, ti_ref, w_ref, vn_ref, o_ref, s_scr):
        @pl.when(pl.program_id(0) == 0)
        def _():
            s_scr[...] = jnp.zeros_like(s_scr)

        ltri = _tril(CHUNK).astype(F32)

        hs = range(B_HEADS)
        cols = [slice(h * B_DIM, (h + 1) * B_DIM) for h in hs]

        def pair(p, carry):
            cs = [2 * p, 2 * p + 1]
            rows = [pl.ds(pl.multiple_of(c * CHUNK, CHUNK), CHUNK) for c in cs]
            ga = [ga_ref[r, :] for r in rows]
            gcum = [_dot_exact_lhs(ltri, g) for g in ga]
            gcum_t = [g.T for g in gcum]
            items = [(i, h) for i in range(2) for h in hs]
            loc = _b_local([q_ref[rows[i], cols[h]] * GDN_SCALE for i, h in items], [k_ref[rows[i], cols[h]] for i, h in items],
                           [v_ref[rows[i], cols[h]] for i, h in items], [ga[i][:, h:h + 1] for i, h in items],
                           [gcum[i][:, B_HEADS + h:B_HEADS + h + 1] for i, h in items],
                           [gcum_t[i][B_HEADS + h:B_HEADS + h + 1, :] for i, h in items],
                           [gcum[i][CHUNK - 1:CHUNK, B_HEADS + h:B_HEADS + h + 1] for i, h in items])
            s0s = [s_scr[h] for h in hs]
            for i in range(2):
                ids = [i * B_HEADS + h for h in hs]
                for h in hs:
                    st_ref[cs[i], h] = s0s[h]
                    ti_ref[cs[i], h] = loc["tinv"][ids[h]]
                vn, o, s0s = _b_state(loc, ids, s0s)
                w_ref[rows[i], :] = jnp.concatenate([loc["w"][j] for j in ids], axis=1)
                vn_ref[rows[i], :] = jnp.concatenate(vn, axis=1)
                o_ref[rows[i], :] = jnp.concatenate(o, axis=1)
                outs = [_headnorm_fwd(o[h], ng_ref[...], z_ref[rows[i], cols[h]]) for h in hs]
                out_ref[rows[i], :] = jnp.concatenate(outs, axis=1).astype(BF16)
            for h in hs:
                s_scr[h] = s0s[h]
            return carry

        lax.fori_loop(0, nch // 2, pair, 0)

    part = lambda j: pl.BlockSpec((tb, B_WIDTH), lambda i, j=j: (i, j))
    wide = pl.BlockSpec((tb, B_WIDTH), lambda i: (i, 0))
    wide_shape = jax.ShapeDtypeStruct((t, B_WIDTH), F32)
    return _call_beside(
        body, beside, t // tb, (qkv, qkv, qkv, gates, proj, norm_g.reshape(1, B_DIM)), grid=(t // tb,),
        in_specs=[part(0), part(1), part(2), pl.BlockSpec((tb, 128), lambda i: (i, 0)),
                  pl.BlockSpec((tb, B_WIDTH), lambda i: (i, COL_B // B_WIDTH + 3)), pl.BlockSpec((1, B_DIM), lambda i: (0, 0))],
        out_specs=[wide, pl.BlockSpec((nch, B_HEADS, B_DIM, B_DIM), lambda i: (i, 0, 0, 0)),
                   pl.BlockSpec((nch, B_HEADS, CHUNK, CHUNK), lambda i: (i, 0, 0, 0)), wide, wide, wide],
        out_shape=[jax.ShapeDtypeStruct((t, B_WIDTH), BF16), jax.ShapeDtypeStruct((t // CHUNK, B_HEADS, B_DIM, B_DIM), F32),
                   jax.ShapeDtypeStruct((t // CHUNK, B_HEADS, CHUNK, CHUNK), F32), wide_shape, wide_shape, wide_shape],
        scratch_shapes=[pltpu.VMEM((B_HEADS, B_DIM, B_DIM), F32)], name="gdn_fwd")


def _b_bwd(qkv, gates, proj, norm_g, states, fwd_saved, dmixed, beside=None, tb=256):
    t = qkv.shape[0]
    nch = tb // CHUNK
    nb = t // tb

    def body(q_ref, k_ref, v_ref, ga_ref, z_ref, ng_ref, st_ref, ti_ref, w_ref, vn_ref, o_ref, dm0_ref, dm1_ref,
             dqkv_ref, dga_ref, dz_ref, dng_ref, ds_scr):
        @pl.when(pl.program_id(0) == 0)
        def _():
            ds_scr[...] = jnp.zeros_like(ds_scr)
            dng_ref[...] = jnp.zeros_like(dng_ref)

        ltri = _tril(CHUNK).astype(F32)
        strict = _tril(CHUNK, strict=True)
        lane = lax.broadcasted_iota(jnp.int32, (CHUNK, 128), 1)
        lane1 = lax.broadcasted_iota(jnp.int32, (1, 128), 1)

        nh = range(B_HEADS)
        cols = [slice(h * B_DIM, (h + 1) * B_DIM) for h in nh]
        rsum = lambda a: jnp.sum(a, axis=-1, keepdims=True)

        def pair(p, carry):
            cs = [nch - 1 - 2 * p, nch - 2 - 2 * p]
            crow = [pl.ds(pl.multiple_of(c * CHUNK, CHUNK), CHUNK) for c in cs]
            gas = [ga_ref[r, :] for r in crow]
            gcum = [_dot_exact_lhs(ltri, g) for g in gas]
            gcum_t = [g.T for g in gcum]
            items = [(i, h) for i in range(2) for h in nh]
            hs = range(len(items))
            q = [q_ref[crow[i], cols[h]] * GDN_SCALE for i, h in items]
            k = [k_ref[crow[i], cols[h]] for i, h in items]
            v = [v_ref[crow[i], cols[h]] for i, h in items]
            z = [z_ref[crow[i], cols[h]] for i, h in items]
            beta = [gas[i][:, h:h + 1] for i, h in items]
            s0 = [st_ref[cs[i], h] for i, h in items]
            r = _b_local(q, k, v, beta, [gcum[i][:, B_HEADS + h:B_HEADS + h + 1] for i, h in items],
                         [gcum_t[i][B_HEADS + h:B_HEADS + h + 1, :] for i, h in items],
                         [gcum[i][CHUNK - 1:CHUNK, B_HEADS + h:B_HEADS + h + 1] for i, h in items], solve=False)
            tinv = [ti_ref[cs[i], h] for i, h in items]
            w = [w_ref[crow[i], cols[h]] for i, h in items]
            vn = [vn_ref[crow[i], cols[h]] for i, h in items]
            decay, eg, qd, kd, kb, cd = (r[n] for n in ("decay", "eg", "qd", "kd", "kb", "cd"))
            dms = [(dm0_ref if h < 2 else dm1_ref)[crow[i], (h % 2) * B_DIM:(h % 2 + 1) * B_DIM].astype(F32) for i, h in items]
            hn = [_headnorm_bwd(dms[j], o_ref[crow[i], cols[h]], ng_ref[...], z[j]) for j, (i, h) in enumerate(items)]
            do = [hn[j][0] for j in hs]
            dvn_o = [_dot(r["qk"][j], do[j], "tn") for j in hs]
            dqk = [_dot(do[j], vn[j], "nt") for j in hs]
            dqd = [_dot(do[j], s0[j], "nt") for j in hs]
            ds_o = [_dot(qd[j], do[j], "tn") for j in hs]
            ds = [ds_scr[h] for h in nh]
            dvn, dkd, dcd = [None] * 8, [None] * 8, [None] * 8
            for i in range(2):
                for h in nh:
                    j = i * B_HEADS + h
                    dvn[j] = dvn_o[j] + _dot(kd[j], ds[h])
                    dkd[j] = _dot(vn[j], ds[h], "nt")
                    dcd[j] = jnp.sum(jnp.sum(s0[j] * ds[h], axis=0, keepdims=True), axis=1, keepdims=True)
                ds = [ds_o[i * B_HEADS + h] + ds[h] * cd[i * B_HEADS + h] - _dot(w[i * B_HEADS + h], dvn[i * B_HEADS + h], "tn")
                      for h in nh]
            for h in nh:
                ds_scr[h] = ds[h]
            dw = [-_dot(dvn[j], s0[j], "nt") for j in hs]
            dbv = [_dot(tinv[h], dvn[h], "tn", hi=True) for h in hs]
            dkg = [_dot(tinv[h], dw[h], "tn", hi=True) for h in hs]
            dt = [_dot(dvn[h], r["bv"][h], "nt", hi=True) + _dot(dw[h], r["kg"][h], "nt", hi=True) for h in hs]
            tdt = [_dot(tinv[h], dt[h], "tn", hi=True) for h in hs]
            da = [jnp.where(strict, -_dot(tdt[h], tinv[h], "nt", hi=True), 0.0) for h in hs]
            dm = [da[h] * decay[h] for h in hs]
            dn = [dqk[h] * decay[h] for h in hs]
            e = [(da[h] * r["kk"][h] + dqk[h] * r["qkr"][h]) * decay[h] for h in hs]
            dkb = [_dot(dm[h], k[h]) + dkg[h] * eg[h] for h in hs]
            dk = [_dot(dm[h], kb[h], "tn") + _dot(dn[h], q[h], "tn") + dkd[h] * r["ekd"][h] + dkb[h] * beta[h] for h in hs]
            dq = [_dot(dn[h], k[h]) + dqd[h] * eg[h] for h in hs]
            tkd = [rsum(dkd[h] * kd[h]) for h in hs]
            dgc = [rsum(e[h]) - rsum(e[h].T) + rsum(dqd[h] * qd[h]) - tkd[h] + rsum(dkg[h] * r["kg"][h]) for h in hs]
            dgl = [jnp.sum(tkd[h], axis=0, keepdims=True) + dcd[h] * cd[h] for h in hs]
            dbeta = [rsum(dbv[h] * v[h]) + rsum(dkb[h] * k[h]) for h in hs]
            for i in range(2):
                ids = [i * B_HEADS + h for h in nh]
                dbeta_m = sum(jnp.where(lane == h, dbeta[ids[h]], 0.0) for h in nh)
                dgc_m = sum(jnp.where(lane == B_HEADS + h, dgc[ids[h]], 0.0) for h in nh)
                dgl_m = sum(jnp.where(lane1 == B_HEADS + h, dgl[ids[h]], 0.0) for h in nh)
                dqkv_ref[crow[i], :] = jnp.concatenate(
                    [dq[j] * GDN_SCALE for j in ids] + [dk[j] for j in ids] + [dbv[j] * beta[j] for j in ids], axis=1)
                dz_ref[crow[i], :] = jnp.concatenate([hn[j][1] for j in ids], axis=1).astype(BF16)
                dga_ref[crow[i], :] = dbeta_m + _dot_exact_lhs(ltri, dgc_m, "tn") + dgl_m
            dng_ref[...] += sum(hn[j][2] for j in hs)
            return carry

        lax.fori_loop(0, nch // 2, pair, 0)

    part = lambda j: pl.BlockSpec((tb, B_WIDTH), lambda i, j=j: (nb - 1 - i, j))
    rowblk = lambda w, j=0: pl.BlockSpec((tb, w), lambda i, j=j: (nb - 1 - i, j))
    return _call_beside(
        body, beside, nb, (qkv, qkv, qkv, gates, proj, norm_g.reshape(1, B_DIM), states, *fwd_saved, dmixed, dmixed), grid=(nb,),
        in_specs=[part(0), part(1), part(2), rowblk(128), rowblk(B_WIDTH, COL_B // B_WIDTH + 3),
                  pl.BlockSpec((1, B_DIM), lambda i: (0, 0)),
                  pl.BlockSpec((nch, B_HEADS, B_DIM, B_DIM), lambda i: (nb - 1 - i, 0, 0, 0)),
                  pl.BlockSpec((nch, B_HEADS, CHUNK, CHUNK), lambda i: (nb - 1 - i, 0, 0, 0)),
                  rowblk(B_WIDTH), rowblk(B_WIDTH), rowblk(B_WIDTH), rowblk(256, 1), rowblk(256, 2)],
        out_specs=[rowblk(3 * B_WIDTH), rowblk(128), rowblk(B_WIDTH), pl.BlockSpec((1, B_DIM), lambda i: (0, 0))],
        out_shape=[jax.ShapeDtypeStruct((t, 3 * B_WIDTH), F32), jax.ShapeDtypeStruct((t, 128), F32),
                   jax.ShapeDtypeStruct((t, B_WIDTH), BF16), jax.ShapeDtypeStruct((1, B_DIM), F32)],
        scratch_shapes=[pltpu.VMEM((B_HEADS, B_DIM, B_DIM), F32)], name="gdn_bwd")


def _b_pre_bwd(proj, conv_w, alog_row, dtb_row, dqkv, dgates, tb=512):
    t = proj.shape[0]
    nb = t // tb
    cb0 = COL_B // B_WIDTH

    def body(q_ref, k_ref, v_ref, qp_ref, kp_ref, vp_ref, w_ref, gi_ref, al_ref, dt_ref, dqkv_ref, dga_ref,
             dy_ref, dgi_ref, dw_ref, dal_ref, ddt_ref, nxt_scr):
        step_id = pl.program_id(0)
        first = step_id == nb - 1

        @pl.when(step_id == 0)
        def _():
            dw_ref[...] = jnp.zeros_like(dw_ref)
            dal_ref[...] = jnp.zeros_like(dal_ref)
            ddt_ref[...] = jnp.zeros_like(ddt_ref)

        for part, (c_ref, p_ref) in enumerate(((q_ref, qp_ref), (k_ref, kp_ref), (v_ref, vp_ref))):
            cols = slice(part * B_WIDTH, (part + 1) * B_WIDTH)
            cur = c_ref[...]
            prev = jnp.where(first, 0.0, p_ref[...])
            w = w_ref[:, cols]
            shifted = [_shift_rows(cur, prev, 3 - j, down=True) for j in range(3)] + [cur]
            y = shifted[0] * w[0:1] + shifted[1] * w[1:2] + shifted[2] * w[2:3] + shifted[3] * w[3:4]
            sg = _sigmoid(y)
            s = y * sg
            dsn = dqkv_ref[:, cols]
            if part < 2:
                outs = []
                for h in range(B_HEADS):
                    hc = slice(h * B_DIM, (h + 1) * B_DIM)
                    sh, dh = s[:, hc], dsn[:, hc]
                    rq = lax.rsqrt(jnp.sum(sh * sh, axis=-1, keepdims=True) + EPS)
                    nh = sh * rq
                    outs.append(rq * (dh - nh * jnp.sum(dh * nh, axis=-1, keepdims=True)))
                dsn = jnp.concatenate(outs, axis=1)
            dy = dsn * (sg * (1.0 + y * (1.0 - sg)))
            dy_ref[:, cols] = _conv_bwd_rows(dy, jnp.where(step_id == 0, 0.0, nxt_scr[:, cols]), w).astype(BF16)
            nxt_scr[:, cols] = dy[0:8]
            dw_ref[:, cols] += jnp.concatenate([jnp.sum(shifted[j] * dy, axis=0, keepdims=True) for j in range(CONV_K)], axis=0)
        g = gi_ref[...]
        dga = dga_ref[...]
        is_b, is_a = _gate_lane_masks(g.shape)
        beta = _sigmoid(g)
        pre = g + dt_ref[...]
        ea = jnp.exp(al_ref[...])
        la = -ea * _softplus(pre)
        dpre = jnp.where(is_a, dga * (-ea) * _sigmoid(pre), 0.0)
        dgi_ref[...] = jnp.where(is_b, dga * beta * (1.0 - beta), dpre).astype(BF16)
        dal_ref[...] += jnp.sum(jnp.where(is_a, dga * la, 0.0), axis=0, keepdims=True)
        ddt_ref[...] += jnp.sum(dpre, axis=0, keepdims=True)

    cur = lambda j: pl.BlockSpec((tb, B_WIDTH), lambda i, j=j: (nb - 1 - i, cb0 + j))
    prv = lambda j: pl.BlockSpec((8, B_WIDTH), lambda i, j=j: (jnp.maximum((nb - 1 - i) * (tb // 8) - 1, 0), cb0 + j))
    vec = pl.BlockSpec((1, 128), lambda i: (0, 0))
    wspec = pl.BlockSpec((CONV_K, 3 * B_WIDTH), lambda i: (0, 0))
    rowblk = lambda width, j=0: pl.BlockSpec((tb, width), lambda i, j=j: (nb - 1 - i, j))
    return pl.pallas_call(
        body, grid=(nb,),
        in_specs=[cur(0), cur(1), cur(2), prv(0), prv(1), prv(2), wspec, rowblk(128, COL_G // 128), vec, vec,
                  rowblk(3 * B_WIDTH), rowblk(128)],
        out_specs=[rowblk(3 * B_WIDTH), rowblk(128), wspec, vec, vec],
        out_shape=[jax.ShapeDtypeStruct((t, 3 * B_WIDTH), BF16), jax.ShapeDtypeStruct((t, 128), BF16),
                   jax.ShapeDtypeStruct((CONV_K, 3 * B_WIDTH), F32), jax.ShapeDtypeStruct((1, 128), F32), jax.ShapeDtypeStruct((1, 128), F32)],
        scratch_shapes=[pltpu.VMEM((8, 3 * B_WIDTH), F32)],
        compiler_params=_cparams(("arbitrary",)), name="gdn_pre_bwd")(
            proj, proj, proj, proj, proj, proj, conv_w, proj, alog_row, dtb_row, dqkv, dgates)


def _c_gates(xc, wa_ref, ba_ref, wx_ref, bx_ref, lam_ref, is_row0):
    r = _sigmoid(_dot(xc, wa_ref[...]) + ba_ref[...])
    i = _sigmoid(_dot(xc, wx_ref[...]) + bx_ref[...])
    sp = _softplus(-lam_ref[...])
    log_a = -RG_C * r * sp
    a = jnp.exp(log_a)
    m2 = _neg_expm1(2.0 * log_a)
    mult = jnp.where(is_row0, 1.0, jnp.sqrt(jnp.maximum(m2, EPS)))
    return r, i, sp, log_a, a, m2, mult


def _row0_mask(tb, first):
    ridx = lax.broadcasted_iota(jnp.int32, (tb, C_WIDTH), 0)
    return (ridx == 0) & first


def _c_fwd(proj, conv_w, conv_b, wa, ba, wx, bx, lam, tb=512):
    t = proj.shape[0]
    cbx = COL_C // C_WIDTH

    def body(x_ref, xp_ref, y_ref, w_ref, cb_ref, wa_ref, ba_ref, wx_ref, bx_ref, lam_ref, out_ref, h_ref, a_scr, b_scr, h_scr):
        first = pl.program_id(0) == 0

        @pl.when(first)
        def _():
            h_scr[...] = jnp.zeros_like(h_scr)

        prev = jnp.where(first, 0.0, xp_ref[...])
        xc = _conv_fwd(x_ref[...], prev, w_ref[...]) + cb_ref[...]
        _, i, _, _, a, _, mult = _c_gates(xc, wa_ref, ba_ref, wx_ref, bx_ref, lam_ref, _row0_mask(tb, first))
        ta, tb_ = _tile_scan(a, mult * i * xc)
        a_scr[...] = ta
        b_scr[...] = tb_

        def step(blk, h):
            rows = pl.ds(pl.multiple_of(blk * 8, 8), 8)
            h_ref[rows, :] = jnp.broadcast_to(h, (8, C_WIDTH))
            return a_scr[rows, :][7:8] * h + b_scr[rows, :][7:8]

        h_scr[...] = lax.fori_loop(0, tb // 8, step, h_scr[...], unroll=8)
        hs = ta * h_ref[...] + tb_
        h_ref[...] = hs
        gl, _ = _gelu_tanh(y_ref[...])
        out_ref[...] = (gl * hs).astype(BF16)

    vec = pl.BlockSpec((1, C_WIDTH), lambda i: (0, 0))
    mat = pl.BlockSpec((C_WIDTH, C_WIDTH), lambda i: (0, 0))
    row = pl.BlockSpec((tb, C_WIDTH), lambda i: (i, 0))
    return pl.pallas_call(
        body, grid=(t // tb,),
        in_specs=[pl.BlockSpec((tb, C_WIDTH), lambda i: (i, cbx)),
                  pl.BlockSpec((8, C_WIDTH), lambda i: (jnp.maximum(i * (tb // 8) - 1, 0), cbx)),
                  pl.BlockSpec((tb, C_WIDTH), lambda i: (i, cbx + 1)),
                  pl.BlockSpec((CONV_K, C_WIDTH), lambda i: (0, 0)), vec, mat, vec, mat, vec, vec],
        out_specs=[row, row],
        out_shape=[jax.ShapeDtypeStruct((t, C_WIDTH), BF16), jax.ShapeDtypeStruct((t, C_WIDTH), F32)],
        scratch_shapes=[pltpu.VMEM((tb, C_WIDTH), F32), pltpu.VMEM((tb, C_WIDTH), F32), pltpu.VMEM((1, C_WIDTH), F32)],
        compiler_params=_cparams(("arbitrary",)), name="lru_fwd")(proj, proj, proj, conv_w, conv_b, wa, ba, wx, bx, lam)


def _c_bwd(proj, conv_w, conv_b, wa, ba, wx, bx, lam, hs, dmixed, tb=512):
    t = proj.shape[0]
    nb = t // tb
    cbx = COL_C // C_WIDTH

    def body(x_ref, xp_ref, y_ref, w_ref, cb_ref, wa_ref, ba_ref, wx_ref, bx_ref, lam_ref, h_ref, hp_ref, dm_ref,
             dxc_ref, dyg_ref, dw_ref, dcb_ref, dwa_ref, dba_ref, dwx_ref, dbx_ref, dlam_ref, g_scr, a_scr, cin_scr, c_scr, nxt_scr):
        step_id = pl.program_id(0)
        first = step_id == nb - 1

        @pl.when(step_id == 0)
        def _():
            c_scr[...] = jnp.zeros_like(c_scr)
            for ref in (dw_ref, dcb_ref, dwa_ref, dba_ref, dwx_ref, dbx_ref, dlam_ref):
                ref[...] = jnp.zeros_like(ref)

        cur = x_ref[...]
        prev = jnp.where(first, 0.0, xp_ref[...])
        w = w_ref[...]
        shifted = [_shift_rows(cur, prev, 3 - j, down=True) for j in range(3)] + [cur]
        xc = shifted[0] * w[0:1] + shifted[1] * w[1:2] + shifted[2] * w[2:3] + shifted[3] * w[3:4] + cb_ref[...]
        row0 = _row0_mask(tb, first)
        r, i, sp, log_a, a, m2, mult = _c_gates(xc, wa_ref, ba_ref, wx_ref, bx_ref, lam_ref, row0)
        h = h_ref[...]
        hprev = _shift_rows(h, jnp.where(first, 0.0, hp_ref[...]), 1, down=True)
        gl, dgl = _gelu_tanh(y_ref[...])
        dm = dm_ref[...].astype(F32)
        dyg_ref[...] = (dm * h * dgl).astype(BF16)
        dout = dm * gl
        ta, te = _tile_scan(a, a * dout, reverse=True)
        a_scr[...] = ta
        g_scr[...] = te

        def step(blk, carry):
            rows = pl.ds(pl.multiple_of((tb // 8 - 1 - blk) * 8, 8), 8)
            cin_scr[rows, :] = jnp.broadcast_to(carry, (8, C_WIDTH))
            return a_scr[rows, :][0:1] * carry + g_scr[rows, :][0:1]

        c_scr[...] = lax.fori_loop(0, tb // 8, step, c_scr[...], unroll=8)
        cin = cin_scr[...]
        cout = ta * cin + te
        last_in_tile = lax.broadcasted_iota(jnp.int32, (tb, C_WIDTH), 0) % 8 == 7
        dbx = dout + jnp.where(last_in_tile, cin, pltpu.roll(cout, tb - 1, 0))
        da = dbx * hprev
        dmult = jnp.where(row0, 0.0, dbx * i * xc)
        di = dbx * mult * xc
        dxc = dbx * mult * i
        dm2 = jnp.where(m2 > EPS, dmult * 0.5 / mult, 0.0)
        dlog_a = da * a - 2.0 * a * a * dm2
        dr = dlog_a * (-RG_C) * sp
        dlam_ref[...] += jnp.sum(dlog_a * (-RG_C) * r, axis=0, keepdims=True) * (-_sigmoid(-lam_ref[...]))
        dpa = dr * r * (1.0 - r)
        dpx = di * i * (1.0 - i)
        dba_ref[...] += jnp.sum(dpa, axis=0, keepdims=True)
        dbx_ref[...] += jnp.sum(dpx, axis=0, keepdims=True)
        dwa_ref[...] += _dot(xc, dpa, "tn")
        dwx_ref[...] += _dot(xc, dpx, "tn")
        dxc = dxc + _dot(dpa, wa_ref[...], "nt") + _dot(dpx, wx_ref[...], "nt")
        dxc_ref[...] = _conv_bwd_rows(dxc, jnp.where(step_id == 0, 0.0, nxt_scr[...]), w).astype(BF16)
        nxt_scr[...] = dxc[0:8]
        dcb_ref[...] += jnp.sum(dxc, axis=0, keepdims=True)
        dw_ref[...] += jnp.concatenate([jnp.sum(shifted[j] * dxc, axis=0, keepdims=True) for j in range(CONV_K)], axis=0)

    vec = pl.BlockSpec((1, C_WIDTH), lambda i: (0, 0))
    mat = pl.BlockSpec((C_WIDTH, C_WIDTH), lambda i: (0, 0))
    cw = pl.BlockSpec((CONV_K, C_WIDTH), lambda i: (0, 0))
    row = lambda j=0: pl.BlockSpec((tb, C_WIDTH), lambda i, j=j: (nb - 1 - i, j))
    halo = lambda j=0: pl.BlockSpec((8, C_WIDTH), lambda i, j=j: (jnp.maximum((nb - 1 - i) * (tb // 8) - 1, 0), j))
    return pl.pallas_call(
        body, grid=(nb,),
        in_specs=[row(cbx), halo(cbx), row(cbx + 1), cw, vec, mat, vec, mat, vec, vec, row(), halo(), row(3)],
        out_specs=[row(), row(), cw, vec, mat, vec, mat, vec, vec],
        out_shape=[jax.ShapeDtypeStruct((t, C_WIDTH), BF16), jax.ShapeDtypeStruct((t, C_WIDTH), BF16),
                   jax.ShapeDtypeStruct((CONV_K, C_WIDTH), F32), jax.ShapeDtypeStruct((1, C_WIDTH), F32),
                   jax.ShapeDtypeStruct((C_WIDTH, C_WIDTH), F32), jax.ShapeDtypeStruct((1, C_WIDTH), F32),
                   jax.ShapeDtypeStruct((C_WIDTH, C_WIDTH), F32), jax.ShapeDtypeStruct((1, C_WIDTH), F32),
                   jax.ShapeDtypeStruct((1, C_WIDTH), F32)],
        scratch_shapes=[pltpu.VMEM((tb, C_WIDTH), F32), pltpu.VMEM((tb, C_WIDTH), F32), pltpu.VMEM((tb, C_WIDTH), F32),
                        pltpu.VMEM((1, C_WIDTH), F32), pltpu.VMEM((8, C_WIDTH), F32)],
        compiler_params=_cparams(("arbitrary",)), name="lru_bwd")(
            proj, proj, proj, conv_w, conv_b, wa, ba, wx, bx, lam, hs, hs, dmixed)


def _mesh_pos():
    return lax.axis_index("x"), lax.axis_index("y"), lax.axis_index("c")


class _Exchange:
    def __init__(self, arrays, layouts):
        self.arrays, self.layouts = list(arrays), list(layouts)
        self.out_shapes = []
        for a, lay in zip(self.arrays, self.layouts):
            if lay == 'a2a':
                shp = a.shape
            elif lay == 'slot':
                shp = (N_DEV,) + a.shape
            elif lay == 'rows':
                shp = (N_DEV * a.shape[0], a.shape[1])
            else:
                shp = (a.shape[0], N_DEV * a.shape[1])
            self.out_shapes.append(jax.ShapeDtypeStruct(shp, a.dtype))
        n = len(self.arrays)
        self.scratch = [pltpu.SemaphoreType.DMA((7 * n,)), pltpu.SemaphoreType.DMA((7 * n,)), pltpu.SemaphoreType.DMA((n,))]

    def _landing(self, a, dst_ref, idx):
        lay, shape = self.layouts[a], self.arrays[a].shape
        if lay in ('a2a', 'slot'):
            return dst_ref.at[idx]
        if lay == 'rows':
            return dst_ref.at[pl.ds(pl.multiple_of(idx * shape[0], shape[0]), shape[0]), :]
        return dst_ref.at[:, pl.ds(pl.multiple_of(idx * shape[1], shape[1]), shape[1])]

    def copies(self, src_refs, dst_refs, send_sems, recv_sems, local_sems):
        mx, my, mc = _mesh_pos()
        me = 4 * mx + 2 * my + mc
        out = []
        for a, (src, dst) in enumerate(zip(src_refs, dst_refs)):
            a2a = self.layouts[a] == 'a2a'
            out.append(pltpu.make_async_copy(src.at[me] if a2a else src, self._landing(a, dst, me), local_sems.at[a]))
            for k in range(1, N_DEV):
                px = 1 - mx if k & 4 else mx
                py = 1 - my if k & 2 else my
                pc = 1 - mc if k & 1 else mc
                out.append(pltpu.make_async_remote_copy(
                    src_ref=src.at[4 * px + 2 * py + pc] if a2a else src, dst_ref=self._landing(a, dst, me),
                    send_sem=send_sems.at[7 * a + k - 1], recv_sem=recv_sems.at[7 * a + k - 1],
                    device_id=(px, py, pc), device_id_type=MESH))
        return out


_ANY = pl.BlockSpec(memory_space=pl.ANY)


def _run_exchange(ex, name):
    n = len(ex.arrays)

    def body(*refs):
        cps = ex.copies(refs[:n], refs[n:2 * n], *refs[2 * n:])
        for cp in cps:
            cp.start()
        for cp in cps:
            cp.wait()

    return pl.pallas_call(body, out_shape=ex.out_shapes, in_specs=[_ANY] * n, out_specs=[_ANY] * n,
                          scratch_shapes=ex.scratch, name=name)(*ex.arrays)


def _call_beside(body, ex, nsteps, args, *, grid, in_specs, out_specs, out_shape, scratch_shapes, name):
    if ex is None:
        outs = pl.pallas_call(body, grid=grid, in_specs=in_specs, out_specs=out_specs, out_shape=out_shape,
                              scratch_shapes=scratch_shapes, compiler_params=_cparams(("arbitrary",)), name=name)(*args)
        return outs, None
    n_in, n_out, n_scr, n = len(in_specs), len(out_specs), len(scratch_shapes), len(ex.arrays)

    def wrapped(*refs):
        ins, refs = refs[:n_in], refs[n_in:]
        ex_ins, refs = refs[:n], refs[n:]
        outs, refs = refs[:n_out], refs[n_out:]
        ex_outs, refs = refs[:n], refs[n:]
        scr, sems = refs[:n_scr], refs[n_scr:]
        step = pl.program_id(0)

        @pl.when(step == 0)
        def _():
            for cp in ex.copies(ex_ins, ex_outs, *sems):
                cp.start()

        body(*ins, *outs, *scr)

        @pl.when(step == nsteps - 1)
        def _():
            for cp in ex.copies(ex_ins, ex_outs, *sems):
                cp.wait()

    res = pl.pallas_call(
        wrapped, grid=grid, in_specs=list(in_specs) + [_ANY] * n, out_specs=list(out_specs) + [_ANY] * n,
        out_shape=list(out_shape) + ex.out_shapes, scratch_shapes=list(scratch_shapes) + ex.scratch,
        compiler_params=_cparams(("arbitrary",)), name=name)(*args, *ex.arrays)
    return res[:n_out], res[n_out:]


def _adamw_math(w, g, m, v):
    m = ADAM_B1 * m + (1.0 - ADAM_B1) * g
    v = ADAM_B2 * v + (1.0 - ADAM_B2) * (g * g)
    m_hat = m / (1.0 - ADAM_B1 ** ADAM_STEP)
    v_hat = v / (1.0 - ADAM_B2 ** ADAM_STEP)
    delta = -ADAM_LR * (m_hat / (jnp.sqrt(v_hat) + ADAM_EPS) + ADAM_WD * w)
    return delta, m, v


def _sum_adamw(parts, w, m, v, tr, name):
    p, r, c = parts.shape
    tr = min(tr, r)
    assert r % tr == 0

    def body(p_ref, w_ref, m_ref, v_ref, g_ref, d_ref, nm_ref, nv_ref):
        g = p_ref[0].astype(F32)
        for j in range(1, p):
            g = g + p_ref[j].astype(F32)
        delta, nm, nv = _adamw_math(w_ref[...], g, m_ref[...], v_ref[...])
        g_ref[...] = g
        d_ref[...] = delta
        nm_ref[...] = nm
        nv_ref[...] = nv

    row = pl.BlockSpec((tr, c), lambda i: (i, 0))
    return pl.pallas_call(
        body, grid=(r // tr,), in_specs=[pl.BlockSpec((p, tr, c), lambda i: (0, i, 0)), row, row, row],
        out_specs=[row] * 4, out_shape=[jax.ShapeDtypeStruct((r, c), F32)] * 4,
        compiler_params=_cparams(("parallel",)), name=name)(parts, w, m, v)


def _sum_parts(parts, name):
    p, r, c = parts.shape

    def body(p_ref, o_ref):
        g = p_ref[0]
        for j in range(1, p):
            g = g + p_ref[j]
        o_ref[...] = g

    return pl.pallas_call(body, out_shape=jax.ShapeDtypeStruct((r, c), F32), name=name)(parts)


def _rows_of(shape):
    n = 1
    for d in shape:
        n *= d
    return n, -(-n // 128)


def _pack(arrs):
    blocks = []
    for a in arrs:
        n, nr = _rows_of(a.shape)
        blocks.append(jnp.pad(a.reshape(-1).astype(F32), (0, nr * 128 - n)).reshape(nr, 128))
    rows = sum(b.shape[0] for b in blocks)
    if rows % 8:
        blocks.append(jnp.zeros((8 - rows % 8, 128), F32))
    return jnp.concatenate(blocks, axis=0)


def _unpack(buf, shapes):
    out, r0 = [], 0
    for s in shapes:
        n, nr = _rows_of(s)
        out.append(buf[r0:r0 + nr].reshape(-1)[:n].reshape(s))
        r0 += nr
    return out


def _block_diag(w):
    rows = [jnp.pad(w[i], ((0, 0), (i * C_BLOCK_DIM, C_WIDTH - (i + 1) * C_BLOCK_DIM))) for i in range(C_BLOCKS)]
    return jnp.concatenate(rows, axis=0)


def _diag_blocks(m):
    m4 = m.reshape(C_BLOCKS, C_BLOCK_DIM, C_BLOCKS, C_BLOCK_DIM)
    return jnp.stack([m4[i, :, i, :] for i in range(C_BLOCKS)])


def _gate_row(v):
    return jnp.pad(v.astype(F32), (B_HEADS, 128 - 2 * B_HEADS)).reshape(1, 128)


def _permute_w_in(w):
    pad = jnp.zeros(w.shape[:-1] + (D_IN_PAD - D_IN,), w.dtype)
    return jnp.concatenate([w[..., :3072], w[..., 3080:3592], w[..., 3072:3080], pad], axis=-1)


def _unpermute_w_in(w):
    return jnp.concatenate([w[..., :3072], w[..., COL_G:COL_G + 8], w[..., 3072:COL_G]], axis=-1)


_WEIGHTS = ['norm1_g', 'w_in', 'hgrn_lb_logits', 'hgrn_norm_g', 'gdn_conv_w', 'gdn_a_log', 'gdn_dt_bias', 'gdn_norm_g',
            'lru_conv_w', 'lru_conv_b', 'lru_w_a', 'lru_b_a', 'lru_w_x', 'lru_b_x', 'lru_lambda', 'w_out', 'norm2_g',
            'w_up', 'w_down', 'final_norm_g']
_BIG = ('w_in', 'w_out', 'w_up', 'w_down')
_SHARDED_SMALL = ('gdn_conv_w', 'lru_conv_w')


def _step(x, target, w, m, v):
    t = x.shape[0]
    mx, my, mc = _mesh_pos()
    me = 4 * mx + 2 * my + mc

    bf = lambda a: a.astype(BF16)

    def full_w_in(g):
        return _permute_w_in(jnp.moveaxis(g, 0, 1).reshape(D_MODEL, D_IN))

    conv_shapes = [w['gdn_conv_w'].shape, w['lru_conv_w'].shape]
    g_in, g_conv = _run_exchange(
        _Exchange([bf(w['w_in'][0]), _pack([w['gdn_conv_w'], w['lru_conv_w']])], ['slot', 'slot']), "gather_first")
    w_in = [full_w_in(g_in)]
    w_out, w_up, w_down = [], [], []
    gdn_cw, lru_cw = [], []
    for j in range(N_DEV):
        a, b = _unpack(g_conv[j], conv_shapes)
        gdn_cw.append(a)
        lru_cw.append(b)
    gdn_cw = jnp.concatenate(gdn_cw, axis=-1)
    lru_cw = jnp.concatenate(lru_cw, axis=-1)

    lbnd = _lb_fwd(w['hgrn_lb_logits'])
    row = lambda a: a.reshape(1, -1)

    def c_args(l):
        return (lru_cw[l], row(w['lru_conv_b'][l]), _block_diag(w['lru_w_a'][l]), row(w['lru_b_a'][l]),
                _block_diag(w['lru_w_x'][l]), row(w['lru_b_x'][l]), row(w['lru_lambda'][l]))

    saved = []
    xl = x
    h = _rms_fwd(x, w['norm1_g'][0], name="rms_fwd")
    for l in range(DEPTH):
        proj = _mm_rows(h, w_in[l], "nn", 256, "mm_proj")
        mix_a, st_a = _a_fwd(proj, lbnd[l], w['hgrn_norm_g'][l])
        alr, dtr = _gate_row(w['gdn_a_log'][l]), _gate_row(w['gdn_dt_bias'][l])
        qkv, gates = _b_pre_fwd(proj, gdn_cw[l], alr, dtr)
        nxt = [bf(w['w_in'][l + 1])] if l + 1 < DEPTH else []
        gather = _Exchange([bf(w['w_out'][l]), bf(w['w_up'][l]), bf(w['w_down'][l])] + nxt, ['rows', 'cols', 'rows'] + ['slot'] * len(nxt))
        (mix_b, st_b, *b_saved), got = _b_fwd(qkv, gates, proj, w['gdn_norm_g'][l], beside=gather)
        w_out.append(got[0])
        w_up.append(got[1])
        w_down.append(got[2])
        if nxt:
            w_in.append(full_w_in(got[3]))
        mix_c, hs = _c_fwd(proj, *c_args(l))
        mixed = [mix_a, mix_b, mix_c]
        x_mid, h2 = _mm_rows(mixed, w_out[l], "nn", 1024, "mm_out", residual=xl, epilogue="rms_fwd", norm=w['norm2_g'][l])
        act, up = _mm_rows(h2, w_up[l], "nn", 256, "mm_up", epilogue="relu2")
        saved.append(dict(x=xl, h=h, proj=proj, st_a=st_a, qkv=qkv, gates=gates, st_b=st_b, b_saved=b_saved, hs=hs, mixed=mixed,
                          x_mid=x_mid, h2=h2, up=up, act=act, alr=alr, dtr=dtr))
        if l + 1 < DEPTH:
            xl, h = _mm_rows(act, w_down[l], "nn", 512, "mm_down", residual=x_mid, epilogue="rms_fwd", norm=w['norm1_g'][l + 1])
        else:
            xl = _mm_rows(act, w_down[l], "nn", 512, "mm_down_last", residual=x_mid)
    loss, dx, dgf = _loss_head(xl, w['final_norm_g'], target)

    gs = {n: [None] * DEPTH for n in _WEIGHTS}
    recv = {n: [None] * DEPTH for n in _BIG}
    dw_in_above = None
    for l in reversed(range(DEPTH)):
        s = saved[l]
        dup = _mm_rows(dx, w_down[l], "nt", 256, "mm_dact", epilogue="drelu2", up=s['up'])
        dw_down = _mm_tn(s['act'], dx, 512, "mm_dw_down").reshape(N_DEV, D_FF // N_DEV, D_MODEL)
        dx_mid, dg2 = _mm_rows(dup, w_up[l], "nt", 512, "mm_dh2", epilogue="rms_bwd", norm=(s['x_mid'], w['norm2_g'][l], dx))
        dw_up = _mm_tn(s['h2'], dup, 512, "mm_dw_up", slab=D_FF // N_DEV)
        gs['norm2_g'][l] = dg2[0]
        dmixed = _mm_rows(dx_mid, w_out[l], "nt", 1024, "mm_dmixed")
        dw_out = _mm_tn(s['mixed'], dx_mid, 1024, "mm_dw_out").reshape(N_DEV, D_MODEL // N_DEV, D_MODEL)
        proj = s['proj']
        above = [dw_in_above] if dw_in_above is not None else []
        (dpa, dlb, dnga), got = _a_bwd(proj, lbnd[l], w['hgrn_norm_g'][l], s['st_a'], dmixed,
                                       beside=_Exchange([dw_out] + above, ['a2a'] * (1 + len(above))))
        recv['w_out'][l] = got[0]
        if above:
            recv['w_in'][l + 1] = got[1]
        gs['hgrn_lb_logits'][l] = dlb[0]
        gs['hgrn_norm_g'][l] = dnga[0, :A_DIM] + dnga[0, A_DIM:]
        (dqkv, dgates, dz, dngb), got = _b_bwd(s['qkv'], s['gates'], proj, w['gdn_norm_g'][l], s['st_b'], s['b_saved'], dmixed,
                                               beside=_Exchange([dw_up, dw_down], ['a2a', 'a2a']))
        recv['w_up'][l], recv['w_down'][l] = got
        dxb, dgi, dcwb, dal, ddt = _b_pre_bwd(proj, gdn_cw[l], s['alr'], s['dtr'], dqkv, dgates)
        gs['gdn_norm_g'][l] = dngb[0]
        gs['gdn_conv_w'][l] = dcwb
        gs['gdn_a_log'][l] = dal[0, B_HEADS:2 * B_HEADS]
        gs['gdn_dt_bias'][l] = ddt[0, B_HEADS:2 * B_HEADS]
        dxc_in, dyg, dcwc, dcb, dwa, dba, dwx, dbx, dlam = _c_bwd(proj, *c_args(l), s['hs'], dmixed)
        gs['lru_conv_w'][l] = dcwc
        gs['lru_conv_b'][l] = dcb[0]
        gs['lru_w_a'][l] = _diag_blocks(dwa)
        gs['lru_b_a'][l] = dba[0]
        gs['lru_w_x'][l] = _diag_blocks(dwx)
        gs['lru_b_x'][l] = dbx[0]
        gs['lru_lambda'][l] = dlam[0]
        dproj = [dpa, dxb, dz, dxc_in, dyg, dgi]
        dx, dg1 = _mm_rows(dproj, w_in[l], "nt", 512, "mm_dh", epilogue="rms_bwd", norm=(s['x'], w['norm1_g'][l], dx_mid))
        dw_in = _unpermute_w_in(_mm_tn(s['h'], dproj, 512, "mm_dw_in"))
        dw_in_above = jnp.moveaxis(dw_in.reshape(D_MODEL, N_DEV, D_IN // N_DEV), 1, 0)
        gs['norm1_g'][l] = dg1[0]
    grad_x = dx
    part = {n: jnp.stack(gs[n]) for n in _WEIGHTS if n != 'final_norm_g' and n not in _BIG}
    part['final_norm_g'] = dgf[0]
    part['hgrn_lb_logits'] = _lb_bwd(w['hgrn_lb_logits'], part['hgrn_lb_logits'])

    small = [n for n in _WEIGHTS if n not in _BIG]
    packed = _pack([part[n] for n in small] + [loss])
    recv['w_in'][0], all_small = _run_exchange(_Exchange([dw_in_above, packed], ['a2a', 'slot']), "exchange_last")

    grads, deltas, new_m, new_v = {}, {}, {}, {}
    for n in _BIG:
        shp = w[n].shape
        r2 = lambda a: a.reshape(-1, shp[-1])
        parts = jnp.stack(recv[n], axis=1).reshape(N_DEV, -1, shp[-1])
        g, d, nm, nv = _sum_adamw(parts, r2(w[n]), r2(m[n]), r2(v[n]), 256, "adamw_" + n)
        grads[n], deltas[n], new_m[n], new_v[n] = (a.reshape(shp) for a in (g, d, nm, nv))

    total = _sum_parts(all_small, "sum_small")
    summed = _unpack(total, [part[n].shape for n in small] + [(1, 1)])
    loss_total = summed[-1].reshape(())
    gsmall = dict(zip(small, summed[:-1]))
    for n in _SHARDED_SMALL:
        width = w[n].shape[-1]
        gsmall[n] = lax.dynamic_slice_in_dim(gsmall[n], me * width, width, axis=2)
    pk = lambda d: _pack([d[n] for n in small])
    _, d, nm, nv = _sum_adamw(pk(gsmall)[None], pk(w), pk(m), pk(v), 4096, "adamw_small")
    shapes = [w[n].shape for n in small]
    for n, dd, mm, vv in zip(small, _unpack(d, shapes), _unpack(nm, shapes), _unpack(nv, shapes)):
        grads[n], deltas[n], new_m[n], new_v[n] = gsmall[n], dd, mm, vv
    return loss_total, grad_x, grads, deltas, new_m, new_v


def kernel(x, norm1_g, w_in, hgrn_lb_logits, hgrn_norm_g, gdn_conv_w, gdn_a_log, gdn_dt_bias, gdn_norm_g, lru_conv_w, lru_conv_b, lru_w_a, lru_b_a, lru_w_x, lru_b_x, lru_lambda, w_out, norm2_g, w_up, w_down, final_norm_g, loss_target, m_norm1_g, m_w_in, m_hgrn_lb_logits, m_hgrn_norm_g, m_gdn_conv_w, m_gdn_a_log, m_gdn_dt_bias, m_gdn_norm_g, m_lru_conv_w, m_lru_conv_b, m_lru_w_a, m_lru_b_a, m_lru_w_x, m_lru_b_x, m_lru_lambda, m_w_out, m_norm2_g, m_w_up, m_w_down, m_final_norm_g, v_norm1_g, v_w_in, v_hgrn_lb_logits, v_hgrn_norm_g, v_gdn_conv_w, v_gdn_a_log, v_gdn_dt_bias, v_gdn_norm_g, v_lru_conv_w, v_lru_conv_b, v_lru_w_a, v_lru_b_a, v_lru_w_x, v_lru_b_x, v_lru_lambda, v_w_out, v_norm2_g, v_w_up, v_w_down, v_final_norm_g):
    w = dict(zip(_WEIGHTS, (norm1_g, w_in, hgrn_lb_logits, hgrn_norm_g, gdn_conv_w, gdn_a_log, gdn_dt_bias, gdn_norm_g, lru_conv_w, lru_conv_b, lru_w_a, lru_b_a, lru_w_x, lru_b_x, lru_lambda, w_out, norm2_g, w_up, w_down, final_norm_g)))
    m = dict(zip(_WEIGHTS, (m_norm1_g, m_w_in, m_hgrn_lb_logits, m_hgrn_norm_g, m_gdn_conv_w, m_gdn_a_log, m_gdn_dt_bias, m_gdn_norm_g, m_lru_conv_w, m_lru_conv_b, m_lru_w_a, m_lru_b_a, m_lru_w_x, m_lru_b_x, m_lru_lambda, m_w_out, m_norm2_g, m_w_up, m_w_down, m_final_norm_g)))
    v = dict(zip(_WEIGHTS, (v_norm1_g, v_w_in, v_hgrn_lb_logits, v_hgrn_norm_g, v_gdn_conv_w, v_gdn_a_log, v_gdn_dt_bias, v_gdn_norm_g, v_lru_conv_w, v_lru_conv_b, v_lru_w_a, v_lru_b_a, v_lru_w_x, v_lru_b_x, v_lru_lambda, v_w_out, v_norm2_g, v_w_up, v_w_down, v_final_norm_g)))
    loss, grad_x, grads, deltas, new_m, new_v = _step(x.reshape(x.shape[1:]), loss_target.reshape(x.shape[1:]), w, m, v)
    return (loss, grad_x[None], *[grads[n] for n in _WEIGHTS], *[deltas[n] for n in _WEIGHTS],
            *[new_m[n] for n in _WEIGHTS], *[new_v[n] for n in _WEIGHTS])
```

```python
import jax
import jax.numpy as jnp
from jax import lax
from jax.experimental import pallas as pl
from jax.experimental.pallas import tpu as pltpu

F32 = jnp.float32
BF16 = jnp.bfloat16
MESH = pl.DeviceIdType.MESH

N_DEV = 8
D_MODEL = 1024
DEPTH = 4
A_HEADS, A_DIM, A_WIDTH = 4, 64, 256
B_HEADS, B_DIM, B_WIDTH = 4, 128, 512
C_WIDTH, C_BLOCKS, C_BLOCK_DIM = 256, 4, 64
D_IN = 3592
D_IN_PAD = 3840
COL_A, COL_B, COL_C, COL_G = 0, 1024, 3072, 3584
D_FF = 4096
CONV_K = 4
CHUNK = 64
SUB = 16
RG_C = 8.0
EPS = 1e-6
TINY = 1e-30
EXP_CLAMP = 80.0
GDN_SCALE = B_DIM ** -0.5
ADAM_LR, ADAM_B1, ADAM_B2, ADAM_EPS, ADAM_WD, ADAM_STEP = 0.001, 0.9, 0.999, 1e-08, 0.01, 10
VMEM_LIMIT = 56 * 1024 * 1024


def _cparams(sem=None):
    return pltpu.CompilerParams(dimension_semantics=sem, vmem_limit_bytes=VMEM_LIMIT)


_DIMS = {"nn": (((1,), (0,)), ((), ())), "nt": (((1,), (1,)), ((), ())), "tn": (((0,), (0,)), ((), ()))}


def _split_bf16(x):
    hi = x.astype(BF16)
    return hi, (x - hi.astype(F32)).astype(BF16)


def _dot(a, b, mode="nn", hi=False):
    if not hi:
        return lax.dot_general(a.astype(BF16), b.astype(BF16), _DIMS[mode], preferred_element_type=F32)
    ah, al = _split_bf16(a.astype(F32))
    bh, bl = _split_bf16(b.astype(F32))
    ka = 0 if mode == "tn" else 1
    kb = 1 if mode == "nt" else 0
    return lax.dot_general(jnp.concatenate([ah, ah, al], axis=ka), jnp.concatenate([bh, bl, bh], axis=kb),
                           _DIMS[mode], preferred_element_type=F32)


def _dot_exact_lhs(lhs, x, mode="nn"):
    l_bf16 = lhs.astype(BF16)
    x1 = x.astype(BF16)
    r1 = x - x1.astype(F32)
    x2 = r1.astype(BF16)
    x3 = (r1 - x2.astype(F32)).astype(BF16)
    ka = 0 if mode == "tn" else 1
    return lax.dot_general(jnp.concatenate([l_bf16] * 3, axis=ka), jnp.concatenate([x1, x2, x3], axis=0),
                           _DIMS[mode], preferred_element_type=F32)


def _iota2(n, m):
    return lax.broadcasted_iota(jnp.int32, (n, m), 0), lax.broadcasted_iota(jnp.int32, (n, m), 1)


def _tril(n, strict=False):
    r, c = _iota2(n, n)
    return (r > c) if strict else (r >= c)


def _sigmoid(x):
    return 1.0 / (1.0 + jnp.exp(-x))


def _softplus(x):
    return jnp.maximum(x, 0.0) + jnp.log(1.0 + jnp.exp(-jnp.abs(x)))


def _neg_expm1(z):
    series = -z * (1.0 + z * (0.5 + z * (1.0 / 6.0)))
    return jnp.where(z > -1e-2, series, 1.0 - jnp.exp(z))


def _gelu_tanh(x):
    c = 0.7978845608028654
    u = c * (x + 0.044715 * x * x * x)
    t = jnp.tanh(u)
    g = 0.5 * x * (1.0 + t)
    dg = 0.5 * (1.0 + t) + 0.5 * x * (1.0 - t * t) * c * (1.0 + 3.0 * 0.044715 * x * x)
    return g, dg


def _shift_rows(cur, halo, s, down=True):
    n = cur.shape[0]
    ridx = lax.broadcasted_iota(jnp.int32, (8, cur.shape[1]), 0)
    if down:
        main = pltpu.roll(cur, s, 0)
        fix = jnp.where(ridx < s, pltpu.roll(halo, s, 0), main[0:8])
        return jnp.concatenate([fix, main[8:]], axis=0)
    main = pltpu.roll(cur, n - s, 0)
    fix = jnp.where(ridx >= 8 - s, pltpu.roll(halo, 8 - s, 0), main[n - 8:n])
    return jnp.concatenate([main[:n - 8], fix], axis=0)


def _later_rows(dy, nxt8):
    return [_shift_rows(dy, nxt8, 3 - j, down=False) for j in range(3)] + [dy]


def _conv_bwd_rows(dy, nxt8, w):
    return sum(d * w[j:j + 1] for j, d in enumerate(_later_rows(dy, nxt8)))


def _tile_scan(a, b, reverse=False):
    n = a.shape[0]
    r = lax.broadcasted_iota(jnp.int32, a.shape, 0) % 8
    for s in (1, 2, 4):
        keep = (r < 8 - s) if reverse else (r >= s)
        shift = n - s if reverse else s
        a_sh = jnp.where(keep, pltpu.roll(a, shift, 0), 1.0)
        b_sh = jnp.where(keep, pltpu.roll(b, shift, 0), 0.0)
        b = b + a * b_sh
        a = a * a_sh
    return a, b


def _conv_fwd(cur, prev8, w):
    y = cur * w[3:4]
    for j in range(3):
        y = y + _shift_rows(cur, prev8, 3 - j, down=True) * w[j:j + 1]
    return y


def _mm_rows(a, w, mode, tm, name, residual=None, epilogue=None, up=None, norm=None):
    parts = list(a) if isinstance(a, (list, tuple)) else [a]
    widths = [p.shape[1] for p in parts]
    t = parts[0].shape[0]
    n = w.shape[1] if mode == "nn" else w.shape[0]
    tm = min(tm, t)
    assert t % tm == 0 and all(wd % 128 == 0 for wd in widths)

    def body(*refs):
        a_refs, w_ref, rest = refs[:len(parts)], refs[len(parts)], refs[len(parts) + 1:]
        y, off = None, 0
        for a_ref, width in zip(a_refs, widths):
            wk = w_ref[off:off + width, :] if mode == "nn" else w_ref[:, off:off + width]
            d = _dot(a_ref[...], wk, mode)
            y = d if y is None else y + d
            off += width
        if residual is not None:
            y = y + rest[0][...]
        if epilogue == "relu2":
            r = jnp.maximum(y, 0.0)
            refs[-2][...] = (r * r).astype(BF16)
            refs[-1][...] = y.astype(BF16)
        elif epilogue == "drelu2":
            refs[-1][...] = (y * 2.0 * jnp.maximum(rest[0][...].astype(F32), 0.0)).astype(BF16)
        elif epilogue == "rms_fwd":
            rinv = lax.rsqrt(jnp.mean(y * y, axis=-1, keepdims=True) + EPS)
            refs[-2][...] = y
            refs[-1][...] = (y * rinv * refs[-3][...]).astype(BF16)
        elif epilogue == "rms_bwd":
            x_ref, g_ref, dres_ref, dx_ref, dg_ref = rest

            @pl.when(pl.program_id(0) == 0)
            def _():
                dg_ref[...] = jnp.zeros_like(dg_ref)

            xv = x_ref[...]
            rinv = lax.rsqrt(jnp.mean(xv * xv, axis=-1, keepdims=True) + EPS)
            xhat = xv * rinv
            dxh = y * g_ref[...]
            dx_ref[...] = dres_ref[...] + rinv * (dxh - xhat * jnp.mean(dxh * xhat, axis=-1, keepdims=True))
            dg_ref[...] += jnp.sum(y * xhat, axis=0, keepdims=True)
        else:
            refs[-1][...] = y

    rows = lambda width: pl.BlockSpec((tm, width), lambda i: (i, 0))
    vec = pl.BlockSpec((1, n), lambda i: (0, 0))
    ins, specs = parts + [w], [rows(wd) for wd in widths] + [pl.BlockSpec(w.shape, lambda i: (0, 0))]
    if residual is not None:
        ins.append(residual)
        specs.append(rows(n))
    if epilogue == "drelu2":
        ins.append(up)
        specs.append(rows(n))
    if epilogue == "rms_fwd":
        ins.append(norm.reshape(1, n))
        specs.append(vec)
        out_specs, out_shape = [rows(n), rows(n)], [jax.ShapeDtypeStruct((t, n), F32), jax.ShapeDtypeStruct((t, n), BF16)]
    elif epilogue == "rms_bwd":
        ins += [norm[0], norm[1].reshape(1, n), norm[2]]
        specs += [rows(n), vec, rows(n)]
        out_specs, out_shape = [rows(n), vec], [jax.ShapeDtypeStruct((t, n), F32), jax.ShapeDtypeStruct((1, n), F32)]
    elif epilogue == "relu2":
        out_specs, out_shape = [rows(n), rows(n)], [jax.ShapeDtypeStruct((t, n), BF16)] * 2
    else:
        out_specs, out_shape = rows(n), jax.ShapeDtypeStruct((t, n), BF16 if epilogue == "drelu2" else F32)
    return pl.pallas_call(body, grid=(t // tm,), in_specs=specs, out_specs=out_specs, out_shape=out_shape,
                          compiler_params=_cparams(("arbitrary" if epilogue == "rms_bwd" else "parallel",)), name=name)(*ins)


MM_TN_TILE = 1024


def _mm_tn(a, b, tk, name, slab=None):
    a_parts = list(a) if isinstance(a, (list, tuple)) else [a]
    b_parts = list(b) if isinstance(b, (list, tuple)) else [b]
    wa, wb = [p.shape[1] for p in a_parts], [p.shape[1] for p in b_parts]
    t, m, n = a_parts[0].shape[0], sum(wa), sum(wb)
    tk = min(tk, t)
    assert t % tk == 0 and all(x % 128 == 0 for x in wa + wb)
    nk = t // tk

    def body(*refs):
        a_refs, b_refs = refs[:len(wa)], refs[len(wa):len(wa) + len(wb)]
        o_ref, acc = refs[-2], refs[-1]
        kk = pl.program_id(0)

        @pl.when(kk == 0)
        def _():
            acc[...] = jnp.zeros_like(acc)

        ro = 0
        for a_ref, width_a in zip(a_refs, wa):
            for r0 in range(0, width_a, MM_TN_TILE):
                rw = min(MM_TN_TILE, width_a - r0)
                av = a_ref[:, r0:r0 + rw]
                co = 0
                for b_ref, width_b in zip(b_refs, wb):
                    for c0 in range(0, width_b, MM_TN_TILE):
                        cw = min(MM_TN_TILE, width_b - c0)
                        acc[ro + r0:ro + r0 + rw, co + c0:co + c0 + cw] += _dot(av, b_ref[:, c0:c0 + cw], "tn")
                    co += width_b
            ro += width_a

        @pl.when(kk == nk - 1)
        def _():
            if slab is None:
                o_ref[...] = acc[...].astype(BF16)
            else:
                for s in range(n // slab):
                    o_ref[s] = acc[:, s * slab:(s + 1) * slab].astype(BF16)

    if slab is None:
        out_spec, out_shape = pl.BlockSpec((m, n), lambda kk: (0, 0)), jax.ShapeDtypeStruct((m, n), BF16)
    else:
        out_spec, out_shape = pl.BlockSpec((n // slab, m, slab), lambda kk: (0, 0, 0)), jax.ShapeDtypeStruct((n // slab, m, slab), BF16)
    return pl.pallas_call(
        body, grid=(nk,),
        in_specs=[pl.BlockSpec((tk, x), lambda kk: (kk, 0)) for x in wa + wb],
        out_specs=out_spec, out_shape=out_shape, scratch_shapes=[pltpu.VMEM((m, n), F32)],
        compiler_params=_cparams(("arbitrary",)), name=name)(*a_parts, *b_parts)


def _rms_fwd(x, g, tb=512, name="rms_fwd"):
    t, d = x.shape

    def body(x_ref, g_ref, h_ref):
        xv = x_ref[...]
        rinv = lax.rsqrt(jnp.mean(xv * xv, axis=-1, keepdims=True) + EPS)
        h_ref[...] = (xv * rinv * g_ref[...]).astype(BF16)

    return pl.pallas_call(
        body, grid=(t // tb,), in_specs=[pl.BlockSpec((tb, d), lambda i: (i, 0)), pl.BlockSpec((1, d), lambda i: (0, 0))],
        out_specs=pl.BlockSpec((tb, d), lambda i: (i, 0)), out_shape=jax.ShapeDtypeStruct((t, d), BF16),
        compiler_params=_cparams(("parallel",)), name=name)(x, g.reshape(1, d))


def _loss_head(x, g, target, tb=512):
    t, d = x.shape

    def body(x_ref, g_ref, t_ref, loss_ref, dx_ref, dg_ref):
        @pl.when(pl.program_id(0) == 0)
        def _():
            dg_ref[...] = jnp.zeros_like(dg_ref)
            loss_ref[...] = jnp.zeros_like(loss_ref)

        xv = x_ref[...]
        rinv = lax.rsqrt(jnp.mean(xv * xv, axis=-1, keepdims=True) + EPS)
        xhat = xv * rinv
        err = xhat * g_ref[...] - t_ref[...]
        loss_ref[...] += 0.5 * jnp.sum(jnp.mean(err * err, axis=-1, keepdims=True), axis=0, keepdims=True)
        dy = err * (1.0 / d)
        dxh = dy * g_ref[...]
        dx_ref[...] = rinv * (dxh - xhat * jnp.mean(dxh * xhat, axis=-1, keepdims=True))
        dg_ref[...] += jnp.sum(dy * xhat, axis=0, keepdims=True)

    row = pl.BlockSpec((tb, d), lambda i: (i, 0))
    vec = pl.BlockSpec((1, d), lambda i: (0, 0))
    one = pl.BlockSpec((1, 1), lambda i: (0, 0))
    return pl.pallas_call(
        body, grid=(t // tb,), in_specs=[row, vec, row], out_specs=[one, row, vec],
        out_shape=[jax.ShapeDtypeStruct((1, 1), F32), jax.ShapeDtypeStruct((t, d), F32), jax.ShapeDtypeStruct((1, d), F32)],
        compiler_params=_cparams(("arbitrary",)), name="loss_head")(x, g.reshape(1, d), target)


def _lb_fwd(logits):
    def body(l_ref, o_ref):
        lg = l_ref[...]
        e = jnp.exp(lg - jnp.max(lg, axis=0, keepdims=True))
        p = e / jnp.sum(e, axis=0, keepdims=True)
        c = jnp.zeros_like(p[0:1])
        rows = [c]
        for l in range(1, DEPTH):
            c = c + p[l:l + 1]
            rows.append(c)
        o_ref[...] = jnp.minimum(jnp.maximum(jnp.concatenate(rows, axis=0), 0.0), 1.0 - EPS)

    return pl.pallas_call(body, out_shape=jax.ShapeDtypeStruct(logits.shape, F32), name="lb_fwd")(logits)


def _lb_bwd(logits, dlb):
    def body(l_ref, d_ref, o_ref):
        lg = l_ref[...]
        e = jnp.exp(lg - jnp.max(lg, axis=0, keepdims=True))
        p = e / jnp.sum(e, axis=0, keepdims=True)
        hi = 1.0 - EPS
        c = jnp.zeros_like(p[0:1])
        dc = []
        for l in range(1, DEPTH):
            c = c + p[l:l + 1]
            gl = jnp.where(c < 0.0, 0.0, jnp.where(c == 0.0, 0.5, 1.0)) * jnp.where(c > hi, 0.0, jnp.where(c == hi, 0.5, 1.0))
            dc.append(d_ref[l:l + 1, :] * gl)
        dp = [jnp.zeros_like(c)]
        for j in range(1, DEPTH):
            s = dc[j - 1]
            for l in range(j + 1, DEPTH):
                s = s + dc[l - 1]
            dp.append(s)
        dpm = jnp.concatenate(dp, axis=0)
        o_ref[...] = p * (dpm - jnp.sum(p * dpm, axis=0, keepdims=True))

    return pl.pallas_call(body, out_shape=jax.ShapeDtypeStruct(logits.shape, F32), name="lb_bwd")(logits, dlb)


def _a_gates(qi, fi, lbh):
    sq = _sigmoid(qi)
    q = qi * sq
    e = jnp.exp(-jnp.abs(fi))
    rec = 1.0 / (1.0 + e)
    pos = fi >= 0.0
    sg = jnp.where(pos, rec, e * rec)
    sgn = jnp.where(pos, e * rec, rec)
    f = lbh + (1.0 - lbh) * sg
    logf = jnp.log(jnp.maximum(f, TINY))
    k = (1.0 - lbh) * sgn
    return q, sq, sg, sgn, f, logf, k


def _headnorm_fwd(o, g, gate_in):
    rinv = lax.rsqrt(jnp.mean(o * o, axis=-1, keepdims=True) + EPS)
    sg = _sigmoid(gate_in)
    return o * rinv * g * (gate_in * sg)


def _headnorm_bwd(dout, o, g, gate_in):
    rinv = lax.rsqrt(jnp.mean(o * o, axis=-1, keepdims=True) + EPS)
    xhat = o * rinv
    sg = _sigmoid(gate_in)
    silu = gate_in * sg
    dy = dout * silu
    dgate = dout * xhat * g * (sg * (1.0 + gate_in * (1.0 - sg)))
    dxh = dy * g
    do = rinv * (dxh - xhat * jnp.mean(dxh * xhat, axis=-1, keepdims=True))
    return do, dgate, jnp.sum(dy * xhat, axis=0, keepdims=True)


A_PAIRS, A_PAIR_W = A_HEADS // 2, 2 * A_DIM


def _lo_half(shape):
    return lax.broadcasted_iota(jnp.int32, shape, len(shape) - 1) < A_DIM


def _pair_blockdiag(x):
    lo = _lo_half(x.shape)
    return jnp.concatenate([jnp.where(lo, x, 0.0), jnp.where(lo, 0.0, x)], axis=0)


def _pair_fold(m):
    n = m.shape[0] // 2
    return jnp.where(_lo_half((n, A_PAIR_W)), m[:n], m[n:])


def _pair_norm_stats(o):
    lo = _lo_half(o.shape)
    sq = o * o
    s0 = jnp.sum(jnp.where(lo, sq, 0.0), axis=-1, keepdims=True)
    s1 = jnp.sum(sq, axis=-1, keepdims=True) - s0
    return jnp.where(lo, lax.rsqrt(s0 * (1.0 / A_DIM) + EPS), lax.rsqrt(s1 * (1.0 / A_DIM) + EPS))


def _pair_mean(x):
    lo = _lo_half(x.shape)
    s0 = jnp.sum(jnp.where(lo, x, 0.0), axis=-1, keepdims=True)
    s1 = jnp.sum(x, axis=-1, keepdims=True) - s0
    return jnp.where(lo, s0, s1) * (1.0 / A_DIM)


def _a_pair_chunk(qi, fi, v, lb2, s_bd, ltri, causal2):
    q, sq, sg, sgn, f, logf, k = _a_gates(qi, fi, lb2)
    cum = _dot_exact_lhs(ltri, logf)
    cl = cum[CHUNK - 1:CHUNK]
    ecum, ekd, cd = jnp.exp(cum), jnp.exp(cl - cum), jnp.exp(cl)
    qd, kd = q * ecum, k * ekd
    subs, rows = [], []
    for i in range(CHUNK // SUB):
        lo = i * SUB
        r = cum[lo - 1:lo] if i > 0 else jnp.zeros_like(cl)
        eq = jnp.exp(cum[lo:lo + SUB] - r)
        ek = jnp.exp(jnp.minimum(r - cum, EXP_CLAMP))
        qt = q[lo:lo + SUB] * eq
        kt_bd = _pair_blockdiag(k * ek)
        rows.append(_dot(qt, kt_bd, "nt", hi=True))
        subs.append((qt, eq, kt_bd, ek))
    attn = jnp.where(causal2, jnp.concatenate(rows, axis=0), 0.0)
    v_bd = _pair_blockdiag(v)
    o = _dot(qd, s_bd) + _dot(attn, v_bd)
    return dict(q=q, sq=sq, sg=sg, sgn=sgn, f=f, k=k, cum=cum, cl=cl, ecum=ecum, ekd=ekd, cd=cd, qd=qd, kd=kd,
                subs=subs, attn=attn, v_bd=v_bd, o=o)


def _a_fwd(proj, lb, norm_g, tb=256):
    t = proj.shape[0]
    nch = tb // CHUNK

    def body(q_ref, f_ref, i_ref, g_ref, lb_ref, ng_ref, out_ref, st_ref, s_scr):
        @pl.when(pl.program_id(0) == 0)
        def _():
            s_scr[...] = jnp.zeros_like(s_scr)

        ltri = _tril(CHUNK).astype(F32)
        r, c = _iota2(CHUNK, A_PAIR_W)
        causal2 = r >= c % CHUNK
        rb, cb = _iota2(A_PAIR_W, A_PAIR_W)
        diag_blocks = (rb < A_DIM) == (cb < A_DIM)

        def chunk(c, carry):
            rows = pl.ds(pl.multiple_of(c * CHUNK, CHUNK), CHUNK)
            ps = range(A_PAIRS)
            cols = [slice(p * A_PAIR_W, (p + 1) * A_PAIR_W) for p in ps]
            s0 = [s_scr[p] for p in ps]
            for p in ps:
                st_ref[c, p] = s0[p]
            v = [i_ref[rows, cols[p]] for p in ps]
            ch = [_a_pair_chunk(q_ref[rows, cols[p]], f_ref[rows, cols[p]], v[p], lb_ref[:, cols[p]], s0[p], ltri, causal2)
                  for p in ps]
            for p in ps:
                upd = jnp.where(diag_blocks, _dot(ch[p]["kd"], v[p], "tn"), 0.0)
                s_scr[p] = s0[p] * ch[p]["cd"].T + upd
            outs = []
            for p in ps:
                gi = g_ref[rows, cols[p]]
                outs.append(ch[p]["o"] * _pair_norm_stats(ch[p]["o"]) * ng_ref[...] * (gi * _sigmoid(gi)))
            out_ref[rows, :] = jnp.concatenate(outs, axis=1).astype(BF16)
            return carry

        lax.fori_loop(0, nch, chunk, 0, unroll=2)

    colblk = lambda j: pl.BlockSpec((tb, A_WIDTH), lambda i, j=j: (i, j))
    return pl.pallas_call(
        body, grid=(t // tb,),
        in_specs=[colblk(0), colblk(1), colblk(2), colblk(3), pl.BlockSpec((1, A_WIDTH), lambda i: (0, 0)),
                  pl.BlockSpec((1, A_PAIR_W), lambda i: (0, 0))],
        out_specs=[pl.BlockSpec((tb, A_WIDTH), lambda i: (i, 0)),
                   pl.BlockSpec((nch, A_PAIRS, A_PAIR_W, A_PAIR_W), lambda i: (i, 0, 0, 0))],
        out_shape=[jax.ShapeDtypeStruct((t, A_WIDTH), BF16),
                   jax.ShapeDtypeStruct((t // CHUNK, A_PAIRS, A_PAIR_W, A_PAIR_W), F32)],
        scratch_shapes=[pltpu.VMEM((A_PAIRS, A_PAIR_W, A_PAIR_W), F32)],
        compiler_params=_cparams(("arbitrary",)), name="hgrn_fwd")(
            proj, proj, proj, proj, lb.reshape(1, A_WIDTH), jnp.tile(norm_g.reshape(1, A_DIM), (1, 2)))


def _a_bwd(proj, lb, norm_g, states, dmixed, beside=None, tb=256):
    t = proj.shape[0]
    nch = tb // CHUNK
    nb = t // tb

    def body(q_ref, f_ref, i_ref, g_ref, lb_ref, ng_ref, st_ref, dm_ref, dp_ref, dlb_ref, dng_ref, ds_scr):
        @pl.when(pl.program_id(0) == 0)
        def _():
            ds_scr[...] = jnp.zeros_like(ds_scr)
            dlb_ref[...] = jnp.zeros_like(dlb_ref)
            dng_ref[...] = jnp.zeros_like(dng_ref)

        ltri = _tril(CHUNK).astype(F32)
        r, c = _iota2(CHUNK, A_PAIR_W)
        causal2 = r >= c % CHUNK
        rb, cb = _iota2(A_PAIR_W, A_PAIR_W)
        diag_blocks = (rb < A_DIM) == (cb < A_DIM)
        ones8 = jnp.ones((8, A_PAIR_W), F32)

        def chunk(cc, carry):
            c = nch - 1 - cc
            rows = pl.ds(pl.multiple_of(c * CHUNK, CHUNK), CHUNK)
            ps = range(A_PAIRS)
            cols = [slice(p * A_PAIR_W, (p + 1) * A_PAIR_W) for p in ps]
            qi = [q_ref[rows, cols[p]] for p in ps]
            gi = [g_ref[rows, cols[p]] for p in ps]
            v = [i_ref[rows, cols[p]] for p in ps]
            lb2 = [lb_ref[:, cols[p]] for p in ps]
            s0 = [st_ref[c, p] for p in ps]
            ds = [ds_scr[p] for p in ps]
            ch = [_a_pair_chunk(qi[p], f_ref[rows, cols[p]], v[p], lb2[p], s0[p], ltri, causal2) for p in ps]
            o = [ch[p]["o"] for p in ps]
            rinv = [_pair_norm_stats(o[p]) for p in ps]
            xhat = [o[p] * rinv[p] for p in ps]
            sgg = [_sigmoid(gi[p]) for p in ps]
            dout = [dm_ref[rows, cols[p]].astype(F32) for p in ps]
            dy = [dout[p] * (gi[p] * sgg[p]) for p in ps]
            dgi = [dout[p] * xhat[p] * ng_ref[...] * (sgg[p] * (1.0 + gi[p] * (1.0 - sgg[p]))) for p in ps]
            dxh = [dy[p] * ng_ref[...] for p in ps]
            do = [rinv[p] * (dxh[p] - xhat[p] * _pair_mean(dxh[p] * xhat[p])) for p in ps]
            dng = sum(jnp.sum(dy[p] * xhat[p], axis=0, keepdims=True) for p in ps)
            dqd = [_dot(do[p], s0[p], "nt") for p in ps]
            dattn = [jnp.where(causal2, _dot(do[p], ch[p]["v_bd"], "nt"), 0.0) for p in ps]
            dv = [_pair_fold(_dot(ch[p]["attn"], do[p], "tn")) + _dot(ch[p]["kd"], ds[p]) for p in ps]
            dkd = [_dot(v[p], ds[p], "nt") for p in ps]
            dcd = [_dot(ones8, s0[p] * ds[p], "nt", hi=True)[0:1] for p in ps]
            for p in ps:
                ds_scr[p] = jnp.where(diag_blocks, _dot(ch[p]["qd"], do[p], "tn"), 0.0) + ds[p] * ch[p]["cd"].T
            dq_i, dk_i = [], []
            for p in ps:
                dq_rows, dk = [], None
                for i, (qt, eq, kt_bd, ek) in enumerate(ch[p]["subs"]):
                    da = dattn[p][i * SUB:(i + 1) * SUB]
                    dq_rows.append(_dot(da, kt_bd, "nn", hi=True) * eq)
                    d = _pair_fold(_dot(da, qt, "tn", hi=True)) * ek
                    dk = d if dk is None else dk + d
                dq_i.append(jnp.concatenate(dq_rows, axis=0))
                dk_i.append(dk)
            dq = [dqd[p] * ch[p]["ecum"] + dq_i[p] for p in ps]
            dk = [dkd[p] * ch[p]["ekd"] + dk_i[p] for p in ps]
            dkk = [dkd[p] * ch[p]["kd"] for p in ps]
            dcum = [dqd[p] * ch[p]["qd"] - dkk[p] + ch[p]["q"] * dq_i[p] - ch[p]["k"] * dk_i[p] for p in ps]
            dcl = [jnp.sum(dkk[p], axis=0, keepdims=True) + dcd[p] * ch[p]["cd"] for p in ps]
            dlogf = [_dot_exact_lhs(ltri, dcum[p], "tn") + dcl[p] for p in ps]
            dfv = [jnp.where(ch[p]["f"] > TINY, dlogf[p] / ch[p]["f"], 0.0) for p in ps]
            dfi = [dfv[p] * (1.0 - lb2[p]) * ch[p]["sg"] * (1.0 - ch[p]["sg"])
                   - dk[p] * (1.0 - lb2[p]) * ch[p]["sgn"] * (1.0 - ch[p]["sgn"]) for p in ps]
            dlbs = [jnp.sum(dfv[p] * (1.0 - ch[p]["sg"]) - dk[p] * ch[p]["sgn"], axis=0, keepdims=True) for p in ps]
            dqs = [dq[p] * (ch[p]["sq"] * (1.0 + qi[p] * (1.0 - ch[p]["sq"]))) for p in ps]
            dp_ref[rows, :] = jnp.concatenate(dqs + dfi + dv + dgi, axis=1).astype(BF16)
            dlb_ref[...] += jnp.concatenate(dlbs, axis=1)
            dng_ref[...] += dng
            return carry

        lax.fori_loop(0, nch, chunk, 0, unroll=2)

    colblk = lambda j: pl.BlockSpec((tb, A_WIDTH), lambda i, j=j: (nb - 1 - i, j))
    vec = lambda n: pl.BlockSpec((1, n), lambda i: (0, 0))
    return _call_beside(
        body, beside, nb,
        (proj, proj, proj, proj, lb.reshape(1, A_WIDTH), jnp.tile(norm_g.reshape(1, A_DIM), (1, 2)), states, dmixed), grid=(nb,),
        in_specs=[colblk(0), colblk(1), colblk(2), colblk(3), vec(A_WIDTH), vec(A_PAIR_W),
                  pl.BlockSpec((nch, A_PAIRS, A_PAIR_W, A_PAIR_W), lambda i: (nb - 1 - i, 0, 0, 0)), colblk(0)],
        out_specs=[pl.BlockSpec((tb, 4 * A_WIDTH), lambda i: (nb - 1 - i, 0)), vec(A_WIDTH), vec(A_PAIR_W)],
        out_shape=[jax.ShapeDtypeStruct((t, 4 * A_WIDTH), BF16), jax.ShapeDtypeStruct((1, A_WIDTH), F32),
                   jax.ShapeDtypeStruct((1, A_PAIR_W), F32)],
        scratch_shapes=[pltpu.VMEM((A_PAIRS, A_PAIR_W, A_PAIR_W), F32)], name="hgrn_bwd")


def _gate_lane_masks(shape):
    lane = lax.broadcasted_iota(jnp.int32, shape, 1)
    return lane < B_HEADS, (lane >= B_HEADS) & (lane < 2 * B_HEADS)


def _b_pre_fwd(proj, conv_w, alog_row, dtb_row, tb=512):
    t = proj.shape[0]
    cb0 = COL_B // B_WIDTH

    def body(q_ref, k_ref, v_ref, qp_ref, kp_ref, vp_ref, w_ref, gi_ref, al_ref, dt_ref, qkv_ref, gates_ref, y_ref):
        first = pl.program_id(0) == 0
        for part, (c_ref, p_ref) in enumerate(((q_ref, qp_ref), (k_ref, kp_ref), (v_ref, vp_ref))):
            cols = slice(part * B_WIDTH, (part + 1) * B_WIDTH)
            prev = jnp.where(first, 0.0, p_ref[...])
            y = _conv_fwd(c_ref[...], prev, w_ref[:, cols])
            y_ref[:, cols] = y
            s = y * _sigmoid(y)
            if part < 2:
                outs = []
                for h in range(B_HEADS):
                    sh = s[:, h * B_DIM:(h + 1) * B_DIM]
                    outs.append(sh * lax.rsqrt(jnp.sum(sh * sh, axis=-1, keepdims=True) + EPS))
                s = jnp.concatenate(outs, axis=1)
            qkv_ref[:, cols] = s
        g = gi_ref[...]
        is_b, is_a = _gate_lane_masks(g.shape)
        la = -jnp.exp(al_ref[...]) * _softplus(g + dt_ref[...])
        gates_ref[...] = jnp.where(is_b, _sigmoid(g), jnp.where(is_a, la, 0.0))

    cur = lambda j: pl.BlockSpec((tb, B_WIDTH), lambda i, j=j: (i, cb0 + j))
    prv = lambda j: pl.BlockSpec((8, B_WIDTH), lambda i, j=j: (jnp.maximum(i * (tb // 8) - 1, 0), cb0 + j))
    vec = pl.BlockSpec((1, 128), lambda i: (0, 0))
    return pl.pallas_call(
        body, grid=(t // tb,),
        in_specs=[cur(0), cur(1), cur(2), prv(0), prv(1), prv(2), pl.BlockSpec((CONV_K, 3 * B_WIDTH), lambda i: (0, 0)),
                  pl.BlockSpec((tb, 128), lambda i: (i, COL_G // 128)), vec, vec],
        out_specs=[pl.BlockSpec((tb, 3 * B_WIDTH), lambda i: (i, 0)), pl.BlockSpec((tb, 128), lambda i: (i, 0)),
                   pl.BlockSpec((tb, 3 * B_WIDTH), lambda i: (i, 0))],
        out_shape=[jax.ShapeDtypeStruct((t, 3 * B_WIDTH), F32), jax.ShapeDtypeStruct((t, 128), F32),
                   jax.ShapeDtypeStruct((t, 3 * B_WIDTH), F32)],
        compiler_params=_cparams(("parallel",)), name="gdn_pre_fwd")(proj, proj, proj, proj, proj, proj, conv_w, proj, alog_row, dtb_row)


def _inv_unit_lower(amats):
    r, c = _iota2(CHUNK, CHUNK)
    eye = jnp.where(r == c, 1.0, 0.0)
    ps = [eye - a for a in amats]
    aks = amats
    for _ in range(5):
        aks = [_dot(ak, ak, hi=True) for ak in aks]
        ps = [p + _dot(p, ak, hi=True) for p, ak in zip(ps, aks)]
    return ps


def _b_local(qs, ks, vs, betas, gcs, grows, gls, solve=True):
    hs = range(len(qs))
    causal, strict = _tril(CHUNK), _tril(CHUNK, strict=True)
    decay = [jnp.where(causal, jnp.exp(jnp.minimum(gcs[h] - grows[h], 0.0)), 0.0) for h in hs]
    kb = [ks[h] * betas[h] for h in hs]
    kk = [_dot(kb[h], ks[h], "nt") for h in hs]
    qkr = [_dot(qs[h], ks[h], "nt") for h in hs]
    eg = [jnp.exp(gcs[h]) for h in hs]
    bv = [vs[h] * betas[h] for h in hs]
    kg = [kb[h] * eg[h] for h in hs]
    qk = [qkr[h] * decay[h] for h in hs]
    qd = [qs[h] * eg[h] for h in hs]
    ekd = [jnp.exp(gls[h] - gcs[h]) for h in hs]
    kd = [ks[h] * ekd[h] for h in hs]
    cd = [jnp.exp(gls[h]) for h in hs]
    loc = dict(decay=decay, kb=kb, kk=kk, eg=eg, bv=bv, kg=kg, qkr=qkr, qk=qk, qd=qd, ekd=ekd, kd=kd, cd=cd)
    if solve:
        tinv = _inv_unit_lower([jnp.where(strict, kk[h] * decay[h], 0.0) for h in hs])
        loc.update(tinv=tinv, u=[_dot(tinv[h], bv[h], hi=True) for h in hs], w=[_dot(tinv[h], kg[h], hi=True) for h in hs])
    return loc


def _b_state(loc, ids, s0s):
    n = range(len(ids))
    ws = [_dot(loc["w"][ids[j]], s0s[j]) for j in n]
    qs0 = [_dot(loc["qd"][ids[j]], s0s[j]) for j in n]
    vn = [loc["u"][ids[j]] - ws[j] for j in n]
    o = [qs0[j] + _dot(loc["qk"][ids[j]], vn[j]) for j in n]
    s1 = [s0s[j] * loc["cd"][ids[j]] + _dot(loc["kd"][ids[j]], vn[j], "tn") for j in n]
    return vn, o, s1


def _b_fwd(qkv, gates, proj, norm_g, beside=None, tb=256):
    t = qkv.shape[0]
    nch = tb // CHUNK

    def body(q_ref, k_ref, v_ref, ga_ref, z_ref, ng_ref, out_ref, st_ref, ti_ref, w_ref, vn_ref, o_ref, s_scr):
        @pl.when(pl.program_id(0) == 0)
        def _():
            s_scr[...] = jnp.zeros_like(s_scr)

        ltri = _tril(CHUNK).astype(F32)

        hs = range(B_HEADS)
        cols = [slice(h * B_DIM, (h + 1) * B_DIM) for h in hs]

        def pair(p, carry):
            cs = [2 * p, 2 * p + 1]
            rows = [pl.ds(pl.multiple_of(c * CHUNK, CHUNK), CHUNK) for c in cs]
            ga = [ga_ref[r, :] for r in rows]
            gcum = [_dot_exact_lhs(ltri, g) for g in ga]
            gcum_t = [g.T for g in gcum]
            items = [(i, h) for i in range(2) for h in hs]
            loc = _b_local([q_ref[rows[i], cols[h]] * GDN_SCALE for i, h in items], [k_ref[rows[i], cols[h]] for i, h in items],
                           [v_ref[rows[i], cols[h]] for i, h in items], [ga[i][:, h:h + 1] for i, h in items],
                           [gcum[i][:, B_HEADS + h:B_HEADS + h + 1] for i, h in items],
                           [gcum_t[i][B_HEADS + h:B_HEADS + h + 1, :] for i, h in items],
                           [gcum[i][CHUNK - 1:CHUNK, B_HEADS + h:B_HEADS + h + 1] for i, h in items])
            s0s = [s_scr[h] for h in hs]
            for i in range(2):
                ids = [i * B_HEADS + h for h in hs]
                for h in hs:
                    st_ref[cs[i], h] = s0s[h]
                    ti_ref[cs[i], h] = loc["tinv"][ids[h]]
                vn, o, s0s = _b_state(loc, ids, s0s)
                w_ref[rows[i], :] = jnp.concatenate([loc["w"][j] for j in ids], axis=1)
                vn_ref[rows[i], :] = jnp.concatenate(vn, axis=1)
                o_ref[rows[i], :] = jnp.concatenate(o, axis=1)
                outs = [_headnorm_fwd(o[h], ng_ref[...], z_ref[rows[i], cols[h]]) for h in hs]
                out_ref[rows[i], :] = jnp.concatenate(outs, axis=1).astype(BF16)
            for h in hs:
                s_scr[h] = s0s[h]
            return carry

        lax.fori_loop(0, nch // 2, pair, 0)

    part = lambda j: pl.BlockSpec((tb, B_WIDTH), lambda i, j=j: (i, j))
    wide = pl.BlockSpec((tb, B_WIDTH), lambda i: (i, 0))
    wide_shape = jax.ShapeDtypeStruct((t, B_WIDTH), F32)
    return _call_beside(
        body, beside, t // tb, (qkv, qkv, qkv, gates, proj, norm_g.reshape(1, B_DIM)), grid=(t // tb,),
        in_specs=[part(0), part(1), part(2), pl.BlockSpec((tb, 128), lambda i: (i, 0)),
                  pl.BlockSpec((tb, B_WIDTH), lambda i: (i, COL_B // B_WIDTH + 3)), pl.BlockSpec((1, B_DIM), lambda i: (0, 0))],
        out_specs=[wide, pl.BlockSpec((nch, B_HEADS, B_DIM, B_DIM), lambda i: (i, 0, 0, 0)),
                   pl.BlockSpec((nch, B_HEADS, CHUNK, CHUNK), lambda i: (i, 0, 0, 0)), wide, wide, wide],
        out_shape=[jax.ShapeDtypeStruct((t, B_WIDTH), BF16), jax.ShapeDtypeStruct((t // CHUNK, B_HEADS, B_DIM, B_DIM), F32),
                   jax.ShapeDtypeStruct((t // CHUNK, B_HEADS, CHUNK, CHUNK), F32), wide_shape, wide_shape, wide_shape],
        scratch_shapes=[pltpu.VMEM((B_HEADS, B_DIM, B_DIM), F32)], name="gdn_fwd")


def _b_bwd(qkv, gates, proj, norm_g, states, fwd_saved, dmixed, beside=None, tb=256):
    t = qkv.shape[0]
    nch = tb // CHUNK
    nb = t // tb

    def body(q_ref, k_ref, v_ref, ga_ref, z_ref, ng_ref, st_ref, ti_ref, w_ref, vn_ref, o_ref, dm0_ref, dm1_ref,
             dqkv_ref, dga_ref, dz_ref, dng_ref, ds_scr):
        @pl.when(pl.program_id(0) == 0)
        def _():
            ds_scr[...] = jnp.zeros_like(ds_scr)
            dng_ref[...] = jnp.zeros_like(dng_ref)

        ltri = _tril(CHUNK).astype(F32)
        strict = _tril(CHUNK, strict=True)
        lane = lax.broadcasted_iota(jnp.int32, (CHUNK, 128), 1)
        lane1 = lax.broadcasted_iota(jnp.int32, (1, 128), 1)

        nh = range(B_HEADS)
        cols = [slice(h * B_DIM, (h + 1) * B_DIM) for h in nh]
        rsum = lambda a: jnp.sum(a, axis=-1, keepdims=True)

        def pair(p, carry):
            cs = [nch - 1 - 2 * p, nch - 2 - 2 * p]
            crow = [pl.ds(pl.multiple_of(c * CHUNK, CHUNK), CHUNK) for c in cs]
            gas = [ga_ref[r, :] for r in crow]
            gcum = [_dot_exact_lhs(ltri, g) for g in gas]
            gcum_t = [g.T for g in gcum]
            items = [(i, h) for i in range(2) for h in nh]
            hs = range(len(items))
            q = [q_ref[crow[i], cols[h]] * GDN_SCALE for i, h in items]
            k = [k_ref[crow[i], cols[h]] for i, h in items]
            v = [v_ref[crow[i], cols[h]] for i, h in items]
            z = [z_ref[crow[i], cols[h]] for i, h in items]
            beta = [gas[i][:, h:h + 1] for i, h in items]
            s0 = [st_ref[cs[i], h] for i, h in items]
            r = _b_local(q, k, v, beta, [gcum[i][:, B_HEADS + h:B_HEADS + h + 1] for i, h in items],
                         [gcum_t[i][B_HEADS + h:B_HEADS + h + 1, :] for i, h in items],
                         [gcum[i][CHUNK - 1:CHUNK, B_HEADS + h:B_HEADS + h + 1] for i, h in items], solve=False)
            tinv = [ti_ref[cs[i], h] for i, h in items]
            w = [w_ref[crow[i], cols[h]] for i, h in items]
            vn = [vn_ref[crow[i], cols[h]] for i, h in items]
            decay, eg, qd, kd, kb, cd = (r[n] for n in ("decay", "eg", "qd", "kd", "kb", "cd"))
            dms = [(dm0_ref if h < 2 else dm1_ref)[crow[i], (h % 2) * B_DIM:(h % 2 + 1) * B_DIM].astype(F32) for i, h in items]
            hn = [_headnorm_bwd(dms[j], o_ref[crow[i], cols[h]], ng_ref[...], z[j]) for j, (i, h) in enumerate(items)]
            do = [hn[j][0] for j in hs]
            dvn_o = [_dot(r["qk"][j], do[j], "tn") for j in hs]
            dqk = [_dot(do[j], vn[j], "nt") for j in hs]
            dqd = [_dot(do[j], s0[j], "nt") for j in hs]
            ds_o = [_dot(qd[j], do[j], "tn") for j in hs]
            ds = [ds_scr[h] for h in nh]
            dvn, dkd, dcd = [None] * 8, [None] * 8, [None] * 8
            for i in range(2):
                for h in nh:
                    j = i * B_HEADS + h
                    dvn[j] = dvn_o[j] + _dot(kd[j], ds[h])
                    dkd[j] = _dot(vn[j], ds[h], "nt")
                    dcd[j] = jnp.sum(jnp.sum(s0[j] * ds[h], axis=0, keepdims=True), axis=1, keepdims=True)
                ds = [ds_o[i * B_HEADS + h] + ds[h] * cd[i * B_HEADS + h] - _dot(w[i * B_HEADS + h], dvn[i * B_HEADS + h], "tn")
                      for h in nh]
            for h in nh:
                ds_scr[h] = ds[h]
            dw = [-_dot(dvn[j], s0[j], "nt") for j in hs]
            dbv = [_dot(tinv[h], dvn[h], "tn", hi=True) for h in hs]
            dkg = [_dot(tinv[h], dw[h], "tn", hi=True) for h in hs]
            dt = [_dot(dvn[h], r["bv"][h], "nt", hi=True) + _dot(dw[h], r["kg"][h], "nt", hi=True) for h in hs]
            tdt = [_dot(tinv[h], dt[h], "tn", hi=True) for h in hs]
            da = [jnp.where(strict, -_dot(tdt[h], tinv[h], "nt", hi=True), 0.0) for h in hs]
            dm = [da[h] * decay[h] for h in hs]
            dn = [dqk[h] * decay[h] for h in hs]
            e = [(da[h] * r["kk"][h] + dqk[h] * r["qkr"][h]) * decay[h] for h in hs]
            dkb = [_dot(dm[h], k[h]) + dkg[h] * eg[h] for h in hs]
            dk = [_dot(dm[h], kb[h], "tn") + _dot(dn[h], q[h], "tn") + dkd[h] * r["ekd"][h] + dkb[h] * beta[h] for h in hs]
            dq = [_dot(dn[h], k[h]) + dqd[h] * eg[h] for h in hs]
            tkd = [rsum(dkd[h] * kd[h]) for h in hs]
            dgc = [rsum(e[h]) - rsum(e[h].T) + rsum(dqd[h] * qd[h]) - tkd[h] + rsum(dkg[h] * r["kg"][h]) for h in hs]
            dgl = [jnp.sum(tkd[h], axis=0, keepdims=True) + dcd[h] * cd[h] for h in hs]
            dbeta = [rsum(dbv[h] * v[h]) + rsum(dkb[h] * k[h]) for h in hs]
            for i in range(2):
                ids = [i * B_HEADS + h for h in nh]
                dbeta_m = sum(jnp.where(lane == h, dbeta[ids[h]], 0.0) for h in nh)
                dgc_m = sum(jnp.where(lane == B_HEADS + h, dgc[ids[h]], 0.0) for h in nh)
                dgl_m = sum(jnp.where(lane1 == B_HEADS + h, dgl[ids[h]], 0.0) for h in nh)
                dqkv_ref[crow[i], :] = jnp.concatenate(
                    [dq[j] * GDN_SCALE for j in ids] + [dk[j] for j in ids] + [dbv[j] * beta[j] for j in ids], axis=1)
                dz_ref[crow[i], :] = jnp.concatenate([hn[j][1] for j in ids], axis=1).astype(BF16)
                dga_ref[crow[i], :] = dbeta_m + _dot_exact_lhs(ltri, dgc_m, "tn") + dgl_m
            dng_ref[...] += sum(hn[j][2] for j in hs)
            return carry

        lax.fori_loop(0, nch // 2, pair, 0)

    part = lambda j: pl.BlockSpec((tb, B_WIDTH), lambda i, j=j: (nb - 1 - i, j))
    rowblk = lambda w, j=0: pl.BlockSpec((tb, w), lambda i, j=j: (nb - 1 - i, j))
    return _call_beside(
        body, beside, nb, (qkv, qkv, qkv, gates, proj, norm_g.reshape(1, B_DIM), states, *fwd_saved, dmixed, dmixed), grid=(nb,),
        in_specs=[part(0), part(1), part(2), rowblk(128), rowblk(B_WIDTH, COL_B // B_WIDTH + 3),
                  pl.BlockSpec((1, B_DIM), lambda i: (0, 0)),
                  pl.BlockSpec((nch, B_HEADS, B_DIM, B_DIM), lambda i: (nb - 1 - i, 0, 0, 0)),
                  pl.BlockSpec((nch, B_HEADS, CHUNK, CHUNK), lambda i: (nb - 1 - i, 0, 0, 0)),
                  rowblk(B_WIDTH), rowblk(B_WIDTH), rowblk(B_WIDTH), rowblk(256, 1), rowblk(256, 2)],
        out_specs=[rowblk(3 * B_WIDTH), rowblk(128), rowblk(B_WIDTH), pl.BlockSpec((1, B_DIM), lambda i: (0, 0))],
        out_shape=[jax.ShapeDtypeStruct((t, 3 * B_WIDTH), F32), jax.ShapeDtypeStruct((t, 128), F32),
                   jax.ShapeDtypeStruct((t, B_WIDTH), BF16), jax.ShapeDtypeStruct((1, B_DIM), F32)],
        scratch_shapes=[pltpu.VMEM((B_HEADS, B_DIM, B_DIM), F32)], name="gdn_bwd")


def _b_pre_bwd(proj, y3, conv_w, alog_row, dtb_row, dqkv, dgates, tb=512):
    t = proj.shape[0]
    nb = t // tb
    cb0 = COL_B // B_WIDTH

    def body(q_ref, k_ref, v_ref, y_ref, w_ref, gi_ref, al_ref, dt_ref, dqkv_ref, dga_ref,
             dy_ref, dgi_ref, dw_ref, dal_ref, ddt_ref, nxt_scr):
        step_id = pl.program_id(0)

        @pl.when(step_id == 0)
        def _():
            dw_ref[...] = jnp.zeros_like(dw_ref)
            dal_ref[...] = jnp.zeros_like(dal_ref)
            ddt_ref[...] = jnp.zeros_like(ddt_ref)

        for part, c_ref in enumerate((q_ref, k_ref, v_ref)):
            cols = slice(part * B_WIDTH, (part + 1) * B_WIDTH)
            cur = c_ref[...]
            w = w_ref[:, cols]
            y = y_ref[:, cols]
            sg = _sigmoid(y)
            s = y * sg
            dsn = dqkv_ref[:, cols]
            if part < 2:
                outs = []
                for h in range(B_HEADS):
                    hc = slice(h * B_DIM, (h + 1) * B_DIM)
                    sh, dh = s[:, hc], dsn[:, hc]
                    rq = lax.rsqrt(jnp.sum(sh * sh, axis=-1, keepdims=True) + EPS)
                    nh = sh * rq
                    outs.append(rq * (dh - nh * jnp.sum(dh * nh, axis=-1, keepdims=True)))
                dsn = jnp.concatenate(outs, axis=1)
            dy = dsn * (sg * (1.0 + y * (1.0 - sg)))
            later = _later_rows(dy, jnp.where(step_id == 0, 0.0, nxt_scr[:, cols]))
            dy_ref[:, cols] = sum(later[j] * w[j:j + 1] for j in range(CONV_K)).astype(BF16)
            nxt_scr[:, cols] = dy[0:8]
            dw_ref[:, cols] += jnp.concatenate([jnp.sum(cur * later[j], axis=0, keepdims=True) for j in range(CONV_K)], axis=0)
        g = gi_ref[...]
        dga = dga_ref[...]
        is_b, is_a = _gate_lane_masks(g.shape)
        beta = _sigmoid(g)
        pre = g + dt_ref[...]
        ea = jnp.exp(al_ref[...])
        la = -ea * _softplus(pre)
        dpre = jnp.where(is_a, dga * (-ea) * _sigmoid(pre), 0.0)
        dgi_ref[...] = jnp.where(is_b, dga * beta * (1.0 - beta), dpre).astype(BF16)
        dal_ref[...] += jnp.sum(jnp.where(is_a, dga * la, 0.0), axis=0, keepdims=True)
        ddt_ref[...] += jnp.sum(dpre, axis=0, keepdims=True)

    cur = lambda j: pl.BlockSpec((tb, B_WIDTH), lambda i, j=j: (nb - 1 - i, cb0 + j))
    vec = pl.BlockSpec((1, 128), lambda i: (0, 0))
    wspec = pl.BlockSpec((CONV_K, 3 * B_WIDTH), lambda i: (0, 0))
    rowblk = lambda width, j=0: pl.BlockSpec((tb, width), lambda i, j=j: (nb - 1 - i, j))
    return pl.pallas_call(
        body, grid=(nb,),
        in_specs=[cur(0), cur(1), cur(2), rowblk(3 * B_WIDTH), wspec, rowblk(128, COL_G // 128), vec, vec,
                  rowblk(3 * B_WIDTH), rowblk(128)],
        out_specs=[rowblk(3 * B_WIDTH), rowblk(128), wspec, vec, vec],
        out_shape=[jax.ShapeDtypeStruct((t, 3 * B_WIDTH), BF16), jax.ShapeDtypeStruct((t, 128), BF16),
                   jax.ShapeDtypeStruct((CONV_K, 3 * B_WIDTH), F32), jax.ShapeDtypeStruct((1, 128), F32), jax.ShapeDtypeStruct((1, 128), F32)],
        scratch_shapes=[pltpu.VMEM((8, 3 * B_WIDTH), F32)],
        compiler_params=_cparams(("arbitrary",)), name="gdn_pre_bwd")(
            proj, proj, proj, y3, conv_w, proj, alog_row, dtb_row, dqkv, dgates)


def _c_gates(xc, wa_ref, ba_ref, wx_ref, bx_ref, lam_ref, is_row0):
    r = _sigmoid(_dot(xc, wa_ref[...]) + ba_ref[...])
    i = _sigmoid(_dot(xc, wx_ref[...]) + bx_ref[...])
    sp = _softplus(-lam_ref[...])
    log_a = -RG_C * r * sp
    a = jnp.exp(log_a)
    m2 = _neg_expm1(2.0 * log_a)
    mult = jnp.where(is_row0, 1.0, jnp.sqrt(jnp.maximum(m2, EPS)))
    return r, i, sp, log_a, a, m2, mult


def _row0_mask(tb, first):
    ridx = lax.broadcasted_iota(jnp.int32, (tb, C_WIDTH), 0)
    return (ridx == 0) & first


def _c_fwd(proj, conv_w, conv_b, wa, ba, wx, bx, lam, tb=512):
    t = proj.shape[0]
    cbx = COL_C // C_WIDTH

    def body(x_ref, xp_ref, y_ref, w_ref, cb_ref, wa_ref, ba_ref, wx_ref, bx_ref, lam_ref, out_ref, h_ref, a_scr, b_scr, h_scr):
        first = pl.program_id(0) == 0

        @pl.when(first)
        def _():
            h_scr[...] = jnp.zeros_like(h_scr)

        prev = jnp.where(first, 0.0, xp_ref[...])
        xc = _conv_fwd(x_ref[...], prev, w_ref[...]) + cb_ref[...]
        _, i, _, _, a, _, mult = _c_gates(xc, wa_ref, ba_ref, wx_ref, bx_ref, lam_ref, _row0_mask(tb, first))
        ta, tb_ = _tile_scan(a, mult * i * xc)
        a_scr[...] = ta
        b_scr[...] = tb_

        def step(blk, h):
            rows = pl.ds(pl.multiple_of(blk * 8, 8), 8)
            h_ref[rows, :] = jnp.broadcast_to(h, (8, C_WIDTH))
            return a_scr[rows, :][7:8] * h + b_scr[rows, :][7:8]

        h_scr[...] = lax.fori_loop(0, tb // 8, step, h_scr[...], unroll=8)
        hs = ta * h_ref[...] + tb_
        h_ref[...] = hs
        gl, _ = _gelu_tanh(y_ref[...])
        out_ref[...] = (gl * hs).astype(BF16)

    vec = pl.BlockSpec((1, C_WIDTH), lambda i: (0, 0))
    mat = pl.BlockSpec((C_WIDTH, C_WIDTH), lambda i: (0, 0))
    row = pl.BlockSpec((tb, C_WIDTH), lambda i: (i, 0))
    return pl.pallas_call(
        body, grid=(t // tb,),
        in_specs=[pl.BlockSpec((tb, C_WIDTH), lambda i: (i, cbx)),
                  pl.BlockSpec((8, C_WIDTH), lambda i: (jnp.maximum(i * (tb // 8) - 1, 0), cbx)),
                  pl.BlockSpec((tb, C_WIDTH), lambda i: (i, cbx + 1)),
                  pl.BlockSpec((CONV_K, C_WIDTH), lambda i: (0, 0)), vec, mat, vec, mat, vec, vec],
        out_specs=[row, row],
        out_shape=[jax.ShapeDtypeStruct((t, C_WIDTH), BF16), jax.ShapeDtypeStruct((t, C_WIDTH), F32)],
        scratch_shapes=[pltpu.VMEM((tb, C_WIDTH), F32), pltpu.VMEM((tb, C_WIDTH), F32), pltpu.VMEM((1, C_WIDTH), F32)],
        compiler_params=_cparams(("arbitrary",)), name="lru_fwd")(proj, proj, proj, conv_w, conv_b, wa, ba, wx, bx, lam)


def _c_bwd(proj, conv_w, conv_b, wa, ba, wx, bx, lam, hs, dmixed, tb=512):
    t = proj.shape[0]
    nb = t // tb
    cbx = COL_C // C_WIDTH

    def body(x_ref, xp_ref, y_ref, w_ref, cb_ref, wa_ref, ba_ref, wx_ref, bx_ref, lam_ref, h_ref, hp_ref, dm_ref,
             dxc_ref, dyg_ref, dw_ref, dcb_ref, dwa_ref, dba_ref, dwx_ref, dbx_ref, dlam_ref, g_scr, a_scr, cin_scr, c_scr, nxt_scr):
        step_id = pl.program_id(0)
        first = step_id == nb - 1

        @pl.when(step_id == 0)
        def _():
            c_scr[...] = jnp.zeros_like(c_scr)
            for ref in (dw_ref, dcb_ref, dwa_ref, dba_ref, dwx_ref, dbx_ref, dlam_ref):
                ref[...] = jnp.zeros_like(ref)

        cur = x_ref[...]
        prev = jnp.where(first, 0.0, xp_ref[...])
        w = w_ref[...]
        shifted = [_shift_rows(cur, prev, 3 - j, down=True) for j in range(3)] + [cur]
        xc = shifted[0] * w[0:1] + shifted[1] * w[1:2] + shifted[2] * w[2:3] + shifted[3] * w[3:4] + cb_ref[...]
        row0 = _row0_mask(tb, first)
        r, i, sp, log_a, a, m2, mult = _c_gates(xc, wa_ref, ba_ref, wx_ref, bx_ref, lam_ref, row0)
        h = h_ref[...]
        hprev = _shift_rows(h, jnp.where(first, 0.0, hp_ref[...]), 1, down=True)
        gl, dgl = _gelu_tanh(y_ref[...])
        dm = dm_ref[...].astype(F32)
        dyg_ref[...] = (dm * h * dgl).astype(BF16)
        dout = dm * gl
        ta, te = _tile_scan(a, a * dout, reverse=True)
        a_scr[...] = ta
        g_scr[...] = te

        def step(blk, carry):
            rows = pl.ds(pl.multiple_of((tb // 8 - 1 - blk) * 8, 8), 8)
            cin_scr[rows, :] = jnp.broadcast_to(carry, (8, C_WIDTH))
            return a_scr[rows, :][0:1] * carry + g_scr[rows, :][0:1]

        c_scr[...] = lax.fori_loop(0, tb // 8, step, c_scr[...], unroll=8)
        cin = cin_scr[...]
        cout = ta * cin + te
        last_in_tile = lax.broadcasted_iota(jnp.int32, (tb, C_WIDTH), 0) % 8 == 7
        dbx = dout + jnp.where(last_in_tile, cin, pltpu.roll(cout, tb - 1, 0))
        da = dbx * hprev
        dmult = jnp.where(row0, 0.0, dbx * i * xc)
        di = dbx * mult * xc
        dxc = dbx * mult * i
        dm2 = jnp.where(m2 > EPS, dmult * 0.5 / mult, 0.0)
        dlog_a = da * a - 2.0 * a * a * dm2
        dr = dlog_a * (-RG_C) * sp
        dlam_ref[...] += jnp.sum(dlog_a * (-RG_C) * r, axis=0, keepdims=True) * (-_sigmoid(-lam_ref[...]))
        dpa = dr * r * (1.0 - r)
        dpx = di * i * (1.0 - i)
        dba_ref[...] += jnp.sum(dpa, axis=0, keepdims=True)
        dbx_ref[...] += jnp.sum(dpx, axis=0, keepdims=True)
        dwa_ref[...] += _dot(xc, dpa, "tn")
        dwx_ref[...] += _dot(xc, dpx, "tn")
        dxc = dxc + _dot(dpa, wa_ref[...], "nt") + _dot(dpx, wx_ref[...], "nt")
        dxc_ref[...] = _conv_bwd_rows(dxc, jnp.where(step_id == 0, 0.0, nxt_scr[...]), w).astype(BF16)
        nxt_scr[...] = dxc[0:8]
        dcb_ref[...] += jnp.sum(dxc, axis=0, keepdims=True)
        dw_ref[...] += jnp.concatenate([jnp.sum(shifted[j] * dxc, axis=0, keepdims=True) for j in range(CONV_K)], axis=0)

    vec = pl.BlockSpec((1, C_WIDTH), lambda i: (0, 0))
    mat = pl.BlockSpec((C_WIDTH, C_WIDTH), lambda i: (0, 0))
    cw = pl.BlockSpec((CONV_K, C_WIDTH), lambda i: (0, 0))
    row = lambda j=0: pl.BlockSpec((tb, C_WIDTH), lambda i, j=j: (nb - 1 - i, j))
    halo = lambda j=0: pl.BlockSpec((8, C_WIDTH), lambda i, j=j: (jnp.maximum((nb - 1 - i) * (tb // 8) - 1, 0), j))
    return pl.pallas_call(
        body, grid=(nb,),
        in_specs=[row(cbx), halo(cbx), row(cbx + 1), cw, vec, mat, vec, mat, vec, vec, row(), halo(), row(3)],
        out_specs=[row(), row(), cw, vec, mat, vec, mat, vec, vec],
        out_shape=[jax.ShapeDtypeStruct((t, C_WIDTH), BF16), jax.ShapeDtypeStruct((t, C_WIDTH), BF16),
                   jax.ShapeDtypeStruct((CONV_K, C_WIDTH), F32), jax.ShapeDtypeStruct((1, C_WIDTH), F32),
                   jax.ShapeDtypeStruct((C_WIDTH, C_WIDTH), F32), jax.ShapeDtypeStruct((1, C_WIDTH), F32),
                   jax.ShapeDtypeStruct((C_WIDTH, C_WIDTH), F32), jax.ShapeDtypeStruct((1, C_WIDTH), F32),
                   jax.ShapeDtypeStruct((1, C_WIDTH), F32)],
        scratch_shapes=[pltpu.VMEM((tb, C_WIDTH), F32), pltpu.VMEM((tb, C_WIDTH), F32), pltpu.VMEM((tb, C_WIDTH), F32),
                        pltpu.VMEM((1, C_WIDTH), F32), pltpu.VMEM((8, C_WIDTH), F32)],
        compiler_params=_cparams(("arbitrary",)), name="lru_bwd")(
            proj, proj, proj, conv_w, conv_b, wa, ba, wx, bx, lam, hs, hs, dmixed)


def _mesh_pos():
    return lax.axis_index("x"), lax.axis_index("y"), lax.axis_index("c")


class _Exchange:
    def __init__(self, arrays, layouts):
        self.arrays, self.layouts = list(arrays), list(layouts)
        self.out_shapes = []
        for a, lay in zip(self.arrays, self.layouts):
            if lay == 'a2a':
                shp = a.shape
            elif lay == 'slot':
                shp = (N_DEV,) + a.shape
            elif lay == 'rows':
                shp = (N_DEV * a.shape[0], a.shape[1])
            else:
                shp = (a.shape[0], N_DEV * a.shape[1])
            self.out_shapes.append(jax.ShapeDtypeStruct(shp, a.dtype))
        n = len(self.arrays)
        self.scratch = [pltpu.SemaphoreType.DMA((7 * n,)), pltpu.SemaphoreType.DMA((7 * n,)), pltpu.SemaphoreType.DMA((n,))]

    def _landing(self, a, dst_ref, idx):
        lay, shape = self.layouts[a], self.arrays[a].shape
        if lay in ('a2a', 'slot'):
            return dst_ref.at[idx]
        if lay == 'rows':
            return dst_ref.at[pl.ds(pl.multiple_of(idx * shape[0], shape[0]), shape[0]), :]
        return dst_ref.at[:, pl.ds(pl.multiple_of(idx * shape[1], shape[1]), shape[1])]

    def copies(self, src_refs, dst_refs, send_sems, recv_sems, local_sems):
        mx, my, mc = _mesh_pos()
        me = 4 * mx + 2 * my + mc
        out = []
        for a, (src, dst) in enumerate(zip(src_refs, dst_refs)):
            a2a = self.layouts[a] == 'a2a'
            out.append(pltpu.make_async_copy(src.at[me] if a2a else src, self._landing(a, dst, me), local_sems.at[a]))
            for k in range(1, N_DEV):
                px = 1 - mx if k & 4 else mx
                py = 1 - my if k & 2 else my
                pc = 1 - mc if k & 1 else mc
                out.append(pltpu.make_async_remote_copy(
                    src_ref=src.at[4 * px + 2 * py + pc] if a2a else src, dst_ref=self._landing(a, dst, me),
                    send_sem=send_sems.at[7 * a + k - 1], recv_sem=recv_sems.at[7 * a + k - 1],
                    device_id=(px, py, pc), device_id_type=MESH))
        return out


_ANY = pl.BlockSpec(memory_space=pl.ANY)


def _run_exchange(ex, name):
    n = len(ex.arrays)

    def body(*refs):
        cps = ex.copies(refs[:n], refs[n:2 * n], *refs[2 * n:])
        for cp in cps:
            cp.start()
        for cp in cps:
            cp.wait()

    return pl.pallas_call(body, out_shape=ex.out_shapes, in_specs=[_ANY] * n, out_specs=[_ANY] * n,
                          scratch_shapes=ex.scratch, name=name)(*ex.arrays)


def _call_beside(body, ex, nsteps, args, *, grid, in_specs, out_specs, out_shape, scratch_shapes, name):
    if ex is None:
        outs = pl.pallas_call(body, grid=grid, in_specs=in_specs, out_specs=out_specs, out_shape=out_shape,
                              scratch_shapes=scratch_shapes, compiler_params=_cparams(("arbitrary",)), name=name)(*args)
        return outs, None
    n_in, n_out, n_scr, n = len(in_specs), len(out_specs), len(scratch_shapes), len(ex.arrays)

    def wrapped(*refs):
        ins, refs = refs[:n_in], refs[n_in:]
        ex_ins, refs = refs[:n], refs[n:]
        outs, refs = refs[:n_out], refs[n_out:]
        ex_outs, refs = refs[:n], refs[n:]
        scr, sems = refs[:n_scr], refs[n_scr:]
        step = pl.program_id(0)

        @pl.when(step == 0)
        def _():
            for cp in ex.copies(ex_ins, ex_outs, *sems):
                cp.start()

        body(*ins, *outs, *scr)

        @pl.when(step == nsteps - 1)
        def _():
            for cp in ex.copies(ex_ins, ex_outs, *sems):
                cp.wait()

    res = pl.pallas_call(
        wrapped, grid=grid, in_specs=list(in_specs) + [_ANY] * n, out_specs=list(out_specs) + [_ANY] * n,
        out_shape=list(out_shape) + ex.out_shapes, scratch_shapes=list(scratch_shapes) + ex.scratch,
        compiler_params=_cparams(("arbitrary",)), name=name)(*args, *ex.arrays)
    return res[:n_out], res[n_out:]


def _adamw_math(w, g, m, v):
    m = ADAM_B1 * m + (1.0 - ADAM_B1) * g
    v = ADAM_B2 * v + (1.0 - ADAM_B2) * (g * g)
    m_hat = m / (1.0 - ADAM_B1 ** ADAM_STEP)
    v_hat = v / (1.0 - ADAM_B2 ** ADAM_STEP)
    delta = -ADAM_LR * (m_hat / (jnp.sqrt(v_hat) + ADAM_EPS) + ADAM_WD * w)
    return delta, m, v


def _sum_adamw(parts, w, m, v, tr, name):
    ns = len(parts)
    p, r, c = parts[0].shape
    tr = min(tr, r)
    assert r % tr == 0
    nt = r // tr

    def body(*refs):
        p_refs, (w_ref, m_ref, v_ref, g_ref, d_ref, nm_ref, nv_ref) = refs[:ns], refs[ns:]
        for s in range(ns):
            @pl.when(pl.program_id(0) == s)
            def _(p_ref=p_refs[s]):
                g = p_ref[0].astype(F32)
                for j in range(1, p):
                    g = g + p_ref[j].astype(F32)
                delta, nm, nv = _adamw_math(w_ref[...], g, m_ref[...], v_ref[...])
                g_ref[...] = g
                d_ref[...] = delta
                nm_ref[...] = nm
                nv_ref[...] = nv

    part_spec = lambda s: pl.BlockSpec((p, tr, c), lambda sec, i, s=s: (0, jnp.where(sec == s, i, 0), 0))
    row = pl.BlockSpec((tr, c), lambda sec, i: (sec * nt + i, 0))
    return pl.pallas_call(
        body, grid=(ns, nt), in_specs=[part_spec(s) for s in range(ns)] + [row, row, row],
        out_specs=[row] * 4, out_shape=[jax.ShapeDtypeStruct((ns * r, c), F32)] * 4,
        compiler_params=_cparams(("arbitrary", "arbitrary")), name=name)(*parts, w, m, v)


def _sum_parts(parts, name):
    p, r, c = parts.shape

    def body(p_ref, o_ref):
        g = p_ref[0]
        for j in range(1, p):
            g = g + p_ref[j]
        o_ref[...] = g

    return pl.pallas_call(body, out_shape=jax.ShapeDtypeStruct((r, c), F32), name=name)(parts)


def _rows_of(shape):
    n = 1
    for d in shape:
        n *= d
    return n, -(-n // 128)


def _pack(arrs):
    blocks = []
    for a in arrs:
        n, nr = _rows_of(a.shape)
        blocks.append(jnp.pad(a.reshape(-1).astype(F32), (0, nr * 128 - n)).reshape(nr, 128))
    rows = sum(b.shape[0] for b in blocks)
    if rows % 8:
        blocks.append(jnp.zeros((8 - rows % 8, 128), F32))
    return jnp.concatenate(blocks, axis=0)


def _unpack(buf, shapes):
    out, r0 = [], 0
    for s in shapes:
        n, nr = _rows_of(s)
        out.append(buf[r0:r0 + nr].reshape(-1)[:n].reshape(s))
        r0 += nr
    return out


def _block_diag(w):
    rows = [jnp.pad(w[i], ((0, 0), (i * C_BLOCK_DIM, C_WIDTH - (i + 1) * C_BLOCK_DIM))) for i in range(C_BLOCKS)]
    return jnp.concatenate(rows, axis=0)


def _diag_blocks(m):
    m4 = m.reshape(C_BLOCKS, C_BLOCK_DIM, C_BLOCKS, C_BLOCK_DIM)
    return jnp.stack([m4[i, :, i, :] for i in range(C_BLOCKS)])


def _gate_row(v):
    return jnp.pad(v.astype(F32), (B_HEADS, 128 - 2 * B_HEADS)).reshape(1, 128)


def _permute_w_in(w):
    pad = jnp.zeros(w.shape[:-1] + (D_IN_PAD - D_IN,), w.dtype)
    return jnp.concatenate([w[..., :3072], w[..., 3080:3592], w[..., 3072:3080], pad], axis=-1)


def _unpermute_w_in(w):
    return jnp.concatenate([w[..., :3072], w[..., COL_G:COL_G + 8], w[..., 3072:COL_G]], axis=-1)


_WEIGHTS = ['norm1_g', 'w_in', 'hgrn_lb_logits', 'hgrn_norm_g', 'gdn_conv_w', 'gdn_a_log', 'gdn_dt_bias', 'gdn_norm_g',
            'lru_conv_w', 'lru_conv_b', 'lru_w_a', 'lru_b_a', 'lru_w_x', 'lru_b_x', 'lru_lambda', 'w_out', 'norm2_g',
            'w_up', 'w_down', 'final_norm_g']
_BIG = ('w_in', 'w_out', 'w_up', 'w_down')
_SHARDED_SMALL = ('gdn_conv_w', 'lru_conv_w')


def _step(x, target, w, m, v):
    t = x.shape[0]
    mx, my, mc = _mesh_pos()
    me = 4 * mx + 2 * my + mc

    bf = lambda a: a.astype(BF16)

    def full_w_in(g):
        return _permute_w_in(jnp.moveaxis(g, 0, 1).reshape(D_MODEL, D_IN))

    conv_shapes = [w['gdn_conv_w'].shape, w['lru_conv_w'].shape]
    g_in, g_conv = _run_exchange(
        _Exchange([bf(w['w_in'][0]), _pack([w['gdn_conv_w'], w['lru_conv_w']])], ['slot', 'slot']), "gather_first")
    w_in = [full_w_in(g_in)]
    w_out, w_up, w_down = [], [], []
    gdn_cw, lru_cw = [], []
    for j in range(N_DEV):
        a, b = _unpack(g_conv[j], conv_shapes)
        gdn_cw.append(a)
        lru_cw.append(b)
    gdn_cw = jnp.concatenate(gdn_cw, axis=-1)
    lru_cw = jnp.concatenate(lru_cw, axis=-1)

    lbnd = _lb_fwd(w['hgrn_lb_logits'])
    row = lambda a: a.reshape(1, -1)

    def c_args(l):
        return (lru_cw[l], row(w['lru_conv_b'][l]), _block_diag(w['lru_w_a'][l]), row(w['lru_b_a'][l]),
                _block_diag(w['lru_w_x'][l]), row(w['lru_b_x'][l]), row(w['lru_lambda'][l]))

    saved = []
    xl = x
    h = _rms_fwd(x, w['norm1_g'][0], name="rms_fwd")
    for l in range(DEPTH):
        proj = _mm_rows(h, w_in[l], "nn", 256, "mm_proj")
        mix_a, st_a = _a_fwd(proj, lbnd[l], w['hgrn_norm_g'][l])
        alr, dtr = _gate_row(w['gdn_a_log'][l]), _gate_row(w['gdn_dt_bias'][l])
        qkv, gates, y3 = _b_pre_fwd(proj, gdn_cw[l], alr, dtr)
        nxt = [bf(w['w_in'][l + 1])] if l + 1 < DEPTH else []
        gather = _Exchange([bf(w['w_out'][l]), bf(w['w_up'][l]), bf(w['w_down'][l])] + nxt, ['rows', 'cols', 'rows'] + ['slot'] * len(nxt))
        (mix_b, st_b, *b_saved), got = _b_fwd(qkv, gates, proj, w['gdn_norm_g'][l], beside=gather)
        w_out.append(got[0])
        w_up.append(got[1])
        w_down.append(got[2])
        if nxt:
            w_in.append(full_w_in(got[3]))
        mix_c, hs = _c_fwd(proj, *c_args(l))
        mixed = [mix_a, mix_b, mix_c]
        x_mid, h2 = _mm_rows(mixed, w_out[l], "nn", 1024, "mm_out", residual=xl, epilogue="rms_fwd", norm=w['norm2_g'][l])
        act, up = _mm_rows(h2, w_up[l], "nn", 256, "mm_up", epilogue="relu2")
        saved.append(dict(x=xl, h=h, proj=proj, st_a=st_a, qkv=qkv, gates=gates, y3=y3, st_b=st_b, b_saved=b_saved, hs=hs, mixed=mixed,
                          x_mid=x_mid, h2=h2, up=up, act=act, alr=alr, dtr=dtr))
        if l + 1 < DEPTH:
            xl, h = _mm_rows(act, w_down[l], "nn", 512, "mm_down", residual=x_mid, epilogue="rms_fwd", norm=w['norm1_g'][l + 1])
        else:
            xl = _mm_rows(act, w_down[l], "nn", 512, "mm_down_last", residual=x_mid)
    loss, dx, dgf = _loss_head(xl, w['final_norm_g'], target)

    gs = {n: [None] * DEPTH for n in _WEIGHTS}
    recv = {n: [None] * DEPTH for n in _BIG}
    dw_in_above = None
    for l in reversed(range(DEPTH)):
        s = saved[l]
        dup = _mm_rows(dx, w_down[l], "nt", 256, "mm_dact", epilogue="drelu2", up=s['up'])
        dw_down = _mm_tn(s['act'], dx, 512, "mm_dw_down").reshape(N_DEV, D_FF // N_DEV, D_MODEL)
        dx_mid, dg2 = _mm_rows(dup, w_up[l], "nt", 512, "mm_dh2", epilogue="rms_bwd", norm=(s['x_mid'], w['norm2_g'][l], dx))
        dw_up = _mm_tn(s['h2'], dup, 512, "mm_dw_up", slab=D_FF // N_DEV)
        gs['norm2_g'][l] = dg2[0]
        dmixed = _mm_rows(dx_mid, w_out[l], "nt", 1024, "mm_dmixed")
        dw_out = _mm_tn(s['mixed'], dx_mid, 1024, "mm_dw_out").reshape(N_DEV, D_MODEL // N_DEV, D_MODEL)
        proj = s['proj']
        above = [dw_in_above] if dw_in_above is not None else []
        (dpa, dlb, dnga), got = _a_bwd(proj, lbnd[l], w['hgrn_norm_g'][l], s['st_a'], dmixed,
                                       beside=_Exchange([dw_out] + above, ['a2a'] * (1 + len(above))))
        recv['w_out'][l] = got[0]
        if above:
            recv['w_in'][l + 1] = got[1]
        gs['hgrn_lb_logits'][l] = dlb[0]
        gs['hgrn_norm_g'][l] = dnga[0, :A_DIM] + dnga[0, A_DIM:]
        (dqkv, dgates, dz, dngb), got = _b_bwd(s['qkv'], s['gates'], proj, w['gdn_norm_g'][l], s['st_b'], s['b_saved'], dmixed,
                                               beside=_Exchange([dw_up, dw_down], ['a2a', 'a2a']))
        recv['w_up'][l], recv['w_down'][l] = got
        dxb, dgi, dcwb, dal, ddt = _b_pre_bwd(proj, s['y3'], gdn_cw[l], s['alr'], s['dtr'], dqkv, dgates)
        gs['gdn_norm_g'][l] = dngb[0]
        gs['gdn_conv_w'][l] = dcwb
        gs['gdn_a_log'][l] = dal[0, B_HEADS:2 * B_HEADS]
        gs['gdn_dt_bias'][l] = ddt[0, B_HEADS:2 * B_HEADS]
        dxc_in, dyg, dcwc, dcb, dwa, dba, dwx, dbx, dlam = _c_bwd(proj, *c_args(l), s['hs'], dmixed)
        gs['lru_conv_w'][l] = dcwc
        gs['lru_conv_b'][l] = dcb[0]
        gs['lru_w_a'][l] = _diag_blocks(dwa)
        gs['lru_b_a'][l] = dba[0]
        gs['lru_w_x'][l] = _diag_blocks(dwx)
        gs['lru_b_x'][l] = dbx[0]
        gs['lru_lambda'][l] = dlam[0]
        dproj = [dpa, dxb, dz, dxc_in, dyg, dgi]
        dx, dg1 = _mm_rows(dproj, w_in[l], "nt", 512, "mm_dh", epilogue="rms_bwd", norm=(s['x'], w['norm1_g'][l], dx_mid))
        dw_in = _unpermute_w_in(_mm_tn(s['h'], dproj, 512, "mm_dw_in"))
        dw_in_above = jnp.moveaxis(dw_in.reshape(D_MODEL, N_DEV, D_IN // N_DEV), 1, 0)
        gs['norm1_g'][l] = dg1[0]
    grad_x = dx
    part = {n: jnp.stack(gs[n]) for n in _WEIGHTS if n != 'final_norm_g' and n not in _BIG}
    part['final_norm_g'] = dgf[0]
    part['hgrn_lb_logits'] = _lb_bwd(w['hgrn_lb_logits'], part['hgrn_lb_logits'])

    small = [n for n in _WEIGHTS if n not in _BIG]
    packed = _pack([part[n] for n in small] + [loss])
    recv['w_in'][0], all_small = _run_exchange(_Exchange([dw_in_above, packed], ['a2a', 'slot']), "exchange_last")

    grads, deltas, new_m, new_v = {}, {}, {}, {}
    for n in _BIG:
        shp = w[n].shape
        r2 = lambda a: a.reshape(-1, shp[-1])
        g, d, nm, nv = _sum_adamw(recv[n], r2(w[n]), r2(m[n]), r2(v[n]), 256, "adamw_" + n)
        grads[n], deltas[n], new_m[n], new_v[n] = (a.reshape(shp) for a in (g, d, nm, nv))

    total = _sum_parts(all_small, "sum_small")
    summed = _unpack(total, [part[n].shape for n in small] + [(1, 1)])
    loss_total = summed[-1].reshape(())
    gsmall = dict(zip(small, summed[:-1]))
    for n in _SHARDED_SMALL:
        width = w[n].shape[-1]
        gsmall[n] = lax.dynamic_slice_in_dim(gsmall[n], me * width, width, axis=2)
    pk = lambda d: _pack([d[n] for n in small])
    _, d, nm, nv = _sum_adamw([pk(gsmall)[None]], pk(w), pk(m), pk(v), 4096, "adamw_small")
    shapes = [w[n].shape for n in small]
    for n, dd, mm, vv in zip(small, _unpack(d, shapes), _unpack(nm, shapes), _unpack(nv, shapes)):
        grads[n], deltas[n], new_m[n], new_v[n] = gsmall[n], dd, mm, vv
    return loss_total, grad_x, grads, deltas, new_m, new_v


def kernel(x, norm1_g, w_in, hgrn_lb_logits, hgrn_norm_g, gdn_conv_w, gdn_a_log, gdn_dt_bias, gdn_norm_g, lru_conv_w, lru_conv_b, lru_w_a, lru_b_a, lru_w_x, lru_b_x, lru_lambda, w_out, norm2_g, w_up, w_down, final_norm_g, loss_target, m_norm1_g, m_w_in, m_hgrn_lb_logits, m_hgrn_norm_g, m_gdn_conv_w, m_gdn_a_log, m_gdn_dt_bias, m_gdn_norm_g, m_lru_conv_w, m_lru_conv_b, m_lru_w_a, m_lru_b_a, m_lru_w_x, m_lru_b_x, m_lru_lambda, m_w_out, m_norm2_g, m_w_up, m_w_down, m_final_norm_g, v_norm1_g, v_w_in, v_hgrn_lb_logits, v_hgrn_norm_g, v_gdn_conv_w, v_gdn_a_log, v_gdn_dt_bias, v_gdn_norm_g, v_lru_conv_w, v_lru_conv_b, v_lru_w_a, v_lru_b_a, v_lru_w_x, v_lru_b_x, v_lru_lambda, v_w_out, v_norm2_g, v_w_up, v_w_down, v_final_norm_g):
    w = dict(zip(_WEIGHTS, (norm1_g, w_in, hgrn_lb_logits, hgrn_norm_g, gdn_conv_w, gdn_a_log, gdn_dt_bias, gdn_norm_g, lru_conv_w, lru_conv_b, lru_w_a, lru_b_a, lru_w_x, lru_b_x, lru_lambda, w_out, norm2_g, w_up, w_down, final_norm_g)))
    m = dict(zip(_WEIGHTS, (m_norm1_g, m_w_in, m_hgrn_lb_logits, m_hgrn_norm_g, m_gdn_conv_w, m_gdn_a_log, m_gdn_dt_bias, m_gdn_norm_g, m_lru_conv_w, m_lru_conv_b, m_lru_w_a, m_lru_b_a, m_lru_w_x, m_lru_b_x, m_lru_lambda, m_w_out, m_norm2_g, m_w_up, m_w_down, m_final_norm_g)))
    v = dict(zip(_WEIGHTS, (v_norm1_g, v_w_in, v_hgrn_lb_logits, v_hgrn_norm_g, v_gdn_conv_w, v_gdn_a_log, v_gdn_dt_bias, v_gdn_norm_g, v_lru_conv_w, v_lru_conv_b, v_lru_w_a, v_lru_b_a, v_lru_w_x, v_lru_b_x, v_lru_lambda, v_w_out, v_norm2_g, v_w_up, v_w_down, v_final_norm_g)))
    loss, grad_x, grads, deltas, new_m, new_v = _step(x.reshape(x.shape[1:]), loss_target.reshape(x.shape[1:]), w, m, v)
    return (loss, grad_x[None], *[grads[n] for n in _WEIGHTS], *[deltas[n] for n in _WEIGHTS],
            *[new_m[n] for n in _WEIGHTS], *[new_v[n] for n in _WEIGHTS])
```

```python
import jax
import jax.numpy as jnp
from jax import lax
from jax.experimental import pallas as pl
from jax.experimental.pallas import tpu as pltpu

F32 = jnp.float32
BF16 = jnp.bfloat16
MESH = pl.DeviceIdType.MESH

N_DEV = 8
D_MODEL = 1024
DEPTH = 4
A_HEADS, A_DIM, A_WIDTH = 4, 64, 256
B_HEADS, B_DIM, B_WIDTH = 4, 128, 512
C_WIDTH, C_BLOCKS, C_BLOCK_DIM = 256, 4, 64
D_IN = 3592
D_IN_PAD = 3840
COL_A, COL_B, COL_C, COL_G = 0, 1024, 3072, 3584
D_FF = 4096
CONV_K = 4
CHUNK = 64
SUB = 16
RG_C = 8.0
EPS = 1e-6
TINY = 1e-30
EXP_CLAMP = 80.0
GDN_SCALE = B_DIM ** -0.5
ADAM_LR, ADAM_B1, ADAM_B2, ADAM_EPS, ADAM_WD, ADAM_STEP = 0.001, 0.9, 0.999, 1e-08, 0.01, 10
VMEM_LIMIT = 56 * 1024 * 1024


def _cparams(sem=None):
    return pltpu.CompilerParams(dimension_semantics=sem, vmem_limit_bytes=VMEM_LIMIT)


_DIMS = {"nn": (((1,), (0,)), ((), ())), "nt": (((1,), (1,)), ((), ())), "tn": (((0,), (0,)), ((), ()))}


def _split_bf16(x):
    hi = x.astype(BF16)
    return hi, (x - hi.astype(F32)).astype(BF16)


def _dot(a, b, mode="nn", hi=False):
    if not hi:
        return lax.dot_general(a.astype(BF16), b.astype(BF16), _DIMS[mode], preferred_element_type=F32)
    ah, al = _split_bf16(a.astype(F32))
    bh, bl = _split_bf16(b.astype(F32))
    ka = 0 if mode == "tn" else 1
    kb = 1 if mode == "nt" else 0
    return lax.dot_general(jnp.concatenate([ah, ah, al], axis=ka), jnp.concatenate([bh, bl, bh], axis=kb),
                           _DIMS[mode], preferred_element_type=F32)


def _dot_exact_lhs(lhs, x, mode="nn"):
    l_bf16 = lhs.astype(BF16)
    x1 = x.astype(BF16)
    r1 = x - x1.astype(F32)
    x2 = r1.astype(BF16)
    x3 = (r1 - x2.astype(F32)).astype(BF16)
    ka = 0 if mode == "tn" else 1
    return lax.dot_general(jnp.concatenate([l_bf16] * 3, axis=ka), jnp.concatenate([x1, x2, x3], axis=0),
                           _DIMS[mode], preferred_element_type=F32)


def _iota2(n, m):
    return lax.broadcasted_iota(jnp.int32, (n, m), 0), lax.broadcasted_iota(jnp.int32, (n, m), 1)


def _tril(n, strict=False):
    r, c = _iota2(n, n)
    return (r > c) if strict else (r >= c)


def _sigmoid(x):
    return 1.0 / (1.0 + jnp.exp(-x))


def _softplus(x):
    return jnp.maximum(x, 0.0) + jnp.log(1.0 + jnp.exp(-jnp.abs(x)))


def _neg_expm1(z):
    series = -z * (1.0 + z * (0.5 + z * (1.0 / 6.0)))
    return jnp.where(z > -1e-2, series, 1.0 - jnp.exp(z))


def _gelu_tanh(x):
    c = 0.7978845608028654
    u = c * (x + 0.044715 * x * x * x)
    t = jnp.tanh(u)
    g = 0.5 * x * (1.0 + t)
    dg = 0.5 * (1.0 + t) + 0.5 * x * (1.0 - t * t) * c * (1.0 + 3.0 * 0.044715 * x * x)
    return g, dg


def _shift_rows(cur, halo, s, down=True):
    n = cur.shape[0]
    ridx = lax.broadcasted_iota(jnp.int32, (8, cur.shape[1]), 0)
    if down:
        main = pltpu.roll(cur, s, 0)
        fix = jnp.where(ridx < s, pltpu.roll(halo, s, 0), main[0:8])
        return jnp.concatenate([fix, main[8:]], axis=0)
    main = pltpu.roll(cur, n - s, 0)
    fix = jnp.where(ridx >= 8 - s, pltpu.roll(halo, 8 - s, 0), main[n - 8:n])
    return jnp.concatenate([main[:n - 8], fix], axis=0)


def _later_rows(dy, nxt8):
    return [_shift_rows(dy, nxt8, 3 - j, down=False) for j in range(3)] + [dy]


def _conv_bwd_rows(dy, nxt8, w):
    return sum(d * w[j:j + 1] for j, d in enumerate(_later_rows(dy, nxt8)))


def _tile_scan(a, b, reverse=False):
    n = a.shape[0]
    r = lax.broadcasted_iota(jnp.int32, a.shape, 0) % 8
    for s in (1, 2, 4):
        keep = (r < 8 - s) if reverse else (r >= s)
        shift = n - s if reverse else s
        a_sh = jnp.where(keep, pltpu.roll(a, shift, 0), 1.0)
        b_sh = jnp.where(keep, pltpu.roll(b, shift, 0), 0.0)
        b = b + a * b_sh
        a = a * a_sh
    return a, b


def _conv_fwd(cur, prev8, w):
    y = cur * w[3:4]
    for j in range(3):
        y = y + _shift_rows(cur, prev8, 3 - j, down=True) * w[j:j + 1]
    return y


def _mm_rows(a, w, mode, tm, name, residual=None, epilogue=None, up=None, norm=None, beside=None):
    parts = list(a) if isinstance(a, (list, tuple)) else [a]
    widths = [p.shape[1] for p in parts]
    t = parts[0].shape[0]
    n = w.shape[1] if mode == "nn" else w.shape[0]
    tm = min(tm, t)
    assert t % tm == 0 and all(wd % 128 == 0 for wd in widths)

    def body(*refs):
        a_refs, w_ref, rest = refs[:len(parts)], refs[len(parts)], refs[len(parts) + 1:]
        y, off = None, 0
        for a_ref, width in zip(a_refs, widths):
            wk = w_ref[off:off + width, :] if mode == "nn" else w_ref[:, off:off + width]
            d = _dot(a_ref[...], wk, mode)
            y = d if y is None else y + d
            off += width
        if residual is not None:
            y = y + rest[0][...]
        if epilogue == "relu2":
            r = jnp.maximum(y, 0.0)
            refs[-2][...] = (r * r).astype(BF16)
            refs[-1][...] = y.astype(BF16)
        elif epilogue == "drelu2":
            refs[-1][...] = (y * 2.0 * jnp.maximum(rest[0][...].astype(F32), 0.0)).astype(BF16)
        elif epilogue == "rms_fwd":
            rinv = lax.rsqrt(jnp.mean(y * y, axis=-1, keepdims=True) + EPS)
            refs[-2][...] = y
            refs[-1][...] = (y * rinv * refs[-3][...]).astype(BF16)
        elif epilogue == "rms_bwd":
            x_ref, g_ref, dres_ref, dx_ref, dg_ref = rest

            @pl.when(pl.program_id(0) == 0)
            def _():
                dg_ref[...] = jnp.zeros_like(dg_ref)

            xv = x_ref[...]
            rinv = lax.rsqrt(jnp.mean(xv * xv, axis=-1, keepdims=True) + EPS)
            xhat = xv * rinv
            dxh = y * g_ref[...]
            dx_ref[...] = dres_ref[...] + rinv * (dxh - xhat * jnp.mean(dxh * xhat, axis=-1, keepdims=True))
            dg_ref[...] += jnp.sum(y * xhat, axis=0, keepdims=True)
        else:
            refs[-1][...] = y

    rows = lambda width: pl.BlockSpec((tm, width), lambda i: (i, 0))
    vec = pl.BlockSpec((1, n), lambda i: (0, 0))
    ins, specs = parts + [w], [rows(wd) for wd in widths] + [pl.BlockSpec(w.shape, lambda i: (0, 0))]
    if residual is not None:
        ins.append(residual)
        specs.append(rows(n))
    if epilogue == "drelu2":
        ins.append(up)
        specs.append(rows(n))
    if epilogue == "rms_fwd":
        ins.append(norm.reshape(1, n))
        specs.append(vec)
        out_specs, out_shape = [rows(n), rows(n)], [jax.ShapeDtypeStruct((t, n), F32), jax.ShapeDtypeStruct((t, n), BF16)]
    elif epilogue == "rms_bwd":
        ins += [norm[0], norm[1].reshape(1, n), norm[2]]
        specs += [rows(n), vec, rows(n)]
        out_specs, out_shape = [rows(n), vec], [jax.ShapeDtypeStruct((t, n), F32), jax.ShapeDtypeStruct((1, n), F32)]
    elif epilogue == "relu2":
        out_specs, out_shape = [rows(n), rows(n)], [jax.ShapeDtypeStruct((t, n), BF16)] * 2
    else:
        out_specs, out_shape = rows(n), jax.ShapeDtypeStruct((t, n), BF16 if epilogue == "drelu2" else F32)
    if beside is not None:
        assert epilogue == "rms_bwd"
        return _call_beside(body, beside, t // tm, ins, grid=(t // tm,), in_specs=specs, out_specs=out_specs,
                            out_shape=out_shape, scratch_shapes=[], name=name)
    return pl.pallas_call(body, grid=(t // tm,), in_specs=specs, out_specs=out_specs, out_shape=out_shape,
                          compiler_params=_cparams(("arbitrary" if epilogue == "rms_bwd" else "parallel",)), name=name)(*ins)


MM_TN_TILE = 1024


def _mm_tn(a, b, tk, name, slab=None):
    a_parts = list(a) if isinstance(a, (list, tuple)) else [a]
    b_parts = list(b) if isinstance(b, (list, tuple)) else [b]
    wa, wb = [p.shape[1] for p in a_parts], [p.shape[1] for p in b_parts]
    t, m, n = a_parts[0].shape[0], sum(wa), sum(wb)
    tk = min(tk, t)
    assert t % tk == 0 and all(x % 128 == 0 for x in wa + wb)
    nk = t // tk

    def body(*refs):
        a_refs, b_refs = refs[:len(wa)], refs[len(wa):len(wa) + len(wb)]
        o_ref, acc = refs[-2], refs[-1]
        kk = pl.program_id(0)

        @pl.when(kk == 0)
        def _():
            acc[...] = jnp.zeros_like(acc)

        ro = 0
        for a_ref, width_a in zip(a_refs, wa):
            for r0 in range(0, width_a, MM_TN_TILE):
                rw = min(MM_TN_TILE, width_a - r0)
                av = a_ref[:, r0:r0 + rw]
                co = 0
                for b_ref, width_b in zip(b_refs, wb):
                    for c0 in range(0, width_b, MM_TN_TILE):
                        cw = min(MM_TN_TILE, width_b - c0)
                        acc[ro + r0:ro + r0 + rw, co + c0:co + c0 + cw] += _dot(av, b_ref[:, c0:c0 + cw], "tn")
                    co += width_b
            ro += width_a

        @pl.when(kk == nk - 1)
        def _():
            if slab is None:
                o_ref[...] = acc[...].astype(BF16)
            else:
                for s in range(n // slab):
                    o_ref[s] = acc[:, s * slab:(s + 1) * slab].astype(BF16)

    if slab is None:
        out_spec, out_shape = pl.BlockSpec((m, n), lambda kk: (0, 0)), jax.ShapeDtypeStruct((m, n), BF16)
    else:
        out_spec, out_shape = pl.BlockSpec((n // slab, m, slab), lambda kk: (0, 0, 0)), jax.ShapeDtypeStruct((n // slab, m, slab), BF16)
    return pl.pallas_call(
        body, grid=(nk,),
        in_specs=[pl.BlockSpec((tk, x), lambda kk: (kk, 0)) for x in wa + wb],
        out_specs=out_spec, out_shape=out_shape, scratch_shapes=[pltpu.VMEM((m, n), F32)],
        compiler_params=_cparams(("arbitrary",)), name=name)(*a_parts, *b_parts)


def _rms_fwd(x, g, tb=512, name="rms_fwd"):
    t, d = x.shape

    def body(x_ref, g_ref, h_ref):
        xv = x_ref[...]
        rinv = lax.rsqrt(jnp.mean(xv * xv, axis=-1, keepdims=True) + EPS)
        h_ref[...] = (xv * rinv * g_ref[...]).astype(BF16)

    return pl.pallas_call(
        body, grid=(t // tb,), in_specs=[pl.BlockSpec((tb, d), lambda i: (i, 0)), pl.BlockSpec((1, d), lambda i: (0, 0))],
        out_specs=pl.BlockSpec((tb, d), lambda i: (i, 0)), out_shape=jax.ShapeDtypeStruct((t, d), BF16),
        compiler_params=_cparams(("parallel",)), name=name)(x, g.reshape(1, d))


def _loss_head(x, g, target, tb=512):
    t, d = x.shape

    def body(x_ref, g_ref, t_ref, loss_ref, dx_ref, dg_ref):
        @pl.when(pl.program_id(0) == 0)
        def _():
            dg_ref[...] = jnp.zeros_like(dg_ref)
            loss_ref[...] = jnp.zeros_like(loss_ref)

        xv = x_ref[...]
        rinv = lax.rsqrt(jnp.mean(xv * xv, axis=-1, keepdims=True) + EPS)
        xhat = xv * rinv
        err = xhat * g_ref[...] - t_ref[...]
        loss_ref[...] += 0.5 * jnp.sum(jnp.mean(err * err, axis=-1, keepdims=True), axis=0, keepdims=True)
        dy = err * (1.0 / d)
        dxh = dy * g_ref[...]
        dx_ref[...] = rinv * (dxh - xhat * jnp.mean(dxh * xhat, axis=-1, keepdims=True))
        dg_ref[...] += jnp.sum(dy * xhat, axis=0, keepdims=True)

    row = pl.BlockSpec((tb, d), lambda i: (i, 0))
    vec = pl.BlockSpec((1, d), lambda i: (0, 0))
    one = pl.BlockSpec((1, 1), lambda i: (0, 0))
    return pl.pallas_call(
        body, grid=(t // tb,), in_specs=[row, vec, row], out_specs=[one, row, vec],
        out_shape=[jax.ShapeDtypeStruct((1, 1), F32), jax.ShapeDtypeStruct((t, d), F32), jax.ShapeDtypeStruct((1, d), F32)],
        compiler_params=_cparams(("arbitrary",)), name="loss_head")(x, g.reshape(1, d), target)


def _lb_fwd(logits):
    def body(l_ref, o_ref):
        lg = l_ref[...]
        e = jnp.exp(lg - jnp.max(lg, axis=0, keepdims=True))
        p = e / jnp.sum(e, axis=0, keepdims=True)
        c = jnp.zeros_like(p[0:1])
        rows = [c]
        for l in range(1, DEPTH):
            c = c + p[l:l + 1]
            rows.append(c)
        o_ref[...] = jnp.minimum(jnp.maximum(jnp.concatenate(rows, axis=0), 0.0), 1.0 - EPS)

    return pl.pallas_call(body, out_shape=jax.ShapeDtypeStruct(logits.shape, F32), name="lb_fwd")(logits)


def _lb_bwd(logits, dlb):
    def body(l_ref, d_ref, o_ref):
        lg = l_ref[...]
        e = jnp.exp(lg - jnp.max(lg, axis=0, keepdims=True))
        p = e / jnp.sum(e, axis=0, keepdims=True)
        hi = 1.0 - EPS
        c = jnp.zeros_like(p[0:1])
        dc = []
        for l in range(1, DEPTH):
            c = c + p[l:l + 1]
            gl = jnp.where(c < 0.0, 0.0, jnp.where(c == 0.0, 0.5, 1.0)) * jnp.where(c > hi, 0.0, jnp.where(c == hi, 0.5, 1.0))
            dc.append(d_ref[l:l + 1, :] * gl)
        dp = [jnp.zeros_like(c)]
        for j in range(1, DEPTH):
            s = dc[j - 1]
            for l in range(j + 1, DEPTH):
                s = s + dc[l - 1]
            dp.append(s)
        dpm = jnp.concatenate(dp, axis=0)
        o_ref[...] = p * (dpm - jnp.sum(p * dpm, axis=0, keepdims=True))

    return pl.pallas_call(body, out_shape=jax.ShapeDtypeStruct(logits.shape, F32), name="lb_bwd")(logits, dlb)


def _a_gates(qi, fi, lbh):
    sq = _sigmoid(qi)
    q = qi * sq
    e = jnp.exp(-jnp.abs(fi))
    rec = 1.0 / (1.0 + e)
    pos = fi >= 0.0
    sg = jnp.where(pos, rec, e * rec)
    sgn = jnp.where(pos, e * rec, rec)
    f = lbh + (1.0 - lbh) * sg
    logf = jnp.log(jnp.maximum(f, TINY))
    k = (1.0 - lbh) * sgn
    return q, sq, sg, sgn, f, logf, k


def _headnorm_fwd(o, g, gate_in):
    rinv = lax.rsqrt(jnp.mean(o * o, axis=-1, keepdims=True) + EPS)
    sg = _sigmoid(gate_in)
    return o * rinv * g * (gate_in * sg)


def _headnorm_bwd(dout, o, g, gate_in):
    rinv = lax.rsqrt(jnp.mean(o * o, axis=-1, keepdims=True) + EPS)
    xhat = o * rinv
    sg = _sigmoid(gate_in)
    silu = gate_in * sg
    dy = dout * silu
    dgate = dout * xhat * g * (sg * (1.0 + gate_in * (1.0 - sg)))
    dxh = dy * g
    do = rinv * (dxh - xhat * jnp.mean(dxh * xhat, axis=-1, keepdims=True))
    return do, dgate, jnp.sum(dy * xhat, axis=0, keepdims=True)


A_PAIRS, A_PAIR_W = A_HEADS // 2, 2 * A_DIM


def _lo_half(shape):
    return lax.broadcasted_iota(jnp.int32, shape, len(shape) - 1) < A_DIM


def _pair_blockdiag(x):
    lo = _lo_half(x.shape)
    return jnp.concatenate([jnp.where(lo, x, 0.0), jnp.where(lo, 0.0, x)], axis=0)


def _pair_fold(m):
    n = m.shape[0] // 2
    return jnp.where(_lo_half((n, A_PAIR_W)), m[:n], m[n:])


def _pair_norm_stats(o):
    lo = _lo_half(o.shape)
    sq = o * o
    s0 = jnp.sum(jnp.where(lo, sq, 0.0), axis=-1, keepdims=True)
    s1 = jnp.sum(sq, axis=-1, keepdims=True) - s0
    return jnp.where(lo, lax.rsqrt(s0 * (1.0 / A_DIM) + EPS), lax.rsqrt(s1 * (1.0 / A_DIM) + EPS))


def _pair_mean(x):
    lo = _lo_half(x.shape)
    s0 = jnp.sum(jnp.where(lo, x, 0.0), axis=-1, keepdims=True)
    s1 = jnp.sum(x, axis=-1, keepdims=True) - s0
    return jnp.where(lo, s0, s1) * (1.0 / A_DIM)


def _a_pair_chunk(qi, fi, v, lb2, s_bd, ltri, causal2):
    q, sq, sg, sgn, f, logf, k = _a_gates(qi, fi, lb2)
    cum = _dot_exact_lhs(ltri, logf)
    cl = cum[CHUNK - 1:CHUNK]
    ecum, ekd, cd = jnp.exp(cum), jnp.exp(cl - cum), jnp.exp(cl)
    qd, kd = q * ecum, k * ekd
    subs, rows = [], []
    for i in range(CHUNK // SUB):
        lo = i * SUB
        r = cum[lo - 1:lo] if i > 0 else jnp.zeros_like(cl)
        eq = jnp.exp(cum[lo:lo + SUB] - r)
        ek = jnp.exp(jnp.minimum(r - cum, EXP_CLAMP))
        qt = q[lo:lo + SUB] * eq
        kt_bd = _pair_blockdiag(k * ek)
        rows.append(_dot(qt, kt_bd, "nt", hi=True))
        subs.append((qt, eq, kt_bd, ek))
    attn = jnp.where(causal2, jnp.concatenate(rows, axis=0), 0.0)
    v_bd = _pair_blockdiag(v)
    o = _dot(qd, s_bd) + _dot(attn, v_bd)
    return dict(q=q, sq=sq, sg=sg, sgn=sgn, f=f, k=k, cum=cum, cl=cl, ecum=ecum, ekd=ekd, cd=cd, qd=qd, kd=kd,
                subs=subs, attn=attn, v_bd=v_bd, o=o)


def _a_fwd(proj, lb, norm_g, tb=256):
    t = proj.shape[0]
    nch = tb // CHUNK

    def body(q_ref, f_ref, i_ref, g_ref, lb_ref, ng_ref, out_ref, st_ref, s_scr):
        @pl.when(pl.program_id(0) == 0)
        def _():
            s_scr[...] = jnp.zeros_like(s_scr)

        ltri = _tril(CHUNK).astype(F32)
        r, c = _iota2(CHUNK, A_PAIR_W)
        causal2 = r >= c % CHUNK
        rb, cb = _iota2(A_PAIR_W, A_PAIR_W)
        diag_blocks = (rb < A_DIM) == (cb < A_DIM)

        def chunk(c, carry):
            rows = pl.ds(pl.multiple_of(c * CHUNK, CHUNK), CHUNK)
            ps = range(A_PAIRS)
            cols = [slice(p * A_PAIR_W, (p + 1) * A_PAIR_W) for p in ps]
            s0 = [s_scr[p] for p in ps]
            for p in ps:
                st_ref[c, p] = s0[p]
            v = [i_ref[rows, cols[p]] for p in ps]
            ch = [_a_pair_chunk(q_ref[rows, cols[p]], f_ref[rows, cols[p]], v[p], lb_ref[:, cols[p]], s0[p], ltri, causal2)
                  for p in ps]
            for p in ps:
                upd = jnp.where(diag_blocks, _dot(ch[p]["kd"], v[p], "tn"), 0.0)
                s_scr[p] = s0[p] * ch[p]["cd"].T + upd
            outs = []
            for p in ps:
                gi = g_ref[rows, cols[p]]
                outs.append(ch[p]["o"] * _pair_norm_stats(ch[p]["o"]) * ng_ref[...] * (gi * _sigmoid(gi)))
            out_ref[rows, :] = jnp.concatenate(outs, axis=1).astype(BF16)
            return carry

        lax.fori_loop(0, nch, chunk, 0, unroll=2)

    colblk = lambda j: pl.BlockSpec((tb, A_WIDTH), lambda i, j=j: (i, j))
    return pl.pallas_call(
        body, grid=(t // tb,),
        in_specs=[colblk(0), colblk(1), colblk(2), colblk(3), pl.BlockSpec((1, A_WIDTH), lambda i: (0, 0)),
                  pl.BlockSpec((1, A_PAIR_W), lambda i: (0, 0))],
        out_specs=[pl.BlockSpec((tb, A_WIDTH), lambda i: (i, 0)),
                   pl.BlockSpec((nch, A_PAIRS, A_PAIR_W, A_PAIR_W), lambda i: (i, 0, 0, 0))],
        out_shape=[jax.ShapeDtypeStruct((t, A_WIDTH), BF16),
                   jax.ShapeDtypeStruct((t // CHUNK, A_PAIRS, A_PAIR_W, A_PAIR_W), F32)],
        scratch_shapes=[pltpu.VMEM((A_PAIRS, A_PAIR_W, A_PAIR_W), F32)],
        compiler_params=_cparams(("arbitrary",)), name="hgrn_fwd")(
            proj, proj, proj, proj, lb.reshape(1, A_WIDTH), jnp.tile(norm_g.reshape(1, A_DIM), (1, 2)))


def _a_bwd(proj, lb, norm_g, states, dmixed, beside=None, tb=256):
    t = proj.shape[0]
    nch = tb // CHUNK
    nb = t // tb

    def body(q_ref, f_ref, i_ref, g_ref, lb_ref, ng_ref, st_ref, dm_ref, dp_ref, dlb_ref, dng_ref, ds_scr):
        @pl.when(pl.program_id(0) == 0)
        def _():
            ds_scr[...] = jnp.zeros_like(ds_scr)
            dlb_ref[...] = jnp.zeros_like(dlb_ref)
            dng_ref[...] = jnp.zeros_like(dng_ref)

        ltri = _tril(CHUNK).astype(F32)
        r, c = _iota2(CHUNK, A_PAIR_W)
        causal2 = r >= c % CHUNK
        rb, cb = _iota2(A_PAIR_W, A_PAIR_W)
        diag_blocks = (rb < A_DIM) == (cb < A_DIM)
        ones8 = jnp.ones((8, A_PAIR_W), F32)

        def chunk(cc, carry):
            c = nch - 1 - cc
            rows = pl.ds(pl.multiple_of(c * CHUNK, CHUNK), CHUNK)
            ps = range(A_PAIRS)
            cols = [slice(p * A_PAIR_W, (p + 1) * A_PAIR_W) for p in ps]
            qi = [q_ref[rows, cols[p]] for p in ps]
            gi = [g_ref[rows, cols[p]] for p in ps]
            v = [i_ref[rows, cols[p]] for p in ps]
            lb2 = [lb_ref[:, cols[p]] for p in ps]
            s0 = [st_ref[c, p] for p in ps]
            ds = [ds_scr[p] for p in ps]
            ch = [_a_pair_chunk(qi[p], f_ref[rows, cols[p]], v[p], lb2[p], s0[p], ltri, causal2) for p in ps]
            o = [ch[p]["o"] for p in ps]
            rinv = [_pair_norm_stats(o[p]) for p in ps]
            xhat = [o[p] * rinv[p] for p in ps]
            sgg = [_sigmoid(gi[p]) for p in ps]
            dout = [dm_ref[rows, cols[p]].astype(F32) for p in ps]
            dy = [dout[p] * (gi[p] * sgg[p]) for p in ps]
            dgi = [dout[p] * xhat[p] * ng_ref[...] * (sgg[p] * (1.0 + gi[p] * (1.0 - sgg[p]))) for p in ps]
            dxh = [dy[p] * ng_ref[...] for p in ps]
            do = [rinv[p] * (dxh[p] - xhat[p] * _pair_mean(dxh[p] * xhat[p])) for p in ps]
            dng = sum(jnp.sum(dy[p] * xhat[p], axis=0, keepdims=True) for p in ps)
            dqd = [_dot(do[p], s0[p], "nt") for p in ps]
            dattn = [jnp.where(causal2, _dot(do[p], ch[p]["v_bd"], "nt"), 0.0) for p in ps]
            dv = [_pair_fold(_dot(ch[p]["attn"], do[p], "tn")) + _dot(ch[p]["kd"], ds[p]) for p in ps]
            dkd = [_dot(v[p], ds[p], "nt") for p in ps]
            dcd = [_dot(ones8, s0[p] * ds[p], "nt", hi=True)[0:1] for p in ps]
            for p in ps:
                ds_scr[p] = jnp.where(diag_blocks, _dot(ch[p]["qd"], do[p], "tn"), 0.0) + ds[p] * ch[p]["cd"].T
            dq_i, dk_i = [], []
            for p in ps:
                dq_rows, dk = [], None
                for i, (qt, eq, kt_bd, ek) in enumerate(ch[p]["subs"]):
                    da = dattn[p][i * SUB:(i + 1) * SUB]
                    dq_rows.append(_dot(da, kt_bd, "nn", hi=True) * eq)
                    d = _pair_fold(_dot(da, qt, "tn", hi=True)) * ek
                    dk = d if dk is None else dk + d
                dq_i.append(jnp.concatenate(dq_rows, axis=0))
                dk_i.append(dk)
            dq = [dqd[p] * ch[p]["ecum"] + dq_i[p] for p in ps]
            dk = [dkd[p] * ch[p]["ekd"] + dk_i[p] for p in ps]
            dkk = [dkd[p] * ch[p]["kd"] for p in ps]
            dcum = [dqd[p] * ch[p]["qd"] - dkk[p] + ch[p]["q"] * dq_i[p] - ch[p]["k"] * dk_i[p] for p in ps]
            dcl = [jnp.sum(dkk[p], axis=0, keepdims=True) + dcd[p] * ch[p]["cd"] for p in ps]
            dlogf = [_dot_exact_lhs(ltri, dcum[p], "tn") + dcl[p] for p in ps]
            dfv = [jnp.where(ch[p]["f"] > TINY, dlogf[p] / ch[p]["f"], 0.0) for p in ps]
            dfi = [dfv[p] * (1.0 - lb2[p]) * ch[p]["sg"] * (1.0 - ch[p]["sg"])
                   - dk[p] * (1.0 - lb2[p]) * ch[p]["sgn"] * (1.0 - ch[p]["sgn"]) for p in ps]
            dlbs = [jnp.sum(dfv[p] * (1.0 - ch[p]["sg"]) - dk[p] * ch[p]["sgn"], axis=0, keepdims=True) for p in ps]
            dqs = [dq[p] * (ch[p]["sq"] * (1.0 + qi[p] * (1.0 - ch[p]["sq"]))) for p in ps]
            dp_ref[rows, :] = jnp.concatenate(dqs + dfi + dv + dgi, axis=1).astype(BF16)
            dlb_ref[...] += jnp.concatenate(dlbs, axis=1)
            dng_ref[...] += dng
            return carry

        lax.fori_loop(0, nch, chunk, 0, unroll=2)

    colblk = lambda j: pl.BlockSpec((tb, A_WIDTH), lambda i, j=j: (nb - 1 - i, j))
    vec = lambda n: pl.BlockSpec((1, n), lambda i: (0, 0))
    return _call_beside(
        body, beside, nb,
        (proj, proj, proj, proj, lb.reshape(1, A_WIDTH), jnp.tile(norm_g.reshape(1, A_DIM), (1, 2)), states, dmixed), grid=(nb,),
        in_specs=[colblk(0), colblk(1), colblk(2), colblk(3), vec(A_WIDTH), vec(A_PAIR_W),
                  pl.BlockSpec((nch, A_PAIRS, A_PAIR_W, A_PAIR_W), lambda i: (nb - 1 - i, 0, 0, 0)), colblk(0)],
        out_specs=[pl.BlockSpec((tb, 4 * A_WIDTH), lambda i: (nb - 1 - i, 0)), vec(A_WIDTH), vec(A_PAIR_W)],
        out_shape=[jax.ShapeDtypeStruct((t, 4 * A_WIDTH), BF16), jax.ShapeDtypeStruct((1, A_WIDTH), F32),
                   jax.ShapeDtypeStruct((1, A_PAIR_W), F32)],
        scratch_shapes=[pltpu.VMEM((A_PAIRS, A_PAIR_W, A_PAIR_W), F32)], name="hgrn_bwd")


def _gate_lane_masks(shape):
    lane = lax.broadcasted_iota(jnp.int32, shape, 1)
    return lane < B_HEADS, (lane >= B_HEADS) & (lane < 2 * B_HEADS)


def _b_pre_fwd(proj, conv_w, alog_row, dtb_row, tb=512):
    t = proj.shape[0]
    cb0 = COL_B // B_WIDTH

    def body(q_ref, k_ref, v_ref, qp_ref, kp_ref, vp_ref, w_ref, gi_ref, al_ref, dt_ref, qkv_ref, gates_ref, y_ref):
        first = pl.program_id(0) == 0
        for part, (c_ref, p_ref) in enumerate(((q_ref, qp_ref), (k_ref, kp_ref), (v_ref, vp_ref))):
            cols = slice(part * B_WIDTH, (part + 1) * B_WIDTH)
            prev = jnp.where(first, 0.0, p_ref[...])
            y = _conv_fwd(c_ref[...], prev, w_ref[:, cols])
            y_ref[:, cols] = y
            s = y * _sigmoid(y)
            if part < 2:
                outs = []
                for h in range(B_HEADS):
                    sh = s[:, h * B_DIM:(h + 1) * B_DIM]
                    outs.append(sh * lax.rsqrt(jnp.sum(sh * sh, axis=-1, keepdims=True) + EPS))
                s = jnp.concatenate(outs, axis=1)
            qkv_ref[:, cols] = s
        g = gi_ref[...]
        is_b, is_a = _gate_lane_masks(g.shape)
        la = -jnp.exp(al_ref[...]) * _softplus(g + dt_ref[...])
        gates_ref[...] = jnp.where(is_b, _sigmoid(g), jnp.where(is_a, la, 0.0))

    cur = lambda j: pl.BlockSpec((tb, B_WIDTH), lambda i, j=j: (i, cb0 + j))
    prv = lambda j: pl.BlockSpec((8, B_WIDTH), lambda i, j=j: (jnp.maximum(i * (tb // 8) - 1, 0), cb0 + j))
    vec = pl.BlockSpec((1, 128), lambda i: (0, 0))
    return pl.pallas_call(
        body, grid=(t // tb,),
        in_specs=[cur(0), cur(1), cur(2), prv(0), prv(1), prv(2), pl.BlockSpec((CONV_K, 3 * B_WIDTH), lambda i: (0, 0)),
                  pl.BlockSpec((tb, 128), lambda i: (i, COL_G // 128)), vec, vec],
        out_specs=[pl.BlockSpec((tb, 3 * B_WIDTH), lambda i: (i, 0)), pl.BlockSpec((tb, 128), lambda i: (i, 0)),
                   pl.BlockSpec((tb, 3 * B_WIDTH), lambda i: (i, 0))],
        out_shape=[jax.ShapeDtypeStruct((t, 3 * B_WIDTH), F32), jax.ShapeDtypeStruct((t, 128), F32),
                   jax.ShapeDtypeStruct((t, 3 * B_WIDTH), F32)],
        compiler_params=_cparams(("parallel",)), name="gdn_pre_fwd")(proj, proj, proj, proj, proj, proj, conv_w, proj, alog_row, dtb_row)


def _inv_unit_lower(amats):
    r, c = _iota2(CHUNK, CHUNK)
    eye = jnp.where(r == c, 1.0, 0.0)
    ps = [eye - a for a in amats]
    aks = amats
    for _ in range(5):
        aks = [_dot(ak, ak, hi=True) for ak in aks]
        ps = [p + _dot(p, ak, hi=True) for p, ak in zip(ps, aks)]
    return ps


def _b_local(qs, ks, vs, betas, gcs, grows, gls, solve=True):
    hs = range(len(qs))
    causal, strict = _tril(CHUNK), _tril(CHUNK, strict=True)
    decay = [jnp.where(causal, jnp.exp(jnp.minimum(gcs[h] - grows[h], 0.0)), 0.0) for h in hs]
    kb = [ks[h] * betas[h] for h in hs]
    kk = [_dot(kb[h], ks[h], "nt") for h in hs]
    qkr = [_dot(qs[h], ks[h], "nt") for h in hs]
    eg = [jnp.exp(gcs[h]) for h in hs]
    bv = [vs[h] * betas[h] for h in hs]
    kg = [kb[h] * eg[h] for h in hs]
    qk = [qkr[h] * decay[h] for h in hs]
    qd = [qs[h] * eg[h] for h in hs]
    ekd = [jnp.exp(gls[h] - gcs[h]) for h in hs]
    kd = [ks[h] * ekd[h] for h in hs]
    cd = [jnp.exp(gls[h]) for h in hs]
    loc = dict(decay=decay, kb=kb, kk=kk, eg=eg, bv=bv, kg=kg, qkr=qkr, qk=qk, qd=qd, ekd=ekd, kd=kd, cd=cd)
    if solve:
        tinv = _inv_unit_lower([jnp.where(strict, kk[h] * decay[h], 0.0) for h in hs])
        loc.update(tinv=tinv, u=[_dot(tinv[h], bv[h], hi=True) for h in hs], w=[_dot(tinv[h], kg[h], hi=True) for h in hs])
    return loc


def _b_state(loc, ids, s0s):
    n = range(len(ids))
    ws = [_dot(loc["w"][ids[j]], s0s[j]) for j in n]
    qs0 = [_dot(loc["qd"][ids[j]], s0s[j]) for j in n]
    vn = [loc["u"][ids[j]] - ws[j] for j in n]
    o = [qs0[j] + _dot(loc["qk"][ids[j]], vn[j]) for j in n]
    s1 = [s0s[j] * loc["cd"][ids[j]] + _dot(loc["kd"][ids[j]], vn[j], "tn") for j in n]
    return vn, o, s1


def _b_fwd(qkv, gates, proj, norm_g, beside=None, tb=256):
    t = qkv.shape[0]
    nch = tb // CHUNK

    def body(q_ref, k_ref, v_ref, ga_ref, z_ref, ng_ref, out_ref, st_ref, ti_ref, w_ref, vn_ref, o_ref, s_scr):
        @pl.when(pl.program_id(0) == 0)
        def _():
            s_scr[...] = jnp.zeros_like(s_scr)

        ltri = _tril(CHUNK).astype(F32)

        hs = range(B_HEADS)
        cols = [slice(h * B_DIM, (h + 1) * B_DIM) for h in hs]

        def pair(p, carry):
            cs = [2 * p, 2 * p + 1]
            rows = [pl.ds(pl.multiple_of(c * CHUNK, CHUNK), CHUNK) for c in cs]
            ga = [ga_ref[r, :] for r in rows]
            gcum = [_dot_exact_lhs(ltri, g) for g in ga]
            gcum_t = [g.T for g in gcum]
            items = [(i, h) for i in range(2) for h in hs]
            loc = _b_local([q_ref[rows[i], cols[h]] * GDN_SCALE for i, h in items], [k_ref[rows[i], cols[h]] for i, h in items],
                           [v_ref[rows[i], cols[h]] for i, h in items], [ga[i][:, h:h + 1] for i, h in items],
                           [gcum[i][:, B_HEADS + h:B_HEADS + h + 1] for i, h in items],
                           [gcum_t[i][B_HEADS + h:B_HEADS + h + 1, :] for i, h in items],
                           [gcum[i][CHUNK - 1:CHUNK, B_HEADS + h:B_HEADS + h + 1] for i, h in items])
            s0s = [s_scr[h] for h in hs]
            for i in range(2):
                ids = [i * B_HEADS + h for h in hs]
                for h in hs:
                    st_ref[cs[i], h] = s0s[h]
                    ti_ref[cs[i], h] = loc["tinv"][ids[h]]
                vn, o, s0s = _b_state(loc, ids, s0s)
                w_ref[rows[i], :] = jnp.concatenate([loc["w"][j] for j in ids], axis=1)
                vn_ref[rows[i], :] = jnp.concatenate(vn, axis=1)
                o_ref[rows[i], :] = jnp.concatenate(o, axis=1)
                outs = [_headnorm_fwd(o[h], ng_ref[...], z_ref[rows[i], cols[h]]) for h in hs]
                out_ref[rows[i], :] = jnp.concatenate(outs, axis=1).astype(BF16)
            for h in hs:
                s_scr[h] = s0s[h]
            return carry

        lax.fori_loop(0, nch // 2, pair, 0)

    part = lambda j: pl.BlockSpec((tb, B_WIDTH), lambda i, j=j: (i, j))
    wide = pl.BlockSpec((tb, B_WIDTH), lambda i: (i, 0))
    wide_shape = jax.ShapeDtypeStruct((t, B_WIDTH), F32)
    return _call_beside(
        body, beside, t // tb, (qkv, qkv, qkv, gates, proj, norm_g.reshape(1, B_DIM)), grid=(t // tb,),
        in_specs=[part(0), part(1), part(2), pl.BlockSpec((tb, 128), lambda i: (i, 0)),
                  pl.BlockSpec((tb, B_WIDTH), lambda i: (i, COL_B // B_WIDTH + 3)), pl.BlockSpec((1, B_DIM), lambda i: (0, 0))],
        out_specs=[wide, pl.BlockSpec((nch, B_HEADS, B_DIM, B_DIM), lambda i: (i, 0, 0, 0)),
                   pl.BlockSpec((nch, B_HEADS, CHUNK, CHUNK), lambda i: (i, 0, 0, 0)), wide, wide, wide],
        out_shape=[jax.ShapeDtypeStruct((t, B_WIDTH), BF16), jax.ShapeDtypeStruct((t // CHUNK, B_HEADS, B_DIM, B_DIM), F32),
                   jax.ShapeDtypeStruct((t // CHUNK, B_HEADS, CHUNK, CHUNK), F32), wide_shape, wide_shape, wide_shape],
        scratch_shapes=[pltpu.VMEM((B_HEADS, B_DIM, B_DIM), F32)], name="gdn_fwd")


def _b_bwd(qkv, gates, proj, norm_g, states, fwd_saved, dmixed, beside=None, tb=256):
    t = qkv.shape[0]
    nch = tb // CHUNK
    nb = t // tb

    def body(q_ref, k_ref, v_ref, ga_ref, z_ref, ng_ref, st_ref, ti_ref, w_ref, vn_ref, o_ref, dm0_ref, dm1_ref,
             dqkv_ref, dga_ref, dz_ref, dng_ref, ds_scr):
        @pl.when(pl.program_id(0) == 0)
        def _():
            ds_scr[...] = jnp.zeros_like(ds_scr)
            dng_ref[...] = jnp.zeros_like(dng_ref)

        ltri = _tril(CHUNK).astype(F32)
        strict = _tril(CHUNK, strict=True)
        lane = lax.broadcasted_iota(jnp.int32, (CHUNK, 128), 1)
        lane1 = lax.broadcasted_iota(jnp.int32, (1, 128), 1)

        nh = range(B_HEADS)
        cols = [slice(h * B_DIM, (h + 1) * B_DIM) for h in nh]
        rsum = lambda a: jnp.sum(a, axis=-1, keepdims=True)

        def pair(p, carry):
            cs = [nch - 1 - 2 * p, nch - 2 - 2 * p]
            crow = [pl.ds(pl.multiple_of(c * CHUNK, CHUNK), CHUNK) for c in cs]
            gas = [ga_ref[r, :] for r in crow]
            gcum = [_dot_exact_lhs(ltri, g) for g in gas]
            gcum_t = [g.T for g in gcum]
            items = [(i, h) for i in range(2) for h in nh]
            hs = range(len(items))
            q = [q_ref[crow[i], cols[h]] * GDN_SCALE for i, h in items]
            k = [k_ref[crow[i], cols[h]] for i, h in items]
            v = [v_ref[crow[i], cols[h]] for i, h in items]
            z = [z_ref[crow[i], cols[h]] for i, h in items]
            beta = [gas[i][:, h:h + 1] for i, h in items]
            s0 = [st_ref[cs[i], h] for i, h in items]
            r = _b_local(q, k, v, beta, [gcum[i][:, B_HEADS + h:B_HEADS + h + 1] for i, h in items],
                         [gcum_t[i][B_HEADS + h:B_HEADS + h + 1, :] for i, h in items],
                         [gcum[i][CHUNK - 1:CHUNK, B_HEADS + h:B_HEADS + h + 1] for i, h in items], solve=False)
            tinv = [ti_ref[cs[i], h] for i, h in items]
            w = [w_ref[crow[i], cols[h]] for i, h in items]
            vn = [vn_ref[crow[i], cols[h]] for i, h in items]
            decay, eg, qd, kd, kb, cd = (r[n] for n in ("decay", "eg", "qd", "kd", "kb", "cd"))
            dms = [(dm0_ref if h < 2 else dm1_ref)[crow[i], (h % 2) * B_DIM:(h % 2 + 1) * B_DIM].astype(F32) for i, h in items]
            hn = [_headnorm_bwd(dms[j], o_ref[crow[i], cols[h]], ng_ref[...], z[j]) for j, (i, h) in enumerate(items)]
            do = [hn[j][0] for j in hs]
            dvn_o = [_dot(r["qk"][j], do[j], "tn") for j in hs]
            dqk = [_dot(do[j], vn[j], "nt") for j in hs]
            dqd = [_dot(do[j], s0[j], "nt") for j in hs]
            ds_o = [_dot(qd[j], do[j], "tn") for j in hs]
            ds = [ds_scr[h] for h in nh]
            dvn, dkd, dcd = [None] * 8, [None] * 8, [None] * 8
            for i in range(2):
                for h in nh:
                    j = i * B_HEADS + h
                    dvn[j] = dvn_o[j] + _dot(kd[j], ds[h])
                    dkd[j] = _dot(vn[j], ds[h], "nt")
                    dcd[j] = jnp.sum(jnp.sum(s0[j] * ds[h], axis=0, keepdims=True), axis=1, keepdims=True)
                ds = [ds_o[i * B_HEADS + h] + ds[h] * cd[i * B_HEADS + h] - _dot(w[i * B_HEADS + h], dvn[i * B_HEADS + h], "tn")
                      for h in nh]
            for h in nh:
                ds_scr[h] = ds[h]
            dw = [-_dot(dvn[j], s0[j], "nt") for j in hs]
            dbv = [_dot(tinv[h], dvn[h], "tn", hi=True) for h in hs]
            dkg = [_dot(tinv[h], dw[h], "tn", hi=True) for h in hs]
            dt = [_dot(dvn[h], r["bv"][h], "nt", hi=True) + _dot(dw[h], r["kg"][h], "nt", hi=True) for h in hs]
            tdt = [_dot(tinv[h], dt[h], "tn", hi=True) for h in hs]
            da = [jnp.where(strict, -_dot(tdt[h], tinv[h], "nt", hi=True), 0.0) for h in hs]
            dm = [da[h] * decay[h] for h in hs]
            dn = [dqk[h] * decay[h] for h in hs]
            e = [(da[h] * r["kk"][h] + dqk[h] * r["qkr"][h]) * decay[h] for h in hs]
            dkb = [_dot(dm[h], k[h]) + dkg[h] * eg[h] for h in hs]
            dk = [_dot(dm[h], kb[h], "tn") + _dot(dn[h], q[h], "tn") + dkd[h] * r["ekd"][h] + dkb[h] * beta[h] for h in hs]
            dq = [_dot(dn[h], k[h]) + dqd[h] * eg[h] for h in hs]
            tkd = [rsum(dkd[h] * kd[h]) for h in hs]
            dgc = [rsum(e[h]) - rsum(e[h].T) + rsum(dqd[h] * qd[h]) - tkd[h] + rsum(dkg[h] * r["kg"][h]) for h in hs]
            dgl = [jnp.sum(tkd[h], axis=0, keepdims=True) + dcd[h] * cd[h] for h in hs]
            dbeta = [rsum(dbv[h] * v[h]) + rsum(dkb[h] * k[h]) for h in hs]
            for i in range(2):
                ids = [i * B_HEADS + h for h in nh]
                dbeta_m = sum(jnp.where(lane == h, dbeta[ids[h]], 0.0) for h in nh)
                dgc_m = sum(jnp.where(lane == B_HEADS + h, dgc[ids[h]], 0.0) for h in nh)
                dgl_m = sum(jnp.where(lane1 == B_HEADS + h, dgl[ids[h]], 0.0) for h in nh)
                dqkv_ref[crow[i], :] = jnp.concatenate(
                    [dq[j] * GDN_SCALE for j in ids] + [dk[j] for j in ids] + [dbv[j] * beta[j] for j in ids], axis=1)
                dz_ref[crow[i], :] = jnp.concatenate([hn[j][1] for j in ids], axis=1).astype(BF16)
                dga_ref[crow[i], :] = dbeta_m + _dot_exact_lhs(ltri, dgc_m, "tn") + dgl_m
            dng_ref[...] += sum(hn[j][2] for j in hs)
            return carry

        lax.fori_loop(0, nch // 2, pair, 0)

    part = lambda j: pl.BlockSpec((tb, B_WIDTH), lambda i, j=j: (nb - 1 - i, j))
    rowblk = lambda w, j=0: pl.BlockSpec((tb, w), lambda i, j=j: (nb - 1 - i, j))
    return _call_beside(
        body, beside, nb, (qkv, qkv, qkv, gates, proj, norm_g.reshape(1, B_DIM), states, *fwd_saved, dmixed, dmixed), grid=(nb,),
        in_specs=[part(0), part(1), part(2), rowblk(128), rowblk(B_WIDTH, COL_B // B_WIDTH + 3),
                  pl.BlockSpec((1, B_DIM), lambda i: (0, 0)),
                  pl.BlockSpec((nch, B_HEADS, B_DIM, B_DIM), lambda i: (nb - 1 - i, 0, 0, 0)),
                  pl.BlockSpec((nch, B_HEADS, CHUNK, CHUNK), lambda i: (nb - 1 - i, 0, 0, 0)),
                  rowblk(B_WIDTH), rowblk(B_WIDTH), rowblk(B_WIDTH), rowblk(256, 1), rowblk(256, 2)],
        out_specs=[rowblk(3 * B_WIDTH), rowblk(128), rowblk(B_WIDTH), pl.BlockSpec((1, B_DIM), lambda i: (0, 0))],
        out_shape=[jax.ShapeDtypeStruct((t, 3 * B_WIDTH), F32), jax.ShapeDtypeStruct((t, 128), F32),
                   jax.ShapeDtypeStruct((t, B_WIDTH), BF16), jax.ShapeDtypeStruct((1, B_DIM), F32)],
        scratch_shapes=[pltpu.VMEM((B_HEADS, B_DIM, B_DIM), F32)], name="gdn_bwd")


def _b_pre_bwd(proj, y3, conv_w, alog_row, dtb_row, dqkv, dgates, tb=512):
    t = proj.shape[0]
    nb = t // tb
    cb0 = COL_B // B_WIDTH

    def body(q_ref, k_ref, v_ref, y_ref, w_ref, gi_ref, al_ref, dt_ref, dqkv_ref, dga_ref,
             dy_ref, dgi_ref, dw_ref, dal_ref, ddt_ref, nxt_scr):
        step_id = pl.program_id(0)

        @pl.when(step_id == 0)
        def _():
            dw_ref[...] = jnp.zeros_like(dw_ref)
            dal_ref[...] = jnp.zeros_like(dal_ref)
            ddt_ref[...] = jnp.zeros_like(ddt_ref)

        for part, c_ref in enumerate((q_ref, k_ref, v_ref)):
            cols = slice(part * B_WIDTH, (part + 1) * B_WIDTH)
            cur = c_ref[...]
            w = w_ref[:, cols]
            y = y_ref[:, cols]
            sg = _sigmoid(y)
            s = y * sg
            dsn = dqkv_ref[:, cols]
            if part < 2:
                outs = []
                for h in range(B_HEADS):
                    hc = slice(h * B_DIM, (h + 1) * B_DIM)
                    sh, dh = s[:, hc], dsn[:, hc]
                    rq = lax.rsqrt(jnp.sum(sh * sh, axis=-1, keepdims=True) + EPS)
                    nh = sh * rq
                    outs.append(rq * (dh - nh * jnp.sum(dh * nh, axis=-1, keepdims=True)))
                dsn = jnp.concatenate(outs, axis=1)
            dy = dsn * (sg * (1.0 + y * (1.0 - sg)))
            later = _later_rows(dy, jnp.where(step_id == 0, 0.0, nxt_scr[:, cols]))
            dy_ref[:, cols] = sum(later[j] * w[j:j + 1] for j in range(CONV_K)).astype(BF16)
            nxt_scr[:, cols] = dy[0:8]
            dw_ref[:, cols] += jnp.concatenate([jnp.sum(cur * later[j], axis=0, keepdims=True) for j in range(CONV_K)], axis=0)
        g = gi_ref[...]
        dga = dga_ref[...]
        is_b, is_a = _gate_lane_masks(g.shape)
        beta = _sigmoid(g)
        pre = g + dt_ref[...]
        ea = jnp.exp(al_ref[...])
        la = -ea * _softplus(pre)
        dpre = jnp.where(is_a, dga * (-ea) * _sigmoid(pre), 0.0)
        dgi_ref[...] = jnp.where(is_b, dga * beta * (1.0 - beta), dpre).astype(BF16)
        dal_ref[...] += jnp.sum(jnp.where(is_a, dga * la, 0.0), axis=0, keepdims=True)
        ddt_ref[...] += jnp.sum(dpre, axis=0, keepdims=True)

    cur = lambda j: pl.BlockSpec((tb, B_WIDTH), lambda i, j=j: (nb - 1 - i, cb0 + j))
    vec = pl.BlockSpec((1, 128), lambda i: (0, 0))
    wspec = pl.BlockSpec((CONV_K, 3 * B_WIDTH), lambda i: (0, 0))
    rowblk = lambda width, j=0: pl.BlockSpec((tb, width), lambda i, j=j: (nb - 1 - i, j))
    return pl.pallas_call(
        body, grid=(nb,),
        in_specs=[cur(0), cur(1), cur(2), rowblk(3 * B_WIDTH), wspec, rowblk(128, COL_G // 128), vec, vec,
                  rowblk(3 * B_WIDTH), rowblk(128)],
        out_specs=[rowblk(3 * B_WIDTH), rowblk(128), wspec, vec, vec],
        out_shape=[jax.ShapeDtypeStruct((t, 3 * B_WIDTH), BF16), jax.ShapeDtypeStruct((t, 128), BF16),
                   jax.ShapeDtypeStruct((CONV_K, 3 * B_WIDTH), F32), jax.ShapeDtypeStruct((1, 128), F32), jax.ShapeDtypeStruct((1, 128), F32)],
        scratch_shapes=[pltpu.VMEM((8, 3 * B_WIDTH), F32)],
        compiler_params=_cparams(("arbitrary",)), name="gdn_pre_bwd")(
            proj, proj, proj, y3, conv_w, proj, alog_row, dtb_row, dqkv, dgates)


def _c_gates(xc, wa_ref, ba_ref, wx_ref, bx_ref, lam_ref, is_row0):
    r = _sigmoid(_dot(xc, wa_ref[...]) + ba_ref[...])
    i = _sigmoid(_dot(xc, wx_ref[...]) + bx_ref[...])
    sp = _softplus(-lam_ref[...])
    log_a = -RG_C * r * sp
    a = jnp.exp(log_a)
    m2 = _neg_expm1(2.0 * log_a)
    mult = jnp.where(is_row0, 1.0, jnp.sqrt(jnp.maximum(m2, EPS)))
    return r, i, sp, log_a, a, m2, mult


def _row0_mask(tb, first):
    ridx = lax.broadcasted_iota(jnp.int32, (tb, C_WIDTH), 0)
    return (ridx == 0) & first


def _c_fwd(proj, conv_w, conv_b, wa, ba, wx, bx, lam, tb=512):
    t = proj.shape[0]
    cbx = COL_C // C_WIDTH

    def body(x_ref, xp_ref, y_ref, w_ref, cb_ref, wa_ref, ba_ref, wx_ref, bx_ref, lam_ref, out_ref, h_ref, a_scr, b_scr, h_scr):
        first = pl.program_id(0) == 0

        @pl.when(first)
        def _():
            h_scr[...] = jnp.zeros_like(h_scr)

        prev = jnp.where(first, 0.0, xp_ref[...])
        xc = _conv_fwd(x_ref[...], prev, w_ref[...]) + cb_ref[...]
        _, i, _, _, a, _, mult = _c_gates(xc, wa_ref, ba_ref, wx_ref, bx_ref, lam_ref, _row0_mask(tb, first))
        ta, tb_ = _tile_scan(a, mult * i * xc)
        a_scr[...] = ta
        b_scr[...] = tb_

        def step(blk, h):
            rows = pl.ds(pl.multiple_of(blk * 8, 8), 8)
            h_ref[rows, :] = jnp.broadcast_to(h, (8, C_WIDTH))
            return a_scr[rows, :][7:8] * h + b_scr[rows, :][7:8]

        h_scr[...] = lax.fori_loop(0, tb // 8, step, h_scr[...], unroll=8)
        hs = ta * h_ref[...] + tb_
        h_ref[...] = hs
        gl, _ = _gelu_tanh(y_ref[...])
        out_ref[...] = (gl * hs).astype(BF16)

    vec = pl.BlockSpec((1, C_WIDTH), lambda i: (0, 0))
    mat = pl.BlockSpec((C_WIDTH, C_WIDTH), lambda i: (0, 0))
    row = pl.BlockSpec((tb, C_WIDTH), lambda i: (i, 0))
    return pl.pallas_call(
        body, grid=(t // tb,),
        in_specs=[pl.BlockSpec((tb, C_WIDTH), lambda i: (i, cbx)),
                  pl.BlockSpec((8, C_WIDTH), lambda i: (jnp.maximum(i * (tb // 8) - 1, 0), cbx)),
                  pl.BlockSpec((tb, C_WIDTH), lambda i: (i, cbx + 1)),
                  pl.BlockSpec((CONV_K, C_WIDTH), lambda i: (0, 0)), vec, mat, vec, mat, vec, vec],
        out_specs=[row, row],
        out_shape=[jax.ShapeDtypeStruct((t, C_WIDTH), BF16), jax.ShapeDtypeStruct((t, C_WIDTH), F32)],
        scratch_shapes=[pltpu.VMEM((tb, C_WIDTH), F32), pltpu.VMEM((tb, C_WIDTH), F32), pltpu.VMEM((1, C_WIDTH), F32)],
        compiler_params=_cparams(("arbitrary",)), name="lru_fwd")(proj, proj, proj, conv_w, conv_b, wa, ba, wx, bx, lam)


def _c_bwd(proj, conv_w, conv_b, wa, ba, wx, bx, lam, hs, dmixed, tb=512):
    t = proj.shape[0]
    nb = t // tb
    cbx = COL_C // C_WIDTH

    def body(x_ref, xp_ref, y_ref, w_ref, cb_ref, wa_ref, ba_ref, wx_ref, bx_ref, lam_ref, h_ref, hp_ref, dm_ref,
             dxc_ref, dyg_ref, dw_ref, dcb_ref, dwa_ref, dba_ref, dwx_ref, dbx_ref, dlam_ref, g_scr, a_scr, cin_scr, c_scr, nxt_scr):
        step_id = pl.program_id(0)
        first = step_id == nb - 1

        @pl.when(step_id == 0)
        def _():
            c_scr[...] = jnp.zeros_like(c_scr)
            for ref in (dw_ref, dcb_ref, dwa_ref, dba_ref, dwx_ref, dbx_ref, dlam_ref):
                ref[...] = jnp.zeros_like(ref)

        cur = x_ref[...]
        prev = jnp.where(first, 0.0, xp_ref[...])
        w = w_ref[...]
        shifted = [_shift_rows(cur, prev, 3 - j, down=True) for j in range(3)] + [cur]
        xc = shifted[0] * w[0:1] + shifted[1] * w[1:2] + shifted[2] * w[2:3] + shifted[3] * w[3:4] + cb_ref[...]
        row0 = _row0_mask(tb, first)
        r, i, sp, log_a, a, m2, mult = _c_gates(xc, wa_ref, ba_ref, wx_ref, bx_ref, lam_ref, row0)
        h = h_ref[...]
        hprev = _shift_rows(h, jnp.where(first, 0.0, hp_ref[...]), 1, down=True)
        gl, dgl = _gelu_tanh(y_ref[...])
        dm = dm_ref[...].astype(F32)
        dyg_ref[...] = (dm * h * dgl).astype(BF16)
        dout = dm * gl
        ta, te = _tile_scan(a, a * dout, reverse=True)
        a_scr[...] = ta
        g_scr[...] = te

        def step(blk, carry):
            rows = pl.ds(pl.multiple_of((tb // 8 - 1 - blk) * 8, 8), 8)
            cin_scr[rows, :] = jnp.broadcast_to(carry, (8, C_WIDTH))
            return a_scr[rows, :][0:1] * carry + g_scr[rows, :][0:1]

        c_scr[...] = lax.fori_loop(0, tb // 8, step, c_scr[...], unroll=8)
        cin = cin_scr[...]
        cout = ta * cin + te
        last_in_tile = lax.broadcasted_iota(jnp.int32, (tb, C_WIDTH), 0) % 8 == 7
        dbx = dout + jnp.where(last_in_tile, cin, pltpu.roll(cout, tb - 1, 0))
        da = dbx * hprev
        dmult = jnp.where(row0, 0.0, dbx * i * xc)
        di = dbx * mult * xc
        dxc = dbx * mult * i
        dm2 = jnp.where(m2 > EPS, dmult * 0.5 / mult, 0.0)
        dlog_a = da * a - 2.0 * a * a * dm2
        dr = dlog_a * (-RG_C) * sp
        dlam_ref[...] += jnp.sum(dlog_a * (-RG_C) * r, axis=0, keepdims=True) * (-_sigmoid(-lam_ref[...]))
        dpa = dr * r * (1.0 - r)
        dpx = di * i * (1.0 - i)
        dba_ref[...] += jnp.sum(dpa, axis=0, keepdims=True)
        dbx_ref[...] += jnp.sum(dpx, axis=0, keepdims=True)
        dwa_ref[...] += _dot(xc, dpa, "tn")
        dwx_ref[...] += _dot(xc, dpx, "tn")
        dxc = dxc + _dot(dpa, wa_ref[...], "nt") + _dot(dpx, wx_ref[...], "nt")
        dxc_ref[...] = _conv_bwd_rows(dxc, jnp.where(step_id == 0, 0.0, nxt_scr[...]), w).astype(BF16)
        nxt_scr[...] = dxc[0:8]
        dcb_ref[...] += jnp.sum(dxc, axis=0, keepdims=True)
        dw_ref[...] += jnp.concatenate([jnp.sum(shifted[j] * dxc, axis=0, keepdims=True) for j in range(CONV_K)], axis=0)

    vec = pl.BlockSpec((1, C_WIDTH), lambda i: (0, 0))
    mat = pl.BlockSpec((C_WIDTH, C_WIDTH), lambda i: (0, 0))
    cw = pl.BlockSpec((CONV_K, C_WIDTH), lambda i: (0, 0))
    row = lambda j=0: pl.BlockSpec((tb, C_WIDTH), lambda i, j=j: (nb - 1 - i, j))
    halo = lambda j=0: pl.BlockSpec((8, C_WIDTH), lambda i, j=j: (jnp.maximum((nb - 1 - i) * (tb // 8) - 1, 0), j))
    return pl.pallas_call(
        body, grid=(nb,),
        in_specs=[row(cbx), halo(cbx), row(cbx + 1), cw, vec, mat, vec, mat, vec, vec, row(), halo(), row(3)],
        out_specs=[row(), row(), cw, vec, mat, vec, mat, vec, vec],
        out_shape=[jax.ShapeDtypeStruct((t, C_WIDTH), BF16), jax.ShapeDtypeStruct((t, C_WIDTH), BF16),
                   jax.ShapeDtypeStruct((CONV_K, C_WIDTH), F32), jax.ShapeDtypeStruct((1, C_WIDTH), F32),
                   jax.ShapeDtypeStruct((C_WIDTH, C_WIDTH), F32), jax.ShapeDtypeStruct((1, C_WIDTH), F32),
                   jax.ShapeDtypeStruct((C_WIDTH, C_WIDTH), F32), jax.ShapeDtypeStruct((1, C_WIDTH), F32),
                   jax.ShapeDtypeStruct((1, C_WIDTH), F32)],
        scratch_shapes=[pltpu.VMEM((tb, C_WIDTH), F32), pltpu.VMEM((tb, C_WIDTH), F32), pltpu.VMEM((tb, C_WIDTH), F32),
                        pltpu.VMEM((1, C_WIDTH), F32), pltpu.VMEM((8, C_WIDTH), F32)],
        compiler_params=_cparams(("arbitrary",)), name="lru_bwd")(
            proj, proj, proj, conv_w, conv_b, wa, ba, wx, bx, lam, hs, hs, dmixed)


def _mesh_pos():
    return lax.axis_index("x"), lax.axis_index("y"), lax.axis_index("c")


class _Exchange:
    def __init__(self, arrays, layouts):
        self.arrays, self.layouts = list(arrays), list(layouts)
        self.out_shapes = []
        for a, lay in zip(self.arrays, self.layouts):
            if lay == 'a2a':
                shp = a.shape
            elif lay == 'slot':
                shp = (N_DEV,) + a.shape
            elif lay == 'rows':
                shp = (N_DEV * a.shape[0], a.shape[1])
            else:
                shp = (a.shape[0], N_DEV * a.shape[1])
            self.out_shapes.append(jax.ShapeDtypeStruct(shp, a.dtype))
        n = len(self.arrays)
        self.scratch = [pltpu.SemaphoreType.DMA((7 * n,)), pltpu.SemaphoreType.DMA((7 * n,)), pltpu.SemaphoreType.DMA((n,))]

    def _landing(self, a, dst_ref, idx):
        lay, shape = self.layouts[a], self.arrays[a].shape
        if lay in ('a2a', 'slot'):
            return dst_ref.at[idx]
        if lay == 'rows':
            return dst_ref.at[pl.ds(pl.multiple_of(idx * shape[0], shape[0]), shape[0]), :]
        return dst_ref.at[:, pl.ds(pl.multiple_of(idx * shape[1], shape[1]), shape[1])]

    def copies(self, src_refs, dst_refs, send_sems, recv_sems, local_sems):
        mx, my, mc = _mesh_pos()
        me = 4 * mx + 2 * my + mc
        out = []
        for a, (src, dst) in enumerate(zip(src_refs, dst_refs)):
            a2a = self.layouts[a] == 'a2a'
            out.append(pltpu.make_async_copy(src.at[me] if a2a else src, self._landing(a, dst, me), local_sems.at[a]))
            for k in range(1, N_DEV):
                px = 1 - mx if k & 4 else mx
                py = 1 - my if k & 2 else my
                pc = 1 - mc if k & 1 else mc
                out.append(pltpu.make_async_remote_copy(
                    src_ref=src.at[4 * px + 2 * py + pc] if a2a else src, dst_ref=self._landing(a, dst, me),
                    send_sem=send_sems.at[7 * a + k - 1], recv_sem=recv_sems.at[7 * a + k - 1],
                    device_id=(px, py, pc), device_id_type=MESH))
        return out


_ANY = pl.BlockSpec(memory_space=pl.ANY)


def _run_exchange(ex, name):
    n = len(ex.arrays)

    def body(*refs):
        cps = ex.copies(refs[:n], refs[n:2 * n], *refs[2 * n:])
        for cp in cps:
            cp.start()
        for cp in cps:
            cp.wait()

    return pl.pallas_call(body, out_shape=ex.out_shapes, in_specs=[_ANY] * n, out_specs=[_ANY] * n,
                          scratch_shapes=ex.scratch, name=name)(*ex.arrays)


def _call_beside(body, ex, nsteps, args, *, grid, in_specs, out_specs, out_shape, scratch_shapes, name):
    if ex is None:
        outs = pl.pallas_call(body, grid=grid, in_specs=in_specs, out_specs=out_specs, out_shape=out_shape,
                              scratch_shapes=scratch_shapes, compiler_params=_cparams(("arbitrary",)), name=name)(*args)
        return outs, None
    n_in, n_out, n_scr, n = len(in_specs), len(out_specs), len(scratch_shapes), len(ex.arrays)

    def wrapped(*refs):
        ins, refs = refs[:n_in], refs[n_in:]
        ex_ins, refs = refs[:n], refs[n:]
        outs, refs = refs[:n_out], refs[n_out:]
        ex_outs, refs = refs[:n], refs[n:]
        scr, sems = refs[:n_scr], refs[n_scr:]
        step = pl.program_id(0)

        @pl.when(step == 0)
        def _():
            for cp in ex.copies(ex_ins, ex_outs, *sems):
                cp.start()

        body(*ins, *outs, *scr)

        @pl.when(step == nsteps - 1)
        def _():
            for cp in ex.copies(ex_ins, ex_outs, *sems):
                cp.wait()

    res = pl.pallas_call(
        wrapped, grid=grid, in_specs=list(in_specs) + [_ANY] * n, out_specs=list(out_specs) + [_ANY] * n,
        out_shape=list(out_shape) + ex.out_shapes, scratch_shapes=list(scratch_shapes) + ex.scratch,
        compiler_params=_cparams(("arbitrary",)), name=name)(*args, *ex.arrays)
    return res[:n_out], res[n_out:]


def _adamw_math(w, g, m, v):
    m = ADAM_B1 * m + (1.0 - ADAM_B1) * g
    v = ADAM_B2 * v + (1.0 - ADAM_B2) * (g * g)
    m_hat = m / (1.0 - ADAM_B1 ** ADAM_STEP)
    v_hat = v / (1.0 - ADAM_B2 ** ADAM_STEP)
    delta = -ADAM_LR * (m_hat / (jnp.sqrt(v_hat) + ADAM_EPS) + ADAM_WD * w)
    return delta, m, v


def _sum_adamw(parts, w, m, v, tr, name):
    ns = len(parts)
    p, r, c = parts[0].shape
    tr = min(tr, r)
    assert r % tr == 0
    nt = r // tr

    def body(*refs):
        p_refs, (w_ref, m_ref, v_ref, g_ref, d_ref, nm_ref, nv_ref) = refs[:ns], refs[ns:]
        for s in range(ns):
            @pl.when(pl.program_id(0) == s)
            def _(p_ref=p_refs[s]):
                g = p_ref[0].astype(F32)
                for j in range(1, p):
                    g = g + p_ref[j].astype(F32)
                delta, nm, nv = _adamw_math(w_ref[...], g, m_ref[...], v_ref[...])
                g_ref[...] = g
                d_ref[...] = delta
                nm_ref[...] = nm
                nv_ref[...] = nv

    part_spec = lambda s: pl.BlockSpec((p, tr, c), lambda sec, i, s=s: (0, jnp.where(sec == s, i, 0), 0))
    row = pl.BlockSpec((tr, c), lambda sec, i: (sec * nt + i, 0))
    return pl.pallas_call(
        body, grid=(ns, nt), in_specs=[part_spec(s) for s in range(ns)] + [row, row, row],
        out_specs=[row] * 4, out_shape=[jax.ShapeDtypeStruct((ns * r, c), F32)] * 4,
        compiler_params=_cparams(("arbitrary", "arbitrary")), name=name)(*parts, w, m, v)


def _sum_parts(parts, name):
    p, r, c = parts.shape

    def body(p_ref, o_ref):
        g = p_ref[0]
        for j in range(1, p):
            g = g + p_ref[j]
        o_ref[...] = g

    return pl.pallas_call(body, out_shape=jax.ShapeDtypeStruct((r, c), F32), name=name)(parts)


def _rows_of(shape):
    n = 1
    for d in shape:
        n *= d
    return n, -(-n // 128)


def _pack(arrs):
    blocks = []
    for a in arrs:
        n, nr = _rows_of(a.shape)
        blocks.append(jnp.pad(a.reshape(-1).astype(F32), (0, nr * 128 - n)).reshape(nr, 128))
    rows = sum(b.shape[0] for b in blocks)
    if rows % 8:
        blocks.append(jnp.zeros((8 - rows % 8, 128), F32))
    return jnp.concatenate(blocks, axis=0)


def _unpack(buf, shapes):
    out, r0 = [], 0
    for s in shapes:
        n, nr = _rows_of(s)
        out.append(buf[r0:r0 + nr].reshape(-1)[:n].reshape(s))
        r0 += nr
    return out


def _block_diag(w):
    rows = [jnp.pad(w[i], ((0, 0), (i * C_BLOCK_DIM, C_WIDTH - (i + 1) * C_BLOCK_DIM))) for i in range(C_BLOCKS)]
    return jnp.concatenate(rows, axis=0)


def _diag_blocks(m):
    m4 = m.reshape(C_BLOCKS, C_BLOCK_DIM, C_BLOCKS, C_BLOCK_DIM)
    return jnp.stack([m4[i, :, i, :] for i in range(C_BLOCKS)])


def _gate_row(v):
    return jnp.pad(v.astype(F32), (B_HEADS, 128 - 2 * B_HEADS)).reshape(1, 128)


def _permute_w_in(w):
    pad = jnp.zeros(w.shape[:-1] + (D_IN_PAD - D_IN,), w.dtype)
    return jnp.concatenate([w[..., :3072], w[..., 3080:3592], w[..., 3072:3080], pad], axis=-1)


def _unpermute_w_in(w):
    return jnp.concatenate([w[..., :3072], w[..., COL_G:COL_G + 8], w[..., 3072:COL_G]], axis=-1)


_WEIGHTS = ['norm1_g', 'w_in', 'hgrn_lb_logits', 'hgrn_norm_g', 'gdn_conv_w', 'gdn_a_log', 'gdn_dt_bias', 'gdn_norm_g',
            'lru_conv_w', 'lru_conv_b', 'lru_w_a', 'lru_b_a', 'lru_w_x', 'lru_b_x', 'lru_lambda', 'w_out', 'norm2_g',
            'w_up', 'w_down', 'final_norm_g']
_BIG = ('w_in', 'w_out', 'w_up', 'w_down')
_SHARDED_SMALL = ('gdn_conv_w', 'lru_conv_w')


def _step(x, target, w, m, v):
    t = x.shape[0]
    mx, my, mc = _mesh_pos()
    me = 4 * mx + 2 * my + mc

    bf = lambda a: a.astype(BF16)

    def full_w_in(g):
        return _permute_w_in(jnp.moveaxis(g, 0, 1).reshape(D_MODEL, D_IN))

    conv_shapes = [w['gdn_conv_w'].shape, w['lru_conv_w'].shape]
    g_in, g_conv = _run_exchange(
        _Exchange([bf(w['w_in'][0]), _pack([w['gdn_conv_w'], w['lru_conv_w']])], ['slot', 'slot']), "gather_first")
    w_in = [full_w_in(g_in)]
    w_out, w_up, w_down = [], [], []
    gdn_cw, lru_cw = [], []
    for j in range(N_DEV):
        a, b = _unpack(g_conv[j], conv_shapes)
        gdn_cw.append(a)
        lru_cw.append(b)
    gdn_cw = jnp.concatenate(gdn_cw, axis=-1)
    lru_cw = jnp.concatenate(lru_cw, axis=-1)

    lbnd = _lb_fwd(w['hgrn_lb_logits'])
    row = lambda a: a.reshape(1, -1)

    def c_args(l):
        return (lru_cw[l], row(w['lru_conv_b'][l]), _block_diag(w['lru_w_a'][l]), row(w['lru_b_a'][l]),
                _block_diag(w['lru_w_x'][l]), row(w['lru_b_x'][l]), row(w['lru_lambda'][l]))

    saved = []
    xl = x
    h = _rms_fwd(x, w['norm1_g'][0], name="rms_fwd")
    for l in range(DEPTH):
        proj = _mm_rows(h, w_in[l], "nn", 256, "mm_proj")
        mix_a, st_a = _a_fwd(proj, lbnd[l], w['hgrn_norm_g'][l])
        alr, dtr = _gate_row(w['gdn_a_log'][l]), _gate_row(w['gdn_dt_bias'][l])
        qkv, gates, y3 = _b_pre_fwd(proj, gdn_cw[l], alr, dtr)
        nxt = [bf(w['w_in'][l + 1])] if l + 1 < DEPTH else []
        gather = _Exchange([bf(w['w_out'][l]), bf(w['w_up'][l]), bf(w['w_down'][l])] + nxt, ['rows', 'cols', 'rows'] + ['slot'] * len(nxt))
        (mix_b, st_b, *b_saved), got = _b_fwd(qkv, gates, proj, w['gdn_norm_g'][l], beside=gather)
        w_out.append(got[0])
        w_up.append(got[1])
        w_down.append(got[2])
        if nxt:
            w_in.append(full_w_in(got[3]))
        mix_c, hs = _c_fwd(proj, *c_args(l))
        mixed = [mix_a, mix_b, mix_c]
        x_mid, h2 = _mm_rows(mixed, w_out[l], "nn", 1024, "mm_out", residual=xl, epilogue="rms_fwd", norm=w['norm2_g'][l])
        act, up = _mm_rows(h2, w_up[l], "nn", 256, "mm_up", epilogue="relu2")
        saved.append(dict(x=xl, h=h, proj=proj, st_a=st_a, qkv=qkv, gates=gates, y3=y3, st_b=st_b, b_saved=b_saved, hs=hs, mixed=mixed,
                          x_mid=x_mid, h2=h2, up=up, act=act, alr=alr, dtr=dtr))
        if l + 1 < DEPTH:
            xl, h = _mm_rows(act, w_down[l], "nn", 512, "mm_down", residual=x_mid, epilogue="rms_fwd", norm=w['norm1_g'][l + 1])
        else:
            xl = _mm_rows(act, w_down[l], "nn", 512, "mm_down_last", residual=x_mid)
    loss, dx, dgf = _loss_head(xl, w['final_norm_g'], target)

    gs = {n: [None] * DEPTH for n in _WEIGHTS}
    recv = {n: [None] * DEPTH for n in _BIG}
    dw_in_above = None
    for l in reversed(range(DEPTH)):
        s = saved[l]
        dup = _mm_rows(dx, w_down[l], "nt", 256, "mm_dact", epilogue="drelu2", up=s['up'])
        dw_down = _mm_tn(s['act'], dx, 512, "mm_dw_down").reshape(N_DEV, D_FF // N_DEV, D_MODEL)
        dx_mid, dg2 = _mm_rows(dup, w_up[l], "nt", 512, "mm_dh2", epilogue="rms_bwd", norm=(s['x_mid'], w['norm2_g'][l], dx))
        dw_up = _mm_tn(s['h2'], dup, 512, "mm_dw_up", slab=D_FF // N_DEV)
        gs['norm2_g'][l] = dg2[0]
        dmixed = _mm_rows(dx_mid, w_out[l], "nt", 1024, "mm_dmixed")
        dw_out = _mm_tn(s['mixed'], dx_mid, 1024, "mm_dw_out").reshape(N_DEV, D_MODEL // N_DEV, D_MODEL)
        proj = s['proj']
        above = [dw_in_above] if dw_in_above is not None else []
        (dpa, dlb, dnga), got = _a_bwd(proj, lbnd[l], w['hgrn_norm_g'][l], s['st_a'], dmixed,
                                       beside=_Exchange([dw_out] + above, ['a2a'] * (1 + len(above))))
        recv['w_out'][l] = got[0]
        if above:
            recv['w_in'][l + 1] = got[1]
        gs['hgrn_lb_logits'][l] = dlb[0]
        gs['hgrn_norm_g'][l] = dnga[0, :A_DIM] + dnga[0, A_DIM:]
        (dqkv, dgates, dz, dngb), got = _b_bwd(s['qkv'], s['gates'], proj, w['gdn_norm_g'][l], s['st_b'], s['b_saved'], dmixed,
                                               beside=_Exchange([dw_up, dw_down], ['a2a', 'a2a']))
        recv['w_up'][l], recv['w_down'][l] = got
        dxb, dgi, dcwb, dal, ddt = _b_pre_bwd(proj, s['y3'], gdn_cw[l], s['alr'], s['dtr'], dqkv, dgates)
        gs['gdn_norm_g'][l] = dngb[0]
        gs['gdn_conv_w'][l] = dcwb
        gs['gdn_a_log'][l] = dal[0, B_HEADS:2 * B_HEADS]
        gs['gdn_dt_bias'][l] = ddt[0, B_HEADS:2 * B_HEADS]
        dxc_in, dyg, dcwc, dcb, dwa, dba, dwx, dbx, dlam = _c_bwd(proj, *c_args(l), s['hs'], dmixed)
        gs['lru_conv_w'][l] = dcwc
        gs['lru_conv_b'][l] = dcb[0]
        gs['lru_w_a'][l] = _diag_blocks(dwa)
        gs['lru_b_a'][l] = dba[0]
        gs['lru_w_x'][l] = _diag_blocks(dwx)
        gs['lru_b_x'][l] = dbx[0]
        gs['lru_lambda'][l] = dlam[0]
        dproj = [dpa, dxb, dz, dxc_in, dyg, dgi]
        dw_in = _unpermute_w_in(_mm_tn(s['h'], dproj, 512, "mm_dw_in"))
        dw_in_above = jnp.moveaxis(dw_in.reshape(D_MODEL, N_DEV, D_IN // N_DEV), 1, 0)
        norm = (s['x'], w['norm1_g'][l], dx_mid)
        if l > 0:
            dx, dg1 = _mm_rows(dproj, w_in[l], "nt", 512, "mm_dh", epilogue="rms_bwd", norm=norm)
        else:
            (dx, dg1), got = _mm_rows(dproj, w_in[l], "nt", 512, "mm_dh_last", epilogue="rms_bwd", norm=norm,
                                      beside=_Exchange([dw_in_above], ['a2a']))
            recv['w_in'][0] = got[0]
        gs['norm1_g'][l] = dg1[0]
    grad_x = dx
    part = {n: jnp.stack(gs[n]) for n in _WEIGHTS if n != 'final_norm_g' and n not in _BIG}
    part['final_norm_g'] = dgf[0]
    part['hgrn_lb_logits'] = _lb_bwd(w['hgrn_lb_logits'], part['hgrn_lb_logits'])

    small = [n for n in _WEIGHTS if n not in _BIG]
    packed = _pack([part[n] for n in small] + [loss])
    all_small, = _run_exchange(_Exchange([packed], ['slot']), "gather_small")

    grads, deltas, new_m, new_v = {}, {}, {}, {}
    for n in _BIG:
        shp = w[n].shape
        r2 = lambda a: a.reshape(-1, shp[-1])
        g, d, nm, nv = _sum_adamw(recv[n], r2(w[n]), r2(m[n]), r2(v[n]), 256, "adamw_" + n)
        grads[n], deltas[n], new_m[n], new_v[n] = (a.reshape(shp) for a in (g, d, nm, nv))

    total = _sum_parts(all_small, "sum_small")
    summed = _unpack(total, [part[n].shape for n in small] + [(1, 1)])
    loss_total = summed[-1].reshape(())
    gsmall = dict(zip(small, summed[:-1]))
    for n in _SHARDED_SMALL:
        width = w[n].shape[-1]
        gsmall[n] = lax.dynamic_slice_in_dim(gsmall[n], me * width, width, axis=2)
    pk = lambda d: _pack([d[n] for n in small])
    _, d, nm, nv = _sum_adamw([pk(gsmall)[None]], pk(w), pk(m), pk(v), 4096, "adamw_small")
    shapes = [w[n].shape for n in small]
    for n, dd, mm, vv in zip(small, _unpack(d, shapes), _unpack(nm, shapes), _unpack(nv, shapes)):
        grads[n], deltas[n], new_m[n], new_v[n] = gsmall[n], dd, mm, vv
    return loss_total, grad_x, grads, deltas, new_m, new_v


def kernel(x, norm1_g, w_in, hgrn_lb_logits, hgrn_norm_g, gdn_conv_w, gdn_a_log, gdn_dt_bias, gdn_norm_g, lru_conv_w, lru_conv_b, lru_w_a, lru_b_a, lru_w_x, lru_b_x, lru_lambda, w_out, norm2_g, w_up, w_down, final_norm_g, loss_target, m_norm1_g, m_w_in, m_hgrn_lb_logits, m_hgrn_norm_g, m_gdn_conv_w, m_gdn_a_log, m_gdn_dt_bias, m_gdn_norm_g, m_lru_conv_w, m_lru_conv_b, m_lru_w_a, m_lru_b_a, m_lru_w_x, m_lru_b_x, m_lru_lambda, m_w_out, m_norm2_g, m_w_up, m_w_down, m_final_norm_g, v_norm1_g, v_w_in, v_hgrn_lb_logits, v_hgrn_norm_g, v_gdn_conv_w, v_gdn_a_log, v_gdn_dt_bias, v_gdn_norm_g, v_lru_conv_w, v_lru_conv_b, v_lru_w_a, v_lru_b_a, v_lru_w_x, v_lru_b_x, v_lru_lambda, v_w_out, v_norm2_g, v_w_up, v_w_down, v_final_norm_g):
    w = dict(zip(_WEIGHTS, (norm1_g, w_in, hgrn_lb_logits, hgrn_norm_g, gdn_conv_w, gdn_a_log, gdn_dt_bias, gdn_norm_g, lru_conv_w, lru_conv_b, lru_w_a, lru_b_a, lru_w_x, lru_b_x, lru_lambda, w_out, norm2_g, w_up, w_down, final_norm_g)))
    m = dict(zip(_WEIGHTS, (m_norm1_g, m_w_in, m_hgrn_lb_logits, m_hgrn_norm_g, m_gdn_conv_w, m_gdn_a_log, m_gdn_dt_bias, m_gdn_norm_g, m_lru_conv_w, m_lru_conv_b, m_lru_w_a, m_lru_b_a, m_lru_w_x, m_lru_b_x, m_lru_lambda, m_w_out, m_norm2_g, m_w_up, m_w_down, m_final_norm_g)))
    v = dict(zip(_WEIGHTS, (v_norm1_g, v_w_in, v_hgrn_lb_logits, v_hgrn_norm_g, v_gdn_conv_w, v_gdn_a_log, v_gdn_dt_bias, v_gdn_norm_g, v_lru_conv_w, v_lru_conv_b, v_lru_w_a, v_lru_b_a, v_lru_w_x, v_lru_b_x, v_lru_lambda, v_w_out, v_norm2_g, v_w_up, v_w_down, v_final_norm_g)))
    loss, grad_x, grads, deltas, new_m, new_v = _step(x.reshape(x.shape[1:]), loss_target.reshape(x.shape[1:]), w, m, v)
    return (loss, grad_x[None], *[grads[n] for n in _WEIGHTS], *[deltas[n] for n in _WEIGHTS],
            *[new_m[n] for n in _WEIGHTS], *[new_v[n] for n in _WEIGHTS])
```

```python
import jax
import jax.numpy as jnp
from jax import lax
from jax.experimental import pallas as pl
from jax.experimental.pallas import tpu as pltpu

F32 = jnp.float32
BF16 = jnp.bfloat16
MESH = pl.DeviceIdType.MESH

N_DEV = 8
D_MODEL = 1024
DEPTH = 4
A_HEADS, A_DIM, A_WIDTH = 4, 64, 256
B_HEADS, B_DIM, B_WIDTH = 4, 128, 512
C_WIDTH, C_BLOCKS, C_BLOCK_DIM = 256, 4, 64
D_IN = 3592
D_IN_PAD = 3840
COL_A, COL_B, COL_C, COL_G = 0, 1024, 3072, 3584
D_FF = 4096
CONV_K = 4
CHUNK = 64
SUB = 16
RG_C = 8.0
EPS = 1e-6
TINY = 1e-30
EXP_CLAMP = 80.0
GDN_SCALE = B_DIM ** -0.5
ADAM_LR, ADAM_B1, ADAM_B2, ADAM_EPS, ADAM_WD, ADAM_STEP = 0.001, 0.9, 0.999, 1e-08, 0.01, 10
VMEM_LIMIT = 56 * 1024 * 1024


def _cparams(sem=None):
    return pltpu.CompilerParams(dimension_semantics=sem, vmem_limit_bytes=VMEM_LIMIT)


_DIMS = {"nn": (((1,), (0,)), ((), ())), "nt": (((1,), (1,)), ((), ())), "tn": (((0,), (0,)), ((), ()))}


def _split_bf16(x):
    hi = x.astype(BF16)
    return hi, (x - hi.astype(F32)).astype(BF16)


def _dot(a, b, mode="nn", hi=False):
    if not hi:
        return lax.dot_general(a.astype(BF16), b.astype(BF16), _DIMS[mode], preferred_element_type=F32)
    ah, al = _split_bf16(a.astype(F32))
    bh, bl = _split_bf16(b.astype(F32))
    ka = 0 if mode == "tn" else 1
    kb = 1 if mode == "nt" else 0
    return lax.dot_general(jnp.concatenate([ah, ah, al], axis=ka), jnp.concatenate([bh, bl, bh], axis=kb),
                           _DIMS[mode], preferred_element_type=F32)


def _dot_exact_lhs(lhs, x, mode="nn"):
    l_bf16 = lhs.astype(BF16)
    x1 = x.astype(BF16)
    r1 = x - x1.astype(F32)
    x2 = r1.astype(BF16)
    x3 = (r1 - x2.astype(F32)).astype(BF16)
    ka = 0 if mode == "tn" else 1
    return lax.dot_general(jnp.concatenate([l_bf16] * 3, axis=ka), jnp.concatenate([x1, x2, x3], axis=0),
                           _DIMS[mode], preferred_element_type=F32)


def _iota2(n, m):
    return lax.broadcasted_iota(jnp.int32, (n, m), 0), lax.broadcasted_iota(jnp.int32, (n, m), 1)


def _tril(n, strict=False):
    r, c = _iota2(n, n)
    return (r > c) if strict else (r >= c)


def _sigmoid(x):
    return 1.0 / (1.0 + jnp.exp(-x))


def _softplus(x):
    return jnp.maximum(x, 0.0) + jnp.log(1.0 + jnp.exp(-jnp.abs(x)))


def _neg_expm1(z):
    series = -z * (1.0 + z * (0.5 + z * (1.0 / 6.0)))
    return jnp.where(z > -1e-2, series, 1.0 - jnp.exp(z))


def _gelu_tanh(x):
    c = 0.7978845608028654
    u = c * (x + 0.044715 * x * x * x)
    t = jnp.tanh(u)
    g = 0.5 * x * (1.0 + t)
    dg = 0.5 * (1.0 + t) + 0.5 * x * (1.0 - t * t) * c * (1.0 + 3.0 * 0.044715 * x * x)
    return g, dg


def _shift_rows(cur, halo, s, down=True):
    n = cur.shape[0]
    ridx = lax.broadcasted_iota(jnp.int32, (8, cur.shape[1]), 0)
    if down:
        main = pltpu.roll(cur, s, 0)
        fix = jnp.where(ridx < s, pltpu.roll(halo, s, 0), main[0:8])
        return jnp.concatenate([fix, main[8:]], axis=0)
    main = pltpu.roll(cur, n - s, 0)
    fix = jnp.where(ridx >= 8 - s, pltpu.roll(halo, 8 - s, 0), main[n - 8:n])
    return jnp.concatenate([main[:n - 8], fix], axis=0)


def _later_rows(dy, nxt8):
    return [_shift_rows(dy, nxt8, 3 - j, down=False) for j in range(3)] + [dy]


def _conv_bwd_rows(dy, nxt8, w):
    return sum(d * w[j:j + 1] for j, d in enumerate(_later_rows(dy, nxt8)))


def _tile_scan(a, b, reverse=False):
    n = a.shape[0]
    r = lax.broadcasted_iota(jnp.int32, a.shape, 0) % 8
    for s in (1, 2, 4):
        keep = (r < 8 - s) if reverse else (r >= s)
        shift = n - s if reverse else s
        a_sh = jnp.where(keep, pltpu.roll(a, shift, 0), 1.0)
        b_sh = jnp.where(keep, pltpu.roll(b, shift, 0), 0.0)
        b = b + a * b_sh
        a = a * a_sh
    return a, b


def _conv_fwd(cur, prev8, w):
    y = cur * w[3:4]
    for j in range(3):
        y = y + _shift_rows(cur, prev8, 3 - j, down=True) * w[j:j + 1]
    return y


def _mm_rows(a, w, mode, tm, name, residual=None, epilogue=None, up=None, norm=None, beside=None):
    parts = list(a) if isinstance(a, (list, tuple)) else [a]
    widths = [p.shape[1] for p in parts]
    t = parts[0].shape[0]
    n = w.shape[1] if mode == "nn" else w.shape[0]
    tm = min(tm, t)
    assert t % tm == 0 and all(wd % 128 == 0 for wd in widths)

    def body(*refs):
        a_refs, w_ref, rest = refs[:len(parts)], refs[len(parts)], refs[len(parts) + 1:]
        if epilogue == "rms_bwd":
            @pl.when(pl.program_id(0) == 0)
            def _():
                rest[4][...] = jnp.zeros_like(rest[4])

        for rs in ((slice(0, tm // 2), slice(tm // 2, tm)) if tm >= 512 else (slice(0, tm),)):
            y, off = None, 0
            for a_ref, width in zip(a_refs, widths):
                wk = w_ref[off:off + width, :] if mode == "nn" else w_ref[:, off:off + width]
                d = _dot(a_ref[rs, :], wk, mode)
                y = d if y is None else y + d
                off += width
            if residual is not None:
                y = y + rest[0][rs, :]
            if epilogue == "relu2":
                r = jnp.maximum(y, 0.0)
                refs[-2][rs, :] = (r * r).astype(BF16)
                refs[-1][rs, :] = y.astype(BF16)
            elif epilogue == "drelu2":
                refs[-1][rs, :] = (y * 2.0 * jnp.maximum(rest[0][rs, :].astype(F32), 0.0)).astype(BF16)
            elif epilogue == "rms_fwd":
                rinv = lax.rsqrt(jnp.mean(y * y, axis=-1, keepdims=True) + EPS)
                refs[-2][rs, :] = y
                refs[-1][rs, :] = (y * rinv * refs[-3][...]).astype(BF16)
            elif epilogue == "rms_bwd":
                x_ref, g_ref, dres_ref, dx_ref, dg_ref = rest
                xv = x_ref[rs, :]
                rinv = lax.rsqrt(jnp.mean(xv * xv, axis=-1, keepdims=True) + EPS)
                xhat = xv * rinv
                dxh = y * g_ref[...]
                dx_ref[rs, :] = dres_ref[rs, :] + rinv * (dxh - xhat * jnp.mean(dxh * xhat, axis=-1, keepdims=True))
                dg_ref[...] += jnp.sum(y * xhat, axis=0, keepdims=True)
            else:
                refs[-1][rs, :] = y

    rows = lambda width: pl.BlockSpec((tm, width), lambda i: (i, 0))
    vec = pl.BlockSpec((1, n), lambda i: (0, 0))
    ins, specs = parts + [w], [rows(wd) for wd in widths] + [pl.BlockSpec(w.shape, lambda i: (0, 0))]
    if residual is not None:
        ins.append(residual)
        specs.append(rows(n))
    if epilogue == "drelu2":
        ins.append(up)
        specs.append(rows(n))
    if epilogue == "rms_fwd":
        ins.append(norm.reshape(1, n))
        specs.append(vec)
        out_specs, out_shape = [rows(n), rows(n)], [jax.ShapeDtypeStruct((t, n), F32), jax.ShapeDtypeStruct((t, n), BF16)]
    elif epilogue == "rms_bwd":
        ins += [norm[0], norm[1].reshape(1, n), norm[2]]
        specs += [rows(n), vec, rows(n)]
        out_specs, out_shape = [rows(n), vec], [jax.ShapeDtypeStruct((t, n), F32), jax.ShapeDtypeStruct((1, n), F32)]
    elif epilogue == "relu2":
        out_specs, out_shape = [rows(n), rows(n)], [jax.ShapeDtypeStruct((t, n), BF16)] * 2
    else:
        out_specs, out_shape = rows(n), jax.ShapeDtypeStruct((t, n), BF16 if epilogue == "drelu2" else F32)
    if beside is not None:
        assert epilogue == "rms_bwd"
        return _call_beside(body, beside, t // tm, ins, grid=(t // tm,), in_specs=specs, out_specs=out_specs,
                            out_shape=out_shape, scratch_shapes=[], name=name)
    return pl.pallas_call(body, grid=(t // tm,), in_specs=specs, out_specs=out_specs, out_shape=out_shape,
                          compiler_params=_cparams(("arbitrary" if epilogue == "rms_bwd" else "parallel",)), name=name)(*ins)


MM_TN_TILE = 1024


def _mm_tn(a, b, tk, name, slab=None):
    a_parts = list(a) if isinstance(a, (list, tuple)) else [a]
    b_parts = list(b) if isinstance(b, (list, tuple)) else [b]
    wa, wb = [p.shape[1] for p in a_parts], [p.shape[1] for p in b_parts]
    t, m, n = a_parts[0].shape[0], sum(wa), sum(wb)
    tk = min(tk, t)
    assert t % tk == 0 and all(x % 128 == 0 for x in wa + wb)
    nk = t // tk

    def body(*refs):
        a_refs, b_refs = refs[:len(wa)], refs[len(wa):len(wa) + len(wb)]
        o_ref, acc = refs[-2], refs[-1]
        kk = pl.program_id(0)

        @pl.when(kk == 0)
        def _():
            acc[...] = jnp.zeros_like(acc)

        ro = 0
        for a_ref, width_a in zip(a_refs, wa):
            for r0 in range(0, width_a, MM_TN_TILE):
                rw = min(MM_TN_TILE, width_a - r0)
                av = a_ref[:, r0:r0 + rw]
                co = 0
                for b_ref, width_b in zip(b_refs, wb):
                    for c0 in range(0, width_b, MM_TN_TILE):
                        cw = min(MM_TN_TILE, width_b - c0)
                        acc[ro + r0:ro + r0 + rw, co + c0:co + c0 + cw] += _dot(av, b_ref[:, c0:c0 + cw], "tn")
                    co += width_b
            ro += width_a

        @pl.when(kk == nk - 1)
        def _():
            if slab is None:
                o_ref[...] = acc[...].astype(BF16)
            else:
                for s in range(n // slab):
                    o_ref[s] = acc[:, s * slab:(s + 1) * slab].astype(BF16)

    if slab is None:
        out_spec, out_shape = pl.BlockSpec((m, n), lambda kk: (0, 0)), jax.ShapeDtypeStruct((m, n), BF16)
    else:
        out_spec, out_shape = pl.BlockSpec((n // slab, m, slab), lambda kk: (0, 0, 0)), jax.ShapeDtypeStruct((n // slab, m, slab), BF16)
    return pl.pallas_call(
        body, grid=(nk,),
        in_specs=[pl.BlockSpec((tk, x), lambda kk: (kk, 0)) for x in wa + wb],
        out_specs=out_spec, out_shape=out_shape, scratch_shapes=[pltpu.VMEM((m, n), F32)],
        compiler_params=_cparams(("arbitrary",)), name=name)(*a_parts, *b_parts)


def _rms_fwd(x, g, tb=512, name="rms_fwd"):
    t, d = x.shape

    def body(x_ref, g_ref, h_ref):
        xv = x_ref[...]
        rinv = lax.rsqrt(jnp.mean(xv * xv, axis=-1, keepdims=True) + EPS)
        h_ref[...] = (xv * rinv * g_ref[...]).astype(BF16)

    return pl.pallas_call(
        body, grid=(t // tb,), in_specs=[pl.BlockSpec((tb, d), lambda i: (i, 0)), pl.BlockSpec((1, d), lambda i: (0, 0))],
        out_specs=pl.BlockSpec((tb, d), lambda i: (i, 0)), out_shape=jax.ShapeDtypeStruct((t, d), BF16),
        compiler_params=_cparams(("parallel",)), name=name)(x, g.reshape(1, d))


def _loss_head(x, g, target, tb=512):
    t, d = x.shape

    def body(x_ref, g_ref, t_ref, loss_ref, dx_ref, dg_ref):
        @pl.when(pl.program_id(0) == 0)
        def _():
            dg_ref[...] = jnp.zeros_like(dg_ref)
            loss_ref[...] = jnp.zeros_like(loss_ref)

        xv = x_ref[...]
        rinv = lax.rsqrt(jnp.mean(xv * xv, axis=-1, keepdims=True) + EPS)
        xhat = xv * rinv
        err = xhat * g_ref[...] - t_ref[...]
        loss_ref[...] += 0.5 * jnp.sum(jnp.mean(err * err, axis=-1, keepdims=True), axis=0, keepdims=True)
        dy = err * (1.0 / d)
        dxh = dy * g_ref[...]
        dx_ref[...] = rinv * (dxh - xhat * jnp.mean(dxh * xhat, axis=-1, keepdims=True))
        dg_ref[...] += jnp.sum(dy * xhat, axis=0, keepdims=True)

    row = pl.BlockSpec((tb, d), lambda i: (i, 0))
    vec = pl.BlockSpec((1, d), lambda i: (0, 0))
    one = pl.BlockSpec((1, 1), lambda i: (0, 0))
    return pl.pallas_call(
        body, grid=(t // tb,), in_specs=[row, vec, row], out_specs=[one, row, vec],
        out_shape=[jax.ShapeDtypeStruct((1, 1), F32), jax.ShapeDtypeStruct((t, d), F32), jax.ShapeDtypeStruct((1, d), F32)],
        compiler_params=_cparams(("arbitrary",)), name="loss_head")(x, g.reshape(1, d), target)


def _lb_fwd(logits):
    def body(l_ref, o_ref):
        lg = l_ref[...]
        e = jnp.exp(lg - jnp.max(lg, axis=0, keepdims=True))
        p = e / jnp.sum(e, axis=0, keepdims=True)
        c = jnp.zeros_like(p[0:1])
        rows = [c]
        for l in range(1, DEPTH):
            c = c + p[l:l + 1]
            rows.append(c)
        o_ref[...] = jnp.minimum(jnp.maximum(jnp.concatenate(rows, axis=0), 0.0), 1.0 - EPS)

    return pl.pallas_call(body, out_shape=jax.ShapeDtypeStruct(logits.shape, F32), name="lb_fwd")(logits)


def _lb_bwd(logits, dlb):
    def body(l_ref, d_ref, o_ref):
        lg = l_ref[...]
        e = jnp.exp(lg - jnp.max(lg, axis=0, keepdims=True))
        p = e / jnp.sum(e, axis=0, keepdims=True)
        hi = 1.0 - EPS
        c = jnp.zeros_like(p[0:1])
        dc = []
        for l in range(1, DEPTH):
            c = c + p[l:l + 1]
            gl = jnp.where(c < 0.0, 0.0, jnp.where(c == 0.0, 0.5, 1.0)) * jnp.where(c > hi, 0.0, jnp.where(c == hi, 0.5, 1.0))
            dc.append(d_ref[l:l + 1, :] * gl)
        dp = [jnp.zeros_like(c)]
        for j in range(1, DEPTH):
            s = dc[j - 1]
            for l in range(j + 1, DEPTH):
                s = s + dc[l - 1]
            dp.append(s)
        dpm = jnp.concatenate(dp, axis=0)
        o_ref[...] = p * (dpm - jnp.sum(p * dpm, axis=0, keepdims=True))

    return pl.pallas_call(body, out_shape=jax.ShapeDtypeStruct(logits.shape, F32), name="lb_bwd")(logits, dlb)


def _a_gates(qi, fi, lbh):
    sq = _sigmoid(qi)
    q = qi * sq
    e = jnp.exp(-jnp.abs(fi))
    rec = 1.0 / (1.0 + e)
    pos = fi >= 0.0
    sg = jnp.where(pos, rec, e * rec)
    sgn = jnp.where(pos, e * rec, rec)
    f = lbh + (1.0 - lbh) * sg
    logf = jnp.log(jnp.maximum(f, TINY))
    k = (1.0 - lbh) * sgn
    return q, sq, sg, sgn, f, logf, k


def _headnorm_fwd(o, g, gate_in):
    rinv = lax.rsqrt(jnp.mean(o * o, axis=-1, keepdims=True) + EPS)
    sg = _sigmoid(gate_in)
    return o * rinv * g * (gate_in * sg)


def _headnorm_bwd(dout, o, g, gate_in):
    rinv = lax.rsqrt(jnp.mean(o * o, axis=-1, keepdims=True) + EPS)
    xhat = o * rinv
    sg = _sigmoid(gate_in)
    silu = gate_in * sg
    dy = dout * silu
    dgate = dout * xhat * g * (sg * (1.0 + gate_in * (1.0 - sg)))
    dxh = dy * g
    do = rinv * (dxh - xhat * jnp.mean(dxh * xhat, axis=-1, keepdims=True))
    return do, dgate, jnp.sum(dy * xhat, axis=0, keepdims=True)


A_PAIRS, A_PAIR_W = A_HEADS // 2, 2 * A_DIM


def _lo_half(shape):
    return lax.broadcasted_iota(jnp.int32, shape, len(shape) - 1) < A_DIM


def _pair_blockdiag(x):
    lo = _lo_half(x.shape)
    return jnp.concatenate([jnp.where(lo, x, 0.0), jnp.where(lo, 0.0, x)], axis=0)


def _pair_fold(m):
    n = m.shape[0] // 2
    return jnp.where(_lo_half((n, A_PAIR_W)), m[:n], m[n:])


def _pair_norm_stats(o):
    lo = _lo_half(o.shape)
    sq = o * o
    s0 = jnp.sum(jnp.where(lo, sq, 0.0), axis=-1, keepdims=True)
    s1 = jnp.sum(sq, axis=-1, keepdims=True) - s0
    return jnp.where(lo, lax.rsqrt(s0 * (1.0 / A_DIM) + EPS), lax.rsqrt(s1 * (1.0 / A_DIM) + EPS))


def _pair_mean(x):
    lo = _lo_half(x.shape)
    s0 = jnp.sum(jnp.where(lo, x, 0.0), axis=-1, keepdims=True)
    s1 = jnp.sum(x, axis=-1, keepdims=True) - s0
    return jnp.where(lo, s0, s1) * (1.0 / A_DIM)


def _a_pair_chunk(qi, fi, v, lb2, s_bd, ltri, causal2):
    q, sq, sg, sgn, f, logf, k = _a_gates(qi, fi, lb2)
    cum = _dot_exact_lhs(ltri, logf)
    cl = cum[CHUNK - 1:CHUNK]
    ecum, ekd, cd = jnp.exp(cum), jnp.exp(cl - cum), jnp.exp(cl)
    qd, kd = q * ecum, k * ekd
    subs, rows = [], []
    for i in range(CHUNK // SUB):
        lo = i * SUB
        r = cum[lo - 1:lo] if i > 0 else jnp.zeros_like(cl)
        eq = jnp.exp(cum[lo:lo + SUB] - r)
        ek = jnp.exp(jnp.minimum(r - cum, EXP_CLAMP))
        qt = q[lo:lo + SUB] * eq
        kt_bd = _pair_blockdiag(k * ek)
        rows.append(_dot(qt, kt_bd, "nt", hi=True))
        subs.append((qt, eq, kt_bd, ek))
    attn = jnp.where(causal2, jnp.concatenate(rows, axis=0), 0.0)
    v_bd = _pair_blockdiag(v)
    o = _dot(qd, s_bd) + _dot(attn, v_bd)
    return dict(q=q, sq=sq, sg=sg, sgn=sgn, f=f, k=k, cum=cum, cl=cl, ecum=ecum, ekd=ekd, cd=cd, qd=qd, kd=kd,
                subs=subs, attn=attn, v_bd=v_bd, o=o)


def _a_fwd(proj, lb, norm_g, tb=256):
    t = proj.shape[0]
    nch = tb // CHUNK

    def body(q_ref, f_ref, i_ref, g_ref, lb_ref, ng_ref, out_ref, st_ref, s_scr):
        @pl.when(pl.program_id(0) == 0)
        def _():
            s_scr[...] = jnp.zeros_like(s_scr)

        ltri = _tril(CHUNK).astype(F32)
        r, c = _iota2(CHUNK, A_PAIR_W)
        causal2 = r >= c % CHUNK
        rb, cb = _iota2(A_PAIR_W, A_PAIR_W)
        diag_blocks = (rb < A_DIM) == (cb < A_DIM)

        def chunk(c, carry):
            rows = pl.ds(pl.multiple_of(c * CHUNK, CHUNK), CHUNK)
            ps = range(A_PAIRS)
            cols = [slice(p * A_PAIR_W, (p + 1) * A_PAIR_W) for p in ps]
            s0 = [s_scr[p] for p in ps]
            for p in ps:
                st_ref[c, p] = s0[p]
            v = [i_ref[rows, cols[p]] for p in ps]
            ch = [_a_pair_chunk(q_ref[rows, cols[p]], f_ref[rows, cols[p]], v[p], lb_ref[:, cols[p]], s0[p], ltri, causal2)
                  for p in ps]
            for p in ps:
                upd = jnp.where(diag_blocks, _dot(ch[p]["kd"], v[p], "tn"), 0.0)
                s_scr[p] = s0[p] * ch[p]["cd"].T + upd
            outs = []
            for p in ps:
                gi = g_ref[rows, cols[p]]
                outs.append(ch[p]["o"] * _pair_norm_stats(ch[p]["o"]) * ng_ref[...] * (gi * _sigmoid(gi)))
            out_ref[rows, :] = jnp.concatenate(outs, axis=1).astype(BF16)
            return carry

        lax.fori_loop(0, nch, chunk, 0, unroll=2)

    colblk = lambda j: pl.BlockSpec((tb, A_WIDTH), lambda i, j=j: (i, j))
    return pl.pallas_call(
        body, grid=(t // tb,),
        in_specs=[colblk(0), colblk(1), colblk(2), colblk(3), pl.BlockSpec((1, A_WIDTH), lambda i: (0, 0)),
                  pl.BlockSpec((1, A_PAIR_W), lambda i: (0, 0))],
        out_specs=[pl.BlockSpec((tb, A_WIDTH), lambda i: (i, 0)),
                   pl.BlockSpec((nch, A_PAIRS, A_PAIR_W, A_PAIR_W), lambda i: (i, 0, 0, 0))],
        out_shape=[jax.ShapeDtypeStruct((t, A_WIDTH), BF16),
                   jax.ShapeDtypeStruct((t // CHUNK, A_PAIRS, A_PAIR_W, A_PAIR_W), F32)],
        scratch_shapes=[pltpu.VMEM((A_PAIRS, A_PAIR_W, A_PAIR_W), F32)],
        compiler_params=_cparams(("arbitrary",)), name="hgrn_fwd")(
            proj, proj, proj, proj, lb.reshape(1, A_WIDTH), jnp.tile(norm_g.reshape(1, A_DIM), (1, 2)))


def _a_bwd(proj, lb, norm_g, states, dmixed, beside=None, tb=256):
    t = proj.shape[0]
    nch = tb // CHUNK
    nb = t // tb

    def body(q_ref, f_ref, i_ref, g_ref, lb_ref, ng_ref, st_ref, dm_ref, dp_ref, dlb_ref, dng_ref, ds_scr):
        @pl.when(pl.program_id(0) == 0)
        def _():
            ds_scr[...] = jnp.zeros_like(ds_scr)
            dlb_ref[...] = jnp.zeros_like(dlb_ref)
            dng_ref[...] = jnp.zeros_like(dng_ref)

        ltri = _tril(CHUNK).astype(F32)
        r, c = _iota2(CHUNK, A_PAIR_W)
        causal2 = r >= c % CHUNK
        rb, cb = _iota2(A_PAIR_W, A_PAIR_W)
        diag_blocks = (rb < A_DIM) == (cb < A_DIM)
        ones8 = jnp.ones((8, A_PAIR_W), F32)

        def chunk(cc, carry):
            c = nch - 1 - cc
            rows = pl.ds(pl.multiple_of(c * CHUNK, CHUNK), CHUNK)
            ps = range(A_PAIRS)
            cols = [slice(p * A_PAIR_W, (p + 1) * A_PAIR_W) for p in ps]
            qi = [q_ref[rows, cols[p]] for p in ps]
            gi = [g_ref[rows, cols[p]] for p in ps]
            v = [i_ref[rows, cols[p]] for p in ps]
            lb2 = [lb_ref[:, cols[p]] for p in ps]
            s0 = [st_ref[c, p] for p in ps]
            ds = [ds_scr[p] for p in ps]
            ch = [_a_pair_chunk(qi[p], f_ref[rows, cols[p]], v[p], lb2[p], s0[p], ltri, causal2) for p in ps]
            o = [ch[p]["o"] for p in ps]
            rinv = [_pair_norm_stats(o[p]) for p in ps]
            xhat = [o[p] * rinv[p] for p in ps]
            sgg = [_sigmoid(gi[p]) for p in ps]
            dout = [dm_ref[rows, cols[p]].astype(F32) for p in ps]
            dy = [dout[p] * (gi[p] * sgg[p]) for p in ps]
            dgi = [dout[p] * xhat[p] * ng_ref[...] * (sgg[p] * (1.0 + gi[p] * (1.0 - sgg[p]))) for p in ps]
            dxh = [dy[p] * ng_ref[...] for p in ps]
            do = [rinv[p] * (dxh[p] - xhat[p] * _pair_mean(dxh[p] * xhat[p])) for p in ps]
            dng = sum(jnp.sum(dy[p] * xhat[p], axis=0, keepdims=True) for p in ps)
            dqd = [_dot(do[p], s0[p], "nt") for p in ps]
            dattn = [jnp.where(causal2, _dot(do[p], ch[p]["v_bd"], "nt"), 0.0) for p in ps]
            dv = [_pair_fold(_dot(ch[p]["attn"], do[p], "tn")) + _dot(ch[p]["kd"], ds[p]) for p in ps]
            dkd = [_dot(v[p], ds[p], "nt") for p in ps]
            dcd = [_dot(ones8, s0[p] * ds[p], "nt", hi=True)[0:1] for p in ps]
            for p in ps:
                ds_scr[p] = jnp.where(diag_blocks, _dot(ch[p]["qd"], do[p], "tn"), 0.0) + ds[p] * ch[p]["cd"].T
            dq_i, dk_i = [], []
            for p in ps:
                dq_rows, dk = [], None
                for i, (qt, eq, kt_bd, ek) in enumerate(ch[p]["subs"]):
                    da = dattn[p][i * SUB:(i + 1) * SUB]
                    dq_rows.append(_dot(da, kt_bd, "nn", hi=True) * eq)
                    d = _pair_fold(_dot(da, qt, "tn", hi=True)) * ek
                    dk = d if dk is None else dk + d
                dq_i.append(jnp.concatenate(dq_rows, axis=0))
                dk_i.append(dk)
            dq = [dqd[p] * ch[p]["ecum"] + dq_i[p] for p in ps]
            dk = [dkd[p] * ch[p]["ekd"] + dk_i[p] for p in ps]
            dkk = [dkd[p] * ch[p]["kd"] for p in ps]
            dcum = [dqd[p] * ch[p]["qd"] - dkk[p] + ch[p]["q"] * dq_i[p] - ch[p]["k"] * dk_i[p] for p in ps]
            dcl = [jnp.sum(dkk[p], axis=0, keepdims=True) + dcd[p] * ch[p]["cd"] for p in ps]
            dlogf = [_dot_exact_lhs(ltri, dcum[p], "tn") + dcl[p] for p in ps]
            dfv = [jnp.where(ch[p]["f"] > TINY, dlogf[p] / ch[p]["f"], 0.0) for p in ps]
            dfi = [dfv[p] * (1.0 - lb2[p]) * ch[p]["sg"] * (1.0 - ch[p]["sg"])
                   - dk[p] * (1.0 - lb2[p]) * ch[p]["sgn"] * (1.0 - ch[p]["sgn"]) for p in ps]
            dlbs = [jnp.sum(dfv[p] * (1.0 - ch[p]["sg"]) - dk[p] * ch[p]["sgn"], axis=0, keepdims=True) for p in ps]
            dqs = [dq[p] * (ch[p]["sq"] * (1.0 + qi[p] * (1.0 - ch[p]["sq"]))) for p in ps]
            dp_ref[rows, :] = jnp.concatenate(dqs + dfi + dv + dgi, axis=1).astype(BF16)
            dlb_ref[...] += jnp.concatenate(dlbs, axis=1)
            dng_ref[...] += dng
            return carry

        lax.fori_loop(0, nch, chunk, 0, unroll=2)

    colblk = lambda j: pl.BlockSpec((tb, A_WIDTH), lambda i, j=j: (nb - 1 - i, j))
    vec = lambda n: pl.BlockSpec((1, n), lambda i: (0, 0))
    return _call_beside(
        body, beside, nb,
        (proj, proj, proj, proj, lb.reshape(1, A_WIDTH), jnp.tile(norm_g.reshape(1, A_DIM), (1, 2)), states, dmixed), grid=(nb,),
        in_specs=[colblk(0), colblk(1), colblk(2), colblk(3), vec(A_WIDTH), vec(A_PAIR_W),
                  pl.BlockSpec((nch, A_PAIRS, A_PAIR_W, A_PAIR_W), lambda i: (nb - 1 - i, 0, 0, 0)), colblk(0)],
        out_specs=[pl.BlockSpec((tb, 4 * A_WIDTH), lambda i: (nb - 1 - i, 0)), vec(A_WIDTH), vec(A_PAIR_W)],
        out_shape=[jax.ShapeDtypeStruct((t, 4 * A_WIDTH), BF16), jax.ShapeDtypeStruct((1, A_WIDTH), F32),
                   jax.ShapeDtypeStruct((1, A_PAIR_W), F32)],
        scratch_shapes=[pltpu.VMEM((A_PAIRS, A_PAIR_W, A_PAIR_W), F32)], name="hgrn_bwd")


def _gate_lane_masks(shape):
    lane = lax.broadcasted_iota(jnp.int32, shape, 1)
    return lane < B_HEADS, (lane >= B_HEADS) & (lane < 2 * B_HEADS)


def _b_pre_fwd(proj, conv_w, alog_row, dtb_row, tb=512):
    t = proj.shape[0]
    cb0 = COL_B // B_WIDTH

    def body(q_ref, k_ref, v_ref, qp_ref, kp_ref, vp_ref, w_ref, gi_ref, al_ref, dt_ref, qkv_ref, gates_ref, y_ref):
        first = pl.program_id(0) == 0
        for part, (c_ref, p_ref) in enumerate(((q_ref, qp_ref), (k_ref, kp_ref), (v_ref, vp_ref))):
            cols = slice(part * B_WIDTH, (part + 1) * B_WIDTH)
            prev = jnp.where(first, 0.0, p_ref[...])
            y = _conv_fwd(c_ref[...], prev, w_ref[:, cols])
            y_ref[:, cols] = y
            s = y * _sigmoid(y)
            if part < 2:
                outs = []
                for h in range(B_HEADS):
                    sh = s[:, h * B_DIM:(h + 1) * B_DIM]
                    outs.append(sh * lax.rsqrt(jnp.sum(sh * sh, axis=-1, keepdims=True) + EPS))
                s = jnp.concatenate(outs, axis=1)
            qkv_ref[:, cols] = s
        g = gi_ref[...]
        is_b, is_a = _gate_lane_masks(g.shape)
        la = -jnp.exp(al_ref[...]) * _softplus(g + dt_ref[...])
        gates_ref[...] = jnp.where(is_b, _sigmoid(g), jnp.where(is_a, la, 0.0))

    cur = lambda j: pl.BlockSpec((tb, B_WIDTH), lambda i, j=j: (i, cb0 + j))
    prv = lambda j: pl.BlockSpec((8, B_WIDTH), lambda i, j=j: (jnp.maximum(i * (tb // 8) - 1, 0), cb0 + j))
    vec = pl.BlockSpec((1, 128), lambda i: (0, 0))
    return pl.pallas_call(
        body, grid=(t // tb,),
        in_specs=[cur(0), cur(1), cur(2), prv(0), prv(1), prv(2), pl.BlockSpec((CONV_K, 3 * B_WIDTH), lambda i: (0, 0)),
                  pl.BlockSpec((tb, 128), lambda i: (i, COL_G // 128)), vec, vec],
        out_specs=[pl.BlockSpec((tb, 3 * B_WIDTH), lambda i: (i, 0)), pl.BlockSpec((tb, 128), lambda i: (i, 0)),
                   pl.BlockSpec((tb, 3 * B_WIDTH), lambda i: (i, 0))],
        out_shape=[jax.ShapeDtypeStruct((t, 3 * B_WIDTH), F32), jax.ShapeDtypeStruct((t, 128), F32),
                   jax.ShapeDtypeStruct((t, 3 * B_WIDTH), F32)],
        compiler_params=_cparams(("parallel",)), name="gdn_pre_fwd")(proj, proj, proj, proj, proj, proj, conv_w, proj, alog_row, dtb_row)


def _inv_unit_lower(amats):
    r, c = _iota2(CHUNK, CHUNK)
    eye = jnp.where(r == c, 1.0, 0.0)
    ps = [eye - a for a in amats]
    aks = amats
    for _ in range(5):
        aks = [_dot(ak, ak, hi=True) for ak in aks]
        ps = [p + _dot(p, ak, hi=True) for p, ak in zip(ps, aks)]
    return ps


def _b_local(qs, ks, vs, betas, gcs, grows, gls, solve=True):
    hs = range(len(qs))
    causal, strict = _tril(CHUNK), _tril(CHUNK, strict=True)
    decay = [jnp.where(causal, jnp.exp(jnp.minimum(gcs[h] - grows[h], 0.0)), 0.0) for h in hs]
    kb = [ks[h] * betas[h] for h in hs]
    kk = [_dot(kb[h], ks[h], "nt") for h in hs]
    qkr = [_dot(qs[h], ks[h], "nt") for h in hs]
    eg = [jnp.exp(gcs[h]) for h in hs]
    bv = [vs[h] * betas[h] for h in hs]
    kg = [kb[h] * eg[h] for h in hs]
    qk = [qkr[h] * decay[h] for h in hs]
    qd = [qs[h] * eg[h] for h in hs]
    ekd = [jnp.exp(gls[h] - gcs[h]) for h in hs]
    kd = [ks[h] * ekd[h] for h in hs]
    cd = [jnp.exp(gls[h]) for h in hs]
    loc = dict(decay=decay, kb=kb, kk=kk, eg=eg, bv=bv, kg=kg, qkr=qkr, qk=qk, qd=qd, ekd=ekd, kd=kd, cd=cd)
    if solve:
        tinv = _inv_unit_lower([jnp.where(strict, kk[h] * decay[h], 0.0) for h in hs])
        loc.update(tinv=tinv, u=[_dot(tinv[h], bv[h], hi=True) for h in hs], w=[_dot(tinv[h], kg[h], hi=True) for h in hs])
    return loc


def _b_state(loc, ids, s0s):
    n = range(len(ids))
    ws = [_dot(loc["w"][ids[j]], s0s[j]) for j in n]
    qs0 = [_dot(loc["qd"][ids[j]], s0s[j]) for j in n]
    vn = [loc["u"][ids[j]] - ws[j] for j in n]
    o = [qs0[j] + _dot(loc["qk"][ids[j]], vn[j]) for j in n]
    s1 = [s0s[j] * loc["cd"][ids[j]] + _dot(loc["kd"][ids[j]], vn[j], "tn") for j in n]
    return vn, o, s1


def _b_fwd(qkv, gates, proj, norm_g, beside=None, tb=256):
    t = qkv.shape[0]
    nch = tb // CHUNK

    def body(q_ref, k_ref, v_ref, ga_ref, z_ref, ng_ref, out_ref, st_ref, ti_ref, w_ref, vn_ref, o_ref, s_scr):
        @pl.when(pl.program_id(0) == 0)
        def _():
            s_scr[...] = jnp.zeros_like(s_scr)

        ltri = _tril(CHUNK).astype(F32)

        hs = range(B_HEADS)
        cols = [slice(h * B_DIM, (h + 1) * B_DIM) for h in hs]

        def pair(p, carry):
            cs = [2 * p, 2 * p + 1]
            rows = [pl.ds(pl.multiple_of(c * CHUNK, CHUNK), CHUNK) for c in cs]
            ga = [ga_ref[r, :] for r in rows]
            gcum = [_dot_exact_lhs(ltri, g) for g in ga]
            gcum_t = [g.T for g in gcum]
            items = [(i, h) for i in range(2) for h in hs]
            loc = _b_local([q_ref[rows[i], cols[h]] * GDN_SCALE for i, h in items], [k_ref[rows[i], cols[h]] for i, h in items],
                           [v_ref[rows[i], cols[h]] for i, h in items], [ga[i][:, h:h + 1] for i, h in items],
                           [gcum[i][:, B_HEADS + h:B_HEADS + h + 1] for i, h in items],
                           [gcum_t[i][B_HEADS + h:B_HEADS + h + 1, :] for i, h in items],
                           [gcum[i][CHUNK - 1:CHUNK, B_HEADS + h:B_HEADS + h + 1] for i, h in items])
            s0s = [s_scr[h] for h in hs]
            for i in range(2):
                ids = [i * B_HEADS + h for h in hs]
                for h in hs:
                    st_ref[cs[i], h] = s0s[h]
                    ti_ref[cs[i], h] = loc["tinv"][ids[h]]
                vn, o, s0s = _b_state(loc, ids, s0s)
                w_ref[rows[i], :] = jnp.concatenate([loc["w"][j] for j in ids], axis=1)
                vn_ref[rows[i], :] = jnp.concatenate(vn, axis=1)
                o_ref[rows[i], :] = jnp.concatenate(o, axis=1)
                outs = [_headnorm_fwd(o[h], ng_ref[...], z_ref[rows[i], cols[h]]) for h in hs]
                out_ref[rows[i], :] = jnp.concatenate(outs, axis=1).astype(BF16)
            for h in hs:
                s_scr[h] = s0s[h]
            return carry

        lax.fori_loop(0, nch // 2, pair, 0)

    part = lambda j: pl.BlockSpec((tb, B_WIDTH), lambda i, j=j: (i, j))
    wide = pl.BlockSpec((tb, B_WIDTH), lambda i: (i, 0))
    wide_shape = jax.ShapeDtypeStruct((t, B_WIDTH), F32)
    return _call_beside(
        body, beside, t // tb, (qkv, qkv, qkv, gates, proj, norm_g.reshape(1, B_DIM)), grid=(t // tb,),
        in_specs=[part(0), part(1), part(2), pl.BlockSpec((tb, 128), lambda i: (i, 0)),
                  pl.BlockSpec((tb, B_WIDTH), lambda i: (i, COL_B // B_WIDTH + 3)), pl.BlockSpec((1, B_DIM), lambda i: (0, 0))],
        out_specs=[wide, pl.BlockSpec((nch, B_HEADS, B_DIM, B_DIM), lambda i: (i, 0, 0, 0)),
                   pl.BlockSpec((nch, B_HEADS, CHUNK, CHUNK), lambda i: (i, 0, 0, 0)), wide, wide, wide],
        out_shape=[jax.ShapeDtypeStruct((t, B_WIDTH), BF16), jax.ShapeDtypeStruct((t // CHUNK, B_HEADS, B_DIM, B_DIM), F32),
                   jax.ShapeDtypeStruct((t // CHUNK, B_HEADS, CHUNK, CHUNK), F32), wide_shape, wide_shape, wide_shape],
        scratch_shapes=[pltpu.VMEM((B_HEADS, B_DIM, B_DIM), F32)], name="gdn_fwd")


def _b_bwd(qkv, gates, proj, norm_g, states, fwd_saved, dmixed, beside=None, tb=256):
    t = qkv.shape[0]
    nch = tb // CHUNK
    nb = t // tb

    def body(q_ref, k_ref, v_ref, ga_ref, z_ref, ng_ref, st_ref, ti_ref, w_ref, vn_ref, o_ref, dm0_ref, dm1_ref,
             dqkv_ref, dga_ref, dz_ref, dng_ref, ds_scr):
        @pl.when(pl.program_id(0) == 0)
        def _():
            ds_scr[...] = jnp.zeros_like(ds_scr)
            dng_ref[...] = jnp.zeros_like(dng_ref)

        ltri = _tril(CHUNK).astype(F32)
        strict = _tril(CHUNK, strict=True)
        lane = lax.broadcasted_iota(jnp.int32, (CHUNK, 128), 1)
        lane1 = lax.broadcasted_iota(jnp.int32, (1, 128), 1)

        nh = range(B_HEADS)
        cols = [slice(h * B_DIM, (h + 1) * B_DIM) for h in nh]
        rsum = lambda a: jnp.sum(a, axis=-1, keepdims=True)

        def pair(p, carry):
            cs = [nch - 1 - 2 * p, nch - 2 - 2 * p]
            crow = [pl.ds(pl.multiple_of(c * CHUNK, CHUNK), CHUNK) for c in cs]
            gas = [ga_ref[r, :] for r in crow]
            gcum = [_dot_exact_lhs(ltri, g) for g in gas]
            gcum_t = [g.T for g in gcum]
            items = [(i, h) for i in range(2) for h in nh]
            hs = range(len(items))
            q = [q_ref[crow[i], cols[h]] * GDN_SCALE for i, h in items]
            k = [k_ref[crow[i], cols[h]] for i, h in items]
            v = [v_ref[crow[i], cols[h]] for i, h in items]
            z = [z_ref[crow[i], cols[h]] for i, h in items]
            beta = [gas[i][:, h:h + 1] for i, h in items]
            s0 = [st_ref[cs[i], h] for i, h in items]
            r = _b_local(q, k, v, beta, [gcum[i][:, B_HEADS + h:B_HEADS + h + 1] for i, h in items],
                         [gcum_t[i][B_HEADS + h:B_HEADS + h + 1, :] for i, h in items],
                         [gcum[i][CHUNK - 1:CHUNK, B_HEADS + h:B_HEADS + h + 1] for i, h in items], solve=False)
            tinv = [ti_ref[cs[i], h] for i, h in items]
            w = [w_ref[crow[i], cols[h]] for i, h in items]
            vn = [vn_ref[crow[i], cols[h]] for i, h in items]
            decay, eg, qd, kd, kb, cd = (r[n] for n in ("decay", "eg", "qd", "kd", "kb", "cd"))
            dms = [(dm0_ref if h < 2 else dm1_ref)[crow[i], (h % 2) * B_DIM:(h % 2 + 1) * B_DIM].astype(F32) for i, h in items]
            hn = [_headnorm_bwd(dms[j], o_ref[crow[i], cols[h]], ng_ref[...], z[j]) for j, (i, h) in enumerate(items)]
            do = [hn[j][0] for j in hs]
            dvn_o = [_dot(r["qk"][j], do[j], "tn") for j in hs]
            dqk = [_dot(do[j], vn[j], "nt") for j in hs]
            dqd = [_dot(do[j], s0[j], "nt") for j in hs]
            ds_o = [_dot(qd[j], do[j], "tn") for j in hs]
            ds = [ds_scr[h] for h in nh]
            dvn, dkd, dcd = [None] * 8, [None] * 8, [None] * 8
            for i in range(2):
                for h in nh:
                    j = i * B_HEADS + h
                    dvn[j] = dvn_o[j] + _dot(kd[j], ds[h])
                    dkd[j] = _dot(vn[j], ds[h], "nt")
                    dcd[j] = jnp.sum(jnp.sum(s0[j] * ds[h], axis=0, keepdims=True), axis=1, keepdims=True)
                ds = [ds_o[i * B_HEADS + h] + ds[h] * cd[i * B_HEADS + h] - _dot(w[i * B_HEADS + h], dvn[i * B_HEADS + h], "tn")
                      for h in nh]
            for h in nh:
                ds_scr[h] = ds[h]
            dw = [-_dot(dvn[j], s0[j], "nt") for j in hs]
            dbv = [_dot(tinv[h], dvn[h], "tn", hi=True) for h in hs]
            dkg = [_dot(tinv[h], dw[h], "tn", hi=True) for h in hs]
            dt = [_dot(dvn[h], r["bv"][h], "nt", hi=True) + _dot(dw[h], r["kg"][h], "nt", hi=True) for h in hs]
            tdt = [_dot(tinv[h], dt[h], "tn", hi=True) for h in hs]
            da = [jnp.where(strict, -_dot(tdt[h], tinv[h], "nt", hi=True), 0.0) for h in hs]
            dm = [da[h] * decay[h] for h in hs]
            dn = [dqk[h] * decay[h] for h in hs]
            e = [(da[h] * r["kk"][h] + dqk[h] * r["qkr"][h]) * decay[h] for h in hs]
            dkb = [_dot(dm[h], k[h]) + dkg[h] * eg[h] for h in hs]
            dk = [_dot(dm[h], kb[h], "tn") + _dot(dn[h], q[h], "tn") + dkd[h] * r["ekd"][h] + dkb[h] * beta[h] for h in hs]
            dq = [_dot(dn[h], k[h]) + dqd[h] * eg[h] for h in hs]
            tkd = [rsum(dkd[h] * kd[h]) for h in hs]
            dgc = [rsum(e[h]) - rsum(e[h].T) + rsum(dqd[h] * qd[h]) - tkd[h] + rsum(dkg[h] * r["kg"][h]) for h in hs]
            dgl = [jnp.sum(tkd[h], axis=0, keepdims=True) + dcd[h] * cd[h] for h in hs]
            dbeta = [rsum(dbv[h] * v[h]) + rsum(dkb[h] * k[h]) for h in hs]
            for i in range(2):
                ids = [i * B_HEADS + h for h in nh]
                dbeta_m = sum(jnp.where(lane == h, dbeta[ids[h]], 0.0) for h in nh)
                dgc_m = sum(jnp.where(lane == B_HEADS + h, dgc[ids[h]], 0.0) for h in nh)
                dgl_m = sum(jnp.where(lane1 == B_HEADS + h, dgl[ids[h]], 0.0) for h in nh)
                dqkv_ref[crow[i], :] = jnp.concatenate(
                    [dq[j] * GDN_SCALE for j in ids] + [dk[j] for j in ids] + [dbv[j] * beta[j] for j in ids], axis=1)
                dz_ref[crow[i], :] = jnp.concatenate([hn[j][1] for j in ids], axis=1).astype(BF16)
                dga_ref[crow[i], :] = dbeta_m + _dot_exact_lhs(ltri, dgc_m, "tn") + dgl_m
            dng_ref[...] += sum(hn[j][2] for j in hs)
            return carry

        lax.fori_loop(0, nch // 2, pair, 0)

    part = lambda j: pl.BlockSpec((tb, B_WIDTH), lambda i, j=j: (nb - 1 - i, j))
    rowblk = lambda w, j=0: pl.BlockSpec((tb, w), lambda i, j=j: (nb - 1 - i, j))
    return _call_beside(
        body, beside, nb, (qkv, qkv, qkv, gates, proj, norm_g.reshape(1, B_DIM), states, *fwd_saved, dmixed, dmixed), grid=(nb,),
        in_specs=[part(0), part(1), part(2), rowblk(128), rowblk(B_WIDTH, COL_B // B_WIDTH + 3),
                  pl.BlockSpec((1, B_DIM), lambda i: (0, 0)),
                  pl.BlockSpec((nch, B_HEADS, B_DIM, B_DIM), lambda i: (nb - 1 - i, 0, 0, 0)),
                  pl.BlockSpec((nch, B_HEADS, CHUNK, CHUNK), lambda i: (nb - 1 - i, 0, 0, 0)),
                  rowblk(B_WIDTH), rowblk(B_WIDTH), rowblk(B_WIDTH), rowblk(256, 1), rowblk(256, 2)],
        out_specs=[rowblk(3 * B_WIDTH), rowblk(128), rowblk(B_WIDTH), pl.BlockSpec((1, B_DIM), lambda i: (0, 0))],
        out_shape=[jax.ShapeDtypeStruct((t, 3 * B_WIDTH), F32), jax.ShapeDtypeStruct((t, 128), F32),
                   jax.ShapeDtypeStruct((t, B_WIDTH), BF16), jax.ShapeDtypeStruct((1, B_DIM), F32)],
        scratch_shapes=[pltpu.VMEM((B_HEADS, B_DIM, B_DIM), F32)], name="gdn_bwd")


def _b_pre_bwd(proj, y3, conv_w, alog_row, dtb_row, dqkv, dgates, tb=512):
    t = proj.shape[0]
    nb = t // tb
    cb0 = COL_B // B_WIDTH

    def body(q_ref, k_ref, v_ref, y_ref, w_ref, gi_ref, al_ref, dt_ref, dqkv_ref, dga_ref,
             dy_ref, dgi_ref, dw_ref, dal_ref, ddt_ref, nxt_scr):
        step_id = pl.program_id(0)

        @pl.when(step_id == 0)
        def _():
            dw_ref[...] = jnp.zeros_like(dw_ref)
            dal_ref[...] = jnp.zeros_like(dal_ref)
            ddt_ref[...] = jnp.zeros_like(ddt_ref)

        for part, c_ref in enumerate((q_ref, k_ref, v_ref)):
            cols = slice(part * B_WIDTH, (part + 1) * B_WIDTH)
            cur = c_ref[...]
            w = w_ref[:, cols]
            y = y_ref[:, cols]
            sg = _sigmoid(y)
            s = y * sg
            dsn = dqkv_ref[:, cols]
            if part < 2:
                outs = []
                for h in range(B_HEADS):
                    hc = slice(h * B_DIM, (h + 1) * B_DIM)
                    sh, dh = s[:, hc], dsn[:, hc]
                    rq = lax.rsqrt(jnp.sum(sh * sh, axis=-1, keepdims=True) + EPS)
                    nh = sh * rq
                    outs.append(rq * (dh - nh * jnp.sum(dh * nh, axis=-1, keepdims=True)))
                dsn = jnp.concatenate(outs, axis=1)
            dy = dsn * (sg * (1.0 + y * (1.0 - sg)))
            later = _later_rows(dy, jnp.where(step_id == 0, 0.0, nxt_scr[:, cols]))
            dy_ref[:, cols] = sum(later[j] * w[j:j + 1] for j in range(CONV_K)).astype(BF16)
            nxt_scr[:, cols] = dy[0:8]
            dw_ref[:, cols] += jnp.concatenate([jnp.sum(cur * later[j], axis=0, keepdims=True) for j in range(CONV_K)], axis=0)
        g = gi_ref[...]
        dga = dga_ref[...]
        is_b, is_a = _gate_lane_masks(g.shape)
        beta = _sigmoid(g)
        pre = g + dt_ref[...]
        ea = jnp.exp(al_ref[...])
        la = -ea * _softplus(pre)
        dpre = jnp.where(is_a, dga * (-ea) * _sigmoid(pre), 0.0)
        dgi_ref[...] = jnp.where(is_b, dga * beta * (1.0 - beta), dpre).astype(BF16)
        dal_ref[...] += jnp.sum(jnp.where(is_a, dga * la, 0.0), axis=0, keepdims=True)
        ddt_ref[...] += jnp.sum(dpre, axis=0, keepdims=True)

    cur = lambda j: pl.BlockSpec((tb, B_WIDTH), lambda i, j=j: (nb - 1 - i, cb0 + j))
    vec = pl.BlockSpec((1, 128), lambda i: (0, 0))
    wspec = pl.BlockSpec((CONV_K, 3 * B_WIDTH), lambda i: (0, 0))
    rowblk = lambda width, j=0: pl.BlockSpec((tb, width), lambda i, j=j: (nb - 1 - i, j))
    return pl.pallas_call(
        body, grid=(nb,),
        in_specs=[cur(0), cur(1), cur(2), rowblk(3 * B_WIDTH), wspec, rowblk(128, COL_G // 128), vec, vec,
                  rowblk(3 * B_WIDTH), rowblk(128)],
        out_specs=[rowblk(3 * B_WIDTH), rowblk(128), wspec, vec, vec],
        out_shape=[jax.ShapeDtypeStruct((t, 3 * B_WIDTH), BF16), jax.ShapeDtypeStruct((t, 128), BF16),
                   jax.ShapeDtypeStruct((CONV_K, 3 * B_WIDTH), F32), jax.ShapeDtypeStruct((1, 128), F32), jax.ShapeDtypeStruct((1, 128), F32)],
        scratch_shapes=[pltpu.VMEM((8, 3 * B_WIDTH), F32)],
        compiler_params=_cparams(("arbitrary",)), name="gdn_pre_bwd")(
            proj, proj, proj, y3, conv_w, proj, alog_row, dtb_row, dqkv, dgates)


def _c_gates(xc, wa_ref, ba_ref, wx_ref, bx_ref, lam_ref, is_row0):
    r = _sigmoid(_dot(xc, wa_ref[...]) + ba_ref[...])
    i = _sigmoid(_dot(xc, wx_ref[...]) + bx_ref[...])
    sp = _softplus(-lam_ref[...])
    log_a = -RG_C * r * sp
    a = jnp.exp(log_a)
    m2 = _neg_expm1(2.0 * log_a)
    mult = jnp.where(is_row0, 1.0, jnp.sqrt(jnp.maximum(m2, EPS)))
    return r, i, sp, log_a, a, m2, mult


def _row0_mask(tb, first):
    ridx = lax.broadcasted_iota(jnp.int32, (tb, C_WIDTH), 0)
    return (ridx == 0) & first


def _c_fwd(proj, conv_w, conv_b, wa, ba, wx, bx, lam, tb=512):
    t = proj.shape[0]
    cbx = COL_C // C_WIDTH

    def body(x_ref, xp_ref, y_ref, w_ref, cb_ref, wa_ref, ba_ref, wx_ref, bx_ref, lam_ref, out_ref, h_ref, a_scr, b_scr, h_scr):
        first = pl.program_id(0) == 0

        @pl.when(first)
        def _():
            h_scr[...] = jnp.zeros_like(h_scr)

        prev = jnp.where(first, 0.0, xp_ref[...])
        xc = _conv_fwd(x_ref[...], prev, w_ref[...]) + cb_ref[...]
        _, i, _, _, a, _, mult = _c_gates(xc, wa_ref, ba_ref, wx_ref, bx_ref, lam_ref, _row0_mask(tb, first))
        ta, tb_ = _tile_scan(a, mult * i * xc)
        a_scr[...] = ta
        b_scr[...] = tb_

        def step(blk, h):
            rows = pl.ds(pl.multiple_of(blk * 8, 8), 8)
            h_ref[rows, :] = jnp.broadcast_to(h, (8, C_WIDTH))
            return a_scr[rows, :][7:8] * h + b_scr[rows, :][7:8]

        h_scr[...] = lax.fori_loop(0, tb // 8, step, h_scr[...], unroll=8)
        hs = ta * h_ref[...] + tb_
        h_ref[...] = hs
        gl, _ = _gelu_tanh(y_ref[...])
        out_ref[...] = (gl * hs).astype(BF16)

    vec = pl.BlockSpec((1, C_WIDTH), lambda i: (0, 0))
    mat = pl.BlockSpec((C_WIDTH, C_WIDTH), lambda i: (0, 0))
    row = pl.BlockSpec((tb, C_WIDTH), lambda i: (i, 0))
    return pl.pallas_call(
        body, grid=(t // tb,),
        in_specs=[pl.BlockSpec((tb, C_WIDTH), lambda i: (i, cbx)),
                  pl.BlockSpec((8, C_WIDTH), lambda i: (jnp.maximum(i * (tb // 8) - 1, 0), cbx)),
                  pl.BlockSpec((tb, C_WIDTH), lambda i: (i, cbx + 1)),
                  pl.BlockSpec((CONV_K, C_WIDTH), lambda i: (0, 0)), vec, mat, vec, mat, vec, vec],
        out_specs=[row, row],
        out_shape=[jax.ShapeDtypeStruct((t, C_WIDTH), BF16), jax.ShapeDtypeStruct((t, C_WIDTH), F32)],
        scratch_shapes=[pltpu.VMEM((tb, C_WIDTH), F32), pltpu.VMEM((tb, C_WIDTH), F32), pltpu.VMEM((1, C_WIDTH), F32)],
        compiler_params=_cparams(("arbitrary",)), name="lru_fwd")(proj, proj, proj, conv_w, conv_b, wa, ba, wx, bx, lam)


def _c_bwd(proj, conv_w, conv_b, wa, ba, wx, bx, lam, hs, dmixed, tb=512):
    t = proj.shape[0]
    nb = t // tb
    cbx = COL_C // C_WIDTH

    def body(x_ref, xp_ref, y_ref, w_ref, cb_ref, wa_ref, ba_ref, wx_ref, bx_ref, lam_ref, h_ref, hp_ref, dm_ref,
             dxc_ref, dyg_ref, dw_ref, dcb_ref, dwa_ref, dba_ref, dwx_ref, dbx_ref, dlam_ref, g_scr, a_scr, cin_scr, c_scr, nxt_scr):
        step_id = pl.program_id(0)
        first = step_id == nb - 1

        @pl.when(step_id == 0)
        def _():
            c_scr[...] = jnp.zeros_like(c_scr)
            for ref in (dw_ref, dcb_ref, dwa_ref, dba_ref, dwx_ref, dbx_ref, dlam_ref):
                ref[...] = jnp.zeros_like(ref)

        cur = x_ref[...]
        prev = jnp.where(first, 0.0, xp_ref[...])
        w = w_ref[...]
        shifted = [_shift_rows(cur, prev, 3 - j, down=True) for j in range(3)] + [cur]
        xc = shifted[0] * w[0:1] + shifted[1] * w[1:2] + shifted[2] * w[2:3] + shifted[3] * w[3:4] + cb_ref[...]
        row0 = _row0_mask(tb, first)
        r, i, sp, log_a, a, m2, mult = _c_gates(xc, wa_ref, ba_ref, wx_ref, bx_ref, lam_ref, row0)
        h = h_ref[...]
        hprev = _shift_rows(h, jnp.where(first, 0.0, hp_ref[...]), 1, down=True)
        gl, dgl = _gelu_tanh(y_ref[...])
        dm = dm_ref[...].astype(F32)
        dyg_ref[...] = (dm * h * dgl).astype(BF16)
        dout = dm * gl
        ta, te = _tile_scan(a, a * dout, reverse=True)
        a_scr[...] = ta
        g_scr[...] = te

        def step(blk, carry):
            rows = pl.ds(pl.multiple_of((tb // 8 - 1 - blk) * 8, 8), 8)
            cin_scr[rows, :] = jnp.broadcast_to(carry, (8, C_WIDTH))
            return a_scr[rows, :][0:1] * carry + g_scr[rows, :][0:1]

        c_scr[...] = lax.fori_loop(0, tb // 8, step, c_scr[...], unroll=8)
        cin = cin_scr[...]
        cout = ta * cin + te
        last_in_tile = lax.broadcasted_iota(jnp.int32, (tb, C_WIDTH), 0) % 8 == 7
        dbx = dout + jnp.where(last_in_tile, cin, pltpu.roll(cout, tb - 1, 0))
        da = dbx * hprev
        dmult = jnp.where(row0, 0.0, dbx * i * xc)
        di = dbx * mult * xc
        dxc = dbx * mult * i
        dm2 = jnp.where(m2 > EPS, dmult * 0.5 / mult, 0.0)
        dlog_a = da * a - 2.0 * a * a * dm2
        dr = dlog_a * (-RG_C) * sp
        dlam_ref[...] += jnp.sum(dlog_a * (-RG_C) * r, axis=0, keepdims=True) * (-_sigmoid(-lam_ref[...]))
        dpa = dr * r * (1.0 - r)
        dpx = di * i * (1.0 - i)
        dba_ref[...] += jnp.sum(dpa, axis=0, keepdims=True)
        dbx_ref[...] += jnp.sum(dpx, axis=0, keepdims=True)
        dwa_ref[...] += _dot(xc, dpa, "tn")
        dwx_ref[...] += _dot(xc, dpx, "tn")
        dxc = dxc + _dot(dpa, wa_ref[...], "nt") + _dot(dpx, wx_ref[...], "nt")
        dxc_ref[...] = _conv_bwd_rows(dxc, jnp.where(step_id == 0, 0.0, nxt_scr[...]), w).astype(BF16)
        nxt_scr[...] = dxc[0:8]
        dcb_ref[...] += jnp.sum(dxc, axis=0, keepdims=True)
        dw_ref[...] += jnp.concatenate([jnp.sum(shifted[j] * dxc, axis=0, keepdims=True) for j in range(CONV_K)], axis=0)

    vec = pl.BlockSpec((1, C_WIDTH), lambda i: (0, 0))
    mat = pl.BlockSpec((C_WIDTH, C_WIDTH), lambda i: (0, 0))
    cw = pl.BlockSpec((CONV_K, C_WIDTH), lambda i: (0, 0))
    row = lambda j=0: pl.BlockSpec((tb, C_WIDTH), lambda i, j=j: (nb - 1 - i, j))
    halo = lambda j=0: pl.BlockSpec((8, C_WIDTH), lambda i, j=j: (jnp.maximum((nb - 1 - i) * (tb // 8) - 1, 0), j))
    return pl.pallas_call(
        body, grid=(nb,),
        in_specs=[row(cbx), halo(cbx), row(cbx + 1), cw, vec, mat, vec, mat, vec, vec, row(), halo(), row(3)],
        out_specs=[row(), row(), cw, vec, mat, vec, mat, vec, vec],
        out_shape=[jax.ShapeDtypeStruct((t, C_WIDTH), BF16), jax.ShapeDtypeStruct((t, C_WIDTH), BF16),
                   jax.ShapeDtypeStruct((CONV_K, C_WIDTH), F32), jax.ShapeDtypeStruct((1, C_WIDTH), F32),
                   jax.ShapeDtypeStruct((C_WIDTH, C_WIDTH), F32), jax.ShapeDtypeStruct((1, C_WIDTH), F32),
                   jax.ShapeDtypeStruct((C_WIDTH, C_WIDTH), F32), jax.ShapeDtypeStruct((1, C_WIDTH), F32),
                   jax.ShapeDtypeStruct((1, C_WIDTH), F32)],
        scratch_shapes=[pltpu.VMEM((tb, C_WIDTH), F32), pltpu.VMEM((tb, C_WIDTH), F32), pltpu.VMEM((tb, C_WIDTH), F32),
                        pltpu.VMEM((1, C_WIDTH), F32), pltpu.VMEM((8, C_WIDTH), F32)],
        compiler_params=_cparams(("arbitrary",)), name="lru_bwd")(
            proj, proj, proj, conv_w, conv_b, wa, ba, wx, bx, lam, hs, hs, dmixed)


def _mesh_pos():
    return lax.axis_index("x"), lax.axis_index("y"), lax.axis_index("c")


class _Exchange:
    def __init__(self, arrays, layouts):
        self.arrays, self.layouts = list(arrays), list(layouts)
        self.out_shapes = []
        for a, lay in zip(self.arrays, self.layouts):
            if lay == 'a2a':
                shp = a.shape
            elif lay == 'slot':
                shp = (N_DEV,) + a.shape
            elif lay == 'rows':
                shp = (N_DEV * a.shape[0], a.shape[1])
            else:
                shp = (a.shape[0], N_DEV * a.shape[1])
            self.out_shapes.append(jax.ShapeDtypeStruct(shp, a.dtype))
        n = len(self.arrays)
        self.scratch = [pltpu.SemaphoreType.DMA((7 * n,)), pltpu.SemaphoreType.DMA((7 * n,)), pltpu.SemaphoreType.DMA((n,))]

    def _landing(self, a, dst_ref, idx):
        lay, shape = self.layouts[a], self.arrays[a].shape
        if lay in ('a2a', 'slot'):
            return dst_ref.at[idx]
        if lay == 'rows':
            return dst_ref.at[pl.ds(pl.multiple_of(idx * shape[0], shape[0]), shape[0]), :]
        return dst_ref.at[:, pl.ds(pl.multiple_of(idx * shape[1], shape[1]), shape[1])]

    def copies(self, src_refs, dst_refs, send_sems, recv_sems, local_sems):
        mx, my, mc = _mesh_pos()
        me = 4 * mx + 2 * my + mc
        out = []
        for a, (src, dst) in enumerate(zip(src_refs, dst_refs)):
            a2a = self.layouts[a] == 'a2a'
            out.append(pltpu.make_async_copy(src.at[me] if a2a else src, self._landing(a, dst, me), local_sems.at[a]))
            for k in range(1, N_DEV):
                px = 1 - mx if k & 4 else mx
                py = 1 - my if k & 2 else my
                pc = 1 - mc if k & 1 else mc
                out.append(pltpu.make_async_remote_copy(
                    src_ref=src.at[4 * px + 2 * py + pc] if a2a else src, dst_ref=self._landing(a, dst, me),
                    send_sem=send_sems.at[7 * a + k - 1], recv_sem=recv_sems.at[7 * a + k - 1],
                    device_id=(px, py, pc), device_id_type=MESH))
        return out


_ANY = pl.BlockSpec(memory_space=pl.ANY)


def _run_exchange(ex, name):
    n = len(ex.arrays)

    def body(*refs):
        cps = ex.copies(refs[:n], refs[n:2 * n], *refs[2 * n:])
        for cp in cps:
            cp.start()
        for cp in cps:
            cp.wait()

    return pl.pallas_call(body, out_shape=ex.out_shapes, in_specs=[_ANY] * n, out_specs=[_ANY] * n,
                          scratch_shapes=ex.scratch, name=name)(*ex.arrays)


def _call_beside(body, ex, nsteps, args, *, grid, in_specs, out_specs, out_shape, scratch_shapes, name):
    if ex is None:
        outs = pl.pallas_call(body, grid=grid, in_specs=in_specs, out_specs=out_specs, out_shape=out_shape,
                              scratch_shapes=scratch_shapes, compiler_params=_cparams(("arbitrary",)), name=name)(*args)
        return outs, None
    n_in, n_out, n_scr, n = len(in_specs), len(out_specs), len(scratch_shapes), len(ex.arrays)

    def wrapped(*refs):
        ins, refs = refs[:n_in], refs[n_in:]
        ex_ins, refs = refs[:n], refs[n:]
        outs, refs = refs[:n_out], refs[n_out:]
        ex_outs, refs = refs[:n], refs[n:]
        scr, sems = refs[:n_scr], refs[n_scr:]
        step = pl.program_id(0)

        @pl.when(step == 0)
        def _():
            for cp in ex.copies(ex_ins, ex_outs, *sems):
                cp.start()

        body(*ins, *outs, *scr)

        @pl.when(step == nsteps - 1)
        def _():
            for cp in ex.copies(ex_ins, ex_outs, *sems):
                cp.wait()

    res = pl.pallas_call(
        wrapped, grid=grid, in_specs=list(in_specs) + [_ANY] * n, out_specs=list(out_specs) + [_ANY] * n,
        out_shape=list(out_shape) + ex.out_shapes, scratch_shapes=list(scratch_shapes) + ex.scratch,
        compiler_params=_cparams(("arbitrary",)), name=name)(*args, *ex.arrays)
    return res[:n_out], res[n_out:]


def _adamw_math(w, g, m, v):
    m = ADAM_B1 * m + (1.0 - ADAM_B1) * g
    v = ADAM_B2 * v + (1.0 - ADAM_B2) * (g * g)
    m_hat = m / (1.0 - ADAM_B1 ** ADAM_STEP)
    v_hat = v / (1.0 - ADAM_B2 ** ADAM_STEP)
    delta = -ADAM_LR * (m_hat / (jnp.sqrt(v_hat) + ADAM_EPS) + ADAM_WD * w)
    return delta, m, v


def _sum_adamw(parts, w, m, v, tr, name):
    ns = len(parts)
    p, r, c = parts[0].shape
    tr = min(tr, r)
    assert r % tr == 0
    nt = r // tr

    def body(*refs):
        p_refs, (w_ref, m_ref, v_ref, g_ref, d_ref, nm_ref, nv_ref) = refs[:ns], refs[ns:]
        for s in range(ns):
            @pl.when(pl.program_id(0) == s)
            def _(p_ref=p_refs[s]):
                g = p_ref[0].astype(F32)
                for j in range(1, p):
                    g = g + p_ref[j].astype(F32)
                delta, nm, nv = _adamw_math(w_ref[...], g, m_ref[...], v_ref[...])
                g_ref[...] = g
                d_ref[...] = delta
                nm_ref[...] = nm
                nv_ref[...] = nv

    part_spec = lambda s: pl.BlockSpec((p, tr, c), lambda sec, i, s=s: (0, jnp.where(sec == s, i, 0), 0))
    row = pl.BlockSpec((tr, c), lambda sec, i: (sec * nt + i, 0))
    return pl.pallas_call(
        body, grid=(ns, nt), in_specs=[part_spec(s) for s in range(ns)] + [row, row, row],
        out_specs=[row] * 4, out_shape=[jax.ShapeDtypeStruct((ns * r, c), F32)] * 4,
        compiler_params=_cparams(("arbitrary", "arbitrary")), name=name)(*parts, w, m, v)


def _sum_parts(parts, name):
    p, r, c = parts.shape

    def body(p_ref, o_ref):
        g = p_ref[0]
        for j in range(1, p):
            g = g + p_ref[j]
        o_ref[...] = g

    return pl.pallas_call(body, out_shape=jax.ShapeDtypeStruct((r, c), F32), name=name)(parts)


def _rows_of(shape):
    n = 1
    for d in shape:
        n *= d
    return n, -(-n // 128)


def _pack(arrs):
    blocks = []
    for a in arrs:
        n, nr = _rows_of(a.shape)
        blocks.append(jnp.pad(a.reshape(-1).astype(F32), (0, nr * 128 - n)).reshape(nr, 128))
    rows = sum(b.shape[0] for b in blocks)
    if rows % 8:
        blocks.append(jnp.zeros((8 - rows % 8, 128), F32))
    return jnp.concatenate(blocks, axis=0)


def _unpack(buf, shapes):
    out, r0 = [], 0
    for s in shapes:
        n, nr = _rows_of(s)
        out.append(buf[r0:r0 + nr].reshape(-1)[:n].reshape(s))
        r0 += nr
    return out


def _block_diag(w):
    rows = [jnp.pad(w[i], ((0, 0), (i * C_BLOCK_DIM, C_WIDTH - (i + 1) * C_BLOCK_DIM))) for i in range(C_BLOCKS)]
    return jnp.concatenate(rows, axis=0)


def _diag_blocks(m):
    m4 = m.reshape(C_BLOCKS, C_BLOCK_DIM, C_BLOCKS, C_BLOCK_DIM)
    return jnp.stack([m4[i, :, i, :] for i in range(C_BLOCKS)])


def _gate_row(v):
    return jnp.pad(v.astype(F32), (B_HEADS, 128 - 2 * B_HEADS)).reshape(1, 128)


def _permute_w_in(w):
    pad = jnp.zeros(w.shape[:-1] + (D_IN_PAD - D_IN,), w.dtype)
    return jnp.concatenate([w[..., :3072], w[..., 3080:3592], w[..., 3072:3080], pad], axis=-1)


def _unpermute_w_in(w):
    return jnp.concatenate([w[..., :3072], w[..., COL_G:COL_G + 8], w[..., 3072:COL_G]], axis=-1)


_WEIGHTS = ['norm1_g', 'w_in', 'hgrn_lb_logits', 'hgrn_norm_g', 'gdn_conv_w', 'gdn_a_log', 'gdn_dt_bias', 'gdn_norm_g',
            'lru_conv_w', 'lru_conv_b', 'lru_w_a', 'lru_b_a', 'lru_w_x', 'lru_b_x', 'lru_lambda', 'w_out', 'norm2_g',
            'w_up', 'w_down', 'final_norm_g']
_BIG = ('w_in', 'w_out', 'w_up', 'w_down')
_SHARDED_SMALL = ('gdn_conv_w', 'lru_conv_w')


def _step(x, target, w, m, v):
    t = x.shape[0]
    mx, my, mc = _mesh_pos()
    me = 4 * mx + 2 * my + mc

    bf = lambda a: a.astype(BF16)

    def full_w_in(g):
        return _permute_w_in(jnp.moveaxis(g, 0, 1).reshape(D_MODEL, D_IN))

    conv_shapes = [w['gdn_conv_w'].shape, w['lru_conv_w'].shape]
    g_in, g_conv = _run_exchange(
        _Exchange([bf(w['w_in'][0]), _pack([w['gdn_conv_w'], w['lru_conv_w']])], ['slot', 'slot']), "gather_first")
    w_in = [full_w_in(g_in)]
    w_out, w_up, w_down = [], [], []
    gdn_cw, lru_cw = [], []
    for j in range(N_DEV):
        a, b = _unpack(g_conv[j], conv_shapes)
        gdn_cw.append(a)
        lru_cw.append(b)
    gdn_cw = jnp.concatenate(gdn_cw, axis=-1)
    lru_cw = jnp.concatenate(lru_cw, axis=-1)

    lbnd = _lb_fwd(w['hgrn_lb_logits'])
    row = lambda a: a.reshape(1, -1)

    def c_args(l):
        return (lru_cw[l], row(w['lru_conv_b'][l]), _block_diag(w['lru_w_a'][l]), row(w['lru_b_a'][l]),
                _block_diag(w['lru_w_x'][l]), row(w['lru_b_x'][l]), row(w['lru_lambda'][l]))

    saved = []
    xl = x
    h = _rms_fwd(x, w['norm1_g'][0], name="rms_fwd")
    for l in range(DEPTH):
        proj = _mm_rows(h, w_in[l], "nn", 512, "mm_proj")
        mix_a, st_a = _a_fwd(proj, lbnd[l], w['hgrn_norm_g'][l])
        alr, dtr = _gate_row(w['gdn_a_log'][l]), _gate_row(w['gdn_dt_bias'][l])
        qkv, gates, y3 = _b_pre_fwd(proj, gdn_cw[l], alr, dtr)
        nxt = [bf(w['w_in'][l + 1])] if l + 1 < DEPTH else []
        gather = _Exchange([bf(w['w_out'][l]), bf(w['w_up'][l]), bf(w['w_down'][l])] + nxt, ['rows', 'cols', 'rows'] + ['slot'] * len(nxt))
        (mix_b, st_b, *b_saved), got = _b_fwd(qkv, gates, proj, w['gdn_norm_g'][l], beside=gather)
        w_out.append(got[0])
        w_up.append(got[1])
        w_down.append(got[2])
        if nxt:
            w_in.append(full_w_in(got[3]))
        mix_c, hs = _c_fwd(proj, *c_args(l))
        mixed = [mix_a, mix_b, mix_c]
        x_mid, h2 = _mm_rows(mixed, w_out[l], "nn", 1024, "mm_out", residual=xl, epilogue="rms_fwd", norm=w['norm2_g'][l])
        act, up = _mm_rows(h2, w_up[l], "nn", 512, "mm_up", epilogue="relu2")
        saved.append(dict(x=xl, h=h, proj=proj, st_a=st_a, qkv=qkv, gates=gates, y3=y3, st_b=st_b, b_saved=b_saved, hs=hs, mixed=mixed,
                          x_mid=x_mid, h2=h2, up=up, act=act, alr=alr, dtr=dtr))
        if l + 1 < DEPTH:
            xl, h = _mm_rows(act, w_down[l], "nn", 512, "mm_down", residual=x_mid, epilogue="rms_fwd", norm=w['norm1_g'][l + 1])
        else:
            xl = _mm_rows(act, w_down[l], "nn", 512, "mm_down_last", residual=x_mid)
    loss, dx, dgf = _loss_head(xl, w['final_norm_g'], target)

    gs = {n: [None] * DEPTH for n in _WEIGHTS}
    recv = {n: [None] * DEPTH for n in _BIG}
    dw_in_above = None
    for l in reversed(range(DEPTH)):
        s = saved[l]
        dup = _mm_rows(dx, w_down[l], "nt", 512, "mm_dact", epilogue="drelu2", up=s['up'])
        dw_down = _mm_tn(s['act'], dx, 1024, "mm_dw_down").reshape(N_DEV, D_FF // N_DEV, D_MODEL)
        dx_mid, dg2 = _mm_rows(dup, w_up[l], "nt", 512, "mm_dh2", epilogue="rms_bwd", norm=(s['x_mid'], w['norm2_g'][l], dx))
        dw_up = _mm_tn(s['h2'], dup, 1024, "mm_dw_up", slab=D_FF // N_DEV)
        gs['norm2_g'][l] = dg2[0]
        dmixed = _mm_rows(dx_mid, w_out[l], "nt", 1024, "mm_dmixed")
        dw_out = _mm_tn(s['mixed'], dx_mid, 1024, "mm_dw_out").reshape(N_DEV, D_MODEL // N_DEV, D_MODEL)
        proj = s['proj']
        above = [dw_in_above] if dw_in_above is not None else []
        (dpa, dlb, dnga), got = _a_bwd(proj, lbnd[l], w['hgrn_norm_g'][l], s['st_a'], dmixed,
                                       beside=_Exchange([dw_out] + above, ['a2a'] * (1 + len(above))))
        recv['w_out'][l] = got[0]
        if above:
            recv['w_in'][l + 1] = got[1]
        gs['hgrn_lb_logits'][l] = dlb[0]
        gs['hgrn_norm_g'][l] = dnga[0, :A_DIM] + dnga[0, A_DIM:]
        (dqkv, dgates, dz, dngb), got = _b_bwd(s['qkv'], s['gates'], proj, w['gdn_norm_g'][l], s['st_b'], s['b_saved'], dmixed,
                                               beside=_Exchange([dw_up, dw_down], ['a2a', 'a2a']))
        recv['w_up'][l], recv['w_down'][l] = got
        dxb, dgi, dcwb, dal, ddt = _b_pre_bwd(proj, s['y3'], gdn_cw[l], s['alr'], s['dtr'], dqkv, dgates)
        gs['gdn_norm_g'][l] = dngb[0]
        gs['gdn_conv_w'][l] = dcwb
        gs['gdn_a_log'][l] = dal[0, B_HEADS:2 * B_HEADS]
        gs['gdn_dt_bias'][l] = ddt[0, B_HEADS:2 * B_HEADS]
        dxc_in, dyg, dcwc, dcb, dwa, dba, dwx, dbx, dlam = _c_bwd(proj, *c_args(l), s['hs'], dmixed)
        gs['lru_conv_w'][l] = dcwc
        gs['lru_conv_b'][l] = dcb[0]
        gs['lru_w_a'][l] = _diag_blocks(dwa)
        gs['lru_b_a'][l] = dba[0]
        gs['lru_w_x'][l] = _diag_blocks(dwx)
        gs['lru_b_x'][l] = dbx[0]
        gs['lru_lambda'][l] = dlam[0]
        dproj = [dpa, dxb, dz, dxc_in, dyg, dgi]
        dw_in = _unpermute_w_in(_mm_tn(s['h'], dproj, 1024, "mm_dw_in"))
        dw_in_above = jnp.moveaxis(dw_in.reshape(D_MODEL, N_DEV, D_IN // N_DEV), 1, 0)
        norm = (s['x'], w['norm1_g'][l], dx_mid)
        if l > 0:
            dx, dg1 = _mm_rows(dproj, w_in[l], "nt", 512, "mm_dh", epilogue="rms_bwd", norm=norm)
        else:
            (dx, dg1), got = _mm_rows(dproj, w_in[l], "nt", 512, "mm_dh_last", epilogue="rms_bwd", norm=norm,
                                      beside=_Exchange([dw_in_above], ['a2a']))
            recv['w_in'][0] = got[0]
        gs['norm1_g'][l] = dg1[0]
    grad_x = dx
    part = {n: jnp.stack(gs[n]) for n in _WEIGHTS if n != 'final_norm_g' and n not in _BIG}
    part['final_norm_g'] = dgf[0]
    part['hgrn_lb_logits'] = _lb_bwd(w['hgrn_lb_logits'], part['hgrn_lb_logits'])

    small = [n for n in _WEIGHTS if n not in _BIG]
    packed = _pack([part[n] for n in small] + [loss])
    all_small, = _run_exchange(_Exchange([packed], ['slot']), "gather_small")

    grads, deltas, new_m, new_v = {}, {}, {}, {}
    for n in _BIG:
        shp = w[n].shape
        r2 = lambda a: a.reshape(-1, shp[-1])
        g, d, nm, nv = _sum_adamw(recv[n], r2(w[n]), r2(m[n]), r2(v[n]), 256, "adamw_" + n)
        grads[n], deltas[n], new_m[n], new_v[n] = (a.reshape(shp) for a in (g, d, nm, nv))

    total = _sum_parts(all_small, "sum_small")
    summed = _unpack(total, [part[n].shape for n in small] + [(1, 1)])
    loss_total = summed[-1].reshape(())
    gsmall = dict(zip(small, summed[:-1]))
    for n in _SHARDED_SMALL:
        width = w[n].shape[-1]
        gsmall[n] = lax.dynamic_slice_in_dim(gsmall[n], me * width, width, axis=2)
    pk = lambda d: _pack([d[n] for n in small])
    _, d, nm, nv = _sum_adamw([pk(gsmall)[None]], pk(w), pk(m), pk(v), 4096, "adamw_small")
    shapes = [w[n].shape for n in small]
    for n, dd, mm, vv in zip(small, _unpack(d, shapes), _unpack(nm, shapes), _unpack(nv, shapes)):
        grads[n], deltas[n], new_m[n], new_v[n] = gsmall[n], dd, mm, vv
    return loss_total, grad_x, grads, deltas, new_m, new_v


def kernel(x, norm1_g, w_in, hgrn_lb_logits, hgrn_norm_g, gdn_conv_w, gdn_a_log, gdn_dt_bias, gdn_norm_g, lru_conv_w, lru_conv_b, lru_w_a, lru_b_a, lru_w_x, lru_b_x, lru_lambda, w_out, norm2_g, w_up, w_down, final_norm_g, loss_target, m_norm1_g, m_w_in, m_hgrn_lb_logits, m_hgrn_norm_g, m_gdn_conv_w, m_gdn_a_log, m_gdn_dt_bias, m_gdn_norm_g, m_lru_conv_w, m_lru_conv_b, m_lru_w_a, m_lru_b_a, m_lru_w_x, m_lru_b_x, m_lru_lambda, m_w_out, m_norm2_g, m_w_up, m_w_down, m_final_norm_g, v_norm1_g, v_w_in, v_hgrn_lb_logits, v_hgrn_norm_g, v_gdn_conv_w, v_gdn_a_log, v_gdn_dt_bias, v_gdn_norm_g, v_lru_conv_w, v_lru_conv_b, v_lru_w_a, v_lru_b_a, v_lru_w_x, v_lru_b_x, v_lru_lambda, v_w_out, v_norm2_g, v_w_up, v_w_down, v_final_norm_g):
    w = dict(zip(_WEIGHTS, (norm1_g, w_in, hgrn_lb_logits, hgrn_norm_g, gdn_conv_w, gdn_a_log, gdn_dt_bias, gdn_norm_g, lru_conv_w, lru_conv_b, lru_w_a, lru_b_a, lru_w_x, lru_b_x, lru_lambda, w_out, norm2_g, w_up, w_down, final_norm_g)))
    m = dict(zip(_WEIGHTS, (m_norm1_g, m_w_in, m_hgrn_lb_logits, m_hgrn_norm_g, m_gdn_conv_w, m_gdn_a_log, m_gdn_dt_bias, m_gdn_norm_g, m_lru_conv_w, m_lru_conv_b, m_lru_w_a, m_lru_b_a, m_lru_w_x, m_lru_b_x, m_lru_lambda, m_w_out, m_norm2_g, m_w_up, m_w_down, m_final_norm_g)))
    v = dict(zip(_WEIGHTS, (v_norm1_g, v_w_in, v_hgrn_lb_logits, v_hgrn_norm_g, v_gdn_conv_w, v_gdn_a_log, v_gdn_dt_bias, v_gdn_norm_g, v_lru_conv_w, v_lru_conv_b, v_lru_w_a, v_lru_b_a, v_lru_w_x, v_lru_b_x, v_lru_lambda, v_w_out, v_norm2_g, v_w_up, v_w_down, v_final_norm_g)))
    loss, grad_x, grads, deltas, new_m, new_v = _step(x.reshape(x.shape[1:]), loss_target.reshape(x.shape[1:]), w, m, v)
    return (loss, grad_x[None], *[grads[n] for n in _WEIGHTS], *[deltas[n] for n in _WEIGHTS],
            *[new_m[n] for n in _WEIGHTS], *[new_v[n] for n in _WEIGHTS])
```

```python
import jax
import jax.numpy as jnp
from jax import lax
from jax.experimental import pallas as pl
from jax.experimental.pallas import tpu as pltpu

F32 = jnp.float32
BF16 = jnp.bfloat16
MESH = pl.DeviceIdType.MESH

N_DEV = 8
D_MODEL = 1024
DEPTH = 4
A_HEADS, A_DIM, A_WIDTH = 4, 64, 256
B_HEADS, B_DIM, B_WIDTH = 4, 128, 512
C_WIDTH, C_BLOCKS, C_BLOCK_DIM = 256, 4, 64
D_IN = 3592
D_IN_PAD = 3840
COL_A, COL_B, COL_C, COL_G = 0, 1024, 3072, 3584
D_FF = 4096
CONV_K = 4
CHUNK = 64
SUB = 16
RG_C = 8.0
EPS = 1e-6
TINY = 1e-30
EXP_CLAMP = 80.0
GDN_SCALE = B_DIM ** -0.5
ADAM_LR, ADAM_B1, ADAM_B2, ADAM_EPS, ADAM_WD, ADAM_STEP = 0.001, 0.9, 0.999, 1e-08, 0.01, 10
VMEM_LIMIT = 56 * 1024 * 1024


def _cparams(sem=None):
    return pltpu.CompilerParams(dimension_semantics=sem, vmem_limit_bytes=VMEM_LIMIT)


_DIMS = {"nn": (((1,), (0,)), ((), ())), "nt": (((1,), (1,)), ((), ())), "tn": (((0,), (0,)), ((), ()))}


def _split_bf16(x):
    hi = x.astype(BF16)
    return hi, (x - hi.astype(F32)).astype(BF16)


def _dot(a, b, mode="nn", hi=False):
    if not hi:
        return lax.dot_general(a.astype(BF16), b.astype(BF16), _DIMS[mode], preferred_element_type=F32)
    ah, al = _split_bf16(a.astype(F32))
    bh, bl = _split_bf16(b.astype(F32))
    ka = 0 if mode == "tn" else 1
    kb = 1 if mode == "nt" else 0
    return lax.dot_general(jnp.concatenate([ah, ah, al], axis=ka), jnp.concatenate([bh, bl, bh], axis=kb),
                           _DIMS[mode], preferred_element_type=F32)


def _dot_exact_lhs(lhs, x, mode="nn"):
    l_bf16 = lhs.astype(BF16)
    x1 = x.astype(BF16)
    r1 = x - x1.astype(F32)
    x2 = r1.astype(BF16)
    x3 = (r1 - x2.astype(F32)).astype(BF16)
    ka = 0 if mode == "tn" else 1
    return lax.dot_general(jnp.concatenate([l_bf16] * 3, axis=ka), jnp.concatenate([x1, x2, x3], axis=0),
                           _DIMS[mode], preferred_element_type=F32)


def _iota2(n, m):
    return lax.broadcasted_iota(jnp.int32, (n, m), 0), lax.broadcasted_iota(jnp.int32, (n, m), 1)


def _tril(n, strict=False):
    r, c = _iota2(n, n)
    return (r > c) if strict else (r >= c)


def _sigmoid(x):
    return 1.0 / (1.0 + jnp.exp(-x))


def _softplus(x):
    return jnp.maximum(x, 0.0) + jnp.log(1.0 + jnp.exp(-jnp.abs(x)))


def _neg_expm1(z):
    series = -z * (1.0 + z * (0.5 + z * (1.0 / 6.0)))
    return jnp.where(z > -1e-2, series, 1.0 - jnp.exp(z))


def _gelu_tanh(x):
    c = 0.7978845608028654
    u = c * (x + 0.044715 * x * x * x)
    t = jnp.tanh(u)
    g = 0.5 * x * (1.0 + t)
    dg = 0.5 * (1.0 + t) + 0.5 * x * (1.0 - t * t) * c * (1.0 + 3.0 * 0.044715 * x * x)
    return g, dg


def _shift_rows(cur, halo, s, down=True):
    n = cur.shape[0]
    ridx = lax.broadcasted_iota(jnp.int32, (8, cur.shape[1]), 0)
    if down:
        main = pltpu.roll(cur, s, 0)
        fix = jnp.where(ridx < s, pltpu.roll(halo, s, 0), main[0:8])
        return jnp.concatenate([fix, main[8:]], axis=0)
    main = pltpu.roll(cur, n - s, 0)
    fix = jnp.where(ridx >= 8 - s, pltpu.roll(halo, 8 - s, 0), main[n - 8:n])
    return jnp.concatenate([main[:n - 8], fix], axis=0)


def _later_rows(dy, nxt8):
    return [_shift_rows(dy, nxt8, 3 - j, down=False) for j in range(3)] + [dy]


def _conv_bwd_rows(dy, nxt8, w):
    return sum(d * w[j:j + 1] for j, d in enumerate(_later_rows(dy, nxt8)))


def _tile_scan(a, b, reverse=False):
    n = a.shape[0]
    r = lax.broadcasted_iota(jnp.int32, a.shape, 0) % 8
    for s in (1, 2, 4):
        keep = (r < 8 - s) if reverse else (r >= s)
        shift = n - s if reverse else s
        a_sh = jnp.where(keep, pltpu.roll(a, shift, 0), 1.0)
        b_sh = jnp.where(keep, pltpu.roll(b, shift, 0), 0.0)
        b = b + a * b_sh
        a = a * a_sh
    return a, b


def _conv_fwd(cur, prev8, w):
    y = cur * w[3:4]
    for j in range(3):
        y = y + _shift_rows(cur, prev8, 3 - j, down=True) * w[j:j + 1]
    return y


def _mm_rows(a, w, mode, tm, name, residual=None, epilogue=None, up=None, norm=None, beside=None):
    parts = list(a) if isinstance(a, (list, tuple)) else [a]
    widths = [p.shape[1] for p in parts]
    t = parts[0].shape[0]
    n = w.shape[1] if mode == "nn" else w.shape[0]
    tm = min(tm, t)
    assert t % tm == 0 and all(wd % 128 == 0 for wd in widths)

    def body(*refs):
        a_refs, w_ref, rest = refs[:len(parts)], refs[len(parts)], refs[len(parts) + 1:]
        if epilogue == "rms_bwd":
            @pl.when(pl.program_id(0) == 0)
            def _():
                rest[4][...] = jnp.zeros_like(rest[4])

        for rs in ((slice(0, tm // 2), slice(tm // 2, tm)) if tm >= 512 else (slice(0, tm),)):
            y, off = None, 0
            for a_ref, width in zip(a_refs, widths):
                wk = w_ref[off:off + width, :] if mode == "nn" else w_ref[:, off:off + width]
                d = _dot(a_ref[rs, :], wk, mode)
                y = d if y is None else y + d
                off += width
            if residual is not None:
                y = y + rest[0][rs, :]
            if epilogue == "relu2":
                r = jnp.maximum(y, 0.0)
                refs[-2][rs, :] = (r * r).astype(BF16)
                refs[-1][rs, :] = y.astype(BF16)
            elif epilogue == "drelu2":
                refs[-1][rs, :] = (y * 2.0 * jnp.maximum(rest[0][rs, :].astype(F32), 0.0)).astype(BF16)
            elif epilogue == "rms_fwd":
                rinv = lax.rsqrt(jnp.mean(y * y, axis=-1, keepdims=True) + EPS)
                refs[-2][rs, :] = y
                refs[-1][rs, :] = (y * rinv * refs[-3][...]).astype(BF16)
            elif epilogue == "rms_bwd":
                x_ref, g_ref, dres_ref, dx_ref, dg_ref = rest
                xv = x_ref[rs, :]
                rinv = lax.rsqrt(jnp.mean(xv * xv, axis=-1, keepdims=True) + EPS)
                xhat = xv * rinv
                dxh = y * g_ref[...]
                dx_ref[rs, :] = dres_ref[rs, :] + rinv * (dxh - xhat * jnp.mean(dxh * xhat, axis=-1, keepdims=True))
                dg_ref[...] += jnp.sum(y * xhat, axis=0, keepdims=True)
            else:
                refs[-1][rs, :] = y

    rows = lambda width: pl.BlockSpec((tm, width), lambda i: (i, 0))
    vec = pl.BlockSpec((1, n), lambda i: (0, 0))
    ins, specs = parts + [w], [rows(wd) for wd in widths] + [pl.BlockSpec(w.shape, lambda i: (0, 0))]
    if residual is not None:
        ins.append(residual)
        specs.append(rows(n))
    if epilogue == "drelu2":
        ins.append(up)
        specs.append(rows(n))
    if epilogue == "rms_fwd":
        ins.append(norm.reshape(1, n))
        specs.append(vec)
        out_specs, out_shape = [rows(n), rows(n)], [jax.ShapeDtypeStruct((t, n), F32), jax.ShapeDtypeStruct((t, n), BF16)]
    elif epilogue == "rms_bwd":
        ins += [norm[0], norm[1].reshape(1, n), norm[2]]
        specs += [rows(n), vec, rows(n)]
        out_specs, out_shape = [rows(n), vec], [jax.ShapeDtypeStruct((t, n), F32), jax.ShapeDtypeStruct((1, n), F32)]
    elif epilogue == "relu2":
        out_specs, out_shape = [rows(n), rows(n)], [jax.ShapeDtypeStruct((t, n), BF16)] * 2
    else:
        out_specs, out_shape = rows(n), jax.ShapeDtypeStruct((t, n), BF16 if epilogue == "drelu2" else F32)
    if beside is not None:
        assert epilogue == "rms_bwd"
        return _call_beside(body, beside, t // tm, ins, grid=(t // tm,), in_specs=specs, out_specs=out_specs,
                            out_shape=out_shape, scratch_shapes=[], name=name)
    return pl.pallas_call(body, grid=(t // tm,), in_specs=specs, out_specs=out_specs, out_shape=out_shape,
                          compiler_params=_cparams(("arbitrary" if epilogue == "rms_bwd" else "parallel",)), name=name)(*ins)


MM_TN_TILE = 1024


def _mm_tn(a, b, tk, name, slab=None):
    a_parts = list(a) if isinstance(a, (list, tuple)) else [a]
    b_parts = list(b) if isinstance(b, (list, tuple)) else [b]
    wa, wb = [p.shape[1] for p in a_parts], [p.shape[1] for p in b_parts]
    t, m, n = a_parts[0].shape[0], sum(wa), sum(wb)
    tk = min(tk, t)
    assert t % tk == 0 and all(x % 128 == 0 for x in wa + wb)
    nk = t // tk

    def body(*refs):
        a_refs, b_refs = refs[:len(wa)], refs[len(wa):len(wa) + len(wb)]
        o_ref, acc = refs[-2], refs[-1]
        kk = pl.program_id(0)

        @pl.when(kk == 0)
        def _():
            acc[...] = jnp.zeros_like(acc)

        ro = 0
        for a_ref, width_a in zip(a_refs, wa):
            for r0 in range(0, width_a, MM_TN_TILE):
                rw = min(MM_TN_TILE, width_a - r0)
                av = a_ref[:, r0:r0 + rw]
                co = 0
                for b_ref, width_b in zip(b_refs, wb):
                    for c0 in range(0, width_b, MM_TN_TILE):
                        cw = min(MM_TN_TILE, width_b - c0)
                        acc[ro + r0:ro + r0 + rw, co + c0:co + c0 + cw] += _dot(av, b_ref[:, c0:c0 + cw], "tn")
                    co += width_b
            ro += width_a

        @pl.when(kk == nk - 1)
        def _():
            if slab is None:
                o_ref[...] = acc[...].astype(BF16)
            else:
                for s in range(n // slab):
                    o_ref[s] = acc[:, s * slab:(s + 1) * slab].astype(BF16)

    if slab is None:
        out_spec, out_shape = pl.BlockSpec((m, n), lambda kk: (0, 0)), jax.ShapeDtypeStruct((m, n), BF16)
    else:
        out_spec, out_shape = pl.BlockSpec((n // slab, m, slab), lambda kk: (0, 0, 0)), jax.ShapeDtypeStruct((n // slab, m, slab), BF16)
    return pl.pallas_call(
        body, grid=(nk,),
        in_specs=[pl.BlockSpec((tk, x), lambda kk: (kk, 0)) for x in wa + wb],
        out_specs=out_spec, out_shape=out_shape, scratch_shapes=[pltpu.VMEM((m, n), F32)],
        compiler_params=_cparams(("arbitrary",)), name=name)(*a_parts, *b_parts)


def _rms_fwd(x, g, tb=512, name="rms_fwd"):
    t, d = x.shape

    def body(x_ref, g_ref, h_ref):
        xv = x_ref[...]
        rinv = lax.rsqrt(jnp.mean(xv * xv, axis=-1, keepdims=True) + EPS)
        h_ref[...] = (xv * rinv * g_ref[...]).astype(BF16)

    return pl.pallas_call(
        body, grid=(t // tb,), in_specs=[pl.BlockSpec((tb, d), lambda i: (i, 0)), pl.BlockSpec((1, d), lambda i: (0, 0))],
        out_specs=pl.BlockSpec((tb, d), lambda i: (i, 0)), out_shape=jax.ShapeDtypeStruct((t, d), BF16),
        compiler_params=_cparams(("parallel",)), name=name)(x, g.reshape(1, d))


def _loss_head(x, g, target, tb=512):
    t, d = x.shape

    def body(x_ref, g_ref, t_ref, loss_ref, dx_ref, dg_ref):
        @pl.when(pl.program_id(0) == 0)
        def _():
            dg_ref[...] = jnp.zeros_like(dg_ref)
            loss_ref[...] = jnp.zeros_like(loss_ref)

        xv = x_ref[...]
        rinv = lax.rsqrt(jnp.mean(xv * xv, axis=-1, keepdims=True) + EPS)
        xhat = xv * rinv
        err = xhat * g_ref[...] - t_ref[...]
        loss_ref[...] += 0.5 * jnp.sum(jnp.mean(err * err, axis=-1, keepdims=True), axis=0, keepdims=True)
        dy = err * (1.0 / d)
        dxh = dy * g_ref[...]
        dx_ref[...] = rinv * (dxh - xhat * jnp.mean(dxh * xhat, axis=-1, keepdims=True))
        dg_ref[...] += jnp.sum(dy * xhat, axis=0, keepdims=True)

    row = pl.BlockSpec((tb, d), lambda i: (i, 0))
    vec = pl.BlockSpec((1, d), lambda i: (0, 0))
    one = pl.BlockSpec((1, 1), lambda i: (0, 0))
    return pl.pallas_call(
        body, grid=(t // tb,), in_specs=[row, vec, row], out_specs=[one, row, vec],
        out_shape=[jax.ShapeDtypeStruct((1, 1), F32), jax.ShapeDtypeStruct((t, d), F32), jax.ShapeDtypeStruct((1, d), F32)],
        compiler_params=_cparams(("arbitrary",)), name="loss_head")(x, g.reshape(1, d), target)


def _lb_fwd(logits):
    def body(l_ref, o_ref):
        lg = l_ref[...]
        e = jnp.exp(lg - jnp.max(lg, axis=0, keepdims=True))
        p = e / jnp.sum(e, axis=0, keepdims=True)
        c = jnp.zeros_like(p[0:1])
        rows = [c]
        for l in range(1, DEPTH):
            c = c + p[l:l + 1]
            rows.append(c)
        o_ref[...] = jnp.minimum(jnp.maximum(jnp.concatenate(rows, axis=0), 0.0), 1.0 - EPS)

    return pl.pallas_call(body, out_shape=jax.ShapeDtypeStruct(logits.shape, F32), name="lb_fwd")(logits)


def _lb_bwd(logits, dlb):
    def body(l_ref, d_ref, o_ref):
        lg = l_ref[...]
        e = jnp.exp(lg - jnp.max(lg, axis=0, keepdims=True))
        p = e / jnp.sum(e, axis=0, keepdims=True)
        hi = 1.0 - EPS
        c = jnp.zeros_like(p[0:1])
        dc = []
        for l in range(1, DEPTH):
            c = c + p[l:l + 1]
            gl = jnp.where(c < 0.0, 0.0, jnp.where(c == 0.0, 0.5, 1.0)) * jnp.where(c > hi, 0.0, jnp.where(c == hi, 0.5, 1.0))
            dc.append(d_ref[l:l + 1, :] * gl)
        dp = [jnp.zeros_like(c)]
        for j in range(1, DEPTH):
            s = dc[j - 1]
            for l in range(j + 1, DEPTH):
                s = s + dc[l - 1]
            dp.append(s)
        dpm = jnp.concatenate(dp, axis=0)
        o_ref[...] = p * (dpm - jnp.sum(p * dpm, axis=0, keepdims=True))

    return pl.pallas_call(body, out_shape=jax.ShapeDtypeStruct(logits.shape, F32), name="lb_bwd")(logits, dlb)


def _a_gates(qi, fi, lbh):
    sq = _sigmoid(qi)
    q = qi * sq
    e = jnp.exp(-jnp.abs(fi))
    rec = 1.0 / (1.0 + e)
    pos = fi >= 0.0
    sg = jnp.where(pos, rec, e * rec)
    sgn = jnp.where(pos, e * rec, rec)
    f = lbh + (1.0 - lbh) * sg
    logf = jnp.log(jnp.maximum(f, TINY))
    k = (1.0 - lbh) * sgn
    return q, sq, sg, sgn, f, logf, k


def _headnorm_fwd(o, g, gate_in):
    rinv = lax.rsqrt(jnp.mean(o * o, axis=-1, keepdims=True) + EPS)
    sg = _sigmoid(gate_in)
    return o * rinv * g * (gate_in * sg)


def _headnorm_bwd(dout, o, g, gate_in):
    rinv = lax.rsqrt(jnp.mean(o * o, axis=-1, keepdims=True) + EPS)
    xhat = o * rinv
    sg = _sigmoid(gate_in)
    silu = gate_in * sg
    dy = dout * silu
    dgate = dout * xhat * g * (sg * (1.0 + gate_in * (1.0 - sg)))
    dxh = dy * g
    do = rinv * (dxh - xhat * jnp.mean(dxh * xhat, axis=-1, keepdims=True))
    return do, dgate, jnp.sum(dy * xhat, axis=0, keepdims=True)


A_PAIRS, A_PAIR_W = A_HEADS // 2, 2 * A_DIM


def _lo_half(shape):
    return lax.broadcasted_iota(jnp.int32, shape, len(shape) - 1) < A_DIM


def _pair_blockdiag(x):
    lo = _lo_half(x.shape)
    return jnp.concatenate([jnp.where(lo, x, 0.0), jnp.where(lo, 0.0, x)], axis=0)


def _pair_fold(m):
    n = m.shape[0] // 2
    return jnp.where(_lo_half((n, A_PAIR_W)), m[:n], m[n:])


def _pair_norm_stats(o):
    lo = _lo_half(o.shape)
    sq = o * o
    s0 = jnp.sum(jnp.where(lo, sq, 0.0), axis=-1, keepdims=True)
    s1 = jnp.sum(sq, axis=-1, keepdims=True) - s0
    return jnp.where(lo, lax.rsqrt(s0 * (1.0 / A_DIM) + EPS), lax.rsqrt(s1 * (1.0 / A_DIM) + EPS))


def _pair_mean(x):
    lo = _lo_half(x.shape)
    s0 = jnp.sum(jnp.where(lo, x, 0.0), axis=-1, keepdims=True)
    s1 = jnp.sum(x, axis=-1, keepdims=True) - s0
    return jnp.where(lo, s0, s1) * (1.0 / A_DIM)


def _a_pair_chunk(qi, fi, v, lb2, s_bd, ltri, causal2):
    q, sq, sg, sgn, f, logf, k = _a_gates(qi, fi, lb2)
    cum = _dot_exact_lhs(ltri, logf)
    cl = cum[CHUNK - 1:CHUNK]
    ecum, ekd, cd = jnp.exp(cum), jnp.exp(cl - cum), jnp.exp(cl)
    qd, kd = q * ecum, k * ekd
    subs, rows = [], []
    for i in range(CHUNK // SUB):
        lo = i * SUB
        r = cum[lo - 1:lo] if i > 0 else jnp.zeros_like(cl)
        eq = jnp.exp(cum[lo:lo + SUB] - r)
        ek = jnp.exp(jnp.minimum(r - cum, EXP_CLAMP))
        qt = q[lo:lo + SUB] * eq
        kt_bd = _pair_blockdiag(k * ek)
        rows.append(_dot(qt, kt_bd, "nt", hi=True))
        subs.append((qt, eq, kt_bd, ek))
    attn = jnp.where(causal2, jnp.concatenate(rows, axis=0), 0.0)
    v_bd = _pair_blockdiag(v)
    o = _dot(qd, s_bd) + _dot(attn, v_bd)
    return dict(q=q, sq=sq, sg=sg, sgn=sgn, f=f, k=k, cum=cum, cl=cl, ecum=ecum, ekd=ekd, cd=cd, qd=qd, kd=kd,
                subs=subs, attn=attn, v_bd=v_bd, o=o)


def _a_fwd(proj, lb, norm_g, tb=256):
    t = proj.shape[0]
    nch = tb // CHUNK

    def body(q_ref, f_ref, i_ref, g_ref, lb_ref, ng_ref, out_ref, st_ref, s_scr):
        @pl.when(pl.program_id(0) == 0)
        def _():
            s_scr[...] = jnp.zeros_like(s_scr)

        ltri = _tril(CHUNK).astype(F32)
        r, c = _iota2(CHUNK, A_PAIR_W)
        causal2 = r >= c % CHUNK
        rb, cb = _iota2(A_PAIR_W, A_PAIR_W)
        diag_blocks = (rb < A_DIM) == (cb < A_DIM)

        def chunk(c, carry):
            rows = pl.ds(pl.multiple_of(c * CHUNK, CHUNK), CHUNK)
            ps = range(A_PAIRS)
            cols = [slice(p * A_PAIR_W, (p + 1) * A_PAIR_W) for p in ps]
            s0 = [s_scr[p] for p in ps]
            for p in ps:
                st_ref[c, p] = s0[p]
            v = [i_ref[rows, cols[p]] for p in ps]
            ch = [_a_pair_chunk(q_ref[rows, cols[p]], f_ref[rows, cols[p]], v[p], lb_ref[:, cols[p]], s0[p], ltri, causal2)
                  for p in ps]
            for p in ps:
                upd = jnp.where(diag_blocks, _dot(ch[p]["kd"], v[p], "tn"), 0.0)
                s_scr[p] = s0[p] * ch[p]["cd"].T + upd
            outs = []
            for p in ps:
                gi = g_ref[rows, cols[p]]
                outs.append(ch[p]["o"] * _pair_norm_stats(ch[p]["o"]) * ng_ref[...] * (gi * _sigmoid(gi)))
            out_ref[rows, :] = jnp.concatenate(outs, axis=1).astype(BF16)
            return carry

        lax.fori_loop(0, nch, chunk, 0, unroll=4)

    colblk = lambda j: pl.BlockSpec((tb, A_WIDTH), lambda i, j=j: (i, j))
    return pl.pallas_call(
        body, grid=(t // tb,),
        in_specs=[colblk(0), colblk(1), colblk(2), colblk(3), pl.BlockSpec((1, A_WIDTH), lambda i: (0, 0)),
                  pl.BlockSpec((1, A_PAIR_W), lambda i: (0, 0))],
        out_specs=[pl.BlockSpec((tb, A_WIDTH), lambda i: (i, 0)),
                   pl.BlockSpec((nch, A_PAIRS, A_PAIR_W, A_PAIR_W), lambda i: (i, 0, 0, 0))],
        out_shape=[jax.ShapeDtypeStruct((t, A_WIDTH), BF16),
                   jax.ShapeDtypeStruct((t // CHUNK, A_PAIRS, A_PAIR_W, A_PAIR_W), F32)],
        scratch_shapes=[pltpu.VMEM((A_PAIRS, A_PAIR_W, A_PAIR_W), F32)],
        compiler_params=_cparams(("arbitrary",)), name="hgrn_fwd")(
            proj, proj, proj, proj, lb.reshape(1, A_WIDTH), jnp.tile(norm_g.reshape(1, A_DIM), (1, 2)))


def _a_bwd(proj, lb, norm_g, states, dmixed, beside=None, tb=256):
    t = proj.shape[0]
    nch = tb // CHUNK
    nb = t // tb

    def body(q_ref, f_ref, i_ref, g_ref, lb_ref, ng_ref, st_ref, dm_ref, dp_ref, dlb_ref, dng_ref, ds_scr):
        @pl.when(pl.program_id(0) == 0)
        def _():
            ds_scr[...] = jnp.zeros_like(ds_scr)
            dlb_ref[...] = jnp.zeros_like(dlb_ref)
            dng_ref[...] = jnp.zeros_like(dng_ref)

        ltri = _tril(CHUNK).astype(F32)
        r, c = _iota2(CHUNK, A_PAIR_W)
        causal2 = r >= c % CHUNK
        rb, cb = _iota2(A_PAIR_W, A_PAIR_W)
        diag_blocks = (rb < A_DIM) == (cb < A_DIM)
        ones8 = jnp.ones((8, A_PAIR_W), F32)

        def chunk(cc, carry):
            c = nch - 1 - cc
            rows = pl.ds(pl.multiple_of(c * CHUNK, CHUNK), CHUNK)
            ps = range(A_PAIRS)
            cols = [slice(p * A_PAIR_W, (p + 1) * A_PAIR_W) for p in ps]
            qi = [q_ref[rows, cols[p]] for p in ps]
            gi = [g_ref[rows, cols[p]] for p in ps]
            v = [i_ref[rows, cols[p]] for p in ps]
            lb2 = [lb_ref[:, cols[p]] for p in ps]
            s0 = [st_ref[c, p] for p in ps]
            ds = [ds_scr[p] for p in ps]
            ch = [_a_pair_chunk(qi[p], f_ref[rows, cols[p]], v[p], lb2[p], s0[p], ltri, causal2) for p in ps]
            o = [ch[p]["o"] for p in ps]
            rinv = [_pair_norm_stats(o[p]) for p in ps]
            xhat = [o[p] * rinv[p] for p in ps]
            sgg = [_sigmoid(gi[p]) for p in ps]
            dout = [dm_ref[rows, cols[p]].astype(F32) for p in ps]
            dy = [dout[p] * (gi[p] * sgg[p]) for p in ps]
            dgi = [dout[p] * xhat[p] * ng_ref[...] * (sgg[p] * (1.0 + gi[p] * (1.0 - sgg[p]))) for p in ps]
            dxh = [dy[p] * ng_ref[...] for p in ps]
            do = [rinv[p] * (dxh[p] - xhat[p] * _pair_mean(dxh[p] * xhat[p])) for p in ps]
            dng = sum(jnp.sum(dy[p] * xhat[p], axis=0, keepdims=True) for p in ps)
            dqd = [_dot(do[p], s0[p], "nt") for p in ps]
            dattn = [jnp.where(causal2, _dot(do[p], ch[p]["v_bd"], "nt"), 0.0) for p in ps]
            dv = [_pair_fold(_dot(ch[p]["attn"], do[p], "tn")) + _dot(ch[p]["kd"], ds[p]) for p in ps]
            dkd = [_dot(v[p], ds[p], "nt") for p in ps]
            dcd = [_dot(ones8, s0[p] * ds[p], "nt", hi=True)[0:1] for p in ps]
            for p in ps:
                ds_scr[p] = jnp.where(diag_blocks, _dot(ch[p]["qd"], do[p], "tn"), 0.0) + ds[p] * ch[p]["cd"].T
            dq_i, dk_i = [], []
            for p in ps:
                dq_rows, dk = [], None
                for i, (qt, eq, kt_bd, ek) in enumerate(ch[p]["subs"]):
                    da = dattn[p][i * SUB:(i + 1) * SUB]
                    dq_rows.append(_dot(da, kt_bd, "nn", hi=True) * eq)
                    d = _pair_fold(_dot(da, qt, "tn", hi=True)) * ek
                    dk = d if dk is None else dk + d
                dq_i.append(jnp.concatenate(dq_rows, axis=0))
                dk_i.append(dk)
            dq = [dqd[p] * ch[p]["ecum"] + dq_i[p] for p in ps]
            dk = [dkd[p] * ch[p]["ekd"] + dk_i[p] for p in ps]
            dkk = [dkd[p] * ch[p]["kd"] for p in ps]
            dcum = [dqd[p] * ch[p]["qd"] - dkk[p] + ch[p]["q"] * dq_i[p] - ch[p]["k"] * dk_i[p] for p in ps]
            dcl = [jnp.sum(dkk[p], axis=0, keepdims=True) + dcd[p] * ch[p]["cd"] for p in ps]
            dlogf = [_dot_exact_lhs(ltri, dcum[p], "tn") + dcl[p] for p in ps]
            dfv = [jnp.where(ch[p]["f"] > TINY, dlogf[p] / ch[p]["f"], 0.0) for p in ps]
            dfi = [dfv[p] * (1.0 - lb2[p]) * ch[p]["sg"] * (1.0 - ch[p]["sg"])
                   - dk[p] * (1.0 - lb2[p]) * ch[p]["sgn"] * (1.0 - ch[p]["sgn"]) for p in ps]
            dlbs = [jnp.sum(dfv[p] * (1.0 - ch[p]["sg"]) - dk[p] * ch[p]["sgn"], axis=0, keepdims=True) for p in ps]
            dqs = [dq[p] * (ch[p]["sq"] * (1.0 + qi[p] * (1.0 - ch[p]["sq"]))) for p in ps]
            dp_ref[rows, :] = jnp.concatenate(dqs + dfi + dv + dgi, axis=1).astype(BF16)
            dlb_ref[...] += jnp.concatenate(dlbs, axis=1)
            dng_ref[...] += dng
            return carry

        lax.fori_loop(0, nch, chunk, 0, unroll=4)

    colblk = lambda j: pl.BlockSpec((tb, A_WIDTH), lambda i, j=j: (nb - 1 - i, j))
    vec = lambda n: pl.BlockSpec((1, n), lambda i: (0, 0))
    return _call_beside(
        body, beside, nb,
        (proj, proj, proj, proj, lb.reshape(1, A_WIDTH), jnp.tile(norm_g.reshape(1, A_DIM), (1, 2)), states, dmixed), grid=(nb,),
        in_specs=[colblk(0), colblk(1), colblk(2), colblk(3), vec(A_WIDTH), vec(A_PAIR_W),
                  pl.BlockSpec((nch, A_PAIRS, A_PAIR_W, A_PAIR_W), lambda i: (nb - 1 - i, 0, 0, 0)), colblk(0)],
        out_specs=[pl.BlockSpec((tb, 4 * A_WIDTH), lambda i: (nb - 1 - i, 0)), vec(A_WIDTH), vec(A_PAIR_W)],
        out_shape=[jax.ShapeDtypeStruct((t, 4 * A_WIDTH), BF16), jax.ShapeDtypeStruct((1, A_WIDTH), F32),
                   jax.ShapeDtypeStruct((1, A_PAIR_W), F32)],
        scratch_shapes=[pltpu.VMEM((A_PAIRS, A_PAIR_W, A_PAIR_W), F32)], name="hgrn_bwd")


def _gate_lane_masks(shape):
    lane = lax.broadcasted_iota(jnp.int32, shape, 1)
    return lane < B_HEADS, (lane >= B_HEADS) & (lane < 2 * B_HEADS)


def _b_pre_fwd(proj, conv_w, alog_row, dtb_row, tb=512):
    t = proj.shape[0]
    cb0 = COL_B // B_WIDTH

    def body(q_ref, k_ref, v_ref, qp_ref, kp_ref, vp_ref, w_ref, gi_ref, al_ref, dt_ref, qkv_ref, gates_ref, y_ref):
        first = pl.program_id(0) == 0
        for part, (c_ref, p_ref) in enumerate(((q_ref, qp_ref), (k_ref, kp_ref), (v_ref, vp_ref))):
            cols = slice(part * B_WIDTH, (part + 1) * B_WIDTH)
            prev = jnp.where(first, 0.0, p_ref[...])
            y = _conv_fwd(c_ref[...], prev, w_ref[:, cols])
            y_ref[:, cols] = y
            s = y * _sigmoid(y)
            if part < 2:
                outs = []
                for h in range(B_HEADS):
                    sh = s[:, h * B_DIM:(h + 1) * B_DIM]
                    outs.append(sh * lax.rsqrt(jnp.sum(sh * sh, axis=-1, keepdims=True) + EPS))
                s = jnp.concatenate(outs, axis=1)
            qkv_ref[:, cols] = s
        g = gi_ref[...]
        is_b, is_a = _gate_lane_masks(g.shape)
        la = -jnp.exp(al_ref[...]) * _softplus(g + dt_ref[...])
        gates_ref[...] = jnp.where(is_b, _sigmoid(g), jnp.where(is_a, la, 0.0))

    cur = lambda j: pl.BlockSpec((tb, B_WIDTH), lambda i, j=j: (i, cb0 + j))
    prv = lambda j: pl.BlockSpec((8, B_WIDTH), lambda i, j=j: (jnp.maximum(i * (tb // 8) - 1, 0), cb0 + j))
    vec = pl.BlockSpec((1, 128), lambda i: (0, 0))
    return pl.pallas_call(
        body, grid=(t // tb,),
        in_specs=[cur(0), cur(1), cur(2), prv(0), prv(1), prv(2), pl.BlockSpec((CONV_K, 3 * B_WIDTH), lambda i: (0, 0)),
                  pl.BlockSpec((tb, 128), lambda i: (i, COL_G // 128)), vec, vec],
        out_specs=[pl.BlockSpec((tb, 3 * B_WIDTH), lambda i: (i, 0)), pl.BlockSpec((tb, 128), lambda i: (i, 0)),
                   pl.BlockSpec((tb, 3 * B_WIDTH), lambda i: (i, 0))],
        out_shape=[jax.ShapeDtypeStruct((t, 3 * B_WIDTH), F32), jax.ShapeDtypeStruct((t, 128), F32),
                   jax.ShapeDtypeStruct((t, 3 * B_WIDTH), F32)],
        compiler_params=_cparams(("parallel",)), name="gdn_pre_fwd")(proj, proj, proj, proj, proj, proj, conv_w, proj, alog_row, dtb_row)


def _inv_unit_lower(amats):
    r, c = _iota2(CHUNK, CHUNK)
    eye = jnp.where(r == c, 1.0, 0.0)
    ps = [eye - a for a in amats]
    aks = amats
    for _ in range(5):
        aks = [_dot(ak, ak, hi=True) for ak in aks]
        ps = [p + _dot(p, ak, hi=True) for p, ak in zip(ps, aks)]
    return ps


def _b_local(qs, ks, vs, betas, gcs, grows, gls, solve=True):
    hs = range(len(qs))
    causal, strict = _tril(CHUNK), _tril(CHUNK, strict=True)
    decay = [jnp.where(causal, jnp.exp(jnp.minimum(gcs[h] - grows[h], 0.0)), 0.0) for h in hs]
    kb = [ks[h] * betas[h] for h in hs]
    kk = [_dot(kb[h], ks[h], "nt") for h in hs]
    qkr = [_dot(qs[h], ks[h], "nt") for h in hs]
    eg = [jnp.exp(gcs[h]) for h in hs]
    bv = [vs[h] * betas[h] for h in hs]
    kg = [kb[h] * eg[h] for h in hs]
    qk = [qkr[h] * decay[h] for h in hs]
    qd = [qs[h] * eg[h] for h in hs]
    ekd = [jnp.exp(gls[h] - gcs[h]) for h in hs]
    kd = [ks[h] * ekd[h] for h in hs]
    cd = [jnp.exp(gls[h]) for h in hs]
    loc = dict(decay=decay, kb=kb, kk=kk, eg=eg, bv=bv, kg=kg, qkr=qkr, qk=qk, qd=qd, ekd=ekd, kd=kd, cd=cd)
    if solve:
        tinv = _inv_unit_lower([jnp.where(strict, kk[h] * decay[h], 0.0) for h in hs])
        loc.update(tinv=tinv, u=[_dot(tinv[h], bv[h], hi=True) for h in hs], w=[_dot(tinv[h], kg[h], hi=True) for h in hs])
    return loc


def _b_state(loc, ids, s0s):
    n = range(len(ids))
    ws = [_dot(loc["w"][ids[j]], s0s[j]) for j in n]
    qs0 = [_dot(loc["qd"][ids[j]], s0s[j]) for j in n]
    vn = [loc["u"][ids[j]] - ws[j] for j in n]
    o = [qs0[j] + _dot(loc["qk"][ids[j]], vn[j]) for j in n]
    s1 = [s0s[j] * loc["cd"][ids[j]] + _dot(loc["kd"][ids[j]], vn[j], "tn") for j in n]
    return vn, o, s1


def _b_fwd(qkv, gates, proj, norm_g, beside=None, tb=256):
    t = qkv.shape[0]
    nch = tb // CHUNK

    def body(q_ref, k_ref, v_ref, ga_ref, z_ref, ng_ref, out_ref, st_ref, ti_ref, w_ref, vn_ref, o_ref, s_scr):
        @pl.when(pl.program_id(0) == 0)
        def _():
            s_scr[...] = jnp.zeros_like(s_scr)

        ltri = _tril(CHUNK).astype(F32)

        hs = range(B_HEADS)
        cols = [slice(h * B_DIM, (h + 1) * B_DIM) for h in hs]

        def pair(p, carry):
            cs = [2 * p, 2 * p + 1]
            rows = [pl.ds(pl.multiple_of(c * CHUNK, CHUNK), CHUNK) for c in cs]
            ga = [ga_ref[r, :] for r in rows]
            gcum = [_dot_exact_lhs(ltri, g) for g in ga]
            gcum_t = [g.T for g in gcum]
            items = [(i, h) for i in range(2) for h in hs]
            loc = _b_local([q_ref[rows[i], cols[h]] * GDN_SCALE for i, h in items], [k_ref[rows[i], cols[h]] for i, h in items],
                           [v_ref[rows[i], cols[h]] for i, h in items], [ga[i][:, h:h + 1] for i, h in items],
                           [gcum[i][:, B_HEADS + h:B_HEADS + h + 1] for i, h in items],
                           [gcum_t[i][B_HEADS + h:B_HEADS + h + 1, :] for i, h in items],
                           [gcum[i][CHUNK - 1:CHUNK, B_HEADS + h:B_HEADS + h + 1] for i, h in items])
            s0s = [s_scr[h] for h in hs]
            for i in range(2):
                ids = [i * B_HEADS + h for h in hs]
                for h in hs:
                    st_ref[cs[i], h] = s0s[h]
                    ti_ref[cs[i], h] = loc["tinv"][ids[h]]
                vn, o, s0s = _b_state(loc, ids, s0s)
                w_ref[rows[i], :] = jnp.concatenate([loc["w"][j] for j in ids], axis=1)
                vn_ref[rows[i], :] = jnp.concatenate(vn, axis=1)
                o_ref[rows[i], :] = jnp.concatenate(o, axis=1)
                outs = [_headnorm_fwd(o[h], ng_ref[...], z_ref[rows[i], cols[h]]) for h in hs]
                out_ref[rows[i], :] = jnp.concatenate(outs, axis=1).astype(BF16)
            for h in hs:
                s_scr[h] = s0s[h]
            return carry

        lax.fori_loop(0, nch // 2, pair, 0, unroll=2)

    part = lambda j: pl.BlockSpec((tb, B_WIDTH), lambda i, j=j: (i, j))
    wide = pl.BlockSpec((tb, B_WIDTH), lambda i: (i, 0))
    wide_shape = jax.ShapeDtypeStruct((t, B_WIDTH), F32)
    return _call_beside(
        body, beside, t // tb, (qkv, qkv, qkv, gates, proj, norm_g.reshape(1, B_DIM)), grid=(t // tb,),
        in_specs=[part(0), part(1), part(2), pl.BlockSpec((tb, 128), lambda i: (i, 0)),
                  pl.BlockSpec((tb, B_WIDTH), lambda i: (i, COL_B // B_WIDTH + 3)), pl.BlockSpec((1, B_DIM), lambda i: (0, 0))],
        out_specs=[wide, pl.BlockSpec((nch, B_HEADS, B_DIM, B_DIM), lambda i: (i, 0, 0, 0)),
                   pl.BlockSpec((nch, B_HEADS, CHUNK, CHUNK), lambda i: (i, 0, 0, 0)), wide, wide, wide],
        out_shape=[jax.ShapeDtypeStruct((t, B_WIDTH), BF16), jax.ShapeDtypeStruct((t // CHUNK, B_HEADS, B_DIM, B_DIM), F32),
                   jax.ShapeDtypeStruct((t // CHUNK, B_HEADS, CHUNK, CHUNK), F32), wide_shape, wide_shape, wide_shape],
        scratch_shapes=[pltpu.VMEM((B_HEADS, B_DIM, B_DIM), F32)], name="gdn_fwd")


def _b_bwd(qkv, gates, proj, norm_g, states, fwd_saved, dmixed, beside=None, tb=256):
    t = qkv.shape[0]
    nch = tb // CHUNK
    nb = t // tb

    def body(q_ref, k_ref, v_ref, ga_ref, z_ref, ng_ref, st_ref, ti_ref, w_ref, vn_ref, o_ref, dm0_ref, dm1_ref,
             dqkv_ref, dga_ref, dz_ref, dng_ref, ds_scr):
        @pl.when(pl.program_id(0) == 0)
        def _():
            ds_scr[...] = jnp.zeros_like(ds_scr)
            dng_ref[...] = jnp.zeros_like(dng_ref)

        ltri = _tril(CHUNK).astype(F32)
        strict = _tril(CHUNK, strict=True)
        lane = lax.broadcasted_iota(jnp.int32, (CHUNK, 128), 1)
        lane1 = lax.broadcasted_iota(jnp.int32, (1, 128), 1)

        nh = range(B_HEADS)
        cols = [slice(h * B_DIM, (h + 1) * B_DIM) for h in nh]
        rsum = lambda a: jnp.sum(a, axis=-1, keepdims=True)

        def pair(p, carry):
            cs = [nch - 1 - 2 * p, nch - 2 - 2 * p]
            crow = [pl.ds(pl.multiple_of(c * CHUNK, CHUNK), CHUNK) for c in cs]
            gas = [ga_ref[r, :] for r in crow]
            gcum = [_dot_exact_lhs(ltri, g) for g in gas]
            gcum_t = [g.T for g in gcum]
            items = [(i, h) for i in range(2) for h in nh]
            hs = range(len(items))
            q = [q_ref[crow[i], cols[h]] * GDN_SCALE for i, h in items]
            k = [k_ref[crow[i], cols[h]] for i, h in items]
            v = [v_ref[crow[i], cols[h]] for i, h in items]
            z = [z_ref[crow[i], cols[h]] for i, h in items]
            beta = [gas[i][:, h:h + 1] for i, h in items]
            s0 = [st_ref[cs[i], h] for i, h in items]
            r = _b_local(q, k, v, beta, [gcum[i][:, B_HEADS + h:B_HEADS + h + 1] for i, h in items],
                         [gcum_t[i][B_HEADS + h:B_HEADS + h + 1, :] for i, h in items],
                         [gcum[i][CHUNK - 1:CHUNK, B_HEADS + h:B_HEADS + h + 1] for i, h in items], solve=False)
            tinv = [ti_ref[cs[i], h] for i, h in items]
            w = [w_ref[crow[i], cols[h]] for i, h in items]
            vn = [vn_ref[crow[i], cols[h]] for i, h in items]
            decay, eg, qd, kd, kb, cd = (r[n] for n in ("decay", "eg", "qd", "kd", "kb", "cd"))
            dms = [(dm0_ref if h < 2 else dm1_ref)[crow[i], (h % 2) * B_DIM:(h % 2 + 1) * B_DIM].astype(F32) for i, h in items]
            hn = [_headnorm_bwd(dms[j], o_ref[crow[i], cols[h]], ng_ref[...], z[j]) for j, (i, h) in enumerate(items)]
            do = [hn[j][0] for j in hs]
            dvn_o = [_dot(r["qk"][j], do[j], "tn") for j in hs]
            dqk = [_dot(do[j], vn[j], "nt") for j in hs]
            dqd = [_dot(do[j], s0[j], "nt") for j in hs]
            ds_o = [_dot(qd[j], do[j], "tn") for j in hs]
            ds = [ds_scr[h] for h in nh]
            dvn, dkd, dcd = [None] * 8, [None] * 8, [None] * 8
            for i in range(2):
                for h in nh:
                    j = i * B_HEADS + h
                    dvn[j] = dvn_o[j] + _dot(kd[j], ds[h])
                    dkd[j] = _dot(vn[j], ds[h], "nt")
                    dcd[j] = jnp.sum(jnp.sum(s0[j] * ds[h], axis=0, keepdims=True), axis=1, keepdims=True)
                ds = [ds_o[i * B_HEADS + h] + ds[h] * cd[i * B_HEADS + h] - _dot(w[i * B_HEADS + h], dvn[i * B_HEADS + h], "tn")
                      for h in nh]
            for h in nh:
                ds_scr[h] = ds[h]
            dw = [-_dot(dvn[j], s0[j], "nt") for j in hs]
            dbv = [_dot(tinv[h], dvn[h], "tn", hi=True) for h in hs]
            dkg = [_dot(tinv[h], dw[h], "tn", hi=True) for h in hs]
            dt = [_dot(dvn[h], r["bv"][h], "nt", hi=True) + _dot(dw[h], r["kg"][h], "nt", hi=True) for h in hs]
            tdt = [_dot(tinv[h], dt[h], "tn", hi=True) for h in hs]
            da = [jnp.where(strict, -_dot(tdt[h], tinv[h], "nt", hi=True), 0.0) for h in hs]
            dm = [da[h] * decay[h] for h in hs]
            dn = [dqk[h] * decay[h] for h in hs]
            e = [(da[h] * r["kk"][h] + dqk[h] * r["qkr"][h]) * decay[h] for h in hs]
            dkb = [_dot(dm[h], k[h]) + dkg[h] * eg[h] for h in hs]
            dk = [_dot(dm[h], kb[h], "tn") + _dot(dn[h], q[h], "tn") + dkd[h] * r["ekd"][h] + dkb[h] * beta[h] for h in hs]
            dq = [_dot(dn[h], k[h]) + dqd[h] * eg[h] for h in hs]
            tkd = [rsum(dkd[h] * kd[h]) for h in hs]
            dgc = [rsum(e[h]) - rsum(e[h].T) + rsum(dqd[h] * qd[h]) - tkd[h] + rsum(dkg[h] * r["kg"][h]) for h in hs]
            dgl = [jnp.sum(tkd[h], axis=0, keepdims=True) + dcd[h] * cd[h] for h in hs]
            dbeta = [rsum(dbv[h] * v[h]) + rsum(dkb[h] * k[h]) for h in hs]
            for i in range(2):
                ids = [i * B_HEADS + h for h in nh]
                dbeta_m = sum(jnp.where(lane == h, dbeta[ids[h]], 0.0) for h in nh)
                dgc_m = sum(jnp.where(lane == B_HEADS + h, dgc[ids[h]], 0.0) for h in nh)
                dgl_m = sum(jnp.where(lane1 == B_HEADS + h, dgl[ids[h]], 0.0) for h in nh)
                dqkv_ref[crow[i], :] = jnp.concatenate(
                    [dq[j] * GDN_SCALE for j in ids] + [dk[j] for j in ids] + [dbv[j] * beta[j] for j in ids], axis=1)
                dz_ref[crow[i], :] = jnp.concatenate([hn[j][1] for j in ids], axis=1).astype(BF16)
                dga_ref[crow[i], :] = dbeta_m + _dot_exact_lhs(ltri, dgc_m, "tn") + dgl_m
            dng_ref[...] += sum(hn[j][2] for j in hs)
            return carry

        lax.fori_loop(0, nch // 2, pair, 0, unroll=2)

    part = lambda j: pl.BlockSpec((tb, B_WIDTH), lambda i, j=j: (nb - 1 - i, j))
    rowblk = lambda w, j=0: pl.BlockSpec((tb, w), lambda i, j=j: (nb - 1 - i, j))
    return _call_beside(
        body, beside, nb, (qkv, qkv, qkv, gates, proj, norm_g.reshape(1, B_DIM), states, *fwd_saved, dmixed, dmixed), grid=(nb,),
        in_specs=[part(0), part(1), part(2), rowblk(128), rowblk(B_WIDTH, COL_B // B_WIDTH + 3),
                  pl.BlockSpec((1, B_DIM), lambda i: (0, 0)),
                  pl.BlockSpec((nch, B_HEADS, B_DIM, B_DIM), lambda i: (nb - 1 - i, 0, 0, 0)),
                  pl.BlockSpec((nch, B_HEADS, CHUNK, CHUNK), lambda i: (nb - 1 - i, 0, 0, 0)),
                  rowblk(B_WIDTH), rowblk(B_WIDTH), rowblk(B_WIDTH), rowblk(256, 1), rowblk(256, 2)],
        out_specs=[rowblk(3 * B_WIDTH), rowblk(128), rowblk(B_WIDTH), pl.BlockSpec((1, B_DIM), lambda i: (0, 0))],
        out_shape=[jax.ShapeDtypeStruct((t, 3 * B_WIDTH), F32), jax.ShapeDtypeStruct((t, 128), F32),
                   jax.ShapeDtypeStruct((t, B_WIDTH), BF16), jax.ShapeDtypeStruct((1, B_DIM), F32)],
        scratch_shapes=[pltpu.VMEM((B_HEADS, B_DIM, B_DIM), F32)], name="gdn_bwd")


def _b_pre_bwd(proj, y3, conv_w, alog_row, dtb_row, dqkv, dgates, tb=512):
    t = proj.shape[0]
    nb = t // tb
    cb0 = COL_B // B_WIDTH

    def body(q_ref, k_ref, v_ref, y_ref, w_ref, gi_ref, al_ref, dt_ref, dqkv_ref, dga_ref,
             dy_ref, dgi_ref, dw_ref, dal_ref, ddt_ref, nxt_scr):
        step_id = pl.program_id(0)

        @pl.when(step_id == 0)
        def _():
            dw_ref[...] = jnp.zeros_like(dw_ref)
            dal_ref[...] = jnp.zeros_like(dal_ref)
            ddt_ref[...] = jnp.zeros_like(ddt_ref)

        for part, c_ref in enumerate((q_ref, k_ref, v_ref)):
            cols = slice(part * B_WIDTH, (part + 1) * B_WIDTH)
            cur = c_ref[...]
            w = w_ref[:, cols]
            y = y_ref[:, cols]
            sg = _sigmoid(y)
            s = y * sg
            dsn = dqkv_ref[:, cols]
            if part < 2:
                outs = []
                for h in range(B_HEADS):
                    hc = slice(h * B_DIM, (h + 1) * B_DIM)
                    sh, dh = s[:, hc], dsn[:, hc]
                    rq = lax.rsqrt(jnp.sum(sh * sh, axis=-1, keepdims=True) + EPS)
                    nh = sh * rq
                    outs.append(rq * (dh - nh * jnp.sum(dh * nh, axis=-1, keepdims=True)))
                dsn = jnp.concatenate(outs, axis=1)
            dy = dsn * (sg * (1.0 + y * (1.0 - sg)))
            later = _later_rows(dy, jnp.where(step_id == 0, 0.0, nxt_scr[:, cols]))
            dy_ref[:, cols] = sum(later[j] * w[j:j + 1] for j in range(CONV_K)).astype(BF16)
            nxt_scr[:, cols] = dy[0:8]
            dw_ref[:, cols] += jnp.concatenate([jnp.sum(cur * later[j], axis=0, keepdims=True) for j in range(CONV_K)], axis=0)
        g = gi_ref[...]
        dga = dga_ref[...]
        is_b, is_a = _gate_lane_masks(g.shape)
        beta = _sigmoid(g)
        pre = g + dt_ref[...]
        ea = jnp.exp(al_ref[...])
        la = -ea * _softplus(pre)
        dpre = jnp.where(is_a, dga * (-ea) * _sigmoid(pre), 0.0)
        dgi_ref[...] = jnp.where(is_b, dga * beta * (1.0 - beta), dpre).astype(BF16)
        dal_ref[...] += jnp.sum(jnp.where(is_a, dga * la, 0.0), axis=0, keepdims=True)
        ddt_ref[...] += jnp.sum(dpre, axis=0, keepdims=True)

    cur = lambda j: pl.BlockSpec((tb, B_WIDTH), lambda i, j=j: (nb - 1 - i, cb0 + j))
    vec = pl.BlockSpec((1, 128), lambda i: (0, 0))
    wspec = pl.BlockSpec((CONV_K, 3 * B_WIDTH), lambda i: (0, 0))
    rowblk = lambda width, j=0: pl.BlockSpec((tb, width), lambda i, j=j: (nb - 1 - i, j))
    return pl.pallas_call(
        body, grid=(nb,),
        in_specs=[cur(0), cur(1), cur(2), rowblk(3 * B_WIDTH), wspec, rowblk(128, COL_G // 128), vec, vec,
                  rowblk(3 * B_WIDTH), rowblk(128)],
        out_specs=[rowblk(3 * B_WIDTH), rowblk(128), wspec, vec, vec],
        out_shape=[jax.ShapeDtypeStruct((t, 3 * B_WIDTH), BF16), jax.ShapeDtypeStruct((t, 128), BF16),
                   jax.ShapeDtypeStruct((CONV_K, 3 * B_WIDTH), F32), jax.ShapeDtypeStruct((1, 128), F32), jax.ShapeDtypeStruct((1, 128), F32)],
        scratch_shapes=[pltpu.VMEM((8, 3 * B_WIDTH), F32)],
        compiler_params=_cparams(("arbitrary",)), name="gdn_pre_bwd")(
            proj, proj, proj, y3, conv_w, proj, alog_row, dtb_row, dqkv, dgates)


def _c_gates(xc, wa_ref, ba_ref, wx_ref, bx_ref, lam_ref, is_row0):
    r = _sigmoid(_dot(xc, wa_ref[...]) + ba_ref[...])
    i = _sigmoid(_dot(xc, wx_ref[...]) + bx_ref[...])
    sp = _softplus(-lam_ref[...])
    log_a = -RG_C * r * sp
    a = jnp.exp(log_a)
    m2 = _neg_expm1(2.0 * log_a)
    mult = jnp.where(is_row0, 1.0, jnp.sqrt(jnp.maximum(m2, EPS)))
    return r, i, sp, log_a, a, m2, mult


def _row0_mask(tb, first):
    ridx = lax.broadcasted_iota(jnp.int32, (tb, C_WIDTH), 0)
    return (ridx == 0) & first


def _c_fwd(proj, conv_w, conv_b, wa, ba, wx, bx, lam, tb=512):
    t = proj.shape[0]
    cbx = COL_C // C_WIDTH

    def body(x_ref, xp_ref, y_ref, w_ref, cb_ref, wa_ref, ba_ref, wx_ref, bx_ref, lam_ref, out_ref, h_ref, a_scr, b_scr, h_scr):
        first = pl.program_id(0) == 0

        @pl.when(first)
        def _():
            h_scr[...] = jnp.zeros_like(h_scr)

        prev = jnp.where(first, 0.0, xp_ref[...])
        xc = _conv_fwd(x_ref[...], prev, w_ref[...]) + cb_ref[...]
        _, i, _, _, a, _, mult = _c_gates(xc, wa_ref, ba_ref, wx_ref, bx_ref, lam_ref, _row0_mask(tb, first))
        ta, tb_ = _tile_scan(a, mult * i * xc)
        a_scr[...] = ta
        b_scr[...] = tb_

        def step(blk, h):
            rows = pl.ds(pl.multiple_of(blk * 8, 8), 8)
            h_ref[rows, :] = jnp.broadcast_to(h, (8, C_WIDTH))
            return a_scr[rows, :][7:8] * h + b_scr[rows, :][7:8]

        h_scr[...] = lax.fori_loop(0, tb // 8, step, h_scr[...], unroll=8)
        hs = ta * h_ref[...] + tb_
        h_ref[...] = hs
        gl, _ = _gelu_tanh(y_ref[...])
        out_ref[...] = (gl * hs).astype(BF16)

    vec = pl.BlockSpec((1, C_WIDTH), lambda i: (0, 0))
    mat = pl.BlockSpec((C_WIDTH, C_WIDTH), lambda i: (0, 0))
    row = pl.BlockSpec((tb, C_WIDTH), lambda i: (i, 0))
    return pl.pallas_call(
        body, grid=(t // tb,),
        in_specs=[pl.BlockSpec((tb, C_WIDTH), lambda i: (i, cbx)),
                  pl.BlockSpec((8, C_WIDTH), lambda i: (jnp.maximum(i * (tb // 8) - 1, 0), cbx)),
                  pl.BlockSpec((tb, C_WIDTH), lambda i: (i, cbx + 1)),
                  pl.BlockSpec((CONV_K, C_WIDTH), lambda i: (0, 0)), vec, mat, vec, mat, vec, vec],
        out_specs=[row, row],
        out_shape=[jax.ShapeDtypeStruct((t, C_WIDTH), BF16), jax.ShapeDtypeStruct((t, C_WIDTH), F32)],
        scratch_shapes=[pltpu.VMEM((tb, C_WIDTH), F32), pltpu.VMEM((tb, C_WIDTH), F32), pltpu.VMEM((1, C_WIDTH), F32)],
        compiler_params=_cparams(("arbitrary",)), name="lru_fwd")(proj, proj, proj, conv_w, conv_b, wa, ba, wx, bx, lam)


def _c_bwd(proj, conv_w, conv_b, wa, ba, wx, bx, lam, hs, dmixed, tb=512):
    t = proj.shape[0]
    nb = t // tb
    cbx = COL_C // C_WIDTH

    def body(x_ref, xp_ref, y_ref, w_ref, cb_ref, wa_ref, ba_ref, wx_ref, bx_ref, lam_ref, h_ref, hp_ref, dm_ref,
             dxc_ref, dyg_ref, dw_ref, dcb_ref, dwa_ref, dba_ref, dwx_ref, dbx_ref, dlam_ref, g_scr, a_scr, cin_scr, c_scr, nxt_scr):
        step_id = pl.program_id(0)
        first = step_id == nb - 1

        @pl.when(step_id == 0)
        def _():
            c_scr[...] = jnp.zeros_like(c_scr)
            for ref in (dw_ref, dcb_ref, dwa_ref, dba_ref, dwx_ref, dbx_ref, dlam_ref):
                ref[...] = jnp.zeros_like(ref)

        cur = x_ref[...]
        prev = jnp.where(first, 0.0, xp_ref[...])
        w = w_ref[...]
        shifted = [_shift_rows(cur, prev, 3 - j, down=True) for j in range(3)] + [cur]
        xc = shifted[0] * w[0:1] + shifted[1] * w[1:2] + shifted[2] * w[2:3] + shifted[3] * w[3:4] + cb_ref[...]
        row0 = _row0_mask(tb, first)
        r, i, sp, log_a, a, m2, mult = _c_gates(xc, wa_ref, ba_ref, wx_ref, bx_ref, lam_ref, row0)
        h = h_ref[...]
        hprev = _shift_rows(h, jnp.where(first, 0.0, hp_ref[...]), 1, down=True)
        gl, dgl = _gelu_tanh(y_ref[...])
        dm = dm_ref[...].astype(F32)
        dyg_ref[...] = (dm * h * dgl).astype(BF16)
        dout = dm * gl
        ta, te = _tile_scan(a, a * dout, reverse=True)
        a_scr[...] = ta
        g_scr[...] = te

        def step(blk, carry):
            rows = pl.ds(pl.multiple_of((tb // 8 - 1 - blk) * 8, 8), 8)
            cin_scr[rows, :] = jnp.broadcast_to(carry, (8, C_WIDTH))
            return a_scr[rows, :][0:1] * carry + g_scr[rows, :][0:1]

        c_scr[...] = lax.fori_loop(0, tb // 8, step, c_scr[...], unroll=8)
        cin = cin_scr[...]
        cout = ta * cin + te
        last_in_tile = lax.broadcasted_iota(jnp.int32, (tb, C_WIDTH), 0) % 8 == 7
        dbx = dout + jnp.where(last_in_tile, cin, pltpu.roll(cout, tb - 1, 0))
        da = dbx * hprev
        dmult = jnp.where(row0, 0.0, dbx * i * xc)
        di = dbx * mult * xc
        dxc = dbx * mult * i
        dm2 = jnp.where(m2 > EPS, dmult * 0.5 / mult, 0.0)
        dlog_a = da * a - 2.0 * a * a * dm2
        dr = dlog_a * (-RG_C) * sp
        dlam_ref[...] += jnp.sum(dlog_a * (-RG_C) * r, axis=0, keepdims=True) * (-_sigmoid(-lam_ref[...]))
        dpa = dr * r * (1.0 - r)
        dpx = di * i * (1.0 - i)
        dba_ref[...] += jnp.sum(dpa, axis=0, keepdims=True)
        dbx_ref[...] += jnp.sum(dpx, axis=0, keepdims=True)
        dwa_ref[...] += _dot(xc, dpa, "tn")
        dwx_ref[...] += _dot(xc, dpx, "tn")
        dxc = dxc + _dot(dpa, wa_ref[...], "nt") + _dot(dpx, wx_ref[...], "nt")
        dxc_ref[...] = _conv_bwd_rows(dxc, jnp.where(step_id == 0, 0.0, nxt_scr[...]), w).astype(BF16)
        nxt_scr[...] = dxc[0:8]
        dcb_ref[...] += jnp.sum(dxc, axis=0, keepdims=True)
        dw_ref[...] += jnp.concatenate([jnp.sum(shifted[j] * dxc, axis=0, keepdims=True) for j in range(CONV_K)], axis=0)

    vec = pl.BlockSpec((1, C_WIDTH), lambda i: (0, 0))
    mat = pl.BlockSpec((C_WIDTH, C_WIDTH), lambda i: (0, 0))
    cw = pl.BlockSpec((CONV_K, C_WIDTH), lambda i: (0, 0))
    row = lambda j=0: pl.BlockSpec((tb, C_WIDTH), lambda i, j=j: (nb - 1 - i, j))
    halo = lambda j=0: pl.BlockSpec((8, C_WIDTH), lambda i, j=j: (jnp.maximum((nb - 1 - i) * (tb // 8) - 1, 0), j))
    return pl.pallas_call(
        body, grid=(nb,),
        in_specs=[row(cbx), halo(cbx), row(cbx + 1), cw, vec, mat, vec, mat, vec, vec, row(), halo(), row(3)],
        out_specs=[row(), row(), cw, vec, mat, vec, mat, vec, vec],
        out_shape=[jax.ShapeDtypeStruct((t, C_WIDTH), BF16), jax.ShapeDtypeStruct((t, C_WIDTH), BF16),
                   jax.ShapeDtypeStruct((CONV_K, C_WIDTH), F32), jax.ShapeDtypeStruct((1, C_WIDTH), F32),
                   jax.ShapeDtypeStruct((C_WIDTH, C_WIDTH), F32), jax.ShapeDtypeStruct((1, C_WIDTH), F32),
                   jax.ShapeDtypeStruct((C_WIDTH, C_WIDTH), F32), jax.ShapeDtypeStruct((1, C_WIDTH), F32),
                   jax.ShapeDtypeStruct((1, C_WIDTH), F32)],
        scratch_shapes=[pltpu.VMEM((tb, C_WIDTH), F32), pltpu.VMEM((tb, C_WIDTH), F32), pltpu.VMEM((tb, C_WIDTH), F32),
                        pltpu.VMEM((1, C_WIDTH), F32), pltpu.VMEM((8, C_WIDTH), F32)],
        compiler_params=_cparams(("arbitrary",)), name="lru_bwd")(
            proj, proj, proj, conv_w, conv_b, wa, ba, wx, bx, lam, hs, hs, dmixed)


def _mesh_pos():
    return lax.axis_index("x"), lax.axis_index("y"), lax.axis_index("c")


class _Exchange:
    def __init__(self, arrays, layouts):
        self.arrays, self.layouts = list(arrays), list(layouts)
        self.out_shapes = []
        for a, lay in zip(self.arrays, self.layouts):
            if lay == 'a2a':
                shp = a.shape
            elif lay == 'slot':
                shp = (N_DEV,) + a.shape
            elif lay == 'rows':
                shp = (N_DEV * a.shape[0], a.shape[1])
            else:
                shp = (a.shape[0], N_DEV * a.shape[1])
            self.out_shapes.append(jax.ShapeDtypeStruct(shp, a.dtype))
        n = len(self.arrays)
        self.scratch = [pltpu.SemaphoreType.DMA((7 * n,)), pltpu.SemaphoreType.DMA((7 * n,)), pltpu.SemaphoreType.DMA((n,))]

    def _landing(self, a, dst_ref, idx):
        lay, shape = self.layouts[a], self.arrays[a].shape
        if lay in ('a2a', 'slot'):
            return dst_ref.at[idx]
        if lay == 'rows':
            return dst_ref.at[pl.ds(pl.multiple_of(idx * shape[0], shape[0]), shape[0]), :]
        return dst_ref.at[:, pl.ds(pl.multiple_of(idx * shape[1], shape[1]), shape[1])]

    def copies(self, src_refs, dst_refs, send_sems, recv_sems, local_sems):
        mx, my, mc = _mesh_pos()
        me = 4 * mx + 2 * my + mc
        out = []
        for a, (src, dst) in enumerate(zip(src_refs, dst_refs)):
            a2a = self.layouts[a] == 'a2a'
            out.append(pltpu.make_async_copy(src.at[me] if a2a else src, self._landing(a, dst, me), local_sems.at[a]))
            for k in range(1, N_DEV):
                px = 1 - mx if k & 4 else mx
                py = 1 - my if k & 2 else my
                pc = 1 - mc if k & 1 else mc
                out.append(pltpu.make_async_remote_copy(
                    src_ref=src.at[4 * px + 2 * py + pc] if a2a else src, dst_ref=self._landing(a, dst, me),
                    send_sem=send_sems.at[7 * a + k - 1], recv_sem=recv_sems.at[7 * a + k - 1],
                    device_id=(px, py, pc), device_id_type=MESH))
        return out


_ANY = pl.BlockSpec(memory_space=pl.ANY)


def _run_exchange(ex, name):
    n = len(ex.arrays)

    def body(*refs):
        cps = ex.copies(refs[:n], refs[n:2 * n], *refs[2 * n:])
        for cp in cps:
            cp.start()
        for cp in cps:
            cp.wait()

    return pl.pallas_call(body, out_shape=ex.out_shapes, in_specs=[_ANY] * n, out_specs=[_ANY] * n,
                          scratch_shapes=ex.scratch, name=name)(*ex.arrays)


def _call_beside(body, ex, nsteps, args, *, grid, in_specs, out_specs, out_shape, scratch_shapes, name):
    if ex is None:
        outs = pl.pallas_call(body, grid=grid, in_specs=in_specs, out_specs=out_specs, out_shape=out_shape,
                              scratch_shapes=scratch_shapes, compiler_params=_cparams(("arbitrary",)), name=name)(*args)
        return outs, None
    n_in, n_out, n_scr, n = len(in_specs), len(out_specs), len(scratch_shapes), len(ex.arrays)

    def wrapped(*refs):
        ins, refs = refs[:n_in], refs[n_in:]
        ex_ins, refs = refs[:n], refs[n:]
        outs, refs = refs[:n_out], refs[n_out:]
        ex_outs, refs = refs[:n], refs[n:]
        scr, sems = refs[:n_scr], refs[n_scr:]
        step = pl.program_id(0)

        @pl.when(step == 0)
        def _():
            for cp in ex.copies(ex_ins, ex_outs, *sems):
                cp.start()

        body(*ins, *outs, *scr)

        @pl.when(step == nsteps - 1)
        def _():
            for cp in ex.copies(ex_ins, ex_outs, *sems):
                cp.wait()

    res = pl.pallas_call(
        wrapped, grid=grid, in_specs=list(in_specs) + [_ANY] * n, out_specs=list(out_specs) + [_ANY] * n,
        out_shape=list(out_shape) + ex.out_shapes, scratch_shapes=list(scratch_shapes) + ex.scratch,
        compiler_params=_cparams(("arbitrary",)), name=name)(*args, *ex.arrays)
    return res[:n_out], res[n_out:]


def _adamw_math(w, g, m, v):
    m = ADAM_B1 * m + (1.0 - ADAM_B1) * g
    v = ADAM_B2 * v + (1.0 - ADAM_B2) * (g * g)
    m_hat = m / (1.0 - ADAM_B1 ** ADAM_STEP)
    v_hat = v / (1.0 - ADAM_B2 ** ADAM_STEP)
    delta = -ADAM_LR * (m_hat / (jnp.sqrt(v_hat) + ADAM_EPS) + ADAM_WD * w)
    return delta, m, v


def _sum_adamw(parts, w, m, v, tr, name):
    ns = len(parts)
    p, r, c = parts[0].shape
    tr = min(tr, r)
    assert r % tr == 0
    nt = r // tr

    def body(*refs):
        p_refs, (w_ref, m_ref, v_ref, g_ref, d_ref, nm_ref, nv_ref) = refs[:ns], refs[ns:]
        for s in range(ns):
            @pl.when(pl.program_id(0) == s)
            def _(p_ref=p_refs[s]):
                g = p_ref[0].astype(F32)
                for j in range(1, p):
                    g = g + p_ref[j].astype(F32)
                delta, nm, nv = _adamw_math(w_ref[...], g, m_ref[...], v_ref[...])
                g_ref[...] = g
                d_ref[...] = delta
                nm_ref[...] = nm
                nv_ref[...] = nv

    part_spec = lambda s: pl.BlockSpec((p, tr, c), lambda sec, i, s=s: (0, jnp.where(sec == s, i, 0), 0))
    row = pl.BlockSpec((tr, c), lambda sec, i: (sec * nt + i, 0))
    return pl.pallas_call(
        body, grid=(ns, nt), in_specs=[part_spec(s) for s in range(ns)] + [row, row, row],
        out_specs=[row] * 4, out_shape=[jax.ShapeDtypeStruct((ns * r, c), F32)] * 4,
        compiler_params=_cparams(("arbitrary", "arbitrary")), name=name)(*parts, w, m, v)


def _sum_parts(parts, name):
    p, r, c = parts.shape

    def body(p_ref, o_ref):
        g = p_ref[0]
        for j in range(1, p):
            g = g + p_ref[j]
        o_ref[...] = g

    return pl.pallas_call(body, out_shape=jax.ShapeDtypeStruct((r, c), F32), name=name)(parts)


def _rows_of(shape):
    n = 1
    for d in shape:
        n *= d
    return n, -(-n // 128)


def _pack(arrs):
    blocks = []
    for a in arrs:
        n, nr = _rows_of(a.shape)
        blocks.append(jnp.pad(a.reshape(-1).astype(F32), (0, nr * 128 - n)).reshape(nr, 128))
    rows = sum(b.shape[0] for b in blocks)
    if rows % 8:
        blocks.append(jnp.zeros((8 - rows % 8, 128), F32))
    return jnp.concatenate(blocks, axis=0)


def _unpack(buf, shapes):
    out, r0 = [], 0
    for s in shapes:
        n, nr = _rows_of(s)
        out.append(buf[r0:r0 + nr].reshape(-1)[:n].reshape(s))
        r0 += nr
    return out


def _block_diag(w):
    rows = [jnp.pad(w[i], ((0, 0), (i * C_BLOCK_DIM, C_WIDTH - (i + 1) * C_BLOCK_DIM))) for i in range(C_BLOCKS)]
    return jnp.concatenate(rows, axis=0)


def _diag_blocks(m):
    m4 = m.reshape(C_BLOCKS, C_BLOCK_DIM, C_BLOCKS, C_BLOCK_DIM)
    return jnp.stack([m4[i, :, i, :] for i in range(C_BLOCKS)])


def _gate_row(v):
    return jnp.pad(v.astype(F32), (B_HEADS, 128 - 2 * B_HEADS)).reshape(1, 128)


def _permute_w_in(w):
    pad = jnp.zeros(w.shape[:-1] + (D_IN_PAD - D_IN,), w.dtype)
    return jnp.concatenate([w[..., :3072], w[..., 3080:3592], w[..., 3072:3080], pad], axis=-1)


def _unpermute_w_in(w):
    return jnp.concatenate([w[..., :3072], w[..., COL_G:COL_G + 8], w[..., 3072:COL_G]], axis=-1)


_WEIGHTS = ['norm1_g', 'w_in', 'hgrn_lb_logits', 'hgrn_norm_g', 'gdn_conv_w', 'gdn_a_log', 'gdn_dt_bias', 'gdn_norm_g',
            'lru_conv_w', 'lru_conv_b', 'lru_w_a', 'lru_b_a', 'lru_w_x', 'lru_b_x', 'lru_lambda', 'w_out', 'norm2_g',
            'w_up', 'w_down', 'final_norm_g']
_BIG = ('w_in', 'w_out', 'w_up', 'w_down')
_SHARDED_SMALL = ('gdn_conv_w', 'lru_conv_w')


def _step(x, target, w, m, v):
    t = x.shape[0]
    mx, my, mc = _mesh_pos()
    me = 4 * mx + 2 * my + mc

    bf = lambda a: a.astype(BF16)

    def full_w_in(g):
        return _permute_w_in(jnp.moveaxis(g, 0, 1).reshape(D_MODEL, D_IN))

    conv_shapes = [w['gdn_conv_w'].shape, w['lru_conv_w'].shape]
    g_in, g_conv = _run_exchange(
        _Exchange([bf(w['w_in'][0]), _pack([w['gdn_conv_w'], w['lru_conv_w']])], ['slot', 'slot']), "gather_first")
    w_in = [full_w_in(g_in)]
    w_out, w_up, w_down = [], [], []
    gdn_cw, lru_cw = [], []
    for j in range(N_DEV):
        a, b = _unpack(g_conv[j], conv_shapes)
        gdn_cw.append(a)
        lru_cw.append(b)
    gdn_cw = jnp.concatenate(gdn_cw, axis=-1)
    lru_cw = jnp.concatenate(lru_cw, axis=-1)

    lbnd = _lb_fwd(w['hgrn_lb_logits'])
    row = lambda a: a.reshape(1, -1)

    def c_args(l):
        return (lru_cw[l], row(w['lru_conv_b'][l]), _block_diag(w['lru_w_a'][l]), row(w['lru_b_a'][l]),
                _block_diag(w['lru_w_x'][l]), row(w['lru_b_x'][l]), row(w['lru_lambda'][l]))

    saved = []
    xl = x
    h = _rms_fwd(x, w['norm1_g'][0], name="rms_fwd")
    for l in range(DEPTH):
        proj = _mm_rows(h, w_in[l], "nn", 512, "mm_proj")
        mix_a, st_a = _a_fwd(proj, lbnd[l], w['hgrn_norm_g'][l])
        alr, dtr = _gate_row(w['gdn_a_log'][l]), _gate_row(w['gdn_dt_bias'][l])
        qkv, gates, y3 = _b_pre_fwd(proj, gdn_cw[l], alr, dtr)
        nxt = [bf(w['w_in'][l + 1])] if l + 1 < DEPTH else []
        gather = _Exchange([bf(w['w_out'][l]), bf(w['w_up'][l]), bf(w['w_down'][l])] + nxt, ['rows', 'cols', 'rows'] + ['slot'] * len(nxt))
        (mix_b, st_b, *b_saved), got = _b_fwd(qkv, gates, proj, w['gdn_norm_g'][l], beside=gather)
        w_out.append(got[0])
        w_up.append(got[1])
        w_down.append(got[2])
        if nxt:
            w_in.append(full_w_in(got[3]))
        mix_c, hs = _c_fwd(proj, *c_args(l))
        mixed = [mix_a, mix_b, mix_c]
        x_mid, h2 = _mm_rows(mixed, w_out[l], "nn", 1024, "mm_out", residual=xl, epilogue="rms_fwd", norm=w['norm2_g'][l])
        act, up = _mm_rows(h2, w_up[l], "nn", 512, "mm_up", epilogue="relu2")
        saved.append(dict(x=xl, h=h, proj=proj, st_a=st_a, qkv=qkv, gates=gates, y3=y3, st_b=st_b, b_saved=b_saved, hs=hs, mixed=mixed,
                          x_mid=x_mid, h2=h2, up=up, act=act, alr=alr, dtr=dtr))
        if l + 1 < DEPTH:
            xl, h = _mm_rows(act, w_down[l], "nn", 512, "mm_down", residual=x_mid, epilogue="rms_fwd", norm=w['norm1_g'][l + 1])
        else:
            xl = _mm_rows(act, w_down[l], "nn", 512, "mm_down_last", residual=x_mid)
    loss, dx, dgf = _loss_head(xl, w['final_norm_g'], target)

    gs = {n: [None] * DEPTH for n in _WEIGHTS}
    recv = {n: [None] * DEPTH for n in _BIG}
    dw_in_above = None
    for l in reversed(range(DEPTH)):
        s = saved[l]
        dup = _mm_rows(dx, w_down[l], "nt", 512, "mm_dact", epilogue="drelu2", up=s['up'])
        dw_down = _mm_tn(s['act'], dx, 1024, "mm_dw_down").reshape(N_DEV, D_FF // N_DEV, D_MODEL)
        dx_mid, dg2 = _mm_rows(dup, w_up[l], "nt", 512, "mm_dh2", epilogue="rms_bwd", norm=(s['x_mid'], w['norm2_g'][l], dx))
        dw_up = _mm_tn(s['h2'], dup, 1024, "mm_dw_up", slab=D_FF // N_DEV)
        gs['norm2_g'][l] = dg2[0]
        dmixed = _mm_rows(dx_mid, w_out[l], "nt", 1024, "mm_dmixed")
        dw_out = _mm_tn(s['mixed'], dx_mid, 1024, "mm_dw_out").reshape(N_DEV, D_MODEL // N_DEV, D_MODEL)
        proj = s['proj']
        above = [dw_in_above] if dw_in_above is not None else []
        (dpa, dlb, dnga), got = _a_bwd(proj, lbnd[l], w['hgrn_norm_g'][l], s['st_a'], dmixed,
                                       beside=_Exchange([dw_out] + above, ['a2a'] * (1 + len(above))))
        recv['w_out'][l] = got[0]
        if above:
            recv['w_in'][l + 1] = got[1]
        gs['hgrn_lb_logits'][l] = dlb[0]
        gs['hgrn_norm_g'][l] = dnga[0, :A_DIM] + dnga[0, A_DIM:]
        (dqkv, dgates, dz, dngb), got = _b_bwd(s['qkv'], s['gates'], proj, w['gdn_norm_g'][l], s['st_b'], s['b_saved'], dmixed,
                                               beside=_Exchange([dw_up, dw_down], ['a2a', 'a2a']))
        recv['w_up'][l], recv['w_down'][l] = got
        dxb, dgi, dcwb, dal, ddt = _b_pre_bwd(proj, s['y3'], gdn_cw[l], s['alr'], s['dtr'], dqkv, dgates)
        gs['gdn_norm_g'][l] = dngb[0]
        gs['gdn_conv_w'][l] = dcwb
        gs['gdn_a_log'][l] = dal[0, B_HEADS:2 * B_HEADS]
        gs['gdn_dt_bias'][l] = ddt[0, B_HEADS:2 * B_HEADS]
        dxc_in, dyg, dcwc, dcb, dwa, dba, dwx, dbx, dlam = _c_bwd(proj, *c_args(l), s['hs'], dmixed)
        gs['lru_conv_w'][l] = dcwc
        gs['lru_conv_b'][l] = dcb[0]
        gs['lru_w_a'][l] = _diag_blocks(dwa)
        gs['lru_b_a'][l] = dba[0]
        gs['lru_w_x'][l] = _diag_blocks(dwx)
        gs['lru_b_x'][l] = dbx[0]
        gs['lru_lambda'][l] = dlam[0]
        dproj = [dpa, dxb, dz, dxc_in, dyg, dgi]
        dw_in = _unpermute_w_in(_mm_tn(s['h'], dproj, 1024, "mm_dw_in"))
        dw_in_above = jnp.moveaxis(dw_in.reshape(D_MODEL, N_DEV, D_IN // N_DEV), 1, 0)
        norm = (s['x'], w['norm1_g'][l], dx_mid)
        if l > 0:
            dx, dg1 = _mm_rows(dproj, w_in[l], "nt", 512, "mm_dh", epilogue="rms_bwd", norm=norm)
        else:
            (dx, dg1), got = _mm_rows(dproj, w_in[l], "nt", 512, "mm_dh_last", epilogue="rms_bwd", norm=norm,
                                      beside=_Exchange([dw_in_above], ['a2a']))
            recv['w_in'][0] = got[0]
        gs['norm1_g'][l] = dg1[0]
    grad_x = dx
    part = {n: jnp.stack(gs[n]) for n in _WEIGHTS if n != 'final_norm_g' and n not in _BIG}
    part['final_norm_g'] = dgf[0]
    part['hgrn_lb_logits'] = _lb_bwd(w['hgrn_lb_logits'], part['hgrn_lb_logits'])

    small = [n for n in _WEIGHTS if n not in _BIG]
    packed = _pack([part[n] for n in small] + [loss])
    all_small, = _run_exchange(_Exchange([packed], ['slot']), "gather_small")

    grads, deltas, new_m, new_v = {}, {}, {}, {}
    for n in _BIG:
        shp = w[n].shape
        r2 = lambda a: a.reshape(-1, shp[-1])
        g, d, nm, nv = _sum_adamw(recv[n], r2(w[n]), r2(m[n]), r2(v[n]), 256, "adamw_" + n)
        grads[n], deltas[n], new_m[n], new_v[n] = (a.reshape(shp) for a in (g, d, nm, nv))

    total = _sum_parts(all_small, "sum_small")
    summed = _unpack(total, [part[n].shape for n in small] + [(1, 1)])
    loss_total = summed[-1].reshape(())
    gsmall = dict(zip(small, summed[:-1]))
    for n in _SHARDED_SMALL:
        width = w[n].shape[-1]
        gsmall[n] = lax.dynamic_slice_in_dim(gsmall[n], me * width, width, axis=2)
    pk = lambda d: _pack([d[n] for n in small])
    _, d, nm, nv = _sum_adamw([pk(gsmall)[None]], pk(w), pk(m), pk(v), 4096, "adamw_small")
    shapes = [w[n].shape for n in small]
    for n, dd, mm, vv in zip(small, _unpack(d, shapes), _unpack(nm, shapes), _unpack(nv, shapes)):
        grads[n], deltas[n], new_m[n], new_v[n] = gsmall[n], dd, mm, vv
    return loss_total, grad_x, grads, deltas, new_m, new_v


def kernel(x, norm1_g, w_in, hgrn_lb_logits, hgrn_norm_g, gdn_conv_w, gdn_a_log, gdn_dt_bias, gdn_norm_g, lru_conv_w, lru_conv_b, lru_w_a, lru_b_a, lru_w_x, lru_b_x, lru_lambda, w_out, norm2_g, w_up, w_down, final_norm_g, loss_target, m_norm1_g, m_w_in, m_hgrn_lb_logits, m_hgrn_norm_g, m_gdn_conv_w, m_gdn_a_log, m_gdn_dt_bias, m_gdn_norm_g, m_lru_conv_w, m_lru_conv_b, m_lru_w_a, m_lru_b_a, m_lru_w_x, m_lru_b_x, m_lru_lambda, m_w_out, m_norm2_g, m_w_up, m_w_down, m_final_norm_g, v_norm1_g, v_w_in, v_hgrn_lb_logits, v_hgrn_norm_g, v_gdn_conv_w, v_gdn_a_log, v_gdn_dt_bias, v_gdn_norm_g, v_lru_conv_w, v_lru_conv_b, v_lru_w_a, v_lru_b_a, v_lru_w_x, v_lru_b_x, v_lru_lambda, v_w_out, v_norm2_g, v_w_up, v_w_down, v_final_norm_g):
    w = dict(zip(_WEIGHTS, (norm1_g, w_in, hgrn_lb_logits, hgrn_norm_g, gdn_conv_w, gdn_a_log, gdn_dt_bias, gdn_norm_g, lru_conv_w, lru_conv_b, lru_w_a, lru_b_a, lru_w_x, lru_b_x, lru_lambda, w_out, norm2_g, w_up, w_down, final_norm_g)))
    m = dict(zip(_WEIGHTS, (m_norm1_g, m_w_in, m_hgrn_lb_logits, m_hgrn_norm_g, m_gdn_conv_w, m_gdn_a_log, m_gdn_dt_bias, m_gdn_norm_g, m_lru_conv_w, m_lru_conv_b, m_lru_w_a, m_lru_b_a, m_lru_w_x, m_lru_b_x, m_lru_lambda, m_w_out, m_norm2_g, m_w_up, m_w_down, m_final_norm_g)))
    v = dict(zip(_WEIGHTS, (v_norm1_g, v_w_in, v_hgrn_lb_logits, v_hgrn_norm_g, v_gdn_conv_w, v_gdn_a_log, v_gdn_dt_bias, v_gdn_norm_g, v_lru_conv_w, v_lru_conv_b, v_lru_w_a, v_lru_b_a, v_lru_w_x, v_lru_b_x, v_lru_lambda, v_w_out, v_norm2_g, v_w_up, v_w_down, v_final_norm_g)))
    loss, grad_x, grads, deltas, new_m, new_v = _step(x.reshape(x.shape[1:]), loss_target.reshape(x.shape[1:]), w, m, v)
    return (loss, grad_x[None], *[grads[n] for n in _WEIGHTS], *[deltas[n] for n in _WEIGHTS],
            *[new_m[n] for n in _WEIGHTS], *[new_v[n] for n in _WEIGHTS])
```

```python
import jax
import jax.numpy as jnp
from jax import lax
from jax.experimental import pallas as pl
from jax.experimental.pallas import tpu as pltpu

F32 = jnp.float32
BF16 = jnp.bfloat16
MESH = pl.DeviceIdType.MESH

N_DEV = 8
D_MODEL = 1024
DEPTH = 4
A_HEADS, A_DIM, A_WIDTH = 4, 64, 256
B_HEADS, B_DIM, B_WIDTH = 4, 128, 512
C_WIDTH, C_BLOCKS, C_BLOCK_DIM = 256, 4, 64
D_IN = 3592
D_IN_PAD = 3840
COL_A, COL_B, COL_C, COL_G = 0, 1024, 3072, 3584
D_FF = 4096
CONV_K = 4
CHUNK = 64
SUB = 16
RG_C = 8.0
EPS = 1e-6
TINY = 1e-30
EXP_CLAMP = 80.0
GDN_SCALE = B_DIM ** -0.5
ADAM_LR, ADAM_B1, ADAM_B2, ADAM_EPS, ADAM_WD, ADAM_STEP = 0.001, 0.9, 0.999, 1e-08, 0.01, 10
VMEM_LIMIT = 56 * 1024 * 1024


def _cparams(sem=None):
    return pltpu.CompilerParams(dimension_semantics=sem, vmem_limit_bytes=VMEM_LIMIT)


_DIMS = {"nn": (((1,), (0,)), ((), ())), "nt": (((1,), (1,)), ((), ())), "tn": (((0,), (0,)), ((), ()))}


def _split_bf16(x):
    hi = x.astype(BF16)
    return hi, (x - hi.astype(F32)).astype(BF16)


def _dot(a, b, mode="nn", hi=False):
    if not hi:
        return lax.dot_general(a.astype(BF16), b.astype(BF16), _DIMS[mode], preferred_element_type=F32)
    ah, al = _split_bf16(a.astype(F32))
    bh, bl = _split_bf16(b.astype(F32))
    ka = 0 if mode == "tn" else 1
    kb = 1 if mode == "nt" else 0
    return lax.dot_general(jnp.concatenate([ah, ah, al], axis=ka), jnp.concatenate([bh, bl, bh], axis=kb),
                           _DIMS[mode], preferred_element_type=F32)


def _dot_exact_lhs(lhs, x, mode="nn"):
    l_bf16 = lhs.astype(BF16)
    x1 = x.astype(BF16)
    r1 = x - x1.astype(F32)
    x2 = r1.astype(BF16)
    x3 = (r1 - x2.astype(F32)).astype(BF16)
    ka = 0 if mode == "tn" else 1
    return lax.dot_general(jnp.concatenate([l_bf16] * 3, axis=ka), jnp.concatenate([x1, x2, x3], axis=0),
                           _DIMS[mode], preferred_element_type=F32)


def _iota2(n, m):
    return lax.broadcasted_iota(jnp.int32, (n, m), 0), lax.broadcasted_iota(jnp.int32, (n, m), 1)


def _tril(n, strict=False):
    r, c = _iota2(n, n)
    return (r > c) if strict else (r >= c)


def _sigmoid(x):
    return 1.0 / (1.0 + jnp.exp(-x))


def _softplus(x):
    return jnp.maximum(x, 0.0) + jnp.log(1.0 + jnp.exp(-jnp.abs(x)))


def _neg_expm1(z):
    series = -z * (1.0 + z * (0.5 + z * (1.0 / 6.0)))
    return jnp.where(z > -1e-2, series, 1.0 - jnp.exp(z))


def _gelu_tanh(x):
    c = 0.7978845608028654
    u = c * (x + 0.044715 * x * x * x)
    t = jnp.tanh(u)
    g = 0.5 * x * (1.0 + t)
    dg = 0.5 * (1.0 + t) + 0.5 * x * (1.0 - t * t) * c * (1.0 + 3.0 * 0.044715 * x * x)
    return g, dg


def _shift_rows(cur, halo, s, down=True):
    n = cur.shape[0]
    ridx = lax.broadcasted_iota(jnp.int32, (8, cur.shape[1]), 0)
    if down:
        main = pltpu.roll(cur, s, 0)
        fix = jnp.where(ridx < s, pltpu.roll(halo, s, 0), main[0:8])
        return jnp.concatenate([fix, main[8:]], axis=0)
    main = pltpu.roll(cur, n - s, 0)
    fix = jnp.where(ridx >= 8 - s, pltpu.roll(halo, 8 - s, 0), main[n - 8:n])
    return jnp.concatenate([main[:n - 8], fix], axis=0)


def _later_rows(dy, nxt8):
    return [_shift_rows(dy, nxt8, 3 - j, down=False) for j in range(3)] + [dy]


def _conv_bwd_rows(dy, nxt8, w):
    return sum(d * w[j:j + 1] for j, d in enumerate(_later_rows(dy, nxt8)))


def _tile_scan(a, b, reverse=False):
    n = a.shape[0]
    r = lax.broadcasted_iota(jnp.int32, a.shape, 0) % 8
    for s in (1, 2, 4):
        keep = (r < 8 - s) if reverse else (r >= s)
        shift = n - s if reverse else s
        a_sh = jnp.where(keep, pltpu.roll(a, shift, 0), 1.0)
        b_sh = jnp.where(keep, pltpu.roll(b, shift, 0), 0.0)
        b = b + a * b_sh
        a = a * a_sh
    return a, b


def _conv_fwd(cur, prev8, w):
    y = cur * w[3:4]
    for j in range(3):
        y = y + _shift_rows(cur, prev8, 3 - j, down=True) * w[j:j + 1]
    return y


def _mm_rows(a, w, mode, tm, name, residual=None, epilogue=None, up=None, norm=None, beside=None):
    parts = list(a) if isinstance(a, (list, tuple)) else [a]
    widths = [p.shape[1] for p in parts]
    t = parts[0].shape[0]
    n = w.shape[1] if mode == "nn" else w.shape[0]
    tm = min(tm, t)
    assert t % tm == 0 and all(wd % 128 == 0 for wd in widths)

    def body(*refs):
        a_refs, w_ref, rest = refs[:len(parts)], refs[len(parts)], refs[len(parts) + 1:]
        if epilogue == "rms_bwd":
            @pl.when(pl.program_id(0) == 0)
            def _():
                rest[4][...] = jnp.zeros_like(rest[4])

        for rs in ((slice(0, tm // 2), slice(tm // 2, tm)) if tm >= 512 else (slice(0, tm),)):
            y, off = None, 0
            for a_ref, width in zip(a_refs, widths):
                wk = w_ref[off:off + width, :] if mode == "nn" else w_ref[:, off:off + width]
                d = _dot(a_ref[rs, :], wk, mode)
                y = d if y is None else y + d
                off += width
            if residual is not None:
                y = y + rest[0][rs, :]
            if epilogue == "relu2":
                r = jnp.maximum(y, 0.0)
                refs[-2][rs, :] = (r * r).astype(BF16)
                refs[-1][rs, :] = y.astype(BF16)
            elif epilogue == "drelu2":
                refs[-1][rs, :] = (y * 2.0 * jnp.maximum(rest[0][rs, :].astype(F32), 0.0)).astype(BF16)
            elif epilogue == "rms_fwd":
                rinv = lax.rsqrt(jnp.mean(y * y, axis=-1, keepdims=True) + EPS)
                refs[-2][rs, :] = y
                refs[-1][rs, :] = (y * rinv * refs[-3][...]).astype(BF16)
            elif epilogue == "rms_bwd":
                x_ref, g_ref, dres_ref, dx_ref, dg_ref = rest
                xv = x_ref[rs, :]
                rinv = lax.rsqrt(jnp.mean(xv * xv, axis=-1, keepdims=True) + EPS)
                xhat = xv * rinv
                dxh = y * g_ref[...]
                dx_ref[rs, :] = dres_ref[rs, :] + rinv * (dxh - xhat * jnp.mean(dxh * xhat, axis=-1, keepdims=True))
                dg_ref[...] += jnp.sum(y * xhat, axis=0, keepdims=True)
            else:
                refs[-1][rs, :] = y

    rows = lambda width: pl.BlockSpec((tm, width), lambda i: (i, 0))
    vec = pl.BlockSpec((1, n), lambda i: (0, 0))
    ins, specs = parts + [w], [rows(wd) for wd in widths] + [pl.BlockSpec(w.shape, lambda i: (0, 0))]
    if residual is not None:
        ins.append(residual)
        specs.append(rows(n))
    if epilogue == "drelu2":
        ins.append(up)
        specs.append(rows(n))
    if epilogue == "rms_fwd":
        ins.append(norm.reshape(1, n))
        specs.append(vec)
        out_specs, out_shape = [rows(n), rows(n)], [jax.ShapeDtypeStruct((t, n), F32), jax.ShapeDtypeStruct((t, n), BF16)]
    elif epilogue == "rms_bwd":
        ins += [norm[0], norm[1].reshape(1, n), norm[2]]
        specs += [rows(n), vec, rows(n)]
        out_specs, out_shape = [rows(n), vec], [jax.ShapeDtypeStruct((t, n), F32), jax.ShapeDtypeStruct((1, n), F32)]
    elif epilogue == "relu2":
        out_specs, out_shape = [rows(n), rows(n)], [jax.ShapeDtypeStruct((t, n), BF16)] * 2
    else:
        out_specs, out_shape = rows(n), jax.ShapeDtypeStruct((t, n), BF16 if epilogue == "drelu2" else F32)
    if beside is not None:
        assert epilogue == "rms_bwd"
        return _call_beside(body, beside, t // tm, ins, grid=(t // tm,), in_specs=specs, out_specs=out_specs,
                            out_shape=out_shape, scratch_shapes=[], name=name)
    return pl.pallas_call(body, grid=(t // tm,), in_specs=specs, out_specs=out_specs, out_shape=out_shape,
                          compiler_params=_cparams(("arbitrary" if epilogue == "rms_bwd" else "parallel",)), name=name)(*ins)


MM_TN_TILE = 1024


def _mm_tn(a, b, tk, name, slab=None):
    a_parts = list(a) if isinstance(a, (list, tuple)) else [a]
    b_parts = list(b) if isinstance(b, (list, tuple)) else [b]
    wa, wb = [p.shape[1] for p in a_parts], [p.shape[1] for p in b_parts]
    t, m, n = a_parts[0].shape[0], sum(wa), sum(wb)
    tk = min(tk, t)
    assert t % tk == 0 and all(x % 128 == 0 for x in wa + wb)
    nk = t // tk

    def body(*refs):
        a_refs, b_refs = refs[:len(wa)], refs[len(wa):len(wa) + len(wb)]
        o_ref, acc = refs[-2], refs[-1]
        kk = pl.program_id(0)

        @pl.when(kk == 0)
        def _():
            acc[...] = jnp.zeros_like(acc)

        ro = 0
        for a_ref, width_a in zip(a_refs, wa):
            for r0 in range(0, width_a, MM_TN_TILE):
                rw = min(MM_TN_TILE, width_a - r0)
                av = a_ref[:, r0:r0 + rw]
                co = 0
                for b_ref, width_b in zip(b_refs, wb):
                    for c0 in range(0, width_b, MM_TN_TILE):
                        cw = min(MM_TN_TILE, width_b - c0)
                        acc[ro + r0:ro + r0 + rw, co + c0:co + c0 + cw] += _dot(av, b_ref[:, c0:c0 + cw], "tn")
                    co += width_b
            ro += width_a

        @pl.when(kk == nk - 1)
        def _():
            if slab is None:
                o_ref[...] = acc[...].astype(BF16)
            else:
                for s in range(n // slab):
                    o_ref[s] = acc[:, s * slab:(s + 1) * slab].astype(BF16)

    if slab is None:
        out_spec, out_shape = pl.BlockSpec((m, n), lambda kk: (0, 0)), jax.ShapeDtypeStruct((m, n), BF16)
    else:
        out_spec, out_shape = pl.BlockSpec((n // slab, m, slab), lambda kk: (0, 0, 0)), jax.ShapeDtypeStruct((n // slab, m, slab), BF16)
    return pl.pallas_call(
        body, grid=(nk,),
        in_specs=[pl.BlockSpec((tk, x), lambda kk: (kk, 0)) for x in wa + wb],
        out_specs=out_spec, out_shape=out_shape, scratch_shapes=[pltpu.VMEM((m, n), F32)],
        compiler_params=_cparams(("arbitrary",)), name=name)(*a_parts, *b_parts)


def _rms_fwd(x, g, tb=512, name="rms_fwd"):
    t, d = x.shape

    def body(x_ref, g_ref, h_ref):
        xv = x_ref[...]
        rinv = lax.rsqrt(jnp.mean(xv * xv, axis=-1, keepdims=True) + EPS)
        h_ref[...] = (xv * rinv * g_ref[...]).astype(BF16)

    return pl.pallas_call(
        body, grid=(t // tb,), in_specs=[pl.BlockSpec((tb, d), lambda i: (i, 0)), pl.BlockSpec((1, d), lambda i: (0, 0))],
        out_specs=pl.BlockSpec((tb, d), lambda i: (i, 0)), out_shape=jax.ShapeDtypeStruct((t, d), BF16),
        compiler_params=_cparams(("parallel",)), name=name)(x, g.reshape(1, d))


def _loss_head(x, g, target, tb=512):
    t, d = x.shape

    def body(x_ref, g_ref, t_ref, loss_ref, dx_ref, dg_ref):
        @pl.when(pl.program_id(0) == 0)
        def _():
            dg_ref[...] = jnp.zeros_like(dg_ref)
            loss_ref[...] = jnp.zeros_like(loss_ref)

        xv = x_ref[...]
        rinv = lax.rsqrt(jnp.mean(xv * xv, axis=-1, keepdims=True) + EPS)
        xhat = xv * rinv
        err = xhat * g_ref[...] - t_ref[...]
        loss_ref[...] += 0.5 * jnp.sum(jnp.mean(err * err, axis=-1, keepdims=True), axis=0, keepdims=True)
        dy = err * (1.0 / d)
        dxh = dy * g_ref[...]
        dx_ref[...] = rinv * (dxh - xhat * jnp.mean(dxh * xhat, axis=-1, keepdims=True))
        dg_ref[...] += jnp.sum(dy * xhat, axis=0, keepdims=True)

    row = pl.BlockSpec((tb, d), lambda i: (i, 0))
    vec = pl.BlockSpec((1, d), lambda i: (0, 0))
    one = pl.BlockSpec((1, 1), lambda i: (0, 0))
    return pl.pallas_call(
        body, grid=(t // tb,), in_specs=[row, vec, row], out_specs=[one, row, vec],
        out_shape=[jax.ShapeDtypeStruct((1, 1), F32), jax.ShapeDtypeStruct((t, d), F32), jax.ShapeDtypeStruct((1, d), F32)],
        compiler_params=_cparams(("arbitrary",)), name="loss_head")(x, g.reshape(1, d), target)


def _lb_fwd(logits):
    def body(l_ref, o_ref):
        lg = l_ref[...]
        e = jnp.exp(lg - jnp.max(lg, axis=0, keepdims=True))
        p = e / jnp.sum(e, axis=0, keepdims=True)
        c = jnp.zeros_like(p[0:1])
        rows = [c]
        for l in range(1, DEPTH):
            c = c + p[l:l + 1]
            rows.append(c)
        o_ref[...] = jnp.minimum(jnp.maximum(jnp.concatenate(rows, axis=0), 0.0), 1.0 - EPS)

    return pl.pallas_call(body, out_shape=jax.ShapeDtypeStruct(logits.shape, F32), name="lb_fwd")(logits)


def _lb_bwd(logits, dlb):
    def body(l_ref, d_ref, o_ref):
        lg = l_ref[...]
        e = jnp.exp(lg - jnp.max(lg, axis=0, keepdims=True))
        p = e / jnp.sum(e, axis=0, keepdims=True)
        hi = 1.0 - EPS
        c = jnp.zeros_like(p[0:1])
        dc = []
        for l in range(1, DEPTH):
            c = c + p[l:l + 1]
            gl = jnp.where(c < 0.0, 0.0, jnp.where(c == 0.0, 0.5, 1.0)) * jnp.where(c > hi, 0.0, jnp.where(c == hi, 0.5, 1.0))
            dc.append(d_ref[l:l + 1, :] * gl)
        dp = [jnp.zeros_like(c)]
        for j in range(1, DEPTH):
            s = dc[j - 1]
            for l in range(j + 1, DEPTH):
                s = s + dc[l - 1]
            dp.append(s)
        dpm = jnp.concatenate(dp, axis=0)
        o_ref[...] = p * (dpm - jnp.sum(p * dpm, axis=0, keepdims=True))

    return pl.pallas_call(body, out_shape=jax.ShapeDtypeStruct(logits.shape, F32), name="lb_bwd")(logits, dlb)


def _a_gates(qi, fi, lbh):
    sq = _sigmoid(qi)
    q = qi * sq
    e = jnp.exp(-jnp.abs(fi))
    rec = 1.0 / (1.0 + e)
    pos = fi >= 0.0
    sg = jnp.where(pos, rec, e * rec)
    sgn = jnp.where(pos, e * rec, rec)
    f = lbh + (1.0 - lbh) * sg
    logf = jnp.log(jnp.maximum(f, TINY))
    k = (1.0 - lbh) * sgn
    return q, sq, sg, sgn, f, logf, k


def _headnorm_fwd(o, g, gate_in):
    rinv = lax.rsqrt(jnp.mean(o * o, axis=-1, keepdims=True) + EPS)
    sg = _sigmoid(gate_in)
    return o * rinv * g * (gate_in * sg)


def _headnorm_bwd(dout, o, g, gate_in):
    rinv = lax.rsqrt(jnp.mean(o * o, axis=-1, keepdims=True) + EPS)
    xhat = o * rinv
    sg = _sigmoid(gate_in)
    silu = gate_in * sg
    dy = dout * silu
    dgate = dout * xhat * g * (sg * (1.0 + gate_in * (1.0 - sg)))
    dxh = dy * g
    do = rinv * (dxh - xhat * jnp.mean(dxh * xhat, axis=-1, keepdims=True))
    return do, dgate, jnp.sum(dy * xhat, axis=0, keepdims=True)


A_PAIRS, A_PAIR_W = A_HEADS // 2, 2 * A_DIM


def _lo_half(shape):
    return lax.broadcasted_iota(jnp.int32, shape, len(shape) - 1) < A_DIM


def _pair_blockdiag(x):
    lo = _lo_half(x.shape)
    return jnp.concatenate([jnp.where(lo, x, 0.0), jnp.where(lo, 0.0, x)], axis=0)


def _pair_fold(m):
    n = m.shape[0] // 2
    return jnp.where(_lo_half((n, A_PAIR_W)), m[:n], m[n:])


def _pair_norm_stats(o):
    lo = _lo_half(o.shape)
    sq = o * o
    s0 = jnp.sum(jnp.where(lo, sq, 0.0), axis=-1, keepdims=True)
    s1 = jnp.sum(sq, axis=-1, keepdims=True) - s0
    return jnp.where(lo, lax.rsqrt(s0 * (1.0 / A_DIM) + EPS), lax.rsqrt(s1 * (1.0 / A_DIM) + EPS))


def _pair_mean(x):
    lo = _lo_half(x.shape)
    s0 = jnp.sum(jnp.where(lo, x, 0.0), axis=-1, keepdims=True)
    s1 = jnp.sum(x, axis=-1, keepdims=True) - s0
    return jnp.where(lo, s0, s1) * (1.0 / A_DIM)


def _a_pair_chunk(qi, fi, v, lb2, s_bd, ltri, causal2):
    q, sq, sg, sgn, f, logf, k = _a_gates(qi, fi, lb2)
    cum = _dot_exact_lhs(ltri, logf)
    cl = cum[CHUNK - 1:CHUNK]
    ecum, ekd, cd = jnp.exp(cum), jnp.exp(cl - cum), jnp.exp(cl)
    qd, kd = q * ecum, k * ekd
    subs, rows = [], []
    for i in range(CHUNK // SUB):
        lo = i * SUB
        r = cum[lo - 1:lo] if i > 0 else jnp.zeros_like(cl)
        eq = jnp.exp(cum[lo:lo + SUB] - r)
        ek = jnp.exp(jnp.minimum(r - cum, EXP_CLAMP))
        qt = q[lo:lo + SUB] * eq
        kt_bd = _pair_blockdiag(k * ek)
        rows.append(_dot(qt, kt_bd, "nt", hi=True))
        subs.append((qt, eq, kt_bd, ek))
    attn = jnp.where(causal2, jnp.concatenate(rows, axis=0), 0.0)
    v_bd = _pair_blockdiag(v)
    o = _dot(qd, s_bd) + _dot(attn, v_bd)
    return dict(q=q, sq=sq, sg=sg, sgn=sgn, f=f, k=k, cum=cum, cl=cl, ecum=ecum, ekd=ekd, cd=cd, qd=qd, kd=kd,
                subs=subs, attn=attn, v_bd=v_bd, o=o)


def _a_fwd(proj, lb, norm_g, tb=256):
    t = proj.shape[0]
    nch = tb // CHUNK

    def body(q_ref, f_ref, i_ref, g_ref, lb_ref, ng_ref, out_ref, st_ref, s_scr):
        @pl.when(pl.program_id(0) == 0)
        def _():
            s_scr[...] = jnp.zeros_like(s_scr)

        ltri = _tril(CHUNK).astype(F32)
        r, c = _iota2(CHUNK, A_PAIR_W)
        causal2 = r >= c % CHUNK
        rb, cb = _iota2(A_PAIR_W, A_PAIR_W)
        diag_blocks = (rb < A_DIM) == (cb < A_DIM)

        def chunk(c, carry):
            rows = pl.ds(pl.multiple_of(c * CHUNK, CHUNK), CHUNK)
            ps = range(A_PAIRS)
            cols = [slice(p * A_PAIR_W, (p + 1) * A_PAIR_W) for p in ps]
            s0 = [s_scr[p] for p in ps]
            for p in ps:
                st_ref[c, p] = s0[p]
            v = [i_ref[rows, cols[p]] for p in ps]
            ch = [_a_pair_chunk(q_ref[rows, cols[p]], f_ref[rows, cols[p]], v[p], lb_ref[:, cols[p]], s0[p], ltri, causal2)
                  for p in ps]
            for p in ps:
                upd = jnp.where(diag_blocks, _dot(ch[p]["kd"], v[p], "tn"), 0.0)
                s_scr[p] = s0[p] * ch[p]["cd"].T + upd
            outs = []
            for p in ps:
                gi = g_ref[rows, cols[p]]
                outs.append(ch[p]["o"] * _pair_norm_stats(ch[p]["o"]) * ng_ref[...] * (gi * _sigmoid(gi)))
            out_ref[rows, :] = jnp.concatenate(outs, axis=1).astype(BF16)
            return carry

        lax.fori_loop(0, nch, chunk, 0, unroll=4)

    colblk = lambda j: pl.BlockSpec((tb, A_WIDTH), lambda i, j=j: (i, j))
    return pl.pallas_call(
        body, grid=(t // tb,),
        in_specs=[colblk(0), colblk(1), colblk(2), colblk(3), pl.BlockSpec((1, A_WIDTH), lambda i: (0, 0)),
                  pl.BlockSpec((1, A_PAIR_W), lambda i: (0, 0))],
        out_specs=[pl.BlockSpec((tb, A_WIDTH), lambda i: (i, 0)),
                   pl.BlockSpec((nch, A_PAIRS, A_PAIR_W, A_PAIR_W), lambda i: (i, 0, 0, 0))],
        out_shape=[jax.ShapeDtypeStruct((t, A_WIDTH), BF16),
                   jax.ShapeDtypeStruct((t // CHUNK, A_PAIRS, A_PAIR_W, A_PAIR_W), F32)],
        scratch_shapes=[pltpu.VMEM((A_PAIRS, A_PAIR_W, A_PAIR_W), F32)],
        compiler_params=_cparams(("arbitrary",)), name="hgrn_fwd")(
            proj, proj, proj, proj, lb.reshape(1, A_WIDTH), jnp.tile(norm_g.reshape(1, A_DIM), (1, 2)))


def _a_bwd(proj, lb, norm_g, states, dmixed, beside=None, tb=256):
    t = proj.shape[0]
    nch = tb // CHUNK
    nb = t // tb

    def body(q_ref, f_ref, i_ref, g_ref, lb_ref, ng_ref, st_ref, dm_ref, dp_ref, dlb_ref, dng_ref, ds_scr):
        @pl.when(pl.program_id(0) == 0)
        def _():
            ds_scr[...] = jnp.zeros_like(ds_scr)
            dlb_ref[...] = jnp.zeros_like(dlb_ref)
            dng_ref[...] = jnp.zeros_like(dng_ref)

        ltri = _tril(CHUNK).astype(F32)
        r, c = _iota2(CHUNK, A_PAIR_W)
        causal2 = r >= c % CHUNK
        rb, cb = _iota2(A_PAIR_W, A_PAIR_W)
        diag_blocks = (rb < A_DIM) == (cb < A_DIM)
        ones8 = jnp.ones((8, A_PAIR_W), F32)

        def chunk(cc, carry):
            c = nch - 1 - cc
            rows = pl.ds(pl.multiple_of(c * CHUNK, CHUNK), CHUNK)
            ps = range(A_PAIRS)
            cols = [slice(p * A_PAIR_W, (p + 1) * A_PAIR_W) for p in ps]
            qi = [q_ref[rows, cols[p]] for p in ps]
            gi = [g_ref[rows, cols[p]] for p in ps]
            v = [i_ref[rows, cols[p]] for p in ps]
            lb2 = [lb_ref[:, cols[p]] for p in ps]
            s0 = [st_ref[c, p] for p in ps]
            ds = [ds_scr[p] for p in ps]
            ch = [_a_pair_chunk(qi[p], f_ref[rows, cols[p]], v[p], lb2[p], s0[p], ltri, causal2) for p in ps]
            o = [ch[p]["o"] for p in ps]
            rinv = [_pair_norm_stats(o[p]) for p in ps]
            xhat = [o[p] * rinv[p] for p in ps]
            sgg = [_sigmoid(gi[p]) for p in ps]
            dout = [dm_ref[rows, cols[p]].astype(F32) for p in ps]
            dy = [dout[p] * (gi[p] * sgg[p]) for p in ps]
            dgi = [dout[p] * xhat[p] * ng_ref[...] * (sgg[p] * (1.0 + gi[p] * (1.0 - sgg[p]))) for p in ps]
            dxh = [dy[p] * ng_ref[...] for p in ps]
            do = [rinv[p] * (dxh[p] - xhat[p] * _pair_mean(dxh[p] * xhat[p])) for p in ps]
            dng = sum(jnp.sum(dy[p] * xhat[p], axis=0, keepdims=True) for p in ps)
            dqd = [_dot(do[p], s0[p], "nt") for p in ps]
            dattn = [jnp.where(causal2, _dot(do[p], ch[p]["v_bd"], "nt"), 0.0) for p in ps]
            dv = [_pair_fold(_dot(ch[p]["attn"], do[p], "tn")) + _dot(ch[p]["kd"], ds[p]) for p in ps]
            dkd = [_dot(v[p], ds[p], "nt") for p in ps]
            dcd = [_dot(ones8, s0[p] * ds[p], "nt", hi=True)[0:1] for p in ps]
            for p in ps:
                ds_scr[p] = jnp.where(diag_blocks, _dot(ch[p]["qd"], do[p], "tn"), 0.0) + ds[p] * ch[p]["cd"].T
            dq_i, dk_i = [], []
            for p in ps:
                dq_rows, dk = [], None
                for i, (qt, eq, kt_bd, ek) in enumerate(ch[p]["subs"]):
                    da = dattn[p][i * SUB:(i + 1) * SUB]
                    dq_rows.append(_dot(da, kt_bd, "nn", hi=True) * eq)
                    d = _pair_fold(_dot(da, qt, "tn", hi=True)) * ek
                    dk = d if dk is None else dk + d
                dq_i.append(jnp.concatenate(dq_rows, axis=0))
                dk_i.append(dk)
            dq = [dqd[p] * ch[p]["ecum"] + dq_i[p] for p in ps]
            dk = [dkd[p] * ch[p]["ekd"] + dk_i[p] for p in ps]
            dkk = [dkd[p] * ch[p]["kd"] for p in ps]
            dcum = [dqd[p] * ch[p]["qd"] - dkk[p] + ch[p]["q"] * dq_i[p] - ch[p]["k"] * dk_i[p] for p in ps]
            dcl = [jnp.sum(dkk[p], axis=0, keepdims=True) + dcd[p] * ch[p]["cd"] for p in ps]
            dlogf = [_dot_exact_lhs(ltri, dcum[p], "tn") + dcl[p] for p in ps]
            dfv = [jnp.where(ch[p]["f"] > TINY, dlogf[p] / ch[p]["f"], 0.0) for p in ps]
            dfi = [dfv[p] * (1.0 - lb2[p]) * ch[p]["sg"] * (1.0 - ch[p]["sg"])
                   - dk[p] * (1.0 - lb2[p]) * ch[p]["sgn"] * (1.0 - ch[p]["sgn"]) for p in ps]
            dlbs = [jnp.sum(dfv[p] * (1.0 - ch[p]["sg"]) - dk[p] * ch[p]["sgn"], axis=0, keepdims=True) for p in ps]
            dqs = [dq[p] * (ch[p]["sq"] * (1.0 + qi[p] * (1.0 - ch[p]["sq"]))) for p in ps]
            dp_ref[rows, :] = jnp.concatenate(dqs + dfi + dv + dgi, axis=1).astype(BF16)
            dlb_ref[...] += jnp.concatenate(dlbs, axis=1)
            dng_ref[...] += dng
            return carry

        lax.fori_loop(0, nch, chunk, 0, unroll=4)

    colblk = lambda j: pl.BlockSpec((tb, A_WIDTH), lambda i, j=j: (nb - 1 - i, j))
    vec = lambda n: pl.BlockSpec((1, n), lambda i: (0, 0))
    return _call_beside(
        body, beside, nb,
        (proj, proj, proj, proj, lb.reshape(1, A_WIDTH), jnp.tile(norm_g.reshape(1, A_DIM), (1, 2)), states, dmixed), grid=(nb,),
        in_specs=[colblk(0), colblk(1), colblk(2), colblk(3), vec(A_WIDTH), vec(A_PAIR_W),
                  pl.BlockSpec((nch, A_PAIRS, A_PAIR_W, A_PAIR_W), lambda i: (nb - 1 - i, 0, 0, 0)), colblk(0)],
        out_specs=[pl.BlockSpec((tb, 4 * A_WIDTH), lambda i: (nb - 1 - i, 0)), vec(A_WIDTH), vec(A_PAIR_W)],
        out_shape=[jax.ShapeDtypeStruct((t, 4 * A_WIDTH), BF16), jax.ShapeDtypeStruct((1, A_WIDTH), F32),
                   jax.ShapeDtypeStruct((1, A_PAIR_W), F32)],
        scratch_shapes=[pltpu.VMEM((A_PAIRS, A_PAIR_W, A_PAIR_W), F32)], name="hgrn_bwd")


def _gate_lane_masks(shape):
    lane = lax.broadcasted_iota(jnp.int32, shape, 1)
    return lane < B_HEADS, (lane >= B_HEADS) & (lane < 2 * B_HEADS)


def _b_pre_fwd(proj, conv_w, alog_row, dtb_row, tb=512):
    t = proj.shape[0]
    cb0 = COL_B // B_WIDTH

    def body(q_ref, k_ref, v_ref, qp_ref, kp_ref, vp_ref, w_ref, gi_ref, al_ref, dt_ref, qkv_ref, gates_ref, y_ref):
        first = pl.program_id(0) == 0
        for part, (c_ref, p_ref) in enumerate(((q_ref, qp_ref), (k_ref, kp_ref), (v_ref, vp_ref))):
            cols = slice(part * B_WIDTH, (part + 1) * B_WIDTH)
            prev = jnp.where(first, 0.0, p_ref[...])
            y = _conv_fwd(c_ref[...], prev, w_ref[:, cols])
            y_ref[:, cols] = y
            s = y * _sigmoid(y)
            if part < 2:
                outs = []
                for h in range(B_HEADS):
                    sh = s[:, h * B_DIM:(h + 1) * B_DIM]
                    outs.append(sh * lax.rsqrt(jnp.sum(sh * sh, axis=-1, keepdims=True) + EPS))
                s = jnp.concatenate(outs, axis=1)
            qkv_ref[:, cols] = s
        g = gi_ref[...]
        is_b, is_a = _gate_lane_masks(g.shape)
        la = -jnp.exp(al_ref[...]) * _softplus(g + dt_ref[...])
        gates_ref[...] = jnp.where(is_b, _sigmoid(g), jnp.where(is_a, la, 0.0))

    cur = lambda j: pl.BlockSpec((tb, B_WIDTH), lambda i, j=j: (i, cb0 + j))
    prv = lambda j: pl.BlockSpec((8, B_WIDTH), lambda i, j=j: (jnp.maximum(i * (tb // 8) - 1, 0), cb0 + j))
    vec = pl.BlockSpec((1, 128), lambda i: (0, 0))
    return pl.pallas_call(
        body, grid=(t // tb,),
        in_specs=[cur(0), cur(1), cur(2), prv(0), prv(1), prv(2), pl.BlockSpec((CONV_K, 3 * B_WIDTH), lambda i: (0, 0)),
                  pl.BlockSpec((tb, 128), lambda i: (i, COL_G // 128)), vec, vec],
        out_specs=[pl.BlockSpec((tb, 3 * B_WIDTH), lambda i: (i, 0)), pl.BlockSpec((tb, 128), lambda i: (i, 0)),
                   pl.BlockSpec((tb, 3 * B_WIDTH), lambda i: (i, 0))],
        out_shape=[jax.ShapeDtypeStruct((t, 3 * B_WIDTH), F32), jax.ShapeDtypeStruct((t, 128), F32),
                   jax.ShapeDtypeStruct((t, 3 * B_WIDTH), F32)],
        compiler_params=_cparams(("parallel",)), name="gdn_pre_fwd")(proj, proj, proj, proj, proj, proj, conv_w, proj, alog_row, dtb_row)


def _inv_unit_lower(amats):
    r, c = _iota2(CHUNK, CHUNK)
    eye = jnp.where(r == c, 1.0, 0.0)
    ps = [eye - a for a in amats]
    aks = amats
    for _ in range(5):
        aks = [_dot(ak, ak, hi=True) for ak in aks]
        ps = [p + _dot(p, ak, hi=True) for p, ak in zip(ps, aks)]
    return ps


def _b_local(qs, ks, vs, betas, gcs, grows, gls, solve=True):
    hs = range(len(qs))
    causal, strict = _tril(CHUNK), _tril(CHUNK, strict=True)
    decay = [jnp.where(causal, jnp.exp(jnp.minimum(gcs[h] - grows[h], 0.0)), 0.0) for h in hs]
    kb = [ks[h] * betas[h] for h in hs]
    kk = [_dot(kb[h], ks[h], "nt") for h in hs]
    qkr = [_dot(qs[h], ks[h], "nt") for h in hs]
    eg = [jnp.exp(gcs[h]) for h in hs]
    bv = [vs[h] * betas[h] for h in hs]
    kg = [kb[h] * eg[h] for h in hs]
    qk = [qkr[h] * decay[h] for h in hs]
    qd = [qs[h] * eg[h] for h in hs]
    ekd = [jnp.exp(gls[h] - gcs[h]) for h in hs]
    kd = [ks[h] * ekd[h] for h in hs]
    cd = [jnp.exp(gls[h]) for h in hs]
    loc = dict(decay=decay, kb=kb, kk=kk, eg=eg, bv=bv, kg=kg, qkr=qkr, qk=qk, qd=qd, ekd=ekd, kd=kd, cd=cd)
    if solve:
        tinv = _inv_unit_lower([jnp.where(strict, kk[h] * decay[h], 0.0) for h in hs])
        loc.update(tinv=tinv, u=[_dot(tinv[h], bv[h], hi=True) for h in hs], w=[_dot(tinv[h], kg[h], hi=True) for h in hs])
    return loc


def _b_state(loc, ids, s0s):
    n = range(len(ids))
    ws = [_dot(loc["w"][ids[j]], s0s[j]) for j in n]
    qs0 = [_dot(loc["qd"][ids[j]], s0s[j]) for j in n]
    vn = [loc["u"][ids[j]] - ws[j] for j in n]
    o = [qs0[j] + _dot(loc["qk"][ids[j]], vn[j]) for j in n]
    s1 = [s0s[j] * loc["cd"][ids[j]] + _dot(loc["kd"][ids[j]], vn[j], "tn") for j in n]
    return vn, o, s1


def _b_fwd(qkv, gates, proj, norm_g, beside=None, tb=256):
    t = qkv.shape[0]
    nch = tb // CHUNK

    def body(q_ref, k_ref, v_ref, ga_ref, z_ref, ng_ref, out_ref, st_ref, ti_ref, w_ref, vn_ref, o_ref, s_scr):
        @pl.when(pl.program_id(0) == 0)
        def _():
            s_scr[...] = jnp.zeros_like(s_scr)

        ltri = _tril(CHUNK).astype(F32)

        hs = range(B_HEADS)
        cols = [slice(h * B_DIM, (h + 1) * B_DIM) for h in hs]

        def pair(p, carry):
            cs = [2 * p, 2 * p + 1]
            rows = [pl.ds(pl.multiple_of(c * CHUNK, CHUNK), CHUNK) for c in cs]
            ga = [ga_ref[r, :] for r in rows]
            gcum = [_dot_exact_lhs(ltri, g) for g in ga]
            gcum_t = [g.T for g in gcum]
            items = [(i, h) for i in range(2) for h in hs]
            loc = _b_local([q_ref[rows[i], cols[h]] * GDN_SCALE for i, h in items], [k_ref[rows[i], cols[h]] for i, h in items],
                           [v_ref[rows[i], cols[h]] for i, h in items], [ga[i][:, h:h + 1] for i, h in items],
                           [gcum[i][:, B_HEADS + h:B_HEADS + h + 1] for i, h in items],
                           [gcum_t[i][B_HEADS + h:B_HEADS + h + 1, :] for i, h in items],
                           [gcum[i][CHUNK - 1:CHUNK, B_HEADS + h:B_HEADS + h + 1] for i, h in items])
            s0s = [s_scr[h] for h in hs]
            for i in range(2):
                ids = [i * B_HEADS + h for h in hs]
                for h in hs:
                    st_ref[cs[i], h] = s0s[h]
                    ti_ref[cs[i], h] = loc["tinv"][ids[h]]
                vn, o, s0s = _b_state(loc, ids, s0s)
                w_ref[rows[i], :] = jnp.concatenate([loc["w"][j] for j in ids], axis=1)
                vn_ref[rows[i], :] = jnp.concatenate(vn, axis=1)
                o_ref[rows[i], :] = jnp.concatenate(o, axis=1)
                outs = [_headnorm_fwd(o[h], ng_ref[...], z_ref[rows[i], cols[h]]) for h in hs]
                out_ref[rows[i], :] = jnp.concatenate(outs, axis=1).astype(BF16)
            for h in hs:
                s_scr[h] = s0s[h]
            return carry

        lax.fori_loop(0, nch // 2, pair, 0, unroll=2)

    part = lambda j: pl.BlockSpec((tb, B_WIDTH), lambda i, j=j: (i, j))
    wide = pl.BlockSpec((tb, B_WIDTH), lambda i: (i, 0))
    wide_shape = jax.ShapeDtypeStruct((t, B_WIDTH), F32)
    return _call_beside(
        body, beside, t // tb, (qkv, qkv, qkv, gates, proj, norm_g.reshape(1, B_DIM)), grid=(t // tb,),
        in_specs=[part(0), part(1), part(2), pl.BlockSpec((tb, 128), lambda i: (i, 0)),
                  pl.BlockSpec((tb, B_WIDTH), lambda i: (i, COL_B // B_WIDTH + 3)), pl.BlockSpec((1, B_DIM), lambda i: (0, 0))],
        out_specs=[wide, pl.BlockSpec((nch, B_HEADS, B_DIM, B_DIM), lambda i: (i, 0, 0, 0)),
                   pl.BlockSpec((nch, B_HEADS, CHUNK, CHUNK), lambda i: (i, 0, 0, 0)), wide, wide, wide],
        out_shape=[jax.ShapeDtypeStruct((t, B_WIDTH), BF16), jax.ShapeDtypeStruct((t // CHUNK, B_HEADS, B_DIM, B_DIM), F32),
                   jax.ShapeDtypeStruct((t // CHUNK, B_HEADS, CHUNK, CHUNK), F32), wide_shape, wide_shape, wide_shape],
        scratch_shapes=[pltpu.VMEM((B_HEADS, B_DIM, B_DIM), F32)], name="gdn_fwd")


def _b_bwd(qkv, gates, proj, norm_g, states, fwd_saved, dmixed, beside=None, tb=256):
    t = qkv.shape[0]
    nch = tb // CHUNK
    nb = t // tb

    def body(q_ref, k_ref, v_ref, ga_ref, z_ref, ng_ref, st_ref, ti_ref, w_ref, vn_ref, o_ref, dm0_ref, dm1_ref,
             dqkv_ref, dga_ref, dz_ref, dng_ref, ds_scr):
        @pl.when(pl.program_id(0) == 0)
        def _():
            ds_scr[...] = jnp.zeros_like(ds_scr)
            dng_ref[...] = jnp.zeros_like(dng_ref)

        ltri = _tril(CHUNK).astype(F32)
        strict = _tril(CHUNK, strict=True)
        lane = lax.broadcasted_iota(jnp.int32, (CHUNK, 128), 1)
        lane1 = lax.broadcasted_iota(jnp.int32, (1, 128), 1)

        nh = range(B_HEADS)
        cols = [slice(h * B_DIM, (h + 1) * B_DIM) for h in nh]
        rsum = lambda a: jnp.sum(a, axis=-1, keepdims=True)

        def pair(p, carry):
            cs = [nch - 1 - 2 * p, nch - 2 - 2 * p]
            crow = [pl.ds(pl.multiple_of(c * CHUNK, CHUNK), CHUNK) for c in cs]
            gas = [ga_ref[r, :] for r in crow]
            gcum = [_dot_exact_lhs(ltri, g) for g in gas]
            gcum_t = [g.T for g in gcum]
            items = [(i, h) for i in range(2) for h in nh]
            hs = range(len(items))
            q = [q_ref[crow[i], cols[h]] * GDN_SCALE for i, h in items]
            k = [k_ref[crow[i], cols[h]] for i, h in items]
            v = [v_ref[crow[i], cols[h]] for i, h in items]
            z = [z_ref[crow[i], cols[h]] for i, h in items]
            beta = [gas[i][:, h:h + 1] for i, h in items]
            s0 = [st_ref[cs[i], h] for i, h in items]
            r = _b_local(q, k, v, beta, [gcum[i][:, B_HEADS + h:B_HEADS + h + 1] for i, h in items],
                         [gcum_t[i][B_HEADS + h:B_HEADS + h + 1, :] for i, h in items],
                         [gcum[i][CHUNK - 1:CHUNK, B_HEADS + h:B_HEADS + h + 1] for i, h in items], solve=False)
            tinv = [ti_ref[cs[i], h] for i, h in items]
            w = [w_ref[crow[i], cols[h]] for i, h in items]
            vn = [vn_ref[crow[i], cols[h]] for i, h in items]
            decay, eg, qd, kd, kb, cd = (r[n] for n in ("decay", "eg", "qd", "kd", "kb", "cd"))
            dms = [(dm0_ref if h < 2 else dm1_ref)[crow[i], (h % 2) * B_DIM:(h % 2 + 1) * B_DIM].astype(F32) for i, h in items]
            hn = [_headnorm_bwd(dms[j], o_ref[crow[i], cols[h]], ng_ref[...], z[j]) for j, (i, h) in enumerate(items)]
            do = [hn[j][0] for j in hs]
            dvn_o = [_dot(r["qk"][j], do[j], "tn") for j in hs]
            dqk = [_dot(do[j], vn[j], "nt") for j in hs]
            dqd = [_dot(do[j], s0[j], "nt") for j in hs]
            ds_o = [_dot(qd[j], do[j], "tn") for j in hs]
            ds = [ds_scr[h] for h in nh]
            dvn, dkd, dcd = [None] * 8, [None] * 8, [None] * 8
            for i in range(2):
                for h in nh:
                    j = i * B_HEADS + h
                    dvn[j] = dvn_o[j] + _dot(kd[j], ds[h])
                    dkd[j] = _dot(vn[j], ds[h], "nt")
                    dcd[j] = jnp.sum(jnp.sum(s0[j] * ds[h], axis=0, keepdims=True), axis=1, keepdims=True)
                ds = [ds_o[i * B_HEADS + h] + ds[h] * cd[i * B_HEADS + h] - _dot(w[i * B_HEADS + h], dvn[i * B_HEADS + h], "tn")
                      for h in nh]
            for h in nh:
                ds_scr[h] = ds[h]
            dw = [-_dot(dvn[j], s0[j], "nt") for j in hs]
            dbv = [_dot(tinv[h], dvn[h], "tn", hi=True) for h in hs]
            dkg = [_dot(tinv[h], dw[h], "tn", hi=True) for h in hs]
            dt = [_dot(dvn[h], r["bv"][h], "nt", hi=True) + _dot(dw[h], r["kg"][h], "nt", hi=True) for h in hs]
            tdt = [_dot(tinv[h], dt[h], "tn", hi=True) for h in hs]
            da = [jnp.where(strict, -_dot(tdt[h], tinv[h], "nt", hi=True), 0.0) for h in hs]
            dm = [da[h] * decay[h] for h in hs]
            dn = [dqk[h] * decay[h] for h in hs]
            e = [(da[h] * r["kk"][h] + dqk[h] * r["qkr"][h]) * decay[h] for h in hs]
            dkb = [_dot(dm[h], k[h]) + dkg[h] * eg[h] for h in hs]
            dk = [_dot(dm[h], kb[h], "tn") + _dot(dn[h], q[h], "tn") + dkd[h] * r["ekd"][h] + dkb[h] * beta[h] for h in hs]
            dq = [_dot(dn[h], k[h]) + dqd[h] * eg[h] for h in hs]
            tkd = [rsum(dkd[h] * kd[h]) for h in hs]
            dgc = [rsum(e[h]) - rsum(e[h].T) + rsum(dqd[h] * qd[h]) - tkd[h] + rsum(dkg[h] * r["kg"][h]) for h in hs]
            dgl = [jnp.sum(tkd[h], axis=0, keepdims=True) + dcd[h] * cd[h] for h in hs]
            dbeta = [rsum(dbv[h] * v[h]) + rsum(dkb[h] * k[h]) for h in hs]
            for i in range(2):
                ids = [i * B_HEADS + h for h in nh]
                dbeta_m = sum(jnp.where(lane == h, dbeta[ids[h]], 0.0) for h in nh)
                dgc_m = sum(jnp.where(lane == B_HEADS + h, dgc[ids[h]], 0.0) for h in nh)
                dgl_m = sum(jnp.where(lane1 == B_HEADS + h, dgl[ids[h]], 0.0) for h in nh)
                dqkv_ref[crow[i], :] = jnp.concatenate(
                    [dq[j] * GDN_SCALE for j in ids] + [dk[j] for j in ids] + [dbv[j] * beta[j] for j in ids], axis=1)
                dz_ref[crow[i], :] = jnp.concatenate([hn[j][1] for j in ids], axis=1).astype(BF16)
                dga_ref[crow[i], :] = dbeta_m + _dot_exact_lhs(ltri, dgc_m, "tn") + dgl_m
            dng_ref[...] += sum(hn[j][2] for j in hs)
            return carry

        lax.fori_loop(0, nch // 2, pair, 0, unroll=2)

    part = lambda j: pl.BlockSpec((tb, B_WIDTH), lambda i, j=j: (nb - 1 - i, j))
    rowblk = lambda w, j=0: pl.BlockSpec((tb, w), lambda i, j=j: (nb - 1 - i, j))
    return _call_beside(
        body, beside, nb, (qkv, qkv, qkv, gates, proj, norm_g.reshape(1, B_DIM), states, *fwd_saved, dmixed, dmixed), grid=(nb,),
        in_specs=[part(0), part(1), part(2), rowblk(128), rowblk(B_WIDTH, COL_B // B_WIDTH + 3),
                  pl.BlockSpec((1, B_DIM), lambda i: (0, 0)),
                  pl.BlockSpec((nch, B_HEADS, B_DIM, B_DIM), lambda i: (nb - 1 - i, 0, 0, 0)),
                  pl.BlockSpec((nch, B_HEADS, CHUNK, CHUNK), lambda i: (nb - 1 - i, 0, 0, 0)),
                  rowblk(B_WIDTH), rowblk(B_WIDTH), rowblk(B_WIDTH), rowblk(256, 1), rowblk(256, 2)],
        out_specs=[rowblk(3 * B_WIDTH), rowblk(128), rowblk(B_WIDTH), pl.BlockSpec((1, B_DIM), lambda i: (0, 0))],
        out_shape=[jax.ShapeDtypeStruct((t, 3 * B_WIDTH), F32), jax.ShapeDtypeStruct((t, 128), F32),
                   jax.ShapeDtypeStruct((t, B_WIDTH), BF16), jax.ShapeDtypeStruct((1, B_DIM), F32)],
        scratch_shapes=[pltpu.VMEM((B_HEADS, B_DIM, B_DIM), F32)], name="gdn_bwd")


def _b_pre_bwd(proj, y3, conv_w, alog_row, dtb_row, dqkv, dgates, tb=512):
    t = proj.shape[0]
    nb = t // tb
    cb0 = COL_B // B_WIDTH

    def body(q_ref, k_ref, v_ref, y_ref, w_ref, gi_ref, al_ref, dt_ref, dqkv_ref, dga_ref,
             dy_ref, dgi_ref, dw_ref, dal_ref, ddt_ref, nxt_scr):
        step_id = pl.program_id(0)

        @pl.when(step_id == 0)
        def _():
            dw_ref[...] = jnp.zeros_like(dw_ref)
            dal_ref[...] = jnp.zeros_like(dal_ref)
            ddt_ref[...] = jnp.zeros_like(ddt_ref)

        for part, c_ref in enumerate((q_ref, k_ref, v_ref)):
            cols = slice(part * B_WIDTH, (part + 1) * B_WIDTH)
            cur = c_ref[...]
            w = w_ref[:, cols]
            y = y_ref[:, cols]
            sg = _sigmoid(y)
            s = y * sg
            dsn = dqkv_ref[:, cols]
            if part < 2:
                outs = []
                for h in range(B_HEADS):
                    hc = slice(h * B_DIM, (h + 1) * B_DIM)
                    sh, dh = s[:, hc], dsn[:, hc]
                    rq = lax.rsqrt(jnp.sum(sh * sh, axis=-1, keepdims=True) + EPS)
                    nh = sh * rq
                    outs.append(rq * (dh - nh * jnp.sum(dh * nh, axis=-1, keepdims=True)))
                dsn = jnp.concatenate(outs, axis=1)
            dy = dsn * (sg * (1.0 + y * (1.0 - sg)))
            later = _later_rows(dy, jnp.where(step_id == 0, 0.0, nxt_scr[:, cols]))
            dy_ref[:, cols] = sum(later[j] * w[j:j + 1] for j in range(CONV_K)).astype(BF16)
            nxt_scr[:, cols] = dy[0:8]
            dw_ref[:, cols] += jnp.concatenate([jnp.sum(cur * later[j], axis=0, keepdims=True) for j in range(CONV_K)], axis=0)
        g = gi_ref[...]
        dga = dga_ref[...]
        is_b, is_a = _gate_lane_masks(g.shape)
        beta = _sigmoid(g)
        pre = g + dt_ref[...]
        ea = jnp.exp(al_ref[...])
        la = -ea * _softplus(pre)
        dpre = jnp.where(is_a, dga * (-ea) * _sigmoid(pre), 0.0)
        dgi_ref[...] = jnp.where(is_b, dga * beta * (1.0 - beta), dpre).astype(BF16)
        dal_ref[...] += jnp.sum(jnp.where(is_a, dga * la, 0.0), axis=0, keepdims=True)
        ddt_ref[...] += jnp.sum(dpre, axis=0, keepdims=True)

    cur = lambda j: pl.BlockSpec((tb, B_WIDTH), lambda i, j=j: (nb - 1 - i, cb0 + j))
    vec = pl.BlockSpec((1, 128), lambda i: (0, 0))
    wspec = pl.BlockSpec((CONV_K, 3 * B_WIDTH), lambda i: (0, 0))
    rowblk = lambda width, j=0: pl.BlockSpec((tb, width), lambda i, j=j: (nb - 1 - i, j))
    return pl.pallas_call(
        body, grid=(nb,),
        in_specs=[cur(0), cur(1), cur(2), rowblk(3 * B_WIDTH), wspec, rowblk(128, COL_G // 128), vec, vec,
                  rowblk(3 * B_WIDTH), rowblk(128)],
        out_specs=[rowblk(3 * B_WIDTH), rowblk(128), wspec, vec, vec],
        out_shape=[jax.ShapeDtypeStruct((t, 3 * B_WIDTH), BF16), jax.ShapeDtypeStruct((t, 128), BF16),
                   jax.ShapeDtypeStruct((CONV_K, 3 * B_WIDTH), F32), jax.ShapeDtypeStruct((1, 128), F32), jax.ShapeDtypeStruct((1, 128), F32)],
        scratch_shapes=[pltpu.VMEM((8, 3 * B_WIDTH), F32)],
        compiler_params=_cparams(("arbitrary",)), name="gdn_pre_bwd")(
            proj, proj, proj, y3, conv_w, proj, alog_row, dtb_row, dqkv, dgates)


def _c_gates(xc, wa_ref, ba_ref, wx_ref, bx_ref, lam_ref, is_row0):
    r = _sigmoid(_dot(xc, wa_ref[...]) + ba_ref[...])
    i = _sigmoid(_dot(xc, wx_ref[...]) + bx_ref[...])
    sp = _softplus(-lam_ref[...])
    log_a = -RG_C * r * sp
    a = jnp.exp(log_a)
    m2 = _neg_expm1(2.0 * log_a)
    mult = jnp.where(is_row0, 1.0, jnp.sqrt(jnp.maximum(m2, EPS)))
    return r, i, sp, log_a, a, m2, mult


def _row0_mask(tb, first):
    ridx = lax.broadcasted_iota(jnp.int32, (tb, C_WIDTH), 0)
    return (ridx == 0) & first


def _c_fwd(proj, conv_w, conv_b, wa, ba, wx, bx, lam, tb=512):
    t = proj.shape[0]
    cbx = COL_C // C_WIDTH

    def body(x_ref, xp_ref, y_ref, w_ref, cb_ref, wa_ref, ba_ref, wx_ref, bx_ref, lam_ref, out_ref, h_ref, a_scr, b_scr, h_scr):
        first = pl.program_id(0) == 0

        @pl.when(first)
        def _():
            h_scr[...] = jnp.zeros_like(h_scr)

        prev = jnp.where(first, 0.0, xp_ref[...])
        xc = _conv_fwd(x_ref[...], prev, w_ref[...]) + cb_ref[...]
        _, i, _, _, a, _, mult = _c_gates(xc, wa_ref, ba_ref, wx_ref, bx_ref, lam_ref, _row0_mask(tb, first))
        ta, tb_ = _tile_scan(a, mult * i * xc)
        a_scr[...] = ta
        b_scr[...] = tb_

        def step(blk, h):
            rows = pl.ds(pl.multiple_of(blk * 8, 8), 8)
            h_ref[rows, :] = jnp.broadcast_to(h, (8, C_WIDTH))
            return a_scr[rows, :][7:8] * h + b_scr[rows, :][7:8]

        h_scr[...] = lax.fori_loop(0, tb // 8, step, h_scr[...], unroll=8)
        hs = ta * h_ref[...] + tb_
        h_ref[...] = hs
        gl, _ = _gelu_tanh(y_ref[...])
        out_ref[...] = (gl * hs).astype(BF16)

    vec = pl.BlockSpec((1, C_WIDTH), lambda i: (0, 0))
    mat = pl.BlockSpec((C_WIDTH, C_WIDTH), lambda i: (0, 0))
    row = pl.BlockSpec((tb, C_WIDTH), lambda i: (i, 0))
    return pl.pallas_call(
        body, grid=(t // tb,),
        in_specs=[pl.BlockSpec((tb, C_WIDTH), lambda i: (i, cbx)),
                  pl.BlockSpec((8, C_WIDTH), lambda i: (jnp.maximum(i * (tb // 8) - 1, 0), cbx)),
                  pl.BlockSpec((tb, C_WIDTH), lambda i: (i, cbx + 1)),
                  pl.BlockSpec((CONV_K, C_WIDTH), lambda i: (0, 0)), vec, mat, vec, mat, vec, vec],
        out_specs=[row, row],
        out_shape=[jax.ShapeDtypeStruct((t, C_WIDTH), BF16), jax.ShapeDtypeStruct((t, C_WIDTH), F32)],
        scratch_shapes=[pltpu.VMEM((tb, C_WIDTH), F32), pltpu.VMEM((tb, C_WIDTH), F32), pltpu.VMEM((1, C_WIDTH), F32)],
        compiler_params=_cparams(("arbitrary",)), name="lru_fwd")(proj, proj, proj, conv_w, conv_b, wa, ba, wx, bx, lam)


def _c_bwd(proj, conv_w, conv_b, wa, ba, wx, bx, lam, hs, dmixed, tb=512):
    t = proj.shape[0]
    nb = t // tb
    cbx = COL_C // C_WIDTH

    def body(x_ref, xp_ref, y_ref, w_ref, cb_ref, wa_ref, ba_ref, wx_ref, bx_ref, lam_ref, h_ref, hp_ref, dm_ref,
             dxc_ref, dyg_ref, dw_ref, dcb_ref, dwa_ref, dba_ref, dwx_ref, dbx_ref, dlam_ref, g_scr, a_scr, cin_scr, c_scr, nxt_scr):
        step_id = pl.program_id(0)
        first = step_id == nb - 1

        @pl.when(step_id == 0)
        def _():
            c_scr[...] = jnp.zeros_like(c_scr)
            for ref in (dw_ref, dcb_ref, dwa_ref, dba_ref, dwx_ref, dbx_ref, dlam_ref):
                ref[...] = jnp.zeros_like(ref)

        cur = x_ref[...]
        prev = jnp.where(first, 0.0, xp_ref[...])
        w = w_ref[...]
        shifted = [_shift_rows(cur, prev, 3 - j, down=True) for j in range(3)] + [cur]
        xc = shifted[0] * w[0:1] + shifted[1] * w[1:2] + shifted[2] * w[2:3] + shifted[3] * w[3:4] + cb_ref[...]
        row0 = _row0_mask(tb, first)
        r, i, sp, log_a, a, m2, mult = _c_gates(xc, wa_ref, ba_ref, wx_ref, bx_ref, lam_ref, row0)
        h = h_ref[...]
        hprev = _shift_rows(h, jnp.where(first, 0.0, hp_ref[...]), 1, down=True)
        gl, dgl = _gelu_tanh(y_ref[...])
        dm = dm_ref[...].astype(F32)
        dyg_ref[...] = (dm * h * dgl).astype(BF16)
        dout = dm * gl
        ta, te = _tile_scan(a, a * dout, reverse=True)
        a_scr[...] = ta
        g_scr[...] = te

        def step(blk, carry):
            rows = pl.ds(pl.multiple_of((tb // 8 - 1 - blk) * 8, 8), 8)
            cin_scr[rows, :] = jnp.broadcast_to(carry, (8, C_WIDTH))
            return a_scr[rows, :][0:1] * carry + g_scr[rows, :][0:1]

        c_scr[...] = lax.fori_loop(0, tb // 8, step, c_scr[...], unroll=8)
        cin = cin_scr[...]
        cout = ta * cin + te
        last_in_tile = lax.broadcasted_iota(jnp.int32, (tb, C_WIDTH), 0) % 8 == 7
        dbx = dout + jnp.where(last_in_tile, cin, pltpu.roll(cout, tb - 1, 0))
        da = dbx * hprev
        dmult = jnp.where(row0, 0.0, dbx * i * xc)
        di = dbx * mult * xc
        dxc = dbx * mult * i
        dm2 = jnp.where(m2 > EPS, dmult * 0.5 / mult, 0.0)
        dlog_a = da * a - 2.0 * a * a * dm2
        dr = dlog_a * (-RG_C) * sp
        dlam_ref[...] += jnp.sum(dlog_a * (-RG_C) * r, axis=0, keepdims=True) * (-_sigmoid(-lam_ref[...]))
        dpa = dr * r * (1.0 - r)
        dpx = di * i * (1.0 - i)
        dba_ref[...] += jnp.sum(dpa, axis=0, keepdims=True)
        dbx_ref[...] += jnp.sum(dpx, axis=0, keepdims=True)
        dwa_ref[...] += _dot(xc, dpa, "tn")
        dwx_ref[...] += _dot(xc, dpx, "tn")
        dxc = dxc + _dot(dpa, wa_ref[...], "nt") + _dot(dpx, wx_ref[...], "nt")
        dxc_ref[...] = _conv_bwd_rows(dxc, jnp.where(step_id == 0, 0.0, nxt_scr[...]), w).astype(BF16)
        nxt_scr[...] = dxc[0:8]
        dcb_ref[...] += jnp.sum(dxc, axis=0, keepdims=True)
        dw_ref[...] += jnp.concatenate([jnp.sum(shifted[j] * dxc, axis=0, keepdims=True) for j in range(CONV_K)], axis=0)

    vec = pl.BlockSpec((1, C_WIDTH), lambda i: (0, 0))
    mat = pl.BlockSpec((C_WIDTH, C_WIDTH), lambda i: (0, 0))
    cw = pl.BlockSpec((CONV_K, C_WIDTH), lambda i: (0, 0))
    row = lambda j=0: pl.BlockSpec((tb, C_WIDTH), lambda i, j=j: (nb - 1 - i, j))
    halo = lambda j=0: pl.BlockSpec((8, C_WIDTH), lambda i, j=j: (jnp.maximum((nb - 1 - i) * (tb // 8) - 1, 0), j))
    return pl.pallas_call(
        body, grid=(nb,),
        in_specs=[row(cbx), halo(cbx), row(cbx + 1), cw, vec, mat, vec, mat, vec, vec, row(), halo(), row(3)],
        out_specs=[row(), row(), cw, vec, mat, vec, mat, vec, vec],
        out_shape=[jax.ShapeDtypeStruct((t, C_WIDTH), BF16), jax.ShapeDtypeStruct((t, C_WIDTH), BF16),
                   jax.ShapeDtypeStruct((CONV_K, C_WIDTH), F32), jax.ShapeDtypeStruct((1, C_WIDTH), F32),
                   jax.ShapeDtypeStruct((C_WIDTH, C_WIDTH), F32), jax.ShapeDtypeStruct((1, C_WIDTH), F32),
                   jax.ShapeDtypeStruct((C_WIDTH, C_WIDTH), F32), jax.ShapeDtypeStruct((1, C_WIDTH), F32),
                   jax.ShapeDtypeStruct((1, C_WIDTH), F32)],
        scratch_shapes=[pltpu.VMEM((tb, C_WIDTH), F32), pltpu.VMEM((tb, C_WIDTH), F32), pltpu.VMEM((tb, C_WIDTH), F32),
                        pltpu.VMEM((1, C_WIDTH), F32), pltpu.VMEM((8, C_WIDTH), F32)],
        compiler_params=_cparams(("arbitrary",)), name="lru_bwd")(
            proj, proj, proj, conv_w, conv_b, wa, ba, wx, bx, lam, hs, hs, dmixed)


def _mesh_pos():
    return lax.axis_index("x"), lax.axis_index("y"), lax.axis_index("c")


class _Exchange:
    def __init__(self, arrays, layouts):
        self.arrays, self.layouts = list(arrays), list(layouts)
        self.out_shapes = []
        for a, lay in zip(self.arrays, self.layouts):
            if lay == 'a2a':
                shp = a.shape
            elif lay == 'slot':
                shp = (N_DEV,) + a.shape
            elif lay == 'rows':
                shp = (N_DEV * a.shape[0], a.shape[1])
            else:
                shp = (a.shape[0], N_DEV * a.shape[1])
            self.out_shapes.append(jax.ShapeDtypeStruct(shp, a.dtype))
        n = len(self.arrays)
        self.scratch = [pltpu.SemaphoreType.DMA((7 * n,)), pltpu.SemaphoreType.DMA((7 * n,)), pltpu.SemaphoreType.DMA((n,))]

    def _landing(self, a, dst_ref, idx):
        lay, shape = self.layouts[a], self.arrays[a].shape
        if lay in ('a2a', 'slot'):
            return dst_ref.at[idx]
        if lay == 'rows':
            return dst_ref.at[pl.ds(pl.multiple_of(idx * shape[0], shape[0]), shape[0]), :]
        return dst_ref.at[:, pl.ds(pl.multiple_of(idx * shape[1], shape[1]), shape[1])]

    def copies(self, src_refs, dst_refs, send_sems, recv_sems, local_sems):
        mx, my, mc = _mesh_pos()
        me = 4 * mx + 2 * my + mc
        out = []
        for a, (src, dst) in enumerate(zip(src_refs, dst_refs)):
            a2a = self.layouts[a] == 'a2a'
            out.append(pltpu.make_async_copy(src.at[me] if a2a else src, self._landing(a, dst, me), local_sems.at[a]))
            for k in range(1, N_DEV):
                px = 1 - mx if k & 4 else mx
                py = 1 - my if k & 2 else my
                pc = 1 - mc if k & 1 else mc
                out.append(pltpu.make_async_remote_copy(
                    src_ref=src.at[4 * px + 2 * py + pc] if a2a else src, dst_ref=self._landing(a, dst, me),
                    send_sem=send_sems.at[7 * a + k - 1], recv_sem=recv_sems.at[7 * a + k - 1],
                    device_id=(px, py, pc), device_id_type=MESH))
        return out


_ANY = pl.BlockSpec(memory_space=pl.ANY)


def _gather_two_level(arrays, name):
    n = len(arrays)

    def body(*refs):
        srcs, outs = refs[:n], refs[n:2 * n]
        send_sems, recv_sems, local_sems = refs[2 * n:]
        mx, my, mc = _mesh_pos()
        me, sibling = (mx, my, mc), (mx, my, 1 - mc)
        chips = [(1 - mx, my), (mx, 1 - my), (1 - mx, 1 - my)]

        def slot(a, px, py, pc):
            return outs[a].at[4 * px + 2 * py + pc]

        def copy(a, k, block, to, src=None):
            return pltpu.make_async_remote_copy(
                src_ref=slot(a, *block) if src is None else src, dst_ref=slot(a, *block),
                send_sem=send_sems.at[7 * a + k], recv_sem=recv_sems.at[7 * a + k], device_id=to, device_id_type=MESH)

        mine = [pltpu.make_async_copy(srcs[a], slot(a, *me), local_sems.at[a]) for a in range(n)]
        for cp in mine:
            cp.start()
        first = []
        for a in range(n):
            first.append(copy(a, 0, me, sibling, src=srcs[a]))
            first += [copy(a, 1 + j, me, (*chip, mc), src=srcs[a]) for j, chip in enumerate(chips)]
        for cp in first:
            cp.start()
        passed = []
        for a in range(n):
            for j, chip in enumerate(chips):
                copy(a, 1 + j, (*chip, mc), me).wait_recv()
                passed.append(copy(a, 4 + j, (*chip, mc), sibling))
                passed[-1].start()
        for a in range(n):
            copy(a, 0, sibling, me).wait_recv()
            for j, chip in enumerate(chips):
                copy(a, 4 + j, (*chip, 1 - mc), me).wait_recv()
        for cp in first + passed:
            cp.wait_send()
        for cp in mine:
            cp.wait()

    return pl.pallas_call(
        body, out_shape=[jax.ShapeDtypeStruct((N_DEV,) + a.shape, a.dtype) for a in arrays],
        in_specs=[_ANY] * n, out_specs=[_ANY] * n,
        scratch_shapes=[pltpu.SemaphoreType.DMA((7 * n,)), pltpu.SemaphoreType.DMA((7 * n,)), pltpu.SemaphoreType.DMA((n,))],
        name=name)(*arrays)


def _call_beside(body, ex, nsteps, args, *, grid, in_specs, out_specs, out_shape, scratch_shapes, name):
    if ex is None:
        outs = pl.pallas_call(body, grid=grid, in_specs=in_specs, out_specs=out_specs, out_shape=out_shape,
                              scratch_shapes=scratch_shapes, compiler_params=_cparams(("arbitrary",)), name=name)(*args)
        return outs, None
    n_in, n_out, n_scr, n = len(in_specs), len(out_specs), len(scratch_shapes), len(ex.arrays)

    def wrapped(*refs):
        ins, refs = refs[:n_in], refs[n_in:]
        ex_ins, refs = refs[:n], refs[n:]
        outs, refs = refs[:n_out], refs[n_out:]
        ex_outs, refs = refs[:n], refs[n:]
        scr, sems = refs[:n_scr], refs[n_scr:]
        step = pl.program_id(0)

        @pl.when(step == 0)
        def _():
            for cp in ex.copies(ex_ins, ex_outs, *sems):
                cp.start()

        body(*ins, *outs, *scr)

        @pl.when(step == nsteps - 1)
        def _():
            for cp in ex.copies(ex_ins, ex_outs, *sems):
                cp.wait()

    res = pl.pallas_call(
        wrapped, grid=grid, in_specs=list(in_specs) + [_ANY] * n, out_specs=list(out_specs) + [_ANY] * n,
        out_shape=list(out_shape) + ex.out_shapes, scratch_shapes=list(scratch_shapes) + ex.scratch,
        compiler_params=_cparams(("arbitrary",)), name=name)(*args, *ex.arrays)
    return res[:n_out], res[n_out:]


def _adamw_math(w, g, m, v):
    m = ADAM_B1 * m + (1.0 - ADAM_B1) * g
    v = ADAM_B2 * v + (1.0 - ADAM_B2) * (g * g)
    m_hat = m / (1.0 - ADAM_B1 ** ADAM_STEP)
    v_hat = v / (1.0 - ADAM_B2 ** ADAM_STEP)
    delta = -ADAM_LR * (m_hat / (jnp.sqrt(v_hat) + ADAM_EPS) + ADAM_WD * w)
    return delta, m, v


def _sum_adamw(parts, w, m, v, tr, name):
    ns = len(parts)
    p, r, c = parts[0].shape
    tr = min(tr, r)
    assert r % tr == 0
    nt = r // tr

    def body(*refs):
        p_refs, (w_ref, m_ref, v_ref, g_ref, d_ref, nm_ref, nv_ref) = refs[:ns], refs[ns:]
        for s in range(ns):
            @pl.when(pl.program_id(0) == s)
            def _(p_ref=p_refs[s]):
                g = p_ref[0].astype(F32)
                for j in range(1, p):
                    g = g + p_ref[j].astype(F32)
                delta, nm, nv = _adamw_math(w_ref[...], g, m_ref[...], v_ref[...])
                g_ref[...] = g
                d_ref[...] = delta
                nm_ref[...] = nm
                nv_ref[...] = nv

    part_spec = lambda s: pl.BlockSpec((p, tr, c), lambda sec, i, s=s: (0, jnp.where(sec == s, i, 0), 0))
    row = pl.BlockSpec((tr, c), lambda sec, i: (sec * nt + i, 0))
    return pl.pallas_call(
        body, grid=(ns, nt), in_specs=[part_spec(s) for s in range(ns)] + [row, row, row],
        out_specs=[row] * 4, out_shape=[jax.ShapeDtypeStruct((ns * r, c), F32)] * 4,
        compiler_params=_cparams(("arbitrary", "arbitrary")), name=name)(*parts, w, m, v)


def _sum_parts(parts, name):
    p, r, c = parts.shape

    def body(p_ref, o_ref):
        g = p_ref[0]
        for j in range(1, p):
            g = g + p_ref[j]
        o_ref[...] = g

    return pl.pallas_call(body, out_shape=jax.ShapeDtypeStruct((r, c), F32), name=name)(parts)


def _rows_of(shape):
    n = 1
    for d in shape:
        n *= d
    return n, -(-n // 128)


def _pack(arrs):
    blocks = []
    for a in arrs:
        n, nr = _rows_of(a.shape)
        blocks.append(jnp.pad(a.reshape(-1).astype(F32), (0, nr * 128 - n)).reshape(nr, 128))
    rows = sum(b.shape[0] for b in blocks)
    if rows % 8:
        blocks.append(jnp.zeros((8 - rows % 8, 128), F32))
    return jnp.concatenate(blocks, axis=0)


def _unpack(buf, shapes):
    out, r0 = [], 0
    for s in shapes:
        n, nr = _rows_of(s)
        out.append(buf[r0:r0 + nr].reshape(-1)[:n].reshape(s))
        r0 += nr
    return out


def _block_diag(w):
    rows = [jnp.pad(w[i], ((0, 0), (i * C_BLOCK_DIM, C_WIDTH - (i + 1) * C_BLOCK_DIM))) for i in range(C_BLOCKS)]
    return jnp.concatenate(rows, axis=0)


def _diag_blocks(m):
    m4 = m.reshape(C_BLOCKS, C_BLOCK_DIM, C_BLOCKS, C_BLOCK_DIM)
    return jnp.stack([m4[i, :, i, :] for i in range(C_BLOCKS)])


def _gate_row(v):
    return jnp.pad(v.astype(F32), (B_HEADS, 128 - 2 * B_HEADS)).reshape(1, 128)


def _permute_w_in(w):
    pad = jnp.zeros(w.shape[:-1] + (D_IN_PAD - D_IN,), w.dtype)
    return jnp.concatenate([w[..., :3072], w[..., 3080:3592], w[..., 3072:3080], pad], axis=-1)


def _unpermute_w_in(w):
    return jnp.concatenate([w[..., :3072], w[..., COL_G:COL_G + 8], w[..., 3072:COL_G]], axis=-1)


_WEIGHTS = ['norm1_g', 'w_in', 'hgrn_lb_logits', 'hgrn_norm_g', 'gdn_conv_w', 'gdn_a_log', 'gdn_dt_bias', 'gdn_norm_g',
            'lru_conv_w', 'lru_conv_b', 'lru_w_a', 'lru_b_a', 'lru_w_x', 'lru_b_x', 'lru_lambda', 'w_out', 'norm2_g',
            'w_up', 'w_down', 'final_norm_g']
_BIG = ('w_in', 'w_out', 'w_up', 'w_down')
_SHARDED_SMALL = ('gdn_conv_w', 'lru_conv_w')


def _step(x, target, w, m, v):
    t = x.shape[0]
    mx, my, mc = _mesh_pos()
    me = 4 * mx + 2 * my + mc

    bf = lambda a: a.astype(BF16)

    def full_w_in(g):
        return _permute_w_in(jnp.moveaxis(g, 0, 1).reshape(D_MODEL, D_IN))

    conv_shapes = [w['gdn_conv_w'].shape, w['lru_conv_w'].shape]
    g_in, g_conv = _gather_two_level([bf(w['w_in'][0]), _pack([w['gdn_conv_w'], w['lru_conv_w']])], "gather_first")
    w_in = [full_w_in(g_in)]
    w_out, w_up, w_down = [], [], []
    gdn_cw, lru_cw = [], []
    for j in range(N_DEV):
        a, b = _unpack(g_conv[j], conv_shapes)
        gdn_cw.append(a)
        lru_cw.append(b)
    gdn_cw = jnp.concatenate(gdn_cw, axis=-1)
    lru_cw = jnp.concatenate(lru_cw, axis=-1)

    lbnd = _lb_fwd(w['hgrn_lb_logits'])
    row = lambda a: a.reshape(1, -1)

    def c_args(l):
        return (lru_cw[l], row(w['lru_conv_b'][l]), _block_diag(w['lru_w_a'][l]), row(w['lru_b_a'][l]),
                _block_diag(w['lru_w_x'][l]), row(w['lru_b_x'][l]), row(w['lru_lambda'][l]))

    saved = []
    xl = x
    h = _rms_fwd(x, w['norm1_g'][0], name="rms_fwd")
    for l in range(DEPTH):
        proj = _mm_rows(h, w_in[l], "nn", 512, "mm_proj")
        mix_a, st_a = _a_fwd(proj, lbnd[l], w['hgrn_norm_g'][l])
        alr, dtr = _gate_row(w['gdn_a_log'][l]), _gate_row(w['gdn_dt_bias'][l])
        qkv, gates, y3 = _b_pre_fwd(proj, gdn_cw[l], alr, dtr)
        nxt = [bf(w['w_in'][l + 1])] if l + 1 < DEPTH else []
        gather = _Exchange([bf(w['w_out'][l]), bf(w['w_up'][l]), bf(w['w_down'][l])] + nxt, ['rows', 'cols', 'rows'] + ['slot'] * len(nxt))
        (mix_b, st_b, *b_saved), got = _b_fwd(qkv, gates, proj, w['gdn_norm_g'][l], beside=gather)
        w_out.append(got[0])
        w_up.append(got[1])
        w_down.append(got[2])
        if nxt:
            w_in.append(full_w_in(got[3]))
        mix_c, hs = _c_fwd(proj, *c_args(l))
        mixed = [mix_a, mix_b, mix_c]
        x_mid, h2 = _mm_rows(mixed, w_out[l], "nn", 1024, "mm_out", residual=xl, epilogue="rms_fwd", norm=w['norm2_g'][l])
        act, up = _mm_rows(h2, w_up[l], "nn", 512, "mm_up", epilogue="relu2")
        saved.append(dict(x=xl, h=h, proj=proj, st_a=st_a, qkv=qkv, gates=gates, y3=y3, st_b=st_b, b_saved=b_saved, hs=hs, mixed=mixed,
                          x_mid=x_mid, h2=h2, up=up, act=act, alr=alr, dtr=dtr))
        if l + 1 < DEPTH:
            xl, h = _mm_rows(act, w_down[l], "nn", 512, "mm_down", residual=x_mid, epilogue="rms_fwd", norm=w['norm1_g'][l + 1])
        else:
            xl = _mm_rows(act, w_down[l], "nn", 512, "mm_down_last", residual=x_mid)
    loss, dx, dgf = _loss_head(xl, w['final_norm_g'], target)

    gs = {n: [None] * DEPTH for n in _WEIGHTS}
    recv = {n: [None] * DEPTH for n in _BIG}
    dw_in_above = None
    for l in reversed(range(DEPTH)):
        s = saved[l]
        dup = _mm_rows(dx, w_down[l], "nt", 512, "mm_dact", epilogue="drelu2", up=s['up'])
        dw_down = _mm_tn(s['act'], dx, 1024, "mm_dw_down").reshape(N_DEV, D_FF // N_DEV, D_MODEL)
        dx_mid, dg2 = _mm_rows(dup, w_up[l], "nt", 512, "mm_dh2", epilogue="rms_bwd", norm=(s['x_mid'], w['norm2_g'][l], dx))
        dw_up = _mm_tn(s['h2'], dup, 1024, "mm_dw_up", slab=D_FF // N_DEV)
        gs['norm2_g'][l] = dg2[0]
        dmixed = _mm_rows(dx_mid, w_out[l], "nt", 1024, "mm_dmixed")
        dw_out = _mm_tn(s['mixed'], dx_mid, 1024, "mm_dw_out").reshape(N_DEV, D_MODEL // N_DEV, D_MODEL)
        proj = s['proj']
        above = [dw_in_above] if dw_in_above is not None else []
        (dpa, dlb, dnga), got = _a_bwd(proj, lbnd[l], w['hgrn_norm_g'][l], s['st_a'], dmixed,
                                       beside=_Exchange([dw_out] + above, ['a2a'] * (1 + len(above))))
        recv['w_out'][l] = got[0]
        if above:
            recv['w_in'][l + 1] = got[1]
        gs['hgrn_lb_logits'][l] = dlb[0]
        gs['hgrn_norm_g'][l] = dnga[0, :A_DIM] + dnga[0, A_DIM:]
        (dqkv, dgates, dz, dngb), got = _b_bwd(s['qkv'], s['gates'], proj, w['gdn_norm_g'][l], s['st_b'], s['b_saved'], dmixed,
                                               beside=_Exchange([dw_up, dw_down], ['a2a', 'a2a']))
        recv['w_up'][l], recv['w_down'][l] = got
        dxb, dgi, dcwb, dal, ddt = _b_pre_bwd(proj, s['y3'], gdn_cw[l], s['alr'], s['dtr'], dqkv, dgates)
        gs['gdn_norm_g'][l] = dngb[0]
        gs['gdn_conv_w'][l] = dcwb
        gs['gdn_a_log'][l] = dal[0, B_HEADS:2 * B_HEADS]
        gs['gdn_dt_bias'][l] = ddt[0, B_HEADS:2 * B_HEADS]
        dxc_in, dyg, dcwc, dcb, dwa, dba, dwx, dbx, dlam = _c_bwd(proj, *c_args(l), s['hs'], dmixed)
        gs['lru_conv_w'][l] = dcwc
        gs['lru_conv_b'][l] = dcb[0]
        gs['lru_w_a'][l] = _diag_blocks(dwa)
        gs['lru_b_a'][l] = dba[0]
        gs['lru_w_x'][l] = _diag_blocks(dwx)
        gs['lru_b_x'][l] = dbx[0]
        gs['lru_lambda'][l] = dlam[0]
        dproj = [dpa, dxb, dz, dxc_in, dyg, dgi]
        dw_in = _unpermute_w_in(_mm_tn(s['h'], dproj, 1024, "mm_dw_in"))
        dw_in_above = jnp.moveaxis(dw_in.reshape(D_MODEL, N_DEV, D_IN // N_DEV), 1, 0)
        norm = (s['x'], w['norm1_g'][l], dx_mid)
        if l > 0:
            dx, dg1 = _mm_rows(dproj, w_in[l], "nt", 512, "mm_dh", epilogue="rms_bwd", norm=norm)
        else:
            (dx, dg1), got = _mm_rows(dproj, w_in[l], "nt", 512, "mm_dh_last", epilogue="rms_bwd", norm=norm,
                                      beside=_Exchange([dw_in_above], ['a2a']))
            recv['w_in'][0] = got[0]
        gs['norm1_g'][l] = dg1[0]
    grad_x = dx
    part = {n: jnp.stack(gs[n]) for n in _WEIGHTS if n != 'final_norm_g' and n not in _BIG}
    part['final_norm_g'] = dgf[0]
    part['hgrn_lb_logits'] = _lb_bwd(w['hgrn_lb_logits'], part['hgrn_lb_logits'])

    small = [n for n in _WEIGHTS if n not in _BIG]
    packed = _pack([part[n] for n in small] + [loss])
    all_small, = _gather_two_level([packed], "gather_small")

    grads, deltas, new_m, new_v = {}, {}, {}, {}
    for n in _BIG:
        shp = w[n].shape
        r2 = lambda a: a.reshape(-1, shp[-1])
        g, d, nm, nv = _sum_adamw(recv[n], r2(w[n]), r2(m[n]), r2(v[n]), 256, "adamw_" + n)
        grads[n], deltas[n], new_m[n], new_v[n] = (a.reshape(shp) for a in (g, d, nm, nv))

    total = _sum_parts(all_small, "sum_small")
    summed = _unpack(total, [part[n].shape for n in small] + [(1, 1)])
    loss_total = summed[-1].reshape(())
    gsmall = dict(zip(small, summed[:-1]))
    for n in _SHARDED_SMALL:
        width = w[n].shape[-1]
        gsmall[n] = lax.dynamic_slice_in_dim(gsmall[n], me * width, width, axis=2)
    pk = lambda d: _pack([d[n] for n in small])
    _, d, nm, nv = _sum_adamw([pk(gsmall)[None]], pk(w), pk(m), pk(v), 4096, "adamw_small")
    shapes = [w[n].shape for n in small]
    for n, dd, mm, vv in zip(small, _unpack(d, shapes), _unpack(nm, shapes), _unpack(nv, shapes)):
        grads[n], deltas[n], new_m[n], new_v[n] = gsmall[n], dd, mm, vv
    return loss_total, grad_x, grads, deltas, new_m, new_v


def kernel(x, norm1_g, w_in, hgrn_lb_logits, hgrn_norm_g, gdn_conv_w, gdn_a_log, gdn_dt_bias, gdn_norm_g, lru_conv_w, lru_conv_b, lru_w_a, lru_b_a, lru_w_x, lru_b_x, lru_lambda, w_out, norm2_g, w_up, w_down, final_norm_g, loss_target, m_norm1_g, m_w_in, m_hgrn_lb_logits, m_hgrn_norm_g, m_gdn_conv_w, m_gdn_a_log, m_gdn_dt_bias, m_gdn_norm_g, m_lru_conv_w, m_lru_conv_b, m_lru_w_a, m_lru_b_a, m_lru_w_x, m_lru_b_x, m_lru_lambda, m_w_out, m_norm2_g, m_w_up, m_w_down, m_final_norm_g, v_norm1_g, v_w_in, v_hgrn_lb_logits, v_hgrn_norm_g, v_gdn_conv_w, v_gdn_a_log, v_gdn_dt_bias, v_gdn_norm_g, v_lru_conv_w, v_lru_conv_b, v_lru_w_a, v_lru_b_a, v_lru_w_x, v_lru_b_x, v_lru_lambda, v_w_out, v_norm2_g, v_w_up, v_w_down, v_final_norm_g):
    w = dict(zip(_WEIGHTS, (norm1_g, w_in, hgrn_lb_logits, hgrn_norm_g, gdn_conv_w, gdn_a_log, gdn_dt_bias, gdn_norm_g, lru_conv_w, lru_conv_b, lru_w_a, lru_b_a, lru_w_x, lru_b_x, lru_lambda, w_out, norm2_g, w_up, w_down, final_norm_g)))
    m = dict(zip(_WEIGHTS, (m_norm1_g, m_w_in, m_hgrn_lb_logits, m_hgrn_norm_g, m_gdn_conv_w, m_gdn_a_log, m_gdn_dt_bias, m_gdn_norm_g, m_lru_conv_w, m_lru_conv_b, m_lru_w_a, m_lru_b_a, m_lru_w_x, m_lru_b_x, m_lru_lambda, m_w_out, m_norm2_g, m_w_up, m_w_down, m_final_norm_g)))
    v = dict(zip(_WEIGHTS, (v_norm1_g, v_w_in, v_hgrn_lb_logits, v_hgrn_norm_g, v_gdn_conv_w, v_gdn_a_log, v_gdn_dt_bias, v_gdn_norm_g, v_lru_conv_w, v_lru_conv_b, v_lru_w_a, v_lru_b_a, v_lru_w_x, v_lru_b_x, v_lru_lambda, v_w_out, v_norm2_g, v_w_up, v_w_down, v_final_norm_g)))
    loss, grad_x, grads, deltas, new_m, new_v = _step(x.reshape(x.shape[1:]), loss_target.reshape(x.shape[1:]), w, m, v)
    return (loss, grad_x[None], *[grads[n] for n in _WEIGHTS], *[deltas[n] for n in _WEIGHTS],
            *[new_m[n] for n in _WEIGHTS], *[new_v[n] for n in _WEIGHTS])
```

```python
import jax
import jax.numpy as jnp
from jax import lax
from jax.experimental import pallas as pl
from jax.experimental.pallas import tpu as pltpu

F32 = jnp.float32
BF16 = jnp.bfloat16
MESH = pl.DeviceIdType.MESH

N_DEV = 8
D_MODEL = 1024
DEPTH = 4
A_HEADS, A_DIM, A_WIDTH = 4, 64, 256
B_HEADS, B_DIM, B_WIDTH = 4, 128, 512
C_WIDTH, C_BLOCKS, C_BLOCK_DIM = 256, 4, 64
D_IN = 3592
D_IN_PAD = 3840
COL_A, COL_B, COL_C, COL_G = 0, 1024, 3072, 3584
D_FF = 4096
CONV_K = 4
CHUNK = 64
SUB = 16
RG_C = 8.0
EPS = 1e-6
TINY = 1e-30
EXP_CLAMP = 80.0
GDN_SCALE = B_DIM ** -0.5
ADAM_LR, ADAM_B1, ADAM_B2, ADAM_EPS, ADAM_WD, ADAM_STEP = 0.001, 0.9, 0.999, 1e-08, 0.01, 10
VMEM_LIMIT = 56 * 1024 * 1024


def _cparams(sem=None):
    return pltpu.CompilerParams(dimension_semantics=sem, vmem_limit_bytes=VMEM_LIMIT)


_DIMS = {"nn": (((1,), (0,)), ((), ())), "nt": (((1,), (1,)), ((), ())), "tn": (((0,), (0,)), ((), ()))}


def _split_bf16(x):
    hi = x.astype(BF16)
    return hi, (x - hi.astype(F32)).astype(BF16)


def _dot(a, b, mode="nn", hi=False):
    if not hi:
        return lax.dot_general(a.astype(BF16), b.astype(BF16), _DIMS[mode], preferred_element_type=F32)
    ah, al = _split_bf16(a.astype(F32))
    bh, bl = _split_bf16(b.astype(F32))
    ka = 0 if mode == "tn" else 1
    kb = 1 if mode == "nt" else 0
    return lax.dot_general(jnp.concatenate([ah, ah, al], axis=ka), jnp.concatenate([bh, bl, bh], axis=kb),
                           _DIMS[mode], preferred_element_type=F32)


def _dot_exact_lhs(lhs, x, mode="nn"):
    l_bf16 = lhs.astype(BF16)
    x1 = x.astype(BF16)
    r1 = x - x1.astype(F32)
    x2 = r1.astype(BF16)
    x3 = (r1 - x2.astype(F32)).astype(BF16)
    ka = 0 if mode == "tn" else 1
    return lax.dot_general(jnp.concatenate([l_bf16] * 3, axis=ka), jnp.concatenate([x1, x2, x3], axis=0),
                           _DIMS[mode], preferred_element_type=F32)


def _iota2(n, m):
    return lax.broadcasted_iota(jnp.int32, (n, m), 0), lax.broadcasted_iota(jnp.int32, (n, m), 1)


def _tril(n, strict=False):
    r, c = _iota2(n, n)
    return (r > c) if strict else (r >= c)


def _sigmoid(x):
    return 1.0 / (1.0 + jnp.exp(-x))


def _softplus(x):
    return jnp.maximum(x, 0.0) + jnp.log(1.0 + jnp.exp(-jnp.abs(x)))


def _neg_expm1(z):
    series = -z * (1.0 + z * (0.5 + z * (1.0 / 6.0)))
    return jnp.where(z > -1e-2, series, 1.0 - jnp.exp(z))


def _gelu_tanh(x):
    c = 0.7978845608028654
    u = c * (x + 0.044715 * x * x * x)
    t = jnp.tanh(u)
    g = 0.5 * x * (1.0 + t)
    dg = 0.5 * (1.0 + t) + 0.5 * x * (1.0 - t * t) * c * (1.0 + 3.0 * 0.044715 * x * x)
    return g, dg


def _shift_rows(cur, halo, s, down=True):
    n = cur.shape[0]
    ridx = lax.broadcasted_iota(jnp.int32, (8, cur.shape[1]), 0)
    if down:
        main = pltpu.roll(cur, s, 0)
        fix = jnp.where(ridx < s, pltpu.roll(halo, s, 0), main[0:8])
        return jnp.concatenate([fix, main[8:]], axis=0)
    main = pltpu.roll(cur, n - s, 0)
    fix = jnp.where(ridx >= 8 - s, pltpu.roll(halo, 8 - s, 0), main[n - 8:n])
    return jnp.concatenate([main[:n - 8], fix], axis=0)


def _later_rows(dy, nxt8):
    return [_shift_rows(dy, nxt8, 3 - j, down=False) for j in range(3)] + [dy]


def _conv_bwd_rows(dy, nxt8, w):
    return sum(d * w[j:j + 1] for j, d in enumerate(_later_rows(dy, nxt8)))


def _tile_scan(a, b, reverse=False):
    n = a.shape[0]
    r = lax.broadcasted_iota(jnp.int32, a.shape, 0) % 8
    for s in (1, 2, 4):
        keep = (r < 8 - s) if reverse else (r >= s)
        shift = n - s if reverse else s
        a_sh = jnp.where(keep, pltpu.roll(a, shift, 0), 1.0)
        b_sh = jnp.where(keep, pltpu.roll(b, shift, 0), 0.0)
        b = b + a * b_sh
        a = a * a_sh
    return a, b


def _conv_fwd(cur, prev8, w):
    y = cur * w[3:4]
    for j in range(3):
        y = y + _shift_rows(cur, prev8, 3 - j, down=True) * w[j:j + 1]
    return y


def _mm_rows(a, w, mode, tm, name, residual=None, epilogue=None, up=None, norm=None, beside=None):
    parts = list(a) if isinstance(a, (list, tuple)) else [a]
    widths = [p.shape[1] for p in parts]
    t = parts[0].shape[0]
    n = w.shape[1] if mode == "nn" else w.shape[0]
    tm = min(tm, t)
    assert t % tm == 0 and all(wd % 128 == 0 for wd in widths)

    def body(*refs):
        a_refs, w_ref, rest = refs[:len(parts)], refs[len(parts)], refs[len(parts) + 1:]
        if epilogue == "rms_bwd":
            @pl.when(pl.program_id(0) == 0)
            def _():
                rest[4][...] = jnp.zeros_like(rest[4])

        for rs in ((slice(0, tm // 2), slice(tm // 2, tm)) if tm >= 512 else (slice(0, tm),)):
            y, off = None, 0
            for a_ref, width in zip(a_refs, widths):
                wk = w_ref[off:off + width, :] if mode == "nn" else w_ref[:, off:off + width]
                d = _dot(a_ref[rs, :], wk, mode)
                y = d if y is None else y + d
                off += width
            if residual is not None:
                y = y + rest[0][rs, :]
            if epilogue == "relu2":
                r = jnp.maximum(y, 0.0)
                refs[-2][rs, :] = (r * r).astype(BF16)
                refs[-1][rs, :] = y.astype(BF16)
            elif epilogue == "drelu2":
                refs[-1][rs, :] = (y * 2.0 * jnp.maximum(rest[0][rs, :].astype(F32), 0.0)).astype(BF16)
            elif epilogue == "rms_fwd":
                rinv = lax.rsqrt(jnp.mean(y * y, axis=-1, keepdims=True) + EPS)
                refs[-2][rs, :] = y
                refs[-1][rs, :] = (y * rinv * refs[-3][...]).astype(BF16)
            elif epilogue == "rms_bwd":
                x_ref, g_ref, dres_ref, dx_ref, dg_ref = rest
                xv = x_ref[rs, :]
                rinv = lax.rsqrt(jnp.mean(xv * xv, axis=-1, keepdims=True) + EPS)
                xhat = xv * rinv
                dxh = y * g_ref[...]
                dx_ref[rs, :] = dres_ref[rs, :] + rinv * (dxh - xhat * jnp.mean(dxh * xhat, axis=-1, keepdims=True))
                dg_ref[...] += jnp.sum(y * xhat, axis=0, keepdims=True)
            else:
                refs[-1][rs, :] = y

    rows = lambda width: pl.BlockSpec((tm, width), lambda i: (i, 0))
    vec = pl.BlockSpec((1, n), lambda i: (0, 0))
    ins, specs = parts + [w], [rows(wd) for wd in widths] + [pl.BlockSpec(w.shape, lambda i: (0, 0))]
    if residual is not None:
        ins.append(residual)
        specs.append(rows(n))
    if epilogue == "drelu2":
        ins.append(up)
        specs.append(rows(n))
    if epilogue == "rms_fwd":
        ins.append(norm.reshape(1, n))
        specs.append(vec)
        out_specs, out_shape = [rows(n), rows(n)], [jax.ShapeDtypeStruct((t, n), F32), jax.ShapeDtypeStruct((t, n), BF16)]
    elif epilogue == "rms_bwd":
        ins += [norm[0], norm[1].reshape(1, n), norm[2]]
        specs += [rows(n), vec, rows(n)]
        out_specs, out_shape = [rows(n), vec], [jax.ShapeDtypeStruct((t, n), F32), jax.ShapeDtypeStruct((1, n), F32)]
    elif epilogue == "relu2":
        out_specs, out_shape = [rows(n), rows(n)], [jax.ShapeDtypeStruct((t, n), BF16)] * 2
    else:
        out_specs, out_shape = rows(n), jax.ShapeDtypeStruct((t, n), BF16 if epilogue == "drelu2" else F32)
    if beside is not None:
        assert epilogue == "rms_bwd"
        return _call_beside(body, beside, t // tm, ins, grid=(t // tm,), in_specs=specs, out_specs=out_specs,
                            out_shape=out_shape, scratch_shapes=[], name=name)
    return pl.pallas_call(body, grid=(t // tm,), in_specs=specs, out_specs=out_specs, out_shape=out_shape,
                          compiler_params=_cparams(("arbitrary" if epilogue == "rms_bwd" else "parallel",)), name=name)(*ins)


MM_TN_TILE = 1024


def _mm_tn(a, b, tk, name, slab=None):
    a_parts = list(a) if isinstance(a, (list, tuple)) else [a]
    b_parts = list(b) if isinstance(b, (list, tuple)) else [b]
    wa, wb = [p.shape[1] for p in a_parts], [p.shape[1] for p in b_parts]
    t, m, n = a_parts[0].shape[0], sum(wa), sum(wb)
    tk = min(tk, t)
    assert t % tk == 0 and all(x % 128 == 0 for x in wa + wb)
    nk = t // tk

    def body(*refs):
        a_refs, b_refs = refs[:len(wa)], refs[len(wa):len(wa) + len(wb)]
        o_ref, acc = refs[-2], refs[-1]
        kk = pl.program_id(0)

        @pl.when(kk == 0)
        def _():
            acc[...] = jnp.zeros_like(acc)

        ro = 0
        for a_ref, width_a in zip(a_refs, wa):
            for r0 in range(0, width_a, MM_TN_TILE):
                rw = min(MM_TN_TILE, width_a - r0)
                av = a_ref[:, r0:r0 + rw]
                co = 0
                for b_ref, width_b in zip(b_refs, wb):
                    for c0 in range(0, width_b, MM_TN_TILE):
                        cw = min(MM_TN_TILE, width_b - c0)
                        acc[ro + r0:ro + r0 + rw, co + c0:co + c0 + cw] += _dot(av, b_ref[:, c0:c0 + cw], "tn")
                    co += width_b
            ro += width_a

        @pl.when(kk == nk - 1)
        def _():
            if slab is None:
                o_ref[...] = acc[...].astype(BF16)
            else:
                for s in range(n // slab):
                    o_ref[s] = acc[:, s * slab:(s + 1) * slab].astype(BF16)

    if slab is None:
        out_spec, out_shape = pl.BlockSpec((m, n), lambda kk: (0, 0)), jax.ShapeDtypeStruct((m, n), BF16)
    else:
        out_spec, out_shape = pl.BlockSpec((n // slab, m, slab), lambda kk: (0, 0, 0)), jax.ShapeDtypeStruct((n // slab, m, slab), BF16)
    return pl.pallas_call(
        body, grid=(nk,),
        in_specs=[pl.BlockSpec((tk, x), lambda kk: (kk, 0)) for x in wa + wb],
        out_specs=out_spec, out_shape=out_shape, scratch_shapes=[pltpu.VMEM((m, n), F32)],
        compiler_params=_cparams(("arbitrary",)), name=name)(*a_parts, *b_parts)


def _rms_fwd(x, g, tb=512, name="rms_fwd"):
    t, d = x.shape

    def body(x_ref, g_ref, h_ref):
        xv = x_ref[...]
        rinv = lax.rsqrt(jnp.mean(xv * xv, axis=-1, keepdims=True) + EPS)
        h_ref[...] = (xv * rinv * g_ref[...]).astype(BF16)

    return pl.pallas_call(
        body, grid=(t // tb,), in_specs=[pl.BlockSpec((tb, d), lambda i: (i, 0)), pl.BlockSpec((1, d), lambda i: (0, 0))],
        out_specs=pl.BlockSpec((tb, d), lambda i: (i, 0)), out_shape=jax.ShapeDtypeStruct((t, d), BF16),
        compiler_params=_cparams(("parallel",)), name=name)(x, g.reshape(1, d))


def _loss_head(x, g, target, tb=512):
    t, d = x.shape

    def body(x_ref, g_ref, t_ref, loss_ref, dx_ref, dg_ref):
        @pl.when(pl.program_id(0) == 0)
        def _():
            dg_ref[...] = jnp.zeros_like(dg_ref)
            loss_ref[...] = jnp.zeros_like(loss_ref)

        xv = x_ref[...]
        rinv = lax.rsqrt(jnp.mean(xv * xv, axis=-1, keepdims=True) + EPS)
        xhat = xv * rinv
        err = xhat * g_ref[...] - t_ref[...]
        loss_ref[...] += 0.5 * jnp.sum(jnp.mean(err * err, axis=-1, keepdims=True), axis=0, keepdims=True)
        dy = err * (1.0 / d)
        dxh = dy * g_ref[...]
        dx_ref[...] = rinv * (dxh - xhat * jnp.mean(dxh * xhat, axis=-1, keepdims=True))
        dg_ref[...] += jnp.sum(dy * xhat, axis=0, keepdims=True)

    row = pl.BlockSpec((tb, d), lambda i: (i, 0))
    vec = pl.BlockSpec((1, d), lambda i: (0, 0))
    one = pl.BlockSpec((1, 1), lambda i: (0, 0))
    return pl.pallas_call(
        body, grid=(t // tb,), in_specs=[row, vec, row], out_specs=[one, row, vec],
        out_shape=[jax.ShapeDtypeStruct((1, 1), F32), jax.ShapeDtypeStruct((t, d), F32), jax.ShapeDtypeStruct((1, d), F32)],
        compiler_params=_cparams(("arbitrary",)), name="loss_head")(x, g.reshape(1, d), target)


def _lb_fwd(logits):
    def body(l_ref, o_ref):
        lg = l_ref[...]
        e = jnp.exp(lg - jnp.max(lg, axis=0, keepdims=True))
        p = e / jnp.sum(e, axis=0, keepdims=True)
        c = jnp.zeros_like(p[0:1])
        rows = [c]
        for l in range(1, DEPTH):
            c = c + p[l:l + 1]
            rows.append(c)
        o_ref[...] = jnp.minimum(jnp.maximum(jnp.concatenate(rows, axis=0), 0.0), 1.0 - EPS)

    return pl.pallas_call(body, out_shape=jax.ShapeDtypeStruct(logits.shape, F32), name="lb_fwd")(logits)


def _lb_bwd(logits, dlb):
    def body(l_ref, d_ref, o_ref):
        lg = l_ref[...]
        e = jnp.exp(lg - jnp.max(lg, axis=0, keepdims=True))
        p = e / jnp.sum(e, axis=0, keepdims=True)
        hi = 1.0 - EPS
        c = jnp.zeros_like(p[0:1])
        dc = []
        for l in range(1, DEPTH):
            c = c + p[l:l + 1]
            gl = jnp.where(c < 0.0, 0.0, jnp.where(c == 0.0, 0.5, 1.0)) * jnp.where(c > hi, 0.0, jnp.where(c == hi, 0.5, 1.0))
            dc.append(d_ref[l:l + 1, :] * gl)
        dp = [jnp.zeros_like(c)]
        for j in range(1, DEPTH):
            s = dc[j - 1]
            for l in range(j + 1, DEPTH):
                s = s + dc[l - 1]
            dp.append(s)
        dpm = jnp.concatenate(dp, axis=0)
        o_ref[...] = p * (dpm - jnp.sum(p * dpm, axis=0, keepdims=True))

    return pl.pallas_call(body, out_shape=jax.ShapeDtypeStruct(logits.shape, F32), name="lb_bwd")(logits, dlb)


def _a_gates(qi, fi, lbh):
    sq = _sigmoid(qi)
    q = qi * sq
    e = jnp.exp(-jnp.abs(fi))
    rec = 1.0 / (1.0 + e)
    pos = fi >= 0.0
    sg = jnp.where(pos, rec, e * rec)
    sgn = jnp.where(pos, e * rec, rec)
    f = lbh + (1.0 - lbh) * sg
    logf = jnp.log(jnp.maximum(f, TINY))
    k = (1.0 - lbh) * sgn
    return q, sq, sg, sgn, f, logf, k


def _headnorm_fwd(o, g, gate_in):
    rinv = lax.rsqrt(jnp.mean(o * o, axis=-1, keepdims=True) + EPS)
    sg = _sigmoid(gate_in)
    return o * rinv * g * (gate_in * sg)


def _headnorm_bwd(dout, o, g, gate_in):
    rinv = lax.rsqrt(jnp.mean(o * o, axis=-1, keepdims=True) + EPS)
    xhat = o * rinv
    sg = _sigmoid(gate_in)
    silu = gate_in * sg
    dy = dout * silu
    dgate = dout * xhat * g * (sg * (1.0 + gate_in * (1.0 - sg)))
    dxh = dy * g
    do = rinv * (dxh - xhat * jnp.mean(dxh * xhat, axis=-1, keepdims=True))
    return do, dgate, jnp.sum(dy * xhat, axis=0, keepdims=True)


A_PAIRS, A_PAIR_W = A_HEADS // 2, 2 * A_DIM


def _lo_half(shape):
    return lax.broadcasted_iota(jnp.int32, shape, len(shape) - 1) < A_DIM


def _pair_blockdiag(x):
    lo = _lo_half(x.shape)
    return jnp.concatenate([jnp.where(lo, x, 0.0), jnp.where(lo, 0.0, x)], axis=0)


def _pair_fold(m):
    n = m.shape[0] // 2
    return jnp.where(_lo_half((n, A_PAIR_W)), m[:n], m[n:])


def _pair_norm_stats(o):
    lo = _lo_half(o.shape)
    sq = o * o
    s0 = jnp.sum(jnp.where(lo, sq, 0.0), axis=-1, keepdims=True)
    s1 = jnp.sum(sq, axis=-1, keepdims=True) - s0
    return jnp.where(lo, lax.rsqrt(s0 * (1.0 / A_DIM) + EPS), lax.rsqrt(s1 * (1.0 / A_DIM) + EPS))


def _pair_mean(x):
    lo = _lo_half(x.shape)
    s0 = jnp.sum(jnp.where(lo, x, 0.0), axis=-1, keepdims=True)
    s1 = jnp.sum(x, axis=-1, keepdims=True) - s0
    return jnp.where(lo, s0, s1) * (1.0 / A_DIM)


def _a_pair_chunk(qi, fi, v, lb2, s_bd, ltri, causal2):
    q, sq, sg, sgn, f, logf, k = _a_gates(qi, fi, lb2)
    cum = _dot_exact_lhs(ltri, logf)
    cl = cum[CHUNK - 1:CHUNK]
    ecum, ekd, cd = jnp.exp(cum), jnp.exp(cl - cum), jnp.exp(cl)
    qd, kd = q * ecum, k * ekd
    subs, rows = [], []
    for i in range(CHUNK // SUB):
        lo = i * SUB
        r = cum[lo - 1:lo] if i > 0 else jnp.zeros_like(cl)
        eq = jnp.exp(cum[lo:lo + SUB] - r)
        ek = jnp.exp(jnp.minimum(r - cum, EXP_CLAMP))
        qt = q[lo:lo + SUB] * eq
        kt_bd = _pair_blockdiag(k * ek)
        rows.append(_dot(qt, kt_bd, "nt", hi=True))
        subs.append((qt, eq, kt_bd, ek))
    attn = jnp.where(causal2, jnp.concatenate(rows, axis=0), 0.0)
    v_bd = _pair_blockdiag(v)
    o = _dot(qd, s_bd) + _dot(attn, v_bd)
    return dict(q=q, sq=sq, sg=sg, sgn=sgn, f=f, k=k, cum=cum, cl=cl, ecum=ecum, ekd=ekd, cd=cd, qd=qd, kd=kd,
                subs=subs, attn=attn, v_bd=v_bd, o=o)


def _a_fwd(proj, lb, norm_g, tb=256):
    t = proj.shape[0]
    nch = tb // CHUNK

    def body(q_ref, f_ref, i_ref, g_ref, lb_ref, ng_ref, out_ref, st_ref, s_scr):
        @pl.when(pl.program_id(0) == 0)
        def _():
            s_scr[...] = jnp.zeros_like(s_scr)

        ltri = _tril(CHUNK).astype(F32)
        r, c = _iota2(CHUNK, A_PAIR_W)
        causal2 = r >= c % CHUNK
        rb, cb = _iota2(A_PAIR_W, A_PAIR_W)
        diag_blocks = (rb < A_DIM) == (cb < A_DIM)

        def chunk(c, carry):
            rows = pl.ds(pl.multiple_of(c * CHUNK, CHUNK), CHUNK)
            ps = range(A_PAIRS)
            cols = [slice(p * A_PAIR_W, (p + 1) * A_PAIR_W) for p in ps]
            s0 = [s_scr[p] for p in ps]
            for p in ps:
                st_ref[c, p] = s0[p]
            v = [i_ref[rows, cols[p]] for p in ps]
            ch = [_a_pair_chunk(q_ref[rows, cols[p]], f_ref[rows, cols[p]], v[p], lb_ref[:, cols[p]], s0[p], ltri, causal2)
                  for p in ps]
            for p in ps:
                upd = jnp.where(diag_blocks, _dot(ch[p]["kd"], v[p], "tn"), 0.0)
                s_scr[p] = s0[p] * ch[p]["cd"].T + upd
            outs = []
            for p in ps:
                gi = g_ref[rows, cols[p]]
                outs.append(ch[p]["o"] * _pair_norm_stats(ch[p]["o"]) * ng_ref[...] * (gi * _sigmoid(gi)))
            out_ref[rows, :] = jnp.concatenate(outs, axis=1).astype(BF16)
            return carry

        lax.fori_loop(0, nch, chunk, 0, unroll=4)

    colblk = lambda j: pl.BlockSpec((tb, A_WIDTH), lambda i, j=j: (i, j))
    return pl.pallas_call(
        body, grid=(t // tb,),
        in_specs=[colblk(0), colblk(1), colblk(2), colblk(3), pl.BlockSpec((1, A_WIDTH), lambda i: (0, 0)),
                  pl.BlockSpec((1, A_PAIR_W), lambda i: (0, 0))],
        out_specs=[pl.BlockSpec((tb, A_WIDTH), lambda i: (i, 0)),
                   pl.BlockSpec((nch, A_PAIRS, A_PAIR_W, A_PAIR_W), lambda i: (i, 0, 0, 0))],
        out_shape=[jax.ShapeDtypeStruct((t, A_WIDTH), BF16),
                   jax.ShapeDtypeStruct((t // CHUNK, A_PAIRS, A_PAIR_W, A_PAIR_W), F32)],
        scratch_shapes=[pltpu.VMEM((A_PAIRS, A_PAIR_W, A_PAIR_W), F32)],
        compiler_params=_cparams(("arbitrary",)), name="hgrn_fwd")(
            proj, proj, proj, proj, lb.reshape(1, A_WIDTH), jnp.tile(norm_g.reshape(1, A_DIM), (1, 2)))


def _a_bwd(proj, lb, norm_g, states, dmixed, beside=None, tb=256):
    t = proj.shape[0]
    nch = tb // CHUNK
    nb = t // tb

    def body(q_ref, f_ref, i_ref, g_ref, lb_ref, ng_ref, st_ref, dm_ref, dp_ref, dlb_ref, dng_ref, ds_scr):
        @pl.when(pl.program_id(0) == 0)
        def _():
            ds_scr[...] = jnp.zeros_like(ds_scr)
            dlb_ref[...] = jnp.zeros_like(dlb_ref)
            dng_ref[...] = jnp.zeros_like(dng_ref)

        ltri = _tril(CHUNK).astype(F32)
        r, c = _iota2(CHUNK, A_PAIR_W)
        causal2 = r >= c % CHUNK
        rb, cb = _iota2(A_PAIR_W, A_PAIR_W)
        diag_blocks = (rb < A_DIM) == (cb < A_DIM)
        ones8 = jnp.ones((8, A_PAIR_W), F32)

        def chunk(cc, carry):
            c = nch - 1 - cc
            rows = pl.ds(pl.multiple_of(c * CHUNK, CHUNK), CHUNK)
            ps = range(A_PAIRS)
            cols = [slice(p * A_PAIR_W, (p + 1) * A_PAIR_W) for p in ps]
            qi = [q_ref[rows, cols[p]] for p in ps]
            gi = [g_ref[rows, cols[p]] for p in ps]
            v = [i_ref[rows, cols[p]] for p in ps]
            lb2 = [lb_ref[:, cols[p]] for p in ps]
            s0 = [st_ref[c, p] for p in ps]
            ds = [ds_scr[p] for p in ps]
            ch = [_a_pair_chunk(qi[p], f_ref[rows, cols[p]], v[p], lb2[p], s0[p], ltri, causal2) for p in ps]
            o = [ch[p]["o"] for p in ps]
            rinv = [_pair_norm_stats(o[p]) for p in ps]
            xhat = [o[p] * rinv[p] for p in ps]
            sgg = [_sigmoid(gi[p]) for p in ps]
            dout = [dm_ref[rows, cols[p]].astype(F32) for p in ps]
            dy = [dout[p] * (gi[p] * sgg[p]) for p in ps]
            dgi = [dout[p] * xhat[p] * ng_ref[...] * (sgg[p] * (1.0 + gi[p] * (1.0 - sgg[p]))) for p in ps]
            dxh = [dy[p] * ng_ref[...] for p in ps]
            do = [rinv[p] * (dxh[p] - xhat[p] * _pair_mean(dxh[p] * xhat[p])) for p in ps]
            dng = sum(jnp.sum(dy[p] * xhat[p], axis=0, keepdims=True) for p in ps)
            dqd = [_dot(do[p], s0[p], "nt") for p in ps]
            dattn = [jnp.where(causal2, _dot(do[p], ch[p]["v_bd"], "nt"), 0.0) for p in ps]
            dv = [_pair_fold(_dot(ch[p]["attn"], do[p], "tn")) + _dot(ch[p]["kd"], ds[p]) for p in ps]
            dkd = [_dot(v[p], ds[p], "nt") for p in ps]
            dcd = [_dot(ones8, s0[p] * ds[p], "nt", hi=True)[0:1] for p in ps]
            for p in ps:
                ds_scr[p] = jnp.where(diag_blocks, _dot(ch[p]["qd"], do[p], "tn"), 0.0) + ds[p] * ch[p]["cd"].T
            dq_i, dk_i = [], []
            for p in ps:
                dq_rows, dk = [], None
                for i, (qt, eq, kt_bd, ek) in enumerate(ch[p]["subs"]):
                    da = dattn[p][i * SUB:(i + 1) * SUB]
                    dq_rows.append(_dot(da, kt_bd, "nn", hi=True) * eq)
                    d = _pair_fold(_dot(da, qt, "tn", hi=True)) * ek
                    dk = d if dk is None else dk + d
                dq_i.append(jnp.concatenate(dq_rows, axis=0))
                dk_i.append(dk)
            dq = [dqd[p] * ch[p]["ecum"] + dq_i[p] for p in ps]
            dk = [dkd[p] * ch[p]["ekd"] + dk_i[p] for p in ps]
            dkk = [dkd[p] * ch[p]["kd"] for p in ps]
            dcum = [dqd[p] * ch[p]["qd"] - dkk[p] + ch[p]["q"] * dq_i[p] - ch[p]["k"] * dk_i[p] for p in ps]
            dcl = [jnp.sum(dkk[p], axis=0, keepdims=True) + dcd[p] * ch[p]["cd"] for p in ps]
            dlogf = [_dot_exact_lhs(ltri, dcum[p], "tn") + dcl[p] for p in ps]
            dfv = [jnp.where(ch[p]["f"] > TINY, dlogf[p] / ch[p]["f"], 0.0) for p in ps]
            dfi = [dfv[p] * (1.0 - lb2[p]) * ch[p]["sg"] * (1.0 - ch[p]["sg"])
                   - dk[p] * (1.0 - lb2[p]) * ch[p]["sgn"] * (1.0 - ch[p]["sgn"]) for p in ps]
            dlbs = [jnp.sum(dfv[p] * (1.0 - ch[p]["sg"]) - dk[p] * ch[p]["sgn"], axis=0, keepdims=True) for p in ps]
            dqs = [dq[p] * (ch[p]["sq"] * (1.0 + qi[p] * (1.0 - ch[p]["sq"]))) for p in ps]
            dp_ref[rows, :] = jnp.concatenate(dqs + dfi + dv + dgi, axis=1).astype(BF16)
            dlb_ref[...] += jnp.concatenate(dlbs, axis=1)
            dng_ref[...] += dng
            return carry

        lax.fori_loop(0, nch, chunk, 0, unroll=4)

    colblk = lambda j: pl.BlockSpec((tb, A_WIDTH), lambda i, j=j: (nb - 1 - i, j))
    vec = lambda n: pl.BlockSpec((1, n), lambda i: (0, 0))
    return _call_beside(
        body, beside, nb,
        (proj, proj, proj, proj, lb.reshape(1, A_WIDTH), jnp.tile(norm_g.reshape(1, A_DIM), (1, 2)), states, dmixed), grid=(nb,),
        in_specs=[colblk(0), colblk(1), colblk(2), colblk(3), vec(A_WIDTH), vec(A_PAIR_W),
                  pl.BlockSpec((nch, A_PAIRS, A_PAIR_W, A_PAIR_W), lambda i: (nb - 1 - i, 0, 0, 0)), colblk(0)],
        out_specs=[pl.BlockSpec((tb, 4 * A_WIDTH), lambda i: (nb - 1 - i, 0)), vec(A_WIDTH), vec(A_PAIR_W)],
        out_shape=[jax.ShapeDtypeStruct((t, 4 * A_WIDTH), BF16), jax.ShapeDtypeStruct((1, A_WIDTH), F32),
                   jax.ShapeDtypeStruct((1, A_PAIR_W), F32)],
        scratch_shapes=[pltpu.VMEM((A_PAIRS, A_PAIR_W, A_PAIR_W), F32)], name="hgrn_bwd")


def _gate_lane_masks(shape):
    lane = lax.broadcasted_iota(jnp.int32, shape, 1)
    return lane < B_HEADS, (lane >= B_HEADS) & (lane < 2 * B_HEADS)


def _b_pre_fwd(proj, conv_w, alog_row, dtb_row, tb=512):
    t = proj.shape[0]
    cb0 = COL_B // B_WIDTH

    def body(q_ref, k_ref, v_ref, qp_ref, kp_ref, vp_ref, w_ref, gi_ref, al_ref, dt_ref, qkv_ref, gates_ref, y_ref):
        first = pl.program_id(0) == 0
        for part, (c_ref, p_ref) in enumerate(((q_ref, qp_ref), (k_ref, kp_ref), (v_ref, vp_ref))):
            cols = slice(part * B_WIDTH, (part + 1) * B_WIDTH)
            prev = jnp.where(first, 0.0, p_ref[...])
            y = _conv_fwd(c_ref[...], prev, w_ref[:, cols])
            y_ref[:, cols] = y
            s = y * _sigmoid(y)
            if part < 2:
                outs = []
                for h in range(B_HEADS):
                    sh = s[:, h * B_DIM:(h + 1) * B_DIM]
                    outs.append(sh * lax.rsqrt(jnp.sum(sh * sh, axis=-1, keepdims=True) + EPS))
                s = jnp.concatenate(outs, axis=1)
            qkv_ref[:, cols] = s
        g = gi_ref[...]
        is_b, is_a = _gate_lane_masks(g.shape)
        la = -jnp.exp(al_ref[...]) * _softplus(g + dt_ref[...])
        gates_ref[...] = jnp.where(is_b, _sigmoid(g), jnp.where(is_a, la, 0.0))

    cur = lambda j: pl.BlockSpec((tb, B_WIDTH), lambda i, j=j: (i, cb0 + j))
    prv = lambda j: pl.BlockSpec((8, B_WIDTH), lambda i, j=j: (jnp.maximum(i * (tb // 8) - 1, 0), cb0 + j))
    vec = pl.BlockSpec((1, 128), lambda i: (0, 0))
    return pl.pallas_call(
        body, grid=(t // tb,),
        in_specs=[cur(0), cur(1), cur(2), prv(0), prv(1), prv(2), pl.BlockSpec((CONV_K, 3 * B_WIDTH), lambda i: (0, 0)),
                  pl.BlockSpec((tb, 128), lambda i: (i, COL_G // 128)), vec, vec],
        out_specs=[pl.BlockSpec((tb, 3 * B_WIDTH), lambda i: (i, 0)), pl.BlockSpec((tb, 128), lambda i: (i, 0)),
                   pl.BlockSpec((tb, 3 * B_WIDTH), lambda i: (i, 0))],
        out_shape=[jax.ShapeDtypeStruct((t, 3 * B_WIDTH), F32), jax.ShapeDtypeStruct((t, 128), F32),
                   jax.ShapeDtypeStruct((t, 3 * B_WIDTH), F32)],
        compiler_params=_cparams(("parallel",)), name="gdn_pre_fwd")(proj, proj, proj, proj, proj, proj, conv_w, proj, alog_row, dtb_row)


def _inv_unit_lower(amats):
    r, c = _iota2(CHUNK, CHUNK)
    eye = jnp.where(r == c, 1.0, 0.0)
    ps = [eye - a for a in amats]
    aks = amats
    for _ in range(5):
        aks = [_dot(ak, ak, hi=True) for ak in aks]
        ps = [p + _dot(p, ak, hi=True) for p, ak in zip(ps, aks)]
    return ps


def _b_local(qs, ks, vs, betas, gcs, grows, gls, solve=True):
    hs = range(len(qs))
    causal, strict = _tril(CHUNK), _tril(CHUNK, strict=True)
    decay = [jnp.where(causal, jnp.exp(jnp.minimum(gcs[h] - grows[h], 0.0)), 0.0) for h in hs]
    kb = [ks[h] * betas[h] for h in hs]
    kk = [_dot(kb[h], ks[h], "nt") for h in hs]
    qkr = [_dot(qs[h], ks[h], "nt") for h in hs]
    eg = [jnp.exp(gcs[h]) for h in hs]
    bv = [vs[h] * betas[h] for h in hs]
    kg = [kb[h] * eg[h] for h in hs]
    qk = [qkr[h] * decay[h] for h in hs]
    qd = [qs[h] * eg[h] for h in hs]
    ekd = [jnp.exp(gls[h] - gcs[h]) for h in hs]
    kd = [ks[h] * ekd[h] for h in hs]
    cd = [jnp.exp(gls[h]) for h in hs]
    loc = dict(decay=decay, kb=kb, kk=kk, eg=eg, bv=bv, kg=kg, qkr=qkr, qk=qk, qd=qd, ekd=ekd, kd=kd, cd=cd)
    if solve:
        tinv = _inv_unit_lower([jnp.where(strict, kk[h] * decay[h], 0.0) for h in hs])
        loc.update(tinv=tinv, u=[_dot(tinv[h], bv[h], hi=True) for h in hs], w=[_dot(tinv[h], kg[h], hi=True) for h in hs])
    return loc


def _b_state(loc, ids, s0s):
    n = range(len(ids))
    ws = [_dot(loc["w"][ids[j]], s0s[j]) for j in n]
    qs0 = [_dot(loc["qd"][ids[j]], s0s[j]) for j in n]
    vn = [loc["u"][ids[j]] - ws[j] for j in n]
    o = [qs0[j] + _dot(loc["qk"][ids[j]], vn[j]) for j in n]
    s1 = [s0s[j] * loc["cd"][ids[j]] + _dot(loc["kd"][ids[j]], vn[j], "tn") for j in n]
    return vn, o, s1


def _b_fwd(qkv, gates, proj, norm_g, beside=None, tb=256):
    t = qkv.shape[0]
    nch = tb // CHUNK

    def body(q_ref, k_ref, v_ref, ga_ref, z_ref, ng_ref, out_ref, st_ref, ti_ref, w_ref, vn_ref, o_ref, s_scr):
        @pl.when(pl.program_id(0) == 0)
        def _():
            s_scr[...] = jnp.zeros_like(s_scr)

        ltri = _tril(CHUNK).astype(F32)

        hs = range(B_HEADS)
        cols = [slice(h * B_DIM, (h + 1) * B_DIM) for h in hs]

        def pair(p, carry):
            cs = [2 * p, 2 * p + 1]
            rows = [pl.ds(pl.multiple_of(c * CHUNK, CHUNK), CHUNK) for c in cs]
            ga = [ga_ref[r, :] for r in rows]
            gcum = [_dot_exact_lhs(ltri, g) for g in ga]
            gcum_t = [g.T for g in gcum]
            items = [(i, h) for i in range(2) for h in hs]
            loc = _b_local([q_ref[rows[i], cols[h]] * GDN_SCALE for i, h in items], [k_ref[rows[i], cols[h]] for i, h in items],
                           [v_ref[rows[i], cols[h]] for i, h in items], [ga[i][:, h:h + 1] for i, h in items],
                           [gcum[i][:, B_HEADS + h:B_HEADS + h + 1] for i, h in items],
                           [gcum_t[i][B_HEADS + h:B_HEADS + h + 1, :] for i, h in items],
                           [gcum[i][CHUNK - 1:CHUNK, B_HEADS + h:B_HEADS + h + 1] for i, h in items])
            s0s = [s_scr[h] for h in hs]
            for i in range(2):
                ids = [i * B_HEADS + h for h in hs]
                for h in hs:
                    st_ref[cs[i], h] = s0s[h]
                    ti_ref[cs[i], h] = loc["tinv"][ids[h]]
                vn, o, s0s = _b_state(loc, ids, s0s)
                w_ref[rows[i], :] = jnp.concatenate([loc["w"][j] for j in ids], axis=1)
                vn_ref[rows[i], :] = jnp.concatenate(vn, axis=1)
                o_ref[rows[i], :] = jnp.concatenate(o, axis=1)
                outs = [_headnorm_fwd(o[h], ng_ref[...], z_ref[rows[i], cols[h]]) for h in hs]
                out_ref[rows[i], :] = jnp.concatenate(outs, axis=1).astype(BF16)
            for h in hs:
                s_scr[h] = s0s[h]
            return carry

        lax.fori_loop(0, nch // 2, pair, 0, unroll=2)

    part = lambda j: pl.BlockSpec((tb, B_WIDTH), lambda i, j=j: (i, j))
    wide = pl.BlockSpec((tb, B_WIDTH), lambda i: (i, 0))
    wide_shape = jax.ShapeDtypeStruct((t, B_WIDTH), F32)
    return _call_beside(
        body, beside, t // tb, (qkv, qkv, qkv, gates, proj, norm_g.reshape(1, B_DIM)), grid=(t // tb,),
        in_specs=[part(0), part(1), part(2), pl.BlockSpec((tb, 128), lambda i: (i, 0)),
                  pl.BlockSpec((tb, B_WIDTH), lambda i: (i, COL_B // B_WIDTH + 3)), pl.BlockSpec((1, B_DIM), lambda i: (0, 0))],
        out_specs=[wide, pl.BlockSpec((nch, B_HEADS, B_DIM, B_DIM), lambda i: (i, 0, 0, 0)),
                   pl.BlockSpec((nch, B_HEADS, CHUNK, CHUNK), lambda i: (i, 0, 0, 0)), wide, wide, wide],
        out_shape=[jax.ShapeDtypeStruct((t, B_WIDTH), BF16), jax.ShapeDtypeStruct((t // CHUNK, B_HEADS, B_DIM, B_DIM), F32),
                   jax.ShapeDtypeStruct((t // CHUNK, B_HEADS, CHUNK, CHUNK), F32), wide_shape, wide_shape, wide_shape],
        scratch_shapes=[pltpu.VMEM((B_HEADS, B_DIM, B_DIM), F32)], name="gdn_fwd")


def _b_bwd(qkv, gates, proj, norm_g, states, fwd_saved, dmixed, beside=None, tb=256):
    t = qkv.shape[0]
    nch = tb // CHUNK
    nb = t // tb

    def body(q_ref, k_ref, v_ref, ga_ref, z_ref, ng_ref, st_ref, ti_ref, w_ref, vn_ref, o_ref, dm0_ref, dm1_ref,
             dqkv_ref, dga_ref, dz_ref, dng_ref, ds_scr):
        @pl.when(pl.program_id(0) == 0)
        def _():
            ds_scr[...] = jnp.zeros_like(ds_scr)
            dng_ref[...] = jnp.zeros_like(dng_ref)

        ltri = _tril(CHUNK).astype(F32)
        strict = _tril(CHUNK, strict=True)
        lane = lax.broadcasted_iota(jnp.int32, (CHUNK, 128), 1)
        lane1 = lax.broadcasted_iota(jnp.int32, (1, 128), 1)

        nh = range(B_HEADS)
        cols = [slice(h * B_DIM, (h + 1) * B_DIM) for h in nh]
        rsum = lambda a: jnp.sum(a, axis=-1, keepdims=True)

        def pair(p, carry):
            cs = [nch - 1 - 2 * p, nch - 2 - 2 * p]
            crow = [pl.ds(pl.multiple_of(c * CHUNK, CHUNK), CHUNK) for c in cs]
            gas = [ga_ref[r, :] for r in crow]
            gcum = [_dot_exact_lhs(ltri, g) for g in gas]
            gcum_t = [g.T for g in gcum]
            items = [(i, h) for i in range(2) for h in nh]
            hs = range(len(items))
            q = [q_ref[crow[i], cols[h]] * GDN_SCALE for i, h in items]
            k = [k_ref[crow[i], cols[h]] for i, h in items]
            v = [v_ref[crow[i], cols[h]] for i, h in items]
            z = [z_ref[crow[i], cols[h]] for i, h in items]
            beta = [gas[i][:, h:h + 1] for i, h in items]
            s0 = [st_ref[cs[i], h] for i, h in items]
            r = _b_local(q, k, v, beta, [gcum[i][:, B_HEADS + h:B_HEADS + h + 1] for i, h in items],
                         [gcum_t[i][B_HEADS + h:B_HEADS + h + 1, :] for i, h in items],
                         [gcum[i][CHUNK - 1:CHUNK, B_HEADS + h:B_HEADS + h + 1] for i, h in items], solve=False)
            tinv = [ti_ref[cs[i], h] for i, h in items]
            w = [w_ref[crow[i], cols[h]] for i, h in items]
            vn = [vn_ref[crow[i], cols[h]] for i, h in items]
            decay, eg, qd, kd, kb, cd = (r[n] for n in ("decay", "eg", "qd", "kd", "kb", "cd"))
            dms = [(dm0_ref if h < 2 else dm1_ref)[crow[i], (h % 2) * B_DIM:(h % 2 + 1) * B_DIM].astype(F32) for i, h in items]
            hn = [_headnorm_bwd(dms[j], o_ref[crow[i], cols[h]], ng_ref[...], z[j]) for j, (i, h) in enumerate(items)]
            do = [hn[j][0] for j in hs]
            dvn_o = [_dot(r["qk"][j], do[j], "tn") for j in hs]
            dqk = [_dot(do[j], vn[j], "nt") for j in hs]
            dqd = [_dot(do[j], s0[j], "nt") for j in hs]
            ds_o = [_dot(qd[j], do[j], "tn") for j in hs]
            ds = [ds_scr[h] for h in nh]
            dvn, dkd, dcd = [None] * 8, [None] * 8, [None] * 8
            for i in range(2):
                for h in nh:
                    j = i * B_HEADS + h
                    dvn[j] = dvn_o[j] + _dot(kd[j], ds[h])
                    dkd[j] = _dot(vn[j], ds[h], "nt")
                    dcd[j] = jnp.sum(jnp.sum(s0[j] * ds[h], axis=0, keepdims=True), axis=1, keepdims=True)
                ds = [ds_o[i * B_HEADS + h] + ds[h] * cd[i * B_HEADS + h] - _dot(w[i * B_HEADS + h], dvn[i * B_HEADS + h], "tn")
                      for h in nh]
            for h in nh:
                ds_scr[h] = ds[h]
            dw = [-_dot(dvn[j], s0[j], "nt") for j in hs]
            dbv = [_dot(tinv[h], dvn[h], "tn", hi=True) for h in hs]
            dkg = [_dot(tinv[h], dw[h], "tn", hi=True) for h in hs]
            dt = [_dot(dvn[h], r["bv"][h], "nt", hi=True) + _dot(dw[h], r["kg"][h], "nt", hi=True) for h in hs]
            tdt = [_dot(tinv[h], dt[h], "tn", hi=True) for h in hs]
            da = [jnp.where(strict, -_dot(tdt[h], tinv[h], "nt", hi=True), 0.0) for h in hs]
            dm = [da[h] * decay[h] for h in hs]
            dn = [dqk[h] * decay[h] for h in hs]
            e = [(da[h] * r["kk"][h] + dqk[h] * r["qkr"][h]) * decay[h] for h in hs]
            dkb = [_dot(dm[h], k[h]) + dkg[h] * eg[h] for h in hs]
            dk = [_dot(dm[h], kb[h], "tn") + _dot(dn[h], q[h], "tn") + dkd[h] * r["ekd"][h] + dkb[h] * beta[h] for h in hs]
            dq = [_dot(dn[h], k[h]) + dqd[h] * eg[h] for h in hs]
            tkd = [rsum(dkd[h] * kd[h]) for h in hs]
            dgc = [rsum(e[h]) - rsum(e[h].T) + rsum(dqd[h] * qd[h]) - tkd[h] + rsum(dkg[h] * r["kg"][h]) for h in hs]
            dgl = [jnp.sum(tkd[h], axis=0, keepdims=True) + dcd[h] * cd[h] for h in hs]
            dbeta = [rsum(dbv[h] * v[h]) + rsum(dkb[h] * k[h]) for h in hs]
            for i in range(2):
                ids = [i * B_HEADS + h for h in nh]
                dbeta_m = sum(jnp.where(lane == h, dbeta[ids[h]], 0.0) for h in nh)
                dgc_m = sum(jnp.where(lane == B_HEADS + h, dgc[ids[h]], 0.0) for h in nh)
                dgl_m = sum(jnp.where(lane1 == B_HEADS + h, dgl[ids[h]], 0.0) for h in nh)
                dqkv_ref[crow[i], :] = jnp.concatenate(
                    [dq[j] * GDN_SCALE for j in ids] + [dk[j] for j in ids] + [dbv[j] * beta[j] for j in ids], axis=1)
                dz_ref[crow[i], :] = jnp.concatenate([hn[j][1] for j in ids], axis=1).astype(BF16)
                dga_ref[crow[i], :] = dbeta_m + _dot_exact_lhs(ltri, dgc_m, "tn") + dgl_m
            dng_ref[...] += sum(hn[j][2] for j in hs)
            return carry

        lax.fori_loop(0, nch // 2, pair, 0, unroll=2)

    part = lambda j: pl.BlockSpec((tb, B_WIDTH), lambda i, j=j: (nb - 1 - i, j))
    rowblk = lambda w, j=0: pl.BlockSpec((tb, w), lambda i, j=j: (nb - 1 - i, j))
    return _call_beside(
        body, beside, nb, (qkv, qkv, qkv, gates, proj, norm_g.reshape(1, B_DIM), states, *fwd_saved, dmixed, dmixed), grid=(nb,),
        in_specs=[part(0), part(1), part(2), rowblk(128), rowblk(B_WIDTH, COL_B // B_WIDTH + 3),
                  pl.BlockSpec((1, B_DIM), lambda i: (0, 0)),
                  pl.BlockSpec((nch, B_HEADS, B_DIM, B_DIM), lambda i: (nb - 1 - i, 0, 0, 0)),
                  pl.BlockSpec((nch, B_HEADS, CHUNK, CHUNK), lambda i: (nb - 1 - i, 0, 0, 0)),
                  rowblk(B_WIDTH), rowblk(B_WIDTH), rowblk(B_WIDTH), rowblk(256, 1), rowblk(256, 2)],
        out_specs=[rowblk(3 * B_WIDTH), rowblk(128), rowblk(B_WIDTH), pl.BlockSpec((1, B_DIM), lambda i: (0, 0))],
        out_shape=[jax.ShapeDtypeStruct((t, 3 * B_WIDTH), F32), jax.ShapeDtypeStruct((t, 128), F32),
                   jax.ShapeDtypeStruct((t, B_WIDTH), BF16), jax.ShapeDtypeStruct((1, B_DIM), F32)],
        scratch_shapes=[pltpu.VMEM((B_HEADS, B_DIM, B_DIM), F32)], name="gdn_bwd")


def _b_pre_bwd(proj, y3, conv_w, alog_row, dtb_row, dqkv, dgates, tb=512):
    t = proj.shape[0]
    nb = t // tb
    cb0 = COL_B // B_WIDTH

    def body(q_ref, k_ref, v_ref, y_ref, w_ref, gi_ref, al_ref, dt_ref, dqkv_ref, dga_ref,
             dy_ref, dgi_ref, dw_ref, dal_ref, ddt_ref, nxt_scr):
        step_id = pl.program_id(0)

        @pl.when(step_id == 0)
        def _():
            dw_ref[...] = jnp.zeros_like(dw_ref)
            dal_ref[...] = jnp.zeros_like(dal_ref)
            ddt_ref[...] = jnp.zeros_like(ddt_ref)

        for part, c_ref in enumerate((q_ref, k_ref, v_ref)):
            cols = slice(part * B_WIDTH, (part + 1) * B_WIDTH)
            cur = c_ref[...]
            w = w_ref[:, cols]
            y = y_ref[:, cols]
            sg = _sigmoid(y)
            s = y * sg
            dsn = dqkv_ref[:, cols]
            if part < 2:
                outs = []
                for h in range(B_HEADS):
                    hc = slice(h * B_DIM, (h + 1) * B_DIM)
                    sh, dh = s[:, hc], dsn[:, hc]
                    rq = lax.rsqrt(jnp.sum(sh * sh, axis=-1, keepdims=True) + EPS)
                    nh = sh * rq
                    outs.append(rq * (dh - nh * jnp.sum(dh * nh, axis=-1, keepdims=True)))
                dsn = jnp.concatenate(outs, axis=1)
            dy = dsn * (sg * (1.0 + y * (1.0 - sg)))
            later = _later_rows(dy, jnp.where(step_id == 0, 0.0, nxt_scr[:, cols]))
            dy_ref[:, cols] = sum(later[j] * w[j:j + 1] for j in range(CONV_K)).astype(BF16)
            nxt_scr[:, cols] = dy[0:8]
            dw_ref[:, cols] += jnp.concatenate([jnp.sum(cur * later[j], axis=0, keepdims=True) for j in range(CONV_K)], axis=0)
        g = gi_ref[...]
        dga = dga_ref[...]
        is_b, is_a = _gate_lane_masks(g.shape)
        beta = _sigmoid(g)
        pre = g + dt_ref[...]
        ea = jnp.exp(al_ref[...])
        la = -ea * _softplus(pre)
        dpre = jnp.where(is_a, dga * (-ea) * _sigmoid(pre), 0.0)
        dgi_ref[...] = jnp.where(is_b, dga * beta * (1.0 - beta), dpre).astype(BF16)
        dal_ref[...] += jnp.sum(jnp.where(is_a, dga * la, 0.0), axis=0, keepdims=True)
        ddt_ref[...] += jnp.sum(dpre, axis=0, keepdims=True)

    cur = lambda j: pl.BlockSpec((tb, B_WIDTH), lambda i, j=j: (nb - 1 - i, cb0 + j))
    vec = pl.BlockSpec((1, 128), lambda i: (0, 0))
    wspec = pl.BlockSpec((CONV_K, 3 * B_WIDTH), lambda i: (0, 0))
    rowblk = lambda width, j=0: pl.BlockSpec((tb, width), lambda i, j=j: (nb - 1 - i, j))
    return pl.pallas_call(
        body, grid=(nb,),
        in_specs=[cur(0), cur(1), cur(2), rowblk(3 * B_WIDTH), wspec, rowblk(128, COL_G // 128), vec, vec,
                  rowblk(3 * B_WIDTH), rowblk(128)],
        out_specs=[rowblk(3 * B_WIDTH), rowblk(128), wspec, vec, vec],
        out_shape=[jax.ShapeDtypeStruct((t, 3 * B_WIDTH), BF16), jax.ShapeDtypeStruct((t, 128), BF16),
                   jax.ShapeDtypeStruct((CONV_K, 3 * B_WIDTH), F32), jax.ShapeDtypeStruct((1, 128), F32), jax.ShapeDtypeStruct((1, 128), F32)],
        scratch_shapes=[pltpu.VMEM((8, 3 * B_WIDTH), F32)],
        compiler_params=_cparams(("arbitrary",)), name="gdn_pre_bwd")(
            proj, proj, proj, y3, conv_w, proj, alog_row, dtb_row, dqkv, dgates)


def _c_gates(xc, wa_ref, ba_ref, wx_ref, bx_ref, lam_ref, is_row0):
    r = _sigmoid(_dot(xc, wa_ref[...]) + ba_ref[...])
    i = _sigmoid(_dot(xc, wx_ref[...]) + bx_ref[...])
    sp = _softplus(-lam_ref[...])
    log_a = -RG_C * r * sp
    a = jnp.exp(log_a)
    m2 = _neg_expm1(2.0 * log_a)
    mult = jnp.where(is_row0, 1.0, jnp.sqrt(jnp.maximum(m2, EPS)))
    return r, i, sp, log_a, a, m2, mult


def _row0_mask(tb, first):
    ridx = lax.broadcasted_iota(jnp.int32, (tb, C_WIDTH), 0)
    return (ridx == 0) & first


def _c_fwd(proj, conv_w, conv_b, wa, ba, wx, bx, lam, tb=512):
    t = proj.shape[0]
    cbx = COL_C // C_WIDTH

    def body(x_ref, xp_ref, y_ref, w_ref, cb_ref, wa_ref, ba_ref, wx_ref, bx_ref, lam_ref, out_ref, h_ref, a_scr, b_scr, h_scr):
        first = pl.program_id(0) == 0

        @pl.when(first)
        def _():
            h_scr[...] = jnp.zeros_like(h_scr)

        prev = jnp.where(first, 0.0, xp_ref[...])
        xc = _conv_fwd(x_ref[...], prev, w_ref[...]) + cb_ref[...]
        _, i, _, _, a, _, mult = _c_gates(xc, wa_ref, ba_ref, wx_ref, bx_ref, lam_ref, _row0_mask(tb, first))
        ta, tb_ = _tile_scan(a, mult * i * xc)
        a_scr[...] = ta
        b_scr[...] = tb_

        def step(blk, h):
            rows = pl.ds(pl.multiple_of(blk * 8, 8), 8)
            h_ref[rows, :] = jnp.broadcast_to(h, (8, C_WIDTH))
            return a_scr[rows, :][7:8] * h + b_scr[rows, :][7:8]

        h_scr[...] = lax.fori_loop(0, tb // 8, step, h_scr[...], unroll=8)
        hs = ta * h_ref[...] + tb_
        h_ref[...] = hs
        gl, _ = _gelu_tanh(y_ref[...])
        out_ref[...] = (gl * hs).astype(BF16)

    vec = pl.BlockSpec((1, C_WIDTH), lambda i: (0, 0))
    mat = pl.BlockSpec((C_WIDTH, C_WIDTH), lambda i: (0, 0))
    row = pl.BlockSpec((tb, C_WIDTH), lambda i: (i, 0))
    return pl.pallas_call(
        body, grid=(t // tb,),
        in_specs=[pl.BlockSpec((tb, C_WIDTH), lambda i: (i, cbx)),
                  pl.BlockSpec((8, C_WIDTH), lambda i: (jnp.maximum(i * (tb // 8) - 1, 0), cbx)),
                  pl.BlockSpec((tb, C_WIDTH), lambda i: (i, cbx + 1)),
                  pl.BlockSpec((CONV_K, C_WIDTH), lambda i: (0, 0)), vec, mat, vec, mat, vec, vec],
        out_specs=[row, row],
        out_shape=[jax.ShapeDtypeStruct((t, C_WIDTH), BF16), jax.ShapeDtypeStruct((t, C_WIDTH), F32)],
        scratch_shapes=[pltpu.VMEM((tb, C_WIDTH), F32), pltpu.VMEM((tb, C_WIDTH), F32), pltpu.VMEM((1, C_WIDTH), F32)],
        compiler_params=_cparams(("arbitrary",)), name="lru_fwd")(proj, proj, proj, conv_w, conv_b, wa, ba, wx, bx, lam)


def _c_bwd(proj, conv_w, conv_b, wa, ba, wx, bx, lam, hs, dmixed, tb=512):
    t = proj.shape[0]
    nb = t // tb
    cbx = COL_C // C_WIDTH

    def body(x_ref, xp_ref, y_ref, w_ref, cb_ref, wa_ref, ba_ref, wx_ref, bx_ref, lam_ref, h_ref, hp_ref, dm_ref,
             dxc_ref, dyg_ref, dw_ref, dcb_ref, dwa_ref, dba_ref, dwx_ref, dbx_ref, dlam_ref, g_scr, a_scr, cin_scr, c_scr, nxt_scr):
        step_id = pl.program_id(0)
        first = step_id == nb - 1

        @pl.when(step_id == 0)
        def _():
            c_scr[...] = jnp.zeros_like(c_scr)
            for ref in (dw_ref, dcb_ref, dwa_ref, dba_ref, dwx_ref, dbx_ref, dlam_ref):
                ref[...] = jnp.zeros_like(ref)

        cur = x_ref[...]
        prev = jnp.where(first, 0.0, xp_ref[...])
        w = w_ref[...]
        shifted = [_shift_rows(cur, prev, 3 - j, down=True) for j in range(3)] + [cur]
        xc = shifted[0] * w[0:1] + shifted[1] * w[1:2] + shifted[2] * w[2:3] + shifted[3] * w[3:4] + cb_ref[...]
        row0 = _row0_mask(tb, first)
        r, i, sp, log_a, a, m2, mult = _c_gates(xc, wa_ref, ba_ref, wx_ref, bx_ref, lam_ref, row0)
        h = h_ref[...]
        hprev = _shift_rows(h, jnp.where(first, 0.0, hp_ref[...]), 1, down=True)
        gl, dgl = _gelu_tanh(y_ref[...])
        dm = dm_ref[...].astype(F32)
        dyg_ref[...] = (dm * h * dgl).astype(BF16)
        dout = dm * gl
        ta, te = _tile_scan(a, a * dout, reverse=True)
        a_scr[...] = ta
        g_scr[...] = te

        def step(blk, carry):
            rows = pl.ds(pl.multiple_of((tb // 8 - 1 - blk) * 8, 8), 8)
            cin_scr[rows, :] = jnp.broadcast_to(carry, (8, C_WIDTH))
            return a_scr[rows, :][0:1] * carry + g_scr[rows, :][0:1]

        c_scr[...] = lax.fori_loop(0, tb // 8, step, c_scr[...], unroll=8)
        cin = cin_scr[...]
        cout = ta * cin + te
        last_in_tile = lax.broadcasted_iota(jnp.int32, (tb, C_WIDTH), 0) % 8 == 7
        dbx = dout + jnp.where(last_in_tile, cin, pltpu.roll(cout, tb - 1, 0))
        da = dbx * hprev
        dmult = jnp.where(row0, 0.0, dbx * i * xc)
        di = dbx * mult * xc
        dxc = dbx * mult * i
        dm2 = jnp.where(m2 > EPS, dmult * 0.5 / mult, 0.0)
        dlog_a = da * a - 2.0 * a * a * dm2
        dr = dlog_a * (-RG_C) * sp
        dlam_ref[...] += jnp.sum(dlog_a * (-RG_C) * r, axis=0, keepdims=True) * (-_sigmoid(-lam_ref[...]))
        dpa = dr * r * (1.0 - r)
        dpx = di * i * (1.0 - i)
        dba_ref[...] += jnp.sum(dpa, axis=0, keepdims=True)
        dbx_ref[...] += jnp.sum(dpx, axis=0, keepdims=True)
        dwa_ref[...] += _dot(xc, dpa, "tn")
        dwx_ref[...] += _dot(xc, dpx, "tn")
        dxc = dxc + _dot(dpa, wa_ref[...], "nt") + _dot(dpx, wx_ref[...], "nt")
        dxc_ref[...] = _conv_bwd_rows(dxc, jnp.where(step_id == 0, 0.0, nxt_scr[...]), w).astype(BF16)
        nxt_scr[...] = dxc[0:8]
        dcb_ref[...] += jnp.sum(dxc, axis=0, keepdims=True)
        dw_ref[...] += jnp.concatenate([jnp.sum(shifted[j] * dxc, axis=0, keepdims=True) for j in range(CONV_K)], axis=0)

    vec = pl.BlockSpec((1, C_WIDTH), lambda i: (0, 0))
    mat = pl.BlockSpec((C_WIDTH, C_WIDTH), lambda i: (0, 0))
    cw = pl.BlockSpec((CONV_K, C_WIDTH), lambda i: (0, 0))
    row = lambda j=0: pl.BlockSpec((tb, C_WIDTH), lambda i, j=j: (nb - 1 - i, j))
    halo = lambda j=0: pl.BlockSpec((8, C_WIDTH), lambda i, j=j: (jnp.maximum((nb - 1 - i) * (tb // 8) - 1, 0), j))
    return pl.pallas_call(
        body, grid=(nb,),
        in_specs=[row(cbx), halo(cbx), row(cbx + 1), cw, vec, mat, vec, mat, vec, vec, row(), halo(), row(3)],
        out_specs=[row(), row(), cw, vec, mat, vec, mat, vec, vec],
        out_shape=[jax.ShapeDtypeStruct((t, C_WIDTH), BF16), jax.ShapeDtypeStruct((t, C_WIDTH), BF16),
                   jax.ShapeDtypeStruct((CONV_K, C_WIDTH), F32), jax.ShapeDtypeStruct((1, C_WIDTH), F32),
                   jax.ShapeDtypeStruct((C_WIDTH, C_WIDTH), F32), jax.ShapeDtypeStruct((1, C_WIDTH), F32),
                   jax.ShapeDtypeStruct((C_WIDTH, C_WIDTH), F32), jax.ShapeDtypeStruct((1, C_WIDTH), F32),
                   jax.ShapeDtypeStruct((1, C_WIDTH), F32)],
        scratch_shapes=[pltpu.VMEM((tb, C_WIDTH), F32), pltpu.VMEM((tb, C_WIDTH), F32), pltpu.VMEM((tb, C_WIDTH), F32),
                        pltpu.VMEM((1, C_WIDTH), F32), pltpu.VMEM((8, C_WIDTH), F32)],
        compiler_params=_cparams(("arbitrary",)), name="lru_bwd")(
            proj, proj, proj, conv_w, conv_b, wa, ba, wx, bx, lam, hs, hs, dmixed)


def _mesh_pos():
    return lax.axis_index("x"), lax.axis_index("y"), lax.axis_index("c")


class _Exchange:
    def __init__(self, arrays, layouts):
        self.arrays, self.layouts = list(arrays), list(layouts)
        self.out_shapes = []
        for a, lay in zip(self.arrays, self.layouts):
            if lay == 'a2a':
                shp = a.shape
            elif lay == 'slot':
                shp = (N_DEV,) + a.shape
            elif lay == 'rows':
                shp = (N_DEV * a.shape[0], a.shape[1])
            else:
                shp = (a.shape[0], N_DEV * a.shape[1])
            self.out_shapes.append(jax.ShapeDtypeStruct(shp, a.dtype))
        n = len(self.arrays)
        self.scratch = [pltpu.SemaphoreType.DMA((7 * n,)), pltpu.SemaphoreType.DMA((7 * n,)), pltpu.SemaphoreType.DMA((n,))]

    def _landing(self, a, dst_ref, idx):
        lay, shape = self.layouts[a], self.arrays[a].shape
        if lay in ('a2a', 'slot'):
            return dst_ref.at[idx]
        if lay == 'rows':
            return dst_ref.at[pl.ds(pl.multiple_of(idx * shape[0], shape[0]), shape[0]), :]
        return dst_ref.at[:, pl.ds(pl.multiple_of(idx * shape[1], shape[1]), shape[1])]

    def copies(self, src_refs, dst_refs, send_sems, recv_sems, local_sems):
        mx, my, mc = _mesh_pos()
        me = 4 * mx + 2 * my + mc
        out = []
        for a, (src, dst) in enumerate(zip(src_refs, dst_refs)):
            a2a = self.layouts[a] == 'a2a'
            out.append(pltpu.make_async_copy(src.at[me] if a2a else src, self._landing(a, dst, me), local_sems.at[a]))
            for k in range(1, N_DEV):
                px = 1 - mx if k & 4 else mx
                py = 1 - my if k & 2 else my
                pc = 1 - mc if k & 1 else mc
                out.append(pltpu.make_async_remote_copy(
                    src_ref=src.at[4 * px + 2 * py + pc] if a2a else src, dst_ref=self._landing(a, dst, me),
                    send_sem=send_sems.at[7 * a + k - 1], recv_sem=recv_sems.at[7 * a + k - 1],
                    device_id=(px, py, pc), device_id_type=MESH))
        return out


_ANY = pl.BlockSpec(memory_space=pl.ANY)


def _gather_two_level(arrays, name):
    n = len(arrays)

    def body(*refs):
        srcs, outs = refs[:n], refs[n:2 * n]
        send_sems, recv_sems, local_sems = refs[2 * n:]
        mx, my, mc = _mesh_pos()
        me, sibling = (mx, my, mc), (mx, my, 1 - mc)
        chips = [(1 - mx, my), (mx, 1 - my), (1 - mx, 1 - my)]

        def slot(a, px, py, pc):
            return outs[a].at[4 * px + 2 * py + pc]

        def copy(a, k, block, to, src=None):
            return pltpu.make_async_remote_copy(
                src_ref=slot(a, *block) if src is None else src, dst_ref=slot(a, *block),
                send_sem=send_sems.at[7 * a + k], recv_sem=recv_sems.at[7 * a + k], device_id=to, device_id_type=MESH)

        mine = [pltpu.make_async_copy(srcs[a], slot(a, *me), local_sems.at[a]) for a in range(n)]
        for cp in mine:
            cp.start()
        first = []
        for a in range(n):
            first.append(copy(a, 0, me, sibling, src=srcs[a]))
            first += [copy(a, 1 + j, me, (*chip, mc), src=srcs[a]) for j, chip in enumerate(chips)]
        for cp in first:
            cp.start()
        passed = []
        for a in range(n):
            for j, chip in enumerate(chips):
                copy(a, 1 + j, (*chip, mc), me).wait_recv()
                passed.append(copy(a, 4 + j, (*chip, mc), sibling))
                passed[-1].start()
        for a in range(n):
            copy(a, 0, sibling, me).wait_recv()
            for j, chip in enumerate(chips):
                copy(a, 4 + j, (*chip, 1 - mc), me).wait_recv()
        for cp in first + passed:
            cp.wait_send()
        for cp in mine:
            cp.wait()

    return pl.pallas_call(
        body, out_shape=[jax.ShapeDtypeStruct((N_DEV,) + a.shape, a.dtype) for a in arrays],
        in_specs=[_ANY] * n, out_specs=[_ANY] * n,
        scratch_shapes=[pltpu.SemaphoreType.DMA((7 * n,)), pltpu.SemaphoreType.DMA((7 * n,)), pltpu.SemaphoreType.DMA((n,))],
        name=name)(*arrays)


def _call_beside(body, ex, nsteps, args, *, grid, in_specs, out_specs, out_shape, scratch_shapes, name):
    if ex is None:
        outs = pl.pallas_call(body, grid=grid, in_specs=in_specs, out_specs=out_specs, out_shape=out_shape,
                              scratch_shapes=scratch_shapes, compiler_params=_cparams(("arbitrary",)), name=name)(*args)
        return outs, None
    n_in, n_out, n_scr, n = len(in_specs), len(out_specs), len(scratch_shapes), len(ex.arrays)

    def wrapped(*refs):
        ins, refs = refs[:n_in], refs[n_in:]
        ex_ins, refs = refs[:n], refs[n:]
        outs, refs = refs[:n_out], refs[n_out:]
        ex_outs, refs = refs[:n], refs[n:]
        scr, sems = refs[:n_scr], refs[n_scr:]
        step = pl.program_id(0)

        @pl.when(step == 0)
        def _():
            for cp in ex.copies(ex_ins, ex_outs, *sems):
                cp.start()

        body(*ins, *outs, *scr)

        @pl.when(step == nsteps - 1)
        def _():
            for cp in ex.copies(ex_ins, ex_outs, *sems):
                cp.wait()

    res = pl.pallas_call(
        wrapped, grid=grid, in_specs=list(in_specs) + [_ANY] * n, out_specs=list(out_specs) + [_ANY] * n,
        out_shape=list(out_shape) + ex.out_shapes, scratch_shapes=list(scratch_shapes) + ex.scratch,
        compiler_params=_cparams(("arbitrary",)), name=name)(*args, *ex.arrays)
    return res[:n_out], res[n_out:]


def _adamw_math(w, g, m, v):
    m = ADAM_B1 * m + (1.0 - ADAM_B1) * g
    v = ADAM_B2 * v + (1.0 - ADAM_B2) * (g * g)
    m_hat = m / (1.0 - ADAM_B1 ** ADAM_STEP)
    v_hat = v / (1.0 - ADAM_B2 ** ADAM_STEP)
    delta = -ADAM_LR * (m_hat / (jnp.sqrt(v_hat) + ADAM_EPS) + ADAM_WD * w)
    return delta, m, v


def _sum_adamw(parts, w, m, v, tr, name):
    ns = len(parts)
    p, r, c = parts[0].shape
    tr = min(tr, r)
    assert r % tr == 0
    nt = r // tr

    def body(*refs):
        p_refs, (w_ref, m_ref, v_ref, g_ref, d_ref, nm_ref, nv_ref) = refs[:ns], refs[ns:]
        for s in range(ns):
            @pl.when(pl.program_id(0) == s)
            def _(p_ref=p_refs[s]):
                g = p_ref[0].astype(F32)
                for j in range(1, p):
                    g = g + p_ref[j].astype(F32)
                delta, nm, nv = _adamw_math(w_ref[...], g, m_ref[...], v_ref[...])
                g_ref[...] = g
                d_ref[...] = delta
                nm_ref[...] = nm
                nv_ref[...] = nv

    part_spec = lambda s: pl.BlockSpec((p, tr, c), lambda sec, i, s=s: (0, jnp.where(sec == s, i, 0), 0))
    row = pl.BlockSpec((tr, c), lambda sec, i: (sec * nt + i, 0))
    return pl.pallas_call(
        body, grid=(ns, nt), in_specs=[part_spec(s) for s in range(ns)] + [row, row, row],
        out_specs=[row] * 4, out_shape=[jax.ShapeDtypeStruct((ns * r, c), F32)] * 4,
        compiler_params=_cparams(("arbitrary", "arbitrary")), name=name)(*parts, w, m, v)


def _sum_parts(parts, name):
    p, r, c = parts.shape

    def body(p_ref, o_ref):
        g = p_ref[0].astype(F32)
        for j in range(1, p):
            g = g + p_ref[j].astype(F32)
        o_ref[...] = g

    return pl.pallas_call(body, out_shape=jax.ShapeDtypeStruct((r, c), F32), name=name)(parts)


def _rows_of(shape):
    n = 1
    for d in shape:
        n *= d
    return n, -(-n // 128)


def _pack(arrs, row_mult=8):
    blocks = []
    for a in arrs:
        n, nr = _rows_of(a.shape)
        blocks.append(jnp.pad(a.reshape(-1).astype(F32), (0, nr * 128 - n)).reshape(nr, 128))
    rows = sum(b.shape[0] for b in blocks)
    if rows % row_mult:
        blocks.append(jnp.zeros((row_mult - rows % row_mult, 128), F32))
    return jnp.concatenate(blocks, axis=0)


def _unpack(buf, shapes):
    out, r0 = [], 0
    for s in shapes:
        n, nr = _rows_of(s)
        out.append(buf[r0:r0 + nr].reshape(-1)[:n].reshape(s))
        r0 += nr
    return out


def _block_diag(w):
    rows = [jnp.pad(w[i], ((0, 0), (i * C_BLOCK_DIM, C_WIDTH - (i + 1) * C_BLOCK_DIM))) for i in range(C_BLOCKS)]
    return jnp.concatenate(rows, axis=0)


def _diag_blocks(m):
    m4 = m.reshape(C_BLOCKS, C_BLOCK_DIM, C_BLOCKS, C_BLOCK_DIM)
    return jnp.stack([m4[i, :, i, :] for i in range(C_BLOCKS)])


def _gate_row(v):
    return jnp.pad(v.astype(F32), (B_HEADS, 128 - 2 * B_HEADS)).reshape(1, 128)


def _permute_w_in(w):
    pad = jnp.zeros(w.shape[:-1] + (D_IN_PAD - D_IN,), w.dtype)
    return jnp.concatenate([w[..., :3072], w[..., 3080:3592], w[..., 3072:3080], pad], axis=-1)


def _unpermute_w_in(w):
    return jnp.concatenate([w[..., :3072], w[..., COL_G:COL_G + 8], w[..., 3072:COL_G]], axis=-1)


_WEIGHTS = ['norm1_g', 'w_in', 'hgrn_lb_logits', 'hgrn_norm_g', 'gdn_conv_w', 'gdn_a_log', 'gdn_dt_bias', 'gdn_norm_g',
            'lru_conv_w', 'lru_conv_b', 'lru_w_a', 'lru_b_a', 'lru_w_x', 'lru_b_x', 'lru_lambda', 'w_out', 'norm2_g',
            'w_up', 'w_down', 'final_norm_g']
_BIG = ('w_in', 'w_out', 'w_up', 'w_down')
_SHARDED_SMALL = ('gdn_conv_w', 'lru_conv_w')


def _step(x, target, w, m, v):
    t = x.shape[0]
    mx, my, mc = _mesh_pos()
    me = 4 * mx + 2 * my + mc

    bf = lambda a: a.astype(BF16)

    def full_w_in(g):
        return _permute_w_in(jnp.moveaxis(g, 0, 1).reshape(D_MODEL, D_IN))

    conv_shapes = [w['gdn_conv_w'].shape, w['lru_conv_w'].shape]
    g_in, g_conv = _gather_two_level([bf(w['w_in'][0]), _pack([w['gdn_conv_w'], w['lru_conv_w']])], "gather_first")
    w_in = [full_w_in(g_in)]
    w_out, w_up, w_down = [], [], []
    gdn_cw, lru_cw = [], []
    for j in range(N_DEV):
        a, b = _unpack(g_conv[j], conv_shapes)
        gdn_cw.append(a)
        lru_cw.append(b)
    gdn_cw = jnp.concatenate(gdn_cw, axis=-1)
    lru_cw = jnp.concatenate(lru_cw, axis=-1)

    lbnd = _lb_fwd(w['hgrn_lb_logits'])
    row = lambda a: a.reshape(1, -1)

    def c_args(l):
        return (lru_cw[l], row(w['lru_conv_b'][l]), _block_diag(w['lru_w_a'][l]), row(w['lru_b_a'][l]),
                _block_diag(w['lru_w_x'][l]), row(w['lru_b_x'][l]), row(w['lru_lambda'][l]))

    saved = []
    xl = x
    h = _rms_fwd(x, w['norm1_g'][0], name="rms_fwd")
    for l in range(DEPTH):
        proj = _mm_rows(h, w_in[l], "nn", 512, "mm_proj")
        mix_a, st_a = _a_fwd(proj, lbnd[l], w['hgrn_norm_g'][l])
        alr, dtr = _gate_row(w['gdn_a_log'][l]), _gate_row(w['gdn_dt_bias'][l])
        qkv, gates, y3 = _b_pre_fwd(proj, gdn_cw[l], alr, dtr)
        nxt = [bf(w['w_in'][l + 1])] if l + 1 < DEPTH else []
        gather = _Exchange([bf(w['w_out'][l]), bf(w['w_up'][l]), bf(w['w_down'][l])] + nxt, ['rows', 'cols', 'rows'] + ['slot'] * len(nxt))
        (mix_b, st_b, *b_saved), got = _b_fwd(qkv, gates, proj, w['gdn_norm_g'][l], beside=gather)
        w_out.append(got[0])
        w_up.append(got[1])
        w_down.append(got[2])
        if nxt:
            w_in.append(full_w_in(got[3]))
        mix_c, hs = _c_fwd(proj, *c_args(l))
        mixed = [mix_a, mix_b, mix_c]
        x_mid, h2 = _mm_rows(mixed, w_out[l], "nn", 1024, "mm_out", residual=xl, epilogue="rms_fwd", norm=w['norm2_g'][l])
        act, up = _mm_rows(h2, w_up[l], "nn", 512, "mm_up", epilogue="relu2")
        saved.append(dict(x=xl, h=h, proj=proj, st_a=st_a, qkv=qkv, gates=gates, y3=y3, st_b=st_b, b_saved=b_saved, hs=hs, mixed=mixed,
                          x_mid=x_mid, h2=h2, up=up, act=act, alr=alr, dtr=dtr))
        if l + 1 < DEPTH:
            xl, h = _mm_rows(act, w_down[l], "nn", 512, "mm_down", residual=x_mid, epilogue="rms_fwd", norm=w['norm1_g'][l + 1])
        else:
            xl = _mm_rows(act, w_down[l], "nn", 512, "mm_down_last", residual=x_mid)
    loss, dx, dgf = _loss_head(xl, w['final_norm_g'], target)

    gs = {n: [None] * DEPTH for n in _WEIGHTS}
    recv = {n: [None] * DEPTH for n in _BIG}
    dw_in_above = None
    for l in reversed(range(DEPTH)):
        s = saved[l]
        dup = _mm_rows(dx, w_down[l], "nt", 512, "mm_dact", epilogue="drelu2", up=s['up'])
        dw_down = _mm_tn(s['act'], dx, 1024, "mm_dw_down").reshape(N_DEV, D_FF // N_DEV, D_MODEL)
        dx_mid, dg2 = _mm_rows(dup, w_up[l], "nt", 512, "mm_dh2", epilogue="rms_bwd", norm=(s['x_mid'], w['norm2_g'][l], dx))
        dw_up = _mm_tn(s['h2'], dup, 1024, "mm_dw_up", slab=D_FF // N_DEV)
        gs['norm2_g'][l] = dg2[0]
        dmixed = _mm_rows(dx_mid, w_out[l], "nt", 1024, "mm_dmixed")
        dw_out = _mm_tn(s['mixed'], dx_mid, 1024, "mm_dw_out").reshape(N_DEV, D_MODEL // N_DEV, D_MODEL)
        proj = s['proj']
        above = [dw_in_above] if dw_in_above is not None else []
        (dpa, dlb, dnga), got = _a_bwd(proj, lbnd[l], w['hgrn_norm_g'][l], s['st_a'], dmixed,
                                       beside=_Exchange([dw_out] + above, ['a2a'] * (1 + len(above))))
        recv['w_out'][l] = got[0]
        if above:
            recv['w_in'][l + 1] = got[1]
        gs['hgrn_lb_logits'][l] = dlb[0]
        gs['hgrn_norm_g'][l] = dnga[0, :A_DIM] + dnga[0, A_DIM:]
        (dqkv, dgates, dz, dngb), got = _b_bwd(s['qkv'], s['gates'], proj, w['gdn_norm_g'][l], s['st_b'], s['b_saved'], dmixed,
                                               beside=_Exchange([dw_up, dw_down], ['a2a', 'a2a']))
        recv['w_up'][l], recv['w_down'][l] = got
        dxb, dgi, dcwb, dal, ddt = _b_pre_bwd(proj, s['y3'], gdn_cw[l], s['alr'], s['dtr'], dqkv, dgates)
        gs['gdn_norm_g'][l] = dngb[0]
        gs['gdn_conv_w'][l] = dcwb
        gs['gdn_a_log'][l] = dal[0, B_HEADS:2 * B_HEADS]
        gs['gdn_dt_bias'][l] = ddt[0, B_HEADS:2 * B_HEADS]
        dxc_in, dyg, dcwc, dcb, dwa, dba, dwx, dbx, dlam = _c_bwd(proj, *c_args(l), s['hs'], dmixed)
        gs['lru_conv_w'][l] = dcwc
        gs['lru_conv_b'][l] = dcb[0]
        gs['lru_w_a'][l] = _diag_blocks(dwa)
        gs['lru_b_a'][l] = dba[0]
        gs['lru_w_x'][l] = _diag_blocks(dwx)
        gs['lru_b_x'][l] = dbx[0]
        gs['lru_lambda'][l] = dlam[0]
        dproj = [dpa, dxb, dz, dxc_in, dyg, dgi]
        dw_in = _unpermute_w_in(_mm_tn(s['h'], dproj, 1024, "mm_dw_in"))
        dw_in_above = jnp.moveaxis(dw_in.reshape(D_MODEL, N_DEV, D_IN // N_DEV), 1, 0)
        norm = (s['x'], w['norm1_g'][l], dx_mid)
        if l > 0:
            dx, dg1 = _mm_rows(dproj, w_in[l], "nt", 512, "mm_dh", epilogue="rms_bwd", norm=norm)
        else:
            (dx, dg1), got = _mm_rows(dproj, w_in[l], "nt", 512, "mm_dh_last", epilogue="rms_bwd", norm=norm,
                                      beside=_Exchange([dw_in_above], ['a2a']))
            recv['w_in'][0] = got[0]
        gs['norm1_g'][l] = dg1[0]
    grad_x = dx
    part = {n: jnp.stack(gs[n]) for n in _WEIGHTS if n != 'final_norm_g' and n not in _BIG}
    part['final_norm_g'] = dgf[0]
    part['hgrn_lb_logits'] = _lb_bwd(w['hgrn_lb_logits'], part['hgrn_lb_logits'])

    small = [n for n in _WEIGHTS if n not in _BIG]
    wide = [n for n in small if n in ('lru_w_a', 'lru_w_x', 'gdn_conv_w', 'lru_conv_w')]
    narrow = [n for n in small if n not in wide]
    packed = _pack([part[n] for n in narrow] + [loss])
    packed_wide = _pack([part[n] for n in wide], row_mult=16).astype(BF16)
    all_small, all_wide = _gather_two_level([packed, packed_wide], "gather_small")

    grads, deltas, new_m, new_v = {}, {}, {}, {}
    for n in _BIG:
        shp = w[n].shape
        r2 = lambda a: a.reshape(-1, shp[-1])
        g, d, nm, nv = _sum_adamw(recv[n], r2(w[n]), r2(m[n]), r2(v[n]), 256, "adamw_" + n)
        grads[n], deltas[n], new_m[n], new_v[n] = (a.reshape(shp) for a in (g, d, nm, nv))

    summed = _unpack(_sum_parts(all_small, "sum_small"), [part[n].shape for n in narrow] + [(1, 1)])
    loss_total = summed[-1].reshape(())
    gsmall = dict(zip(narrow, summed[:-1]))
    gsmall.update(zip(wide, _unpack(_sum_parts(all_wide, "sum_small_wide"), [part[n].shape for n in wide])))
    for n in _SHARDED_SMALL:
        width = w[n].shape[-1]
        gsmall[n] = lax.dynamic_slice_in_dim(gsmall[n], me * width, width, axis=2)
    pk = lambda d: _pack([d[n] for n in small])
    _, d, nm, nv = _sum_adamw([pk(gsmall)[None]], pk(w), pk(m), pk(v), 4096, "adamw_small")
    shapes = [w[n].shape for n in small]
    for n, dd, mm, vv in zip(small, _unpack(d, shapes), _unpack(nm, shapes), _unpack(nv, shapes)):
        grads[n], deltas[n], new_m[n], new_v[n] = gsmall[n], dd, mm, vv
    return loss_total, grad_x, grads, deltas, new_m, new_v


def kernel(x, norm1_g, w_in, hgrn_lb_logits, hgrn_norm_g, gdn_conv_w, gdn_a_log, gdn_dt_bias, gdn_norm_g, lru_conv_w, lru_conv_b, lru_w_a, lru_b_a, lru_w_x, lru_b_x, lru_lambda, w_out, norm2_g, w_up, w_down, final_norm_g, loss_target, m_norm1_g, m_w_in, m_hgrn_lb_logits, m_hgrn_norm_g, m_gdn_conv_w, m_gdn_a_log, m_gdn_dt_bias, m_gdn_norm_g, m_lru_conv_w, m_lru_conv_b, m_lru_w_a, m_lru_b_a, m_lru_w_x, m_lru_b_x, m_lru_lambda, m_w_out, m_norm2_g, m_w_up, m_w_down, m_final_norm_g, v_norm1_g, v_w_in, v_hgrn_lb_logits, v_hgrn_norm_g, v_gdn_conv_w, v_gdn_a_log, v_gdn_dt_bias, v_gdn_norm_g, v_lru_conv_w, v_lru_conv_b, v_lru_w_a, v_lru_b_a, v_lru_w_x, v_lru_b_x, v_lru_lambda, v_w_out, v_norm2_g, v_w_up, v_w_down, v_final_norm_g):
    w = dict(zip(_WEIGHTS, (norm1_g, w_in, hgrn_lb_logits, hgrn_norm_g, gdn_conv_w, gdn_a_log, gdn_dt_bias, gdn_norm_g, lru_conv_w, lru_conv_b, lru_w_a, lru_b_a, lru_w_x, lru_b_x, lru_lambda, w_out, norm2_g, w_up, w_down, final_norm_g)))
    m = dict(zip(_WEIGHTS, (m_norm1_g, m_w_in, m_hgrn_lb_logits, m_hgrn_norm_g, m_gdn_conv_w, m_gdn_a_log, m_gdn_dt_bias, m_gdn_norm_g, m_lru_conv_w, m_lru_conv_b, m_lru_w_a, m_lru_b_a, m_lru_w_x, m_lru_b_x, m_lru_lambda, m_w_out, m_norm2_g, m_w_up, m_w_down, m_final_norm_g)))
    v = dict(zip(_WEIGHTS, (v_norm1_g, v_w_in, v_hgrn_lb_logits, v_hgrn_norm_g, v_gdn_conv_w, v_gdn_a_log, v_gdn_dt_bias, v_gdn_norm_g, v_lru_conv_w, v_lru_conv_b, v_lru_w_a, v_lru_b_a, v_lru_w_x, v_lru_b_x, v_lru_lambda, v_w_out, v_norm2_g, v_w_up, v_w_down, v_final_norm_g)))
    loss, grad_x, grads, deltas, new_m, new_v = _step(x.reshape(x.shape[1:]), loss_target.reshape(x.shape[1:]), w, m, v)
    return (loss, grad_x[None], *[grads[n] for n in _WEIGHTS], *[deltas[n] for n in _WEIGHTS],
            *[new_m[n] for n in _WEIGHTS], *[new_v[n] for n in _WEIGHTS])
```
